```python
import jax, jax.numpy as jnp
from jax import lax
import numpy as np

D_MODEL = 4096
BATCH = 8
SEQ = 4096
DEPTH = 1

GRID_W = 64
CTX_LEN = 256
GLA_VAL_W = D_MODEL // 2
GLA_HEADS = 8
GLA_DV = GLA_VAL_W // GLA_HEADS
GLA_DK = GLA_DV // 2
GLA_KEY_W = GLA_HEADS * GLA_DK
GLA_CHUNK = 64
GLA_LOWRANK = 16
GLA_TAU = 16.0
ROPE_BASE = 10000.0
SG_WIDTH = D_MODEL - GLA_VAL_W
SG_GROUPS = 4
SG_GROUP_W = SG_WIDTH // SG_GROUPS
SG_CHUNK = 128
MIX_W = GLA_VAL_W + SG_WIDTH
D_FF = 4 * D_MODEL
N_MOD = 6
EPS = 1e-6
Q0 = 0
K0 = Q0 + GLA_KEY_W
V0 = K0 + GLA_KEY_W
R0 = V0 + GLA_VAL_W
LF0 = R0 + GLA_VAL_W
LB0 = LF0 + GLA_LOWRANK
SG0 = LB0 + GLA_LOWRANK
IN_COLS = SG0 + 2 * SG_WIDTH

kernel_name = "hybrid_gla_gmlp_prefix_dit_block"


def rmsnorm(t, g):
    tf = t.astype(jnp.float32)
    y = tf * lax.rsqrt(jnp.mean(tf * tf, axis=-1, keepdims=True) + EPS)
    return (y * g.astype(jnp.float32)).astype(t.dtype)


def layernorm(t, g, b):
    tf = t.astype(jnp.float32)
    mu = jnp.mean(tf, axis=-1, keepdims=True)
    var = jnp.mean(jnp.square(tf - mu), axis=-1, keepdims=True)
    y = (tf - mu) * lax.rsqrt(var + EPS)
    return (y * g.astype(jnp.float32) + b.astype(jnp.float32)).astype(t.dtype)


def modulate(h, shift, scale):
    return h * (1.0 + scale) + shift


def split_heads(t, d):
    return t.reshape(t.shape[:-1] + (GLA_HEADS, d))


def flip_seq(t):
    return jnp.flip(t, axis=1)


def rope_axis(t, pos):
    m = t.shape[-1] // 2
    inv_freq = ROPE_BASE ** (-jnp.arange(m, dtype=jnp.float32) / m)
    ang = pos.astype(jnp.float32)[:, None] * inv_freq[None, :]
    cos = jnp.cos(ang)[:, None, :]
    sin = jnp.sin(ang)[:, None, :]
    t1 = t[..., :m].astype(jnp.float32)
    t2 = t[..., m:].astype(jnp.float32)
    return jnp.concatenate([t1 * cos - t2 * sin, t1 * sin + t2 * cos], axis=-1).astype(t.dtype)


def rope2d(t, row_pos, col_pos):
    half = t.shape[-1] // 2
    return jnp.concatenate([rope_axis(t[..., :half], row_pos), rope_axis(t[..., half:], col_pos)], axis=-1)


def gla_qk(z):
    q = split_heads(z[..., Q0:K0], GLA_DK) * (GLA_DK ** -0.5)
    k = split_heads(z[..., K0:V0], GLA_DK)
    return q, k


def gla_log_decay(lr, w_dec, b_dec):
    a = (lr @ w_dec + b_dec).astype(jnp.float32)
    return split_heads(jax.nn.log_sigmoid(a) / GLA_TAU, GLA_DK)


def gla_chunked(q, k, v, log_a, s0):
    bsz, n, h, dk = q.shape
    dv = v.shape[-1]
    nc = n // GLA_CHUNK

    def to_chunks(t):
        return t.astype(jnp.float32).reshape(bsz, nc, GLA_CHUNK, h, t.shape[-1]).transpose(1, 0, 3, 2, 4)

    mask = jnp.tril(jnp.ones((GLA_CHUNK, GLA_CHUNK), dtype=bool))[None, None, :, :, None]

    def step(s, inp):
        qi, ki, vi, ai = inp
        b = jnp.cumsum(ai, axis=2)
        inter = jnp.einsum('bhck,bhkv->bhcv', qi * jnp.exp(b), s)
        diff = b[:, :, :, None, :] - b[:, :, None, :, :]
        decay = jnp.exp(jnp.where(mask, diff, -jnp.inf))
        att = jnp.einsum('bhik,bhjk,bhijk->bhij', qi, ki, decay)
        intra = jnp.einsum('bhij,bhjv->bhiv', att, vi)
        b_last = b[:, :, -1, :]
        s_new = jnp.exp(b_last)[..., None] * s + jnp.einsum(
            'bhck,bhcv->bhkv', ki * jnp.exp(b_last[:, :, None, :] - b), vi)
        return s_new, inter + intra

    s_fin, o = lax.scan(step, s0.astype(jnp.float32), (to_chunks(q), to_chunks(k), to_chunks(v), to_chunks(log_a)))
    o = o.transpose(1, 0, 3, 2, 4).reshape(bsz, n, h, dv)
    return o, s_fin


def gla_final_state(k, v, log_a):
    b = jnp.cumsum(log_a.astype(jnp.float32), axis=1)
    w = jnp.exp(b[:, -1:] - b)
    return jnp.einsum('bnhk,bnhv->bhkv', k.astype(jnp.float32) * w, v.astype(jnp.float32))


def gla_bidir(q, k, v, la_f, la_b, s_f0, s_b0):
    o_f, s_f = gla_chunked(q, k, v, la_f, s_f0)
    o_b, s_b = gla_chunked(flip_seq(q), flip_seq(k), flip_seq(v), flip_seq(la_b), s_b0)
    return o_f + flip_seq(o_b), s_f, s_b


def gla_readout(o, r, g):
    bsz, n = o.shape[0], o.shape[1]
    y = rmsnorm(o, g).astype(r.dtype).reshape(bsz, n, GLA_VAL_W)
    return y * jax.nn.silu(r)


def spatial_gating(zs, ln_g, ln_b, w_s, b_s):
    zs = jax.nn.gelu(zs, approximate=False)
    u, vv = zs[..., :SG_WIDTH], zs[..., SG_WIDTH:]
    vv = layernorm(vv, ln_g, ln_b)
    bsz, n = vv.shape[0], vv.shape[1]
    vv = vv.reshape(bsz, n // SG_CHUNK, SG_CHUNK, SG_GROUPS, SG_GROUP_W)
    s = jnp.einsum('gij,bnjgc->bnigc', w_s, vv) + b_s.T[:, :, None]
    return u * s.reshape(bsz, n, SG_WIDTH)


def token_mix(z, q, k, s_f0, s_b0, w_dec_f, b_dec_f, w_dec_b, b_dec_b,
              gla_norm_g, sg_ln_g, sg_ln_b, w_s, b_s, w_o):
    v = split_heads(z[..., V0:R0], GLA_DV)
    la_f = gla_log_decay(z[..., LF0:LB0], w_dec_f, b_dec_f)
    la_b = gla_log_decay(z[..., LB0:SG0], w_dec_b, b_dec_b)
    o, s_f, s_b = gla_bidir(q, k, v, la_f, la_b, s_f0, s_b0)
    y_gla = gla_readout(o, z[..., R0:LF0], gla_norm_g)
    y_sg = spatial_gating(z[..., SG0:], sg_ln_g, sg_ln_b, w_s, b_s)
    y = jnp.concatenate([y_gla, y_sg], axis=-1) @ w_o
    return y, s_f, s_b


def ctx_states(hc, w_in, w_dec_f, b_dec_f, w_dec_b, b_dec_b):
    k = split_heads(hc @ w_in[:, K0:V0], GLA_DK)
    v = split_heads(hc @ w_in[:, V0:R0], GLA_DV)
    lr = hc @ w_in[:, LF0:SG0]
    la_f = gla_log_decay(lr[..., :GLA_LOWRANK], w_dec_f, b_dec_f)
    la_b = gla_log_decay(lr[..., GLA_LOWRANK:], w_dec_b, b_dec_b)
    s_f = gla_final_state(k, v, la_f)
    s_b = gla_final_state(flip_seq(k), flip_seq(v), flip_seq(la_b))
    return s_f, s_b


def sq_relu_mlp(h, w_1, w_2):
    return jnp.square(jax.nn.relu(h @ w_1)) @ w_2


def _fwd_setup_inputs(seed: int = 0) -> dict:
    key = jax.random.key(seed)
    ks = jax.random.split(key, 24)

    def nrm(k, shape, scale):
        return jax.random.normal(k, shape, jnp.float32) * scale

    def gain(k, shape):
        return 1.0 + nrm(k, shape, 0.01)

    L = DEPTH
    return {
        "x": nrm(ks[0], (BATCH, SEQ, D_MODEL), 1.0),
        "c": nrm(ks[1], (BATCH, D_MODEL), 1.0),
        "ctx": nrm(ks[2], (BATCH, CTX_LEN, D_MODEL), 1.0),
        "c_ctx": nrm(ks[3], (D_MODEL,), 1.0),
        "w_ada": nrm(ks[4], (L, D_MODEL, N_MOD * D_MODEL), D_MODEL ** -0.5),
        "b_ada": nrm(ks[5], (L, N_MOD * D_MODEL), 0.01),
        "pre1_g": gain(ks[6], (L, D_MODEL)),
        "post1_g": gain(ks[7], (L, D_MODEL)),
        "pre2_g": gain(ks[8], (L, D_MODEL)),
        "post2_g": gain(ks[9], (L, D_MODEL)),
        "w_in": nrm(ks[10], (L, D_MODEL, IN_COLS), D_MODEL ** -0.5),
        "w_dec_f": nrm(ks[11], (L, GLA_LOWRANK, GLA_KEY_W), GLA_LOWRANK ** -0.5),
        "b_dec_f": nrm(ks[12], (L, GLA_KEY_W), 0.1),
        "w_dec_b": nrm(ks[13], (L, GLA_LOWRANK, GLA_KEY_W), GLA_LOWRANK ** -0.5),
        "b_dec_b": nrm(ks[14], (L, GLA_KEY_W), 0.1),
        "gla_norm_g": gain(ks[15], (L, GLA_HEADS, GLA_DV)),
        "sg_ln_g": gain(ks[16], (L, SG_WIDTH)),
        "sg_ln_b": nrm(ks[17], (L, SG_WIDTH), 0.01),
        "w_s": nrm(ks[18], (L, SG_GROUPS, SG_CHUNK, SG_CHUNK), SG_CHUNK ** -0.5),
        "b_s": gain(ks[19], (L, SG_GROUPS, SG_CHUNK)),
        "w_o": nrm(ks[20], (L, MIX_W, D_MODEL), MIX_W ** -0.5),
        "w_1": nrm(ks[21], (L, D_MODEL, D_FF), D_MODEL ** -0.5),
        "w_2": nrm(ks[22], (L, D_FF, D_MODEL), D_FF ** -0.5),
    }


def _fwd_reference(x, c, ctx, c_ctx, w_ada, b_ada, pre1_g, post1_g, pre2_g, post2_g, w_in,
              w_dec_f, b_dec_f, w_dec_b, b_dec_b, gla_norm_g, sg_ln_g, sg_ln_b, w_s, b_s,
              w_o, w_1, w_2):
    bsz, n = x.shape[0], x.shape[1]
    ROWS = n // GRID_W
    pos = jnp.arange(ROWS * GRID_W)
    row_pos = pos // GRID_W
    col_pos = pos % GRID_W
    zero_state = jnp.zeros((bsz, GLA_HEADS, GLA_DK, GLA_DV), jnp.float32)
    cond_x = jax.nn.silu(c)[:, None, :]
    cond_c = jax.nn.silu(c_ctx)

    for l in range(DEPTH):
        mod_x = cond_x @ w_ada[l] + b_ada[l]
        sh1, sc1, g1, sh2, sc2, g2 = jnp.split(mod_x, N_MOD, axis=-1)
        hx = modulate(rmsnorm(x, pre1_g[l]), sh1, sc1)

        if l == DEPTH - 1:
            mod_c = cond_c @ w_ada[l][:, :2 * D_MODEL] + b_ada[l][:2 * D_MODEL]
            csh1, csc1 = jnp.split(mod_c, 2, axis=-1)
            hc = modulate(rmsnorm(ctx, pre1_g[l]), csh1, csc1)
            s_f, s_b = ctx_states(hc, w_in[l], w_dec_f[l], b_dec_f[l], w_dec_b[l], b_dec_b[l])
        else:
            mod_c = cond_c @ w_ada[l] + b_ada[l]
            csh1, csc1, cg1, csh2, csc2, cg2 = jnp.split(mod_c, N_MOD, axis=-1)
            hc = modulate(rmsnorm(ctx, pre1_g[l]), csh1, csc1)
            zc = hc @ w_in[l]
            qc, kc = gla_qk(zc)
            mix_c, s_f, s_b = token_mix(zc, qc, kc, zero_state, zero_state,
                                        w_dec_f[l], b_dec_f[l], w_dec_b[l], b_dec_b[l],
                                        gla_norm_g[l], sg_ln_g[l], sg_ln_b[l], w_s[l], b_s[l], w_o[l])
            ctx = ctx + cg1 * rmsnorm(mix_c, post1_g[l])
            hc2 = modulate(rmsnorm(ctx, pre2_g[l]), csh2, csc2)
            ctx = ctx + cg2 * rmsnorm(sq_relu_mlp(hc2, w_1[l], w_2[l]), post2_g[l])

        zx = hx @ w_in[l]
        qx, kx = gla_qk(zx)
        qx = rope2d(qx, row_pos, col_pos)
        kx = rope2d(kx, row_pos, col_pos)
        mix_x, _, _ = token_mix(zx, qx, kx, s_f, s_b,
                                w_dec_f[l], b_dec_f[l], w_dec_b[l], b_dec_b[l],
                                gla_norm_g[l], sg_ln_g[l], sg_ln_b[l], w_s[l], b_s[l], w_o[l])
        x = x + g1 * rmsnorm(mix_x, post1_g[l])
        h2 = modulate(rmsnorm(x, pre2_g[l]), sh2, sc2)
        x = x + g2 * rmsnorm(sq_relu_mlp(h2, w_1[l], w_2[l]), post2_g[l])
    return x


import jax as _jax
import jax.numpy as _jnp

TWIN_FORMAT = 'train_step'
FWD_PARAMS = ['x', 'c', 'ctx', 'c_ctx', 'w_ada', 'b_ada', 'pre1_g', 'post1_g', 'pre2_g', 'post2_g', 'w_in', 'w_dec_f', 'b_dec_f', 'w_dec_b', 'b_dec_b', 'gla_norm_g', 'sg_ln_g', 'sg_ln_b', 'w_s', 'b_s', 'w_o', 'w_1', 'w_2']
TWIN_WEIGHTS = ['c_ctx', 'w_ada', 'b_ada', 'pre1_g', 'post1_g', 'pre2_g', 'post2_g', 'w_in', 'w_dec_f', 'b_dec_f', 'w_dec_b', 'b_dec_b', 'gla_norm_g', 'sg_ln_g', 'sg_ln_b', 'w_s', 'b_s', 'w_o', 'w_1', 'w_2']
TWIN_DIFF_INPUT = 'x'
TWIN_INPUTS = ['x', 'c', 'ctx', 'c_ctx', 'w_ada', 'b_ada', 'pre1_g', 'post1_g', 'pre2_g', 'post2_g', 'w_in', 'w_dec_f', 'b_dec_f', 'w_dec_b', 'b_dec_b', 'gla_norm_g', 'sg_ln_g', 'sg_ln_b', 'w_s', 'b_s', 'w_o', 'w_1', 'w_2', 'loss_target', 'm_c_ctx', 'm_w_ada', 'm_b_ada', 'm_pre1_g', 'm_post1_g', 'm_pre2_g', 'm_post2_g', 'm_w_in', 'm_w_dec_f', 'm_b_dec_f', 'm_w_dec_b', 'm_b_dec_b', 'm_gla_norm_g', 'm_sg_ln_g', 'm_sg_ln_b', 'm_w_s', 'm_b_s', 'm_w_o', 'm_w_1', 'm_w_2', 'v_c_ctx', 'v_w_ada', 'v_b_ada', 'v_pre1_g', 'v_post1_g', 'v_pre2_g', 'v_post2_g', 'v_w_in', 'v_w_dec_f', 'v_b_dec_f', 'v_w_dec_b', 'v_b_dec_b', 'v_gla_norm_g', 'v_sg_ln_g', 'v_sg_ln_b', 'v_w_s', 'v_b_s', 'v_w_o', 'v_w_1', 'v_w_2']
TWIN_OUTPUTS = ['loss', 'grad_x', 'grad_c_ctx', 'grad_w_ada', 'grad_b_ada', 'grad_pre1_g', 'grad_post1_g', 'grad_pre2_g', 'grad_post2_g', 'grad_w_in', 'grad_w_dec_f', 'grad_b_dec_f', 'grad_w_dec_b', 'grad_b_dec_b', 'grad_gla_norm_g', 'grad_sg_ln_g', 'grad_sg_ln_b', 'grad_w_s', 'grad_b_s', 'grad_w_o', 'grad_w_1', 'grad_w_2', 'delta_c_ctx', 'delta_w_ada', 'delta_b_ada', 'delta_pre1_g', 'delta_post1_g', 'delta_pre2_g', 'delta_post2_g', 'delta_w_in', 'delta_w_dec_f', 'delta_b_dec_f', 'delta_w_dec_b', 'delta_b_dec_b', 'delta_gla_norm_g', 'delta_sg_ln_g', 'delta_sg_ln_b', 'delta_w_s', 'delta_b_s', 'delta_w_o', 'delta_w_1', 'delta_w_2', 'new_m_c_ctx', 'new_m_w_ada', 'new_m_b_ada', 'new_m_pre1_g', 'new_m_post1_g', 'new_m_pre2_g', 'new_m_post2_g', 'new_m_w_in', 'new_m_w_dec_f', 'new_m_b_dec_f', 'new_m_w_dec_b', 'new_m_b_dec_b', 'new_m_gla_norm_g', 'new_m_sg_ln_g', 'new_m_sg_ln_b', 'new_m_w_s', 'new_m_b_s', 'new_m_w_o', 'new_m_w_1', 'new_m_w_2', 'new_v_c_ctx', 'new_v_w_ada', 'new_v_b_ada', 'new_v_pre1_g', 'new_v_post1_g', 'new_v_pre2_g', 'new_v_post2_g', 'new_v_w_in', 'new_v_w_dec_f', 'new_v_b_dec_f', 'new_v_w_dec_b', 'new_v_b_dec_b', 'new_v_gla_norm_g', 'new_v_sg_ln_g', 'new_v_sg_ln_b', 'new_v_w_s', 'new_v_b_s', 'new_v_w_o', 'new_v_w_1', 'new_v_w_2']
TWIN_LEAF_KINDS = {'loss': 'loss', 'grad_x': 'grad_x', 'grad_c_ctx': 'grad_w', 'grad_w_ada': 'grad_w', 'grad_b_ada': 'grad_w', 'grad_pre1_g': 'grad_w', 'grad_post1_g': 'grad_w', 'grad_pre2_g': 'grad_w', 'grad_post2_g': 'grad_w', 'grad_w_in': 'grad_w', 'grad_w_dec_f': 'grad_w', 'grad_b_dec_f': 'grad_w', 'grad_w_dec_b': 'grad_w', 'grad_b_dec_b': 'grad_w', 'grad_gla_norm_g': 'grad_w', 'grad_sg_ln_g': 'grad_w', 'grad_sg_ln_b': 'grad_w', 'grad_w_s': 'grad_w', 'grad_b_s': 'grad_w', 'grad_w_o': 'grad_w', 'grad_w_1': 'grad_w', 'grad_w_2': 'grad_w', 'delta_c_ctx': 'delta_w', 'delta_w_ada': 'delta_w', 'delta_b_ada': 'delta_w', 'delta_pre1_g': 'delta_w', 'delta_post1_g': 'delta_w', 'delta_pre2_g': 'delta_w', 'delta_post2_g': 'delta_w', 'delta_w_in': 'delta_w', 'delta_w_dec_f': 'delta_w', 'delta_b_dec_f': 'delta_w', 'delta_w_dec_b': 'delta_w', 'delta_b_dec_b': 'delta_w', 'delta_gla_norm_g': 'delta_w', 'delta_sg_ln_g': 'delta_w', 'delta_sg_ln_b': 'delta_w', 'delta_w_s': 'delta_w', 'delta_b_s': 'delta_w', 'delta_w_o': 'delta_w', 'delta_w_1': 'delta_w', 'delta_w_2': 'delta_w', 'new_m_c_ctx': 'new_m', 'new_m_w_ada': 'new_m', 'new_m_b_ada': 'new_m', 'new_m_pre1_g': 'new_m', 'new_m_post1_g': 'new_m', 'new_m_pre2_g': 'new_m', 'new_m_post2_g': 'new_m', 'new_m_w_in': 'new_m', 'new_m_w_dec_f': 'new_m', 'new_m_b_dec_f': 'new_m', 'new_m_w_dec_b': 'new_m', 'new_m_b_dec_b': 'new_m', 'new_m_gla_norm_g': 'new_m', 'new_m_sg_ln_g': 'new_m', 'new_m_sg_ln_b': 'new_m', 'new_m_w_s': 'new_m', 'new_m_b_s': 'new_m', 'new_m_w_o': 'new_m', 'new_m_w_1': 'new_m', 'new_m_w_2': 'new_m', 'new_v_c_ctx': 'new_v', 'new_v_w_ada': 'new_v', 'new_v_b_ada': 'new_v', 'new_v_pre1_g': 'new_v', 'new_v_post1_g': 'new_v', 'new_v_pre2_g': 'new_v', 'new_v_post2_g': 'new_v', 'new_v_w_in': 'new_v', 'new_v_w_dec_f': 'new_v', 'new_v_b_dec_f': 'new_v', 'new_v_w_dec_b': 'new_v', 'new_v_b_dec_b': 'new_v', 'new_v_gla_norm_g': 'new_v', 'new_v_sg_ln_g': 'new_v', 'new_v_sg_ln_b': 'new_v', 'new_v_w_s': 'new_v', 'new_v_b_s': 'new_v', 'new_v_w_o': 'new_v', 'new_v_w_1': 'new_v', 'new_v_w_2': 'new_v'}


def _forward(args):
    return _fwd_reference(*[args[k] for k in FWD_PARAMS])


def _output_shape():
    out = _jax.eval_shape(lambda: _forward(_fwd_setup_inputs(0)))
    return out.shape, out.dtype

N_MICROBATCH = 1
ADAM_LR = 0.001
ADAM_B1 = 0.9
ADAM_B2 = 0.999
ADAM_EPS = 1e-08
ADAM_WD = 0.01
ADAM_STEP = 10
PER_EXAMPLE_BATCH_AXIS = {'x': 0, 'c': 0, 'ctx': 0, 'loss_target': 0}
SHARED_INPUTS = []
_WEIGHT_DTYPES = {'c_ctx': _jnp.float32, 'w_ada': _jnp.float32, 'b_ada': _jnp.float32, 'pre1_g': _jnp.float32, 'post1_g': _jnp.float32, 'pre2_g': _jnp.float32, 'post2_g': _jnp.float32, 'w_in': _jnp.float32, 'w_dec_f': _jnp.float32, 'b_dec_f': _jnp.float32, 'w_dec_b': _jnp.float32, 'b_dec_b': _jnp.float32, 'gla_norm_g': _jnp.float32, 'sg_ln_g': _jnp.float32, 'sg_ln_b': _jnp.float32, 'w_s': _jnp.float32, 'b_s': _jnp.float32, 'w_o': _jnp.float32, 'w_1': _jnp.float32, 'w_2': _jnp.float32}
MOMENT_SCALE = {'c_ctx': 1.089548e-02, 'w_ada': 9.069433e-01, 'b_ada': 1.714213e+00, 'pre1_g': 7.875813e-02, 'post1_g': 3.607700e+00, 'pre2_g': 1.204889e-01, 'post2_g': 3.727060e+00, 'w_in': 1.483801e-01, 'w_dec_f': 1.746300e-02, 'b_dec_f': 4.189421e-02, 'w_dec_b': 2.899554e-02, 'b_dec_b': 4.846012e-02, 'gla_norm_g': 1.125814e-01, 'sg_ln_g': 5.753310e-02, 'sg_ln_b': 7.275942e-02, 'w_s': 9.337228e-02, 'b_s': 9.579037e-02, 'w_o': 3.524086e-01, 'w_1': 1.969250e-01, 'w_2': 6.110684e-01}


def _to_microbatches(a, axis):
    t = _jnp.moveaxis(a, axis, 0)
    t = t.reshape((N_MICROBATCH, t.shape[0] // N_MICROBATCH) + t.shape[1:])
    return _jnp.moveaxis(t, 1, axis + 1)


def setup_inputs(seed: int = 0) -> dict:
    inp = _fwd_setup_inputs(seed)
    key = _jax.random.fold_in(_jax.random.key(seed), 7919)
    shape, _ = _output_shape()
    out = dict(inp)
    out["loss_target"] = _jax.random.normal(_jax.random.fold_in(key, 0), shape, _jnp.float32)
    for i, name in enumerate(TWIN_WEIGHTS):
        w = inp[name].astype(_jnp.float32)
        if MOMENT_SCALE is None:
            s = _jnp.sqrt(_jnp.mean(_jnp.square(w)) + 1e-30)
        else:
            s = MOMENT_SCALE[name]
        km, kv = _jax.random.split(_jax.random.fold_in(key, i + 1))
        out[name] = w
        out["m_" + name] = s * _jax.random.normal(km, w.shape, _jnp.float32)
        out["v_" + name] = (s * s) * _jax.random.uniform(kv, w.shape, _jnp.float32, 0.5, 1.5)
    if N_MICROBATCH > 1:
        for name, axis in PER_EXAMPLE_BATCH_AXIS.items():
            out[name] = _to_microbatches(out[name], axis)
    return {'x': out['x'], 'c': out['c'], 'ctx': out['ctx'], 'c_ctx': out['c_ctx'], 'w_ada': out['w_ada'], 'b_ada': out['b_ada'], 'pre1_g': out['pre1_g'], 'post1_g': out['post1_g'], 'pre2_g': out['pre2_g'], 'post2_g': out['post2_g'], 'w_in': out['w_in'], 'w_dec_f': out['w_dec_f'], 'b_dec_f': out['b_dec_f'], 'w_dec_b': out['w_dec_b'], 'b_dec_b': out['b_dec_b'], 'gla_norm_g': out['gla_norm_g'], 'sg_ln_g': out['sg_ln_g'], 'sg_ln_b': out['sg_ln_b'], 'w_s': out['w_s'], 'b_s': out['b_s'], 'w_o': out['w_o'], 'w_1': out['w_1'], 'w_2': out['w_2'], 'loss_target': out['loss_target'], 'm_c_ctx': out['m_c_ctx'], 'm_w_ada': out['m_w_ada'], 'm_b_ada': out['m_b_ada'], 'm_pre1_g': out['m_pre1_g'], 'm_post1_g': out['m_post1_g'], 'm_pre2_g': out['m_pre2_g'], 'm_post2_g': out['m_post2_g'], 'm_w_in': out['m_w_in'], 'm_w_dec_f': out['m_w_dec_f'], 'm_b_dec_f': out['m_b_dec_f'], 'm_w_dec_b': out['m_w_dec_b'], 'm_b_dec_b': out['m_b_dec_b'], 'm_gla_norm_g': out['m_gla_norm_g'], 'm_sg_ln_g': out['m_sg_ln_g'], 'm_sg_ln_b': out['m_sg_ln_b'], 'm_w_s': out['m_w_s'], 'm_b_s': out['m_b_s'], 'm_w_o': out['m_w_o'], 'm_w_1': out['m_w_1'], 'm_w_2': out['m_w_2'], 'v_c_ctx': out['v_c_ctx'], 'v_w_ada': out['v_w_ada'], 'v_b_ada': out['v_b_ada'], 'v_pre1_g': out['v_pre1_g'], 'v_post1_g': out['v_post1_g'], 'v_pre2_g': out['v_pre2_g'], 'v_post2_g': out['v_post2_g'], 'v_w_in': out['v_w_in'], 'v_w_dec_f': out['v_w_dec_f'], 'v_b_dec_f': out['v_b_dec_f'], 'v_w_dec_b': out['v_w_dec_b'], 'v_b_dec_b': out['v_b_dec_b'], 'v_gla_norm_g': out['v_gla_norm_g'], 'v_sg_ln_g': out['v_sg_ln_g'], 'v_sg_ln_b': out['v_sg_ln_b'], 'v_w_s': out['v_w_s'], 'v_b_s': out['v_b_s'], 'v_w_o': out['v_w_o'], 'v_w_1': out['v_w_1'], 'v_w_2': out['v_w_2']}


def _loss(weights, diff, rest, loss_target):
    with _jax.named_scope("forward"):
        args = {**rest, TWIN_DIFF_INPUT: diff, **{k: w.astype(_WEIGHT_DTYPES[k]) for k, w in weights.items()}}
        y = _forward(args)
    with _jax.named_scope("loss_head"):
        err = _jnp.square(y.astype(_jnp.float32) - loss_target)
        return 0.5 * _jnp.sum(_jnp.mean(err, axis=-1)) if err.ndim else 0.5 * err


def _adamw(w, g, m, v):
    m = ADAM_B1 * m + (1.0 - ADAM_B1) * g
    v = ADAM_B2 * v + (1.0 - ADAM_B2) * _jnp.square(g)
    m_hat = m / (1.0 - ADAM_B1 ** ADAM_STEP)
    v_hat = v / (1.0 - ADAM_B2 ** ADAM_STEP)
    delta = -ADAM_LR * (m_hat / (_jnp.sqrt(v_hat) + ADAM_EPS) + ADAM_WD * w)
    return delta, m, v


def reference(x, c, ctx, c_ctx, w_ada, b_ada, pre1_g, post1_g, pre2_g, post2_g, w_in, w_dec_f, b_dec_f, w_dec_b, b_dec_b, gla_norm_g, sg_ln_g, sg_ln_b, w_s, b_s, w_o, w_1, w_2, loss_target, m_c_ctx, m_w_ada, m_b_ada, m_pre1_g, m_post1_g, m_pre2_g, m_post2_g, m_w_in, m_w_dec_f, m_b_dec_f, m_w_dec_b, m_b_dec_b, m_gla_norm_g, m_sg_ln_g, m_sg_ln_b, m_w_s, m_b_s, m_w_o, m_w_1, m_w_2, v_c_ctx, v_w_ada, v_b_ada, v_pre1_g, v_post1_g, v_pre2_g, v_post2_g, v_w_in, v_w_dec_f, v_b_dec_f, v_w_dec_b, v_b_dec_b, v_gla_norm_g, v_sg_ln_g, v_sg_ln_b, v_w_s, v_b_s, v_w_o, v_w_1, v_w_2):
    given = dict(x=x, c=c, ctx=ctx, c_ctx=c_ctx, w_ada=w_ada, b_ada=b_ada, pre1_g=pre1_g, post1_g=post1_g, pre2_g=pre2_g, post2_g=post2_g, w_in=w_in, w_dec_f=w_dec_f, b_dec_f=b_dec_f, w_dec_b=w_dec_b, b_dec_b=b_dec_b, gla_norm_g=gla_norm_g, sg_ln_g=sg_ln_g, sg_ln_b=sg_ln_b, w_s=w_s, b_s=b_s, w_o=w_o, w_1=w_1, w_2=w_2, loss_target=loss_target, m_c_ctx=m_c_ctx, m_w_ada=m_w_ada, m_b_ada=m_b_ada, m_pre1_g=m_pre1_g, m_post1_g=m_post1_g, m_pre2_g=m_pre2_g, m_post2_g=m_post2_g, m_w_in=m_w_in, m_w_dec_f=m_w_dec_f, m_b_dec_f=m_b_dec_f, m_w_dec_b=m_w_dec_b, m_b_dec_b=m_b_dec_b, m_gla_norm_g=m_gla_norm_g, m_sg_ln_g=m_sg_ln_g, m_sg_ln_b=m_sg_ln_b, m_w_s=m_w_s, m_b_s=m_b_s, m_w_o=m_w_o, m_w_1=m_w_1, m_w_2=m_w_2, v_c_ctx=v_c_ctx, v_w_ada=v_w_ada, v_b_ada=v_b_ada, v_pre1_g=v_pre1_g, v_post1_g=v_post1_g, v_pre2_g=v_pre2_g, v_post2_g=v_post2_g, v_w_in=v_w_in, v_w_dec_f=v_w_dec_f, v_b_dec_f=v_b_dec_f, v_w_dec_b=v_w_dec_b, v_b_dec_b=v_b_dec_b, v_gla_norm_g=v_gla_norm_g, v_sg_ln_g=v_sg_ln_g, v_sg_ln_b=v_sg_ln_b, v_w_s=v_w_s, v_b_s=v_b_s, v_w_o=v_w_o, v_w_1=v_w_1, v_w_2=v_w_2)
    weights = {n: given[n] for n in TWIN_WEIGHTS}
    shared = {n: given[n] for n in SHARED_INPUTS}
    per_example = {n: given[n] for n in ['x', 'c', 'ctx']}
    grad_fn = _jax.value_and_grad(_loss, argnums=(0, 1))

    def one_microbatch(ex, loss_target):
        ex = dict(ex)
        diff = ex.pop(TWIN_DIFF_INPUT)
        return grad_fn(weights, diff, {**shared, **ex}, loss_target)

    if N_MICROBATCH == 1:
        loss, (grad_w, grad_x) = one_microbatch(per_example, given["loss_target"])
    else:
        def body(carry, xs):
            loss_sum, grad_sum = carry
            l_k, (gw_k, gx_k) = one_microbatch(xs[0], xs[1])
            with _jax.named_scope("update"):
                return (loss_sum + l_k, _jax.tree.map(_jnp.add, grad_sum, gw_k)), gx_k

        init = (_jnp.zeros((), _jnp.float32), _jax.tree.map(_jnp.zeros_like, weights))
        (loss, grad_w), grad_x = _jax.lax.scan(body, init, (per_example, given["loss_target"]))
    with _jax.named_scope("update"):
        delta_w, new_m, new_v = {}, {}, {}
        for n in TWIN_WEIGHTS:
            delta_w[n], new_m[n], new_v[n] = _adamw(weights[n], grad_w[n], given["m_" + n], given["v_" + n])
    return (loss, grad_x, *[grad_w[n] for n in TWIN_WEIGHTS], *[delta_w[n] for n in TWIN_WEIGHTS],
            *[new_m[n] for n in TWIN_WEIGHTS], *[new_v[n] for n in TWIN_WEIGHTS])
```

```python
import math

import jax
import jax.numpy as jnp
from jax import lax
from jax.experimental import pallas as pl
from jax.experimental.pallas import tpu as pltpu

F32 = jnp.float32
BF16 = jnp.bfloat16
MXU_DTYPE = jnp.bfloat16
HI = lax.Precision.HIGHEST

N_DEV = 8
AXES = ("x", "y", "c")
MESH = pl.DeviceIdType.MESH
LANES = 128
VMEM_LIMIT = 56 * 1024 * 1024

EPS = 1e-6
GRID_W = 64
GLA_CHUNK = 64
GLA_TAU = 16.0
ROPE_BASE = 10000.0
ADAM_LR = 0.001
ADAM_B1 = 0.9
ADAM_B2 = 0.999
ADAM_EPS = 1e-08
ADAM_WD = 0.01
ADAM_STEP = 10


def _cparams(sem):
    return pltpu.CompilerParams(dimension_semantics=sem, vmem_limit_bytes=VMEM_LIMIT)


def _tile(n, target, align):
    if n <= target:
        return n
    best = None
    for t in range(align, target + 1, align):
        if n % t == 0:
            best = t
    assert best is not None, (n, target, align)
    return best


def _dg(a, b, dims, prec=None):
    return lax.dot_general(a, b, (dims, ((), ())), precision=prec, preferred_element_type=F32)


def _nn(a, b):
    return _dg(a.astype(MXU_DTYPE), b.astype(MXU_DTYPE), ((1,), (0,)))


def _nt(a, b):
    return _dg(a.astype(MXU_DTYPE), b.astype(MXU_DTYPE), ((1,), (1,)))


def _tn(a, b):
    return _dg(a.astype(MXU_DTYPE), b.astype(MXU_DTYPE), ((0,), (0,)))


def _sigmoid(x):
    return 1.0 / (1.0 + jnp.exp(-x))


def _silu(x):
    return x * _sigmoid(x)


def _dsilu(x):
    s = _sigmoid(x)
    return s * (1.0 + x * (1.0 - s))


def _gelu(x):
    return 0.5 * x * (1.0 + lax.erf(x * (1.0 / math.sqrt(2.0))))


def _dgelu(x):
    return 0.5 * (1.0 + lax.erf(x * (1.0 / math.sqrt(2.0)))) + x * jnp.exp(-0.5 * x * x) * (1.0 / math.sqrt(2.0 * math.pi))


def _rstd(x):
    return lax.rsqrt(jnp.mean(x * x, axis=-1, keepdims=True) + EPS)


def _rms_bwd(x, r, dn):
    return r * dn - x * (r * r * r) * jnp.mean(dn * x, axis=-1, keepdims=True)


def _colsum(x):
    return jnp.sum(x, axis=0, keepdims=True)


def _matmul(a, b, mode, m, n, k, *, tm, tn, tk, name, out_shapes, b_spec=None, out_specs=None,
            epilogue=None, extras=()):
    nk = k // tk
    assert m % tm == 0 and n % tn == 0 and k % tk == 0, (name, m, n, k, tm, tn, tk)
    dot = {"nn": _nn, "nt": _nt, "tn": _tn}[mode]
    if mode == "tn":
        a_spec = pl.BlockSpec((tk, tm), lambda i, j, kk: (kk, i))
    else:
        a_spec = pl.BlockSpec((tm, tk), lambda i, j, kk: (i, kk))
    if b_spec is None:
        if mode == "nt":
            b_spec = pl.BlockSpec((tn, tk), lambda i, j, kk: (j, kk))
        else:
            b_spec = pl.BlockSpec((tk, tn), lambda i, j, kk: (kk, j))
    mn_spec = pl.BlockSpec((tm, tn), lambda i, j, kk: (i, j))
    if out_specs is None:
        out_specs = [mn_spec] * len(out_shapes)
    n_extra = len(extras)
    n_out = len(out_shapes)

    def body(a_ref, b_ref, *rest):
        extra_refs = rest[:n_extra]
        out_refs = rest[n_extra:n_extra + n_out]
        acc = rest[n_extra + n_out]
        kk = pl.program_id(2)

        @pl.when(kk == 0)
        def _():
            acc[...] = jnp.zeros_like(acc)

        acc[...] += dot(a_ref[...], b_ref[...])

        @pl.when(kk == nk - 1)
        def _():
            vals = (acc[...],) if epilogue is None else epilogue(acc[...], *[e[...] for e in extra_refs])
            for o, v in zip(out_refs, vals):
                o[...] = v.astype(o.dtype)

    res = pl.pallas_call(
        body, name=name, grid=(m // tm, n // tn, nk),
        in_specs=[a_spec, b_spec] + [mn_spec] * n_extra,
        out_specs=list(out_specs), out_shape=list(out_shapes),
        scratch_shapes=[pltpu.VMEM((tm, tn), F32)],
        compiler_params=_cparams(("parallel", "parallel", "arbitrary")),
    )(a, b, *extras)
    return res


def _blocked_b_nn(ns, tk, tn):
    assert ns % tn == 0
    return pl.BlockSpec((None, tk, tn), lambda i, j, kk: ((j * tn) // ns, kk, ((j * tn) % ns) // tn))


def _blocked_b_nt(ks, tn, tk):
    assert ks % tk == 0
    return pl.BlockSpec((None, tn, tk), lambda i, j, kk: ((kk * tk) // ks, j, ((kk * tk) % ks) // tk))


def _blocked_out(ns, tm, tn):
    assert ns % tn == 0
    return pl.BlockSpec((None, tm, tn), lambda i, j, kk: ((j * tn) // ns, i, ((j * tn) % ns) // tn))


def _rows_call(body, *, name, nblk, tr, row_ins, consts, row_outs, accs=()):
    n_ri, n_c, n_ro, n_acc = len(row_ins), len(consts), len(row_outs), len(accs)

    def kern(*refs):
        i = pl.program_id(0)
        rin = refs[:n_ri]
        cin = refs[n_ri:n_ri + n_c]
        rout = refs[n_ri + n_c:n_ri + n_c + n_ro]
        acc = refs[n_ri + n_c + n_ro:]

        if n_acc:
            @pl.when(i == 0)
            def _():
                for r in acc:
                    r[...] = jnp.zeros_like(r)

        body(rin, cin, rout, acc)

    in_specs = [pl.BlockSpec((tr, w), lambda i, ro=ro, co=co: (i + ro, co)) for (_, w, ro, co) in row_ins]
    in_specs += [pl.BlockSpec(cst.shape, lambda i, nd=cst.ndim: (0,) * nd) for cst in consts]
    out_specs = [pl.BlockSpec((tr, w), lambda i: (i, 0)) for (_, w, _) in row_outs]
    out_specs += [pl.BlockSpec(s, lambda i, nd=len(s): (0,) * nd) for s in accs]
    out_shape = [jax.ShapeDtypeStruct((r, w), dt) for (r, w, dt) in row_outs]
    out_shape += [jax.ShapeDtypeStruct(s, F32) for s in accs]
    return pl.pallas_call(
        kern, name=name, grid=(nblk,), in_specs=in_specs, out_specs=out_specs, out_shape=out_shape,
        compiler_params=_cparams(("arbitrary",)),
    )(*[r[0] for r in row_ins], *consts)


def _norm_mod(x, g, shift, scale, *, name, tr):
    rows, d = x.shape

    def body(rin, cin, rout, acc):
        xv = rin[0][...]
        n = xv * _rstd(xv) * cin[0][...]
        rout[0][...] = (n * (1.0 + cin[2][...]) + cin[1][...]).astype(BF16)

    return _rows_call(body, name=name, nblk=rows // tr, tr=tr, row_ins=[(x, d, 0, 0)],
                      consts=[g, shift, scale], row_outs=[(rows, d, BF16)])[0]


def _swap_halves(t, width):
    lane = lax.broadcasted_iota(jnp.int32, t.shape, 1)
    return jnp.where(lane % 64 < 32, pltpu.roll(t, width - 32, 1), pltpu.roll(t, 32, 1))


def _gla_prep(z, tabs, wdf_pad, wdb_pad, bdf, bdb, cfg, *, tr):
    rows = z.shape[0]
    kw, h = cfg["KW"], cfg["H"]

    def body(rin, cin, rout, acc):
        zq, zk, zl = rin[0][...], rin[1][...], rin[2][...]
        cq, sq, ck, sk = [jnp.concatenate([rin[3 + t][...]] * h, axis=1) for t in range(4)]
        rout[0][...] = zq * cq + _swap_halves(zq, kw) * sq
        rout[1][...] = zk * ck + _swap_halves(zk, kw) * sk
        for o, w, b in ((2, cin[0], cin[2]), (3, cin[1], cin[3])):
            a = _nn(zl, w[...]) + b[...]
            rout[o][...] = (jnp.minimum(a, 0.0) - jnp.log(1.0 + jnp.exp(-jnp.abs(a)))) * (1.0 / GLA_TAU)

    row_ins = [(z, kw, 0, 0), (z, kw, 0, 1), (z, LANES, 0, cfg["L0"] // LANES)]
    row_ins += [(t, LANES, 0, 0) for t in tabs]
    return _rows_call(body, name="gla_prep", nblk=rows // tr, tr=tr, row_ins=row_ins,
                      consts=[wdf_pad, wdb_pad, bdf, bdb], row_outs=[(rows, kw, F32)] * 4)


def _chunk_consts(rev):
    c = GLA_CHUNK
    r = lax.broadcasted_iota(jnp.int32, (c, c), 0)
    cc = lax.broadcasted_iota(jnp.int32, (c, c), 1)
    keep = (cc >= r) if rev else (cc <= r)
    return keep, keep.astype(F32)


def _chunk_decay(la, keep_f):
    b = _dg(keep_f, la, ((1,), (0,)), HI)
    return b, _colsum(la)


def _gla_fwd(qr, kr, z, la, st0, cfg, *, rev, row_off, nrows, tb, name):
    h, dk, dv = cfg["H"], cfg["DK"], cfg["DV"]
    c = GLA_CHUNK
    nsub = tb // c
    nblk = nrows // tb
    roff = row_off // tb
    v_cb = cfg["V0"] // dv

    def blk(j):
        return (nblk - 1 - j) if rev else j

    def body(q_ref, k_ref, v_ref, la_ref, st0_ref, o_ref, save_ref, fin_ref, st):
        j = pl.program_id(1)

        @pl.when(j == 0)
        def _():
            st[...] = st0_ref[...]

        keep, keep_f = _chunk_consts(rev)
        order = range(nsub - 1, -1, -1) if rev else range(nsub)
        for s in order:
            rs = pl.ds(s * c, c)
            q, k, v, lac = q_ref[rs, :], k_ref[rs, :], v_ref[rs, :], la_ref[rs, :]
            b, btot = _chunk_decay(lac, keep_f)
            qe = q * jnp.exp(b)
            ke = k * jnp.exp(-b)
            kl = k * jnp.exp(btot - b)
            s_in = st[...]
            save_ref[s] = s_in
            att = jnp.where(keep, _nt(qe, ke), 0.0)
            o_ref[rs, :] = _nt(qe, s_in) + _nn(att, v)
            st[...] = s_in * jnp.exp(btot) + _tn(v, kl)

        @pl.when(j == nblk - 1)
        def _():
            fin_ref[...] = st[...]

    in_specs = [
        pl.BlockSpec((tb, dk), lambda hh, j: (roff + blk(j), hh)),
        pl.BlockSpec((tb, dk), lambda hh, j: (roff + blk(j), hh)),
        pl.BlockSpec((tb, dv), lambda hh, j: (roff + blk(j), v_cb + hh)),
        pl.BlockSpec((tb, dk), lambda hh, j: (roff + blk(j), hh)),
        pl.BlockSpec((None, dv, dk), lambda hh, j: (hh, 0, 0)),
    ]
    out_specs = [
        pl.BlockSpec((tb, dv), lambda hh, j: (blk(j), hh)),
        pl.BlockSpec((None, nsub, dv, dk), lambda hh, j: (hh, blk(j), 0, 0)),
        pl.BlockSpec((None, dv, dk), lambda hh, j: (hh, 0, 0)),
    ]
    out_shape = [
        jax.ShapeDtypeStruct((nrows, h * dv), F32),
        jax.ShapeDtypeStruct((h, nrows // c, dv, dk), F32),
        jax.ShapeDtypeStruct((h, dv, dk), F32),
    ]
    return pl.pallas_call(
        body, name=name, grid=(h, nblk), in_specs=in_specs, out_specs=out_specs, out_shape=out_shape,
        scratch_shapes=[pltpu.VMEM((dv, dk), F32)],
        compiler_params=_cparams(("arbitrary", "arbitrary")),
    )(qr, kr, z, la, st0)


def _gla_bwd(qr, kr, z, la, do, save, dst_init, cfg, *, rev, row_off, nrows, tb, name):
    h, dk, dv = cfg["H"], cfg["DK"], cfg["DV"]
    c = GLA_CHUNK
    nsub = tb // c
    nblk = nrows // tb
    roff = row_off // tb
    v_cb = cfg["V0"] // dv

    def blk(j):
        return j if rev else (nblk - 1 - j)

    def body(q_ref, k_ref, v_ref, la_ref, do_ref, save_ref, di_ref, dq_ref, dk_ref, dv_ref, dla_ref, d0_ref, dst):
        j = pl.program_id(1)

        @pl.when(j == 0)
        def _():
            dst[...] = di_ref[...]

        keep, keep_f = _chunk_consts(rev)
        keep_t = _chunk_consts(not rev)[1]
        order = range(nsub) if rev else range(nsub - 1, -1, -1)
        for s in order:
            rs = pl.ds(s * c, c)
            q, k, v, lac, dout = q_ref[rs, :], k_ref[rs, :], v_ref[rs, :], la_ref[rs, :], do_ref[rs, :]
            b, btot = _chunk_decay(lac, keep_f)
            eb = jnp.exp(b)
            enb = jnp.exp(-b)
            elb = jnp.exp(btot - b)
            etot = jnp.exp(btot)
            qe, ke, kl = q * eb, k * enb, k * elb
            s_in = save_ref[s]
            d_after = dst[...]
            att = jnp.where(keep, _nt(qe, ke), 0.0)
            datt = jnp.where(keep, _nt(dout, v), 0.0)
            dqe = _nn(dout, s_in) + _nn(datt, ke)
            dke = _tn(datt, qe)
            dkl = _nn(v, d_after)
            dv_ref[rs, :] = _tn(att, dout) + _nt(kl, d_after)
            db = dqe * qe - dke * ke - dkl * kl
            dbtot = _colsum(dkl * kl) + _colsum(d_after * s_in) * etot
            dla_ref[rs, :] = _dg(keep_t, db, ((1,), (0,)), HI) + dbtot
            dq_ref[rs, :] = dqe * eb
            dk_ref[rs, :] = dke * enb + dkl * elb
            dst[...] = d_after * etot + _tn(dout, qe)

        @pl.when(j == nblk - 1)
        def _():
            d0_ref[...] = dst[...]

    in_specs = [
        pl.BlockSpec((tb, dk), lambda hh, j: (roff + blk(j), hh)),
        pl.BlockSpec((tb, dk), lambda hh, j: (roff + blk(j), hh)),
        pl.BlockSpec((tb, dv), lambda hh, j: (roff + blk(j), v_cb + hh)),
        pl.BlockSpec((tb, dk), lambda hh, j: (roff + blk(j), hh)),
        pl.BlockSpec((tb, dv), lambda hh, j: (blk(j), hh)),
        pl.BlockSpec((None, nsub, dv, dk), lambda hh, j: (hh, blk(j), 0, 0)),
        pl.BlockSpec((None, dv, dk), lambda hh, j: (hh, 0, 0)),
    ]
    out_specs = [
        pl.BlockSpec((tb, dk), lambda hh, j: (blk(j), hh)),
        pl.BlockSpec((tb, dk), lambda hh, j: (blk(j), hh)),
        pl.BlockSpec((tb, dv), lambda hh, j: (blk(j), hh)),
        pl.BlockSpec((tb, dk), lambda hh, j: (blk(j), hh)),
        pl.BlockSpec((None, dv, dk), lambda hh, j: (hh, 0, 0)),
    ]
    out_shape = [
        jax.ShapeDtypeStruct((nrows, h * dk), F32),
        jax.ShapeDtypeStruct((nrows, h * dk), F32),
        jax.ShapeDtypeStruct((nrows, h * dv), F32),
        jax.ShapeDtypeStruct((nrows, h * dk), F32),
        jax.ShapeDtypeStruct((h, dv, dk), F32),
    ]
    return pl.pallas_call(
        body, name=name, grid=(h, nblk), in_specs=in_specs, out_specs=out_specs, out_shape=out_shape,
        scratch_shapes=[pltpu.VMEM((dv, dk), F32)],
        compiler_params=_cparams(("arbitrary", "arbitrary")),
    )(qr, kr, z, la, do, save, dst_init)


def _gla_post(gf, gb, la_f, la_b, z, tabs, wdf_pad, wdb_pad, cfg, *, row_off, nrows, tr, name):
    kw, vw = cfg["KW"], cfg["VW"]
    h = cfg["H"]
    ro = row_off // tr

    def body(rin, cin, rout, acc):
        dq = rin[0][...] + rin[1][...]
        dk_ = rin[2][...] + rin[3][...]
        zl = rin[10][...]
        cq, sq, ck, sk = [jnp.concatenate([rin[11 + t][...]] * h, axis=1) for t in range(4)]
        rout[0][...] = (dq * cq + _swap_halves(dq * sq, kw)).astype(BF16)
        rout[1][...] = (dk_ * ck + _swap_halves(dk_ * sk, kw)).astype(BF16)
        rout[2][...] = (rin[8][...] + rin[9][...]).astype(BF16)
        dzl = jnp.zeros(zl.shape, F32)
        for t, w in ((0, cin[0]), (1, cin[1])):
            la = rin[6 + t][...]
            da = rin[4 + t][...] * ((1.0 - jnp.exp(la * GLA_TAU)) * (1.0 / GLA_TAU))
            dzl = dzl + _nt(da, w[...])
            acc[t][...] += _tn(zl, da)
            acc[2 + t][...] += _colsum(da)
        rout[3][...] = dzl.astype(BF16)

    row_ins = [(gf[0], kw, 0, 0), (gb[0], kw, 0, 0), (gf[1], kw, 0, 0), (gb[1], kw, 0, 0),
               (gf[3], kw, 0, 0), (gb[3], kw, 0, 0), (la_f, kw, ro, 0), (la_b, kw, ro, 0),
               (gf[2], vw, 0, 0), (gb[2], vw, 0, 0), (z, LANES, ro, cfg["L0"] // LANES)]
    row_ins += [(t, LANES, ro, 0) for t in tabs]
    return _rows_call(body, name=name, nblk=nrows // tr, tr=tr, row_ins=row_ins, consts=[wdf_pad, wdb_pad],
                      row_outs=[(nrows, kw, BF16), (nrows, kw, BF16), (nrows, vw, BF16), (nrows, LANES, BF16)],
                      accs=[(LANES, kw), (LANES, kw), (1, kw), (1, kw)])


def _readout_fwd(o_f, o_b, z, g, cfg, *, tr):
    n, vw = o_f.shape
    h, dv = cfg["H"], cfg["DV"]

    def body(rin, cin, rout, acc):
        for hh in range(h):
            cs = slice(hh * dv, (hh + 1) * dv)
            oh = rin[0][:, cs] + rin[1][:, cs]
            y = oh * _rstd(oh) * cin[0][:, cs]
            rout[0][:, cs] = (y * _silu(rin[2][:, cs])).astype(BF16)

    return _rows_call(body, name="gla_readout", nblk=n // tr, tr=tr,
                      row_ins=[(o_f, vw, 0, 0), (o_b, vw, 0, 0), (z, vw, 0, cfg["R0"] // vw)], consts=[g],
                      row_outs=[(n, vw, BF16)])[0]


def _readout_bwd(o_f, o_b, z, dycat, g, cfg, *, tr):
    n, vw = o_f.shape
    h, dv = cfg["H"], cfg["DV"]

    def body(rin, cin, rout, acc):
        for hh in range(h):
            cs = slice(hh * dv, (hh + 1) * dv)
            oh = rin[0][:, cs] + rin[1][:, cs]
            r, dyg, gh = rin[2][:, cs], rin[3][:, cs], cin[0][:, cs]
            rs = _rstd(oh)
            dy = dyg * _silu(r)
            rout[0][:, cs] = _rms_bwd(oh, rs, dy * gh).astype(BF16)
            rout[1][:, cs] = (dyg * (oh * rs * gh) * _dsilu(r)).astype(BF16)
            acc[0][:, cs] += _colsum(dy * oh * rs)

    return _rows_call(body, name="gla_readout_bwd", nblk=n // tr, tr=tr,
                      row_ins=[(o_f, vw, 0, 0), (o_b, vw, 0, 0), (z, vw, 0, cfg["R0"] // vw), (dycat, vw, 0, 0)],
                      consts=[g], row_outs=[(n, vw, BF16), (n, vw, BF16)], accs=[(1, vw)])


def _sg_ln(vv):
    mu = jnp.mean(vv, axis=-1, keepdims=True)
    cen = vv - mu
    rstd = lax.rsqrt(jnp.mean(cen * cen, axis=-1, keepdims=True) + EPS)
    return cen * rstd, rstd


def _sg_fwd(z, n, lng, lnb, w_s, bs_full, cfg):
    sgw, grp, sc = cfg["SGW"], cfg["SG_G"], cfg["SG_C"]
    gw = sgw // grp

    def body(rin, cin, rout, acc):
        u = _gelu(rin[0][...])
        xhat, _ = _sg_ln(_gelu(rin[1][...]))
        vvn = xhat * cin[0][...] + cin[1][...]
        for gg in range(grp):
            cs = slice(gg * gw, (gg + 1) * gw)
            s = _nn(cin[2][gg], vvn[:, cs]) + cin[3][:, cs]
            rout[0][:, cs] = (u[:, cs] * s).astype(BF16)

    return _rows_call(body, name="sg_fwd", nblk=n // sc, tr=sc,
                      row_ins=[(z, sgw, 0, cfg["U0"] // sgw), (z, sgw, 0, cfg["VV0"] // sgw)],
                      consts=[lng, lnb, w_s, bs_full], row_outs=[(n, sgw, BF16)])[0]


def _sg_bwd(z, dycat, n, lng, lnb, w_s, bs_full, cfg):
    sgw, grp, sc = cfg["SGW"], cfg["SG_G"], cfg["SG_C"]
    gw = sgw // grp

    def body(rin, cin, rout, acc):
        up, vp, dy = rin[0][...], rin[1][...], rin[2][...]
        u = _gelu(up)
        xhat, rstd = _sg_ln(_gelu(vp))
        lng_v = cin[0][...]
        vvn = xhat * lng_v + cin[1][...]
        ds = dy * u
        acc[1][...] += ds
        dvvn_parts = []
        for gg in range(grp):
            cs = slice(gg * gw, (gg + 1) * gw)
            w = cin[2][gg]
            s = _nn(w, vvn[:, cs]) + cin[3][:, cs]
            rout[0][:, cs] = (dy[:, cs] * s * _dgelu(up[:, cs])).astype(BF16)
            acc[0][gg] += _nt(ds[:, cs], vvn[:, cs])
            dvvn_parts.append(_tn(w, ds[:, cs]))
        dvvn = jnp.concatenate(dvvn_parts, axis=1)
        acc[2][...] += _colsum(dvvn * xhat)
        acc[3][...] += _colsum(dvvn)
        dxh = dvvn * lng_v
        dvv = rstd * (dxh - jnp.mean(dxh, axis=-1, keepdims=True)
                      - xhat * jnp.mean(dxh * xhat, axis=-1, keepdims=True))
        rout[1][...] = (dvv * _dgelu(vp)).astype(BF16)

    vw = cfg["VW"]
    return _rows_call(body, name="sg_bwd", nblk=n // sc, tr=sc,
                      row_ins=[(z, sgw, 0, cfg["U0"] // sgw), (z, sgw, 0, cfg["VV0"] // sgw),
                               (dycat, sgw, 0, vw // sgw)],
                      consts=[lng, lnb, w_s, bs_full], row_outs=[(n, sgw, BF16), (n, sgw, BF16)],
                      accs=[(grp, sc, sc), (sc, sgw), (1, sgw), (1, sgw)])


def _mid_fwd(x, mix, g1, post1, pre2, sh2, sc2, *, tr):
    n, d = x.shape

    def body(rin, cin, rout, acc):
        xv, mv = rin[0][...], rin[1][...]
        x1 = xv + cin[0][...] * (mv * _rstd(mv) * cin[1][...])
        rout[0][...] = x1
        n2 = x1 * _rstd(x1) * cin[2][...]
        rout[1][...] = (n2 * (1.0 + cin[4][...]) + cin[3][...]).astype(BF16)

    return _rows_call(body, name="mid_fwd", nblk=n // tr, tr=tr, row_ins=[(x, d, 0, 0), (mix, d, 0, 0)],
                      consts=[g1, post1, pre2, sh2, sc2], row_outs=[(n, d, F32), (n, d, BF16)])


def _head_bwd(x1, m2, target, g2, post2, *, tr):
    n, d = x1.shape

    def body(rin, cin, rout, acc):
        x1v, mv, tv = rin[0][...], rin[1][...], rin[2][...]
        g2v, pg = cin[0][...], cin[1][...]
        r = _rstd(mv)
        y2 = mv * r * pg
        err = (x1v + g2v * y2) - tv
        acc[2][...] += _colsum(err * err) * (0.5 / d)
        dx2 = err * (1.0 / d)
        rout[0][...] = dx2
        dy2 = dx2 * g2v
        acc[0][...] += _colsum(dx2 * y2)
        acc[1][...] += _colsum(dy2 * mv * r)
        rout[1][...] = _rms_bwd(mv, r, dy2 * pg).astype(BF16)

    return _rows_call(body, name="head_bwd", nblk=n // tr, tr=tr,
                      row_ins=[(x1, d, 0, 0), (m2, d, 0, 0), (target, d, 0, 0)], consts=[g2, post2],
                      row_outs=[(n, d, F32), (n, d, BF16)], accs=[(1, d)] * 3)


def _mid_bwd(dh2, x1, dx2, mix, sc2, pre2, g1, post1, *, tr):
    n, d = x1.shape

    def body(rin, cin, rout, acc):
        dh, x1v, dx2v, mv = rin[0][...], rin[1][...], rin[2][...], rin[3][...]
        sc2v, pre2v, g1v, post1v = cin[0][...], cin[1][...], cin[2][...], cin[3][...]
        r2 = _rstd(x1v)
        xr = x1v * r2
        acc[0][...] += _colsum(dh)
        acc[1][...] += _colsum(dh * (xr * pre2v))
        dn2 = dh * (1.0 + sc2v)
        acc[2][...] += _colsum(dn2 * xr)
        dx1 = dx2v + _rms_bwd(x1v, r2, dn2 * pre2v)
        rout[0][...] = dx1
        r1 = _rstd(mv)
        mr = mv * r1
        acc[3][...] += _colsum(dx1 * (mr * post1v))
        dy1 = dx1 * g1v
        acc[4][...] += _colsum(dy1 * mr)
        rout[1][...] = _rms_bwd(mv, r1, dy1 * post1v).astype(BF16)

    return _rows_call(body, name="mid_bwd", nblk=n // tr, tr=tr,
                      row_ins=[(dh2, d, 0, 0), (x1, d, 0, 0), (dx2, d, 0, 0), (mix, d, 0, 0)],
                      consts=[sc2, pre2, g1, post1], row_outs=[(n, d, F32), (n, d, BF16)], accs=[(1, d)] * 5)


def _in_bwd(da, x, dres, sc1, pre1, *, row_off, tr, name):
    n, d = x.shape
    with_res = dres is not None

    def body(rin, cin, rout, acc):
        dav, xv = rin[0][...], rin[1][...]
        sc1v, pre1v = cin[0][...], cin[1][...]
        r = _rstd(xv)
        xr = xv * r
        acc[0][...] += _colsum(dav)
        acc[1][...] += _colsum(dav * (xr * pre1v))
        dn = dav * (1.0 + sc1v)
        acc[2][...] += _colsum(dn * xr)
        if with_res:
            rout[0][...] = rin[2][...] + _rms_bwd(xv, r, dn * pre1v)

    row_ins = [(da, d, row_off // tr, 0), (x, d, 0, 0)] + ([(dres, d, 0, 0)] if with_res else [])
    return _rows_call(body, name=name, nblk=n // tr, tr=tr, row_ins=row_ins, consts=[sc1, pre1],
                      row_outs=[(n, d, F32)] if with_res else [], accs=[(1, d)] * 3)


def _ada_fwd(c16, w, b, *, tn):
    d, ncol = w.shape

    def body(c_ref, w_ref, b_ref, o_ref):
        o_ref[...] = _nn(_silu(c_ref[...]), w_ref[...]) + b_ref[...]

    return pl.pallas_call(
        body, name="ada_fwd", grid=(ncol // tn,),
        in_specs=[pl.BlockSpec((16, d), lambda j: (0, 0)), pl.BlockSpec((d, tn), lambda j: (0, j)),
                  pl.BlockSpec((1, tn), lambda j: (0, j))],
        out_specs=pl.BlockSpec((16, tn), lambda j: (0, j)),
        out_shape=jax.ShapeDtypeStruct((16, ncol), F32),
        compiler_params=_cparams(("arbitrary",)),
    )(c16, w, b)


def _ada_bwd(c16, dm, w, c_ctx, *, tn):
    d, ncol = w.shape

    def body(c_ref, dm_ref, w_ref, cc_ref, gw_ref, dcc_ref, acc):
        j = pl.program_id(0)

        @pl.when(j == 0)
        def _():
            acc[...] = jnp.zeros_like(acc)

        gw_ref[...] = _tn(_silu(c_ref[...]), dm_ref[...])
        acc[...] += _nt(dm_ref[...], w_ref[...])

        @pl.when(j == ncol // tn - 1)
        def _():
            dcc_ref[...] = _colsum(acc[8:16, :]) * _dsilu(cc_ref[...])

    return pl.pallas_call(
        body, name="ada_bwd", grid=(ncol // tn,),
        in_specs=[pl.BlockSpec((16, d), lambda j: (0, 0)), pl.BlockSpec((16, tn), lambda j: (0, j)),
                  pl.BlockSpec((d, tn), lambda j: (0, j)), pl.BlockSpec((1, d), lambda j: (0, 0))],
        out_specs=[pl.BlockSpec((d, tn), lambda j: (0, j)), pl.BlockSpec((1, d), lambda j: (0, 0))],
        out_shape=[jax.ShapeDtypeStruct((d, ncol), F32), jax.ShapeDtypeStruct((1, d), F32)],
        scratch_shapes=[pltpu.VMEM((16, d), F32)],
        compiler_params=_cparams(("arbitrary",)),
    )(c16, dm, w, c_ctx)


def _adam_math(w, g, m, v):
    m = ADAM_B1 * m + (1.0 - ADAM_B1) * g
    v = ADAM_B2 * v + (1.0 - ADAM_B2) * (g * g)
    m_hat = m / (1.0 - ADAM_B1 ** ADAM_STEP)
    v_hat = v / (1.0 - ADAM_B2 ** ADAM_STEP)
    delta = -ADAM_LR * (m_hat / (jnp.sqrt(v_hat) + ADAM_EPS) + ADAM_WD * w)
    return delta, m, v


def _adam_big(parts, w, m, v, *, name, tr):
    rows, cols = w.shape
    n_p = len(parts)

    def body(*refs):
        g = refs[0][...]
        for p in refs[1:n_p]:
            g = g + p[...]
        wv, mv, vv = refs[n_p][...], refs[n_p + 1][...], refs[n_p + 2][...]
        delta, m2, v2 = _adam_math(wv, g, mv, vv)
        refs[n_p + 3][...] = g
        refs[n_p + 4][...] = delta
        refs[n_p + 5][...] = m2
        refs[n_p + 6][...] = v2

    plain = pl.BlockSpec((tr, cols), lambda i: (i, 0))
    in_specs = []
    for arr, idx in parts:
        if idx is None:
            in_specs.append(plain)
        else:
            in_specs.append(pl.BlockSpec((None, tr, cols), lambda i, idx=idx: (idx, i, 0)))
    in_specs += [plain] * 3
    return pl.pallas_call(
        body, name=name, grid=(rows // tr,), in_specs=in_specs, out_specs=[plain] * 4,
        out_shape=[jax.ShapeDtypeStruct((rows, cols), F32)] * 4,
        compiler_params=_cparams(("parallel",)),
    )(*[p[0] for p in parts], w, m, v)


def _adam_small(g8, w, m, v):
    def body(g_ref, w_ref, m_ref, v_ref, go, do, mo, vo):
        g = g_ref[0]
        for r in range(1, N_DEV):
            g = g + g_ref[r]
        delta, m2, v2 = _adam_math(w_ref[...], g, m_ref[...], v_ref[...])
        go[...] = g
        do[...] = delta
        mo[...] = m2
        vo[...] = v2

    return pl.pallas_call(
        body, name="adam_small", out_shape=[jax.ShapeDtypeStruct(w.shape, F32)] * 4,
        compiler_params=pltpu.CompilerParams(vmem_limit_bytes=VMEM_LIMIT),
    )(g8, w, m, v)


VEC_W = 1024
ELEMS_PER_BLOCK = 256 * 1024


def _dense(v):
    a, k = v.shape
    kp = -(-k // (8 * VEC_W)) * (8 * VEC_W)
    return jnp.pad(v, ((0, 0), (0, kp - k))).reshape(a, kp // VEC_W, VEC_W)


def _all_gather_vec(v, *, name):
    k = v.shape[1]
    return _all_gather_small(_dense(v)[0], name=name).reshape(N_DEV, -1)[:, :k]


def _my_pos():
    return lax.axis_index("x"), lax.axis_index("y"), lax.axis_index("c")


def _flip(v, bit):
    return (1 - v) if bit else v


def _all_gather_small(v, *, name):
    r, k = v.shape

    def body(v_ref, out_ref, send, recv, lsem):
        x, y, c = _my_pos()
        me = 4 * x + 2 * y + c
        local = pltpu.make_async_copy(v_ref, out_ref.at[me], lsem)
        local.start()
        sends = []
        for kk in range(1, N_DEV):
            peer = (_flip(x, kk & 4), _flip(y, kk & 2), _flip(c, kk & 1))
            cp = pltpu.make_async_remote_copy(src_ref=v_ref, dst_ref=out_ref.at[me], send_sem=send.at[kk - 1],
                                              recv_sem=recv.at[kk - 1], device_id=peer, device_id_type=MESH)
            cp.start()
            sends.append(cp)
        for kk in range(1, N_DEV):
            px, py, pc = _flip(x, kk & 4), _flip(y, kk & 2), _flip(c, kk & 1)
            src = 4 * px + 2 * py + pc
            pltpu.make_async_remote_copy(src_ref=v_ref, dst_ref=out_ref.at[src], send_sem=send.at[kk - 1],
                                         recv_sem=recv.at[kk - 1], device_id=(px, py, pc),
                                         device_id_type=MESH).wait_recv()
        for cp in sends:
            cp.wait_send()
        local.wait()

    return pl.pallas_call(
        body, name=name, out_shape=jax.ShapeDtypeStruct((N_DEV, r, k), v.dtype),
        in_specs=[pl.BlockSpec(memory_space=pltpu.VMEM)], out_specs=pl.BlockSpec(memory_space=pltpu.VMEM),
        scratch_shapes=[pltpu.SemaphoreType.DMA((N_DEV - 1,)), pltpu.SemaphoreType.DMA((N_DEV - 1,)),
                        pltpu.SemaphoreType.DMA],
        compiler_params=pltpu.CompilerParams(vmem_limit_bytes=VMEM_LIMIT),
    )(v)


def _all_gather_big(shards, *, name):
    n_arr = len(shards)

    def body(*refs):
        ins, outs = refs[:n_arr], refs[n_arr:2 * n_arr]
        send, recv, lsem = refs[2 * n_arr:]
        x, y, c = _my_pos()
        me, sibling = (x, y, c), (x, y, 1 - c)
        chips = [(1 - x, y), (x, 1 - y), (1 - x, 1 - y)]

        def copy(a, kk, block, to, src=None):
            dst = outs[a].at[4 * block[0] + 2 * block[1] + block[2]]
            return pltpu.make_async_remote_copy(src_ref=dst if src is None else src, dst_ref=dst,
                                                send_sem=send.at[a, kk], recv_sem=recv.at[a, kk],
                                                device_id=to, device_id_type=MESH)

        locals_, firsts, passed = [], [], []
        for a in range(n_arr):
            lc = pltpu.make_async_copy(ins[a], outs[a].at[4 * x + 2 * y + c], lsem.at[a])
            lc.start()
            locals_.append(lc)
            firsts.append(copy(a, 0, me, sibling, src=ins[a]))
            firsts += [copy(a, 1 + j, me, (*chip, c), src=ins[a]) for j, chip in enumerate(chips)]
        for cp in firsts:
            cp.start()
        for a in range(n_arr):
            for j, chip in enumerate(chips):
                copy(a, 1 + j, (*chip, c), me).wait_recv()
                fw = copy(a, 4 + j, (*chip, c), sibling)
                fw.start()
                passed.append(fw)
        for a in range(n_arr):
            copy(a, 0, sibling, me).wait_recv()
            for j, chip in enumerate(chips):
                copy(a, 4 + j, (*chip, 1 - c), me).wait_recv()
        for cp in firsts + passed:
            cp.wait_send()
        for lc in locals_:
            lc.wait()

    anyspec = pl.BlockSpec(memory_space=pl.ANY)
    return pl.pallas_call(
        body, name=name,
        out_shape=[jax.ShapeDtypeStruct((N_DEV,) + s.shape, s.dtype) for s in shards],
        in_specs=[anyspec] * n_arr, out_specs=[anyspec] * n_arr,
        scratch_shapes=[pltpu.SemaphoreType.DMA((n_arr, 7)), pltpu.SemaphoreType.DMA((n_arr, 7)),
                        pltpu.SemaphoreType.DMA((n_arr,))],
    )(*shards)


def _pair_exchange(grads, *, name):
    n_arr = len(grads)

    def body(*refs):
        ins, outs = refs[:n_arr], refs[n_arr:2 * n_arr]
        send, recv = refs[2 * n_arr:]
        x, y, c = _my_pos()
        sends = []
        for a in range(n_arr):
            for j in range(4):
                tx, ty = _flip(x, j & 2), _flip(y, j & 1)
                cp = pltpu.make_async_remote_copy(
                    src_ref=ins[a].at[4 * tx + 2 * ty + (1 - c)], dst_ref=outs[a].at[j],
                    send_sem=send.at[a, j], recv_sem=recv.at[a, j], device_id=(x, y, 1 - c), device_id_type=MESH)
                cp.start()
                sends.append(cp)
        for cp in sends:
            cp.wait_recv()
        for cp in sends:
            cp.wait_send()

    anyspec = pl.BlockSpec(memory_space=pl.ANY)
    return pl.pallas_call(
        body, name=name,
        out_shape=[jax.ShapeDtypeStruct((4,) + g.shape[1:], g.dtype) for g in grads],
        in_specs=[anyspec] * n_arr, out_specs=[anyspec] * n_arr,
        scratch_shapes=[pltpu.SemaphoreType.DMA((n_arr, 4)), pltpu.SemaphoreType.DMA((n_arr, 4))],
    )(*grads)


def _pair_add(g, t, pos, *, name, tr):
    _, r, cols = g.shape
    g4 = g.reshape(4, 2, r, cols)

    def body(pos_ref, g_ref, t_ref, o_ref):
        o_ref[...] = g_ref[...] + t_ref[...]

    grid_spec = pltpu.PrefetchScalarGridSpec(
        num_scalar_prefetch=1, grid=(4, r // tr),
        in_specs=[pl.BlockSpec((None, None, tr, cols), lambda j, i, p: (jnp.bitwise_xor(p[0], j), p[1], i, 0)),
                  pl.BlockSpec((None, tr, cols), lambda j, i, p: (j, i, 0))],
        out_specs=pl.BlockSpec((None, tr, cols), lambda j, i, p: (j, i, 0)))
    return pl.pallas_call(
        body, name=name, grid_spec=grid_spec, out_shape=jax.ShapeDtypeStruct((4, r, cols), F32),
        compiler_params=_cparams(("arbitrary", "arbitrary")),
    )(pos, g4, t)


def _chip_exchange(sums, *, name):
    n_arr = len(sums)

    def body(*refs):
        ins, outs = refs[:n_arr], refs[n_arr:2 * n_arr]
        send, recv = refs[2 * n_arr:]
        x, y, c = _my_pos()
        sends = []
        for a in range(n_arr):
            for j in range(1, 4):
                cp = pltpu.make_async_remote_copy(
                    src_ref=ins[a].at[j], dst_ref=outs[a].at[j - 1], send_sem=send.at[a, j - 1],
                    recv_sem=recv.at[a, j - 1], device_id=(_flip(x, j & 2), _flip(y, j & 1), c),
                    device_id_type=MESH)
                cp.start()
                sends.append(cp)
        for cp in sends:
            cp.wait_recv()
        for cp in sends:
            cp.wait_send()

    anyspec = pl.BlockSpec(memory_space=pl.ANY)
    return pl.pallas_call(
        body, name=name,
        out_shape=[jax.ShapeDtypeStruct((3,) + s.shape[1:], s.dtype) for s in sums],
        in_specs=[anyspec] * n_arr, out_specs=[anyspec] * n_arr,
        scratch_shapes=[pltpu.SemaphoreType.DMA((n_arr, 3)), pltpu.SemaphoreType.DMA((n_arr, 3))],
    )(*sums)


def _config(x, ctx, w_in, w_dec_f, gla_norm_g, sg_ln_g, w_s):
    n, d = x.shape[1], x.shape[2]
    tc = ctx.shape[1]
    h = gla_norm_g.shape[1]
    dv = gla_norm_g.shape[2] * N_DEV
    dk = dv // 2
    kw, vw = h * dk, h * dv
    lr = w_dec_f.shape[1]
    sgw = sg_ln_g.shape[1]
    cfg = dict(N=n, D=d, TC=tc, H=h, DV=dv, DK=dk, KW=kw, VW=vw, LR=lr, SGW=sgw, SG_G=w_s.shape[1],
               SG_C=w_s.shape[2], IN=w_in.shape[2] * N_DEV)
    cfg.update(K0=kw, V0=2 * kw, R0=2 * kw + vw, U0=2 * kw + 2 * vw)
    cfg.update(VV0=cfg["U0"] + sgw, L0=cfg["U0"] + 2 * sgw, ZW=cfg["U0"] + 2 * sgw + LANES)
    assert dk == LANES and vw == 2 * kw and 2 * lr <= LANES
    assert cfg["R0"] % vw == 0 and cfg["U0"] % sgw == 0 and cfg["VV0"] % sgw == 0 and vw % sgw == 0
    assert cfg["IN"] == 2 * kw + 2 * vw + 2 * lr + 2 * sgw
    return cfg


def _rope_tables(cfg):
    n, tc, dk = cfg["N"], cfg["TC"], cfg["DK"]
    m = dk // 4
    pos = jnp.arange(n)
    inv = ROPE_BASE ** (-jnp.arange(m, dtype=F32) / m)
    ang_r = (pos // GRID_W).astype(F32)[:, None] * inv[None, :]
    ang_c = (pos % GRID_W).astype(F32)[:, None] * inv[None, :]
    cos = jnp.concatenate([jnp.cos(ang_r)] * 2 + [jnp.cos(ang_c)] * 2, axis=1)
    sin = jnp.concatenate([-jnp.sin(ang_r), jnp.sin(ang_r), -jnp.sin(ang_c), jnp.sin(ang_c)], axis=1)
    scale = dk ** -0.5
    z = jnp.zeros((tc, dk), F32)
    one = jnp.ones((tc, dk), F32)
    return [jnp.concatenate([cos * scale, z]), jnp.concatenate([sin * scale, z]),
            jnp.concatenate([cos, one]), jnp.concatenate([sin, z])]


def _local_step(x, ctx, target, mods, c_mods, w, cfg):
    n, d, tc = cfg["N"], cfg["D"], cfg["TC"]
    kw, vw, sgw, zw = cfg["KW"], cfg["VW"], cfg["SGW"], cfg["ZW"]
    sh1, sc1, g1, sh2, sc2, g2 = mods
    csh1, csc1 = c_mods
    rt = n + tc
    tb = math.gcd(256, math.gcd(n, tc))
    tr = math.gcd(128, tb)
    tr_s = math.gcd(64, tb)
    fs = w["w_1"].shape[2]
    ff = fs * N_DEV

    hx = _norm_mod(x, w["pre1_g"], sh1, sc1, name="in_norm_x", tr=tr)
    hc = _norm_mod(ctx, w["pre1_g"], csh1, csc1, name="in_norm_ctx", tr=tr)
    a_all = jnp.concatenate([hx, hc], axis=0)

    tm_a = _tile(rt, 1152, 16)
    z = _matmul(a_all, w["w_pad"], "nn", rt, zw, d, tm=tm_a, tn=_tile(zw, 1152, LANES), tk=_tile(d, 512, LANES),
                name="mm_in", out_shapes=[jax.ShapeDtypeStruct((rt, zw), F32)])[0]

    tabs = _rope_tables(cfg)
    qr, kr, la_f, la_b = _gla_prep(z, tabs, w["wdf_pad"], w["wdb_pad"], w["b_dec_f"], w["b_dec_b"], cfg, tr=tr)

    zero_st = jnp.zeros((cfg["H"], cfg["DV"], cfg["DK"]), F32)
    gla = dict(cfg=cfg, tb=tb)
    _, save_cf, st_cf = _gla_fwd(qr, kr, z, la_f, zero_st, rev=False, row_off=n, nrows=tc, name="gla_ctx_f", **gla)
    _, save_cb, st_cb = _gla_fwd(qr, kr, z, la_b, zero_st, rev=True, row_off=n, nrows=tc, name="gla_ctx_b", **gla)
    o_f, save_f, _ = _gla_fwd(qr, kr, z, la_f, st_cf, rev=False, row_off=0, nrows=n, name="gla_f", **gla)
    o_b, save_b, _ = _gla_fwd(qr, kr, z, la_b, st_cb, rev=True, row_off=0, nrows=n, name="gla_b", **gla)
    y_gla = _readout_fwd(o_f, o_b, z, w["gla_g"], cfg, tr=tr)
    y_sg = _sg_fwd(z, n, w["sg_ln_g"], w["sg_ln_b"], w["w_s"], w["bs_full"], cfg)
    ycat = jnp.concatenate([y_gla, y_sg], axis=1)

    tm_n = _tile(n, 1024, 16)
    mix = _matmul(ycat, w["w_o"], "nn", n, d, d, tm=tm_n, tn=_tile(d, 1024, LANES), tk=_tile(d, 512, LANES),
                  name="mm_o", out_shapes=[jax.ShapeDtypeStruct((n, d), F32)])[0]
    x1, h2 = _mid_fwd(x, mix, g1, w["post1_g"], w["pre2_g"], sh2, sc2, tr=tr_s)

    tn_f = _tile(fs, 1024, LANES)
    tk_d = _tile(d, 512, LANES)

    def relu2(acc):
        return acc, jnp.square(jnp.maximum(acc, 0.0))

    a1, p1 = _matmul(h2, w["w_1"], "nn", n, ff, d, tm=tm_n, tn=tn_f, tk=tk_d, name="mm_1",
                     b_spec=_blocked_b_nn(fs, tk_d, tn_f), epilogue=relu2,
                     out_shapes=[jax.ShapeDtypeStruct((n, ff), BF16)] * 2)
    tk_f = _tile(ff, 512, LANES)
    m2 = _matmul(p1, w["w_2"], "nn", n, d, ff, tm=tm_n, tn=_tile(d, 1024, LANES), tk=tk_f, name="mm_2",
                 out_shapes=[jax.ShapeDtypeStruct((n, d), F32)])[0]

    dx2, dm2, dg2, dpost2, lossc = _head_bwd(x1, m2, target, g2, w["post2_g"], tr=tr_s)

    def drelu2(acc, a):
        return (acc * (2.0 * jnp.maximum(a.astype(F32), 0.0)),)

    da1 = _matmul(dm2, w["w_2"], "nt", n, ff, d, tm=tm_n, tn=_tile(ff, 1024, LANES), tk=tk_d, name="mm_2_dx",
                  epilogue=drelu2, extras=(a1,), out_shapes=[jax.ShapeDtypeStruct((n, ff), BF16)])[0]
    tk_n = _tile(n, 512, 16)
    dw_2 = _matmul(p1, dm2, "tn", ff, d, n, tm=_tile(ff, 1024, LANES), tn=_tile(d, 1024, LANES), tk=tk_n,
                   name="mm_2_dw", out_shapes=[jax.ShapeDtypeStruct((ff, d), F32)])[0]
    tk_fs = _tile(fs, 512, LANES)
    dh2 = _matmul(da1, w["w_1"], "nt", n, d, ff, tm=tm_n, tn=_tile(d, 1024, LANES), tk=tk_fs, name="mm_1_dx",
                  b_spec=_blocked_b_nt(fs, _tile(d, 1024, LANES), tk_fs),
                  out_shapes=[jax.ShapeDtypeStruct((n, d), F32)])[0]
    tm_d = _tile(d, 1024, LANES)
    dw_1 = _matmul(h2, da1, "tn", d, ff, n, tm=tm_d, tn=tn_f, tk=tk_n, name="mm_1_dw",
                   out_specs=[_blocked_out(fs, tm_d, tn_f)],
                   out_shapes=[jax.ShapeDtypeStruct((N_DEV, d, fs), F32)])[0]

    dx1, dmix, dsh2, dsc2, dpre2, dg1, dpost1 = _mid_bwd(dh2, x1, dx2, mix, sc2, w["pre2_g"], g1, w["post1_g"],
                                                         tr=tr_s)
    dycat = _matmul(dmix, w["w_o"], "nt", n, d, d, tm=tm_n, tn=_tile(d, 1024, LANES), tk=tk_d, name="mm_o_dx",
                    out_shapes=[jax.ShapeDtypeStruct((n, d), F32)])[0]
    dw_o = _matmul(ycat, dmix, "tn", d, d, n, tm=tm_d, tn=_tile(d, 1024, LANES), tk=tk_n, name="mm_o_dw",
                   out_shapes=[jax.ShapeDtypeStruct((d, d), F32)])[0]

    dzu, dzvv, dws, dbs_acc, dlng, dlnb = _sg_bwd(z, dycat, n, w["sg_ln_g"], w["sg_ln_b"], w["w_s"],
                                                  w["bs_full"], cfg)
    do, dzr, dgla_g = _readout_bwd(o_f, o_b, z, dycat, w["gla_g"], cfg, tr=tr)

    gf = _gla_bwd(qr, kr, z, la_f, do, save_f, zero_st, rev=False, row_off=0, nrows=n, name="gla_f_bwd", **gla)
    gb = _gla_bwd(qr, kr, z, la_b, do, save_b, zero_st, rev=True, row_off=0, nrows=n, name="gla_b_bwd", **gla)
    do_c = jnp.zeros((tc, vw), BF16)
    gcf = _gla_bwd(qr, kr, z, la_f, do_c, save_cf, gf[4], rev=False, row_off=n, nrows=tc, name="gla_ctx_f_bwd",
                   **gla)
    gcb = _gla_bwd(qr, kr, z, la_b, do_c, save_cb, gb[4], rev=True, row_off=n, nrows=tc, name="gla_ctx_b_bwd",
                   **gla)

    post = dict(la_f=la_f, la_b=la_b, z=z, tabs=tabs, wdf_pad=w["wdf_pad"], wdb_pad=w["wdb_pad"], cfg=cfg, tr=tr)
    dzq, dzk, dzv, dzl, dwdf, dwdb, dbdf, dbdb = _gla_post(gf, gb, row_off=0, nrows=n, name="gla_post", **post)
    czq, czk, czv, czl, cwdf, cwdb, cbdf, cbdb = _gla_post(gcf, gcb, row_off=n, nrows=tc, name="gla_post_ctx",
                                                           **post)
    zc = lambda wd: jnp.zeros((tc, wd), BF16)
    dz = jnp.concatenate([
        jnp.concatenate([dzq, dzk, dzv, dzr, dzu, dzvv, dzl], axis=1),
        jnp.concatenate([czq, czk, czv, zc(vw), zc(sgw), zc(sgw), czl], axis=1)], axis=0)

    da_all = _matmul(dz, w["w_pad"], "nt", rt, d, zw, tm=tm_a, tn=_tile(d, 1024, LANES), tk=_tile(zw, 1152, LANES),
                     name="mm_in_dx", out_shapes=[jax.ShapeDtypeStruct((rt, d), F32)])[0]
    dw_pad = _matmul(a_all, dz, "tn", d, zw, rt, tm=tm_d, tn=_tile(zw, 1152, LANES), tk=_tile(rt, 1152, 16),
                     name="mm_in_dw", out_shapes=[jax.ShapeDtypeStruct((d, zw), F32)])[0]

    grad_x, dsh1, dsc1, dpre1 = _in_bwd(da_all, x, dx1, sc1, w["pre1_g"], row_off=0, tr=tr_s, name="in_bwd_x")
    dcsh1, dcsc1, dpre1_c = _in_bwd(da_all, ctx, None, csc1, w["pre1_g"], row_off=n, tr=tr_s, name="in_bwd_ctx")

    lr = cfg["LR"]
    small = dict(
        pre1_g=dpre1 + dpre1_c, post1_g=dpost1, pre2_g=dpre2, post2_g=dpost2,
        w_dec_f=(dwdf + cwdf)[:lr], w_dec_b=(dwdb + cwdb)[lr:2 * lr], b_dec_f=dbdf + cbdf, b_dec_b=dbdb + cbdb,
        gla_norm_g=dgla_g, sg_ln_g=dlng, sg_ln_b=dlnb, w_s=dws,
        b_s=dbs_acc.reshape(cfg["SG_C"], cfg["SG_G"], sgw // cfg["SG_G"]).sum(-1).T)
    dmod = jnp.concatenate([dsh1, dsc1, dg1, dsh2, dsc2, dg2], axis=1)
    dmod_c = jnp.concatenate([dcsh1, dcsc1], axis=1)
    big = dict(w_pad=dw_pad, w_o=dw_o, w_1=dw_1, w_2=dw_2)
    return lossc, grad_x, big, small, dmod, dmod_c


SMALL_NAMES = ["b_ada", "pre1_g", "post1_g", "pre2_g", "post2_g", "w_dec_f", "b_dec_f", "w_dec_b", "b_dec_b",
               "gla_norm_g", "sg_ln_g", "sg_ln_b", "w_s", "b_s", "c_ctx"]
WEIGHT_ORDER = ["c_ctx", "w_ada", "b_ada", "pre1_g", "post1_g", "pre2_g", "post2_g", "w_in", "w_dec_f", "b_dec_f",
                "w_dec_b", "b_dec_b", "gla_norm_g", "sg_ln_g", "sg_ln_b", "w_s", "b_s", "w_o", "w_1", "w_2"]


def kernel(x, c, ctx, c_ctx, w_ada, b_ada, pre1_g, post1_g, pre2_g, post2_g, w_in, w_dec_f, b_dec_f, w_dec_b, b_dec_b, gla_norm_g, sg_ln_g, sg_ln_b, w_s, b_s, w_o, w_1, w_2, loss_target, m_c_ctx, m_w_ada, m_b_ada, m_pre1_g, m_post1_g, m_pre2_g, m_post2_g, m_w_in, m_w_dec_f, m_b_dec_f, m_w_dec_b, m_b_dec_b, m_gla_norm_g, m_sg_ln_g, m_sg_ln_b, m_w_s, m_b_s, m_w_o, m_w_1, m_w_2, v_c_ctx, v_w_ada, v_b_ada, v_pre1_g, v_post1_g, v_pre2_g, v_post2_g, v_w_in, v_w_dec_f, v_b_dec_f, v_w_dec_b, v_b_dec_b, v_gla_norm_g, v_sg_ln_g, v_sg_ln_b, v_w_s, v_b_s, v_w_o, v_w_1, v_w_2):
    weights = dict(c_ctx=c_ctx, w_ada=w_ada, b_ada=b_ada, pre1_g=pre1_g, post1_g=post1_g, pre2_g=pre2_g,
                   post2_g=post2_g, w_in=w_in, w_dec_f=w_dec_f, b_dec_f=b_dec_f, w_dec_b=w_dec_b, b_dec_b=b_dec_b,
                   gla_norm_g=gla_norm_g, sg_ln_g=sg_ln_g, sg_ln_b=sg_ln_b, w_s=w_s, b_s=b_s, w_o=w_o, w_1=w_1,
                   w_2=w_2)
    mom_m = dict(c_ctx=m_c_ctx, w_ada=m_w_ada, b_ada=m_b_ada, pre1_g=m_pre1_g, post1_g=m_post1_g, pre2_g=m_pre2_g,
                 post2_g=m_post2_g, w_in=m_w_in, w_dec_f=m_w_dec_f, b_dec_f=m_b_dec_f, w_dec_b=m_w_dec_b,
                 b_dec_b=m_b_dec_b, gla_norm_g=m_gla_norm_g, sg_ln_g=m_sg_ln_g, sg_ln_b=m_sg_ln_b, w_s=m_w_s,
                 b_s=m_b_s, w_o=m_w_o, w_1=m_w_1, w_2=m_w_2)
    mom_v = dict(c_ctx=v_c_ctx, w_ada=v_w_ada, b_ada=v_b_ada, pre1_g=v_pre1_g, post1_g=v_post1_g, pre2_g=v_pre2_g,
                 post2_g=v_post2_g, w_in=v_w_in, w_dec_f=v_w_dec_f, b_dec_f=v_b_dec_f, w_dec_b=v_w_dec_b,
                 b_dec_b=v_b_dec_b, gla_norm_g=v_gla_norm_g, sg_ln_g=v_sg_ln_g, sg_ln_b=v_sg_ln_b, w_s=v_w_s,
                 b_s=v_b_s, w_o=v_w_o, w_1=v_w_1, w_2=v_w_2)

    cfg = _config(x, ctx, w_in, w_dec_f, gla_norm_g, sg_ln_g, w_s)
    n, d, h, dv, kw, vw, lr, sgw = (cfg[k] for k in ("N", "D", "H", "DV", "KW", "VW", "LR", "SGW"))
    dvs, kws = dv // N_DEV, kw // N_DEV
    ix, iy, ic = _my_pos()
    me = 4 * ix + 2 * iy + ic
    pos = jnp.stack([2 * ix + iy, ic]).astype(jnp.int32)

    pack1 = jnp.concatenate([c.reshape(1, d), w_dec_f.reshape(1, lr * kws), w_dec_b.reshape(1, lr * kws),
                             gla_norm_g.reshape(1, h * dvs)], axis=1)
    g1 = _all_gather_vec(pack1, name="ag_small_in")
    c_all = g1[:, :d]
    o1 = d
    wdf = g1[:, o1:o1 + lr * kws].reshape(N_DEV, lr, kws).transpose(1, 0, 2).reshape(lr, kw)
    o1 += lr * kws
    wdb = g1[:, o1:o1 + lr * kws].reshape(N_DEV, lr, kws).transpose(1, 0, 2).reshape(lr, kw)
    o1 += lr * kws
    gla_g = g1[:, o1:o1 + h * dvs].reshape(N_DEV, h, dvs).transpose(1, 0, 2).reshape(1, h * dv)

    c16 = jnp.concatenate([c_all, jnp.broadcast_to(c_ctx.reshape(1, d), (N_DEV, d))], axis=0)
    ncol = w_ada.shape[2]
    wa = w_ada.reshape(d, ncol)
    b_mine = lax.dynamic_slice(b_ada, (0, me * ncol), (1, ncol))
    tn_ada = _tile(ncol, 512, LANES)
    mod_mine = _ada_fwd(c16, wa, b_mine, tn=tn_ada)
    mod_all = _all_gather_small(mod_mine, name="ag_mod").transpose(1, 0, 2).reshape(16, N_DEV * ncol)
    mod_b = lax.dynamic_slice(mod_all, (me, 0), (1, 6 * d))
    mods = [mod_b[:, i * d:(i + 1) * d] for i in range(6)]
    c_mods = [mod_all[N_DEV:N_DEV + 1, :d], mod_all[N_DEV:N_DEV + 1, d:2 * d]]

    cs_in = w_in.shape[2]
    wg_in, wg_o, wg_1, wg_2 = _all_gather_big(
        [w_in.reshape(d, cs_in).astype(BF16), w_o.reshape(w_o.shape[1], d).astype(BF16),
         w_1.reshape(d, w_1.shape[2]).astype(BF16), w_2.reshape(w_2.shape[1], d).astype(BF16)], name="ag_weights")
    w_full = wg_in.transpose(1, 0, 2).reshape(d, cfg["IN"])
    lf0 = 2 * kw + 2 * vw
    sg0 = lf0 + 2 * lr
    w_pad = jnp.concatenate([w_full[:, :lf0], w_full[:, sg0:], w_full[:, lf0:sg0],
                             jnp.zeros((d, LANES - 2 * lr), BF16)], axis=1)
    zpad = lambda r: jnp.zeros((r, kw), F32)
    w = dict(
        w_pad=w_pad, w_o=wg_o.reshape(d, d), w_1=wg_1, w_2=wg_2.reshape(-1, d),
        pre1_g=pre1_g, post1_g=post1_g, pre2_g=pre2_g, post2_g=post2_g, b_dec_f=b_dec_f, b_dec_b=b_dec_b,
        wdf_pad=jnp.concatenate([wdf, zpad(LANES - lr)], axis=0),
        wdb_pad=jnp.concatenate([zpad(lr), wdb, zpad(LANES - 2 * lr)], axis=0),
        gla_g=gla_g, sg_ln_g=sg_ln_g, sg_ln_b=sg_ln_b, w_s=w_s[0],
        bs_full=jnp.repeat(b_s[0].T, sgw // cfg["SG_G"], axis=1))

    lossc, grad_x, big, small, dmod, dmod_c = _local_step(x[0], ctx[0], loss_target[0], mods, c_mods, w, cfg)
    loss = lax.psum(jnp.sum(lossc), AXES)

    order3 = ["pre1_g", "post1_g", "pre2_g", "post2_g", "w_dec_f", "b_dec_f", "w_dec_b", "b_dec_b", "gla_norm_g",
              "sg_ln_g", "sg_ln_b", "w_s", "b_s"]
    pieces = [dmod, dmod_c] + [small[k].reshape(1, -1) for k in order3]
    sizes = [p.shape[1] for p in pieces]
    g3 = _all_gather_vec(jnp.concatenate(pieces, axis=1), name="ag_small_grads")
    offs = [0]
    for s in sizes:
        offs.append(offs[-1] + s)
    dmod_all = g3[:, :6 * d]
    dmod_c_all = jnp.pad(g3[:, offs[1]:offs[2]], ((0, 0), (0, 4 * d)))
    parts8 = {k: g3[:, offs[2 + i]:offs[3 + i]] for i, k in enumerate(order3)}
    parts8["b_ada"] = dmod_all + dmod_c_all
    parts8["w_dec_f"] = lax.dynamic_slice(parts8["w_dec_f"].reshape(N_DEV, lr, kw), (0, 0, me * kws),
                                          (N_DEV, lr, kws)).reshape(N_DEV, -1)
    parts8["w_dec_b"] = lax.dynamic_slice(parts8["w_dec_b"].reshape(N_DEV, lr, kw), (0, 0, me * kws),
                                          (N_DEV, lr, kws)).reshape(N_DEV, -1)
    parts8["gla_norm_g"] = lax.dynamic_slice(parts8["gla_norm_g"].reshape(N_DEV, h, dv), (0, 0, me * dvs),
                                             (N_DEV, h, dvs)).reshape(N_DEV, -1)

    dm16 = jnp.concatenate([dmod_all, dmod_c_all], axis=0)
    dm_mine = lax.dynamic_slice(dm16, (0, me * ncol), (16, ncol))
    g_w_ada, dcc = _ada_bwd(c16, dm_mine, wa, c_ctx.reshape(1, d), tn=tn_ada)
    parts8["c_ctx"] = _all_gather_vec(dcc, name="ag_cctx")

    flat = lambda t: t.reshape(1, -1)
    g8 = _dense(jnp.concatenate([parts8[k] for k in SMALL_NAMES], axis=1))
    ws, ms, vs = [_dense(jnp.concatenate([flat(src[k]) for k in SMALL_NAMES], axis=1))[0]
                  for src in (weights, mom_m, mom_v)]
    res_small = [r.reshape(1, -1) for r in _adam_small(g8, ws, ms, vs)]
    out = {}
    off = 0
    for k in SMALL_NAMES:
        sz = weights[k].size
        out[k] = [r[:, off:off + sz].reshape(weights[k].shape) for r in res_small]
        off += sz

    dw_in = jnp.concatenate([big["w_pad"][:, :cfg["U0"]], big["w_pad"][:, cfg["L0"]:cfg["L0"] + 2 * lr],
                             big["w_pad"][:, cfg["U0"]:cfg["L0"]]], axis=1)
    grads = [dw_in.reshape(d, N_DEV, cs_in).transpose(1, 0, 2), big["w_o"].reshape(N_DEV, -1, d), big["w_1"],
             big["w_2"].reshape(N_DEV, -1, d)]
    names = ["w_in", "w_o", "w_1", "w_2"]
    rows_for = lambda r, cols: _tile(r, max(8, ELEMS_PER_BLOCK // cols), 8)
    from_sibling = _pair_exchange(grads, name="rs_pair")
    sums = [_pair_add(g, t, pos, name="rs_pair_add_" + nm, tr=rows_for(g.shape[1], g.shape[2]))
            for g, t, nm in zip(grads, from_sibling, names)]
    from_chips = _chip_exchange(sums, name="rs_chips")
    for nm, s, u in zip(names, sums, from_chips):
        shp = weights[nm].shape
        r2 = (shp[1], shp[2])
        res = _adam_big([(s, 0), (u, 0), (u, 1), (u, 2)], weights[nm].reshape(r2), mom_m[nm].reshape(r2),
                        mom_v[nm].reshape(r2), name="adam_" + nm, tr=rows_for(*r2))
        out[nm] = [r.reshape(shp) for r in res]
    r2 = (d, ncol)
    res = _adam_big([(g_w_ada, None)], wa, m_w_ada.reshape(r2), v_w_ada.reshape(r2), name="adam_w_ada",
                    tr=rows_for(*r2))
    out["w_ada"] = [r.reshape(w_ada.shape) for r in res]

    outs = [loss, grad_x[None]]
    for i in range(4):
        outs += [out[k][i] for k in WEIGHT_ORDER]
    return tuple(outs)
```

```python
import math

import jax
import jax.numpy as jnp
from jax import lax
from jax.experimental import pallas as pl
from jax.experimental.pallas import tpu as pltpu

F32 = jnp.float32
BF16 = jnp.bfloat16
MXU_DTYPE = jnp.bfloat16
HI = lax.Precision.HIGHEST

N_DEV = 8
AXES = ("x", "y", "c")
MESH = pl.DeviceIdType.MESH
LANES = 128
VMEM_LIMIT = 56 * 1024 * 1024

EPS = 1e-6
GRID_W = 64
GLA_CHUNK = 64
GLA_TAU = 16.0
ROPE_BASE = 10000.0
ADAM_LR = 0.001
ADAM_B1 = 0.9
ADAM_B2 = 0.999
ADAM_EPS = 1e-08
ADAM_WD = 0.01
ADAM_STEP = 10


def _cparams(sem):
    return pltpu.CompilerParams(dimension_semantics=sem, vmem_limit_bytes=VMEM_LIMIT)


def _tile(n, target, align):
    if n <= target:
        return n
    best = None
    for t in range(align, target + 1, align):
        if n % t == 0:
            best = t
    assert best is not None, (n, target, align)
    return best


def _dg(a, b, dims, prec=None):
    return lax.dot_general(a, b, (dims, ((), ())), precision=prec, preferred_element_type=F32)


def _nn(a, b):
    return _dg(a.astype(MXU_DTYPE), b.astype(MXU_DTYPE), ((1,), (0,)))


def _nt(a, b):
    return _dg(a.astype(MXU_DTYPE), b.astype(MXU_DTYPE), ((1,), (1,)))


def _tn(a, b):
    return _dg(a.astype(MXU_DTYPE), b.astype(MXU_DTYPE), ((0,), (0,)))


def _sigmoid(x):
    return 1.0 / (1.0 + jnp.exp(-x))


def _silu(x):
    return x * _sigmoid(x)


def _dsilu(x):
    s = _sigmoid(x)
    return s * (1.0 + x * (1.0 - s))


def _gelu(x):
    return 0.5 * x * (1.0 + lax.erf(x * (1.0 / math.sqrt(2.0))))


def _dgelu(x):
    return 0.5 * (1.0 + lax.erf(x * (1.0 / math.sqrt(2.0)))) + x * jnp.exp(-0.5 * x * x) * (1.0 / math.sqrt(2.0 * math.pi))


def _rstd(x):
    return lax.rsqrt(jnp.mean(x * x, axis=-1, keepdims=True) + EPS)


def _rms_bwd(x, r, dn):
    return r * dn - x * (r * r * r) * jnp.mean(dn * x, axis=-1, keepdims=True)


def _colsum(x):
    return jnp.sum(x, axis=0, keepdims=True)


class _Sides:
    def __init__(self, jobs):
        self.jobs = list(jobs)
        self.ins = [a for j in self.jobs for a in j["ins"]]
        self.outs = [o for j in self.jobs for o in j["outs"]]
        self.sems = [s for j in self.jobs for s in j["sems"]]

    def aliases(self, in_base, out_base):
        res, oi, oo = {}, 0, 0
        for j in self.jobs:
            for a, b in j.get("alias", {}).items():
                res[in_base + oi + a] = out_base + oo + b
            oi += len(j["ins"])
            oo += len(j["outs"])
        return res

    def run(self, phase, in_refs, out_refs, sem_refs):
        oi = oo = os_ = 0
        for j in self.jobs:
            ni, no, ns = len(j["ins"]), len(j["outs"]), len(j["sems"])
            j[phase](in_refs[oi:oi + ni], out_refs[oo:oo + no], sem_refs[os_:os_ + ns])
            oi, oo, os_ = oi + ni, oo + no, os_ + ns


_NO_SIDES = _Sides([])
ANY_SPEC = pl.BlockSpec(memory_space=pl.ANY)


def _matmul(a, b, mode, m, n, k, *, tm, tn, tk, name, out_shapes, b_spec=None, out_specs=None,
            epilogue=None, extras=(), sides=_NO_SIDES):
    nk = k // tk
    assert m % tm == 0 and n % tn == 0 and k % tk == 0, (name, m, n, k, tm, tn, tk)
    dot = {"nn": _nn, "nt": _nt, "tn": _tn}[mode]
    if mode == "tn":
        a_spec = pl.BlockSpec((tk, tm), lambda i, j, kk: (kk, i))
    else:
        a_spec = pl.BlockSpec((tm, tk), lambda i, j, kk: (i, kk))
    if b_spec is None:
        if mode == "nt":
            b_spec = pl.BlockSpec((tn, tk), lambda i, j, kk: (j, kk))
        else:
            b_spec = pl.BlockSpec((tk, tn), lambda i, j, kk: (kk, j))
    mn_spec = pl.BlockSpec((tm, tn), lambda i, j, kk: (i, j))
    if out_specs is None:
        out_specs = [mn_spec] * len(out_shapes)
    n_extra = len(extras)
    n_out = len(out_shapes)
    n_si, n_so = len(sides.ins), len(sides.outs)
    ni, nj = m // tm, n // tn

    def body(a_ref, b_ref, *rest):
        extra_refs = rest[:n_extra]
        rest = rest[n_extra:]
        side_in, rest = rest[:n_si], rest[n_si:]
        out_refs, rest = rest[:n_out], rest[n_out:]
        side_out, rest = rest[:n_so], rest[n_so:]
        acc, side_sems = rest[0], rest[1:]
        i, j, kk = pl.program_id(0), pl.program_id(1), pl.program_id(2)

        if sides.jobs:
            @pl.when((i == 0) & (j == 0) & (kk == 0))
            def _():
                sides.run("start", side_in, side_out, side_sems)

        @pl.when(kk == 0)
        def _():
            acc[...] = jnp.zeros_like(acc)

        acc[...] += dot(a_ref[...], b_ref[...])

        @pl.when(kk == nk - 1)
        def _():
            vals = (acc[...],) if epilogue is None else epilogue(acc[...], *[e[...] for e in extra_refs])
            for o, v in zip(out_refs, vals):
                o[...] = v.astype(o.dtype)

        if sides.jobs:
            @pl.when((i == ni - 1) & (j == nj - 1) & (kk == nk - 1))
            def _():
                sides.run("finish", side_in, side_out, side_sems)

    sem = ("arbitrary",) * 3 if sides.jobs else ("parallel", "parallel", "arbitrary")
    res = pl.pallas_call(
        body, name=name, grid=(ni, nj, nk),
        in_specs=[a_spec, b_spec] + [mn_spec] * n_extra + [ANY_SPEC] * n_si,
        out_specs=list(out_specs) + [ANY_SPEC] * n_so, out_shape=list(out_shapes) + list(sides.outs),
        scratch_shapes=[pltpu.VMEM((tm, tn), F32)] + list(sides.sems),
        input_output_aliases=sides.aliases(2 + n_extra, n_out),
        compiler_params=_cparams(sem),
    )(a, b, *extras, *sides.ins)
    return res


def _blocked_b_nn(ns, tk, tn):
    assert ns % tn == 0
    return pl.BlockSpec((None, tk, tn), lambda i, j, kk: ((j * tn) // ns, kk, ((j * tn) % ns) // tn))


def _blocked_b_nt(ks, tn, tk):
    assert ks % tk == 0
    return pl.BlockSpec((None, tn, tk), lambda i, j, kk: ((kk * tk) // ks, j, ((kk * tk) % ks) // tk))


def _blocked_out(ns, tm, tn):
    assert ns % tn == 0
    return pl.BlockSpec((None, tm, tn), lambda i, j, kk: ((j * tn) // ns, i, ((j * tn) % ns) // tn))


def _rows_call(body, *, name, nblk, tr, row_ins, consts, row_outs, accs=(), sides=_NO_SIDES):
    n_ri, n_c, n_ro, n_acc = len(row_ins), len(consts), len(row_outs), len(accs)
    n_si, n_so = len(sides.ins), len(sides.outs)

    def kern(*refs):
        i = pl.program_id(0)
        rin, refs = refs[:n_ri], refs[n_ri:]
        cin, refs = refs[:n_c], refs[n_c:]
        side_in, refs = refs[:n_si], refs[n_si:]
        rout, refs = refs[:n_ro], refs[n_ro:]
        acc, refs = refs[:n_acc], refs[n_acc:]
        side_out, side_sems = refs[:n_so], refs[n_so:]

        if sides.jobs:
            @pl.when(i == 0)
            def _():
                sides.run("start", side_in, side_out, side_sems)

        if n_acc:
            @pl.when(i == 0)
            def _():
                for r in acc:
                    r[...] = jnp.zeros_like(r)

        body(rin, cin, rout, acc)

        if sides.jobs:
            @pl.when(i == nblk - 1)
            def _():
                sides.run("finish", side_in, side_out, side_sems)

    in_specs = [pl.BlockSpec((tr, w), lambda i, ro=ro, co=co: (i + ro, co)) for (_, w, ro, co) in row_ins]
    in_specs += [pl.BlockSpec(cst.shape, lambda i, nd=cst.ndim: (0,) * nd) for cst in consts]
    out_specs = [pl.BlockSpec((tr, w), lambda i: (i, 0)) for (_, w, _) in row_outs]
    out_specs += [pl.BlockSpec(s, lambda i, nd=len(s): (0,) * nd) for s in accs]
    out_shape = [jax.ShapeDtypeStruct((r, w), dt) for (r, w, dt) in row_outs]
    out_shape += [jax.ShapeDtypeStruct(s, F32) for s in accs]
    return pl.pallas_call(
        kern, name=name, grid=(nblk,), in_specs=in_specs + [ANY_SPEC] * n_si,
        out_specs=out_specs + [ANY_SPEC] * n_so, out_shape=out_shape + list(sides.outs),
        scratch_shapes=list(sides.sems),
        input_output_aliases=sides.aliases(n_ri + n_c, n_ro + n_acc),
        compiler_params=_cparams(("arbitrary",)),
    )(*[r[0] for r in row_ins], *consts, *sides.ins)


def _norm_mod(x, g, shift, scale, *, name, tr):
    rows, d = x.shape

    def body(rin, cin, rout, acc):
        xv = rin[0][...]
        n = xv * _rstd(xv) * cin[0][...]
        rout[0][...] = (n * (1.0 + cin[2][...]) + cin[1][...]).astype(BF16)

    return _rows_call(body, name=name, nblk=rows // tr, tr=tr, row_ins=[(x, d, 0, 0)],
                      consts=[g, shift, scale], row_outs=[(rows, d, BF16)])[0]


def _swap_halves(t, width):
    lane = lax.broadcasted_iota(jnp.int32, t.shape, 1)
    return jnp.where(lane % 64 < 32, pltpu.roll(t, width - 32, 1), pltpu.roll(t, 32, 1))


def _gla_prep(z, tabs, wdf_pad, wdb_pad, bdf, bdb, cfg, *, tr):
    rows = z.shape[0]
    kw, h = cfg["KW"], cfg["H"]

    def body(rin, cin, rout, acc):
        zq, zk, zl = rin[0][...], rin[1][...], rin[2][...]
        cq, sq, ck, sk = [jnp.concatenate([rin[3 + t][...]] * h, axis=1) for t in range(4)]
        rout[0][...] = zq * cq + _swap_halves(zq, kw) * sq
        rout[1][...] = zk * ck + _swap_halves(zk, kw) * sk
        for o, w, b in ((2, cin[0], cin[2]), (3, cin[1], cin[3])):
            a = _nn(zl, w[...]) + b[...]
            rout[o][...] = (jnp.minimum(a, 0.0) - jnp.log(1.0 + jnp.exp(-jnp.abs(a)))) * (1.0 / GLA_TAU)

    row_ins = [(z, kw, 0, 0), (z, kw, 0, 1), (z, LANES, 0, cfg["L0"] // LANES)]
    row_ins += [(t, LANES, 0, 0) for t in tabs]
    return _rows_call(body, name="gla_prep", nblk=rows // tr, tr=tr, row_ins=row_ins,
                      consts=[wdf_pad, wdb_pad, bdf, bdb], row_outs=[(rows, kw, F32)] * 4)


def _chunk_consts(rev):
    c = GLA_CHUNK
    r = lax.broadcasted_iota(jnp.int32, (c, c), 0)
    cc = lax.broadcasted_iota(jnp.int32, (c, c), 1)
    keep = (cc >= r) if rev else (cc <= r)
    return keep, keep.astype(F32)


def _chunk_decay(la, keep_f):
    b = _dg(keep_f, la, ((1,), (0,)), HI)
    return b, _colsum(la)


def _gla_fwd(qr, kr, z, la, st0, cfg, *, rev, row_off, nrows, tb, name):
    h, dk, dv = cfg["H"], cfg["DK"], cfg["DV"]
    c = GLA_CHUNK
    nsub = tb // c
    nblk = nrows // tb
    roff = row_off // tb
    v_cb = cfg["V0"] // dv

    def blk(j):
        return (nblk - 1 - j) if rev else j

    def body(q_ref, k_ref, v_ref, la_ref, st0_ref, o_ref, save_ref, fin_ref, st):
        j = pl.program_id(1)

        @pl.when(j == 0)
        def _():
            st[...] = st0_ref[...]

        keep, keep_f = _chunk_consts(rev)
        order = range(nsub - 1, -1, -1) if rev else range(nsub)
        for s in order:
            rs = pl.ds(s * c, c)
            q, k, v, lac = q_ref[rs, :], k_ref[rs, :], v_ref[rs, :], la_ref[rs, :]
            b, btot = _chunk_decay(lac, keep_f)
            qe = q * jnp.exp(b)
            ke = k * jnp.exp(-b)
            kl = k * jnp.exp(btot - b)
            s_in = st[...]
            save_ref[s] = s_in
            att = jnp.where(keep, _nt(qe, ke), 0.0)
            o_ref[rs, :] = _nt(qe, s_in) + _nn(att, v)
            st[...] = s_in * jnp.exp(btot) + _tn(v, kl)

        @pl.when(j == nblk - 1)
        def _():
            fin_ref[...] = st[...]

    in_specs = [
        pl.BlockSpec((tb, dk), lambda hh, j: (roff + blk(j), hh)),
        pl.BlockSpec((tb, dk), lambda hh, j: (roff + blk(j), hh)),
        pl.BlockSpec((tb, dv), lambda hh, j: (roff + blk(j), v_cb + hh)),
        pl.BlockSpec((tb, dk), lambda hh, j: (roff + blk(j), hh)),
        pl.BlockSpec((None, dv, dk), lambda hh, j: (hh, 0, 0)),
    ]
    out_specs = [
        pl.BlockSpec((tb, dv), lambda hh, j: (blk(j), hh)),
        pl.BlockSpec((None, nsub, dv, dk), lambda hh, j: (hh, blk(j), 0, 0)),
        pl.BlockSpec((None, dv, dk), lambda hh, j: (hh, 0, 0)),
    ]
    out_shape = [
        jax.ShapeDtypeStruct((nrows, h * dv), F32),
        jax.ShapeDtypeStruct((h, nrows // c, dv, dk), F32),
        jax.ShapeDtypeStruct((h, dv, dk), F32),
    ]
    return pl.pallas_call(
        body, name=name, grid=(h, nblk), in_specs=in_specs, out_specs=out_specs, out_shape=out_shape,
        scratch_shapes=[pltpu.VMEM((dv, dk), F32)],
        compiler_params=_cparams(("arbitrary", "arbitrary")),
    )(qr, kr, z, la, st0)


def _gla_bwd(qr, kr, z, la, do, save, dst_init, cfg, *, rev, row_off, nrows, tb, name):
    h, dk, dv = cfg["H"], cfg["DK"], cfg["DV"]
    c = GLA_CHUNK
    nsub = tb // c
    nblk = nrows // tb
    roff = row_off // tb
    v_cb = cfg["V0"] // dv

    def blk(j):
        return j if rev else (nblk - 1 - j)

    def body(q_ref, k_ref, v_ref, la_ref, do_ref, save_ref, di_ref, dq_ref, dk_ref, dv_ref, dla_ref, d0_ref, dst):
        j = pl.program_id(1)

        @pl.when(j == 0)
        def _():
            dst[...] = di_ref[...]

        keep, keep_f = _chunk_consts(rev)
        keep_t = _chunk_consts(not rev)[1]
        order = range(nsub) if rev else range(nsub - 1, -1, -1)
        for s in order:
            rs = pl.ds(s * c, c)
            q, k, v, lac, dout = q_ref[rs, :], k_ref[rs, :], v_ref[rs, :], la_ref[rs, :], do_ref[rs, :]
            b, btot = _chunk_decay(lac, keep_f)
            eb = jnp.exp(b)
            enb = jnp.exp(-b)
            elb = jnp.exp(btot - b)
            etot = jnp.exp(btot)
            qe, ke, kl = q * eb, k * enb, k * elb
            s_in = save_ref[s]
            d_after = dst[...]
            att = jnp.where(keep, _nt(qe, ke), 0.0)
            datt = jnp.where(keep, _nt(dout, v), 0.0)
            dqe = _nn(dout, s_in) + _nn(datt, ke)
            dke = _tn(datt, qe)
            dkl = _nn(v, d_after)
            dv_ref[rs, :] = _tn(att, dout) + _nt(kl, d_after)
            db = dqe * qe - dke * ke - dkl * kl
            dbtot = _colsum(dkl * kl) + _colsum(d_after * s_in) * etot
            dla_ref[rs, :] = _dg(keep_t, db, ((1,), (0,)), HI) + dbtot
            dq_ref[rs, :] = dqe * eb
            dk_ref[rs, :] = dke * enb + dkl * elb
            dst[...] = d_after * etot + _tn(dout, qe)

        @pl.when(j == nblk - 1)
        def _():
            d0_ref[...] = dst[...]

    in_specs = [
        pl.BlockSpec((tb, dk), lambda hh, j: (roff + blk(j), hh)),
        pl.BlockSpec((tb, dk), lambda hh, j: (roff + blk(j), hh)),
        pl.BlockSpec((tb, dv), lambda hh, j: (roff + blk(j), v_cb + hh)),
        pl.BlockSpec((tb, dk), lambda hh, j: (roff + blk(j), hh)),
        pl.BlockSpec((tb, dv), lambda hh, j: (blk(j), hh)),
        pl.BlockSpec((None, nsub, dv, dk), lambda hh, j: (hh, blk(j), 0, 0)),
        pl.BlockSpec((None, dv, dk), lambda hh, j: (hh, 0, 0)),
    ]
    out_specs = [
        pl.BlockSpec((tb, dk), lambda hh, j: (blk(j), hh)),
        pl.BlockSpec((tb, dk), lambda hh, j: (blk(j), hh)),
        pl.BlockSpec((tb, dv), lambda hh, j: (blk(j), hh)),
        pl.BlockSpec((tb, dk), lambda hh, j: (blk(j), hh)),
        pl.BlockSpec((None, dv, dk), lambda hh, j: (hh, 0, 0)),
    ]
    out_shape = [
        jax.ShapeDtypeStruct((nrows, h * dk), F32),
        jax.ShapeDtypeStruct((nrows, h * dk), F32),
        jax.ShapeDtypeStruct((nrows, h * dv), F32),
        jax.ShapeDtypeStruct((nrows, h * dk), F32),
        jax.ShapeDtypeStruct((h, dv, dk), F32),
    ]
    return pl.pallas_call(
        body, name=name, grid=(h, nblk), in_specs=in_specs, out_specs=out_specs, out_shape=out_shape,
        scratch_shapes=[pltpu.VMEM((dv, dk), F32)],
        compiler_params=_cparams(("arbitrary", "arbitrary")),
    )(qr, kr, z, la, do, save, dst_init)


def _gla_post(gf, gb, la_f, la_b, z, tabs, wdf_pad, wdb_pad, cfg, *, row_off, nrows, tr, name):
    kw, vw = cfg["KW"], cfg["VW"]
    h = cfg["H"]
    ro = row_off // tr

    def body(rin, cin, rout, acc):
        dq = rin[0][...] + rin[1][...]
        dk_ = rin[2][...] + rin[3][...]
        zl = rin[10][...]
        cq, sq, ck, sk = [jnp.concatenate([rin[11 + t][...]] * h, axis=1) for t in range(4)]
        rout[0][...] = (dq * cq + _swap_halves(dq * sq, kw)).astype(BF16)
        rout[1][...] = (dk_ * ck + _swap_halves(dk_ * sk, kw)).astype(BF16)
        rout[2][...] = (rin[8][...] + rin[9][...]).astype(BF16)
        dzl = jnp.zeros(zl.shape, F32)
        for t, w in ((0, cin[0]), (1, cin[1])):
            la = rin[6 + t][...]
            da = rin[4 + t][...] * ((1.0 - jnp.exp(la * GLA_TAU)) * (1.0 / GLA_TAU))
            dzl = dzl + _nt(da, w[...])
            acc[t][...] += _tn(zl, da)
            acc[2 + t][...] += _colsum(da)
        rout[3][...] = dzl.astype(BF16)

    row_ins = [(gf[0], kw, 0, 0), (gb[0], kw, 0, 0), (gf[1], kw, 0, 0), (gb[1], kw, 0, 0),
               (gf[3], kw, 0, 0), (gb[3], kw, 0, 0), (la_f, kw, ro, 0), (la_b, kw, ro, 0),
               (gf[2], vw, 0, 0), (gb[2], vw, 0, 0), (z, LANES, ro, cfg["L0"] // LANES)]
    row_ins += [(t, LANES, ro, 0) for t in tabs]
    return _rows_call(body, name=name, nblk=nrows // tr, tr=tr, row_ins=row_ins, consts=[wdf_pad, wdb_pad],
                      row_outs=[(nrows, kw, BF16), (nrows, kw, BF16), (nrows, vw, BF16), (nrows, LANES, BF16)],
                      accs=[(LANES, kw), (LANES, kw), (1, kw), (1, kw)])


def _readout_fwd(o_f, o_b, z, g, cfg, *, tr):
    n, vw = o_f.shape
    h, dv = cfg["H"], cfg["DV"]

    def body(rin, cin, rout, acc):
        for hh in range(h):
            cs = slice(hh * dv, (hh + 1) * dv)
            oh = rin[0][:, cs] + rin[1][:, cs]
            y = oh * _rstd(oh) * cin[0][:, cs]
            rout[0][:, cs] = (y * _silu(rin[2][:, cs])).astype(BF16)

    return _rows_call(body, name="gla_readout", nblk=n // tr, tr=tr,
                      row_ins=[(o_f, vw, 0, 0), (o_b, vw, 0, 0), (z, vw, 0, cfg["R0"] // vw)], consts=[g],
                      row_outs=[(n, vw, BF16)])[0]


def _readout_bwd(o_f, o_b, z, dycat, g, cfg, *, tr):
    n, vw = o_f.shape
    h, dv = cfg["H"], cfg["DV"]

    def body(rin, cin, rout, acc):
        for hh in range(h):
            cs = slice(hh * dv, (hh + 1) * dv)
            oh = rin[0][:, cs] + rin[1][:, cs]
            r, dyg, gh = rin[2][:, cs], rin[3][:, cs], cin[0][:, cs]
            rs = _rstd(oh)
            dy = dyg * _silu(r)
            rout[0][:, cs] = _rms_bwd(oh, rs, dy * gh).astype(BF16)
            rout[1][:, cs] = (dyg * (oh * rs * gh) * _dsilu(r)).astype(BF16)
            acc[0][:, cs] += _colsum(dy * oh * rs)

    return _rows_call(body, name="gla_readout_bwd", nblk=n // tr, tr=tr,
                      row_ins=[(o_f, vw, 0, 0), (o_b, vw, 0, 0), (z, vw, 0, cfg["R0"] // vw), (dycat, vw, 0, 0)],
                      consts=[g], row_outs=[(n, vw, BF16), (n, vw, BF16)], accs=[(1, vw)])


def _sg_ln(vv):
    mu = jnp.mean(vv, axis=-1, keepdims=True)
    cen = vv - mu
    rstd = lax.rsqrt(jnp.mean(cen * cen, axis=-1, keepdims=True) + EPS)
    return cen * rstd, rstd


def _sg_fwd(z, n, lng, lnb, w_s, bs_full, cfg):
    sgw, grp, sc = cfg["SGW"], cfg["SG_G"], cfg["SG_C"]
    gw = sgw // grp

    def body(rin, cin, rout, acc):
        u = _gelu(rin[0][...])
        xhat, _ = _sg_ln(_gelu(rin[1][...]))
        vvn = xhat * cin[0][...] + cin[1][...]
        for gg in range(grp):
            cs = slice(gg * gw, (gg + 1) * gw)
            s = _nn(cin[2][gg], vvn[:, cs]) + cin[3][:, cs]
            rout[0][:, cs] = (u[:, cs] * s).astype(BF16)

    return _rows_call(body, name="sg_fwd", nblk=n // sc, tr=sc,
                      row_ins=[(z, sgw, 0, cfg["U0"] // sgw), (z, sgw, 0, cfg["VV0"] // sgw)],
                      consts=[lng, lnb, w_s, bs_full], row_outs=[(n, sgw, BF16)])[0]


def _sg_bwd(z, dycat, n, lng, lnb, w_s, bs_full, cfg):
    sgw, grp, sc = cfg["SGW"], cfg["SG_G"], cfg["SG_C"]
    gw = sgw // grp

    def body(rin, cin, rout, acc):
        up, vp, dy = rin[0][...], rin[1][...], rin[2][...]
        u = _gelu(up)
        xhat, rstd = _sg_ln(_gelu(vp))
        lng_v = cin[0][...]
        vvn = xhat * lng_v + cin[1][...]
        ds = dy * u
        acc[1][...] += ds
        dvvn_parts = []
        for gg in range(grp):
            cs = slice(gg * gw, (gg + 1) * gw)
            w = cin[2][gg]
            s = _nn(w, vvn[:, cs]) + cin[3][:, cs]
            rout[0][:, cs] = (dy[:, cs] * s * _dgelu(up[:, cs])).astype(BF16)
            acc[0][gg] += _nt(ds[:, cs], vvn[:, cs])
            dvvn_parts.append(_tn(w, ds[:, cs]))
        dvvn = jnp.concatenate(dvvn_parts, axis=1)
        acc[2][...] += _colsum(dvvn * xhat)
        acc[3][...] += _colsum(dvvn)
        dxh = dvvn * lng_v
        dvv = rstd * (dxh - jnp.mean(dxh, axis=-1, keepdims=True)
                      - xhat * jnp.mean(dxh * xhat, axis=-1, keepdims=True))
        rout[1][...] = (dvv * _dgelu(vp)).astype(BF16)

    vw = cfg["VW"]
    return _rows_call(body, name="sg_bwd", nblk=n // sc, tr=sc,
                      row_ins=[(z, sgw, 0, cfg["U0"] // sgw), (z, sgw, 0, cfg["VV0"] // sgw),
                               (dycat, sgw, 0, vw // sgw)],
                      consts=[lng, lnb, w_s, bs_full], row_outs=[(n, sgw, BF16), (n, sgw, BF16)],
                      accs=[(grp, sc, sc), (sc, sgw), (1, sgw), (1, sgw)])


def _mid_fwd(x, mix, g1, post1, pre2, sh2, sc2, *, tr):
    n, d = x.shape

    def body(rin, cin, rout, acc):
        xv, mv = rin[0][...], rin[1][...]
        x1 = xv + cin[0][...] * (mv * _rstd(mv) * cin[1][...])
        rout[0][...] = x1
        n2 = x1 * _rstd(x1) * cin[2][...]
        rout[1][...] = (n2 * (1.0 + cin[4][...]) + cin[3][...]).astype(BF16)

    return _rows_call(body, name="mid_fwd", nblk=n // tr, tr=tr, row_ins=[(x, d, 0, 0), (mix, d, 0, 0)],
                      consts=[g1, post1, pre2, sh2, sc2], row_outs=[(n, d, F32), (n, d, BF16)])


def _head_bwd(x1, m2, target, g2, post2, *, tr):
    n, d = x1.shape

    def body(rin, cin, rout, acc):
        x1v, mv, tv = rin[0][...], rin[1][...], rin[2][...]
        g2v, pg = cin[0][...], cin[1][...]
        r = _rstd(mv)
        y2 = mv * r * pg
        err = (x1v + g2v * y2) - tv
        acc[2][...] += _colsum(err * err) * (0.5 / d)
        dx2 = err * (1.0 / d)
        rout[0][...] = dx2
        dy2 = dx2 * g2v
        acc[0][...] += _colsum(dx2 * y2)
        acc[1][...] += _colsum(dy2 * mv * r)
        rout[1][...] = _rms_bwd(mv, r, dy2 * pg).astype(BF16)

    return _rows_call(body, name="head_bwd", nblk=n // tr, tr=tr,
                      row_ins=[(x1, d, 0, 0), (m2, d, 0, 0), (target, d, 0, 0)], consts=[g2, post2],
                      row_outs=[(n, d, F32), (n, d, BF16)], accs=[(1, d)] * 3)


def _mid_bwd(dh2, x1, dx2, mix, sc2, pre2, g1, post1, *, tr):
    n, d = x1.shape

    def body(rin, cin, rout, acc):
        dh, x1v, dx2v, mv = rin[0][...], rin[1][...], rin[2][...], rin[3][...]
        sc2v, pre2v, g1v, post1v = cin[0][...], cin[1][...], cin[2][...], cin[3][...]
        r2 = _rstd(x1v)
        xr = x1v * r2
        acc[0][...] += _colsum(dh)
        acc[1][...] += _colsum(dh * (xr * pre2v))
        dn2 = dh * (1.0 + sc2v)
        acc[2][...] += _colsum(dn2 * xr)
        dx1 = dx2v + _rms_bwd(x1v, r2, dn2 * pre2v)
        rout[0][...] = dx1
        r1 = _rstd(mv)
        mr = mv * r1
        acc[3][...] += _colsum(dx1 * (mr * post1v))
        dy1 = dx1 * g1v
        acc[4][...] += _colsum(dy1 * mr)
        rout[1][...] = _rms_bwd(mv, r1, dy1 * post1v).astype(BF16)

    return _rows_call(body, name="mid_bwd", nblk=n // tr, tr=tr,
                      row_ins=[(dh2, d, 0, 0), (x1, d, 0, 0), (dx2, d, 0, 0), (mix, d, 0, 0)],
                      consts=[sc2, pre2, g1, post1], row_outs=[(n, d, F32), (n, d, BF16)], accs=[(1, d)] * 5)


def _in_bwd(da, x, dres, sc1, pre1, *, row_off, tr, name, sides=_NO_SIDES):
    n, d = x.shape
    with_res = dres is not None

    def body(rin, cin, rout, acc):
        dav, xv = rin[0][...], rin[1][...]
        sc1v, pre1v = cin[0][...], cin[1][...]
        r = _rstd(xv)
        xr = xv * r
        acc[0][...] += _colsum(dav)
        acc[1][...] += _colsum(dav * (xr * pre1v))
        dn = dav * (1.0 + sc1v)
        acc[2][...] += _colsum(dn * xr)
        if with_res:
            rout[0][...] = rin[2][...] + _rms_bwd(xv, r, dn * pre1v)

    row_ins = [(da, d, row_off // tr, 0), (x, d, 0, 0)] + ([(dres, d, 0, 0)] if with_res else [])
    return _rows_call(body, name=name, nblk=n // tr, tr=tr, row_ins=row_ins, consts=[sc1, pre1],
                      row_outs=[(n, d, F32)] if with_res else [], accs=[(1, d)] * 3, sides=sides)


def _ada_fwd(c16, w, b, *, tn):
    d, ncol = w.shape

    def body(c_ref, w_ref, b_ref, o_ref):
        o_ref[...] = _nn(_silu(c_ref[...]), w_ref[...]) + b_ref[...]

    return pl.pallas_call(
        body, name="ada_fwd", grid=(ncol // tn,),
        in_specs=[pl.BlockSpec((16, d), lambda j: (0, 0)), pl.BlockSpec((d, tn), lambda j: (0, j)),
                  pl.BlockSpec((1, tn), lambda j: (0, j))],
        out_specs=pl.BlockSpec((16, tn), lambda j: (0, j)),
        out_shape=jax.ShapeDtypeStruct((16, ncol), F32),
        compiler_params=_cparams(("arbitrary",)),
    )(c16, w, b)


def _ada_bwd(c16, dm, w, c_ctx, *, tn):
    d, ncol = w.shape

    def body(c_ref, dm_ref, w_ref, cc_ref, gw_ref, dcc_ref, acc):
        j = pl.program_id(0)

        @pl.when(j == 0)
        def _():
            acc[...] = jnp.zeros_like(acc)

        gw_ref[...] = _tn(_silu(c_ref[...]), dm_ref[...])
        acc[...] += _nt(dm_ref[...], w_ref[...])

        @pl.when(j == ncol // tn - 1)
        def _():
            dcc_ref[...] = _colsum(acc[8:16, :]) * _dsilu(cc_ref[...])

    return pl.pallas_call(
        body, name="ada_bwd", grid=(ncol // tn,),
        in_specs=[pl.BlockSpec((16, d), lambda j: (0, 0)), pl.BlockSpec((16, tn), lambda j: (0, j)),
                  pl.BlockSpec((d, tn), lambda j: (0, j)), pl.BlockSpec((1, d), lambda j: (0, 0))],
        out_specs=[pl.BlockSpec((d, tn), lambda j: (0, j)), pl.BlockSpec((1, d), lambda j: (0, 0))],
        out_shape=[jax.ShapeDtypeStruct((d, ncol), F32), jax.ShapeDtypeStruct((1, d), F32)],
        scratch_shapes=[pltpu.VMEM((16, d), F32)],
        compiler_params=_cparams(("arbitrary",)),
    )(c16, dm, w, c_ctx)


def _adam_math(w, g, m, v):
    m = ADAM_B1 * m + (1.0 - ADAM_B1) * g
    v = ADAM_B2 * v + (1.0 - ADAM_B2) * (g * g)
    m_hat = m / (1.0 - ADAM_B1 ** ADAM_STEP)
    v_hat = v / (1.0 - ADAM_B2 ** ADAM_STEP)
    delta = -ADAM_LR * (m_hat / (jnp.sqrt(v_hat) + ADAM_EPS) + ADAM_WD * w)
    return delta, m, v


def _adam_big(parts, w, m, v, *, name, tr):
    rows, cols = w.shape
    n_p = len(parts)

    def body(*refs):
        g = refs[0][...]
        for p in refs[1:n_p]:
            g = g + p[...].astype(F32)
        wv, mv, vv = refs[n_p][...], refs[n_p + 1][...], refs[n_p + 2][...]
        delta, m2, v2 = _adam_math(wv, g, mv, vv)
        refs[n_p + 3][...] = g
        refs[n_p + 4][...] = delta
        refs[n_p + 5][...] = m2
        refs[n_p + 6][...] = v2

    plain = pl.BlockSpec((tr, cols), lambda i: (i, 0))
    in_specs = []
    for arr, idx in parts:
        if idx is None:
            in_specs.append(plain)
        else:
            in_specs.append(pl.BlockSpec((None, tr, cols), lambda i, idx=idx: (idx, i, 0)))
    in_specs += [plain] * 3
    return pl.pallas_call(
        body, name=name, grid=(rows // tr,), in_specs=in_specs, out_specs=[plain] * 4,
        out_shape=[jax.ShapeDtypeStruct((rows, cols), F32)] * 4,
        compiler_params=_cparams(("parallel",)),
    )(*[p[0] for p in parts], w, m, v)


def _adam_small(g8, w, m, v):
    def body(g_ref, w_ref, m_ref, v_ref, go, do, mo, vo):
        g = g_ref[0]
        for r in range(1, N_DEV):
            g = g + g_ref[r]
        delta, m2, v2 = _adam_math(w_ref[...], g, m_ref[...], v_ref[...])
        go[...] = g
        do[...] = delta
        mo[...] = m2
        vo[...] = v2

    return pl.pallas_call(
        body, name="adam_small", out_shape=[jax.ShapeDtypeStruct(w.shape, F32)] * 4,
        compiler_params=pltpu.CompilerParams(vmem_limit_bytes=VMEM_LIMIT),
    )(g8, w, m, v)


VEC_W = 1024
ELEMS_PER_BLOCK = 256 * 1024


def _dense(v):
    a, k = v.shape
    kp = -(-k // (8 * VEC_W)) * (8 * VEC_W)
    return jnp.pad(v, ((0, 0), (0, kp - k))).reshape(a, kp // VEC_W, VEC_W)


def _all_gather_vec(v, *, name):
    k = v.shape[1]
    return _all_gather_small(_dense(v)[0], name=name).reshape(N_DEV, -1)[:, :k]


def _my_pos():
    return lax.axis_index("x"), lax.axis_index("y"), lax.axis_index("c")


def _flip(v, bit):
    return (1 - v) if bit else v


def _all_gather_small(v, *, name):
    r, k = v.shape

    def body(v_ref, out_ref, send, recv, lsem):
        x, y, c = _my_pos()
        me = 4 * x + 2 * y + c
        local = pltpu.make_async_copy(v_ref, out_ref.at[me], lsem)
        local.start()
        sends = []
        for kk in range(1, N_DEV):
            peer = (_flip(x, kk & 4), _flip(y, kk & 2), _flip(c, kk & 1))
            cp = pltpu.make_async_remote_copy(src_ref=v_ref, dst_ref=out_ref.at[me], send_sem=send.at[kk - 1],
                                              recv_sem=recv.at[kk - 1], device_id=peer, device_id_type=MESH)
            cp.start()
            sends.append(cp)
        for kk in range(1, N_DEV):
            px, py, pc = _flip(x, kk & 4), _flip(y, kk & 2), _flip(c, kk & 1)
            src = 4 * px + 2 * py + pc
            pltpu.make_async_remote_copy(src_ref=v_ref, dst_ref=out_ref.at[src], send_sem=send.at[kk - 1],
                                         recv_sem=recv.at[kk - 1], device_id=(px, py, pc),
                                         device_id_type=MESH).wait_recv()
        for cp in sends:
            cp.wait_send()
        local.wait()

    return pl.pallas_call(
        body, name=name, out_shape=jax.ShapeDtypeStruct((N_DEV, r, k), v.dtype),
        in_specs=[pl.BlockSpec(memory_space=pltpu.VMEM)], out_specs=pl.BlockSpec(memory_space=pltpu.VMEM),
        scratch_shapes=[pltpu.SemaphoreType.DMA((N_DEV - 1,)), pltpu.SemaphoreType.DMA((N_DEV - 1,)),
                        pltpu.SemaphoreType.DMA],
        compiler_params=pltpu.CompilerParams(vmem_limit_bytes=VMEM_LIMIT),
    )(v)


def _ag_job(shards):
    n_arr = len(shards)

    def tools(ins, outs, sems):
        send, recv, lsem = sems
        x, y, c = _my_pos()
        chips = [(1 - x, y), (x, 1 - y), (1 - x, 1 - y)]

        def copy(a, kk, block, to, src=None):
            dst = outs[a].at[4 * block[0] + 2 * block[1] + block[2]]
            return pltpu.make_async_remote_copy(src_ref=dst if src is None else src, dst_ref=dst,
                                                send_sem=send.at[a, kk], recv_sem=recv.at[a, kk],
                                                device_id=to, device_id_type=MESH)

        locals_ = [pltpu.make_async_copy(ins[a], outs[a].at[4 * x + 2 * y + c], lsem.at[a]) for a in range(n_arr)]
        firsts = []
        for a in range(n_arr):
            firsts.append(copy(a, 0, (x, y, c), (x, y, 1 - c), src=ins[a]))
            firsts += [copy(a, 1 + j, (x, y, c), (*chip, c), src=ins[a]) for j, chip in enumerate(chips)]
        return copy, locals_, firsts, chips, (x, y, c)

    def start(ins, outs, sems):
        _, locals_, firsts, _, _ = tools(ins, outs, sems)
        for cp in locals_ + firsts:
            cp.start()

    def finish(ins, outs, sems):
        copy, locals_, firsts, chips, (x, y, c) = tools(ins, outs, sems)
        me, sibling = (x, y, c), (x, y, 1 - c)
        passed = []
        for a in range(n_arr):
            for j, chip in enumerate(chips):
                copy(a, 1 + j, (*chip, c), me).wait_recv()
                fw = copy(a, 4 + j, (*chip, c), sibling)
                fw.start()
                passed.append(fw)
        for a in range(n_arr):
            copy(a, 0, sibling, me).wait_recv()
            for j, chip in enumerate(chips):
                copy(a, 4 + j, (*chip, 1 - c), me).wait_recv()
        for cp in firsts + passed:
            cp.wait_send()
        for lc in locals_:
            lc.wait()

    return dict(ins=list(shards), outs=[jax.ShapeDtypeStruct((N_DEV,) + s.shape, s.dtype) for s in shards],
                sems=[pltpu.SemaphoreType.DMA((n_arr, 7)), pltpu.SemaphoreType.DMA((n_arr, 7)),
                      pltpu.SemaphoreType.DMA((n_arr,))], start=start, finish=finish)


def _exchange_job(arrays, n_slots, out_slots, src_of, dst_of, peer_of, rows=None, chained=None):
    n_arr = len(arrays)

    def copies(ins, outs, sems):
        send, recv = sems
        x, y, c = _my_pos()
        res = []
        for a in range(n_arr):
            for s in range(n_slots):
                src, dst = ins[a].at[src_of(s, x, y, c)], outs[a].at[dst_of(s)]
                if rows is not None:
                    src, dst = src.at[pl.ds(rows[0], rows[1])], dst.at[pl.ds(rows[0], rows[1])]
                res.append(pltpu.make_async_remote_copy(
                    src_ref=src, dst_ref=dst, send_sem=send.at[a, s], recv_sem=recv.at[a, s],
                    device_id=peer_of(s, x, y, c), device_id_type=MESH))
        return res

    def start(ins, outs, sems):
        for cp in copies(ins, outs, sems):
            cp.start()

    def finish(ins, outs, sems):
        cps = copies(ins, outs, sems)
        for cp in cps:
            cp.wait_recv()
        for cp in cps:
            cp.wait_send()

    job = dict(ins=list(arrays), outs=[jax.ShapeDtypeStruct((out_slots,) + g.shape[1:], g.dtype) for g in arrays],
               sems=[pltpu.SemaphoreType.DMA((n_arr, n_slots)), pltpu.SemaphoreType.DMA((n_arr, n_slots))],
               start=start, finish=finish)
    if chained is not None:
        job["ins"] = list(arrays) + list(chained)
        job["alias"] = {n_arr + a: a for a in range(n_arr)}
    return job


def _pair_job(grads):
    return _exchange_job(
        grads, 4, 4,
        src_of=lambda s, x, y, c: 4 * _flip(x, s & 2) + 2 * _flip(y, s & 1) + (1 - c),
        dst_of=lambda s: s, peer_of=lambda s, x, y, c: (x, y, 1 - c))


def _chip_job(sums, rows=None, chained=None):
    return _exchange_job(
        sums, 3, 3, src_of=lambda s, x, y, c: s, dst_of=lambda s: s,
        peer_of=lambda s, x, y, c: (_flip(x, (s + 1) & 2), _flip(y, (s + 1) & 1), c), rows=rows, chained=chained)


def _run_sides(sides, *, name):
    n_si, n_so = len(sides.ins), len(sides.outs)

    def body(*refs):
        ins, outs, sems = refs[:n_si], refs[n_si:n_si + n_so], refs[n_si + n_so:]
        sides.run("start", ins, outs, sems)
        sides.run("finish", ins, outs, sems)

    return pl.pallas_call(
        body, name=name, out_shape=list(sides.outs), in_specs=[ANY_SPEC] * n_si, out_specs=[ANY_SPEC] * n_so,
        scratch_shapes=list(sides.sems), input_output_aliases=sides.aliases(0, 0),
    )(*sides.ins)


def _pair_add(g, t, *, name, tr, wire):
    _, r, cols = g.shape
    g4 = g.reshape(4, 2, r, cols)
    j0 = 1 if wire else 0

    def g_index(j, i):
        x, y, c = _my_pos()
        return (jnp.bitwise_xor(2 * x + y, j + j0), c, i, 0)

    def body(g_ref, t_ref, o_ref):
        o_ref[...] = (g_ref[...] + t_ref[...]).astype(o_ref.dtype)

    return pl.pallas_call(
        body, name=name, grid=(3 if wire else 1, r // tr),
        in_specs=[pl.BlockSpec((None, None, tr, cols), g_index),
                  pl.BlockSpec((None, tr, cols), lambda j, i: (j + j0, i, 0))],
        out_specs=pl.BlockSpec((None, tr, cols), lambda j, i: (j, i, 0)),
        out_shape=jax.ShapeDtypeStruct((3 if wire else 1, r, cols), BF16 if wire else F32),
        compiler_params=_cparams(("arbitrary", "arbitrary")),
    )(g4, t)


def _config(x, ctx, w_in, w_dec_f, gla_norm_g, sg_ln_g, w_s):
    n, d = x.shape[1], x.shape[2]
    tc = ctx.shape[1]
    h = gla_norm_g.shape[1]
    dv = gla_norm_g.shape[2] * N_DEV
    dk = dv // 2
    kw, vw = h * dk, h * dv
    lr = w_dec_f.shape[1]
    sgw = sg_ln_g.shape[1]
    cfg = dict(N=n, D=d, TC=tc, H=h, DV=dv, DK=dk, KW=kw, VW=vw, LR=lr, SGW=sgw, SG_G=w_s.shape[1],
               SG_C=w_s.shape[2], IN=w_in.shape[2] * N_DEV)
    cfg.update(K0=kw, V0=2 * kw, R0=2 * kw + vw, U0=2 * kw + 2 * vw)
    cfg.update(VV0=cfg["U0"] + sgw, L0=cfg["U0"] + 2 * sgw, ZW=cfg["U0"] + 2 * sgw + LANES)
    assert dk == LANES and vw == 2 * kw and 2 * lr <= LANES
    assert cfg["R0"] % vw == 0 and cfg["U0"] % sgw == 0 and cfg["VV0"] % sgw == 0 and vw % sgw == 0
    assert cfg["IN"] == 2 * kw + 2 * vw + 2 * lr + 2 * sgw
    return cfg


def _rope_tables(cfg):
    n, tc, dk = cfg["N"], cfg["TC"], cfg["DK"]
    m = dk // 4
    pos = jnp.arange(n)
    inv = ROPE_BASE ** (-jnp.arange(m, dtype=F32) / m)
    ang_r = (pos // GRID_W).astype(F32)[:, None] * inv[None, :]
    ang_c = (pos % GRID_W).astype(F32)[:, None] * inv[None, :]
    cos = jnp.concatenate([jnp.cos(ang_r)] * 2 + [jnp.cos(ang_c)] * 2, axis=1)
    sin = jnp.concatenate([-jnp.sin(ang_r), jnp.sin(ang_r), -jnp.sin(ang_c), jnp.sin(ang_c)], axis=1)
    scale = dk ** -0.5
    z = jnp.zeros((tc, dk), F32)
    one = jnp.ones((tc, dk), F32)
    return [jnp.concatenate([cos * scale, z]), jnp.concatenate([sin * scale, z]),
            jnp.concatenate([cos, one]), jnp.concatenate([sin, z])]


def _pair_sums(g, t, nm):
    rows_for = _tile(g.shape[1], max(8, ELEMS_PER_BLOCK // g.shape[2]), 16)
    return (_pair_add(g, t, name="rs_own_" + nm, tr=rows_for, wire=False),
            _pair_add(g, t, name="rs_wire_" + nm, tr=rows_for, wire=True))


def _local_step(x, ctx, target, mods, c_mods, w, cfg):
    n, d, tc = cfg["N"], cfg["D"], cfg["TC"]
    kw, vw, sgw, zw, lr = cfg["KW"], cfg["VW"], cfg["SGW"], cfg["ZW"], cfg["LR"]
    sh1, sc1, g1, sh2, sc2, g2 = mods
    csh1, csc1 = c_mods
    rt = n + tc
    tb = math.gcd(256, math.gcd(n, tc))
    tr = math.gcd(128, tb)
    tr_s = math.gcd(64, tb)
    fs = w["sh_1"].shape[1]
    ff = fs * N_DEV
    cs_in = w["sh_in"].shape[1]

    wg_in = _run_sides(_Sides([_ag_job([w["sh_in"]])]), name="ag_w_in")[0]
    w_full = wg_in.transpose(1, 0, 2).reshape(d, cfg["IN"])
    lf0 = 2 * kw + 2 * vw
    sg0 = lf0 + 2 * lr
    w_pad = jnp.concatenate([w_full[:, :lf0], w_full[:, sg0:], w_full[:, lf0:sg0],
                             jnp.zeros((d, LANES - 2 * lr), BF16)], axis=1)

    hx = _norm_mod(x, w["pre1_g"], sh1, sc1, name="in_norm_x", tr=tr)
    hc = _norm_mod(ctx, w["pre1_g"], csh1, csc1, name="in_norm_ctx", tr=tr)
    a_all = jnp.concatenate([hx, hc], axis=0)

    tm_a = _tile(rt, 1152, 16)
    z, wg_o, w_1 = _matmul(a_all, w_pad, "nn", rt, zw, d, tm=tm_a, tn=_tile(zw, 1152, LANES),
                           tk=_tile(d, 512, LANES), name="mm_in", out_shapes=[jax.ShapeDtypeStruct((rt, zw), F32)],
                           sides=_Sides([_ag_job([w["sh_o"], w["sh_1"]])]))
    w_o = wg_o.reshape(d, d)

    tabs = _rope_tables(cfg)
    qr, kr, la_f, la_b = _gla_prep(z, tabs, w["wdf_pad"], w["wdb_pad"], w["b_dec_f"], w["b_dec_b"], cfg, tr=tr)

    zero_st = jnp.zeros((cfg["H"], cfg["DV"], cfg["DK"]), F32)
    gla = dict(cfg=cfg, tb=tb)
    _, save_cf, st_cf = _gla_fwd(qr, kr, z, la_f, zero_st, rev=False, row_off=n, nrows=tc, name="gla_ctx_f", **gla)
    _, save_cb, st_cb = _gla_fwd(qr, kr, z, la_b, zero_st, rev=True, row_off=n, nrows=tc, name="gla_ctx_b", **gla)
    o_f, save_f, _ = _gla_fwd(qr, kr, z, la_f, st_cf, rev=False, row_off=0, nrows=n, name="gla_f", **gla)
    o_b, save_b, _ = _gla_fwd(qr, kr, z, la_b, st_cb, rev=True, row_off=0, nrows=n, name="gla_b", **gla)
    y_gla = _readout_fwd(o_f, o_b, z, w["gla_g"], cfg, tr=tr)
    y_sg = _sg_fwd(z, n, w["sg_ln_g"], w["sg_ln_b"], w["w_s"], w["bs_full"], cfg)
    ycat = jnp.concatenate([y_gla, y_sg], axis=1)

    tm_n = _tile(n, 1024, 16)
    mix = _matmul(ycat, w_o, "nn", n, d, d, tm=tm_n, tn=_tile(d, 1024, LANES), tk=_tile(d, 512, LANES),
                  name="mm_o", out_shapes=[jax.ShapeDtypeStruct((n, d), F32)])[0]
    x1, h2 = _mid_fwd(x, mix, g1, w["post1_g"], w["pre2_g"], sh2, sc2, tr=tr_s)

    tn_f = _tile(fs, 1024, LANES)
    tk_d = _tile(d, 512, LANES)

    def relu2(acc):
        return acc, jnp.square(jnp.maximum(acc, 0.0))

    a1, p1, wg_2 = _matmul(h2, w_1, "nn", n, ff, d, tm=tm_n, tn=tn_f, tk=tk_d, name="mm_1",
                           b_spec=_blocked_b_nn(fs, tk_d, tn_f), epilogue=relu2,
                           out_shapes=[jax.ShapeDtypeStruct((n, ff), BF16)] * 2,
                           sides=_Sides([_ag_job([w["sh_2"]])]))
    w_2 = wg_2.reshape(ff, d)
    tk_f = _tile(ff, 512, LANES)
    m2 = _matmul(p1, w_2, "nn", n, d, ff, tm=tm_n, tn=_tile(d, 1024, LANES), tk=tk_f, name="mm_2",
                 out_shapes=[jax.ShapeDtypeStruct((n, d), F32)])[0]

    dx2, dm2, dg2, dpost2, lossc = _head_bwd(x1, m2, target, g2, w["post2_g"], tr=tr_s)

    def drelu2(acc, a):
        return (acc * (2.0 * jnp.maximum(a.astype(F32), 0.0)),)

    da1 = _matmul(dm2, w_2, "nt", n, ff, d, tm=tm_n, tn=_tile(ff, 1024, LANES), tk=tk_d, name="mm_2_dx",
                  epilogue=drelu2, extras=(a1,), out_shapes=[jax.ShapeDtypeStruct((n, ff), BF16)])[0]
    tk_n = _tile(n, 512, 16)
    tm_d = _tile(d, 1024, LANES)
    g_1 = _matmul(h2, da1, "tn", d, ff, n, tm=tm_d, tn=tn_f, tk=tk_n, name="mm_1_dw",
                  out_specs=[_blocked_out(fs, tm_d, tn_f)],
                  out_shapes=[jax.ShapeDtypeStruct((N_DEV, d, fs), F32)])[0]
    dw_2, t_1 = _matmul(p1, dm2, "tn", ff, d, n, tm=_tile(ff, 1024, LANES), tn=_tile(d, 1024, LANES), tk=tk_n,
                        name="mm_2_dw", out_shapes=[jax.ShapeDtypeStruct((ff, d), F32)],
                        sides=_Sides([_pair_job([g_1])]))
    g_2 = dw_2.reshape(N_DEV, fs, d)
    p1_own, p1_wire = _pair_sums(g_1, t_1, "w_1")
    tk_fs = _tile(fs, 512, LANES)
    dh2, u_1, t_2 = _matmul(da1, w_1, "nt", n, d, ff, tm=tm_n, tn=_tile(d, 1024, LANES), tk=tk_fs, name="mm_1_dx",
                            b_spec=_blocked_b_nt(fs, _tile(d, 1024, LANES), tk_fs),
                            out_shapes=[jax.ShapeDtypeStruct((n, d), F32)],
                            sides=_Sides([_chip_job([p1_wire]), _pair_job([g_2])]))
    p2_own, p2_wire = _pair_sums(g_2, t_2, "w_2")
    rows_a = (fs * 5 // 16) // 16 * 16

    dx1, dmix, dsh2, dsc2, dpre2, dg1, dpost1 = _mid_bwd(dh2, x1, dx2, mix, sc2, w["pre2_g"], g1, w["post1_g"],
                                                         tr=tr_s)
    dw_o, u_2 = _matmul(ycat, dmix, "tn", d, d, n, tm=tm_d, tn=_tile(d, 1024, LANES), tk=tk_n, name="mm_o_dw",
                        out_shapes=[jax.ShapeDtypeStruct((d, d), F32)],
                        sides=_Sides([_chip_job([p2_wire], rows=(0, rows_a))]))
    g_o = dw_o.reshape(N_DEV, d // N_DEV, d)
    dycat, t_o = _matmul(dmix, w_o, "nt", n, d, d, tm=tm_n, tn=_tile(d, 1024, LANES), tk=tk_d, name="mm_o_dx",
                         out_shapes=[jax.ShapeDtypeStruct((n, d), F32)], sides=_Sides([_pair_job([g_o])]))
    po_own, po_wire = _pair_sums(g_o, t_o, "w_o")

    dzu, dzvv, dws, dbs_acc, dlng, dlnb = _sg_bwd(z, dycat, n, w["sg_ln_g"], w["sg_ln_b"], w["w_s"],
                                                  w["bs_full"], cfg)
    do, dzr, dgla_g = _readout_bwd(o_f, o_b, z, dycat, w["gla_g"], cfg, tr=tr)

    gf = _gla_bwd(qr, kr, z, la_f, do, save_f, zero_st, rev=False, row_off=0, nrows=n, name="gla_f_bwd", **gla)
    gb = _gla_bwd(qr, kr, z, la_b, do, save_b, zero_st, rev=True, row_off=0, nrows=n, name="gla_b_bwd", **gla)
    do_c = jnp.zeros((tc, vw), BF16)
    gcf = _gla_bwd(qr, kr, z, la_f, do_c, save_cf, gf[4], rev=False, row_off=n, nrows=tc, name="gla_ctx_f_bwd",
                   **gla)
    gcb = _gla_bwd(qr, kr, z, la_b, do_c, save_cb, gb[4], rev=True, row_off=n, nrows=tc, name="gla_ctx_b_bwd",
                   **gla)

    post = dict(la_f=la_f, la_b=la_b, z=z, tabs=tabs, wdf_pad=w["wdf_pad"], wdb_pad=w["wdb_pad"], cfg=cfg, tr=tr)
    dzq, dzk, dzv, dzl, dwdf, dwdb, dbdf, dbdb = _gla_post(gf, gb, row_off=0, nrows=n, name="gla_post", **post)
    czq, czk, czv, czl, cwdf, cwdb, cbdf, cbdb = _gla_post(gcf, gcb, row_off=n, nrows=tc, name="gla_post_ctx",
                                                           **post)
    zc = lambda wd: jnp.zeros((tc, wd), BF16)
    dz = jnp.concatenate([
        jnp.concatenate([dzq, dzk, dzv, dzr, dzu, dzvv, dzl], axis=1),
        jnp.concatenate([czq, czk, czv, zc(vw), zc(sgw), zc(sgw), czl], axis=1)], axis=0)

    dw_pad, u_2 = _matmul(a_all, dz, "tn", d, zw, rt, tm=tm_d, tn=_tile(zw, 1152, LANES), tk=_tile(rt, 1152, 16),
                          name="mm_in_dw", out_shapes=[jax.ShapeDtypeStruct((d, zw), F32)],
                          sides=_Sides([_chip_job([p2_wire], rows=(rows_a, fs - rows_a), chained=[u_2])]))
    dw_in = jnp.concatenate([dw_pad[:, :cfg["U0"]], dw_pad[:, cfg["L0"]:cfg["L0"] + 2 * lr],
                             dw_pad[:, cfg["U0"]:cfg["L0"]]], axis=1)
    g_in = dw_in.reshape(d, N_DEV, cs_in).transpose(1, 0, 2)
    da_all, u_o, t_in = _matmul(dz, w_pad, "nt", rt, d, zw, tm=tm_a, tn=_tile(d, 1024, LANES),
                                tk=_tile(zw, 1152, LANES), name="mm_in_dx",
                                out_shapes=[jax.ShapeDtypeStruct((rt, d), F32)],
                                sides=_Sides([_chip_job([po_wire]), _pair_job([g_in])]))
    pin_own, pin_wire = _pair_sums(g_in, t_in, "w_in")

    grad_x, dsh1, dsc1, dpre1, u_in = _in_bwd(da_all, x, dx1, sc1, w["pre1_g"], row_off=0, tr=tr_s,
                                              name="in_bwd_x", sides=_Sides([_chip_job([pin_wire])]))
    dcsh1, dcsc1, dpre1_c = _in_bwd(da_all, ctx, None, csc1, w["pre1_g"], row_off=n, tr=tr_s, name="in_bwd_ctx")

    small = dict(
        pre1_g=dpre1 + dpre1_c, post1_g=dpost1, pre2_g=dpre2, post2_g=dpost2,
        w_dec_f=(dwdf + cwdf)[:lr], w_dec_b=(dwdb + cwdb)[lr:2 * lr], b_dec_f=dbdf + cbdf, b_dec_b=dbdb + cbdb,
        gla_norm_g=dgla_g, sg_ln_g=dlng, sg_ln_b=dlnb, w_s=dws,
        b_s=dbs_acc.reshape(cfg["SG_C"], cfg["SG_G"], sgw // cfg["SG_G"]).sum(-1).T)
    dmod = jnp.concatenate([dsh1, dsc1, dg1, dsh2, dsc2, dg2], axis=1)
    dmod_c = jnp.concatenate([dcsh1, dcsc1], axis=1)
    big = dict(w_in=(pin_own, u_in), w_o=(po_own, u_o), w_1=(p1_own, u_1), w_2=(p2_own, u_2))
    return lossc, grad_x, big, small, dmod, dmod_c


SMALL_NAMES = ["b_ada", "pre1_g", "post1_g", "pre2_g", "post2_g", "w_dec_f", "b_dec_f", "w_dec_b", "b_dec_b",
               "gla_norm_g", "sg_ln_g", "sg_ln_b", "w_s", "b_s", "c_ctx"]
WEIGHT_ORDER = ["c_ctx", "w_ada", "b_ada", "pre1_g", "post1_g", "pre2_g", "post2_g", "w_in", "w_dec_f", "b_dec_f",
                "w_dec_b", "b_dec_b", "gla_norm_g", "sg_ln_g", "sg_ln_b", "w_s", "b_s", "w_o", "w_1", "w_2"]


def kernel(x, c, ctx, c_ctx, w_ada, b_ada, pre1_g, post1_g, pre2_g, post2_g, w_in, w_dec_f, b_dec_f, w_dec_b, b_dec_b, gla_norm_g, sg_ln_g, sg_ln_b, w_s, b_s, w_o, w_1, w_2, loss_target, m_c_ctx, m_w_ada, m_b_ada, m_pre1_g, m_post1_g, m_pre2_g, m_post2_g, m_w_in, m_w_dec_f, m_b_dec_f, m_w_dec_b, m_b_dec_b, m_gla_norm_g, m_sg_ln_g, m_sg_ln_b, m_w_s, m_b_s, m_w_o, m_w_1, m_w_2, v_c_ctx, v_w_ada, v_b_ada, v_pre1_g, v_post1_g, v_pre2_g, v_post2_g, v_w_in, v_w_dec_f, v_b_dec_f, v_w_dec_b, v_b_dec_b, v_gla_norm_g, v_sg_ln_g, v_sg_ln_b, v_w_s, v_b_s, v_w_o, v_w_1, v_w_2):
    weights = dict(c_ctx=c_ctx, w_ada=w_ada, b_ada=b_ada, pre1_g=pre1_g, post1_g=post1_g, pre2_g=pre2_g,
                   post2_g=post2_g, w_in=w_in, w_dec_f=w_dec_f, b_dec_f=b_dec_f, w_dec_b=w_dec_b, b_dec_b=b_dec_b,
                   gla_norm_g=gla_norm_g, sg_ln_g=sg_ln_g, sg_ln_b=sg_ln_b, w_s=w_s, b_s=b_s, w_o=w_o, w_1=w_1,
                   w_2=w_2)
    mom_m = dict(c_ctx=m_c_ctx, w_ada=m_w_ada, b_ada=m_b_ada, pre1_g=m_pre1_g, post1_g=m_post1_g, pre2_g=m_pre2_g,
                 post2_g=m_post2_g, w_in=m_w_in, w_dec_f=m_w_dec_f, b_dec_f=m_b_dec_f, w_dec_b=m_w_dec_b,
                 b_dec_b=m_b_dec_b, gla_norm_g=m_gla_norm_g, sg_ln_g=m_sg_ln_g, sg_ln_b=m_sg_ln_b, w_s=m_w_s,
                 b_s=m_b_s, w_o=m_w_o, w_1=m_w_1, w_2=m_w_2)
    mom_v = dict(c_ctx=v_c_ctx, w_ada=v_w_ada, b_ada=v_b_ada, pre1_g=v_pre1_g, post1_g=v_post1_g, pre2_g=v_pre2_g,
                 post2_g=v_post2_g, w_in=v_w_in, w_dec_f=v_w_dec_f, b_dec_f=v_b_dec_f, w_dec_b=v_w_dec_b,
                 b_dec_b=v_b_dec_b, gla_norm_g=v_gla_norm_g, sg_ln_g=v_sg_ln_g, sg_ln_b=v_sg_ln_b, w_s=v_w_s,
                 b_s=v_b_s, w_o=v_w_o, w_1=v_w_1, w_2=v_w_2)

    cfg = _config(x, ctx, w_in, w_dec_f, gla_norm_g, sg_ln_g, w_s)
    n, d, h, dv, kw, vw, lr, sgw = (cfg[k] for k in ("N", "D", "H", "DV", "KW", "VW", "LR", "SGW"))
    dvs, kws = dv // N_DEV, kw // N_DEV
    ix, iy, ic = _my_pos()
    me = 4 * ix + 2 * iy + ic

    pack1 = jnp.concatenate([c.reshape(1, d), w_dec_f.reshape(1, lr * kws), w_dec_b.reshape(1, lr * kws),
                             gla_norm_g.reshape(1, h * dvs)], axis=1)
    g1 = _all_gather_vec(pack1, name="ag_small_in")
    c_all = g1[:, :d]
    o1 = d
    wdf = g1[:, o1:o1 + lr * kws].reshape(N_DEV, lr, kws).transpose(1, 0, 2).reshape(lr, kw)
    o1 += lr * kws
    wdb = g1[:, o1:o1 + lr * kws].reshape(N_DEV, lr, kws).transpose(1, 0, 2).reshape(lr, kw)
    o1 += lr * kws
    gla_g = g1[:, o1:o1 + h * dvs].reshape(N_DEV, h, dvs).transpose(1, 0, 2).reshape(1, h * dv)

    c16 = jnp.concatenate([c_all, jnp.broadcast_to(c_ctx.reshape(1, d), (N_DEV, d))], axis=0)
    ncol = w_ada.shape[2]
    wa = w_ada.reshape(d, ncol)
    b_mine = lax.dynamic_slice(b_ada, (0, me * ncol), (1, ncol))
    tn_ada = _tile(ncol, 512, LANES)
    mod_mine = _ada_fwd(c16, wa, b_mine, tn=tn_ada)
    mod_all = _all_gather_small(mod_mine, name="ag_mod").transpose(1, 0, 2).reshape(16, N_DEV * ncol)
    mod_b = lax.dynamic_slice(mod_all, (me, 0), (1, 6 * d))
    mods = [mod_b[:, i * d:(i + 1) * d] for i in range(6)]
    c_mods = [mod_all[N_DEV:N_DEV + 1, :d], mod_all[N_DEV:N_DEV + 1, d:2 * d]]

    zpad = lambda r: jnp.zeros((r, kw), F32)
    w = dict(
        sh_in=w_in.reshape(d, w_in.shape[2]).astype(BF16), sh_o=w_o.reshape(w_o.shape[1], d).astype(BF16),
        sh_1=w_1.reshape(d, w_1.shape[2]).astype(BF16), sh_2=w_2.reshape(w_2.shape[1], d).astype(BF16),
        pre1_g=pre1_g, post1_g=post1_g, pre2_g=pre2_g, post2_g=post2_g, b_dec_f=b_dec_f, b_dec_b=b_dec_b,
        wdf_pad=jnp.concatenate([wdf, zpad(LANES - lr)], axis=0),
        wdb_pad=jnp.concatenate([zpad(lr), wdb, zpad(LANES - 2 * lr)], axis=0),
        gla_g=gla_g, sg_ln_g=sg_ln_g, sg_ln_b=sg_ln_b, w_s=w_s[0],
        bs_full=jnp.repeat(b_s[0].T, sgw // cfg["SG_G"], axis=1))

    lossc, grad_x, big, small, dmod, dmod_c = _local_step(x[0], ctx[0], loss_target[0], mods, c_mods, w, cfg)
    loss = lax.psum(jnp.sum(lossc), AXES)

    order3 = ["pre1_g", "post1_g", "pre2_g", "post2_g", "w_dec_f", "b_dec_f", "w_dec_b", "b_dec_b", "gla_norm_g",
              "sg_ln_g", "sg_ln_b", "w_s", "b_s"]
    pieces = [dmod, dmod_c] + [small[k].reshape(1, -1) for k in order3]
    sizes = [p.shape[1] for p in pieces]
    g3 = _all_gather_vec(jnp.concatenate(pieces, axis=1), name="ag_small_grads")
    offs = [0]
    for s in sizes:
        offs.append(offs[-1] + s)
    dmod_all = g3[:, :6 * d]
    dmod_c_all = jnp.pad(g3[:, offs[1]:offs[2]], ((0, 0), (0, 4 * d)))
    parts8 = {k: g3[:, offs[2 + i]:offs[3 + i]] for i, k in enumerate(order3)}
    parts8["b_ada"] = dmod_all + dmod_c_all
    parts8["w_dec_f"] = lax.dynamic_slice(parts8["w_dec_f"].reshape(N_DEV, lr, kw), (0, 0, me * kws),
                                          (N_DEV, lr, kws)).reshape(N_DEV, -1)
    parts8["w_dec_b"] = lax.dynamic_slice(parts8["w_dec_b"].reshape(N_DEV, lr, kw), (0, 0, me * kws),
                                          (N_DEV, lr, kws)).reshape(N_DEV, -1)
    parts8["gla_norm_g"] = lax.dynamic_slice(parts8["gla_norm_g"].reshape(N_DEV, h, dv), (0, 0, me * dvs),
                                             (N_DEV, h, dvs)).reshape(N_DEV, -1)

    dm16 = jnp.concatenate([dmod_all, dmod_c_all], axis=0)
    dm_mine = lax.dynamic_slice(dm16, (0, me * ncol), (16, ncol))
    g_w_ada, dcc = _ada_bwd(c16, dm_mine, wa, c_ctx.reshape(1, d), tn=tn_ada)
    parts8["c_ctx"] = _all_gather_vec(dcc, name="ag_cctx")

    flat = lambda t: t.reshape(1, -1)
    g8 = _dense(jnp.concatenate([parts8[k] for k in SMALL_NAMES], axis=1))
    ws, ms, vs = [_dense(jnp.concatenate([flat(src[k]) for k in SMALL_NAMES], axis=1))[0]
                  for src in (weights, mom_m, mom_v)]
    res_small = [r.reshape(1, -1) for r in _adam_small(g8, ws, ms, vs)]
    out = {}
    off = 0
    for k in SMALL_NAMES:
        sz = weights[k].size
        out[k] = [r[:, off:off + sz].reshape(weights[k].shape) for r in res_small]
        off += sz

    rows_for = lambda r, cols: _tile(r, max(8, ELEMS_PER_BLOCK // cols), 16)
    for nm in ("w_in", "w_o", "w_1", "w_2"):
        own, u = big[nm]
        shp = weights[nm].shape
        r2 = (shp[1], shp[2])
        res = _adam_big([(own, 0), (u, 0), (u, 1), (u, 2)], weights[nm].reshape(r2), mom_m[nm].reshape(r2),
                        mom_v[nm].reshape(r2), name="adam_" + nm, tr=rows_for(*r2))
        out[nm] = [r.reshape(shp) for r in res]
    r2 = (d, ncol)
    res = _adam_big([(g_w_ada, None)], wa, m_w_ada.reshape(r2), v_w_ada.reshape(r2), name="adam_w_ada",
                    tr=rows_for(*r2))
    out["w_ada"] = [r.reshape(w_ada.shape) for r in res]

    outs = [loss, grad_x[None]]
    for i in range(4):
        outs += [out[k][i] for k in WEIGHT_ORDER]
    return tuple(outs)
```

```python
import math

import jax
import jax.numpy as jnp
from jax import lax
from jax.experimental import pallas as pl
from jax.experimental.pallas import tpu as pltpu

F32 = jnp.float32
BF16 = jnp.bfloat16
MXU_DTYPE = jnp.bfloat16
HI = lax.Precision.HIGHEST

N_DEV = 8
AXES = ("x", "y", "c")
MESH = pl.DeviceIdType.MESH
LANES = 128
VMEM_LIMIT = 56 * 1024 * 1024

EPS = 1e-6
GRID_W = 64
GLA_CHUNK = 64
GLA_TAU = 16.0
ROPE_BASE = 10000.0
ADAM_LR = 0.001
ADAM_B1 = 0.9
ADAM_B2 = 0.999
ADAM_EPS = 1e-08
ADAM_WD = 0.01
ADAM_STEP = 10


def _cparams(sem):
    return pltpu.CompilerParams(dimension_semantics=sem, vmem_limit_bytes=VMEM_LIMIT)


def _tile(n, target, align):
    if n <= target:
        return n
    best = None
    for t in range(align, target + 1, align):
        if n % t == 0:
            best = t
    assert best is not None, (n, target, align)
    return best


def _dg(a, b, dims, prec=None):
    return lax.dot_general(a, b, (dims, ((), ())), precision=prec, preferred_element_type=F32)


def _nn(a, b):
    return _dg(a.astype(MXU_DTYPE), b.astype(MXU_DTYPE), ((1,), (0,)))


def _nt(a, b):
    return _dg(a.astype(MXU_DTYPE), b.astype(MXU_DTYPE), ((1,), (1,)))


def _tn(a, b):
    return _dg(a.astype(MXU_DTYPE), b.astype(MXU_DTYPE), ((0,), (0,)))


def _sigmoid(x):
    return 1.0 / (1.0 + jnp.exp(-x))


def _silu(x):
    return x * _sigmoid(x)


def _dsilu(x):
    s = _sigmoid(x)
    return s * (1.0 + x * (1.0 - s))


def _gelu(x):
    return 0.5 * x * (1.0 + lax.erf(x * (1.0 / math.sqrt(2.0))))


def _dgelu(x):
    return 0.5 * (1.0 + lax.erf(x * (1.0 / math.sqrt(2.0)))) + x * jnp.exp(-0.5 * x * x) * (1.0 / math.sqrt(2.0 * math.pi))


def _rstd(x):
    return lax.rsqrt(jnp.mean(x * x, axis=-1, keepdims=True) + EPS)


def _rms_bwd(x, r, dn):
    return r * dn - x * (r * r * r) * jnp.mean(dn * x, axis=-1, keepdims=True)


def _colsum(x):
    return jnp.sum(x, axis=0, keepdims=True)


class _Sides:
    def __init__(self, jobs):
        self.jobs = list(jobs)
        self.ins = [a for j in self.jobs for a in j["ins"]]
        self.outs = [o for j in self.jobs for o in j["outs"]]
        self.sems = [s for j in self.jobs for s in j["sems"]]

    def aliases(self, in_base, out_base):
        res, oi, oo = {}, 0, 0
        for j in self.jobs:
            for a, b in j.get("alias", {}).items():
                res[in_base + oi + a] = out_base + oo + b
            oi += len(j["ins"])
            oo += len(j["outs"])
        return res

    def run(self, phase, in_refs, out_refs, sem_refs):
        oi = oo = os_ = 0
        for j in self.jobs:
            ni, no, ns = len(j["ins"]), len(j["outs"]), len(j["sems"])
            j[phase](in_refs[oi:oi + ni], out_refs[oo:oo + no], sem_refs[os_:os_ + ns])
            oi, oo, os_ = oi + ni, oo + no, os_ + ns


_NO_SIDES = _Sides([])
ANY_SPEC = pl.BlockSpec(memory_space=pl.ANY)


def _matmul(a, b, mode, m, n, k, *, tm, tn, tk, name, out_shapes, b_spec=None, out_specs=None,
            epilogue=None, extras=(), sides=_NO_SIDES):
    nk = k // tk
    assert m % tm == 0 and n % tn == 0 and k % tk == 0, (name, m, n, k, tm, tn, tk)
    dot = {"nn": _nn, "nt": _nt, "tn": _tn}[mode]
    if mode == "tn":
        a_spec = pl.BlockSpec((tk, tm), lambda i, j, kk: (kk, i))
    else:
        a_spec = pl.BlockSpec((tm, tk), lambda i, j, kk: (i, kk))
    if b_spec is None:
        if mode == "nt":
            b_spec = pl.BlockSpec((tn, tk), lambda i, j, kk: (j, kk))
        else:
            b_spec = pl.BlockSpec((tk, tn), lambda i, j, kk: (kk, j))
    mn_spec = pl.BlockSpec((tm, tn), lambda i, j, kk: (i, j))
    if out_specs is None:
        out_specs = [mn_spec] * len(out_shapes)
    n_extra = len(extras)
    n_out = len(out_shapes)
    n_si, n_so = len(sides.ins), len(sides.outs)
    ni, nj = m // tm, n // tn

    def body(a_ref, b_ref, *rest):
        extra_refs = rest[:n_extra]
        rest = rest[n_extra:]
        side_in, rest = rest[:n_si], rest[n_si:]
        out_refs, rest = rest[:n_out], rest[n_out:]
        side_out, rest = rest[:n_so], rest[n_so:]
        acc, side_sems = rest[0], rest[1:]
        i, j, kk = pl.program_id(0), pl.program_id(1), pl.program_id(2)

        if sides.jobs:
            @pl.when((i == 0) & (j == 0) & (kk == 0))
            def _():
                sides.run("start", side_in, side_out, side_sems)

        @pl.when(kk == 0)
        def _():
            acc[...] = jnp.zeros_like(acc)

        acc[...] += dot(a_ref[...], b_ref[...])

        @pl.when(kk == nk - 1)
        def _():
            vals = (acc[...],) if epilogue is None else epilogue(acc[...], *[e[...] for e in extra_refs])
            for o, v in zip(out_refs, vals):
                o[...] = v.astype(o.dtype)

        if sides.jobs:
            @pl.when((i == ni - 1) & (j == nj - 1) & (kk == nk - 1))
            def _():
                sides.run("finish", side_in, side_out, side_sems)

    sem = ("arbitrary",) * 3 if sides.jobs else ("parallel", "parallel", "arbitrary")
    res = pl.pallas_call(
        body, name=name, grid=(ni, nj, nk),
        in_specs=[a_spec, b_spec] + [mn_spec] * n_extra + [ANY_SPEC] * n_si,
        out_specs=list(out_specs) + [ANY_SPEC] * n_so, out_shape=list(out_shapes) + list(sides.outs),
        scratch_shapes=[pltpu.VMEM((tm, tn), F32)] + list(sides.sems),
        input_output_aliases=sides.aliases(2 + n_extra, n_out),
        compiler_params=_cparams(sem),
    )(a, b, *extras, *sides.ins)
    return res


def _blocked_b_nn(ns, tk, tn):
    assert ns % tn == 0
    return pl.BlockSpec((None, tk, tn), lambda i, j, kk: ((j * tn) // ns, kk, ((j * tn) % ns) // tn))


def _blocked_b_nt(ks, tn, tk):
    assert ks % tk == 0
    return pl.BlockSpec((None, tn, tk), lambda i, j, kk: ((kk * tk) // ks, j, ((kk * tk) % ks) // tk))


def _blocked_out(ns, tm, tn):
    assert ns % tn == 0
    return pl.BlockSpec((None, tm, tn), lambda i, j, kk: ((j * tn) // ns, i, ((j * tn) % ns) // tn))


def _rows_call(body, *, name, nblk, tr, row_ins, consts, row_outs, accs=(), sides=_NO_SIDES):
    n_ri, n_c, n_ro, n_acc = len(row_ins), len(consts), len(row_outs), len(accs)
    n_si, n_so = len(sides.ins), len(sides.outs)

    def kern(*refs):
        i = pl.program_id(0)
        rin, refs = refs[:n_ri], refs[n_ri:]
        cin, refs = refs[:n_c], refs[n_c:]
        side_in, refs = refs[:n_si], refs[n_si:]
        rout, refs = refs[:n_ro], refs[n_ro:]
        acc, refs = refs[:n_acc], refs[n_acc:]
        side_out, side_sems = refs[:n_so], refs[n_so:]

        if sides.jobs:
            @pl.when(i == 0)
            def _():
                sides.run("start", side_in, side_out, side_sems)

        if n_acc:
            @pl.when(i == 0)
            def _():
                for r in acc:
                    r[...] = jnp.zeros_like(r)

        body(rin, cin, rout, acc)

        if sides.jobs:
            @pl.when(i == nblk - 1)
            def _():
                sides.run("finish", side_in, side_out, side_sems)

    in_specs = [pl.BlockSpec((tr, w), lambda i, ro=ro, co=co: (i + ro, co)) for (_, w, ro, co) in row_ins]
    in_specs += [pl.BlockSpec(cst.shape, lambda i, nd=cst.ndim: (0,) * nd) for cst in consts]
    out_specs = [pl.BlockSpec((tr, w), lambda i: (i, 0)) for (_, w, _) in row_outs]
    out_specs += [pl.BlockSpec(s, lambda i, nd=len(s): (0,) * nd) for s in accs]
    out_shape = [jax.ShapeDtypeStruct((r, w), dt) for (r, w, dt) in row_outs]
    out_shape += [jax.ShapeDtypeStruct(s, F32) for s in accs]
    return pl.pallas_call(
        kern, name=name, grid=(nblk,), in_specs=in_specs + [ANY_SPEC] * n_si,
        out_specs=out_specs + [ANY_SPEC] * n_so, out_shape=out_shape + list(sides.outs),
        scratch_shapes=list(sides.sems),
        input_output_aliases=sides.aliases(n_ri + n_c, n_ro + n_acc),
        compiler_params=_cparams(("arbitrary",)),
    )(*[r[0] for r in row_ins], *consts, *sides.ins)


def _norm_mod(x, g, shift, scale, *, name, tr):
    rows, d = x.shape

    def body(rin, cin, rout, acc):
        xv = rin[0][...]
        n = xv * _rstd(xv) * cin[0][...]
        rout[0][...] = (n * (1.0 + cin[2][...]) + cin[1][...]).astype(BF16)

    return _rows_call(body, name=name, nblk=rows // tr, tr=tr, row_ins=[(x, d, 0, 0)],
                      consts=[g, shift, scale], row_outs=[(rows, d, BF16)])[0]


def _swap_halves(t, width):
    lane = lax.broadcasted_iota(jnp.int32, t.shape, 1)
    return jnp.where(lane % 64 < 32, pltpu.roll(t, width - 32, 1), pltpu.roll(t, 32, 1))


def _gla_prep(z, tabs, wdf_pad, wdb_pad, bdf, bdb, cfg, *, tr):
    rows = z.shape[0]
    kw, h = cfg["KW"], cfg["H"]

    def body(rin, cin, rout, acc):
        zq, zk, zl = rin[0][...], rin[1][...], rin[2][...]
        cq, sq, ck, sk = [jnp.concatenate([rin[3 + t][...]] * h, axis=1) for t in range(4)]
        rout[0][...] = zq * cq + _swap_halves(zq, kw) * sq
        rout[1][...] = zk * ck + _swap_halves(zk, kw) * sk
        for o, w, b in ((2, cin[0], cin[2]), (3, cin[1], cin[3])):
            a = _nn(zl, w[...]) + b[...]
            rout[o][...] = (jnp.minimum(a, 0.0) - jnp.log(1.0 + jnp.exp(-jnp.abs(a)))) * (1.0 / GLA_TAU)

    row_ins = [(z, kw, 0, 0), (z, kw, 0, 1), (z, LANES, 0, cfg["L0"] // LANES)]
    row_ins += [(t, LANES, 0, 0) for t in tabs]
    return _rows_call(body, name="gla_prep", nblk=rows // tr, tr=tr, row_ins=row_ins,
                      consts=[wdf_pad, wdb_pad, bdf, bdb], row_outs=[(rows, kw, F32)] * 4)


def _chunk_consts(rev):
    c = GLA_CHUNK
    r = lax.broadcasted_iota(jnp.int32, (c, c), 0)
    cc = lax.broadcasted_iota(jnp.int32, (c, c), 1)
    keep = (cc >= r) if rev else (cc <= r)
    return keep, keep.astype(F32)


def _chunk_decay(la, keep_f):
    b = _dg(keep_f, la, ((1,), (0,)), HI)
    return b, _colsum(la)


def _gla_fwd(qr, kr, z, la, st0, cfg, *, rev, row_off, nrows, tb, name, sides=_NO_SIDES):
    h, dk, dv = cfg["H"], cfg["DK"], cfg["DV"]
    c = GLA_CHUNK
    nsub = tb // c
    nblk = nrows // tb
    roff = row_off // tb
    v_cb = cfg["V0"] // dv
    n_si, n_so = len(sides.ins), len(sides.outs)

    def blk(j):
        return (nblk - 1 - j) if rev else j

    def body(q_ref, k_ref, v_ref, la_ref, st0_ref, *rest):
        side_in, rest = rest[:n_si], rest[n_si:]
        o_ref, save_ref, fin_ref = rest[:3]
        side_out, st, side_sems = rest[3:3 + n_so], rest[3 + n_so], rest[4 + n_so:]
        hh, j = pl.program_id(0), pl.program_id(1)

        if sides.jobs:
            @pl.when((hh == 0) & (j == 0))
            def _():
                sides.run("start", side_in, side_out, side_sems)

        @pl.when(j == 0)
        def _():
            st[...] = st0_ref[...]

        keep, keep_f = _chunk_consts(rev)
        order = range(nsub - 1, -1, -1) if rev else range(nsub)
        for s in order:
            rs = pl.ds(s * c, c)
            q, k, v, lac = q_ref[rs, :], k_ref[rs, :], v_ref[rs, :], la_ref[rs, :]
            b, btot = _chunk_decay(lac, keep_f)
            qe = q * jnp.exp(b)
            ke = k * jnp.exp(-b)
            kl = k * jnp.exp(btot - b)
            s_in = st[...]
            save_ref[s] = s_in
            att = jnp.where(keep, _nt(qe, ke), 0.0)
            o_ref[rs, :] = _nt(qe, s_in) + _nn(att, v)
            st[...] = s_in * jnp.exp(btot) + _tn(v, kl)

        @pl.when(j == nblk - 1)
        def _():
            fin_ref[...] = st[...]

        if sides.jobs:
            @pl.when((hh == h - 1) & (j == nblk - 1))
            def _():
                sides.run("finish", side_in, side_out, side_sems)

    in_specs = [
        pl.BlockSpec((tb, dk), lambda hh, j: (roff + blk(j), hh)),
        pl.BlockSpec((tb, dk), lambda hh, j: (roff + blk(j), hh)),
        pl.BlockSpec((tb, dv), lambda hh, j: (roff + blk(j), v_cb + hh)),
        pl.BlockSpec((tb, dk), lambda hh, j: (roff + blk(j), hh)),
        pl.BlockSpec((None, dv, dk), lambda hh, j: (hh, 0, 0)),
    ]
    out_specs = [
        pl.BlockSpec((tb, dv), lambda hh, j: (blk(j), hh)),
        pl.BlockSpec((None, nsub, dv, dk), lambda hh, j: (hh, blk(j), 0, 0)),
        pl.BlockSpec((None, dv, dk), lambda hh, j: (hh, 0, 0)),
    ]
    out_shape = [
        jax.ShapeDtypeStruct((nrows, h * dv), F32),
        jax.ShapeDtypeStruct((h, nrows // c, dv, dk), F32),
        jax.ShapeDtypeStruct((h, dv, dk), F32),
    ]
    return pl.pallas_call(
        body, name=name, grid=(h, nblk), in_specs=in_specs + [ANY_SPEC] * n_si,
        out_specs=out_specs + [ANY_SPEC] * n_so, out_shape=out_shape + list(sides.outs),
        scratch_shapes=[pltpu.VMEM((dv, dk), F32)] + list(sides.sems),
        input_output_aliases=sides.aliases(5, 3),
        compiler_params=_cparams(("arbitrary", "arbitrary")),
    )(qr, kr, z, la, st0, *sides.ins)


def _gla_bwd(qr, kr, z, la, do, save, dst_init, cfg, *, rev, row_off, nrows, tb, name):
    h, dk, dv = cfg["H"], cfg["DK"], cfg["DV"]
    c = GLA_CHUNK
    nsub = tb // c
    nblk = nrows // tb
    roff = row_off // tb
    v_cb = cfg["V0"] // dv

    def blk(j):
        return j if rev else (nblk - 1 - j)

    def body(q_ref, k_ref, v_ref, la_ref, do_ref, save_ref, di_ref, dq_ref, dk_ref, dv_ref, dla_ref, d0_ref, dst):
        j = pl.program_id(1)

        @pl.when(j == 0)
        def _():
            dst[...] = di_ref[...]

        keep, keep_f = _chunk_consts(rev)
        keep_t = _chunk_consts(not rev)[1]
        order = range(nsub) if rev else range(nsub - 1, -1, -1)
        for s in order:
            rs = pl.ds(s * c, c)
            q, k, v, lac, dout = q_ref[rs, :], k_ref[rs, :], v_ref[rs, :], la_ref[rs, :], do_ref[rs, :]
            b, btot = _chunk_decay(lac, keep_f)
            eb = jnp.exp(b)
            enb = jnp.exp(-b)
            elb = jnp.exp(btot - b)
            etot = jnp.exp(btot)
            qe, ke, kl = q * eb, k * enb, k * elb
            s_in = save_ref[s]
            d_after = dst[...]
            att = jnp.where(keep, _nt(qe, ke), 0.0)
            datt = jnp.where(keep, _nt(dout, v), 0.0)
            dqe = _nn(dout, s_in) + _nn(datt, ke)
            dke = _tn(datt, qe)
            dkl = _nn(v, d_after)
            dv_ref[rs, :] = _tn(att, dout) + _nt(kl, d_after)
            db = dqe * qe - dke * ke - dkl * kl
            dbtot = _colsum(dkl * kl) + _colsum(d_after * s_in) * etot
            dla_ref[rs, :] = _dg(keep_t, db, ((1,), (0,)), HI) + dbtot
            dq_ref[rs, :] = dqe * eb
            dk_ref[rs, :] = dke * enb + dkl * elb
            dst[...] = d_after * etot + _tn(dout, qe)

        @pl.when(j == nblk - 1)
        def _():
            d0_ref[...] = dst[...]

    in_specs = [
        pl.BlockSpec((tb, dk), lambda hh, j: (roff + blk(j), hh)),
        pl.BlockSpec((tb, dk), lambda hh, j: (roff + blk(j), hh)),
        pl.BlockSpec((tb, dv), lambda hh, j: (roff + blk(j), v_cb + hh)),
        pl.BlockSpec((tb, dk), lambda hh, j: (roff + blk(j), hh)),
        pl.BlockSpec((tb, dv), lambda hh, j: (blk(j), hh)),
        pl.BlockSpec((None, nsub, dv, dk), lambda hh, j: (hh, blk(j), 0, 0)),
        pl.BlockSpec((None, dv, dk), lambda hh, j: (hh, 0, 0)),
    ]
    out_specs = [
        pl.BlockSpec((tb, dk), lambda hh, j: (blk(j), hh)),
        pl.BlockSpec((tb, dk), lambda hh, j: (blk(j), hh)),
        pl.BlockSpec((tb, dv), lambda hh, j: (blk(j), hh)),
        pl.BlockSpec((tb, dk), lambda hh, j: (blk(j), hh)),
        pl.BlockSpec((None, dv, dk), lambda hh, j: (hh, 0, 0)),
    ]
    out_shape = [
        jax.ShapeDtypeStruct((nrows, h * dk), F32),
        jax.ShapeDtypeStruct((nrows, h * dk), F32),
        jax.ShapeDtypeStruct((nrows, h * dv), F32),
        jax.ShapeDtypeStruct((nrows, h * dk), F32),
        jax.ShapeDtypeStruct((h, dv, dk), F32),
    ]
    return pl.pallas_call(
        body, name=name, grid=(h, nblk), in_specs=in_specs, out_specs=out_specs, out_shape=out_shape,
        scratch_shapes=[pltpu.VMEM((dv, dk), F32)],
        compiler_params=_cparams(("arbitrary", "arbitrary")),
    )(qr, kr, z, la, do, save, dst_init)


def _gla_post(gf, gb, la_f, la_b, z, tabs, wdf_pad, wdb_pad, cfg, *, row_off, nrows, tr, name):
    kw, vw = cfg["KW"], cfg["VW"]
    h = cfg["H"]
    ro = row_off // tr

    def body(rin, cin, rout, acc):
        dq = rin[0][...] + rin[1][...]
        dk_ = rin[2][...] + rin[3][...]
        zl = rin[10][...]
        cq, sq, ck, sk = [jnp.concatenate([rin[11 + t][...]] * h, axis=1) for t in range(4)]
        rout[0][...] = (dq * cq + _swap_halves(dq * sq, kw)).astype(BF16)
        rout[1][...] = (dk_ * ck + _swap_halves(dk_ * sk, kw)).astype(BF16)
        rout[2][...] = (rin[8][...] + rin[9][...]).astype(BF16)
        dzl = jnp.zeros(zl.shape, F32)
        for t, w in ((0, cin[0]), (1, cin[1])):
            la = rin[6 + t][...]
            da = rin[4 + t][...] * ((1.0 - jnp.exp(la * GLA_TAU)) * (1.0 / GLA_TAU))
            dzl = dzl + _nt(da, w[...])
            acc[t][...] += _tn(zl, da)
            acc[2 + t][...] += _colsum(da)
        rout[3][...] = dzl.astype(BF16)

    row_ins = [(gf[0], kw, 0, 0), (gb[0], kw, 0, 0), (gf[1], kw, 0, 0), (gb[1], kw, 0, 0),
               (gf[3], kw, 0, 0), (gb[3], kw, 0, 0), (la_f, kw, ro, 0), (la_b, kw, ro, 0),
               (gf[2], vw, 0, 0), (gb[2], vw, 0, 0), (z, LANES, ro, cfg["L0"] // LANES)]
    row_ins += [(t, LANES, ro, 0) for t in tabs]
    return _rows_call(body, name=name, nblk=nrows // tr, tr=tr, row_ins=row_ins, consts=[wdf_pad, wdb_pad],
                      row_outs=[(nrows, kw, BF16), (nrows, kw, BF16), (nrows, vw, BF16), (nrows, LANES, BF16)],
                      accs=[(LANES, kw), (LANES, kw), (1, kw), (1, kw)])


def _readout_fwd(o_f, o_b, z, g, cfg, *, tr):
    n, vw = o_f.shape
    h, dv = cfg["H"], cfg["DV"]

    def body(rin, cin, rout, acc):
        for hh in range(h):
            cs = slice(hh * dv, (hh + 1) * dv)
            oh = rin[0][:, cs] + rin[1][:, cs]
            y = oh * _rstd(oh) * cin[0][:, cs]
            rout[0][:, cs] = (y * _silu(rin[2][:, cs])).astype(BF16)

    return _rows_call(body, name="gla_readout", nblk=n // tr, tr=tr,
                      row_ins=[(o_f, vw, 0, 0), (o_b, vw, 0, 0), (z, vw, 0, cfg["R0"] // vw)], consts=[g],
                      row_outs=[(n, vw, BF16)])[0]


def _readout_bwd(o_f, o_b, z, dycat, g, cfg, *, tr):
    n, vw = o_f.shape
    h, dv = cfg["H"], cfg["DV"]

    def body(rin, cin, rout, acc):
        for hh in range(h):
            cs = slice(hh * dv, (hh + 1) * dv)
            oh = rin[0][:, cs] + rin[1][:, cs]
            r, dyg, gh = rin[2][:, cs], rin[3][:, cs], cin[0][:, cs]
            rs = _rstd(oh)
            dy = dyg * _silu(r)
            rout[0][:, cs] = _rms_bwd(oh, rs, dy * gh).astype(BF16)
            rout[1][:, cs] = (dyg * (oh * rs * gh) * _dsilu(r)).astype(BF16)
            acc[0][:, cs] += _colsum(dy * oh * rs)

    return _rows_call(body, name="gla_readout_bwd", nblk=n // tr, tr=tr,
                      row_ins=[(o_f, vw, 0, 0), (o_b, vw, 0, 0), (z, vw, 0, cfg["R0"] // vw), (dycat, vw, 0, 0)],
                      consts=[g], row_outs=[(n, vw, BF16), (n, vw, BF16)], accs=[(1, vw)])


def _sg_ln(vv):
    mu = jnp.mean(vv, axis=-1, keepdims=True)
    cen = vv - mu
    rstd = lax.rsqrt(jnp.mean(cen * cen, axis=-1, keepdims=True) + EPS)
    return cen * rstd, rstd


def _sg_fwd(z, n, lng, lnb, w_s, bs_full, cfg):
    sgw, grp, sc = cfg["SGW"], cfg["SG_G"], cfg["SG_C"]
    gw = sgw // grp

    def body(rin, cin, rout, acc):
        u = _gelu(rin[0][...])
        xhat, _ = _sg_ln(_gelu(rin[1][...]))
        vvn = xhat * cin[0][...] + cin[1][...]
        for gg in range(grp):
            cs = slice(gg * gw, (gg + 1) * gw)
            s = _nn(cin[2][gg], vvn[:, cs]) + cin[3][:, cs]
            rout[0][:, cs] = (u[:, cs] * s).astype(BF16)

    return _rows_call(body, name="sg_fwd", nblk=n // sc, tr=sc,
                      row_ins=[(z, sgw, 0, cfg["U0"] // sgw), (z, sgw, 0, cfg["VV0"] // sgw)],
                      consts=[lng, lnb, w_s, bs_full], row_outs=[(n, sgw, BF16)])[0]


def _sg_bwd(z, dycat, n, lng, lnb, w_s, bs_full, cfg):
    sgw, grp, sc = cfg["SGW"], cfg["SG_G"], cfg["SG_C"]
    gw = sgw // grp

    def body(rin, cin, rout, acc):
        up, vp, dy = rin[0][...], rin[1][...], rin[2][...]
        u = _gelu(up)
        xhat, rstd = _sg_ln(_gelu(vp))
        lng_v = cin[0][...]
        vvn = xhat * lng_v + cin[1][...]
        ds = dy * u
        acc[1][...] += ds
        dvvn_parts = []
        for gg in range(grp):
            cs = slice(gg * gw, (gg + 1) * gw)
            w = cin[2][gg]
            s = _nn(w, vvn[:, cs]) + cin[3][:, cs]
            rout[0][:, cs] = (dy[:, cs] * s * _dgelu(up[:, cs])).astype(BF16)
            acc[0][gg] += _nt(ds[:, cs], vvn[:, cs])
            dvvn_parts.append(_tn(w, ds[:, cs]))
        dvvn = jnp.concatenate(dvvn_parts, axis=1)
        acc[2][...] += _colsum(dvvn * xhat)
        acc[3][...] += _colsum(dvvn)
        dxh = dvvn * lng_v
        dvv = rstd * (dxh - jnp.mean(dxh, axis=-1, keepdims=True)
                      - xhat * jnp.mean(dxh * xhat, axis=-1, keepdims=True))
        rout[1][...] = (dvv * _dgelu(vp)).astype(BF16)

    vw = cfg["VW"]
    return _rows_call(body, name="sg_bwd", nblk=n // sc, tr=sc,
                      row_ins=[(z, sgw, 0, cfg["U0"] // sgw), (z, sgw, 0, cfg["VV0"] // sgw),
                               (dycat, sgw, 0, vw // sgw)],
                      consts=[lng, lnb, w_s, bs_full], row_outs=[(n, sgw, BF16), (n, sgw, BF16)],
                      accs=[(grp, sc, sc), (sc, sgw), (1, sgw), (1, sgw)])


def _mid_fwd(x, mix, g1, post1, pre2, sh2, sc2, *, tr):
    n, d = x.shape

    def body(rin, cin, rout, acc):
        xv, mv = rin[0][...], rin[1][...]
        x1 = xv + cin[0][...] * (mv * _rstd(mv) * cin[1][...])
        rout[0][...] = x1
        n2 = x1 * _rstd(x1) * cin[2][...]
        rout[1][...] = (n2 * (1.0 + cin[4][...]) + cin[3][...]).astype(BF16)

    return _rows_call(body, name="mid_fwd", nblk=n // tr, tr=tr, row_ins=[(x, d, 0, 0), (mix, d, 0, 0)],
                      consts=[g1, post1, pre2, sh2, sc2], row_outs=[(n, d, F32), (n, d, BF16)])


def _head_bwd(x1, m2, target, g2, post2, *, tr):
    n, d = x1.shape

    def body(rin, cin, rout, acc):
        x1v, mv, tv = rin[0][...], rin[1][...], rin[2][...]
        g2v, pg = cin[0][...], cin[1][...]
        r = _rstd(mv)
        y2 = mv * r * pg
        err = (x1v + g2v * y2) - tv
        acc[2][...] += _colsum(err * err) * (0.5 / d)
        dx2 = err * (1.0 / d)
        rout[0][...] = dx2
        dy2 = dx2 * g2v
        acc[0][...] += _colsum(dx2 * y2)
        acc[1][...] += _colsum(dy2 * mv * r)
        rout[1][...] = _rms_bwd(mv, r, dy2 * pg).astype(BF16)

    return _rows_call(body, name="head_bwd", nblk=n // tr, tr=tr,
                      row_ins=[(x1, d, 0, 0), (m2, d, 0, 0), (target, d, 0, 0)], consts=[g2, post2],
                      row_outs=[(n, d, F32), (n, d, BF16)], accs=[(1, d)] * 3)


def _mid_bwd(dh2, x1, dx2, mix, sc2, pre2, g1, post1, *, tr):
    n, d = x1.shape

    def body(rin, cin, rout, acc):
        dh, x1v, dx2v, mv = rin[0][...], rin[1][...], rin[2][...], rin[3][...]
        sc2v, pre2v, g1v, post1v = cin[0][...], cin[1][...], cin[2][...], cin[3][...]
        r2 = _rstd(x1v)
        xr = x1v * r2
        acc[0][...] += _colsum(dh)
        acc[1][...] += _colsum(dh * (xr * pre2v))
        dn2 = dh * (1.0 + sc2v)
        acc[2][...] += _colsum(dn2 * xr)
        dx1 = dx2v + _rms_bwd(x1v, r2, dn2 * pre2v)
        rout[0][...] = dx1
        r1 = _rstd(mv)
        mr = mv * r1
        acc[3][...] += _colsum(dx1 * (mr * post1v))
        dy1 = dx1 * g1v
        acc[4][...] += _colsum(dy1 * mr)
        rout[1][...] = _rms_bwd(mv, r1, dy1 * post1v).astype(BF16)

    return _rows_call(body, name="mid_bwd", nblk=n // tr, tr=tr,
                      row_ins=[(dh2, d, 0, 0), (x1, d, 0, 0), (dx2, d, 0, 0), (mix, d, 0, 0)],
                      consts=[sc2, pre2, g1, post1], row_outs=[(n, d, F32), (n, d, BF16)], accs=[(1, d)] * 5)


def _in_bwd(da, x, dres, sc1, pre1, *, row_off, tr, name, sides=_NO_SIDES):
    n, d = x.shape
    with_res = dres is not None

    def body(rin, cin, rout, acc):
        dav, xv = rin[0][...], rin[1][...]
        sc1v, pre1v = cin[0][...], cin[1][...]
        r = _rstd(xv)
        xr = xv * r
        acc[0][...] += _colsum(dav)
        acc[1][...] += _colsum(dav * (xr * pre1v))
        dn = dav * (1.0 + sc1v)
        acc[2][...] += _colsum(dn * xr)
        if with_res:
            rout[0][...] = rin[2][...] + _rms_bwd(xv, r, dn * pre1v)

    row_ins = [(da, d, row_off // tr, 0), (x, d, 0, 0)] + ([(dres, d, 0, 0)] if with_res else [])
    return _rows_call(body, name=name, nblk=n // tr, tr=tr, row_ins=row_ins, consts=[sc1, pre1],
                      row_outs=[(n, d, F32)] if with_res else [], accs=[(1, d)] * 3, sides=sides)


def _ada_fwd(c16, w, b, *, tn):
    d, ncol = w.shape

    def body(c_ref, w_ref, b_ref, o_ref):
        o_ref[...] = _nn(_silu(c_ref[...]), w_ref[...]) + b_ref[...]

    return pl.pallas_call(
        body, name="ada_fwd", grid=(ncol // tn,),
        in_specs=[pl.BlockSpec((16, d), lambda j: (0, 0)), pl.BlockSpec((d, tn), lambda j: (0, j)),
                  pl.BlockSpec((1, tn), lambda j: (0, j))],
        out_specs=pl.BlockSpec((16, tn), lambda j: (0, j)),
        out_shape=jax.ShapeDtypeStruct((16, ncol), F32),
        compiler_params=_cparams(("arbitrary",)),
    )(c16, w, b)


def _ada_bwd(c16, dm, w, c_ctx, *, tn):
    d, ncol = w.shape

    def body(c_ref, dm_ref, w_ref, cc_ref, gw_ref, dcc_ref, acc):
        j = pl.program_id(0)

        @pl.when(j == 0)
        def _():
            acc[...] = jnp.zeros_like(acc)

        gw_ref[...] = _tn(_silu(c_ref[...]), dm_ref[...])
        acc[...] += _nt(dm_ref[...], w_ref[...])

        @pl.when(j == ncol // tn - 1)
        def _():
            dcc_ref[...] = _colsum(acc[8:16, :]) * _dsilu(cc_ref[...])

    return pl.pallas_call(
        body, name="ada_bwd", grid=(ncol // tn,),
        in_specs=[pl.BlockSpec((16, d), lambda j: (0, 0)), pl.BlockSpec((16, tn), lambda j: (0, j)),
                  pl.BlockSpec((d, tn), lambda j: (0, j)), pl.BlockSpec((1, d), lambda j: (0, 0))],
        out_specs=[pl.BlockSpec((d, tn), lambda j: (0, j)), pl.BlockSpec((1, d), lambda j: (0, 0))],
        out_shape=[jax.ShapeDtypeStruct((d, ncol), F32), jax.ShapeDtypeStruct((1, d), F32)],
        scratch_shapes=[pltpu.VMEM((16, d), F32)],
        compiler_params=_cparams(("arbitrary",)),
    )(c16, dm, w, c_ctx)


def _adam_math(w, g, m, v):
    m = ADAM_B1 * m + (1.0 - ADAM_B1) * g
    v = ADAM_B2 * v + (1.0 - ADAM_B2) * (g * g)
    m_hat = m / (1.0 - ADAM_B1 ** ADAM_STEP)
    v_hat = v / (1.0 - ADAM_B2 ** ADAM_STEP)
    delta = -ADAM_LR * (m_hat / (jnp.sqrt(v_hat) + ADAM_EPS) + ADAM_WD * w)
    return delta, m, v


def _adam_big(parts, w, m, v, *, name, tr, sides=_NO_SIDES):
    rows, cols = w.shape
    n_p = len(parts)
    n_si, n_so = len(sides.ins), len(sides.outs)
    nblk = rows // tr

    def body(*refs):
        ins, refs = refs[:n_p + 3], refs[n_p + 3:]
        side_in, refs = refs[:n_si], refs[n_si:]
        outs, side_out, side_sems = refs[:4], refs[4:4 + n_so], refs[4 + n_so:]
        i = pl.program_id(0)

        if sides.jobs:
            @pl.when(i == 0)
            def _():
                sides.run("start", side_in, side_out, side_sems)

        g = ins[0][...]
        for p in ins[1:n_p]:
            g = g + p[...].astype(F32)
        delta, m2, v2 = _adam_math(ins[n_p][...], g, ins[n_p + 1][...], ins[n_p + 2][...])
        outs[0][...] = g
        outs[1][...] = delta
        outs[2][...] = m2
        outs[3][...] = v2

        if sides.jobs:
            @pl.when(i == nblk - 1)
            def _():
                sides.run("finish", side_in, side_out, side_sems)

    plain = pl.BlockSpec((tr, cols), lambda i: (i, 0))
    in_specs = []
    for arr, idx in parts:
        if idx is None:
            in_specs.append(plain)
        else:
            in_specs.append(pl.BlockSpec((None, tr, cols), lambda i, idx=idx: (idx, i, 0)))
    in_specs += [plain] * 3
    return pl.pallas_call(
        body, name=name, grid=(nblk,), in_specs=in_specs + [ANY_SPEC] * n_si,
        out_specs=[plain] * 4 + [ANY_SPEC] * n_so,
        out_shape=[jax.ShapeDtypeStruct((rows, cols), F32)] * 4 + list(sides.outs),
        scratch_shapes=list(sides.sems), input_output_aliases=sides.aliases(n_p + 3, 4),
        compiler_params=_cparams(("arbitrary",) if sides.jobs else ("parallel",)),
    )(*[p[0] for p in parts], w, m, v, *sides.ins)


def _adam_small(g8, w, m, v):
    def body(g_ref, w_ref, m_ref, v_ref, go, do, mo, vo):
        g = g_ref[0]
        for r in range(1, N_DEV):
            g = g + g_ref[r]
        delta, m2, v2 = _adam_math(w_ref[...], g, m_ref[...], v_ref[...])
        go[...] = g
        do[...] = delta
        mo[...] = m2
        vo[...] = v2

    return pl.pallas_call(
        body, name="adam_small", out_shape=[jax.ShapeDtypeStruct(w.shape, F32)] * 4,
        compiler_params=pltpu.CompilerParams(vmem_limit_bytes=VMEM_LIMIT),
    )(g8, w, m, v)


VEC_W = 1024
TK = 2048
ELEMS_PER_BLOCK = 256 * 1024


def _dense(v):
    a, k = v.shape
    kp = -(-k // (8 * VEC_W)) * (8 * VEC_W)
    return jnp.pad(v, ((0, 0), (0, kp - k))).reshape(a, kp // VEC_W, VEC_W)


def _all_gather_vec(v, *, name):
    k = v.shape[1]
    return _all_gather_small(_dense(v)[0], name=name).reshape(N_DEV, -1)[:, :k]


def _my_pos():
    return lax.axis_index("x"), lax.axis_index("y"), lax.axis_index("c")


def _flip(v, bit):
    return (1 - v) if bit else v


def _all_gather_small(v, *, name):
    r, k = v.shape

    def body(v_ref, out_ref, send, recv, lsem):
        x, y, c = _my_pos()
        me = 4 * x + 2 * y + c
        local = pltpu.make_async_copy(v_ref, out_ref.at[me], lsem)
        local.start()
        sends = []
        for kk in range(1, N_DEV):
            peer = (_flip(x, kk & 4), _flip(y, kk & 2), _flip(c, kk & 1))
            cp = pltpu.make_async_remote_copy(src_ref=v_ref, dst_ref=out_ref.at[me], send_sem=send.at[kk - 1],
                                              recv_sem=recv.at[kk - 1], device_id=peer, device_id_type=MESH)
            cp.start()
            sends.append(cp)
        for kk in range(1, N_DEV):
            px, py, pc = _flip(x, kk & 4), _flip(y, kk & 2), _flip(c, kk & 1)
            src = 4 * px + 2 * py + pc
            pltpu.make_async_remote_copy(src_ref=v_ref, dst_ref=out_ref.at[src], send_sem=send.at[kk - 1],
                                         recv_sem=recv.at[kk - 1], device_id=(px, py, pc),
                                         device_id_type=MESH).wait_recv()
        for cp in sends:
            cp.wait_send()
        local.wait()

    return pl.pallas_call(
        body, name=name, out_shape=jax.ShapeDtypeStruct((N_DEV, r, k), v.dtype),
        in_specs=[pl.BlockSpec(memory_space=pltpu.VMEM)], out_specs=pl.BlockSpec(memory_space=pltpu.VMEM),
        scratch_shapes=[pltpu.SemaphoreType.DMA((N_DEV - 1,)), pltpu.SemaphoreType.DMA((N_DEV - 1,)),
                        pltpu.SemaphoreType.DMA],
        compiler_params=pltpu.CompilerParams(vmem_limit_bytes=VMEM_LIMIT),
    )(v)


def _ag_job(shards, rows=None, chained=None):
    n_arr = len(shards)

    def part(ref):
        return ref if rows is None else ref.at[pl.ds(rows[0], rows[1])]

    def tools(ins, outs, sems):
        send, recv, lsem = sems
        x, y, c = _my_pos()
        chips = [(1 - x, y), (x, 1 - y), (1 - x, 1 - y)]

        def copy(a, kk, block, to, src=None):
            dst = part(outs[a].at[4 * block[0] + 2 * block[1] + block[2]])
            return pltpu.make_async_remote_copy(src_ref=dst if src is None else part(src), dst_ref=dst,
                                                send_sem=send.at[a, kk], recv_sem=recv.at[a, kk],
                                                device_id=to, device_id_type=MESH)

        locals_ = [pltpu.make_async_copy(part(ins[a]), part(outs[a].at[4 * x + 2 * y + c]), lsem.at[a])
                   for a in range(n_arr)]
        firsts = []
        for a in range(n_arr):
            firsts.append(copy(a, 0, (x, y, c), (x, y, 1 - c), src=ins[a]))
            firsts += [copy(a, 1 + j, (x, y, c), (*chip, c), src=ins[a]) for j, chip in enumerate(chips)]
        return copy, locals_, firsts, chips, (x, y, c)

    def start(ins, outs, sems):
        _, locals_, firsts, _, _ = tools(ins, outs, sems)
        for cp in locals_ + firsts:
            cp.start()

    def finish(ins, outs, sems):
        copy, locals_, firsts, chips, (x, y, c) = tools(ins, outs, sems)
        me, sibling = (x, y, c), (x, y, 1 - c)
        passed = []
        for a in range(n_arr):
            for j, chip in enumerate(chips):
                copy(a, 1 + j, (*chip, c), me).wait_recv()
                fw = copy(a, 4 + j, (*chip, c), sibling)
                fw.start()
                passed.append(fw)
        for a in range(n_arr):
            copy(a, 0, sibling, me).wait_recv()
            for j, chip in enumerate(chips):
                copy(a, 4 + j, (*chip, 1 - c), me).wait_recv()
        for cp in firsts + passed:
            cp.wait_send()
        for lc in locals_:
            lc.wait()

    job = dict(ins=list(shards), outs=[jax.ShapeDtypeStruct((N_DEV,) + s.shape, s.dtype) for s in shards],
               sems=[pltpu.SemaphoreType.DMA((n_arr, 7)), pltpu.SemaphoreType.DMA((n_arr, 7)),
                     pltpu.SemaphoreType.DMA((n_arr,))], start=start, finish=finish)
    if chained is not None:
        job["ins"] = list(shards) + list(chained)
        job["alias"] = {n_arr + a: a for a in range(n_arr)}
    return job


def _exchange_job(arrays, n_slots, out_slots, src_of, dst_of, peer_of, rows=None, chained=None):
    n_arr = len(arrays)

    def copies(ins, outs, sems):
        send, recv = sems
        x, y, c = _my_pos()
        res = []
        for a in range(n_arr):
            for s in range(n_slots):
                src, dst = ins[a].at[src_of(s, x, y, c)], outs[a].at[dst_of(s)]
                if rows is not None:
                    src, dst = src.at[pl.ds(rows[0], rows[1])], dst.at[pl.ds(rows[0], rows[1])]
                res.append(pltpu.make_async_remote_copy(
                    src_ref=src, dst_ref=dst, send_sem=send.at[a, s], recv_sem=recv.at[a, s],
                    device_id=peer_of(s, x, y, c), device_id_type=MESH))
        return res

    def start(ins, outs, sems):
        for cp in copies(ins, outs, sems):
            cp.start()

    def finish(ins, outs, sems):
        cps = copies(ins, outs, sems)
        for cp in cps:
            cp.wait_recv()
        for cp in cps:
            cp.wait_send()

    job = dict(ins=list(arrays), outs=[jax.ShapeDtypeStruct((out_slots,) + g.shape[1:], g.dtype) for g in arrays],
               sems=[pltpu.SemaphoreType.DMA((n_arr, n_slots)), pltpu.SemaphoreType.DMA((n_arr, n_slots))],
               start=start, finish=finish)
    if chained is not None:
        job["ins"] = list(arrays) + list(chained)
        job["alias"] = {n_arr + a: a for a in range(n_arr)}
    return job


def _pair_job(grads):
    return _exchange_job(
        grads, 4, 4,
        src_of=lambda s, x, y, c: 4 * _flip(x, s & 2) + 2 * _flip(y, s & 1) + (1 - c),
        dst_of=lambda s: s, peer_of=lambda s, x, y, c: (x, y, 1 - c))


def _chip_job(sums, rows=None, chained=None):
    return _exchange_job(
        sums, 3, 3, src_of=lambda s, x, y, c: s, dst_of=lambda s: s,
        peer_of=lambda s, x, y, c: (_flip(x, (s + 1) & 2), _flip(y, (s + 1) & 1), c), rows=rows, chained=chained)


def _run_sides(sides, *, name):
    n_si, n_so = len(sides.ins), len(sides.outs)

    def body(*refs):
        ins, outs, sems = refs[:n_si], refs[n_si:n_si + n_so], refs[n_si + n_so:]
        sides.run("start", ins, outs, sems)
        sides.run("finish", ins, outs, sems)

    return pl.pallas_call(
        body, name=name, out_shape=list(sides.outs), in_specs=[ANY_SPEC] * n_si, out_specs=[ANY_SPEC] * n_so,
        scratch_shapes=list(sides.sems), input_output_aliases=sides.aliases(0, 0),
    )(*sides.ins)


def _pair_add(g, t, *, name, tr, wire):
    _, r, cols = g.shape
    g4 = g.reshape(4, 2, r, cols)
    j0 = 1 if wire else 0

    def g_index(j, i):
        x, y, c = _my_pos()
        return (jnp.bitwise_xor(2 * x + y, j + j0), c, i, 0)

    def body(g_ref, t_ref, o_ref):
        o_ref[...] = (g_ref[...] + t_ref[...]).astype(o_ref.dtype)

    return pl.pallas_call(
        body, name=name, grid=(3 if wire else 1, r // tr),
        in_specs=[pl.BlockSpec((None, None, tr, cols), g_index),
                  pl.BlockSpec((None, tr, cols), lambda j, i: (j + j0, i, 0))],
        out_specs=pl.BlockSpec((None, tr, cols), lambda j, i: (j, i, 0)),
        out_shape=jax.ShapeDtypeStruct((3 if wire else 1, r, cols), BF16 if wire else F32),
        compiler_params=_cparams(("arbitrary", "arbitrary")),
    )(g4, t)


def _config(x, ctx, w_in, w_dec_f, gla_norm_g, sg_ln_g, w_s):
    n, d = x.shape[1], x.shape[2]
    tc = ctx.shape[1]
    h = gla_norm_g.shape[1]
    dv = gla_norm_g.shape[2] * N_DEV
    dk = dv // 2
    kw, vw = h * dk, h * dv
    lr = w_dec_f.shape[1]
    sgw = sg_ln_g.shape[1]
    cfg = dict(N=n, D=d, TC=tc, H=h, DV=dv, DK=dk, KW=kw, VW=vw, LR=lr, SGW=sgw, SG_G=w_s.shape[1],
               SG_C=w_s.shape[2], IN=w_in.shape[2] * N_DEV)
    cfg.update(K0=kw, V0=2 * kw, R0=2 * kw + vw, U0=2 * kw + 2 * vw)
    cfg.update(VV0=cfg["U0"] + sgw, L0=cfg["U0"] + 2 * sgw, ZW=cfg["U0"] + 2 * sgw + LANES)
    assert dk == LANES and vw == 2 * kw and 2 * lr <= LANES
    assert cfg["R0"] % vw == 0 and cfg["U0"] % sgw == 0 and cfg["VV0"] % sgw == 0 and vw % sgw == 0
    assert cfg["IN"] == 2 * kw + 2 * vw + 2 * lr + 2 * sgw
    return cfg


def _rope_tables(cfg):
    n, tc, dk = cfg["N"], cfg["TC"], cfg["DK"]
    m = dk // 4
    pos = jnp.arange(n)
    inv = ROPE_BASE ** (-jnp.arange(m, dtype=F32) / m)
    ang_r = (pos // GRID_W).astype(F32)[:, None] * inv[None, :]
    ang_c = (pos % GRID_W).astype(F32)[:, None] * inv[None, :]
    cos = jnp.concatenate([jnp.cos(ang_r)] * 2 + [jnp.cos(ang_c)] * 2, axis=1)
    sin = jnp.concatenate([-jnp.sin(ang_r), jnp.sin(ang_r), -jnp.sin(ang_c), jnp.sin(ang_c)], axis=1)
    scale = dk ** -0.5
    z = jnp.zeros((tc, dk), F32)
    one = jnp.ones((tc, dk), F32)
    return [jnp.concatenate([cos * scale, z]), jnp.concatenate([sin * scale, z]),
            jnp.concatenate([cos, one]), jnp.concatenate([sin, z])]


def _w_in_splits(d):
    return (d * 11 // 32) // 16 * 16, d * 3 // 4


def _pair_sums(g, t, nm):
    rows_for = _tile(g.shape[1], max(8, ELEMS_PER_BLOCK // g.shape[2]), 16)
    return (_pair_add(g, t, name="rs_own_" + nm, tr=rows_for, wire=False),
            _pair_add(g, t, name="rs_wire_" + nm, tr=rows_for, wire=True))


def _local_step(x, ctx, target, mods, c_mods, w, cfg):
    n, d, tc = cfg["N"], cfg["D"], cfg["TC"]
    kw, vw, sgw, zw, lr = cfg["KW"], cfg["VW"], cfg["SGW"], cfg["ZW"], cfg["LR"]
    sh1, sc1, g1, sh2, sc2, g2 = mods
    csh1, csc1 = c_mods
    rt = n + tc
    tb = math.gcd(256, math.gcd(n, tc))
    tr = math.gcd(128, tb)
    tr_s = math.gcd(64, tb)
    fs = w["sh_1"].shape[1]
    ff = fs * N_DEV
    cs_in = w["sh_in"].shape[1]

    wg_in = _run_sides(_Sides([_ag_job([w["sh_in"]])]), name="ag_w_in")[0]
    w_full = wg_in.transpose(1, 0, 2).reshape(d, cfg["IN"])
    lf0 = 2 * kw + 2 * vw
    sg0 = lf0 + 2 * lr
    w_pad = jnp.concatenate([w_full[:, :lf0], w_full[:, sg0:], w_full[:, lf0:sg0],
                             jnp.zeros((d, LANES - 2 * lr), BF16)], axis=1)

    hx = _norm_mod(x, w["pre1_g"], sh1, sc1, name="in_norm_x", tr=tr)
    hc = _norm_mod(ctx, w["pre1_g"], csh1, csc1, name="in_norm_ctx", tr=tr)
    a_all = jnp.concatenate([hx, hc], axis=0)

    tm_a = _tile(rt, 1152, 16)
    z, wg_o, w_1 = _matmul(a_all, w_pad, "nn", rt, zw, d, tm=tm_a, tn=_tile(zw, 1152, LANES),
                           tk=_tile(d, TK, LANES), name="mm_in", out_shapes=[jax.ShapeDtypeStruct((rt, zw), F32)],
                           sides=_Sides([_ag_job([w["sh_o"]]), _ag_job([w["sh_1"]], rows=(0, d // 2))]))
    w_o = wg_o.reshape(d, d)

    tabs = _rope_tables(cfg)
    qr, kr, la_f, la_b = _gla_prep(z, tabs, w["wdf_pad"], w["wdb_pad"], w["b_dec_f"], w["b_dec_b"], cfg, tr=tr)

    zero_st = jnp.zeros((cfg["H"], cfg["DV"], cfg["DK"]), F32)
    gla = dict(cfg=cfg, tb=tb)
    _, save_cf, st_cf = _gla_fwd(qr, kr, z, la_f, zero_st, rev=False, row_off=n, nrows=tc, name="gla_ctx_f", **gla)
    _, save_cb, st_cb = _gla_fwd(qr, kr, z, la_b, zero_st, rev=True, row_off=n, nrows=tc, name="gla_ctx_b", **gla)
    o_f, save_f, _, w_1 = _gla_fwd(
        qr, kr, z, la_f, st_cf, rev=False, row_off=0, nrows=n, name="gla_f",
        sides=_Sides([_ag_job([w["sh_1"]], rows=(d // 2, d // 4), chained=[w_1])]), **gla)
    o_b, save_b, _, w_1 = _gla_fwd(
        qr, kr, z, la_b, st_cb, rev=True, row_off=0, nrows=n, name="gla_b",
        sides=_Sides([_ag_job([w["sh_1"]], rows=(3 * d // 4, d // 4), chained=[w_1])]), **gla)
    y_gla = _readout_fwd(o_f, o_b, z, w["gla_g"], cfg, tr=tr)
    y_sg = _sg_fwd(z, n, w["sg_ln_g"], w["sg_ln_b"], w["w_s"], w["bs_full"], cfg)
    ycat = jnp.concatenate([y_gla, y_sg], axis=1)

    tm_n = _tile(n, 1024, 16)
    mix = _matmul(ycat, w_o, "nn", n, d, d, tm=tm_n, tn=_tile(d, 1024, LANES), tk=_tile(d, TK, LANES),
                  name="mm_o", out_shapes=[jax.ShapeDtypeStruct((n, d), F32)])[0]
    x1, h2 = _mid_fwd(x, mix, g1, w["post1_g"], w["pre2_g"], sh2, sc2, tr=tr_s)

    tn_f = _tile(fs, 1024, LANES)
    tk_d = _tile(d, TK, LANES)

    def relu2(acc):
        return acc, jnp.square(jnp.maximum(acc, 0.0))

    a1, p1, wg_2 = _matmul(h2, w_1, "nn", n, ff, d, tm=tm_n, tn=tn_f, tk=tk_d, name="mm_1",
                           b_spec=_blocked_b_nn(fs, tk_d, tn_f), epilogue=relu2,
                           out_shapes=[jax.ShapeDtypeStruct((n, ff), BF16)] * 2,
                           sides=_Sides([_ag_job([w["sh_2"]])]))
    w_2 = wg_2.reshape(ff, d)
    tk_f = _tile(ff, TK, LANES)
    m2 = _matmul(p1, w_2, "nn", n, d, ff, tm=tm_n, tn=_tile(d, 1024, LANES), tk=tk_f, name="mm_2",
                 out_shapes=[jax.ShapeDtypeStruct((n, d), F32)])[0]

    dx2, dm2, dg2, dpost2, lossc = _head_bwd(x1, m2, target, g2, w["post2_g"], tr=tr_s)

    def drelu2(acc, a):
        return (acc * (2.0 * jnp.maximum(a.astype(F32), 0.0)),)

    da1 = _matmul(dm2, w_2, "nt", n, ff, d, tm=tm_n, tn=_tile(ff, 1024, LANES), tk=tk_d, name="mm_2_dx",
                  epilogue=drelu2, extras=(a1,), out_shapes=[jax.ShapeDtypeStruct((n, ff), BF16)])[0]
    tk_n = _tile(n, TK, 16)
    tm_d = _tile(d, 1024, LANES)
    g_1 = _matmul(h2, da1, "tn", d, ff, n, tm=tm_d, tn=tn_f, tk=tk_n, name="mm_1_dw",
                  out_specs=[_blocked_out(fs, tm_d, tn_f)],
                  out_shapes=[jax.ShapeDtypeStruct((N_DEV, d, fs), F32)])[0]
    dw_2, t_1 = _matmul(p1, dm2, "tn", ff, d, n, tm=_tile(ff, 1024, LANES), tn=_tile(d, 1024, LANES), tk=tk_n,
                        name="mm_2_dw", out_shapes=[jax.ShapeDtypeStruct((ff, d), F32)],
                        sides=_Sides([_pair_job([g_1])]))
    g_2 = dw_2.reshape(N_DEV, fs, d)
    p1_own, p1_wire = _pair_sums(g_1, t_1, "w_1")
    tk_fs = _tile(fs, TK, LANES)
    dh2, u_1, t_2 = _matmul(da1, w_1, "nt", n, d, ff, tm=tm_n, tn=_tile(d, 1024, LANES), tk=tk_fs, name="mm_1_dx",
                            b_spec=_blocked_b_nt(fs, _tile(d, 1024, LANES), tk_fs),
                            out_shapes=[jax.ShapeDtypeStruct((n, d), F32)],
                            sides=_Sides([_chip_job([p1_wire]), _pair_job([g_2])]))
    p2_own, p2_wire = _pair_sums(g_2, t_2, "w_2")
    rows_a = (fs * 5 // 16) // 16 * 16

    dx1, dmix, dsh2, dsc2, dpre2, dg1, dpost1 = _mid_bwd(dh2, x1, dx2, mix, sc2, w["pre2_g"], g1, w["post1_g"],
                                                         tr=tr_s)
    dw_o, u_2 = _matmul(ycat, dmix, "tn", d, d, n, tm=tm_d, tn=_tile(d, 1024, LANES), tk=tk_n, name="mm_o_dw",
                        out_shapes=[jax.ShapeDtypeStruct((d, d), F32)],
                        sides=_Sides([_chip_job([p2_wire], rows=(0, rows_a))]))
    g_o = dw_o.reshape(N_DEV, d // N_DEV, d)
    dycat, t_o = _matmul(dmix, w_o, "nt", n, d, d, tm=tm_n, tn=_tile(d, 1024, LANES), tk=tk_d, name="mm_o_dx",
                         out_shapes=[jax.ShapeDtypeStruct((n, d), F32)], sides=_Sides([_pair_job([g_o])]))
    po_own, po_wire = _pair_sums(g_o, t_o, "w_o")

    dzu, dzvv, dws, dbs_acc, dlng, dlnb = _sg_bwd(z, dycat, n, w["sg_ln_g"], w["sg_ln_b"], w["w_s"],
                                                  w["bs_full"], cfg)
    do, dzr, dgla_g = _readout_bwd(o_f, o_b, z, dycat, w["gla_g"], cfg, tr=tr)

    gf = _gla_bwd(qr, kr, z, la_f, do, save_f, zero_st, rev=False, row_off=0, nrows=n, name="gla_f_bwd", **gla)
    gb = _gla_bwd(qr, kr, z, la_b, do, save_b, zero_st, rev=True, row_off=0, nrows=n, name="gla_b_bwd", **gla)
    do_c = jnp.zeros((tc, vw), BF16)
    gcf = _gla_bwd(qr, kr, z, la_f, do_c, save_cf, gf[4], rev=False, row_off=n, nrows=tc, name="gla_ctx_f_bwd",
                   **gla)
    gcb = _gla_bwd(qr, kr, z, la_b, do_c, save_cb, gb[4], rev=True, row_off=n, nrows=tc, name="gla_ctx_b_bwd",
                   **gla)

    post = dict(la_f=la_f, la_b=la_b, z=z, tabs=tabs, wdf_pad=w["wdf_pad"], wdb_pad=w["wdb_pad"], cfg=cfg, tr=tr)
    dzq, dzk, dzv, dzl, dwdf, dwdb, dbdf, dbdb = _gla_post(gf, gb, row_off=0, nrows=n, name="gla_post", **post)
    czq, czk, czv, czl, cwdf, cwdb, cbdf, cbdb = _gla_post(gcf, gcb, row_off=n, nrows=tc, name="gla_post_ctx",
                                                           **post)
    zc = lambda wd: jnp.zeros((tc, wd), BF16)
    dz = jnp.concatenate([
        jnp.concatenate([dzq, dzk, dzv, dzr, dzu, dzvv, dzl], axis=1),
        jnp.concatenate([czq, czk, czv, zc(vw), zc(sgw), zc(sgw), czl], axis=1)], axis=0)

    dw_pad, u_2 = _matmul(a_all, dz, "tn", d, zw, rt, tm=tm_d, tn=_tile(zw, 1152, LANES), tk=_tile(rt, 1152, 16),
                          name="mm_in_dw", out_shapes=[jax.ShapeDtypeStruct((d, zw), F32)],
                          sides=_Sides([_chip_job([p2_wire], rows=(rows_a, fs - rows_a), chained=[u_2])]))
    dw_in = jnp.concatenate([dw_pad[:, :cfg["U0"]], dw_pad[:, cfg["L0"]:cfg["L0"] + 2 * lr],
                             dw_pad[:, cfg["U0"]:cfg["L0"]]], axis=1)
    g_in = dw_in.reshape(d, N_DEV, cs_in).transpose(1, 0, 2)
    da_all, u_o, t_in = _matmul(dz, w_pad, "nt", rt, d, zw, tm=tm_a, tn=_tile(d, 1024, LANES),
                                tk=_tile(zw, 1152, LANES), name="mm_in_dx",
                                out_shapes=[jax.ShapeDtypeStruct((rt, d), F32)],
                                sides=_Sides([_chip_job([po_wire]), _pair_job([g_in])]))
    pin_own, pin_wire = _pair_sums(g_in, t_in, "w_in")

    grad_x, dsh1, dsc1, dpre1, u_in = _in_bwd(
        da_all, x, dx1, sc1, w["pre1_g"], row_off=0, tr=tr_s, name="in_bwd_x",
        sides=_Sides([_chip_job([pin_wire], rows=(0, _w_in_splits(d)[0]))]))
    dcsh1, dcsc1, dpre1_c = _in_bwd(da_all, ctx, None, csc1, w["pre1_g"], row_off=n, tr=tr_s, name="in_bwd_ctx")

    small = dict(
        pre1_g=dpre1 + dpre1_c, post1_g=dpost1, pre2_g=dpre2, post2_g=dpost2,
        w_dec_f=(dwdf + cwdf)[:lr], w_dec_b=(dwdb + cwdb)[lr:2 * lr], b_dec_f=dbdf + cbdf, b_dec_b=dbdb + cbdb,
        gla_norm_g=dgla_g, sg_ln_g=dlng, sg_ln_b=dlnb, w_s=dws,
        b_s=dbs_acc.reshape(cfg["SG_C"], cfg["SG_G"], sgw // cfg["SG_G"]).sum(-1).T)
    dmod = jnp.concatenate([dsh1, dsc1, dg1, dsh2, dsc2, dg2], axis=1)
    dmod_c = jnp.concatenate([dcsh1, dcsc1], axis=1)
    big = dict(w_in=(pin_own, u_in, pin_wire), w_o=(po_own, u_o), w_1=(p1_own, u_1), w_2=(p2_own, u_2))
    return lossc, grad_x, big, small, dmod, dmod_c


SMALL_NAMES = ["b_ada", "pre1_g", "post1_g", "pre2_g", "post2_g", "w_dec_f", "b_dec_f", "w_dec_b", "b_dec_b",
               "gla_norm_g", "sg_ln_g", "sg_ln_b", "w_s", "b_s", "c_ctx"]
WEIGHT_ORDER = ["c_ctx", "w_ada", "b_ada", "pre1_g", "post1_g", "pre2_g", "post2_g", "w_in", "w_dec_f", "b_dec_f",
                "w_dec_b", "b_dec_b", "gla_norm_g", "sg_ln_g", "sg_ln_b", "w_s", "b_s", "w_o", "w_1", "w_2"]


def kernel(x, c, ctx, c_ctx, w_ada, b_ada, pre1_g, post1_g, pre2_g, post2_g, w_in, w_dec_f, b_dec_f, w_dec_b, b_dec_b, gla_norm_g, sg_ln_g, sg_ln_b, w_s, b_s, w_o, w_1, w_2, loss_target, m_c_ctx, m_w_ada, m_b_ada, m_pre1_g, m_post1_g, m_pre2_g, m_post2_g, m_w_in, m_w_dec_f, m_b_dec_f, m_w_dec_b, m_b_dec_b, m_gla_norm_g, m_sg_ln_g, m_sg_ln_b, m_w_s, m_b_s, m_w_o, m_w_1, m_w_2, v_c_ctx, v_w_ada, v_b_ada, v_pre1_g, v_post1_g, v_pre2_g, v_post2_g, v_w_in, v_w_dec_f, v_b_dec_f, v_w_dec_b, v_b_dec_b, v_gla_norm_g, v_sg_ln_g, v_sg_ln_b, v_w_s, v_b_s, v_w_o, v_w_1, v_w_2):
    weights = dict(c_ctx=c_ctx, w_ada=w_ada, b_ada=b_ada, pre1_g=pre1_g, post1_g=post1_g, pre2_g=pre2_g,
                   post2_g=post2_g, w_in=w_in, w_dec_f=w_dec_f, b_dec_f=b_dec_f, w_dec_b=w_dec_b, b_dec_b=b_dec_b,
                   gla_norm_g=gla_norm_g, sg_ln_g=sg_ln_g, sg_ln_b=sg_ln_b, w_s=w_s, b_s=b_s, w_o=w_o, w_1=w_1,
                   w_2=w_2)
    mom_m = dict(c_ctx=m_c_ctx, w_ada=m_w_ada, b_ada=m_b_ada, pre1_g=m_pre1_g, post1_g=m_post1_g, pre2_g=m_pre2_g,
                 post2_g=m_post2_g, w_in=m_w_in, w_dec_f=m_w_dec_f, b_dec_f=m_b_dec_f, w_dec_b=m_w_dec_b,
                 b_dec_b=m_b_dec_b, gla_norm_g=m_gla_norm_g, sg_ln_g=m_sg_ln_g, sg_ln_b=m_sg_ln_b, w_s=m_w_s,
                 b_s=m_b_s, w_o=m_w_o, w_1=m_w_1, w_2=m_w_2)
    mom_v = dict(c_ctx=v_c_ctx, w_ada=v_w_ada, b_ada=v_b_ada, pre1_g=v_pre1_g, post1_g=v_post1_g, pre2_g=v_pre2_g,
                 post2_g=v_post2_g, w_in=v_w_in, w_dec_f=v_w_dec_f, b_dec_f=v_b_dec_f, w_dec_b=v_w_dec_b,
                 b_dec_b=v_b_dec_b, gla_norm_g=v_gla_norm_g, sg_ln_g=v_sg_ln_g, sg_ln_b=v_sg_ln_b, w_s=v_w_s,
                 b_s=v_b_s, w_o=v_w_o, w_1=v_w_1, w_2=v_w_2)

    cfg = _config(x, ctx, w_in, w_dec_f, gla_norm_g, sg_ln_g, w_s)
    n, d, h, dv, kw, vw, lr, sgw = (cfg[k] for k in ("N", "D", "H", "DV", "KW", "VW", "LR", "SGW"))
    dvs, kws = dv // N_DEV, kw // N_DEV
    ix, iy, ic = _my_pos()
    me = 4 * ix + 2 * iy + ic

    pack1 = jnp.concatenate([c.reshape(1, d), w_dec_f.reshape(1, lr * kws), w_dec_b.reshape(1, lr * kws),
                             gla_norm_g.reshape(1, h * dvs)], axis=1)
    g1 = _all_gather_vec(pack1, name="ag_small_in")
    c_all = g1[:, :d]
    o1 = d
    wdf = g1[:, o1:o1 + lr * kws].reshape(N_DEV, lr, kws).transpose(1, 0, 2).reshape(lr, kw)
    o1 += lr * kws
    wdb = g1[:, o1:o1 + lr * kws].reshape(N_DEV, lr, kws).transpose(1, 0, 2).reshape(lr, kw)
    o1 += lr * kws
    gla_g = g1[:, o1:o1 + h * dvs].reshape(N_DEV, h, dvs).transpose(1, 0, 2).reshape(1, h * dv)

    c16 = jnp.concatenate([c_all, jnp.broadcast_to(c_ctx.reshape(1, d), (N_DEV, d))], axis=0)
    ncol = w_ada.shape[2]
    wa = w_ada.reshape(d, ncol)
    b_mine = lax.dynamic_slice(b_ada, (0, me * ncol), (1, ncol))
    tn_ada = _tile(ncol, 512, LANES)
    mod_mine = _ada_fwd(c16, wa, b_mine, tn=tn_ada)
    mod_all = _all_gather_small(mod_mine, name="ag_mod").transpose(1, 0, 2).reshape(16, N_DEV * ncol)
    mod_b = lax.dynamic_slice(mod_all, (me, 0), (1, 6 * d))
    mods = [mod_b[:, i * d:(i + 1) * d] for i in range(6)]
    c_mods = [mod_all[N_DEV:N_DEV + 1, :d], mod_all[N_DEV:N_DEV + 1, d:2 * d]]

    zpad = lambda r: jnp.zeros((r, kw), F32)
    w = dict(
        sh_in=w_in.reshape(d, w_in.shape[2]).astype(BF16), sh_o=w_o.reshape(w_o.shape[1], d).astype(BF16),
        sh_1=w_1.reshape(d, w_1.shape[2]).astype(BF16), sh_2=w_2.reshape(w_2.shape[1], d).astype(BF16),
        pre1_g=pre1_g, post1_g=post1_g, pre2_g=pre2_g, post2_g=post2_g, b_dec_f=b_dec_f, b_dec_b=b_dec_b,
        wdf_pad=jnp.concatenate([wdf, zpad(LANES - lr)], axis=0),
        wdb_pad=jnp.concatenate([zpad(lr), wdb, zpad(LANES - 2 * lr)], axis=0),
        gla_g=gla_g, sg_ln_g=sg_ln_g, sg_ln_b=sg_ln_b, w_s=w_s[0],
        bs_full=jnp.repeat(b_s[0].T, sgw // cfg["SG_G"], axis=1))

    lossc, grad_x, big, small, dmod, dmod_c = _local_step(x[0], ctx[0], loss_target[0], mods, c_mods, w, cfg)
    loss = lax.psum(jnp.sum(lossc), AXES)

    order3 = ["pre1_g", "post1_g", "pre2_g", "post2_g", "w_dec_f", "b_dec_f", "w_dec_b", "b_dec_b", "gla_norm_g",
              "sg_ln_g", "sg_ln_b", "w_s", "b_s"]
    pieces = [dmod, dmod_c] + [small[k].reshape(1, -1) for k in order3]
    sizes = [p.shape[1] for p in pieces]
    g3 = _all_gather_vec(jnp.concatenate(pieces, axis=1), name="ag_small_grads")
    offs = [0]
    for s in sizes:
        offs.append(offs[-1] + s)
    dmod_all = g3[:, :6 * d]
    dmod_c_all = jnp.pad(g3[:, offs[1]:offs[2]], ((0, 0), (0, 4 * d)))
    parts8 = {k: g3[:, offs[2 + i]:offs[3 + i]] for i, k in enumerate(order3)}
    parts8["b_ada"] = dmod_all + dmod_c_all
    parts8["w_dec_f"] = lax.dynamic_slice(parts8["w_dec_f"].reshape(N_DEV, lr, kw), (0, 0, me * kws),
                                          (N_DEV, lr, kws)).reshape(N_DEV, -1)
    parts8["w_dec_b"] = lax.dynamic_slice(parts8["w_dec_b"].reshape(N_DEV, lr, kw), (0, 0, me * kws),
                                          (N_DEV, lr, kws)).reshape(N_DEV, -1)
    parts8["gla_norm_g"] = lax.dynamic_slice(parts8["gla_norm_g"].reshape(N_DEV, h, dv), (0, 0, me * dvs),
                                             (N_DEV, h, dvs)).reshape(N_DEV, -1)

    dm16 = jnp.concatenate([dmod_all, dmod_c_all], axis=0)
    dm_mine = lax.dynamic_slice(dm16, (0, me * ncol), (16, ncol))
    g_w_ada, dcc = _ada_bwd(c16, dm_mine, wa, c_ctx.reshape(1, d), tn=tn_ada)
    parts8["c_ctx"] = _all_gather_vec(dcc, name="ag_cctx")

    flat = lambda t: t.reshape(1, -1)
    g8 = _dense(jnp.concatenate([parts8[k] for k in SMALL_NAMES], axis=1))
    ws, ms, vs = [_dense(jnp.concatenate([flat(src[k]) for k in SMALL_NAMES], axis=1))[0]
                  for src in (weights, mom_m, mom_v)]
    res_small = [r.reshape(1, -1) for r in _adam_small(g8, ws, ms, vs)]
    out = {}
    off = 0
    for k in SMALL_NAMES:
        sz = weights[k].size
        out[k] = [r[:, off:off + sz].reshape(weights[k].shape) for r in res_small]
        off += sz

    rows_for = lambda r, cols: _tile(r, max(8, ELEMS_PER_BLOCK // cols), 16)
    pin_own, u_in, pin_wire = big["w_in"]
    ra, rb = _w_in_splits(d)
    r2 = (d, ncol)
    *res, u_in = _adam_big([(g_w_ada, None)], wa, m_w_ada.reshape(r2), v_w_ada.reshape(r2), name="adam_w_ada",
                           tr=rows_for(*r2),
                           sides=_Sides([_chip_job([pin_wire], rows=(ra, rb - ra), chained=[u_in])]))
    out["w_ada"] = [r.reshape(w_ada.shape) for r in res]
    big["w_in"] = (pin_own, u_in)
    for nm in ("w_2", "w_1", "w_o", "w_in"):
        own, u = big[nm]
        shp = weights[nm].shape
        r2 = (shp[1], shp[2])
        sides = _Sides([_chip_job([pin_wire], rows=(rb, d - rb), chained=[big["w_in"][1]])]) if nm == "w_2" else _NO_SIDES
        res = _adam_big([(own, 0), (u, 0), (u, 1), (u, 2)], weights[nm].reshape(r2), mom_m[nm].reshape(r2),
                        mom_v[nm].reshape(r2), name="adam_" + nm, tr=rows_for(*r2), sides=sides)
        if nm == "w_2":
            big["w_in"] = (pin_own, res[4])
        out[nm] = [r.reshape(shp) for r in res[:4]]

    outs = [loss, grad_x[None]]
    for i in range(4):
        outs += [out[k][i] for k in WEIGHT_ORDER]
    return tuple(outs)
```

```python
import math

import jax
import jax.numpy as jnp
from jax import lax
from jax.experimental import pallas as pl
from jax.experimental.pallas import tpu as pltpu

F32 = jnp.float32
BF16 = jnp.bfloat16
MXU_DTYPE = jnp.bfloat16
HI = lax.Precision.HIGHEST

N_DEV = 8
AXES = ("x", "y", "c")
MESH = pl.DeviceIdType.MESH
LANES = 128
VMEM_LIMIT = 56 * 1024 * 1024

EPS = 1e-6
GRID_W = 64
GLA_CHUNK = 64
GLA_TAU = 16.0
ROPE_BASE = 10000.0
ADAM_LR = 0.001
ADAM_B1 = 0.9
ADAM_B2 = 0.999
ADAM_EPS = 1e-08
ADAM_WD = 0.01
ADAM_STEP = 10


def _cparams(sem):
    return pltpu.CompilerParams(dimension_semantics=sem, vmem_limit_bytes=VMEM_LIMIT)


def _tile(n, target, align):
    if n <= target:
        return n
    best = None
    for t in range(align, target + 1, align):
        if n % t == 0:
            best = t
    assert best is not None, (n, target, align)
    return best


def _dg(a, b, dims, prec=None):
    return lax.dot_general(a, b, (dims, ((), ())), precision=prec, preferred_element_type=F32)


def _nn(a, b):
    return _dg(a.astype(MXU_DTYPE), b.astype(MXU_DTYPE), ((1,), (0,)))


def _nt(a, b):
    return _dg(a.astype(MXU_DTYPE), b.astype(MXU_DTYPE), ((1,), (1,)))


def _tn(a, b):
    return _dg(a.astype(MXU_DTYPE), b.astype(MXU_DTYPE), ((0,), (0,)))


def _sigmoid(x):
    return 1.0 / (1.0 + jnp.exp(-x))


def _silu(x):
    return x * _sigmoid(x)


def _dsilu(x):
    s = _sigmoid(x)
    return s * (1.0 + x * (1.0 - s))


def _gelu(x):
    return 0.5 * x * (1.0 + lax.erf(x * (1.0 / math.sqrt(2.0))))


def _dgelu(x):
    return 0.5 * (1.0 + lax.erf(x * (1.0 / math.sqrt(2.0)))) + x * jnp.exp(-0.5 * x * x) * (1.0 / math.sqrt(2.0 * math.pi))


def _rstd(x):
    return lax.rsqrt(jnp.mean(x * x, axis=-1, keepdims=True) + EPS)


def _rms_bwd(x, r, dn):
    return r * dn - x * (r * r * r) * jnp.mean(dn * x, axis=-1, keepdims=True)


def _colsum(x):
    return jnp.sum(x, axis=0, keepdims=True)


class _Sides:
    def __init__(self, jobs):
        self.jobs = list(jobs)
        self.ins = [a for j in self.jobs for a in j["ins"]]
        self.outs = [o for j in self.jobs for o in j["outs"]]
        self.sems = [s for j in self.jobs for s in j["sems"]]

    def aliases(self, in_base, out_base):
        res, oi, oo = {}, 0, 0
        for j in self.jobs:
            for a, b in j.get("alias", {}).items():
                res[in_base + oi + a] = out_base + oo + b
            oi += len(j["ins"])
            oo += len(j["outs"])
        return res

    def run(self, phase, in_refs, out_refs, sem_refs):
        oi = oo = os_ = 0
        for j in self.jobs:
            ni, no, ns = len(j["ins"]), len(j["outs"]), len(j["sems"])
            j[phase](in_refs[oi:oi + ni], out_refs[oo:oo + no], sem_refs[os_:os_ + ns])
            oi, oo, os_ = oi + ni, oo + no, os_ + ns


_NO_SIDES = _Sides([])
ANY_SPEC = pl.BlockSpec(memory_space=pl.ANY)


def _matmul(a, b, mode, m, n, k, *, tm, tn, tk, name, out_shapes, b_spec=None, out_specs=None,
            epilogue=None, extras=(), sides=_NO_SIDES):
    nk = k // tk
    assert m % tm == 0 and n % tn == 0 and k % tk == 0, (name, m, n, k, tm, tn, tk)
    dot = {"nn": _nn, "nt": _nt, "tn": _tn}[mode]
    if mode == "tn":
        a_spec = pl.BlockSpec((tk, tm), lambda i, j, kk: (kk, i))
    else:
        a_spec = pl.BlockSpec((tm, tk), lambda i, j, kk: (i, kk))
    if b_spec is None:
        if mode == "nt":
            b_spec = pl.BlockSpec((tn, tk), lambda i, j, kk: (j, kk))
        else:
            b_spec = pl.BlockSpec((tk, tn), lambda i, j, kk: (kk, j))
    mn_spec = pl.BlockSpec((tm, tn), lambda i, j, kk: (i, j))
    if out_specs is None:
        out_specs = [mn_spec] * len(out_shapes)
    n_extra = len(extras)
    n_out = len(out_shapes)
    n_si, n_so = len(sides.ins), len(sides.outs)
    ni, nj = m // tm, n // tn

    def body(a_ref, b_ref, *rest):
        extra_refs = rest[:n_extra]
        rest = rest[n_extra:]
        side_in, rest = rest[:n_si], rest[n_si:]
        out_refs, rest = rest[:n_out], rest[n_out:]
        side_out, rest = rest[:n_so], rest[n_so:]
        acc, side_sems = rest[0], rest[1:]
        i, j, kk = pl.program_id(0), pl.program_id(1), pl.program_id(2)

        if sides.jobs:
            @pl.when((i == 0) & (j == 0) & (kk == 0))
            def _():
                sides.run("start", side_in, side_out, side_sems)

        @pl.when(kk == 0)
        def _():
            acc[...] = jnp.zeros_like(acc)

        acc[...] += dot(a_ref[...], b_ref[...])

        @pl.when(kk == nk - 1)
        def _():
            vals = (acc[...],) if epilogue is None else epilogue(acc[...], *[e[...] for e in extra_refs])
            for o, v in zip(out_refs, vals):
                o[...] = v.astype(o.dtype)

        if sides.jobs:
            @pl.when((i == ni - 1) & (j == nj - 1) & (kk == nk - 1))
            def _():
                sides.run("finish", side_in, side_out, side_sems)

    sem = ("arbitrary",) * 3 if sides.jobs else ("parallel", "parallel", "arbitrary")
    res = pl.pallas_call(
        body, name=name, grid=(ni, nj, nk),
        in_specs=[a_spec, b_spec] + [mn_spec] * n_extra + [ANY_SPEC] * n_si,
        out_specs=list(out_specs) + [ANY_SPEC] * n_so, out_shape=list(out_shapes) + list(sides.outs),
        scratch_shapes=[pltpu.VMEM((tm, tn), F32)] + list(sides.sems),
        input_output_aliases=sides.aliases(2 + n_extra, n_out),
        compiler_params=_cparams(sem),
    )(a, b, *extras, *sides.ins)
    return res


def _blocked_b_nn(ns, tk, tn):
    assert ns % tn == 0
    return pl.BlockSpec((None, tk, tn), lambda i, j, kk: ((j * tn) // ns, kk, ((j * tn) % ns) // tn))


def _blocked_b_nt(ks, tn, tk):
    assert ks % tk == 0
    return pl.BlockSpec((None, tn, tk), lambda i, j, kk: ((kk * tk) // ks, j, ((kk * tk) % ks) // tk))


def _blocked_out(ns, tm, tn):
    assert ns % tn == 0
    return pl.BlockSpec((None, tm, tn), lambda i, j, kk: ((j * tn) // ns, i, ((j * tn) % ns) // tn))


def _rows_call(body, *, name, nblk, tr, row_ins, consts, row_outs, accs=(), sides=_NO_SIDES):
    n_ri, n_c, n_ro, n_acc = len(row_ins), len(consts), len(row_outs), len(accs)
    n_si, n_so = len(sides.ins), len(sides.outs)

    def kern(*refs):
        i = pl.program_id(0)
        rin, refs = refs[:n_ri], refs[n_ri:]
        cin, refs = refs[:n_c], refs[n_c:]
        side_in, refs = refs[:n_si], refs[n_si:]
        rout, refs = refs[:n_ro], refs[n_ro:]
        acc, refs = refs[:n_acc], refs[n_acc:]
        side_out, side_sems = refs[:n_so], refs[n_so:]

        if sides.jobs:
            @pl.when(i == 0)
            def _():
                sides.run("start", side_in, side_out, side_sems)

        if n_acc:
            @pl.when(i == 0)
            def _():
                for r in acc:
                    r[...] = jnp.zeros_like(r)

        body(rin, cin, rout, acc)

        if sides.jobs:
            @pl.when(i == nblk - 1)
            def _():
                sides.run("finish", side_in, side_out, side_sems)

    in_specs = [pl.BlockSpec((tr, w), lambda i, ro=ro, co=co: (i + ro, co)) for (_, w, ro, co) in row_ins]
    in_specs += [pl.BlockSpec(cst.shape, lambda i, nd=cst.ndim: (0,) * nd) for cst in consts]
    out_specs = [pl.BlockSpec((tr, w), lambda i: (i, 0)) for (_, w, _) in row_outs]
    out_specs += [pl.BlockSpec(s, lambda i, nd=len(s): (0,) * nd) for s in accs]
    out_shape = [jax.ShapeDtypeStruct((r, w), dt) for (r, w, dt) in row_outs]
    out_shape += [jax.ShapeDtypeStruct(s, F32) for s in accs]
    return pl.pallas_call(
        kern, name=name, grid=(nblk,), in_specs=in_specs + [ANY_SPEC] * n_si,
        out_specs=out_specs + [ANY_SPEC] * n_so, out_shape=out_shape + list(sides.outs),
        scratch_shapes=list(sides.sems),
        input_output_aliases=sides.aliases(n_ri + n_c, n_ro + n_acc),
        compiler_params=_cparams(("arbitrary",)),
    )(*[r[0] for r in row_ins], *consts, *sides.ins)


def _norm_mod(x, g, shift, scale, *, name, tr):
    rows, d = x.shape

    def body(rin, cin, rout, acc):
        xv = rin[0][...]
        n = xv * _rstd(xv) * cin[0][...]
        rout[0][...] = (n * (1.0 + cin[2][...]) + cin[1][...]).astype(BF16)

    return _rows_call(body, name=name, nblk=rows // tr, tr=tr, row_ins=[(x, d, 0, 0)],
                      consts=[g, shift, scale], row_outs=[(rows, d, BF16)])[0]


def _swap_halves(t, width):
    lane = lax.broadcasted_iota(jnp.int32, t.shape, 1)
    return jnp.where(lane % 64 < 32, pltpu.roll(t, width - 32, 1), pltpu.roll(t, 32, 1))


def _gla_prep(z, tabs, wdf_pad, wdb_pad, bdf, bdb, cfg, *, tr):
    rows = z.shape[0]
    kw, h = cfg["KW"], cfg["H"]

    def body(rin, cin, rout, acc):
        zq, zk, zl = rin[0][...], rin[1][...], rin[2][...]
        cq, sq, ck, sk = [jnp.concatenate([rin[3 + t][...]] * h, axis=1) for t in range(4)]
        rout[0][...] = zq * cq + _swap_halves(zq, kw) * sq
        rout[1][...] = zk * ck + _swap_halves(zk, kw) * sk
        for o, w, b in ((2, cin[0], cin[2]), (3, cin[1], cin[3])):
            a = _nn(zl, w[...]) + b[...]
            rout[o][...] = (jnp.minimum(a, 0.0) - jnp.log(1.0 + jnp.exp(-jnp.abs(a)))) * (1.0 / GLA_TAU)

    row_ins = [(z, kw, 0, 0), (z, kw, 0, 1), (z, LANES, 0, cfg["L0"] // LANES)]
    row_ins += [(t, LANES, 0, 0) for t in tabs]
    return _rows_call(body, name="gla_prep", nblk=rows // tr, tr=tr, row_ins=row_ins,
                      consts=[wdf_pad, wdb_pad, bdf, bdb], row_outs=[(rows, kw, F32)] * 4)


def _chunk_consts(rev):
    c = GLA_CHUNK
    r = lax.broadcasted_iota(jnp.int32, (c, c), 0)
    cc = lax.broadcasted_iota(jnp.int32, (c, c), 1)
    keep = (cc >= r) if rev else (cc <= r)
    return keep, keep.astype(F32)


def _heads_per_step(cfg):
    hb = 2 if cfg["H"] % 2 == 0 else 1
    assert cfg["V0"] % (hb * cfg["DV"]) == 0
    return hb


def _chunk_decay(la, keep_f):
    b = _dg(keep_f, la, ((1,), (0,)), HI)
    return b, _colsum(la)


def _gla_fwd(qr, kr, z, la, st0, cfg, *, rev, row_off, nrows, tb, name, sides=_NO_SIDES):
    h, dk, dv = cfg["H"], cfg["DK"], cfg["DV"]
    c = GLA_CHUNK
    nsub = tb // c
    nblk = nrows // tb
    roff = row_off // tb
    hb = _heads_per_step(cfg)
    v_cb = cfg["V0"] // (hb * dv)
    n_si, n_so = len(sides.ins), len(sides.outs)

    def blk(j):
        return (nblk - 1 - j) if rev else j

    def body(q_ref, k_ref, v_ref, la_ref, st0_ref, *rest):
        side_in, rest = rest[:n_si], rest[n_si:]
        o_ref, save_ref, fin_ref = rest[:3]
        side_out, st, side_sems = rest[3:3 + n_so], rest[3 + n_so], rest[4 + n_so:]
        hh, j = pl.program_id(0), pl.program_id(1)

        if sides.jobs:
            @pl.when((hh == 0) & (j == 0))
            def _():
                sides.run("start", side_in, side_out, side_sems)

        @pl.when(j == 0)
        def _():
            st[...] = st0_ref[...]

        keep, keep_f = _chunk_consts(rev)
        order = range(nsub - 1, -1, -1) if rev else range(nsub)
        heads = range(hb)
        ksl = [slice(g * dk, (g + 1) * dk) for g in heads]
        vsl = [slice(g * dv, (g + 1) * dv) for g in heads]
        state = [st[g] for g in heads]
        for s in order:
            rs = pl.ds(s * c, c)
            q = [q_ref[rs, ksl[g]] for g in heads]
            k = [k_ref[rs, ksl[g]] for g in heads]
            v = [v_ref[rs, vsl[g]] for g in heads]
            bb = [_chunk_decay(la_ref[rs, ksl[g]], keep_f) for g in heads]
            qe = [q[g] * jnp.exp(bb[g][0]) for g in heads]
            ke = [k[g] * jnp.exp(-bb[g][0]) for g in heads]
            kl = [k[g] * jnp.exp(bb[g][1] - bb[g][0]) for g in heads]
            att = [jnp.where(keep, _nt(qe[g], ke[g]), 0.0) for g in heads]
            out = [_nt(qe[g], state[g]) + _nn(att[g], v[g]) for g in heads]
            new = [state[g] * jnp.exp(bb[g][1]) + _tn(v[g], kl[g]) for g in heads]
            for g in heads:
                save_ref[g, s] = state[g]
                o_ref[rs, vsl[g]] = out[g]
            state = new
        for g in heads:
            st[g] = state[g]

        @pl.when(j == nblk - 1)
        def _():
            fin_ref[...] = st[...]

        if sides.jobs:
            @pl.when((hh == h // hb - 1) & (j == nblk - 1))
            def _():
                sides.run("finish", side_in, side_out, side_sems)

    in_specs = [
        pl.BlockSpec((tb, hb * dk), lambda hh, j: (roff + blk(j), hh)),
        pl.BlockSpec((tb, hb * dk), lambda hh, j: (roff + blk(j), hh)),
        pl.BlockSpec((tb, hb * dv), lambda hh, j: (roff + blk(j), v_cb + hh)),
        pl.BlockSpec((tb, hb * dk), lambda hh, j: (roff + blk(j), hh)),
        pl.BlockSpec((hb, dv, dk), lambda hh, j: (hh, 0, 0)),
    ]
    out_specs = [
        pl.BlockSpec((tb, hb * dv), lambda hh, j: (blk(j), hh)),
        pl.BlockSpec((hb, nsub, dv, dk), lambda hh, j: (hh, blk(j), 0, 0)),
        pl.BlockSpec((hb, dv, dk), lambda hh, j: (hh, 0, 0)),
    ]
    out_shape = [
        jax.ShapeDtypeStruct((nrows, h * dv), F32),
        jax.ShapeDtypeStruct((h, nrows // c, dv, dk), F32),
        jax.ShapeDtypeStruct((h, dv, dk), F32),
    ]
    return pl.pallas_call(
        body, name=name, grid=(h // hb, nblk), in_specs=in_specs + [ANY_SPEC] * n_si,
        out_specs=out_specs + [ANY_SPEC] * n_so, out_shape=out_shape + list(sides.outs),
        scratch_shapes=[pltpu.VMEM((hb, dv, dk), F32)] + list(sides.sems),
        input_output_aliases=sides.aliases(5, 3),
        compiler_params=_cparams(("arbitrary", "arbitrary")),
    )(qr, kr, z, la, st0, *sides.ins)


def _gla_bwd(qr, kr, z, la, do, save, dst_init, cfg, *, rev, row_off, nrows, tb, name):
    h, dk, dv = cfg["H"], cfg["DK"], cfg["DV"]
    c = GLA_CHUNK
    nsub = tb // c
    nblk = nrows // tb
    roff = row_off // tb
    hb = _heads_per_step(cfg)
    v_cb = cfg["V0"] // (hb * dv)

    def blk(j):
        return j if rev else (nblk - 1 - j)

    def body(q_ref, k_ref, v_ref, la_ref, do_ref, save_ref, di_ref, dq_ref, dk_ref, dv_ref, dla_ref, d0_ref, dst):
        j = pl.program_id(1)

        @pl.when(j == 0)
        def _():
            dst[...] = di_ref[...]

        keep, keep_f = _chunk_consts(rev)
        keep_t = _chunk_consts(not rev)[1]
        order = range(nsub) if rev else range(nsub - 1, -1, -1)
        heads = range(hb)
        ksl = [slice(g * dk, (g + 1) * dk) for g in heads]
        vsl = [slice(g * dv, (g + 1) * dv) for g in heads]
        d_after = [dst[g] for g in heads]
        for s in order:
            rs = pl.ds(s * c, c)
            q = [q_ref[rs, ksl[g]] for g in heads]
            k = [k_ref[rs, ksl[g]] for g in heads]
            v = [v_ref[rs, vsl[g]] for g in heads]
            lac = [la_ref[rs, ksl[g]] for g in heads]
            dout = [do_ref[rs, vsl[g]] for g in heads]
            s_in = [save_ref[g, s] for g in heads]
            bb = [_chunk_decay(lac[g], keep_f) for g in heads]
            eb = [jnp.exp(bb[g][0]) for g in heads]
            enb = [jnp.exp(-bb[g][0]) for g in heads]
            elb = [jnp.exp(bb[g][1] - bb[g][0]) for g in heads]
            etot = [jnp.exp(bb[g][1]) for g in heads]
            qe = [q[g] * eb[g] for g in heads]
            ke = [k[g] * enb[g] for g in heads]
            kl = [k[g] * elb[g] for g in heads]
            att = [jnp.where(keep, _nt(qe[g], ke[g]), 0.0) for g in heads]
            datt = [jnp.where(keep, _nt(dout[g], v[g]), 0.0) for g in heads]
            dqe = [_nn(dout[g], s_in[g]) + _nn(datt[g], ke[g]) for g in heads]
            dke = [_tn(datt[g], qe[g]) for g in heads]
            dkl = [_nn(v[g], d_after[g]) for g in heads]
            dvv = [_tn(att[g], dout[g]) + _nt(kl[g], d_after[g]) for g in heads]
            db = [dqe[g] * qe[g] - dke[g] * ke[g] - dkl[g] * kl[g] for g in heads]
            dbtot = [_colsum(dkl[g] * kl[g]) + _colsum(d_after[g] * s_in[g]) * etot[g] for g in heads]
            dla = [_dg(keep_t, db[g], ((1,), (0,)), HI) + dbtot[g] for g in heads]
            d_after = [d_after[g] * etot[g] + _tn(dout[g], qe[g]) for g in heads]
            for g in heads:
                dv_ref[rs, vsl[g]] = dvv[g]
                dla_ref[rs, ksl[g]] = dla[g]
                dq_ref[rs, ksl[g]] = dqe[g] * eb[g]
                dk_ref[rs, ksl[g]] = dke[g] * enb[g] + dkl[g] * elb[g]
        for g in heads:
            dst[g] = d_after[g]

        @pl.when(j == nblk - 1)
        def _():
            d0_ref[...] = dst[...]

    in_specs = [
        pl.BlockSpec((tb, hb * dk), lambda hh, j: (roff + blk(j), hh)),
        pl.BlockSpec((tb, hb * dk), lambda hh, j: (roff + blk(j), hh)),
        pl.BlockSpec((tb, hb * dv), lambda hh, j: (roff + blk(j), v_cb + hh)),
        pl.BlockSpec((tb, hb * dk), lambda hh, j: (roff + blk(j), hh)),
        pl.BlockSpec((tb, hb * dv), lambda hh, j: (blk(j), hh)),
        pl.BlockSpec((hb, nsub, dv, dk), lambda hh, j: (hh, blk(j), 0, 0)),
        pl.BlockSpec((hb, dv, dk), lambda hh, j: (hh, 0, 0)),
    ]
    out_specs = [
        pl.BlockSpec((tb, hb * dk), lambda hh, j: (blk(j), hh)),
        pl.BlockSpec((tb, hb * dk), lambda hh, j: (blk(j), hh)),
        pl.BlockSpec((tb, hb * dv), lambda hh, j: (blk(j), hh)),
        pl.BlockSpec((tb, hb * dk), lambda hh, j: (blk(j), hh)),
        pl.BlockSpec((hb, dv, dk), lambda hh, j: (hh, 0, 0)),
    ]
    out_shape = [
        jax.ShapeDtypeStruct((nrows, h * dk), F32),
        jax.ShapeDtypeStruct((nrows, h * dk), F32),
        jax.ShapeDtypeStruct((nrows, h * dv), F32),
        jax.ShapeDtypeStruct((nrows, h * dk), F32),
        jax.ShapeDtypeStruct((h, dv, dk), F32),
    ]
    return pl.pallas_call(
        body, name=name, grid=(h // hb, nblk), in_specs=in_specs, out_specs=out_specs, out_shape=out_shape,
        scratch_shapes=[pltpu.VMEM((hb, dv, dk), F32)],
        compiler_params=_cparams(("arbitrary", "arbitrary")),
    )(qr, kr, z, la, do, save, dst_init)


def _gla_post(gf, gb, la_f, la_b, z, tabs, wdf_pad, wdb_pad, cfg, *, row_off, nrows, tr, name):
    kw, vw = cfg["KW"], cfg["VW"]
    h = cfg["H"]
    ro = row_off // tr

    def body(rin, cin, rout, acc):
        dq = rin[0][...] + rin[1][...]
        dk_ = rin[2][...] + rin[3][...]
        zl = rin[10][...]
        cq, sq, ck, sk = [jnp.concatenate([rin[11 + t][...]] * h, axis=1) for t in range(4)]
        rout[0][...] = (dq * cq + _swap_halves(dq * sq, kw)).astype(BF16)
        rout[1][...] = (dk_ * ck + _swap_halves(dk_ * sk, kw)).astype(BF16)
        rout[2][...] = (rin[8][...] + rin[9][...]).astype(BF16)
        dzl = jnp.zeros(zl.shape, F32)
        for t, w in ((0, cin[0]), (1, cin[1])):
            la = rin[6 + t][...]
            da = rin[4 + t][...] * ((1.0 - jnp.exp(la * GLA_TAU)) * (1.0 / GLA_TAU))
            dzl = dzl + _nt(da, w[...])
            acc[t][...] += _tn(zl, da)
            acc[2 + t][...] += _colsum(da)
        rout[3][...] = dzl.astype(BF16)

    row_ins = [(gf[0], kw, 0, 0), (gb[0], kw, 0, 0), (gf[1], kw, 0, 0), (gb[1], kw, 0, 0),
               (gf[3], kw, 0, 0), (gb[3], kw, 0, 0), (la_f, kw, ro, 0), (la_b, kw, ro, 0),
               (gf[2], vw, 0, 0), (gb[2], vw, 0, 0), (z, LANES, ro, cfg["L0"] // LANES)]
    row_ins += [(t, LANES, ro, 0) for t in tabs]
    return _rows_call(body, name=name, nblk=nrows // tr, tr=tr, row_ins=row_ins, consts=[wdf_pad, wdb_pad],
                      row_outs=[(nrows, kw, BF16), (nrows, kw, BF16), (nrows, vw, BF16), (nrows, LANES, BF16)],
                      accs=[(LANES, kw), (LANES, kw), (1, kw), (1, kw)])


def _readout_fwd(o_f, o_b, z, g, cfg, *, tr):
    n, vw = o_f.shape
    h, dv = cfg["H"], cfg["DV"]

    def body(rin, cin, rout, acc):
        for hh in range(h):
            cs = slice(hh * dv, (hh + 1) * dv)
            oh = rin[0][:, cs] + rin[1][:, cs]
            y = oh * _rstd(oh) * cin[0][:, cs]
            rout[0][:, cs] = (y * _silu(rin[2][:, cs])).astype(BF16)

    return _rows_call(body, name="gla_readout", nblk=n // tr, tr=tr,
                      row_ins=[(o_f, vw, 0, 0), (o_b, vw, 0, 0), (z, vw, 0, cfg["R0"] // vw)], consts=[g],
                      row_outs=[(n, vw, BF16)])[0]


def _readout_bwd(o_f, o_b, z, dycat, g, cfg, *, tr):
    n, vw = o_f.shape
    h, dv = cfg["H"], cfg["DV"]

    def body(rin, cin, rout, acc):
        for hh in range(h):
            cs = slice(hh * dv, (hh + 1) * dv)
            oh = rin[0][:, cs] + rin[1][:, cs]
            r, dyg, gh = rin[2][:, cs], rin[3][:, cs], cin[0][:, cs]
            rs = _rstd(oh)
            dy = dyg * _silu(r)
            rout[0][:, cs] = _rms_bwd(oh, rs, dy * gh).astype(BF16)
            rout[1][:, cs] = (dyg * (oh * rs * gh) * _dsilu(r)).astype(BF16)
            acc[0][:, cs] += _colsum(dy * oh * rs)

    return _rows_call(body, name="gla_readout_bwd", nblk=n // tr, tr=tr,
                      row_ins=[(o_f, vw, 0, 0), (o_b, vw, 0, 0), (z, vw, 0, cfg["R0"] // vw), (dycat, vw, 0, 0)],
                      consts=[g], row_outs=[(n, vw, BF16), (n, vw, BF16)], accs=[(1, vw)])


def _sg_ln(vv):
    mu = jnp.mean(vv, axis=-1, keepdims=True)
    cen = vv - mu
    rstd = lax.rsqrt(jnp.mean(cen * cen, axis=-1, keepdims=True) + EPS)
    return cen * rstd, rstd


def _sg_fwd(z, n, lng, lnb, w_s, bs_full, cfg):
    sgw, grp, sc = cfg["SGW"], cfg["SG_G"], cfg["SG_C"]
    gw = sgw // grp

    def body(rin, cin, rout, acc):
        u = _gelu(rin[0][...])
        xhat, _ = _sg_ln(_gelu(rin[1][...]))
        vvn = xhat * cin[0][...] + cin[1][...]
        for gg in range(grp):
            cs = slice(gg * gw, (gg + 1) * gw)
            s = _nn(cin[2][gg], vvn[:, cs]) + cin[3][:, cs]
            rout[0][:, cs] = (u[:, cs] * s).astype(BF16)

    return _rows_call(body, name="sg_fwd", nblk=n // sc, tr=sc,
                      row_ins=[(z, sgw, 0, cfg["U0"] // sgw), (z, sgw, 0, cfg["VV0"] // sgw)],
                      consts=[lng, lnb, w_s, bs_full], row_outs=[(n, sgw, BF16)])[0]


def _sg_bwd(z, dycat, n, lng, lnb, w_s, bs_full, cfg):
    sgw, grp, sc = cfg["SGW"], cfg["SG_G"], cfg["SG_C"]
    gw = sgw // grp

    def body(rin, cin, rout, acc):
        up, vp, dy = rin[0][...], rin[1][...], rin[2][...]
        u = _gelu(up)
        xhat, rstd = _sg_ln(_gelu(vp))
        lng_v = cin[0][...]
        vvn = xhat * lng_v + cin[1][...]
        ds = dy * u
        acc[1][...] += ds
        dvvn_parts = []
        for gg in range(grp):
            cs = slice(gg * gw, (gg + 1) * gw)
            w = cin[2][gg]
            s = _nn(w, vvn[:, cs]) + cin[3][:, cs]
            rout[0][:, cs] = (dy[:, cs] * s * _dgelu(up[:, cs])).astype(BF16)
            acc[0][gg] += _nt(ds[:, cs], vvn[:, cs])
            dvvn_parts.append(_tn(w, ds[:, cs]))
        dvvn = jnp.concatenate(dvvn_parts, axis=1)
        acc[2][...] += _colsum(dvvn * xhat)
        acc[3][...] += _colsum(dvvn)
        dxh = dvvn * lng_v
        dvv = rstd * (dxh - jnp.mean(dxh, axis=-1, keepdims=True)
                      - xhat * jnp.mean(dxh * xhat, axis=-1, keepdims=True))
        rout[1][...] = (dvv * _dgelu(vp)).astype(BF16)

    vw = cfg["VW"]
    return _rows_call(body, name="sg_bwd", nblk=n // sc, tr=sc,
                      row_ins=[(z, sgw, 0, cfg["U0"] // sgw), (z, sgw, 0, cfg["VV0"] // sgw),
                               (dycat, sgw, 0, vw // sgw)],
                      consts=[lng, lnb, w_s, bs_full], row_outs=[(n, sgw, BF16), (n, sgw, BF16)],
                      accs=[(grp, sc, sc), (sc, sgw), (1, sgw), (1, sgw)])


def _mid_fwd(x, mix, g1, post1, pre2, sh2, sc2, *, tr):
    n, d = x.shape

    def body(rin, cin, rout, acc):
        xv, mv = rin[0][...], rin[1][...]
        x1 = xv + cin[0][...] * (mv * _rstd(mv) * cin[1][...])
        rout[0][...] = x1
        n2 = x1 * _rstd(x1) * cin[2][...]
        rout[1][...] = (n2 * (1.0 + cin[4][...]) + cin[3][...]).astype(BF16)

    return _rows_call(body, name="mid_fwd", nblk=n // tr, tr=tr, row_ins=[(x, d, 0, 0), (mix, d, 0, 0)],
                      consts=[g1, post1, pre2, sh2, sc2], row_outs=[(n, d, F32), (n, d, BF16)])


def _head_bwd(x1, m2, target, g2, post2, *, tr):
    n, d = x1.shape

    def body(rin, cin, rout, acc):
        x1v, mv, tv = rin[0][...], rin[1][...], rin[2][...]
        g2v, pg = cin[0][...], cin[1][...]
        r = _rstd(mv)
        y2 = mv * r * pg
        err = (x1v + g2v * y2) - tv
        acc[2][...] += _colsum(err * err) * (0.5 / d)
        dx2 = err * (1.0 / d)
        rout[0][...] = dx2
        dy2 = dx2 * g2v
        acc[0][...] += _colsum(dx2 * y2)
        acc[1][...] += _colsum(dy2 * mv * r)
        rout[1][...] = _rms_bwd(mv, r, dy2 * pg).astype(BF16)

    return _rows_call(body, name="head_bwd", nblk=n // tr, tr=tr,
                      row_ins=[(x1, d, 0, 0), (m2, d, 0, 0), (target, d, 0, 0)], consts=[g2, post2],
                      row_outs=[(n, d, F32), (n, d, BF16)], accs=[(1, d)] * 3)


def _mid_bwd(dh2, x1, dx2, mix, sc2, pre2, g1, post1, *, tr, sides=_NO_SIDES):
    n, d = x1.shape

    def body(rin, cin, rout, acc):
        dh, x1v, dx2v, mv = rin[0][...], rin[1][...], rin[2][...], rin[3][...]
        sc2v, pre2v, g1v, post1v = cin[0][...], cin[1][...], cin[2][...], cin[3][...]
        r2 = _rstd(x1v)
        xr = x1v * r2
        acc[0][...] += _colsum(dh)
        acc[1][...] += _colsum(dh * (xr * pre2v))
        dn2 = dh * (1.0 + sc2v)
        acc[2][...] += _colsum(dn2 * xr)
        dx1 = dx2v + _rms_bwd(x1v, r2, dn2 * pre2v)
        rout[0][...] = dx1
        r1 = _rstd(mv)
        mr = mv * r1
        acc[3][...] += _colsum(dx1 * (mr * post1v))
        dy1 = dx1 * g1v
        acc[4][...] += _colsum(dy1 * mr)
        rout[1][...] = _rms_bwd(mv, r1, dy1 * post1v).astype(BF16)

    return _rows_call(body, name="mid_bwd", nblk=n // tr, tr=tr,
                      row_ins=[(dh2, d, 0, 0), (x1, d, 0, 0), (dx2, d, 0, 0), (mix, d, 0, 0)],
                      consts=[sc2, pre2, g1, post1], row_outs=[(n, d, F32), (n, d, BF16)], accs=[(1, d)] * 5,
                      sides=sides)


def _in_bwd(da, x, dres, sc1, pre1, *, row_off, tr, name, sides=_NO_SIDES):
    n, d = x.shape
    with_res = dres is not None

    def body(rin, cin, rout, acc):
        dav, xv = rin[0][...], rin[1][...]
        sc1v, pre1v = cin[0][...], cin[1][...]
        r = _rstd(xv)
        xr = xv * r
        acc[0][...] += _colsum(dav)
        acc[1][...] += _colsum(dav * (xr * pre1v))
        dn = dav * (1.0 + sc1v)
        acc[2][...] += _colsum(dn * xr)
        if with_res:
            rout[0][...] = rin[2][...] + _rms_bwd(xv, r, dn * pre1v)

    row_ins = [(da, d, row_off // tr, 0), (x, d, 0, 0)] + ([(dres, d, 0, 0)] if with_res else [])
    return _rows_call(body, name=name, nblk=n // tr, tr=tr, row_ins=row_ins, consts=[sc1, pre1],
                      row_outs=[(n, d, F32)] if with_res else [], accs=[(1, d)] * 3, sides=sides)


def _ada_fwd(c16, w, b, *, tn):
    d, ncol = w.shape

    def body(c_ref, w_ref, b_ref, o_ref):
        o_ref[...] = _nn(_silu(c_ref[...]), w_ref[...]) + b_ref[...]

    return pl.pallas_call(
        body, name="ada_fwd", grid=(ncol // tn,),
        in_specs=[pl.BlockSpec((16, d), lambda j: (0, 0)), pl.BlockSpec((d, tn), lambda j: (0, j)),
                  pl.BlockSpec((1, tn), lambda j: (0, j))],
        out_specs=pl.BlockSpec((16, tn), lambda j: (0, j)),
        out_shape=jax.ShapeDtypeStruct((16, ncol), F32),
        compiler_params=_cparams(("arbitrary",)),
    )(c16, w, b)


def _ada_bwd(c16, dm, w, c_ctx, *, tn):
    d, ncol = w.shape

    def body(c_ref, dm_ref, w_ref, cc_ref, gw_ref, dcc_ref, acc):
        j = pl.program_id(0)

        @pl.when(j == 0)
        def _():
            acc[...] = jnp.zeros_like(acc)

        gw_ref[...] = _tn(_silu(c_ref[...]), dm_ref[...])
        acc[...] += _nt(dm_ref[...], w_ref[...])

        @pl.when(j == ncol // tn - 1)
        def _():
            dcc_ref[...] = _colsum(acc[8:16, :]) * _dsilu(cc_ref[...])

    return pl.pallas_call(
        body, name="ada_bwd", grid=(ncol // tn,),
        in_specs=[pl.BlockSpec((16, d), lambda j: (0, 0)), pl.BlockSpec((16, tn), lambda j: (0, j)),
                  pl.BlockSpec((d, tn), lambda j: (0, j)), pl.BlockSpec((1, d), lambda j: (0, 0))],
        out_specs=[pl.BlockSpec((d, tn), lambda j: (0, j)), pl.BlockSpec((1, d), lambda j: (0, 0))],
        out_shape=[jax.ShapeDtypeStruct((d, ncol), F32), jax.ShapeDtypeStruct((1, d), F32)],
        scratch_shapes=[pltpu.VMEM((16, d), F32)],
        compiler_params=_cparams(("arbitrary",)),
    )(c16, dm, w, c_ctx)


def _adam_math(w, g, m, v):
    m = ADAM_B1 * m + (1.0 - ADAM_B1) * g
    v = ADAM_B2 * v + (1.0 - ADAM_B2) * (g * g)
    m_hat = m / (1.0 - ADAM_B1 ** ADAM_STEP)
    v_hat = v / (1.0 - ADAM_B2 ** ADAM_STEP)
    delta = -ADAM_LR * (m_hat / (jnp.sqrt(v_hat) + ADAM_EPS) + ADAM_WD * w)
    return delta, m, v


def _adam_big(parts, w, m, v, *, name, tr, sides=_NO_SIDES):
    _, rows, cols = w.shape
    n_p = len(parts)
    n_si, n_so = len(sides.ins), len(sides.outs)
    nblk = rows // tr

    def body(*refs):
        ins, refs = refs[:n_p + 3], refs[n_p + 3:]
        side_in, refs = refs[:n_si], refs[n_si:]
        outs, side_out, side_sems = refs[:4], refs[4:4 + n_so], refs[4 + n_so:]
        i = pl.program_id(0)

        if sides.jobs:
            @pl.when(i == 0)
            def _():
                sides.run("start", side_in, side_out, side_sems)

        g = ins[0][...]
        for p in ins[1:n_p]:
            g = g + p[...].astype(F32)
        delta, m2, v2 = _adam_math(ins[n_p][...], g, ins[n_p + 1][...], ins[n_p + 2][...])
        outs[0][...] = g
        outs[1][...] = delta
        outs[2][...] = m2
        outs[3][...] = v2

        if sides.jobs:
            @pl.when(i == nblk - 1)
            def _():
                sides.run("finish", side_in, side_out, side_sems)

    plain = pl.BlockSpec((tr, cols), lambda i: (i, 0))
    lead = pl.BlockSpec((None, tr, cols), lambda i: (0, i, 0))
    in_specs = []
    for arr, idx in parts:
        if idx is None:
            in_specs.append(plain)
        else:
            in_specs.append(pl.BlockSpec((None, tr, cols), lambda i, idx=idx: (idx, i, 0)))
    in_specs += [lead] * 3
    return pl.pallas_call(
        body, name=name, grid=(nblk,), in_specs=in_specs + [ANY_SPEC] * n_si,
        out_specs=[lead] * 4 + [ANY_SPEC] * n_so,
        out_shape=[jax.ShapeDtypeStruct((1, rows, cols), F32)] * 4 + list(sides.outs),
        scratch_shapes=list(sides.sems), input_output_aliases=sides.aliases(n_p + 3, 4),
        compiler_params=_cparams(("arbitrary",) if sides.jobs else ("parallel",)),
    )(*[p[0] for p in parts], w, m, v, *sides.ins)


def _adam_small(g8, w, m, v):
    def body(g_ref, w_ref, m_ref, v_ref, go, do, mo, vo):
        g = g_ref[0]
        for r in range(1, N_DEV):
            g = g + g_ref[r]
        delta, m2, v2 = _adam_math(w_ref[...], g, m_ref[...], v_ref[...])
        go[...] = g
        do[...] = delta
        mo[...] = m2
        vo[...] = v2

    return pl.pallas_call(
        body, name="adam_small", out_shape=[jax.ShapeDtypeStruct(w.shape, F32)] * 4,
        compiler_params=pltpu.CompilerParams(vmem_limit_bytes=VMEM_LIMIT),
    )(g8, w, m, v)


VEC_W = 1024
TK = 2048
ELEMS_PER_BLOCK = 256 * 1024


def _dense(v):
    a, k = v.shape
    kp = -(-k // (8 * VEC_W)) * (8 * VEC_W)
    return jnp.pad(v, ((0, 0), (0, kp - k))).reshape(a, kp // VEC_W, VEC_W)


def _all_gather_vec(v, *, name):
    k = v.shape[1]
    return _all_gather_small(_dense(v)[0], name=name).reshape(N_DEV, -1)[:, :k]


def _my_pos():
    return lax.axis_index("x"), lax.axis_index("y"), lax.axis_index("c")


def _flip(v, bit):
    return (1 - v) if bit else v


def _all_gather_small(v, *, name):
    r, k = v.shape

    def body(v_ref, out_ref, send, recv, lsem):
        x, y, c = _my_pos()
        me = 4 * x + 2 * y + c
        local = pltpu.make_async_copy(v_ref, out_ref.at[me], lsem)
        local.start()
        sends = []
        for kk in range(1, N_DEV):
            peer = (_flip(x, kk & 4), _flip(y, kk & 2), _flip(c, kk & 1))
            cp = pltpu.make_async_remote_copy(src_ref=v_ref, dst_ref=out_ref.at[me], send_sem=send.at[kk - 1],
                                              recv_sem=recv.at[kk - 1], device_id=peer, device_id_type=MESH)
            cp.start()
            sends.append(cp)
        for kk in range(1, N_DEV):
            px, py, pc = _flip(x, kk & 4), _flip(y, kk & 2), _flip(c, kk & 1)
            src = 4 * px + 2 * py + pc
            pltpu.make_async_remote_copy(src_ref=v_ref, dst_ref=out_ref.at[src], send_sem=send.at[kk - 1],
                                         recv_sem=recv.at[kk - 1], device_id=(px, py, pc),
                                         device_id_type=MESH).wait_recv()
        for cp in sends:
            cp.wait_send()
        local.wait()

    return pl.pallas_call(
        body, name=name, out_shape=jax.ShapeDtypeStruct((N_DEV, r, k), v.dtype),
        in_specs=[pl.BlockSpec(memory_space=pltpu.VMEM)], out_specs=pl.BlockSpec(memory_space=pltpu.VMEM),
        scratch_shapes=[pltpu.SemaphoreType.DMA((N_DEV - 1,)), pltpu.SemaphoreType.DMA((N_DEV - 1,)),
                        pltpu.SemaphoreType.DMA],
        compiler_params=pltpu.CompilerParams(vmem_limit_bytes=VMEM_LIMIT),
    )(v)


def _ag_job(shards, rows=None, chained=None):
    n_arr = len(shards)

    def part(ref):
        return ref if rows is None else ref.at[pl.ds(rows[0], rows[1])]

    def tools(ins, outs, sems):
        send, recv, lsem = sems
        x, y, c = _my_pos()
        chips = [(1 - x, y), (x, 1 - y), (1 - x, 1 - y)]

        def copy(a, kk, block, to, src=None):
            dst = part(outs[a].at[4 * block[0] + 2 * block[1] + block[2]])
            return pltpu.make_async_remote_copy(src_ref=dst if src is None else part(src), dst_ref=dst,
                                                send_sem=send.at[a, kk], recv_sem=recv.at[a, kk],
                                                device_id=to, device_id_type=MESH)

        locals_ = [pltpu.make_async_copy(part(ins[a]), part(outs[a].at[4 * x + 2 * y + c]), lsem.at[a])
                   for a in range(n_arr)]
        firsts = []
        for a in range(n_arr):
            firsts.append(copy(a, 0, (x, y, c), (x, y, 1 - c), src=ins[a]))
            firsts += [copy(a, 1 + j, (x, y, c), (*chip, c), src=ins[a]) for j, chip in enumerate(chips)]
        return copy, locals_, firsts, chips, (x, y, c)

    def start(ins, outs, sems):
        _, locals_, firsts, _, _ = tools(ins, outs, sems)
        for cp in locals_ + firsts:
            cp.start()

    def finish(ins, outs, sems):
        copy, locals_, firsts, chips, (x, y, c) = tools(ins, outs, sems)
        me, sibling = (x, y, c), (x, y, 1 - c)
        passed = []
        for a in range(n_arr):
            for j, chip in enumerate(chips):
                copy(a, 1 + j, (*chip, c), me).wait_recv()
                fw = copy(a, 4 + j, (*chip, c), sibling)
                fw.start()
                passed.append(fw)
        for a in range(n_arr):
            copy(a, 0, sibling, me).wait_recv()
            for j, chip in enumerate(chips):
                copy(a, 4 + j, (*chip, 1 - c), me).wait_recv()
        for cp in firsts + passed:
            cp.wait_send()
        for lc in locals_:
            lc.wait()

    job = dict(ins=list(shards), outs=[jax.ShapeDtypeStruct((N_DEV,) + s.shape, s.dtype) for s in shards],
               sems=[pltpu.SemaphoreType.DMA((n_arr, 7)), pltpu.SemaphoreType.DMA((n_arr, 7)),
                     pltpu.SemaphoreType.DMA((n_arr,))], start=start, finish=finish)
    if chained is not None:
        job["ins"] = list(shards) + list(chained)
        job["alias"] = {n_arr + a: a for a in range(n_arr)}
    return job


def _exchange_job(arrays, n_slots, out_slots, src_of, dst_of, peer_of, rows=None, chained=None):
    n_arr = len(arrays)

    def copies(ins, outs, sems):
        send, recv = sems
        x, y, c = _my_pos()
        res = []
        for a in range(n_arr):
            for s in range(n_slots):
                src, dst = ins[a].at[src_of(s, x, y, c)], outs[a].at[dst_of(s)]
                if rows is not None:
                    src, dst = src.at[pl.ds(rows[0], rows[1])], dst.at[pl.ds(rows[0], rows[1])]
                res.append(pltpu.make_async_remote_copy(
                    src_ref=src, dst_ref=dst, send_sem=send.at[a, s], recv_sem=recv.at[a, s],
                    device_id=peer_of(s, x, y, c), device_id_type=MESH))
        return res

    def start(ins, outs, sems):
        for cp in copies(ins, outs, sems):
            cp.start()

    def finish(ins, outs, sems):
        cps = copies(ins, outs, sems)
        for cp in cps:
            cp.wait_recv()
        for cp in cps:
            cp.wait_send()

    job = dict(ins=list(arrays), outs=[jax.ShapeDtypeStruct((out_slots,) + g.shape[1:], g.dtype) for g in arrays],
               sems=[pltpu.SemaphoreType.DMA((n_arr, n_slots)), pltpu.SemaphoreType.DMA((n_arr, n_slots))],
               start=start, finish=finish)
    if chained is not None:
        job["ins"] = list(arrays) + list(chained)
        job["alias"] = {n_arr + a: a for a in range(n_arr)}
    return job


def _pair_job(grads):
    return _exchange_job(
        grads, 4, 4,
        src_of=lambda s, x, y, c: 4 * _flip(x, s & 2) + 2 * _flip(y, s & 1) + (1 - c),
        dst_of=lambda s: s, peer_of=lambda s, x, y, c: (x, y, 1 - c))


def _chip_job(sums, rows=None, chained=None):
    return _exchange_job(
        sums, 3, 3, src_of=lambda s, x, y, c: s, dst_of=lambda s: s,
        peer_of=lambda s, x, y, c: (_flip(x, (s + 1) & 2), _flip(y, (s + 1) & 1), c), rows=rows, chained=chained)


def _run_sides(sides, *, name):
    n_si, n_so = len(sides.ins), len(sides.outs)

    def body(*refs):
        ins, outs, sems = refs[:n_si], refs[n_si:n_si + n_so], refs[n_si + n_so:]
        sides.run("start", ins, outs, sems)
        sides.run("finish", ins, outs, sems)

    return pl.pallas_call(
        body, name=name, out_shape=list(sides.outs), in_specs=[ANY_SPEC] * n_si, out_specs=[ANY_SPEC] * n_so,
        scratch_shapes=list(sides.sems), input_output_aliases=sides.aliases(0, 0),
    )(*sides.ins)


def _pair_add(g, t, *, name, tr, wire):
    _, r, cols = g.shape
    g4 = g.reshape(4, 2, r, cols)
    j0 = 1 if wire else 0

    def g_index(j, i):
        x, y, c = _my_pos()
        return (jnp.bitwise_xor(2 * x + y, j + j0), c, i, 0)

    def body(g_ref, t_ref, o_ref):
        o_ref[...] = (g_ref[...] + t_ref[...]).astype(o_ref.dtype)

    return pl.pallas_call(
        body, name=name, grid=(3 if wire else 1, r // tr),
        in_specs=[pl.BlockSpec((None, None, tr, cols), g_index),
                  pl.BlockSpec((None, tr, cols), lambda j, i: (j + j0, i, 0))],
        out_specs=pl.BlockSpec((None, tr, cols), lambda j, i: (j, i, 0)),
        out_shape=jax.ShapeDtypeStruct((3 if wire else 1, r, cols), BF16 if wire else F32),
        compiler_params=_cparams(("arbitrary", "arbitrary")),
    )(g4, t)


def _config(x, ctx, w_in, w_dec_f, gla_norm_g, sg_ln_g, w_s):
    n, d = x.shape[1], x.shape[2]
    tc = ctx.shape[1]
    h = gla_norm_g.shape[1]
    dv = gla_norm_g.shape[2] * N_DEV
    dk = dv // 2
    kw, vw = h * dk, h * dv
    lr = w_dec_f.shape[1]
    sgw = sg_ln_g.shape[1]
    cfg = dict(N=n, D=d, TC=tc, H=h, DV=dv, DK=dk, KW=kw, VW=vw, LR=lr, SGW=sgw, SG_G=w_s.shape[1],
               SG_C=w_s.shape[2], IN=w_in.shape[2] * N_DEV)
    cfg.update(K0=kw, V0=2 * kw, R0=2 * kw + vw, U0=2 * kw + 2 * vw)
    cfg.update(VV0=cfg["U0"] + sgw, L0=cfg["U0"] + 2 * sgw, ZW=cfg["U0"] + 2 * sgw + LANES)
    assert dk == LANES and vw == 2 * kw and 2 * lr <= LANES
    assert cfg["R0"] % vw == 0 and cfg["U0"] % sgw == 0 and cfg["VV0"] % sgw == 0 and vw % sgw == 0
    assert cfg["IN"] == 2 * kw + 2 * vw + 2 * lr + 2 * sgw
    return cfg


def _rope_tables(cfg):
    n, tc, dk = cfg["N"], cfg["TC"], cfg["DK"]
    m = dk // 4
    pos = jnp.arange(n)
    inv = ROPE_BASE ** (-jnp.arange(m, dtype=F32) / m)
    ang_r = (pos // GRID_W).astype(F32)[:, None] * inv[None, :]
    ang_c = (pos % GRID_W).astype(F32)[:, None] * inv[None, :]
    cos = jnp.concatenate([jnp.cos(ang_r)] * 2 + [jnp.cos(ang_c)] * 2, axis=1)
    sin = jnp.concatenate([-jnp.sin(ang_r), jnp.sin(ang_r), -jnp.sin(ang_c), jnp.sin(ang_c)], axis=1)
    scale = dk ** -0.5
    z = jnp.zeros((tc, dk), F32)
    one = jnp.ones((tc, dk), F32)
    return [jnp.concatenate([cos * scale, z]), jnp.concatenate([sin * scale, z]),
            jnp.concatenate([cos, one]), jnp.concatenate([sin, z])]


def _w_in_splits(d):
    return (d * 11 // 32) // 16 * 16, d * 3 // 4


def _pair_sums(g, t, nm):
    rows_for = _tile(g.shape[1], max(8, ELEMS_PER_BLOCK // g.shape[2]), 16)
    return (_pair_add(g, t, name="rs_own_" + nm, tr=rows_for, wire=False),
            _pair_add(g, t, name="rs_wire_" + nm, tr=rows_for, wire=True))


def _local_step(x, ctx, target, mods, c_mods, w, cfg):
    n, d, tc = cfg["N"], cfg["D"], cfg["TC"]
    kw, vw, sgw, zw, lr = cfg["KW"], cfg["VW"], cfg["SGW"], cfg["ZW"], cfg["LR"]
    sh1, sc1, g1, sh2, sc2, g2 = mods
    csh1, csc1 = c_mods
    rt = n + tc
    tb = math.gcd(256, math.gcd(n, tc))
    tr = math.gcd(128, tb)
    tr_s = math.gcd(64, tb)
    fs = w["sh_1"].shape[1]
    ff = fs * N_DEV
    cs_in = w["sh_in"].shape[1]

    wg_in = _run_sides(_Sides([_ag_job([w["sh_in"]])]), name="ag_w_in")[0]
    w_full = wg_in.transpose(1, 0, 2).reshape(d, cfg["IN"])
    lf0 = 2 * kw + 2 * vw
    sg0 = lf0 + 2 * lr
    w_pad = jnp.concatenate([w_full[:, :lf0], w_full[:, sg0:], w_full[:, lf0:sg0],
                             jnp.zeros((d, LANES - 2 * lr), BF16)], axis=1)

    hx = _norm_mod(x, w["pre1_g"], sh1, sc1, name="in_norm_x", tr=tr)
    hc = _norm_mod(ctx, w["pre1_g"], csh1, csc1, name="in_norm_ctx", tr=tr)
    a_all = jnp.concatenate([hx, hc], axis=0)

    tm_a = _tile(rt, 1152, 16)
    z, wg_o, w_1 = _matmul(a_all, w_pad, "nn", rt, zw, d, tm=tm_a, tn=_tile(zw, 1152, LANES),
                           tk=_tile(d, TK, LANES), name="mm_in", out_shapes=[jax.ShapeDtypeStruct((rt, zw), F32)],
                           sides=_Sides([_ag_job([w["sh_o"]]), _ag_job([w["sh_1"]], rows=(0, d // 2))]))
    w_o = wg_o.reshape(d, d)

    tabs = _rope_tables(cfg)
    qr, kr, la_f, la_b = _gla_prep(z, tabs, w["wdf_pad"], w["wdb_pad"], w["b_dec_f"], w["b_dec_b"], cfg, tr=tr)

    zero_st = jnp.zeros((cfg["H"], cfg["DV"], cfg["DK"]), F32)
    gla = dict(cfg=cfg, tb=tb)
    _, save_cf, st_cf = _gla_fwd(qr, kr, z, la_f, zero_st, rev=False, row_off=n, nrows=tc, name="gla_ctx_f", **gla)
    _, save_cb, st_cb = _gla_fwd(qr, kr, z, la_b, zero_st, rev=True, row_off=n, nrows=tc, name="gla_ctx_b", **gla)
    o_f, save_f, _, w_1 = _gla_fwd(
        qr, kr, z, la_f, st_cf, rev=False, row_off=0, nrows=n, name="gla_f",
        sides=_Sides([_ag_job([w["sh_1"]], rows=(d // 2, d // 4), chained=[w_1])]), **gla)
    o_b, save_b, _, w_1 = _gla_fwd(
        qr, kr, z, la_b, st_cb, rev=True, row_off=0, nrows=n, name="gla_b",
        sides=_Sides([_ag_job([w["sh_1"]], rows=(3 * d // 4, d // 4), chained=[w_1])]), **gla)
    y_gla = _readout_fwd(o_f, o_b, z, w["gla_g"], cfg, tr=tr)
    y_sg = _sg_fwd(z, n, w["sg_ln_g"], w["sg_ln_b"], w["w_s"], w["bs_full"], cfg)
    ycat = jnp.concatenate([y_gla, y_sg], axis=1)

    tm_n = _tile(n, 1024, 16)
    mix, wg_2 = _matmul(ycat, w_o, "nn", n, d, d, tm=tm_n, tn=_tile(d, 1024, LANES), tk=_tile(d, TK, LANES),
                        name="mm_o", out_shapes=[jax.ShapeDtypeStruct((n, d), F32)],
                        sides=_Sides([_ag_job([w["sh_2"]], rows=(0, fs // 4))]))
    x1, h2 = _mid_fwd(x, mix, g1, w["post1_g"], w["pre2_g"], sh2, sc2, tr=tr_s)

    tn_f = _tile(fs, 1024, LANES)
    tk_d = _tile(d, TK, LANES)

    def relu2(acc):
        return acc, jnp.square(jnp.maximum(acc, 0.0))

    a1, p1, wg_2 = _matmul(h2, w_1, "nn", n, ff, d, tm=tm_n, tn=tn_f, tk=tk_d, name="mm_1",
                           b_spec=_blocked_b_nn(fs, tk_d, tn_f), epilogue=relu2,
                           out_shapes=[jax.ShapeDtypeStruct((n, ff), BF16)] * 2,
                           sides=_Sides([_ag_job([w["sh_2"]], rows=(fs // 4, fs - fs // 4), chained=[wg_2])]))
    w_2 = wg_2.reshape(ff, d)
    tk_f = _tile(ff, TK, LANES)
    m2 = _matmul(p1, w_2, "nn", n, d, ff, tm=tm_n, tn=_tile(d, 1024, LANES), tk=tk_f, name="mm_2",
                 out_shapes=[jax.ShapeDtypeStruct((n, d), F32)])[0]

    dx2, dm2, dg2, dpost2, lossc = _head_bwd(x1, m2, target, g2, w["post2_g"], tr=tr_s)

    def drelu2(acc, a):
        return (acc * (2.0 * jnp.maximum(a.astype(F32), 0.0)),)

    da1 = _matmul(dm2, w_2, "nt", n, ff, d, tm=tm_n, tn=_tile(ff, 1024, LANES), tk=tk_d, name="mm_2_dx",
                  epilogue=drelu2, extras=(a1,), out_shapes=[jax.ShapeDtypeStruct((n, ff), BF16)])[0]
    tk_n = _tile(n, TK, 16)
    tm_d = _tile(d, 1024, LANES)
    g_1 = _matmul(h2, da1, "tn", d, ff, n, tm=tm_d, tn=tn_f, tk=tk_n, name="mm_1_dw",
                  out_specs=[_blocked_out(fs, tm_d, tn_f)],
                  out_shapes=[jax.ShapeDtypeStruct((N_DEV, d, fs), F32)])[0]
    dw_2, t_1 = _matmul(p1, dm2, "tn", ff, d, n, tm=_tile(ff, 1024, LANES), tn=_tile(d, 1024, LANES), tk=tk_n,
                        name="mm_2_dw", out_shapes=[jax.ShapeDtypeStruct((ff, d), F32)],
                        sides=_Sides([_pair_job([g_1])]))
    g_2 = dw_2.reshape(N_DEV, fs, d)
    p1_own, p1_wire = _pair_sums(g_1, t_1, "w_1")
    tk_fs = _tile(fs, TK, LANES)
    dh2, u_1, t_2 = _matmul(da1, w_1, "nt", n, d, ff, tm=tm_n, tn=_tile(d, 1024, LANES), tk=tk_fs, name="mm_1_dx",
                            b_spec=_blocked_b_nt(fs, _tile(d, 1024, LANES), tk_fs),
                            out_shapes=[jax.ShapeDtypeStruct((n, d), F32)],
                            sides=_Sides([_chip_job([p1_wire], rows=(0, d * 13 // 16)), _pair_job([g_2])]))
    p2_own, p2_wire = _pair_sums(g_2, t_2, "w_2")
    rows_a = (fs * 5 // 16) // 16 * 16

    dx1, dmix, dsh2, dsc2, dpre2, dg1, dpost1, u_1 = _mid_bwd(
        dh2, x1, dx2, mix, sc2, w["pre2_g"], g1, w["post1_g"], tr=tr_s,
        sides=_Sides([_chip_job([p1_wire], rows=(d * 13 // 16, d - d * 13 // 16), chained=[u_1])]))
    dw_o, u_2 = _matmul(ycat, dmix, "tn", d, d, n, tm=tm_d, tn=_tile(d, 1024, LANES), tk=tk_n, name="mm_o_dw",
                        out_shapes=[jax.ShapeDtypeStruct((d, d), F32)],
                        sides=_Sides([_chip_job([p2_wire], rows=(0, rows_a))]))
    g_o = dw_o.reshape(N_DEV, d // N_DEV, d)
    dycat, t_o = _matmul(dmix, w_o, "nt", n, d, d, tm=tm_n, tn=_tile(d, 1024, LANES), tk=tk_d, name="mm_o_dx",
                         out_shapes=[jax.ShapeDtypeStruct((n, d), F32)], sides=_Sides([_pair_job([g_o])]))
    po_own, po_wire = _pair_sums(g_o, t_o, "w_o")

    dzu, dzvv, dws, dbs_acc, dlng, dlnb = _sg_bwd(z, dycat, n, w["sg_ln_g"], w["sg_ln_b"], w["w_s"],
                                                  w["bs_full"], cfg)
    do, dzr, dgla_g = _readout_bwd(o_f, o_b, z, dycat, w["gla_g"], cfg, tr=tr)

    gf = _gla_bwd(qr, kr, z, la_f, do, save_f, zero_st, rev=False, row_off=0, nrows=n, name="gla_f_bwd", **gla)
    gb = _gla_bwd(qr, kr, z, la_b, do, save_b, zero_st, rev=True, row_off=0, nrows=n, name="gla_b_bwd", **gla)
    do_c = jnp.zeros((tc, vw), BF16)
    gcf = _gla_bwd(qr, kr, z, la_f, do_c, save_cf, gf[4], rev=False, row_off=n, nrows=tc, name="gla_ctx_f_bwd",
                   **gla)
    gcb = _gla_bwd(qr, kr, z, la_b, do_c, save_cb, gb[4], rev=True, row_off=n, nrows=tc, name="gla_ctx_b_bwd",
                   **gla)

    post = dict(la_f=la_f, la_b=la_b, z=z, tabs=tabs, wdf_pad=w["wdf_pad"], wdb_pad=w["wdb_pad"], cfg=cfg, tr=tr)
    dzq, dzk, dzv, dzl, dwdf, dwdb, dbdf, dbdb = _gla_post(gf, gb, row_off=0, nrows=n, name="gla_post", **post)
    czq, czk, czv, czl, cwdf, cwdb, cbdf, cbdb = _gla_post(gcf, gcb, row_off=n, nrows=tc, name="gla_post_ctx",
                                                           **post)
    zc = lambda wd: jnp.zeros((tc, wd), BF16)
    dz = jnp.concatenate([
        jnp.concatenate([dzq, dzk, dzv, dzr, dzu, dzvv, dzl], axis=1),
        jnp.concatenate([czq, czk, czv, zc(vw), zc(sgw), zc(sgw), czl], axis=1)], axis=0)

    dw_pad, u_2 = _matmul(a_all, dz, "tn", d, zw, rt, tm=tm_d, tn=_tile(zw, 1152, LANES), tk=_tile(rt, 2176, 16),
                          name="mm_in_dw", out_shapes=[jax.ShapeDtypeStruct((d, zw), F32)],
                          sides=_Sides([_chip_job([p2_wire], rows=(rows_a, fs - rows_a), chained=[u_2])]))
    dw_in = jnp.concatenate([dw_pad[:, :cfg["U0"]], dw_pad[:, cfg["L0"]:cfg["L0"] + 2 * lr],
                             dw_pad[:, cfg["U0"]:cfg["L0"]]], axis=1)
    g_in = dw_in.reshape(d, N_DEV, cs_in).transpose(1, 0, 2)
    da_all, u_o, t_in = _matmul(dz, w_pad, "nt", rt, d, zw, tm=_tile(rt, 576, 16), tn=_tile(d, 1024, LANES),
                                tk=_tile(zw, 3456, LANES), name="mm_in_dx",
                                out_shapes=[jax.ShapeDtypeStruct((rt, d), F32)],
                                sides=_Sides([_chip_job([po_wire]), _pair_job([g_in])]))
    pin_own, pin_wire = _pair_sums(g_in, t_in, "w_in")

    grad_x, dsh1, dsc1, dpre1, u_in = _in_bwd(
        da_all, x, dx1, sc1, w["pre1_g"], row_off=0, tr=tr_s, name="in_bwd_x",
        sides=_Sides([_chip_job([pin_wire], rows=(0, _w_in_splits(d)[0]))]))
    dcsh1, dcsc1, dpre1_c = _in_bwd(da_all, ctx, None, csc1, w["pre1_g"], row_off=n, tr=tr_s, name="in_bwd_ctx")

    small = dict(
        pre1_g=dpre1 + dpre1_c, post1_g=dpost1, pre2_g=dpre2, post2_g=dpost2,
        w_dec_f=(dwdf + cwdf)[:lr], w_dec_b=(dwdb + cwdb)[lr:2 * lr], b_dec_f=dbdf + cbdf, b_dec_b=dbdb + cbdb,
        gla_norm_g=dgla_g, sg_ln_g=dlng, sg_ln_b=dlnb, w_s=dws,
        b_s=dbs_acc.reshape(cfg["SG_C"], cfg["SG_G"], sgw // cfg["SG_G"]).sum(-1).T)
    dmod = jnp.concatenate([dsh1, dsc1, dg1, dsh2, dsc2, dg2], axis=1)
    dmod_c = jnp.concatenate([dcsh1, dcsc1], axis=1)
    big = dict(w_in=(pin_own, u_in, pin_wire), w_o=(po_own, u_o), w_1=(p1_own, u_1), w_2=(p2_own, u_2))
    return lossc, grad_x, big, small, dmod, dmod_c


SMALL_NAMES = ["b_ada", "pre1_g", "post1_g", "pre2_g", "post2_g", "w_dec_f", "b_dec_f", "w_dec_b", "b_dec_b",
               "gla_norm_g", "sg_ln_g", "sg_ln_b", "w_s", "b_s", "c_ctx"]
WEIGHT_ORDER = ["c_ctx", "w_ada", "b_ada", "pre1_g", "post1_g", "pre2_g", "post2_g", "w_in", "w_dec_f", "b_dec_f",
                "w_dec_b", "b_dec_b", "gla_norm_g", "sg_ln_g", "sg_ln_b", "w_s", "b_s", "w_o", "w_1", "w_2"]


def kernel(x, c, ctx, c_ctx, w_ada, b_ada, pre1_g, post1_g, pre2_g, post2_g, w_in, w_dec_f, b_dec_f, w_dec_b, b_dec_b, gla_norm_g, sg_ln_g, sg_ln_b, w_s, b_s, w_o, w_1, w_2, loss_target, m_c_ctx, m_w_ada, m_b_ada, m_pre1_g, m_post1_g, m_pre2_g, m_post2_g, m_w_in, m_w_dec_f, m_b_dec_f, m_w_dec_b, m_b_dec_b, m_gla_norm_g, m_sg_ln_g, m_sg_ln_b, m_w_s, m_b_s, m_w_o, m_w_1, m_w_2, v_c_ctx, v_w_ada, v_b_ada, v_pre1_g, v_post1_g, v_pre2_g, v_post2_g, v_w_in, v_w_dec_f, v_b_dec_f, v_w_dec_b, v_b_dec_b, v_gla_norm_g, v_sg_ln_g, v_sg_ln_b, v_w_s, v_b_s, v_w_o, v_w_1, v_w_2):
    weights = dict(c_ctx=c_ctx, w_ada=w_ada, b_ada=b_ada, pre1_g=pre1_g, post1_g=post1_g, pre2_g=pre2_g,
                   post2_g=post2_g, w_in=w_in, w_dec_f=w_dec_f, b_dec_f=b_dec_f, w_dec_b=w_dec_b, b_dec_b=b_dec_b,
                   gla_norm_g=gla_norm_g, sg_ln_g=sg_ln_g, sg_ln_b=sg_ln_b, w_s=w_s, b_s=b_s, w_o=w_o, w_1=w_1,
                   w_2=w_2)
    mom_m = dict(c_ctx=m_c_ctx, w_ada=m_w_ada, b_ada=m_b_ada, pre1_g=m_pre1_g, post1_g=m_post1_g, pre2_g=m_pre2_g,
                 post2_g=m_post2_g, w_in=m_w_in, w_dec_f=m_w_dec_f, b_dec_f=m_b_dec_f, w_dec_b=m_w_dec_b,
                 b_dec_b=m_b_dec_b, gla_norm_g=m_gla_norm_g, sg_ln_g=m_sg_ln_g, sg_ln_b=m_sg_ln_b, w_s=m_w_s,
                 b_s=m_b_s, w_o=m_w_o, w_1=m_w_1, w_2=m_w_2)
    mom_v = dict(c_ctx=v_c_ctx, w_ada=v_w_ada, b_ada=v_b_ada, pre1_g=v_pre1_g, post1_g=v_post1_g, pre2_g=v_pre2_g,
                 post2_g=v_post2_g, w_in=v_w_in, w_dec_f=v_w_dec_f, b_dec_f=v_b_dec_f, w_dec_b=v_w_dec_b,
                 b_dec_b=v_b_dec_b, gla_norm_g=v_gla_norm_g, sg_ln_g=v_sg_ln_g, sg_ln_b=v_sg_ln_b, w_s=v_w_s,
                 b_s=v_b_s, w_o=v_w_o, w_1=v_w_1, w_2=v_w_2)

    cfg = _config(x, ctx, w_in, w_dec_f, gla_norm_g, sg_ln_g, w_s)
    n, d, h, dv, kw, vw, lr, sgw = (cfg[k] for k in ("N", "D", "H", "DV", "KW", "VW", "LR", "SGW"))
    dvs, kws = dv // N_DEV, kw // N_DEV
    ix, iy, ic = _my_pos()
    me = 4 * ix + 2 * iy + ic

    pack1 = jnp.concatenate([c.reshape(1, d), w_dec_f.reshape(1, lr * kws), w_dec_b.reshape(1, lr * kws),
                             gla_norm_g.reshape(1, h * dvs)], axis=1)
    g1 = _all_gather_vec(pack1, name="ag_small_in")
    c_all = g1[:, :d]
    o1 = d
    wdf = g1[:, o1:o1 + lr * kws].reshape(N_DEV, lr, kws).transpose(1, 0, 2).reshape(lr, kw)
    o1 += lr * kws
    wdb = g1[:, o1:o1 + lr * kws].reshape(N_DEV, lr, kws).transpose(1, 0, 2).reshape(lr, kw)
    o1 += lr * kws
    gla_g = g1[:, o1:o1 + h * dvs].reshape(N_DEV, h, dvs).transpose(1, 0, 2).reshape(1, h * dv)

    c16 = jnp.concatenate([c_all, jnp.broadcast_to(c_ctx.reshape(1, d), (N_DEV, d))], axis=0)
    ncol = w_ada.shape[2]
    wa = w_ada.reshape(d, ncol)
    b_mine = lax.dynamic_slice(b_ada, (0, me * ncol), (1, ncol))
    tn_ada = _tile(ncol, 512, LANES)
    mod_mine = _ada_fwd(c16, wa, b_mine, tn=tn_ada)
    mod_all = _all_gather_small(mod_mine, name="ag_mod").transpose(1, 0, 2).reshape(16, N_DEV * ncol)
    mod_b = lax.dynamic_slice(mod_all, (me, 0), (1, 6 * d))
    mods = [mod_b[:, i * d:(i + 1) * d] for i in range(6)]
    c_mods = [mod_all[N_DEV:N_DEV + 1, :d], mod_all[N_DEV:N_DEV + 1, d:2 * d]]

    zpad = lambda r: jnp.zeros((r, kw), F32)
    w = dict(
        sh_in=w_in.reshape(d, w_in.shape[2]).astype(BF16), sh_o=w_o.reshape(w_o.shape[1], d).astype(BF16),
        sh_1=w_1.reshape(d, w_1.shape[2]).astype(BF16), sh_2=w_2.reshape(w_2.shape[1], d).astype(BF16),
        pre1_g=pre1_g, post1_g=post1_g, pre2_g=pre2_g, post2_g=post2_g, b_dec_f=b_dec_f, b_dec_b=b_dec_b,
        wdf_pad=jnp.concatenate([wdf, zpad(LANES - lr)], axis=0),
        wdb_pad=jnp.concatenate([zpad(lr), wdb, zpad(LANES - 2 * lr)], axis=0),
        gla_g=gla_g, sg_ln_g=sg_ln_g, sg_ln_b=sg_ln_b, w_s=w_s[0],
        bs_full=jnp.repeat(b_s[0].T, sgw // cfg["SG_G"], axis=1))

    lossc, grad_x, big, small, dmod, dmod_c = _local_step(x[0], ctx[0], loss_target[0], mods, c_mods, w, cfg)
    loss = lax.psum(jnp.sum(lossc), AXES)

    order3 = ["pre1_g", "post1_g", "pre2_g", "post2_g", "w_dec_f", "b_dec_f", "w_dec_b", "b_dec_b", "gla_norm_g",
              "sg_ln_g", "sg_ln_b", "w_s", "b_s"]
    pieces = [dmod, dmod_c] + [small[k].reshape(1, -1) for k in order3]
    sizes = [p.shape[1] for p in pieces]
    g3 = _all_gather_vec(jnp.concatenate(pieces, axis=1), name="ag_small_grads")
    offs = [0]
    for s in sizes:
        offs.append(offs[-1] + s)
    dmod_all = g3[:, :6 * d]
    dmod_c_all = jnp.pad(g3[:, offs[1]:offs[2]], ((0, 0), (0, 4 * d)))
    parts8 = {k: g3[:, offs[2 + i]:offs[3 + i]] for i, k in enumerate(order3)}
    parts8["b_ada"] = dmod_all + dmod_c_all
    parts8["w_dec_f"] = lax.dynamic_slice(parts8["w_dec_f"].reshape(N_DEV, lr, kw), (0, 0, me * kws),
                                          (N_DEV, lr, kws)).reshape(N_DEV, -1)
    parts8["w_dec_b"] = lax.dynamic_slice(parts8["w_dec_b"].reshape(N_DEV, lr, kw), (0, 0, me * kws),
                                          (N_DEV, lr, kws)).reshape(N_DEV, -1)
    parts8["gla_norm_g"] = lax.dynamic_slice(parts8["gla_norm_g"].reshape(N_DEV, h, dv), (0, 0, me * dvs),
                                             (N_DEV, h, dvs)).reshape(N_DEV, -1)

    dm16 = jnp.concatenate([dmod_all, dmod_c_all], axis=0)
    dm_mine = lax.dynamic_slice(dm16, (0, me * ncol), (16, ncol))
    g_w_ada, dcc = _ada_bwd(c16, dm_mine, wa, c_ctx.reshape(1, d), tn=tn_ada)
    parts8["c_ctx"] = _all_gather_vec(dcc, name="ag_cctx")

    flat = lambda t: t.reshape(1, -1)
    g8 = _dense(jnp.concatenate([parts8[k] for k in SMALL_NAMES], axis=1))
    ws, ms, vs = [_dense(jnp.concatenate([flat(src[k]) for k in SMALL_NAMES], axis=1))[0]
                  for src in (weights, mom_m, mom_v)]
    res_small = [r.reshape(1, -1) for r in _adam_small(g8, ws, ms, vs)]
    out = {}
    off = 0
    for k in SMALL_NAMES:
        sz = weights[k].size
        out[k] = [r[:, off:off + sz].reshape(weights[k].shape) for r in res_small]
        off += sz

    rows_for = lambda r, cols: _tile(r, max(8, ELEMS_PER_BLOCK // cols), 16)
    pin_own, u_in, pin_wire = big["w_in"]
    ra, rb = _w_in_splits(d)
    *res, u_in = _adam_big([(g_w_ada, None)], w_ada, m_w_ada, v_w_ada, name="adam_w_ada", tr=rows_for(d, ncol),
                           sides=_Sides([_chip_job([pin_wire], rows=(ra, rb - ra), chained=[u_in])]))
    out["w_ada"] = res
    big["w_in"] = (pin_own, u_in)
    for nm in ("w_2", "w_1", "w_o", "w_in"):
        own, u = big[nm]
        shp = weights[nm].shape
        sides = _Sides([_chip_job([pin_wire], rows=(rb, d - rb), chained=[big["w_in"][1]])]) if nm == "w_2" else _NO_SIDES
        res = _adam_big([(own, 0), (u, 0), (u, 1), (u, 2)], weights[nm], mom_m[nm], mom_v[nm], name="adam_" + nm,
                        tr=rows_for(shp[1], shp[2]), sides=sides)
        if nm == "w_2":
            big["w_in"] = (pin_own, res[4])
        out[nm] = res[:4]

    outs = [loss, grad_x[None]]
    for i in range(4):
        outs += [out[k][i] for k in WEIGHT_ORDER]
    return tuple(outs)
```

```python
import math

import jax
import jax.numpy as jnp
from jax import lax
from jax.experimental import pallas as pl
from jax.experimental.pallas import tpu as pltpu

F32 = jnp.float32
BF16 = jnp.bfloat16
MXU_DTYPE = jnp.bfloat16
HI = lax.Precision.HIGHEST

N_DEV = 8
AXES = ("x", "y", "c")
MESH = pl.DeviceIdType.MESH
LANES = 128
VMEM_LIMIT = 56 * 1024 * 1024

EPS = 1e-6
GRID_W = 64
GLA_CHUNK = 64
GLA_TAU = 16.0
ROPE_BASE = 10000.0
ADAM_LR = 0.001
ADAM_B1 = 0.9
ADAM_B2 = 0.999
ADAM_EPS = 1e-08
ADAM_WD = 0.01
ADAM_STEP = 10


def _cparams(sem):
    return pltpu.CompilerParams(dimension_semantics=sem, vmem_limit_bytes=VMEM_LIMIT)


def _tile(n, target, align):
    if n <= target:
        return n
    best = None
    for t in range(align, target + 1, align):
        if n % t == 0:
            best = t
    assert best is not None, (n, target, align)
    return best


def _dg(a, b, dims, prec=None):
    return lax.dot_general(a, b, (dims, ((), ())), precision=prec, preferred_element_type=F32)


def _nn(a, b):
    return _dg(a.astype(MXU_DTYPE), b.astype(MXU_DTYPE), ((1,), (0,)))


def _nt(a, b):
    return _dg(a.astype(MXU_DTYPE), b.astype(MXU_DTYPE), ((1,), (1,)))


def _tn(a, b):
    return _dg(a.astype(MXU_DTYPE), b.astype(MXU_DTYPE), ((0,), (0,)))


def _sigmoid(x):
    return 1.0 / (1.0 + jnp.exp(-x))


def _silu(x):
    return x * _sigmoid(x)


def _dsilu(x):
    s = _sigmoid(x)
    return s * (1.0 + x * (1.0 - s))


def _gelu(x):
    return 0.5 * x * (1.0 + lax.erf(x * (1.0 / math.sqrt(2.0))))


def _dgelu(x):
    return 0.5 * (1.0 + lax.erf(x * (1.0 / math.sqrt(2.0)))) + x * jnp.exp(-0.5 * x * x) * (1.0 / math.sqrt(2.0 * math.pi))


def _rstd(x):
    return lax.rsqrt(jnp.mean(x * x, axis=-1, keepdims=True) + EPS)


def _rms_bwd(x, r, dn):
    return r * dn - x * (r * r * r) * jnp.mean(dn * x, axis=-1, keepdims=True)


def _colsum(x):
    return jnp.sum(x, axis=0, keepdims=True)


class _Sides:
    def __init__(self, jobs):
        self.jobs = list(jobs)
        self.ins = [a for j in self.jobs for a in j["ins"]]
        self.outs = [o for j in self.jobs for o in j["outs"]]
        self.sems = [s for j in self.jobs for s in j["sems"]]

    def aliases(self, in_base, out_base):
        res, oi, oo = {}, 0, 0
        for j in self.jobs:
            for a, b in j.get("alias", {}).items():
                res[in_base + oi + a] = out_base + oo + b
            oi += len(j["ins"])
            oo += len(j["outs"])
        return res

    def run(self, phase, in_refs, out_refs, sem_refs):
        oi = oo = os_ = 0
        for j in self.jobs:
            ni, no, ns = len(j["ins"]), len(j["outs"]), len(j["sems"])
            j[phase](in_refs[oi:oi + ni], out_refs[oo:oo + no], sem_refs[os_:os_ + ns])
            oi, oo, os_ = oi + ni, oo + no, os_ + ns


_NO_SIDES = _Sides([])
ANY_SPEC = pl.BlockSpec(memory_space=pl.ANY)


def _matmul(a, b, mode, m, n, k, *, tm, tn, tk, name, out_shapes, b_spec=None, out_specs=None,
            epilogue=None, extras=(), sides=_NO_SIDES):
    nk = k // tk
    assert m % tm == 0 and n % tn == 0 and k % tk == 0, (name, m, n, k, tm, tn, tk)
    dot = {"nn": _nn, "nt": _nt, "tn": _tn}[mode]
    if mode == "tn":
        a_spec = pl.BlockSpec((tk, tm), lambda i, j, kk: (kk, i))
    else:
        a_spec = pl.BlockSpec((tm, tk), lambda i, j, kk: (i, kk))
    if b_spec is None:
        if mode == "nt":
            b_spec = pl.BlockSpec((tn, tk), lambda i, j, kk: (j, kk))
        else:
            b_spec = pl.BlockSpec((tk, tn), lambda i, j, kk: (kk, j))
    mn_spec = pl.BlockSpec((tm, tn), lambda i, j, kk: (i, j))
    if out_specs is None:
        out_specs = [mn_spec] * len(out_shapes)
    n_extra = len(extras)
    n_out = len(out_shapes)
    n_si, n_so = len(sides.ins), len(sides.outs)
    ni, nj = m // tm, n // tn

    def body(a_ref, b_ref, *rest):
        extra_refs = rest[:n_extra]
        rest = rest[n_extra:]
        side_in, rest = rest[:n_si], rest[n_si:]
        out_refs, rest = rest[:n_out], rest[n_out:]
        side_out, rest = rest[:n_so], rest[n_so:]
        acc, side_sems = rest[0], rest[1:]
        i, j, kk = pl.program_id(0), pl.program_id(1), pl.program_id(2)

        if sides.jobs:
            @pl.when((i == 0) & (j == 0) & (kk == 0))
            def _():
                sides.run("start", side_in, side_out, side_sems)

        @pl.when(kk == 0)
        def _():
            acc[...] = jnp.zeros_like(acc)

        acc[...] += dot(a_ref[...], b_ref[...])

        @pl.when(kk == nk - 1)
        def _():
            vals = (acc[...],) if epilogue is None else epilogue(acc[...], *[e[...] for e in extra_refs])
            for o, v in zip(out_refs, vals):
                o[...] = v.astype(o.dtype)

        if sides.jobs:
            @pl.when((i == ni - 1) & (j == nj - 1) & (kk == nk - 1))
            def _():
                sides.run("finish", side_in, side_out, side_sems)

    sem = ("arbitrary",) * 3 if sides.jobs else ("parallel", "parallel", "arbitrary")
    res = pl.pallas_call(
        body, name=name, grid=(ni, nj, nk),
        in_specs=[a_spec, b_spec] + [mn_spec] * n_extra + [ANY_SPEC] * n_si,
        out_specs=list(out_specs) + [ANY_SPEC] * n_so, out_shape=list(out_shapes) + list(sides.outs),
        scratch_shapes=[pltpu.VMEM((tm, tn), F32)] + list(sides.sems),
        input_output_aliases=sides.aliases(2 + n_extra, n_out),
        compiler_params=_cparams(sem),
    )(a, b, *extras, *sides.ins)
    return res


def _blocked_b_nn(ns, tk, tn):
    assert ns % tn == 0
    return pl.BlockSpec((None, tk, tn), lambda i, j, kk: ((j * tn) // ns, kk, ((j * tn) % ns) // tn))


def _blocked_b_nt(ks, tn, tk):
    assert ks % tk == 0
    return pl.BlockSpec((None, tn, tk), lambda i, j, kk: ((kk * tk) // ks, j, ((kk * tk) % ks) // tk))


def _blocked_out(ns, tm, tn):
    assert ns % tn == 0
    return pl.BlockSpec((None, tm, tn), lambda i, j, kk: ((j * tn) // ns, i, ((j * tn) % ns) // tn))


def _rows_call(body, *, name, nblk, tr, row_ins, consts, row_outs, accs=(), sides=_NO_SIDES):
    n_ri, n_c, n_ro, n_acc = len(row_ins), len(consts), len(row_outs), len(accs)
    n_si, n_so = len(sides.ins), len(sides.outs)

    def kern(*refs):
        i = pl.program_id(0)
        rin, refs = refs[:n_ri], refs[n_ri:]
        cin, refs = refs[:n_c], refs[n_c:]
        side_in, refs = refs[:n_si], refs[n_si:]
        rout, refs = refs[:n_ro], refs[n_ro:]
        acc, refs = refs[:n_acc], refs[n_acc:]
        side_out, side_sems = refs[:n_so], refs[n_so:]

        if sides.jobs:
            @pl.when(i == 0)
            def _():
                sides.run("start", side_in, side_out, side_sems)

        if n_acc:
            @pl.when(i == 0)
            def _():
                for r in acc:
                    r[...] = jnp.zeros_like(r)

        body(rin, cin, rout, acc)

        if sides.jobs:
            @pl.when(i == nblk - 1)
            def _():
                sides.run("finish", side_in, side_out, side_sems)

    in_specs = [pl.BlockSpec((tr, w), lambda i, ro=ro, co=co: (i + ro, co)) for (_, w, ro, co) in row_ins]
    in_specs += [pl.BlockSpec(cst.shape, lambda i, nd=cst.ndim: (0,) * nd) for cst in consts]
    out_specs = [pl.BlockSpec((tr, w), lambda i: (i, 0)) for (_, w, _) in row_outs]
    out_specs += [pl.BlockSpec(s, lambda i, nd=len(s): (0,) * nd) for s in accs]
    out_shape = [jax.ShapeDtypeStruct((r, w), dt) for (r, w, dt) in row_outs]
    out_shape += [jax.ShapeDtypeStruct(s, F32) for s in accs]
    return pl.pallas_call(
        kern, name=name, grid=(nblk,), in_specs=in_specs + [ANY_SPEC] * n_si,
        out_specs=out_specs + [ANY_SPEC] * n_so, out_shape=out_shape + list(sides.outs),
        scratch_shapes=list(sides.sems),
        input_output_aliases=sides.aliases(n_ri + n_c, n_ro + n_acc),
        compiler_params=_cparams(("arbitrary",)),
    )(*[r[0] for r in row_ins], *consts, *sides.ins)


def _norm_mod(x, g, shift, scale, *, name, tr):
    rows, d = x.shape

    def body(rin, cin, rout, acc):
        xv = rin[0][...]
        n = xv * _rstd(xv) * cin[0][...]
        rout[0][...] = (n * (1.0 + cin[2][...]) + cin[1][...]).astype(BF16)

    return _rows_call(body, name=name, nblk=rows // tr, tr=tr, row_ins=[(x, d, 0, 0)],
                      consts=[g, shift, scale], row_outs=[(rows, d, BF16)])[0]


def _swap_halves(t, width):
    lane = lax.broadcasted_iota(jnp.int32, t.shape, 1)
    return jnp.where(lane % 64 < 32, pltpu.roll(t, width - 32, 1), pltpu.roll(t, 32, 1))


def _gla_prep(z, tabs, wdf_pad, wdb_pad, bdf, bdb, cfg, *, tr):
    rows = z.shape[0]
    kw, h = cfg["KW"], cfg["H"]

    def body(rin, cin, rout, acc):
        zq, zk, zl = rin[0][...], rin[1][...], rin[2][...]
        cq, sq, ck, sk = [jnp.concatenate([rin[3 + t][...]] * h, axis=1) for t in range(4)]
        rout[0][...] = zq * cq + _swap_halves(zq, kw) * sq
        rout[1][...] = zk * ck + _swap_halves(zk, kw) * sk
        for o, w, b in ((2, cin[0], cin[2]), (3, cin[1], cin[3])):
            a = _nn(zl, w[...]) + b[...]
            rout[o][...] = (jnp.minimum(a, 0.0) - jnp.log(1.0 + jnp.exp(-jnp.abs(a)))) * (1.0 / GLA_TAU)

    row_ins = [(z, kw, 0, 0), (z, kw, 0, 1), (z, LANES, 0, cfg["L0"] // LANES)]
    row_ins += [(t, LANES, 0, 0) for t in tabs]
    return _rows_call(body, name="gla_prep", nblk=rows // tr, tr=tr, row_ins=row_ins,
                      consts=[wdf_pad, wdb_pad, bdf, bdb], row_outs=[(rows, kw, F32)] * 4)


def _chunk_consts(rev):
    c = GLA_CHUNK
    r = lax.broadcasted_iota(jnp.int32, (c, c), 0)
    cc = lax.broadcasted_iota(jnp.int32, (c, c), 1)
    keep = (cc >= r) if rev else (cc <= r)
    return keep, keep.astype(F32)


def _heads_per_step(cfg):
    hb = 2 if cfg["H"] % 2 == 0 else 1
    assert cfg["V0"] % (hb * cfg["DV"]) == 0
    return hb


def _chunk_decay(la, keep_f):
    b = _dg(keep_f, la, ((1,), (0,)), HI)
    return b, _colsum(la)


def _gla_fwd(qr, kr, z, la, st0, cfg, *, rev, row_off, nrows, tb, name, sides=_NO_SIDES):
    h, dk, dv = cfg["H"], cfg["DK"], cfg["DV"]
    c = GLA_CHUNK
    nsub = tb // c
    nblk = nrows // tb
    roff = row_off // tb
    hb = _heads_per_step(cfg)
    v_cb = cfg["V0"] // (hb * dv)
    n_si, n_so = len(sides.ins), len(sides.outs)

    def blk(j):
        return (nblk - 1 - j) if rev else j

    def body(q_ref, k_ref, v_ref, la_ref, st0_ref, *rest):
        side_in, rest = rest[:n_si], rest[n_si:]
        o_ref, save_ref, fin_ref = rest[:3]
        side_out, st, side_sems = rest[3:3 + n_so], rest[3 + n_so], rest[4 + n_so:]
        hh, j = pl.program_id(0), pl.program_id(1)

        if sides.jobs:
            @pl.when((hh == 0) & (j == 0))
            def _():
                sides.run("start", side_in, side_out, side_sems)

        @pl.when(j == 0)
        def _():
            st[...] = st0_ref[...]

        keep, keep_f = _chunk_consts(rev)
        order = range(nsub - 1, -1, -1) if rev else range(nsub)
        heads = range(hb)
        ksl = [slice(g * dk, (g + 1) * dk) for g in heads]
        vsl = [slice(g * dv, (g + 1) * dv) for g in heads]
        state = [st[g] for g in heads]
        for s in order:
            rs = pl.ds(s * c, c)
            q = [q_ref[rs, ksl[g]] for g in heads]
            k = [k_ref[rs, ksl[g]] for g in heads]
            v = [v_ref[rs, vsl[g]] for g in heads]
            bb = [_chunk_decay(la_ref[rs, ksl[g]], keep_f) for g in heads]
            qe = [q[g] * jnp.exp(bb[g][0]) for g in heads]
            ke = [k[g] * jnp.exp(-bb[g][0]) for g in heads]
            kl = [k[g] * jnp.exp(bb[g][1] - bb[g][0]) for g in heads]
            att = [jnp.where(keep, _nt(qe[g], ke[g]), 0.0) for g in heads]
            out = [_nt(qe[g], state[g]) + _nn(att[g], v[g]) for g in heads]
            new = [state[g] * jnp.exp(bb[g][1]) + _tn(v[g], kl[g]) for g in heads]
            for g in heads:
                save_ref[g, s] = state[g]
                o_ref[rs, vsl[g]] = out[g]
            state = new
        for g in heads:
            st[g] = state[g]

        @pl.when(j == nblk - 1)
        def _():
            fin_ref[...] = st[...]

        if sides.jobs:
            @pl.when((hh == h // hb - 1) & (j == nblk - 1))
            def _():
                sides.run("finish", side_in, side_out, side_sems)

    in_specs = [
        pl.BlockSpec((tb, hb * dk), lambda hh, j: (roff + blk(j), hh)),
        pl.BlockSpec((tb, hb * dk), lambda hh, j: (roff + blk(j), hh)),
        pl.BlockSpec((tb, hb * dv), lambda hh, j: (roff + blk(j), v_cb + hh)),
        pl.BlockSpec((tb, hb * dk), lambda hh, j: (roff + blk(j), hh)),
        pl.BlockSpec((hb, dv, dk), lambda hh, j: (hh, 0, 0)),
    ]
    out_specs = [
        pl.BlockSpec((tb, hb * dv), lambda hh, j: (blk(j), hh)),
        pl.BlockSpec((hb, nsub, dv, dk), lambda hh, j: (hh, blk(j), 0, 0)),
        pl.BlockSpec((hb, dv, dk), lambda hh, j: (hh, 0, 0)),
    ]
    out_shape = [
        jax.ShapeDtypeStruct((nrows, h * dv), F32),
        jax.ShapeDtypeStruct((h, nrows // c, dv, dk), F32),
        jax.ShapeDtypeStruct((h, dv, dk), F32),
    ]
    return pl.pallas_call(
        body, name=name, grid=(h // hb, nblk), in_specs=in_specs + [ANY_SPEC] * n_si,
        out_specs=out_specs + [ANY_SPEC] * n_so, out_shape=out_shape + list(sides.outs),
        scratch_shapes=[pltpu.VMEM((hb, dv, dk), F32)] + list(sides.sems),
        input_output_aliases=sides.aliases(5, 3),
        compiler_params=_cparams(("arbitrary", "arbitrary")),
    )(qr, kr, z, la, st0, *sides.ins)


def _gla_bwd(qr, kr, z, la, do, save, dst_init, cfg, *, rev, row_off, nrows, tb, name):
    h, dk, dv = cfg["H"], cfg["DK"], cfg["DV"]
    c = GLA_CHUNK
    nsub = tb // c
    nblk = nrows // tb
    roff = row_off // tb
    hb = _heads_per_step(cfg)
    v_cb = cfg["V0"] // (hb * dv)

    def blk(j):
        return j if rev else (nblk - 1 - j)

    def body(q_ref, k_ref, v_ref, la_ref, do_ref, save_ref, di_ref, dq_ref, dk_ref, dv_ref, dla_ref, d0_ref, dst):
        j = pl.program_id(1)

        @pl.when(j == 0)
        def _():
            dst[...] = di_ref[...]

        keep, keep_f = _chunk_consts(rev)
        keep_t = _chunk_consts(not rev)[1]
        order = range(nsub) if rev else range(nsub - 1, -1, -1)
        heads = range(hb)
        ksl = [slice(g * dk, (g + 1) * dk) for g in heads]
        vsl = [slice(g * dv, (g + 1) * dv) for g in heads]
        d_after = [dst[g] for g in heads]
        for s in order:
            rs = pl.ds(s * c, c)
            q = [q_ref[rs, ksl[g]] for g in heads]
            k = [k_ref[rs, ksl[g]] for g in heads]
            v = [v_ref[rs, vsl[g]] for g in heads]
            lac = [la_ref[rs, ksl[g]] for g in heads]
            dout = [do_ref[rs, vsl[g]] for g in heads]
            s_in = [save_ref[g, s] for g in heads]
            bb = [_chunk_decay(lac[g], keep_f) for g in heads]
            eb = [jnp.exp(bb[g][0]) for g in heads]
            enb = [jnp.exp(-bb[g][0]) for g in heads]
            elb = [jnp.exp(bb[g][1] - bb[g][0]) for g in heads]
            etot = [jnp.exp(bb[g][1]) for g in heads]
            qe = [q[g] * eb[g] for g in heads]
            ke = [k[g] * enb[g] for g in heads]
            kl = [k[g] * elb[g] for g in heads]
            att = [jnp.where(keep, _nt(qe[g], ke[g]), 0.0) for g in heads]
            datt = [jnp.where(keep, _nt(dout[g], v[g]), 0.0) for g in heads]
            dqe = [_nn(dout[g], s_in[g]) + _nn(datt[g], ke[g]) for g in heads]
            dke = [_tn(datt[g], qe[g]) for g in heads]
            dkl = [_nn(v[g], d_after[g]) for g in heads]
            dvv = [_tn(att[g], dout[g]) + _nt(kl[g], d_after[g]) for g in heads]
            db = [dqe[g] * qe[g] - dke[g] * ke[g] - dkl[g] * kl[g] for g in heads]
            dbtot = [_colsum(dkl[g] * kl[g]) + _colsum(d_after[g] * s_in[g]) * etot[g] for g in heads]
            dla = [_dg(keep_t, db[g], ((1,), (0,)), HI) + dbtot[g] for g in heads]
            d_after = [d_after[g] * etot[g] + _tn(dout[g], qe[g]) for g in heads]
            for g in heads:
                dv_ref[rs, vsl[g]] = dvv[g]
                dla_ref[rs, ksl[g]] = dla[g]
                dq_ref[rs, ksl[g]] = dqe[g] * eb[g]
                dk_ref[rs, ksl[g]] = dke[g] * enb[g] + dkl[g] * elb[g]
        for g in heads:
            dst[g] = d_after[g]

        @pl.when(j == nblk - 1)
        def _():
            d0_ref[...] = dst[...]

    in_specs = [
        pl.BlockSpec((tb, hb * dk), lambda hh, j: (roff + blk(j), hh)),
        pl.BlockSpec((tb, hb * dk), lambda hh, j: (roff + blk(j), hh)),
        pl.BlockSpec((tb, hb * dv), lambda hh, j: (roff + blk(j), v_cb + hh)),
        pl.BlockSpec((tb, hb * dk), lambda hh, j: (roff + blk(j), hh)),
        pl.BlockSpec((tb, hb * dv), lambda hh, j: (blk(j), hh)),
        pl.BlockSpec((hb, nsub, dv, dk), lambda hh, j: (hh, blk(j), 0, 0)),
        pl.BlockSpec((hb, dv, dk), lambda hh, j: (hh, 0, 0)),
    ]
    out_specs = [
        pl.BlockSpec((tb, hb * dk), lambda hh, j: (blk(j), hh)),
        pl.BlockSpec((tb, hb * dk), lambda hh, j: (blk(j), hh)),
        pl.BlockSpec((tb, hb * dv), lambda hh, j: (blk(j), hh)),
        pl.BlockSpec((tb, hb * dk), lambda hh, j: (blk(j), hh)),
        pl.BlockSpec((hb, dv, dk), lambda hh, j: (hh, 0, 0)),
    ]
    out_shape = [
        jax.ShapeDtypeStruct((nrows, h * dk), F32),
        jax.ShapeDtypeStruct((nrows, h * dk), F32),
        jax.ShapeDtypeStruct((nrows, h * dv), F32),
        jax.ShapeDtypeStruct((nrows, h * dk), F32),
        jax.ShapeDtypeStruct((h, dv, dk), F32),
    ]
    return pl.pallas_call(
        body, name=name, grid=(h // hb, nblk), in_specs=in_specs, out_specs=out_specs, out_shape=out_shape,
        scratch_shapes=[pltpu.VMEM((hb, dv, dk), F32)],
        compiler_params=_cparams(("arbitrary", "arbitrary")),
    )(qr, kr, z, la, do, save, dst_init)


def _gla_post(gf, gb, la_f, la_b, z, tabs, wdf_pad, wdb_pad, cfg, *, row_off, nrows, tr, name):
    kw, vw = cfg["KW"], cfg["VW"]
    h = cfg["H"]
    ro = row_off // tr

    def body(rin, cin, rout, acc):
        dq = rin[0][...] + rin[1][...]
        dk_ = rin[2][...] + rin[3][...]
        zl = rin[10][...]
        cq, sq, ck, sk = [jnp.concatenate([rin[11 + t][...]] * h, axis=1) for t in range(4)]
        rout[0][...] = (dq * cq + _swap_halves(dq * sq, kw)).astype(BF16)
        rout[1][...] = (dk_ * ck + _swap_halves(dk_ * sk, kw)).astype(BF16)
        rout[2][...] = (rin[8][...] + rin[9][...]).astype(BF16)
        dzl = jnp.zeros(zl.shape, F32)
        for t, w in ((0, cin[0]), (1, cin[1])):
            la = rin[6 + t][...]
            da = rin[4 + t][...] * ((1.0 - jnp.exp(la * GLA_TAU)) * (1.0 / GLA_TAU))
            dzl = dzl + _nt(da, w[...])
            acc[t][...] += _tn(zl, da)
            acc[2 + t][...] += _colsum(da)
        rout[3][...] = dzl.astype(BF16)

    row_ins = [(gf[0], kw, 0, 0), (gb[0], kw, 0, 0), (gf[1], kw, 0, 0), (gb[1], kw, 0, 0),
               (gf[3], kw, 0, 0), (gb[3], kw, 0, 0), (la_f, kw, ro, 0), (la_b, kw, ro, 0),
               (gf[2], vw, 0, 0), (gb[2], vw, 0, 0), (z, LANES, ro, cfg["L0"] // LANES)]
    row_ins += [(t, LANES, ro, 0) for t in tabs]
    return _rows_call(body, name=name, nblk=nrows // tr, tr=tr, row_ins=row_ins, consts=[wdf_pad, wdb_pad],
                      row_outs=[(nrows, kw, BF16), (nrows, kw, BF16), (nrows, vw, BF16), (nrows, LANES, BF16)],
                      accs=[(LANES, kw), (LANES, kw), (1, kw), (1, kw)])


def _readout_fwd(o_f, o_b, z, g, cfg, *, tr):
    n, vw = o_f.shape
    h, dv = cfg["H"], cfg["DV"]

    def body(rin, cin, rout, acc):
        for hh in range(h):
            cs = slice(hh * dv, (hh + 1) * dv)
            oh = rin[0][:, cs] + rin[1][:, cs]
            y = oh * _rstd(oh) * cin[0][:, cs]
            rout[0][:, cs] = (y * _silu(rin[2][:, cs])).astype(BF16)

    return _rows_call(body, name="gla_readout", nblk=n // tr, tr=tr,
                      row_ins=[(o_f, vw, 0, 0), (o_b, vw, 0, 0), (z, vw, 0, cfg["R0"] // vw)], consts=[g],
                      row_outs=[(n, vw, BF16)])[0]


def _readout_bwd(o_f, o_b, z, dycat, g, cfg, *, tr):
    n, vw = o_f.shape
    h, dv = cfg["H"], cfg["DV"]

    def body(rin, cin, rout, acc):
        for hh in range(h):
            cs = slice(hh * dv, (hh + 1) * dv)
            oh = rin[0][:, cs] + rin[1][:, cs]
            r, dyg, gh = rin[2][:, cs], rin[3][:, cs], cin[0][:, cs]
            rs = _rstd(oh)
            dy = dyg * _silu(r)
            rout[0][:, cs] = _rms_bwd(oh, rs, dy * gh).astype(BF16)
            rout[1][:, cs] = (dyg * (oh * rs * gh) * _dsilu(r)).astype(BF16)
            acc[0][:, cs] += _colsum(dy * oh * rs)

    return _rows_call(body, name="gla_readout_bwd", nblk=n // tr, tr=tr,
                      row_ins=[(o_f, vw, 0, 0), (o_b, vw, 0, 0), (z, vw, 0, cfg["R0"] // vw), (dycat, vw, 0, 0)],
                      consts=[g], row_outs=[(n, vw, BF16), (n, vw, BF16)], accs=[(1, vw)])


def _sg_ln(vv):
    mu = jnp.mean(vv, axis=-1, keepdims=True)
    cen = vv - mu
    rstd = lax.rsqrt(jnp.mean(cen * cen, axis=-1, keepdims=True) + EPS)
    return cen * rstd, rstd


def _sg_fwd(z, n, lng, lnb, w_s, bs_full, cfg):
    sgw, grp, sc = cfg["SGW"], cfg["SG_G"], cfg["SG_C"]
    gw = sgw // grp

    def body(rin, cin, rout, acc):
        u = _gelu(rin[0][...])
        xhat, _ = _sg_ln(_gelu(rin[1][...]))
        vvn = xhat * cin[0][...] + cin[1][...]
        for gg in range(grp):
            cs = slice(gg * gw, (gg + 1) * gw)
            s = _nn(cin[2][gg], vvn[:, cs]) + cin[3][:, cs]
            rout[0][:, cs] = (u[:, cs] * s).astype(BF16)

    return _rows_call(body, name="sg_fwd", nblk=n // sc, tr=sc,
                      row_ins=[(z, sgw, 0, cfg["U0"] // sgw), (z, sgw, 0, cfg["VV0"] // sgw)],
                      consts=[lng, lnb, w_s, bs_full], row_outs=[(n, sgw, BF16)])[0]


def _sg_bwd(z, dycat, n, lng, lnb, w_s, bs_full, cfg):
    sgw, grp, sc = cfg["SGW"], cfg["SG_G"], cfg["SG_C"]
    gw = sgw // grp

    def body(rin, cin, rout, acc):
        up, vp, dy = rin[0][...], rin[1][...], rin[2][...]
        u = _gelu(up)
        xhat, rstd = _sg_ln(_gelu(vp))
        lng_v = cin[0][...]
        vvn = xhat * lng_v + cin[1][...]
        ds = dy * u
        acc[1][...] += ds
        dvvn_parts = []
        for gg in range(grp):
            cs = slice(gg * gw, (gg + 1) * gw)
            w = cin[2][gg]
            s = _nn(w, vvn[:, cs]) + cin[3][:, cs]
            rout[0][:, cs] = (dy[:, cs] * s * _dgelu(up[:, cs])).astype(BF16)
            acc[0][gg] += _nt(ds[:, cs], vvn[:, cs])
            dvvn_parts.append(_tn(w, ds[:, cs]))
        dvvn = jnp.concatenate(dvvn_parts, axis=1)
        acc[2][...] += _colsum(dvvn * xhat)
        acc[3][...] += _colsum(dvvn)
        dxh = dvvn * lng_v
        dvv = rstd * (dxh - jnp.mean(dxh, axis=-1, keepdims=True)
                      - xhat * jnp.mean(dxh * xhat, axis=-1, keepdims=True))
        rout[1][...] = (dvv * _dgelu(vp)).astype(BF16)

    vw = cfg["VW"]
    return _rows_call(body, name="sg_bwd", nblk=n // sc, tr=sc,
                      row_ins=[(z, sgw, 0, cfg["U0"] // sgw), (z, sgw, 0, cfg["VV0"] // sgw),
                               (dycat, sgw, 0, vw // sgw)],
                      consts=[lng, lnb, w_s, bs_full], row_outs=[(n, sgw, BF16), (n, sgw, BF16)],
                      accs=[(grp, sc, sc), (sc, sgw), (1, sgw), (1, sgw)])


def _mid_fwd(x, mix, g1, post1, pre2, sh2, sc2, *, tr):
    n, d = x.shape

    def body(rin, cin, rout, acc):
        xv, mv = rin[0][...], rin[1][...]
        x1 = xv + cin[0][...] * (mv * _rstd(mv) * cin[1][...])
        rout[0][...] = x1
        n2 = x1 * _rstd(x1) * cin[2][...]
        rout[1][...] = (n2 * (1.0 + cin[4][...]) + cin[3][...]).astype(BF16)

    return _rows_call(body, name="mid_fwd", nblk=n // tr, tr=tr, row_ins=[(x, d, 0, 0), (mix, d, 0, 0)],
                      consts=[g1, post1, pre2, sh2, sc2], row_outs=[(n, d, F32), (n, d, BF16)])


def _head_bwd(x1, m2, target, g2, post2, *, tr):
    n, d = x1.shape

    def body(rin, cin, rout, acc):
        x1v, mv, tv = rin[0][...], rin[1][...], rin[2][...]
        g2v, pg = cin[0][...], cin[1][...]
        r = _rstd(mv)
        y2 = mv * r * pg
        err = (x1v + g2v * y2) - tv
        acc[2][...] += _colsum(err * err) * (0.5 / d)
        dx2 = err * (1.0 / d)
        rout[0][...] = dx2
        dy2 = dx2 * g2v
        acc[0][...] += _colsum(dx2 * y2)
        acc[1][...] += _colsum(dy2 * mv * r)
        rout[1][...] = _rms_bwd(mv, r, dy2 * pg).astype(BF16)

    return _rows_call(body, name="head_bwd", nblk=n // tr, tr=tr,
                      row_ins=[(x1, d, 0, 0), (m2, d, 0, 0), (target, d, 0, 0)], consts=[g2, post2],
                      row_outs=[(n, d, F32), (n, d, BF16)], accs=[(1, d)] * 3)


def _mid_bwd(dh2, x1, dx2, mix, sc2, pre2, g1, post1, *, tr, sides=_NO_SIDES):
    n, d = x1.shape

    def body(rin, cin, rout, acc):
        dh, x1v, dx2v, mv = rin[0][...], rin[1][...], rin[2][...], rin[3][...]
        sc2v, pre2v, g1v, post1v = cin[0][...], cin[1][...], cin[2][...], cin[3][...]
        r2 = _rstd(x1v)
        xr = x1v * r2
        acc[0][...] += _colsum(dh)
        acc[1][...] += _colsum(dh * (xr * pre2v))
        dn2 = dh * (1.0 + sc2v)
        acc[2][...] += _colsum(dn2 * xr)
        dx1 = dx2v + _rms_bwd(x1v, r2, dn2 * pre2v)
        rout[0][...] = dx1
        r1 = _rstd(mv)
        mr = mv * r1
        acc[3][...] += _colsum(dx1 * (mr * post1v))
        dy1 = dx1 * g1v
        acc[4][...] += _colsum(dy1 * mr)
        rout[1][...] = _rms_bwd(mv, r1, dy1 * post1v).astype(BF16)

    return _rows_call(body, name="mid_bwd", nblk=n // tr, tr=tr,
                      row_ins=[(dh2, d, 0, 0), (x1, d, 0, 0), (dx2, d, 0, 0), (mix, d, 0, 0)],
                      consts=[sc2, pre2, g1, post1], row_outs=[(n, d, F32), (n, d, BF16)], accs=[(1, d)] * 5,
                      sides=sides)


def _in_bwd(da, x, dres, sc1, pre1, *, row_off, tr, name, sides=_NO_SIDES):
    n, d = x.shape
    with_res = dres is not None

    def body(rin, cin, rout, acc):
        dav, xv = rin[0][...], rin[1][...]
        sc1v, pre1v = cin[0][...], cin[1][...]
        r = _rstd(xv)
        xr = xv * r
        acc[0][...] += _colsum(dav)
        acc[1][...] += _colsum(dav * (xr * pre1v))
        dn = dav * (1.0 + sc1v)
        acc[2][...] += _colsum(dn * xr)
        if with_res:
            rout[0][...] = rin[2][...] + _rms_bwd(xv, r, dn * pre1v)

    row_ins = [(da, d, row_off // tr, 0), (x, d, 0, 0)] + ([(dres, d, 0, 0)] if with_res else [])
    return _rows_call(body, name=name, nblk=n // tr, tr=tr, row_ins=row_ins, consts=[sc1, pre1],
                      row_outs=[(n, d, F32)] if with_res else [], accs=[(1, d)] * 3, sides=sides)


def _ada_fwd(c16, w, b, *, tn):
    d, ncol = w.shape

    def body(c_ref, w_ref, b_ref, o_ref):
        o_ref[...] = _nn(_silu(c_ref[...]), w_ref[...]) + b_ref[...]

    return pl.pallas_call(
        body, name="ada_fwd", grid=(ncol // tn,),
        in_specs=[pl.BlockSpec((16, d), lambda j: (0, 0)), pl.BlockSpec((d, tn), lambda j: (0, j)),
                  pl.BlockSpec((1, tn), lambda j: (0, j))],
        out_specs=pl.BlockSpec((16, tn), lambda j: (0, j)),
        out_shape=jax.ShapeDtypeStruct((16, ncol), F32),
        compiler_params=_cparams(("arbitrary",)),
    )(c16, w, b)


def _ada_bwd(c16, dm, w, c_ctx, *, tn):
    d, ncol = w.shape

    def body(c_ref, dm_ref, w_ref, cc_ref, gw_ref, dcc_ref, acc):
        j = pl.program_id(0)

        @pl.when(j == 0)
        def _():
            acc[...] = jnp.zeros_like(acc)

        gw_ref[...] = _tn(_silu(c_ref[...]), dm_ref[...])
        acc[...] += _nt(dm_ref[...], w_ref[...])

        @pl.when(j == ncol // tn - 1)
        def _():
            dcc_ref[...] = _colsum(acc[8:16, :]) * _dsilu(cc_ref[...])

    return pl.pallas_call(
        body, name="ada_bwd", grid=(ncol // tn,),
        in_specs=[pl.BlockSpec((16, d), lambda j: (0, 0)), pl.BlockSpec((16, tn), lambda j: (0, j)),
                  pl.BlockSpec((d, tn), lambda j: (0, j)), pl.BlockSpec((1, d), lambda j: (0, 0))],
        out_specs=[pl.BlockSpec((d, tn), lambda j: (0, j)), pl.BlockSpec((1, d), lambda j: (0, 0))],
        out_shape=[jax.ShapeDtypeStruct((d, ncol), F32), jax.ShapeDtypeStruct((1, d), F32)],
        scratch_shapes=[pltpu.VMEM((16, d), F32)],
        compiler_params=_cparams(("arbitrary",)),
    )(c16, dm, w, c_ctx)


def _adam_math(w, g, m, v):
    m = ADAM_B1 * m + (1.0 - ADAM_B1) * g
    v = ADAM_B2 * v + (1.0 - ADAM_B2) * (g * g)
    m_hat = m / (1.0 - ADAM_B1 ** ADAM_STEP)
    v_hat = v / (1.0 - ADAM_B2 ** ADAM_STEP)
    delta = -ADAM_LR * (m_hat / (jnp.sqrt(v_hat) + ADAM_EPS) + ADAM_WD * w)
    return delta, m, v


def _adam_big(parts, w, m, v, *, name, tr, sides=_NO_SIDES):
    rows, cols = w.shape
    n_p = len(parts)
    n_si, n_so = len(sides.ins), len(sides.outs)
    nblk = rows // tr

    def body(*refs):
        ins, refs = refs[:n_p + 3], refs[n_p + 3:]
        side_in, refs = refs[:n_si], refs[n_si:]
        outs, side_out, side_sems = refs[:4], refs[4:4 + n_so], refs[4 + n_so:]
        i = pl.program_id(0)

        if sides.jobs:
            @pl.when(i == 0)
            def _():
                sides.run("start", side_in, side_out, side_sems)

        g = ins[0][...]
        for p in ins[1:n_p]:
            g = g + p[...].astype(F32)
        delta, m2, v2 = _adam_math(ins[n_p][...], g, ins[n_p + 1][...], ins[n_p + 2][...])
        outs[0][...] = g
        outs[1][...] = delta
        outs[2][...] = m2
        outs[3][...] = v2

        if sides.jobs:
            @pl.when(i == nblk - 1)
            def _():
                sides.run("finish", side_in, side_out, side_sems)

    plain = pl.BlockSpec((tr, cols), lambda i: (i, 0))
    in_specs = []
    for arr, idx in parts:
        if idx is None:
            in_specs.append(plain)
        else:
            in_specs.append(pl.BlockSpec((None, tr, cols), lambda i, idx=idx: (idx, i, 0)))
    in_specs += [plain] * 3
    return pl.pallas_call(
        body, name=name, grid=(nblk,), in_specs=in_specs + [ANY_SPEC] * n_si,
        out_specs=[plain] * 4 + [ANY_SPEC] * n_so,
        out_shape=[jax.ShapeDtypeStruct((rows, cols), F32)] * 4 + list(sides.outs),
        scratch_shapes=list(sides.sems), input_output_aliases=sides.aliases(n_p + 3, 4),
        compiler_params=_cparams(("arbitrary",) if sides.jobs else ("parallel",)),
    )(*[p[0] for p in parts], w, m, v, *sides.ins)


def _sum_parts(parts, rows, cols, *, name, tr):
    def body(*refs):
        g = refs[0][...].astype(F32)
        for p in refs[1:-1]:
            g = g + p[...].astype(F32)
        refs[-1][...] = g

    in_specs = [pl.BlockSpec((None, tr, cols), lambda i, idx=idx: (idx, i, 0)) for _, idx in parts]
    return pl.pallas_call(
        body, name=name, grid=(rows // tr,), in_specs=in_specs, out_specs=pl.BlockSpec((tr, cols), lambda i: (i, 0)),
        out_shape=jax.ShapeDtypeStruct((rows, cols), F32), compiler_params=_cparams(("parallel",)),
    )(*[p[0] for p in parts])


def _adam_small(g8, w, m, v):
    def body(g_ref, w_ref, m_ref, v_ref, go, do, mo, vo):
        g = g_ref[0]
        for r in range(1, N_DEV):
            g = g + g_ref[r]
        delta, m2, v2 = _adam_math(w_ref[...], g, m_ref[...], v_ref[...])
        go[...] = g
        do[...] = delta
        mo[...] = m2
        vo[...] = v2

    return pl.pallas_call(
        body, name="adam_small", out_shape=[jax.ShapeDtypeStruct(w.shape, F32)] * 4,
        compiler_params=pltpu.CompilerParams(vmem_limit_bytes=VMEM_LIMIT),
    )(g8, w, m, v)


VEC_W = 1024
TK = 2048
ELEMS_PER_BLOCK = 256 * 1024


def _dense(v):
    a, k = v.shape
    kp = -(-k // (8 * VEC_W)) * (8 * VEC_W)
    return jnp.pad(v, ((0, 0), (0, kp - k))).reshape(a, kp // VEC_W, VEC_W)


def _all_gather_vec(v, *, name):
    k = v.shape[1]
    return _all_gather_small(_dense(v)[0], name=name).reshape(N_DEV, -1)[:, :k]


def _my_pos():
    return lax.axis_index("x"), lax.axis_index("y"), lax.axis_index("c")


def _flip(v, bit):
    return (1 - v) if bit else v


def _all_gather_small(v, *, name):
    r, k = v.shape

    def body(v_ref, out_ref, send, recv, lsem):
        x, y, c = _my_pos()
        me = 4 * x + 2 * y + c
        local = pltpu.make_async_copy(v_ref, out_ref.at[me], lsem)
        local.start()
        sends = []
        for kk in range(1, N_DEV):
            peer = (_flip(x, kk & 4), _flip(y, kk & 2), _flip(c, kk & 1))
            cp = pltpu.make_async_remote_copy(src_ref=v_ref, dst_ref=out_ref.at[me], send_sem=send.at[kk - 1],
                                              recv_sem=recv.at[kk - 1], device_id=peer, device_id_type=MESH)
            cp.start()
            sends.append(cp)
        for kk in range(1, N_DEV):
            px, py, pc = _flip(x, kk & 4), _flip(y, kk & 2), _flip(c, kk & 1)
            src = 4 * px + 2 * py + pc
            pltpu.make_async_remote_copy(src_ref=v_ref, dst_ref=out_ref.at[src], send_sem=send.at[kk - 1],
                                         recv_sem=recv.at[kk - 1], device_id=(px, py, pc),
                                         device_id_type=MESH).wait_recv()
        for cp in sends:
            cp.wait_send()
        local.wait()

    return pl.pallas_call(
        body, name=name, out_shape=jax.ShapeDtypeStruct((N_DEV, r, k), v.dtype),
        in_specs=[pl.BlockSpec(memory_space=pltpu.VMEM)], out_specs=pl.BlockSpec(memory_space=pltpu.VMEM),
        scratch_shapes=[pltpu.SemaphoreType.DMA((N_DEV - 1,)), pltpu.SemaphoreType.DMA((N_DEV - 1,)),
                        pltpu.SemaphoreType.DMA],
        compiler_params=pltpu.CompilerParams(vmem_limit_bytes=VMEM_LIMIT),
    )(v)


def _ag_job(shards, rows=None, chained=None):
    n_arr = len(shards)

    def part(ref):
        return ref if rows is None else ref.at[pl.ds(rows[0], rows[1])]

    def tools(ins, outs, sems):
        send, recv, lsem = sems
        x, y, c = _my_pos()
        chips = [(1 - x, y), (x, 1 - y), (1 - x, 1 - y)]

        def copy(a, kk, block, to, src=None):
            dst = part(outs[a].at[4 * block[0] + 2 * block[1] + block[2]])
            return pltpu.make_async_remote_copy(src_ref=dst if src is None else part(src), dst_ref=dst,
                                                send_sem=send.at[a, kk], recv_sem=recv.at[a, kk],
                                                device_id=to, device_id_type=MESH)

        locals_ = [pltpu.make_async_copy(part(ins[a]), part(outs[a].at[4 * x + 2 * y + c]), lsem.at[a])
                   for a in range(n_arr)]
        firsts = []
        for a in range(n_arr):
            firsts.append(copy(a, 0, (x, y, c), (x, y, 1 - c), src=ins[a]))
            firsts += [copy(a, 1 + j, (x, y, c), (*chip, c), src=ins[a]) for j, chip in enumerate(chips)]
        return copy, locals_, firsts, chips, (x, y, c)

    def start(ins, outs, sems):
        _, locals_, firsts, _, _ = tools(ins, outs, sems)
        for cp in locals_ + firsts:
            cp.start()

    def finish(ins, outs, sems):
        copy, locals_, firsts, chips, (x, y, c) = tools(ins, outs, sems)
        me, sibling = (x, y, c), (x, y, 1 - c)
        passed = []
        for a in range(n_arr):
            for j, chip in enumerate(chips):
                copy(a, 1 + j, (*chip, c), me).wait_recv()
                fw = copy(a, 4 + j, (*chip, c), sibling)
                fw.start()
                passed.append(fw)
        for a in range(n_arr):
            copy(a, 0, sibling, me).wait_recv()
            for j, chip in enumerate(chips):
                copy(a, 4 + j, (*chip, 1 - c), me).wait_recv()
        for cp in firsts + passed:
            cp.wait_send()
        for lc in locals_:
            lc.wait()

    job = dict(ins=list(shards), outs=[jax.ShapeDtypeStruct((N_DEV,) + s.shape, s.dtype) for s in shards],
               sems=[pltpu.SemaphoreType.DMA((n_arr, 7)), pltpu.SemaphoreType.DMA((n_arr, 7)),
                     pltpu.SemaphoreType.DMA((n_arr,))], start=start, finish=finish)
    if chained is not None:
        job["ins"] = list(shards) + list(chained)
        job["alias"] = {n_arr + a: a for a in range(n_arr)}
    return job


def _exchange_job(arrays, n_slots, out_slots, src_of, dst_of, peer_of, rows=None, chained=None):
    n_arr = len(arrays)

    def copies(ins, outs, sems):
        send, recv = sems
        x, y, c = _my_pos()
        res = []
        for a in range(n_arr):
            for s in range(n_slots):
                src, dst = ins[a].at[src_of(s, x, y, c)], outs[a].at[dst_of(s)]
                if rows is not None:
                    src, dst = src.at[pl.ds(rows[0], rows[1])], dst.at[pl.ds(rows[0], rows[1])]
                res.append(pltpu.make_async_remote_copy(
                    src_ref=src, dst_ref=dst, send_sem=send.at[a, s], recv_sem=recv.at[a, s],
                    device_id=peer_of(s, x, y, c), device_id_type=MESH))
        return res

    def start(ins, outs, sems):
        for cp in copies(ins, outs, sems):
            cp.start()

    def finish(ins, outs, sems):
        cps = copies(ins, outs, sems)
        for cp in cps:
            cp.wait_recv()
        for cp in cps:
            cp.wait_send()

    job = dict(ins=list(arrays), outs=[jax.ShapeDtypeStruct((out_slots,) + g.shape[1:], g.dtype) for g in arrays],
               sems=[pltpu.SemaphoreType.DMA((n_arr, n_slots)), pltpu.SemaphoreType.DMA((n_arr, n_slots))],
               start=start, finish=finish)
    if chained is not None:
        job["ins"] = list(arrays) + list(chained)
        job["alias"] = {n_arr + a: a for a in range(n_arr)}
    return job


def _pair_job(grads):
    return _exchange_job(
        grads, 4, 4,
        src_of=lambda s, x, y, c: 4 * _flip(x, s & 2) + 2 * _flip(y, s & 1) + (1 - c),
        dst_of=lambda s: s, peer_of=lambda s, x, y, c: (x, y, 1 - c))


def _chip_job(sums, rows=None, chained=None):
    return _exchange_job(
        sums, 3, 3, src_of=lambda s, x, y, c: s, dst_of=lambda s: s,
        peer_of=lambda s, x, y, c: (_flip(x, (s + 1) & 2), _flip(y, (s + 1) & 1), c), rows=rows, chained=chained)


def _a2a_job(x):
    def copies(ins, outs, sems):
        send, recv, lsem = sems
        x_, y_, c_ = _my_pos()
        me = 4 * x_ + 2 * y_ + c_
        local = pltpu.make_async_copy(ins[0].at[me], outs[0].at[me], lsem)
        res = []
        for s in range(1, N_DEV):
            px, py, pc = _flip(x_, s & 4), _flip(y_, s & 2), _flip(c_, s & 1)
            res.append(pltpu.make_async_remote_copy(
                src_ref=ins[0].at[4 * px + 2 * py + pc], dst_ref=outs[0].at[me], send_sem=send.at[s - 1],
                recv_sem=recv.at[s - 1], device_id=(px, py, pc), device_id_type=MESH))
        return local, res

    def start(ins, outs, sems):
        local, res = copies(ins, outs, sems)
        local.start()
        for cp in res:
            cp.start()

    def finish(ins, outs, sems):
        local, res = copies(ins, outs, sems)
        for cp in res:
            cp.wait_recv()
        for cp in res:
            cp.wait_send()
        local.wait()

    return dict(ins=[x], outs=[jax.ShapeDtypeStruct(x.shape, x.dtype)],
                sems=[pltpu.SemaphoreType.DMA((N_DEV - 1,)), pltpu.SemaphoreType.DMA((N_DEV - 1,)),
                      pltpu.SemaphoreType.DMA], start=start, finish=finish)


def _run_sides(sides, *, name):
    n_si, n_so = len(sides.ins), len(sides.outs)

    def body(*refs):
        ins, outs, sems = refs[:n_si], refs[n_si:n_si + n_so], refs[n_si + n_so:]
        sides.run("start", ins, outs, sems)
        sides.run("finish", ins, outs, sems)

    return pl.pallas_call(
        body, name=name, out_shape=list(sides.outs), in_specs=[ANY_SPEC] * n_si, out_specs=[ANY_SPEC] * n_so,
        scratch_shapes=list(sides.sems), input_output_aliases=sides.aliases(0, 0),
    )(*sides.ins)


def _pair_add(g, t, *, name, tr, wire):
    _, r, cols = g.shape
    g4 = g.reshape(4, 2, r, cols)
    j0 = 1 if wire else 0

    def g_index(j, i):
        x, y, c = _my_pos()
        return (jnp.bitwise_xor(2 * x + y, j + j0), c, i, 0)

    def body(g_ref, t_ref, o_ref):
        o_ref[...] = (g_ref[...] + t_ref[...]).astype(o_ref.dtype)

    return pl.pallas_call(
        body, name=name, grid=(3 if wire else 1, r // tr),
        in_specs=[pl.BlockSpec((None, None, tr, cols), g_index),
                  pl.BlockSpec((None, tr, cols), lambda j, i: (j + j0, i, 0))],
        out_specs=pl.BlockSpec((None, tr, cols), lambda j, i: (j, i, 0)),
        out_shape=jax.ShapeDtypeStruct((3 if wire else 1, r, cols), BF16 if wire else F32),
        compiler_params=_cparams(("arbitrary", "arbitrary")),
    )(g4, t)


def _config(x, ctx, w_in, w_dec_f, gla_norm_g, sg_ln_g, w_s):
    n, d = x.shape[1], x.shape[2]
    tc = ctx.shape[1]
    h = gla_norm_g.shape[1]
    dv = gla_norm_g.shape[2] * N_DEV
    dk = dv // 2
    kw, vw = h * dk, h * dv
    lr = w_dec_f.shape[1]
    sgw = sg_ln_g.shape[1]
    cfg = dict(N=n, D=d, TC=tc, H=h, DV=dv, DK=dk, KW=kw, VW=vw, LR=lr, SGW=sgw, SG_G=w_s.shape[1],
               SG_C=w_s.shape[2], IN=w_in.shape[2] * N_DEV)
    cfg.update(K0=kw, V0=2 * kw, R0=2 * kw + vw, U0=2 * kw + 2 * vw)
    cfg.update(VV0=cfg["U0"] + sgw, L0=cfg["U0"] + 2 * sgw, ZW=cfg["U0"] + 2 * sgw + LANES)
    assert dk == LANES and vw == 2 * kw and 2 * lr <= LANES
    assert cfg["R0"] % vw == 0 and cfg["U0"] % sgw == 0 and cfg["VV0"] % sgw == 0 and vw % sgw == 0
    assert cfg["IN"] == 2 * kw + 2 * vw + 2 * lr + 2 * sgw
    return cfg


def _rope_tables(cfg):
    n, tc, dk = cfg["N"], cfg["TC"], cfg["DK"]
    m = dk // 4
    pos = jnp.arange(n)
    inv = ROPE_BASE ** (-jnp.arange(m, dtype=F32) / m)
    ang_r = (pos // GRID_W).astype(F32)[:, None] * inv[None, :]
    ang_c = (pos % GRID_W).astype(F32)[:, None] * inv[None, :]
    cos = jnp.concatenate([jnp.cos(ang_r)] * 2 + [jnp.cos(ang_c)] * 2, axis=1)
    sin = jnp.concatenate([-jnp.sin(ang_r), jnp.sin(ang_r), -jnp.sin(ang_c), jnp.sin(ang_c)], axis=1)
    scale = dk ** -0.5
    z = jnp.zeros((tc, dk), F32)
    one = jnp.ones((tc, dk), F32)
    return [jnp.concatenate([cos * scale, z]), jnp.concatenate([sin * scale, z]),
            jnp.concatenate([cos, one]), jnp.concatenate([sin, z])]


def _w_in_splits(r):
    return (r * 13 // 32) // 16 * 16, (r * 23 // 32) // 16 * 16


def _pair_sums(g, t, nm):
    rows_for = _tile(g.shape[1], max(8, ELEMS_PER_BLOCK // g.shape[2]), 16)
    return (_pair_add(g, t, name="rs_own_" + nm, tr=rows_for, wire=False),
            _pair_add(g, t, name="rs_wire_" + nm, tr=rows_for, wire=True))


def _local_step(x, ctx, target, mods, c_mods, w, cfg):
    n, d, tc = cfg["N"], cfg["D"], cfg["TC"]
    kw, vw, sgw, zw, lr = cfg["KW"], cfg["VW"], cfg["SGW"], cfg["ZW"], cfg["LR"]
    sh1, sc1, g1, sh2, sc2, g2 = mods
    csh1, csc1 = c_mods
    rt = n + tc
    tb = math.gcd(256, math.gcd(n, tc))
    tr = math.gcd(128, tb)
    tr_s = math.gcd(64, tb)
    fs = w["sh_1"].shape[1]
    ff = fs * N_DEV
    cs_in = w["sh_in"].shape[1]

    wg_in = _run_sides(_Sides([_ag_job([w["sh_in"]])]), name="ag_w_in")[0]
    w_full = wg_in.transpose(1, 0, 2).reshape(d, cfg["IN"])
    lf0 = 2 * kw + 2 * vw
    sg0 = lf0 + 2 * lr
    w_pad = jnp.concatenate([w_full[:, :lf0], w_full[:, sg0:], w_full[:, lf0:sg0],
                             jnp.zeros((d, LANES - 2 * lr), BF16)], axis=1)

    hx = _norm_mod(x, w["pre1_g"], sh1, sc1, name="in_norm_x", tr=tr)
    hc = _norm_mod(ctx, w["pre1_g"], csh1, csc1, name="in_norm_ctx", tr=tr)
    a_all = jnp.concatenate([hx, hc], axis=0)

    tm_a = _tile(rt, 1152, 16)
    z, wg_o, w_1 = _matmul(a_all, w_pad, "nn", rt, zw, d, tm=tm_a, tn=_tile(zw, 1152, LANES),
                           tk=_tile(d, TK, LANES), name="mm_in", out_shapes=[jax.ShapeDtypeStruct((rt, zw), F32)],
                           sides=_Sides([_ag_job([w["sh_o"]]), _ag_job([w["sh_1"]], rows=(0, d // 2))]))
    w_o = wg_o.reshape(d, d)

    tabs = _rope_tables(cfg)
    qr, kr, la_f, la_b = _gla_prep(z, tabs, w["wdf_pad"], w["wdb_pad"], w["b_dec_f"], w["b_dec_b"], cfg, tr=tr)

    zero_st = jnp.zeros((cfg["H"], cfg["DV"], cfg["DK"]), F32)
    gla = dict(cfg=cfg, tb=tb)
    _, save_cf, st_cf = _gla_fwd(qr, kr, z, la_f, zero_st, rev=False, row_off=n, nrows=tc, name="gla_ctx_f", **gla)
    _, save_cb, st_cb = _gla_fwd(qr, kr, z, la_b, zero_st, rev=True, row_off=n, nrows=tc, name="gla_ctx_b", **gla)
    o_f, save_f, _, w_1 = _gla_fwd(
        qr, kr, z, la_f, st_cf, rev=False, row_off=0, nrows=n, name="gla_f",
        sides=_Sides([_ag_job([w["sh_1"]], rows=(d // 2, d // 4), chained=[w_1])]), **gla)
    o_b, save_b, _, w_1 = _gla_fwd(
        qr, kr, z, la_b, st_cb, rev=True, row_off=0, nrows=n, name="gla_b",
        sides=_Sides([_ag_job([w["sh_1"]], rows=(3 * d // 4, d // 4), chained=[w_1])]), **gla)
    y_gla = _readout_fwd(o_f, o_b, z, w["gla_g"], cfg, tr=tr)
    y_sg = _sg_fwd(z, n, w["sg_ln_g"], w["sg_ln_b"], w["w_s"], w["bs_full"], cfg)
    ycat = jnp.concatenate([y_gla, y_sg], axis=1)

    tm_n = _tile(n, 1024, 16)
    mix, wg_2 = _matmul(ycat, w_o, "nn", n, d, d, tm=tm_n, tn=_tile(d, 1024, LANES), tk=_tile(d, TK, LANES),
                        name="mm_o", out_shapes=[jax.ShapeDtypeStruct((n, d), F32)],
                        sides=_Sides([_ag_job([w["sh_2"]], rows=(0, fs // 4))]))
    x1, h2 = _mid_fwd(x, mix, g1, w["post1_g"], w["pre2_g"], sh2, sc2, tr=tr_s)

    tn_f = _tile(fs, 1024, LANES)
    tk_d = _tile(d, TK, LANES)

    def relu2(acc):
        return acc, jnp.square(jnp.maximum(acc, 0.0))

    a1, p1, wg_2 = _matmul(h2, w_1, "nn", n, ff, d, tm=tm_n, tn=tn_f, tk=tk_d, name="mm_1",
                           b_spec=_blocked_b_nn(fs, tk_d, tn_f), epilogue=relu2,
                           out_shapes=[jax.ShapeDtypeStruct((n, ff), BF16)] * 2,
                           sides=_Sides([_ag_job([w["sh_2"]], rows=(fs // 4, fs - fs // 4), chained=[wg_2])]))
    w_2 = wg_2.reshape(ff, d)
    tk_f = _tile(ff, TK, LANES)
    m2 = _matmul(p1, w_2, "nn", n, d, ff, tm=tm_n, tn=_tile(d, 1024, LANES), tk=tk_f, name="mm_2",
                 out_shapes=[jax.ShapeDtypeStruct((n, d), F32)])[0]

    dx2, dm2, dg2, dpost2, lossc = _head_bwd(x1, m2, target, g2, w["post2_g"], tr=tr_s)

    def drelu2(acc, a):
        return (acc * (2.0 * jnp.maximum(a.astype(F32), 0.0)),)

    da1 = _matmul(dm2, w_2, "nt", n, ff, d, tm=tm_n, tn=_tile(ff, 1024, LANES), tk=tk_d, name="mm_2_dx",
                  epilogue=drelu2, extras=(a1,), out_shapes=[jax.ShapeDtypeStruct((n, ff), BF16)])[0]
    tk_n = _tile(n, TK, 16)
    tm_d = _tile(d, 1024, LANES)
    g_1 = _matmul(h2, da1, "tn", d, ff, n, tm=tm_d, tn=tn_f, tk=tk_n, name="mm_1_dw",
                  out_specs=[_blocked_out(fs, tm_d, tn_f)],
                  out_shapes=[jax.ShapeDtypeStruct((N_DEV, d, fs), F32)])[0]
    dw_2, t_1 = _matmul(p1, dm2, "tn", ff, d, n, tm=_tile(ff, 1024, LANES), tn=_tile(d, 1024, LANES), tk=tk_n,
                        name="mm_2_dw", out_shapes=[jax.ShapeDtypeStruct((ff, d), F32)],
                        sides=_Sides([_pair_job([g_1])]))
    g_2 = dw_2.reshape(N_DEV, fs, d)
    p1_own, p1_wire = _pair_sums(g_1, t_1, "w_1")
    tk_fs = _tile(fs, TK, LANES)
    dh2, u_1, t_2 = _matmul(da1, w_1, "nt", n, d, ff, tm=tm_n, tn=_tile(d, 1024, LANES), tk=tk_fs, name="mm_1_dx",
                            b_spec=_blocked_b_nt(fs, _tile(d, 1024, LANES), tk_fs),
                            out_shapes=[jax.ShapeDtypeStruct((n, d), F32)],
                            sides=_Sides([_chip_job([p1_wire], rows=(0, d * 13 // 16)), _pair_job([g_2])]))
    p2_own, p2_wire = _pair_sums(g_2, t_2, "w_2")
    rows_a = (fs * 5 // 16) // 16 * 16

    dx1, dmix, dsh2, dsc2, dpre2, dg1, dpost1, u_1 = _mid_bwd(
        dh2, x1, dx2, mix, sc2, w["pre2_g"], g1, w["post1_g"], tr=tr_s,
        sides=_Sides([_chip_job([p1_wire], rows=(d * 13 // 16, d - d * 13 // 16), chained=[u_1])]))
    dw_o, u_2 = _matmul(ycat, dmix, "tn", d, d, n, tm=tm_d, tn=_tile(d, 1024, LANES), tk=tk_n, name="mm_o_dw",
                        out_shapes=[jax.ShapeDtypeStruct((d, d), F32)],
                        sides=_Sides([_chip_job([p2_wire], rows=(0, rows_a))]))
    g_o = dw_o.reshape(N_DEV, d // N_DEV, d)
    dycat, t_o = _matmul(dmix, w_o, "nt", n, d, d, tm=tm_n, tn=_tile(d, 1024, LANES), tk=tk_d, name="mm_o_dx",
                         out_shapes=[jax.ShapeDtypeStruct((n, d), F32)], sides=_Sides([_pair_job([g_o])]))
    po_own, po_wire = _pair_sums(g_o, t_o, "w_o")

    dzu, dzvv, dws, dbs_acc, dlng, dlnb = _sg_bwd(z, dycat, n, w["sg_ln_g"], w["sg_ln_b"], w["w_s"],
                                                  w["bs_full"], cfg)
    do, dzr, dgla_g = _readout_bwd(o_f, o_b, z, dycat, w["gla_g"], cfg, tr=tr)

    gf = _gla_bwd(qr, kr, z, la_f, do, save_f, zero_st, rev=False, row_off=0, nrows=n, name="gla_f_bwd", **gla)
    gb = _gla_bwd(qr, kr, z, la_b, do, save_b, zero_st, rev=True, row_off=0, nrows=n, name="gla_b_bwd", **gla)
    do_c = jnp.zeros((tc, vw), BF16)
    gcf = _gla_bwd(qr, kr, z, la_f, do_c, save_cf, gf[4], rev=False, row_off=n, nrows=tc, name="gla_ctx_f_bwd",
                   **gla)
    gcb = _gla_bwd(qr, kr, z, la_b, do_c, save_cb, gb[4], rev=True, row_off=n, nrows=tc, name="gla_ctx_b_bwd",
                   **gla)

    post = dict(la_f=la_f, la_b=la_b, z=z, tabs=tabs, wdf_pad=w["wdf_pad"], wdb_pad=w["wdb_pad"], cfg=cfg, tr=tr)
    dzq, dzk, dzv, dzl, dwdf, dwdb, dbdf, dbdb = _gla_post(gf, gb, row_off=0, nrows=n, name="gla_post", **post)
    czq, czk, czv, czl, cwdf, cwdb, cbdf, cbdb = _gla_post(gcf, gcb, row_off=n, nrows=tc, name="gla_post_ctx",
                                                           **post)
    zc = lambda wd: jnp.zeros((tc, wd), BF16)
    dz = jnp.concatenate([
        jnp.concatenate([dzq, dzk, dzv, dzr, dzu, dzvv, dzl], axis=1),
        jnp.concatenate([czq, czk, czv, zc(vw), zc(sgw), zc(sgw), czl], axis=1)], axis=0)

    dw_pad, u_2 = _matmul(a_all, dz, "tn", d, zw, rt, tm=tm_d, tn=_tile(zw, 1152, LANES), tk=_tile(rt, 2176, 16),
                          name="mm_in_dw", out_shapes=[jax.ShapeDtypeStruct((d, zw), F32)],
                          sides=_Sides([_chip_job([p2_wire], rows=(rows_a, fs - rows_a), chained=[u_2])]))
    g_in = dw_pad.reshape(N_DEV, d // N_DEV, zw)
    da_all, u_o, t_in = _matmul(dz, w_pad, "nt", rt, d, zw, tm=_tile(rt, 576, 16), tn=_tile(d, 1024, LANES),
                                tk=_tile(zw, 3456, LANES), name="mm_in_dx",
                                out_shapes=[jax.ShapeDtypeStruct((rt, d), F32)],
                                sides=_Sides([_chip_job([po_wire]), _pair_job([g_in])]))
    pin_own, pin_wire = _pair_sums(g_in, t_in, "w_in")

    grad_x, dsh1, dsc1, dpre1, u_in = _in_bwd(
        da_all, x, dx1, sc1, w["pre1_g"], row_off=0, tr=tr_s, name="in_bwd_x",
        sides=_Sides([_chip_job([pin_wire], rows=(0, _w_in_splits(d // N_DEV)[0]))]))
    dcsh1, dcsc1, dpre1_c = _in_bwd(da_all, ctx, None, csc1, w["pre1_g"], row_off=n, tr=tr_s, name="in_bwd_ctx")

    small = dict(
        pre1_g=dpre1 + dpre1_c, post1_g=dpost1, pre2_g=dpre2, post2_g=dpost2,
        w_dec_f=(dwdf + cwdf)[:lr], w_dec_b=(dwdb + cwdb)[lr:2 * lr], b_dec_f=dbdf + cbdf, b_dec_b=dbdb + cbdb,
        gla_norm_g=dgla_g, sg_ln_g=dlng, sg_ln_b=dlnb, w_s=dws,
        b_s=dbs_acc.reshape(cfg["SG_C"], cfg["SG_G"], sgw // cfg["SG_G"]).sum(-1).T)
    dmod = jnp.concatenate([dsh1, dsc1, dg1, dsh2, dsc2, dg2], axis=1)
    dmod_c = jnp.concatenate([dcsh1, dcsc1], axis=1)
    big = dict(w_in=(pin_own, u_in, pin_wire), w_o=(po_own, u_o), w_1=(p1_own, u_1), w_2=(p2_own, u_2))
    return lossc, grad_x, big, small, dmod, dmod_c


SMALL_NAMES = ["b_ada", "pre1_g", "post1_g", "pre2_g", "post2_g", "w_dec_f", "b_dec_f", "w_dec_b", "b_dec_b",
               "gla_norm_g", "sg_ln_g", "sg_ln_b", "w_s", "b_s", "c_ctx"]
WEIGHT_ORDER = ["c_ctx", "w_ada", "b_ada", "pre1_g", "post1_g", "pre2_g", "post2_g", "w_in", "w_dec_f", "b_dec_f",
                "w_dec_b", "b_dec_b", "gla_norm_g", "sg_ln_g", "sg_ln_b", "w_s", "b_s", "w_o", "w_1", "w_2"]


def kernel(x, c, ctx, c_ctx, w_ada, b_ada, pre1_g, post1_g, pre2_g, post2_g, w_in, w_dec_f, b_dec_f, w_dec_b, b_dec_b, gla_norm_g, sg_ln_g, sg_ln_b, w_s, b_s, w_o, w_1, w_2, loss_target, m_c_ctx, m_w_ada, m_b_ada, m_pre1_g, m_post1_g, m_pre2_g, m_post2_g, m_w_in, m_w_dec_f, m_b_dec_f, m_w_dec_b, m_b_dec_b, m_gla_norm_g, m_sg_ln_g, m_sg_ln_b, m_w_s, m_b_s, m_w_o, m_w_1, m_w_2, v_c_ctx, v_w_ada, v_b_ada, v_pre1_g, v_post1_g, v_pre2_g, v_post2_g, v_w_in, v_w_dec_f, v_b_dec_f, v_w_dec_b, v_b_dec_b, v_gla_norm_g, v_sg_ln_g, v_sg_ln_b, v_w_s, v_b_s, v_w_o, v_w_1, v_w_2):
    weights = dict(c_ctx=c_ctx, w_ada=w_ada, b_ada=b_ada, pre1_g=pre1_g, post1_g=post1_g, pre2_g=pre2_g,
                   post2_g=post2_g, w_in=w_in, w_dec_f=w_dec_f, b_dec_f=b_dec_f, w_dec_b=w_dec_b, b_dec_b=b_dec_b,
                   gla_norm_g=gla_norm_g, sg_ln_g=sg_ln_g, sg_ln_b=sg_ln_b, w_s=w_s, b_s=b_s, w_o=w_o, w_1=w_1,
                   w_2=w_2)
    mom_m = dict(c_ctx=m_c_ctx, w_ada=m_w_ada, b_ada=m_b_ada, pre1_g=m_pre1_g, post1_g=m_post1_g, pre2_g=m_pre2_g,
                 post2_g=m_post2_g, w_in=m_w_in, w_dec_f=m_w_dec_f, b_dec_f=m_b_dec_f, w_dec_b=m_w_dec_b,
                 b_dec_b=m_b_dec_b, gla_norm_g=m_gla_norm_g, sg_ln_g=m_sg_ln_g, sg_ln_b=m_sg_ln_b, w_s=m_w_s,
                 b_s=m_b_s, w_o=m_w_o, w_1=m_w_1, w_2=m_w_2)
    mom_v = dict(c_ctx=v_c_ctx, w_ada=v_w_ada, b_ada=v_b_ada, pre1_g=v_pre1_g, post1_g=v_post1_g, pre2_g=v_pre2_g,
                 post2_g=v_post2_g, w_in=v_w_in, w_dec_f=v_w_dec_f, b_dec_f=v_b_dec_f, w_dec_b=v_w_dec_b,
                 b_dec_b=v_b_dec_b, gla_norm_g=v_gla_norm_g, sg_ln_g=v_sg_ln_g, sg_ln_b=v_sg_ln_b, w_s=v_w_s,
                 b_s=v_b_s, w_o=v_w_o, w_1=v_w_1, w_2=v_w_2)

    cfg = _config(x, ctx, w_in, w_dec_f, gla_norm_g, sg_ln_g, w_s)
    n, d, h, dv, kw, vw, lr, sgw = (cfg[k] for k in ("N", "D", "H", "DV", "KW", "VW", "LR", "SGW"))
    dvs, kws = dv // N_DEV, kw // N_DEV
    ix, iy, ic = _my_pos()
    me = 4 * ix + 2 * iy + ic

    pack1 = jnp.concatenate([c.reshape(1, d), w_dec_f.reshape(1, lr * kws), w_dec_b.reshape(1, lr * kws),
                             gla_norm_g.reshape(1, h * dvs)], axis=1)
    g1 = _all_gather_vec(pack1, name="ag_small_in")
    c_all = g1[:, :d]
    o1 = d
    wdf = g1[:, o1:o1 + lr * kws].reshape(N_DEV, lr, kws).transpose(1, 0, 2).reshape(lr, kw)
    o1 += lr * kws
    wdb = g1[:, o1:o1 + lr * kws].reshape(N_DEV, lr, kws).transpose(1, 0, 2).reshape(lr, kw)
    o1 += lr * kws
    gla_g = g1[:, o1:o1 + h * dvs].reshape(N_DEV, h, dvs).transpose(1, 0, 2).reshape(1, h * dv)

    c16 = jnp.concatenate([c_all, jnp.broadcast_to(c_ctx.reshape(1, d), (N_DEV, d))], axis=0)
    ncol = w_ada.shape[2]
    wa = w_ada.reshape(d, ncol)
    b_mine = lax.dynamic_slice(b_ada, (0, me * ncol), (1, ncol))
    tn_ada = _tile(ncol, 512, LANES)
    mod_mine = _ada_fwd(c16, wa, b_mine, tn=tn_ada)
    mod_all = _all_gather_small(mod_mine, name="ag_mod").transpose(1, 0, 2).reshape(16, N_DEV * ncol)
    mod_b = lax.dynamic_slice(mod_all, (me, 0), (1, 6 * d))
    mods = [mod_b[:, i * d:(i + 1) * d] for i in range(6)]
    c_mods = [mod_all[N_DEV:N_DEV + 1, :d], mod_all[N_DEV:N_DEV + 1, d:2 * d]]

    zpad = lambda r: jnp.zeros((r, kw), F32)
    w = dict(
        sh_in=w_in.reshape(d, w_in.shape[2]).astype(BF16), sh_o=w_o.reshape(w_o.shape[1], d).astype(BF16),
        sh_1=w_1.reshape(d, w_1.shape[2]).astype(BF16), sh_2=w_2.reshape(w_2.shape[1], d).astype(BF16),
        pre1_g=pre1_g, post1_g=post1_g, pre2_g=pre2_g, post2_g=post2_g, b_dec_f=b_dec_f, b_dec_b=b_dec_b,
        wdf_pad=jnp.concatenate([wdf, zpad(LANES - lr)], axis=0),
        wdb_pad=jnp.concatenate([zpad(lr), wdb, zpad(LANES - 2 * lr)], axis=0),
        gla_g=gla_g, sg_ln_g=sg_ln_g, sg_ln_b=sg_ln_b, w_s=w_s[0],
        bs_full=jnp.repeat(b_s[0].T, sgw // cfg["SG_G"], axis=1))

    lossc, grad_x, big, small, dmod, dmod_c = _local_step(x[0], ctx[0], loss_target[0], mods, c_mods, w, cfg)
    loss = lax.psum(jnp.sum(lossc), AXES)

    order3 = ["pre1_g", "post1_g", "pre2_g", "post2_g", "w_dec_f", "b_dec_f", "w_dec_b", "b_dec_b", "gla_norm_g",
              "sg_ln_g", "sg_ln_b", "w_s", "b_s"]
    pieces = [dmod, dmod_c] + [small[k].reshape(1, -1) for k in order3]
    sizes = [p.shape[1] for p in pieces]
    g3 = _all_gather_vec(jnp.concatenate(pieces, axis=1), name="ag_small_grads")
    offs = [0]
    for s in sizes:
        offs.append(offs[-1] + s)
    dmod_all = g3[:, :6 * d]
    dmod_c_all = jnp.pad(g3[:, offs[1]:offs[2]], ((0, 0), (0, 4 * d)))
    parts8 = {k: g3[:, offs[2 + i]:offs[3 + i]] for i, k in enumerate(order3)}
    parts8["b_ada"] = dmod_all + dmod_c_all
    parts8["w_dec_f"] = lax.dynamic_slice(parts8["w_dec_f"].reshape(N_DEV, lr, kw), (0, 0, me * kws),
                                          (N_DEV, lr, kws)).reshape(N_DEV, -1)
    parts8["w_dec_b"] = lax.dynamic_slice(parts8["w_dec_b"].reshape(N_DEV, lr, kw), (0, 0, me * kws),
                                          (N_DEV, lr, kws)).reshape(N_DEV, -1)
    parts8["gla_norm_g"] = lax.dynamic_slice(parts8["gla_norm_g"].reshape(N_DEV, h, dv), (0, 0, me * dvs),
                                             (N_DEV, h, dvs)).reshape(N_DEV, -1)

    dm16 = jnp.concatenate([dmod_all, dmod_c_all], axis=0)
    dm_mine = lax.dynamic_slice(dm16, (0, me * ncol), (16, ncol))
    g_w_ada, dcc = _ada_bwd(c16, dm_mine, wa, c_ctx.reshape(1, d), tn=tn_ada)
    parts8["c_ctx"] = _all_gather_vec(dcc, name="ag_cctx")

    flat = lambda t: t.reshape(1, -1)
    g8 = _dense(jnp.concatenate([parts8[k] for k in SMALL_NAMES], axis=1))
    ws, ms, vs = [_dense(jnp.concatenate([flat(src[k]) for k in SMALL_NAMES], axis=1))[0]
                  for src in (weights, mom_m, mom_v)]
    res_small = [r.reshape(1, -1) for r in _adam_small(g8, ws, ms, vs)]
    out = {}
    off = 0
    for k in SMALL_NAMES:
        sz = weights[k].size
        out[k] = [r[:, off:off + sz].reshape(weights[k].shape) for r in res_small]
        off += sz

    rows_for = lambda r, cols: _tile(r, max(8, ELEMS_PER_BLOCK // cols), 16)
    pin_own, u_in, pin_wire = big["w_in"]
    r8, zw, cs_in = d // N_DEV, cfg["ZW"], w_in.shape[2]
    ra, rb = _w_in_splits(r8)
    ride = {"w_2": (ra, rb - ra), "w_1": (rb, r8 - rb)}

    def adam(nm, parts, sides=_NO_SIDES):
        shp = weights[nm].shape
        r2 = (shp[1], shp[2])
        res = _adam_big(parts, weights[nm].reshape(r2), mom_m[nm].reshape(r2), mom_v[nm].reshape(r2),
                        name="adam_" + nm, tr=rows_for(*r2), sides=sides)
        out[nm] = [r.reshape(shp) for r in res[:4]]
        return res[4:]

    for nm in ("w_2", "w_1"):
        own, u = big[nm]
        (u_in,) = adam(nm, [(own, 0), (u, 0), (u, 1), (u, 2)],
                       _Sides([_chip_job([pin_wire], rows=ride[nm], chained=[u_in])]))
    red = _sum_parts([(pin_own, 0), (u_in, 0), (u_in, 1), (u_in, 2)], r8, zw, name="rs_sum_w_in",
                     tr=rows_for(r8, zw))
    red = jnp.concatenate([red[:, :cfg["U0"]], red[:, cfg["L0"]:cfg["L0"] + 2 * lr], red[:, cfg["U0"]:cfg["L0"]]],
                          axis=1)
    (g_in,) = adam("w_ada", [(g_w_ada, None)],
                   _Sides([_a2a_job(red.reshape(r8, N_DEV, cs_in).transpose(1, 0, 2))]))
    own, u = big["w_o"]
    adam("w_o", [(own, 0), (u, 0), (u, 1), (u, 2)])
    adam("w_in", [(g_in.reshape(d, cs_in), None)])

    outs = [loss, grad_x[None]]
    for i in range(4):
        outs += [out[k][i] for k in WEIGHT_ORDER]
    return tuple(outs)
```

```python
import math

import jax
import jax.numpy as jnp
from jax import lax
from jax.experimental import pallas as pl
from jax.experimental.pallas import tpu as pltpu

F32 = jnp.float32
BF16 = jnp.bfloat16
MXU_DTYPE = jnp.bfloat16
HI = lax.Precision.HIGHEST

N_DEV = 8
AXES = ("x", "y", "c")
MESH = pl.DeviceIdType.MESH
LANES = 128
VMEM_LIMIT = 56 * 1024 * 1024

EPS = 1e-6
GRID_W = 64
GLA_CHUNK = 64
GLA_TAU = 16.0
ROPE_BASE = 10000.0
ADAM_LR = 0.001
ADAM_B1 = 0.9
ADAM_B2 = 0.999
ADAM_EPS = 1e-08
ADAM_WD = 0.01
ADAM_STEP = 10


def _cparams(sem):
    return pltpu.CompilerParams(dimension_semantics=sem, vmem_limit_bytes=VMEM_LIMIT)


def _tile(n, target, align):
    if n <= target:
        return n
    best = None
    for t in range(align, target + 1, align):
        if n % t == 0:
            best = t
    assert best is not None, (n, target, align)
    return best


def _dg(a, b, dims, prec=None):
    return lax.dot_general(a, b, (dims, ((), ())), precision=prec, preferred_element_type=F32)


def _nn(a, b):
    return _dg(a.astype(MXU_DTYPE), b.astype(MXU_DTYPE), ((1,), (0,)))


def _nt(a, b):
    return _dg(a.astype(MXU_DTYPE), b.astype(MXU_DTYPE), ((1,), (1,)))


def _tn(a, b):
    return _dg(a.astype(MXU_DTYPE), b.astype(MXU_DTYPE), ((0,), (0,)))


def _sigmoid(x):
    return 1.0 / (1.0 + jnp.exp(-x))


def _silu(x):
    return x * _sigmoid(x)


def _dsilu(x):
    s = _sigmoid(x)
    return s * (1.0 + x * (1.0 - s))


def _gelu(x):
    return 0.5 * x * (1.0 + lax.erf(x * (1.0 / math.sqrt(2.0))))


def _dgelu(x):
    return 0.5 * (1.0 + lax.erf(x * (1.0 / math.sqrt(2.0)))) + x * jnp.exp(-0.5 * x * x) * (1.0 / math.sqrt(2.0 * math.pi))


def _rstd(x):
    return lax.rsqrt(jnp.mean(x * x, axis=-1, keepdims=True) + EPS)


def _rms_bwd(x, r, dn):
    return r * dn - x * (r * r * r) * jnp.mean(dn * x, axis=-1, keepdims=True)


def _colsum(x):
    return jnp.sum(x, axis=0, keepdims=True)


class _Sides:
    def __init__(self, jobs):
        self.jobs = list(jobs)
        self.ins = [a for j in self.jobs for a in j["ins"]]
        self.outs = [o for j in self.jobs for o in j["outs"]]
        self.sems = [s for j in self.jobs for s in j["sems"]]

    def aliases(self, in_base, out_base):
        res, oi, oo = {}, 0, 0
        for j in self.jobs:
            for a, b in j.get("alias", {}).items():
                res[in_base + oi + a] = out_base + oo + b
            oi += len(j["ins"])
            oo += len(j["outs"])
        return res

    def run(self, phase, in_refs, out_refs, sem_refs):
        oi = oo = os_ = 0
        for j in self.jobs:
            ni, no, ns = len(j["ins"]), len(j["outs"]), len(j["sems"])
            j[phase](in_refs[oi:oi + ni], out_refs[oo:oo + no], sem_refs[os_:os_ + ns])
            oi, oo, os_ = oi + ni, oo + no, os_ + ns


_NO_SIDES = _Sides([])
ANY_SPEC = pl.BlockSpec(memory_space=pl.ANY)


def _matmul(a, b, mode, m, n, k, *, tm, tn, tk, name, out_shapes, b_spec=None, out_specs=None,
            epilogue=None, extras=(), sides=_NO_SIDES):
    nk = k // tk
    assert m % tm == 0 and n % tn == 0 and k % tk == 0, (name, m, n, k, tm, tn, tk)
    dot = {"nn": _nn, "nt": _nt, "tn": _tn}[mode]
    if mode == "tn":
        a_spec = pl.BlockSpec((tk, tm), lambda i, j, kk: (kk, i))
    else:
        a_spec = pl.BlockSpec((tm, tk), lambda i, j, kk: (i, kk))
    if b_spec is None:
        if mode == "nt":
            b_spec = pl.BlockSpec((tn, tk), lambda i, j, kk: (j, kk))
        else:
            b_spec = pl.BlockSpec((tk, tn), lambda i, j, kk: (kk, j))
    mn_spec = pl.BlockSpec((tm, tn), lambda i, j, kk: (i, j))
    if out_specs is None:
        out_specs = [mn_spec] * len(out_shapes)
    n_extra = len(extras)
    n_out = len(out_shapes)
    n_si, n_so = len(sides.ins), len(sides.outs)
    ni, nj = m // tm, n // tn

    def body(a_ref, b_ref, *rest):
        extra_refs = rest[:n_extra]
        rest = rest[n_extra:]
        side_in, rest = rest[:n_si], rest[n_si:]
        out_refs, rest = rest[:n_out], rest[n_out:]
        side_out, rest = rest[:n_so], rest[n_so:]
        acc, side_sems = rest[0], rest[1:]
        i, j, kk = pl.program_id(0), pl.program_id(1), pl.program_id(2)

        if sides.jobs:
            @pl.when((i == 0) & (j == 0) & (kk == 0))
            def _():
                sides.run("start", side_in, side_out, side_sems)

        @pl.when(kk == 0)
        def _():
            acc[...] = jnp.zeros_like(acc)

        acc[...] += dot(a_ref[...], b_ref[...])

        @pl.when(kk == nk - 1)
        def _():
            vals = (acc[...],) if epilogue is None else epilogue(acc[...], *[e[...] for e in extra_refs])
            for o, v in zip(out_refs, vals):
                o[...] = v.astype(o.dtype)

        if sides.jobs:
            @pl.when((i == ni - 1) & (j == nj - 1) & (kk == nk - 1))
            def _():
                sides.run("finish", side_in, side_out, side_sems)

    sem = ("arbitrary",) * 3 if sides.jobs else ("parallel", "parallel", "arbitrary")
    res = pl.pallas_call(
        body, name=name, grid=(ni, nj, nk),
        in_specs=[a_spec, b_spec] + [mn_spec] * n_extra + [ANY_SPEC] * n_si,
        out_specs=list(out_specs) + [ANY_SPEC] * n_so, out_shape=list(out_shapes) + list(sides.outs),
        scratch_shapes=[pltpu.VMEM((tm, tn), F32)] + list(sides.sems),
        input_output_aliases=sides.aliases(2 + n_extra, n_out),
        compiler_params=_cparams(sem),
    )(a, b, *extras, *sides.ins)
    return res


def _blocked_b_nn(ns, tk, tn):
    assert ns % tn == 0
    return pl.BlockSpec((None, tk, tn), lambda i, j, kk: ((j * tn) // ns, kk, ((j * tn) % ns) // tn))


def _blocked_b_nt(ks, tn, tk):
    assert ks % tk == 0
    return pl.BlockSpec((None, tn, tk), lambda i, j, kk: ((kk * tk) // ks, j, ((kk * tk) % ks) // tk))


def _blocked_out(ns, tm, tn):
    assert ns % tn == 0
    return pl.BlockSpec((None, tm, tn), lambda i, j, kk: ((j * tn) // ns, i, ((j * tn) % ns) // tn))


def _rows_call(body, *, name, nblk, tr, row_ins, consts, row_outs, accs=(), sides=_NO_SIDES):
    n_ri, n_c, n_ro, n_acc = len(row_ins), len(consts), len(row_outs), len(accs)
    n_si, n_so = len(sides.ins), len(sides.outs)

    def kern(*refs):
        i = pl.program_id(0)
        rin, refs = refs[:n_ri], refs[n_ri:]
        cin, refs = refs[:n_c], refs[n_c:]
        side_in, refs = refs[:n_si], refs[n_si:]
        rout, refs = refs[:n_ro], refs[n_ro:]
        acc, refs = refs[:n_acc], refs[n_acc:]
        side_out, side_sems = refs[:n_so], refs[n_so:]

        if sides.jobs:
            @pl.when(i == 0)
            def _():
                sides.run("start", side_in, side_out, side_sems)

        if n_acc:
            @pl.when(i == 0)
            def _():
                for r in acc:
                    r[...] = jnp.zeros_like(r)

        body(rin, cin, rout, acc)

        if sides.jobs:
            @pl.when(i == nblk - 1)
            def _():
                sides.run("finish", side_in, side_out, side_sems)

    in_specs = [pl.BlockSpec((tr, w), lambda i, ro=ro, co=co: (i + ro, co)) for (_, w, ro, co) in row_ins]
    in_specs += [pl.BlockSpec(cst.shape, lambda i, nd=cst.ndim: (0,) * nd) for cst in consts]
    out_specs = [pl.BlockSpec((tr, w), lambda i: (i, 0)) for (_, w, _) in row_outs]
    out_specs += [pl.BlockSpec(s, lambda i, nd=len(s): (0,) * nd) for s in accs]
    out_shape = [jax.ShapeDtypeStruct((r, w), dt) for (r, w, dt) in row_outs]
    out_shape += [jax.ShapeDtypeStruct(s, F32) for s in accs]
    return pl.pallas_call(
        kern, name=name, grid=(nblk,), in_specs=in_specs + [ANY_SPEC] * n_si,
        out_specs=out_specs + [ANY_SPEC] * n_so, out_shape=out_shape + list(sides.outs),
        scratch_shapes=list(sides.sems),
        input_output_aliases=sides.aliases(n_ri + n_c, n_ro + n_acc),
        compiler_params=_cparams(("arbitrary",)),
    )(*[r[0] for r in row_ins], *consts, *sides.ins)


def _norm_mod(x, g, shift, scale, *, name, tr):
    rows, d = x.shape

    def body(rin, cin, rout, acc):
        xv = rin[0][...]
        n = xv * _rstd(xv) * cin[0][...]
        rout[0][...] = (n * (1.0 + cin[2][...]) + cin[1][...]).astype(BF16)

    return _rows_call(body, name=name, nblk=rows // tr, tr=tr, row_ins=[(x, d, 0, 0)],
                      consts=[g, shift, scale], row_outs=[(rows, d, BF16)])[0]


def _swap_halves(t, width):
    lane = lax.broadcasted_iota(jnp.int32, t.shape, 1)
    return jnp.where(lane % 64 < 32, pltpu.roll(t, width - 32, 1), pltpu.roll(t, 32, 1))


def _gla_prep(z, tabs, wdf_pad, wdb_pad, bdf, bdb, cfg, *, tr):
    rows = z.shape[0]
    kw, h = cfg["KW"], cfg["H"]

    def body(rin, cin, rout, acc):
        zq, zk, zl = rin[0][...], rin[1][...], rin[2][...]
        cq, sq, ck, sk = [jnp.concatenate([rin[3 + t][...]] * h, axis=1) for t in range(4)]
        rout[0][...] = zq * cq + _swap_halves(zq, kw) * sq
        rout[1][...] = zk * ck + _swap_halves(zk, kw) * sk
        for o, w, b in ((2, cin[0], cin[2]), (3, cin[1], cin[3])):
            a = _nn(zl, w[...]) + b[...]
            rout[o][...] = (jnp.minimum(a, 0.0) - jnp.log(1.0 + jnp.exp(-jnp.abs(a)))) * (1.0 / GLA_TAU)

    row_ins = [(z, kw, 0, 0), (z, kw, 0, 1), (z, LANES, 0, cfg["L0"] // LANES)]
    row_ins += [(t, LANES, 0, 0) for t in tabs]
    return _rows_call(body, name="gla_prep", nblk=rows // tr, tr=tr, row_ins=row_ins,
                      consts=[wdf_pad, wdb_pad, bdf, bdb], row_outs=[(rows, kw, F32)] * 4)


def _chunk_consts(rev):
    c = GLA_CHUNK
    r = lax.broadcasted_iota(jnp.int32, (c, c), 0)
    cc = lax.broadcasted_iota(jnp.int32, (c, c), 1)
    keep = (cc >= r) if rev else (cc <= r)
    return keep, keep.astype(F32)


def _heads_per_step(cfg):
    hb = 2 if cfg["H"] % 2 == 0 else 1
    assert cfg["V0"] % (hb * cfg["DV"]) == 0
    return hb


def _chunk_decay(la, keep_f):
    b = _dg(keep_f, la, ((1,), (0,)), HI)
    return b, _colsum(la)


def _gla_fwd(qr, kr, z, la, st0, cfg, *, rev, row_off, nrows, tb, name, sides=_NO_SIDES):
    h, dk, dv = cfg["H"], cfg["DK"], cfg["DV"]
    c = GLA_CHUNK
    nsub = tb // c
    nblk = nrows // tb
    roff = row_off // tb
    hb = _heads_per_step(cfg)
    v_cb = cfg["V0"] // (hb * dv)
    n_si, n_so = len(sides.ins), len(sides.outs)

    def blk(j):
        return (nblk - 1 - j) if rev else j

    def body(q_ref, k_ref, v_ref, la_ref, st0_ref, *rest):
        side_in, rest = rest[:n_si], rest[n_si:]
        o_ref, save_ref, fin_ref = rest[:3]
        side_out, st, side_sems = rest[3:3 + n_so], rest[3 + n_so], rest[4 + n_so:]
        hh, j = pl.program_id(0), pl.program_id(1)

        if sides.jobs:
            @pl.when((hh == 0) & (j == 0))
            def _():
                sides.run("start", side_in, side_out, side_sems)

        @pl.when(j == 0)
        def _():
            st[...] = st0_ref[...]

        keep, keep_f = _chunk_consts(rev)
        order = range(nsub - 1, -1, -1) if rev else range(nsub)
        heads = range(hb)
        ksl = [slice(g * dk, (g + 1) * dk) for g in heads]
        vsl = [slice(g * dv, (g + 1) * dv) for g in heads]
        state = [st[g] for g in heads]
        for s in order:
            rs = pl.ds(s * c, c)
            q = [q_ref[rs, ksl[g]] for g in heads]
            k = [k_ref[rs, ksl[g]] for g in heads]
            v = [v_ref[rs, vsl[g]] for g in heads]
            bb = [_chunk_decay(la_ref[rs, ksl[g]], keep_f) for g in heads]
            qe = [q[g] * jnp.exp(bb[g][0]) for g in heads]
            ke = [k[g] * jnp.exp(-bb[g][0]) for g in heads]
            kl = [k[g] * jnp.exp(bb[g][1] - bb[g][0]) for g in heads]
            att = [jnp.where(keep, _nt(qe[g], ke[g]), 0.0) for g in heads]
            out = [_nt(qe[g], state[g]) + _nn(att[g], v[g]) for g in heads]
            new = [state[g] * jnp.exp(bb[g][1]) + _tn(v[g], kl[g]) for g in heads]
            for g in heads:
                save_ref[g, s] = state[g]
                o_ref[rs, vsl[g]] = out[g]
            state = new
        for g in heads:
            st[g] = state[g]

        @pl.when(j == nblk - 1)
        def _():
            fin_ref[...] = st[...]

        if sides.jobs:
            @pl.when((hh == h // hb - 1) & (j == nblk - 1))
            def _():
                sides.run("finish", side_in, side_out, side_sems)

    in_specs = [
        pl.BlockSpec((tb, hb * dk), lambda hh, j: (roff + blk(j), hh)),
        pl.BlockSpec((tb, hb * dk), lambda hh, j: (roff + blk(j), hh)),
        pl.BlockSpec((tb, hb * dv), lambda hh, j: (roff + blk(j), v_cb + hh)),
        pl.BlockSpec((tb, hb * dk), lambda hh, j: (roff + blk(j), hh)),
        pl.BlockSpec((hb, dv, dk), lambda hh, j: (hh, 0, 0)),
    ]
    out_specs = [
        pl.BlockSpec((tb, hb * dv), lambda hh, j: (blk(j), hh)),
        pl.BlockSpec((hb, nsub, dv, dk), lambda hh, j: (hh, blk(j), 0, 0)),
        pl.BlockSpec((hb, dv, dk), lambda hh, j: (hh, 0, 0)),
    ]
    out_shape = [
        jax.ShapeDtypeStruct((nrows, h * dv), F32),
        jax.ShapeDtypeStruct((h, nrows // c, dv, dk), F32),
        jax.ShapeDtypeStruct((h, dv, dk), F32),
    ]
    return pl.pallas_call(
        body, name=name, grid=(h // hb, nblk), in_specs=in_specs + [ANY_SPEC] * n_si,
        out_specs=out_specs + [ANY_SPEC] * n_so, out_shape=out_shape + list(sides.outs),
        scratch_shapes=[pltpu.VMEM((hb, dv, dk), F32)] + list(sides.sems),
        input_output_aliases=sides.aliases(5, 3),
        compiler_params=_cparams(("arbitrary", "arbitrary")),
    )(qr, kr, z, la, st0, *sides.ins)


def _gla_bwd(qr, kr, z, la, do, save, dst_init, cfg, *, rev, row_off, nrows, tb, name, sides=_NO_SIDES):
    h, dk, dv = cfg["H"], cfg["DK"], cfg["DV"]
    c = GLA_CHUNK
    nsub = tb // c
    nblk = nrows // tb
    roff = row_off // tb
    hb = _heads_per_step(cfg)
    v_cb = cfg["V0"] // (hb * dv)
    n_si, n_so = len(sides.ins), len(sides.outs)

    def blk(j):
        return j if rev else (nblk - 1 - j)

    def body(q_ref, k_ref, v_ref, la_ref, do_ref, save_ref, di_ref, *rest):
        side_in, rest = rest[:n_si], rest[n_si:]
        dq_ref, dk_ref, dv_ref, dla_ref, d0_ref = rest[:5]
        side_out, dst, side_sems = rest[5:5 + n_so], rest[5 + n_so], rest[6 + n_so:]
        hh, j = pl.program_id(0), pl.program_id(1)

        if sides.jobs:
            @pl.when((hh == 0) & (j == 0))
            def _():
                sides.run("start", side_in, side_out, side_sems)

        @pl.when(j == 0)
        def _():
            dst[...] = di_ref[...]

        keep, keep_f = _chunk_consts(rev)
        keep_t = _chunk_consts(not rev)[1]
        order = range(nsub) if rev else range(nsub - 1, -1, -1)
        heads = range(hb)
        ksl = [slice(g * dk, (g + 1) * dk) for g in heads]
        vsl = [slice(g * dv, (g + 1) * dv) for g in heads]
        d_after = [dst[g] for g in heads]
        for s in order:
            rs = pl.ds(s * c, c)
            q = [q_ref[rs, ksl[g]] for g in heads]
            k = [k_ref[rs, ksl[g]] for g in heads]
            v = [v_ref[rs, vsl[g]] for g in heads]
            lac = [la_ref[rs, ksl[g]] for g in heads]
            dout = [do_ref[rs, vsl[g]] for g in heads]
            s_in = [save_ref[g, s] for g in heads]
            bb = [_chunk_decay(lac[g], keep_f) for g in heads]
            eb = [jnp.exp(bb[g][0]) for g in heads]
            enb = [jnp.exp(-bb[g][0]) for g in heads]
            elb = [jnp.exp(bb[g][1] - bb[g][0]) for g in heads]
            etot = [jnp.exp(bb[g][1]) for g in heads]
            qe = [q[g] * eb[g] for g in heads]
            ke = [k[g] * enb[g] for g in heads]
            kl = [k[g] * elb[g] for g in heads]
            att = [jnp.where(keep, _nt(qe[g], ke[g]), 0.0) for g in heads]
            datt = [jnp.where(keep, _nt(dout[g], v[g]), 0.0) for g in heads]
            dqe = [_nn(dout[g], s_in[g]) + _nn(datt[g], ke[g]) for g in heads]
            dke = [_tn(datt[g], qe[g]) for g in heads]
            dkl = [_nn(v[g], d_after[g]) for g in heads]
            dvv = [_tn(att[g], dout[g]) + _nt(kl[g], d_after[g]) for g in heads]
            db = [dqe[g] * qe[g] - dke[g] * ke[g] - dkl[g] * kl[g] for g in heads]
            dbtot = [_colsum(dkl[g] * kl[g]) + _colsum(d_after[g] * s_in[g]) * etot[g] for g in heads]
            dla = [_dg(keep_t, db[g], ((1,), (0,)), HI) + dbtot[g] for g in heads]
            d_after = [d_after[g] * etot[g] + _tn(dout[g], qe[g]) for g in heads]
            for g in heads:
                dv_ref[rs, vsl[g]] = dvv[g]
                dla_ref[rs, ksl[g]] = dla[g]
                dq_ref[rs, ksl[g]] = dqe[g] * eb[g]
                dk_ref[rs, ksl[g]] = dke[g] * enb[g] + dkl[g] * elb[g]
        for g in heads:
            dst[g] = d_after[g]

        @pl.when(j == nblk - 1)
        def _():
            d0_ref[...] = dst[...]

        if sides.jobs:
            @pl.when((hh == h // hb - 1) & (j == nblk - 1))
            def _():
                sides.run("finish", side_in, side_out, side_sems)

    in_specs = [
        pl.BlockSpec((tb, hb * dk), lambda hh, j: (roff + blk(j), hh)),
        pl.BlockSpec((tb, hb * dk), lambda hh, j: (roff + blk(j), hh)),
        pl.BlockSpec((tb, hb * dv), lambda hh, j: (roff + blk(j), v_cb + hh)),
        pl.BlockSpec((tb, hb * dk), lambda hh, j: (roff + blk(j), hh)),
        pl.BlockSpec((tb, hb * dv), lambda hh, j: (blk(j), hh)),
        pl.BlockSpec((hb, nsub, dv, dk), lambda hh, j: (hh, blk(j), 0, 0)),
        pl.BlockSpec((hb, dv, dk), lambda hh, j: (hh, 0, 0)),
    ]
    out_specs = [
        pl.BlockSpec((tb, hb * dk), lambda hh, j: (blk(j), hh)),
        pl.BlockSpec((tb, hb * dk), lambda hh, j: (blk(j), hh)),
        pl.BlockSpec((tb, hb * dv), lambda hh, j: (blk(j), hh)),
        pl.BlockSpec((tb, hb * dk), lambda hh, j: (blk(j), hh)),
        pl.BlockSpec((hb, dv, dk), lambda hh, j: (hh, 0, 0)),
    ]
    out_shape = [
        jax.ShapeDtypeStruct((nrows, h * dk), F32),
        jax.ShapeDtypeStruct((nrows, h * dk), F32),
        jax.ShapeDtypeStruct((nrows, h * dv), F32),
        jax.ShapeDtypeStruct((nrows, h * dk), F32),
        jax.ShapeDtypeStruct((h, dv, dk), F32),
    ]
    return pl.pallas_call(
        body, name=name, grid=(h // hb, nblk), in_specs=in_specs + [ANY_SPEC] * n_si,
        out_specs=out_specs + [ANY_SPEC] * n_so, out_shape=out_shape + list(sides.outs),
        scratch_shapes=[pltpu.VMEM((hb, dv, dk), F32)] + list(sides.sems),
        input_output_aliases=sides.aliases(7, 5),
        compiler_params=_cparams(("arbitrary", "arbitrary")),
    )(qr, kr, z, la, do, save, dst_init, *sides.ins)


def _gla_post(gf, gb, la_f, la_b, z, tabs, wdf_pad, wdb_pad, cfg, *, row_off, nrows, tr, name):
    kw, vw = cfg["KW"], cfg["VW"]
    h = cfg["H"]
    ro = row_off // tr

    def body(rin, cin, rout, acc):
        dq = rin[0][...] + rin[1][...]
        dk_ = rin[2][...] + rin[3][...]
        zl = rin[10][...]
        cq, sq, ck, sk = [jnp.concatenate([rin[11 + t][...]] * h, axis=1) for t in range(4)]
        rout[0][...] = (dq * cq + _swap_halves(dq * sq, kw)).astype(BF16)
        rout[1][...] = (dk_ * ck + _swap_halves(dk_ * sk, kw)).astype(BF16)
        rout[2][...] = (rin[8][...] + rin[9][...]).astype(BF16)
        dzl = jnp.zeros(zl.shape, F32)
        for t, w in ((0, cin[0]), (1, cin[1])):
            la = rin[6 + t][...]
            da = rin[4 + t][...] * ((1.0 - jnp.exp(la * GLA_TAU)) * (1.0 / GLA_TAU))
            dzl = dzl + _nt(da, w[...])
            acc[t][...] += _tn(zl, da)
            acc[2 + t][...] += _colsum(da)
        rout[3][...] = dzl.astype(BF16)

    row_ins = [(gf[0], kw, 0, 0), (gb[0], kw, 0, 0), (gf[1], kw, 0, 0), (gb[1], kw, 0, 0),
               (gf[3], kw, 0, 0), (gb[3], kw, 0, 0), (la_f, kw, ro, 0), (la_b, kw, ro, 0),
               (gf[2], vw, 0, 0), (gb[2], vw, 0, 0), (z, LANES, ro, cfg["L0"] // LANES)]
    row_ins += [(t, LANES, ro, 0) for t in tabs]
    return _rows_call(body, name=name, nblk=nrows // tr, tr=tr, row_ins=row_ins, consts=[wdf_pad, wdb_pad],
                      row_outs=[(nrows, kw, BF16), (nrows, kw, BF16), (nrows, vw, BF16), (nrows, LANES, BF16)],
                      accs=[(LANES, kw), (LANES, kw), (1, kw), (1, kw)])


def _readout_fwd(o_f, o_b, z, g, cfg, *, tr):
    n, vw = o_f.shape
    h, dv = cfg["H"], cfg["DV"]

    def body(rin, cin, rout, acc):
        for hh in range(h):
            cs = slice(hh * dv, (hh + 1) * dv)
            oh = rin[0][:, cs] + rin[1][:, cs]
            y = oh * _rstd(oh) * cin[0][:, cs]
            rout[0][:, cs] = (y * _silu(rin[2][:, cs])).astype(BF16)

    return _rows_call(body, name="gla_readout", nblk=n // tr, tr=tr,
                      row_ins=[(o_f, vw, 0, 0), (o_b, vw, 0, 0), (z, vw, 0, cfg["R0"] // vw)], consts=[g],
                      row_outs=[(n, vw, BF16)])[0]


def _readout_bwd(o_f, o_b, z, dycat, g, cfg, *, tr):
    n, vw = o_f.shape
    h, dv = cfg["H"], cfg["DV"]

    def body(rin, cin, rout, acc):
        for hh in range(h):
            cs = slice(hh * dv, (hh + 1) * dv)
            oh = rin[0][:, cs] + rin[1][:, cs]
            r, dyg, gh = rin[2][:, cs], rin[3][:, cs], cin[0][:, cs]
            rs = _rstd(oh)
            dy = dyg * _silu(r)
            rout[0][:, cs] = _rms_bwd(oh, rs, dy * gh).astype(BF16)
            rout[1][:, cs] = (dyg * (oh * rs * gh) * _dsilu(r)).astype(BF16)
            acc[0][:, cs] += _colsum(dy * oh * rs)

    return _rows_call(body, name="gla_readout_bwd", nblk=n // tr, tr=tr,
                      row_ins=[(o_f, vw, 0, 0), (o_b, vw, 0, 0), (z, vw, 0, cfg["R0"] // vw), (dycat, vw, 0, 0)],
                      consts=[g], row_outs=[(n, vw, BF16), (n, vw, BF16)], accs=[(1, vw)])


def _sg_ln(vv):
    mu = jnp.mean(vv, axis=-1, keepdims=True)
    cen = vv - mu
    rstd = lax.rsqrt(jnp.mean(cen * cen, axis=-1, keepdims=True) + EPS)
    return cen * rstd, rstd


def _sg_fwd(z, n, lng, lnb, w_s, bs_full, cfg):
    sgw, grp, sc = cfg["SGW"], cfg["SG_G"], cfg["SG_C"]
    gw = sgw // grp

    def body(rin, cin, rout, acc):
        u = _gelu(rin[0][...])
        xhat, _ = _sg_ln(_gelu(rin[1][...]))
        vvn = xhat * cin[0][...] + cin[1][...]
        for gg in range(grp):
            cs = slice(gg * gw, (gg + 1) * gw)
            s = _nn(cin[2][gg], vvn[:, cs]) + cin[3][:, cs]
            rout[0][:, cs] = (u[:, cs] * s).astype(BF16)

    return _rows_call(body, name="sg_fwd", nblk=n // sc, tr=sc,
                      row_ins=[(z, sgw, 0, cfg["U0"] // sgw), (z, sgw, 0, cfg["VV0"] // sgw)],
                      consts=[lng, lnb, w_s, bs_full], row_outs=[(n, sgw, BF16)])[0]


def _sg_bwd(z, dycat, n, lng, lnb, w_s, bs_full, cfg):
    sgw, grp, sc = cfg["SGW"], cfg["SG_G"], cfg["SG_C"]
    gw = sgw // grp

    def body(rin, cin, rout, acc):
        up, vp, dy = rin[0][...], rin[1][...], rin[2][...]
        u = _gelu(up)
        xhat, rstd = _sg_ln(_gelu(vp))
        lng_v = cin[0][...]
        vvn = xhat * lng_v + cin[1][...]
        ds = dy * u
        acc[1][...] += ds
        dvvn_parts = []
        for gg in range(grp):
            cs = slice(gg * gw, (gg + 1) * gw)
            w = cin[2][gg]
            s = _nn(w, vvn[:, cs]) + cin[3][:, cs]
            rout[0][:, cs] = (dy[:, cs] * s * _dgelu(up[:, cs])).astype(BF16)
            acc[0][gg] += _nt(ds[:, cs], vvn[:, cs])
            dvvn_parts.append(_tn(w, ds[:, cs]))
        dvvn = jnp.concatenate(dvvn_parts, axis=1)
        acc[2][...] += _colsum(dvvn * xhat)
        acc[3][...] += _colsum(dvvn)
        dxh = dvvn * lng_v
        dvv = rstd * (dxh - jnp.mean(dxh, axis=-1, keepdims=True)
                      - xhat * jnp.mean(dxh * xhat, axis=-1, keepdims=True))
        rout[0][:, sgw:] = (dvv * _dgelu(vp)).astype(BF16)

    vw = cfg["VW"]
    return _rows_call(body, name="sg_bwd", nblk=n // sc, tr=sc,
                      row_ins=[(z, sgw, 0, cfg["U0"] // sgw), (z, sgw, 0, cfg["VV0"] // sgw),
                               (dycat, sgw, 0, vw // sgw)],
                      consts=[lng, lnb, w_s, bs_full], row_outs=[(n, 2 * sgw, BF16)],
                      accs=[(grp, sc, sc), (sc, sgw), (1, sgw), (1, sgw)])


def _mid_fwd(x, mix, g1, post1, pre2, sh2, sc2, *, tr, sides=_NO_SIDES):
    n, d = x.shape

    def body(rin, cin, rout, acc):
        xv, mv = rin[0][...], rin[1][...]
        x1 = xv + cin[0][...] * (mv * _rstd(mv) * cin[1][...])
        rout[0][...] = x1
        n2 = x1 * _rstd(x1) * cin[2][...]
        rout[1][...] = (n2 * (1.0 + cin[4][...]) + cin[3][...]).astype(BF16)

    return _rows_call(body, name="mid_fwd", nblk=n // tr, tr=tr, row_ins=[(x, d, 0, 0), (mix, d, 0, 0)],
                      consts=[g1, post1, pre2, sh2, sc2], row_outs=[(n, d, F32), (n, d, BF16)], sides=sides)


def _head_bwd(x1, m2, target, g2, post2, *, tr):
    n, d = x1.shape

    def body(rin, cin, rout, acc):
        x1v, mv, tv = rin[0][...], rin[1][...], rin[2][...]
        g2v, pg = cin[0][...], cin[1][...]
        r = _rstd(mv)
        y2 = mv * r * pg
        err = (x1v + g2v * y2) - tv
        acc[2][...] += _colsum(err * err) * (0.5 / d)
        dx2 = err * (1.0 / d)
        rout[0][...] = dx2
        dy2 = dx2 * g2v
        acc[0][...] += _colsum(dx2 * y2)
        acc[1][...] += _colsum(dy2 * mv * r)
        rout[1][...] = _rms_bwd(mv, r, dy2 * pg).astype(BF16)

    return _rows_call(body, name="head_bwd", nblk=n // tr, tr=tr,
                      row_ins=[(x1, d, 0, 0), (m2, d, 0, 0), (target, d, 0, 0)], consts=[g2, post2],
                      row_outs=[(n, d, F32), (n, d, BF16)], accs=[(1, d)] * 3)


def _mid_bwd(dh2, x1, dx2, mix, sc2, pre2, g1, post1, *, tr, sides=_NO_SIDES):
    n, d = x1.shape

    def body(rin, cin, rout, acc):
        dh, x1v, dx2v, mv = rin[0][...], rin[1][...], rin[2][...], rin[3][...]
        sc2v, pre2v, g1v, post1v = cin[0][...], cin[1][...], cin[2][...], cin[3][...]
        r2 = _rstd(x1v)
        xr = x1v * r2
        acc[0][...] += _colsum(dh)
        acc[1][...] += _colsum(dh * (xr * pre2v))
        dn2 = dh * (1.0 + sc2v)
        acc[2][...] += _colsum(dn2 * xr)
        dx1 = dx2v + _rms_bwd(x1v, r2, dn2 * pre2v)
        rout[0][...] = dx1
        r1 = _rstd(mv)
        mr = mv * r1
        acc[3][...] += _colsum(dx1 * (mr * post1v))
        dy1 = dx1 * g1v
        acc[4][...] += _colsum(dy1 * mr)
        rout[1][...] = _rms_bwd(mv, r1, dy1 * post1v).astype(BF16)

    return _rows_call(body, name="mid_bwd", nblk=n // tr, tr=tr,
                      row_ins=[(dh2, d, 0, 0), (x1, d, 0, 0), (dx2, d, 0, 0), (mix, d, 0, 0)],
                      consts=[sc2, pre2, g1, post1], row_outs=[(n, d, F32), (n, d, BF16)], accs=[(1, d)] * 5,
                      sides=sides)


def _in_bwd(da, x, dres, sc1, pre1, *, row_off, tr, name, sides=_NO_SIDES):
    n, d = x.shape
    with_res = dres is not None

    def body(rin, cin, rout, acc):
        dav, xv = rin[0][...], rin[1][...]
        sc1v, pre1v = cin[0][...], cin[1][...]
        r = _rstd(xv)
        xr = xv * r
        acc[0][...] += _colsum(dav)
        acc[1][...] += _colsum(dav * (xr * pre1v))
        dn = dav * (1.0 + sc1v)
        acc[2][...] += _colsum(dn * xr)
        if with_res:
            rout[0][...] = rin[2][...] + _rms_bwd(xv, r, dn * pre1v)

    row_ins = [(da, d, row_off // tr, 0), (x, d, 0, 0)] + ([(dres, d, 0, 0)] if with_res else [])
    return _rows_call(body, name=name, nblk=n // tr, tr=tr, row_ins=row_ins, consts=[sc1, pre1],
                      row_outs=[(n, d, F32)] if with_res else [], accs=[(1, d)] * 3, sides=sides)


def _ada_fwd(c16, w, b, *, tn):
    d, ncol = w.shape

    def body(c_ref, w_ref, b_ref, o_ref):
        o_ref[...] = _nn(_silu(c_ref[...]), w_ref[...]) + b_ref[...]

    return pl.pallas_call(
        body, name="ada_fwd", grid=(ncol // tn,),
        in_specs=[pl.BlockSpec((16, d), lambda j: (0, 0)), pl.BlockSpec((d, tn), lambda j: (0, j)),
                  pl.BlockSpec((1, tn), lambda j: (0, j))],
        out_specs=pl.BlockSpec((16, tn), lambda j: (0, j)),
        out_shape=jax.ShapeDtypeStruct((16, ncol), F32),
        compiler_params=_cparams(("arbitrary",)),
    )(c16, w, b)


def _ada_bwd(c16, dm, w, c_ctx, *, tn):
    d, ncol = w.shape

    def body(c_ref, dm_ref, w_ref, cc_ref, gw_ref, dcc_ref, acc):
        j = pl.program_id(0)

        @pl.when(j == 0)
        def _():
            acc[...] = jnp.zeros_like(acc)

        gw_ref[...] = _tn(_silu(c_ref[...]), dm_ref[...])
        acc[...] += _nt(dm_ref[...], w_ref[...])

        @pl.when(j == ncol // tn - 1)
        def _():
            dcc_ref[...] = _colsum(acc[8:16, :]) * _dsilu(cc_ref[...])

    return pl.pallas_call(
        body, name="ada_bwd", grid=(ncol // tn,),
        in_specs=[pl.BlockSpec((16, d), lambda j: (0, 0)), pl.BlockSpec((16, tn), lambda j: (0, j)),
                  pl.BlockSpec((d, tn), lambda j: (0, j)), pl.BlockSpec((1, d), lambda j: (0, 0))],
        out_specs=[pl.BlockSpec((d, tn), lambda j: (0, j)), pl.BlockSpec((1, d), lambda j: (0, 0))],
        out_shape=[jax.ShapeDtypeStruct((d, ncol), F32), jax.ShapeDtypeStruct((1, d), F32)],
        scratch_shapes=[pltpu.VMEM((16, d), F32)],
        compiler_params=_cparams(("arbitrary",)),
    )(c16, dm, w, c_ctx)


def _adam_math(w, g, m, v):
    m = ADAM_B1 * m + (1.0 - ADAM_B1) * g
    v = ADAM_B2 * v + (1.0 - ADAM_B2) * (g * g)
    m_hat = m / (1.0 - ADAM_B1 ** ADAM_STEP)
    v_hat = v / (1.0 - ADAM_B2 ** ADAM_STEP)
    delta = -ADAM_LR * (m_hat / (jnp.sqrt(v_hat) + ADAM_EPS) + ADAM_WD * w)
    return delta, m, v


def _adam_big(parts, w, m, v, *, name, tr, sides=_NO_SIDES):
    rows, cols = w.shape
    n_p = len(parts)
    n_si, n_so = len(sides.ins), len(sides.outs)
    nblk = rows // tr

    def body(*refs):
        ins, refs = refs[:n_p + 3], refs[n_p + 3:]
        side_in, refs = refs[:n_si], refs[n_si:]
        outs, side_out, side_sems = refs[:4], refs[4:4 + n_so], refs[4 + n_so:]
        i = pl.program_id(0)

        if sides.jobs:
            @pl.when(i == 0)
            def _():
                sides.run("start", side_in, side_out, side_sems)

        g = ins[0][...]
        for p in ins[1:n_p]:
            g = g + p[...].astype(F32)
        delta, m2, v2 = _adam_math(ins[n_p][...], g, ins[n_p + 1][...], ins[n_p + 2][...])
        outs[0][...] = g
        outs[1][...] = delta
        outs[2][...] = m2
        outs[3][...] = v2

        if sides.jobs:
            @pl.when(i == nblk - 1)
            def _():
                sides.run("finish", side_in, side_out, side_sems)

    plain = pl.BlockSpec((tr, cols), lambda i: (i, 0))
    in_specs = []
    for arr, idx in parts:
        if idx is None:
            in_specs.append(plain)
        else:
            in_specs.append(pl.BlockSpec((None, tr, cols), lambda i, idx=idx: (idx, i, 0)))
    in_specs += [plain] * 3
    return pl.pallas_call(
        body, name=name, grid=(nblk,), in_specs=in_specs + [ANY_SPEC] * n_si,
        out_specs=[plain] * 4 + [ANY_SPEC] * n_so,
        out_shape=[jax.ShapeDtypeStruct((rows, cols), F32)] * 4 + list(sides.outs),
        scratch_shapes=list(sides.sems), input_output_aliases=sides.aliases(n_p + 3, 4),
        compiler_params=_cparams(("arbitrary",) if sides.jobs else ("parallel",)),
    )(*[p[0] for p in parts], w, m, v, *sides.ins)


def _sum_parts(parts, rows, cols, *, name, tr):
    def body(*refs):
        g = refs[0][...].astype(F32)
        for p in refs[1:-1]:
            g = g + p[...].astype(F32)
        refs[-1][...] = g

    in_specs = [pl.BlockSpec((None, tr, cols), lambda i, idx=idx: (idx, i, 0)) for _, idx in parts]
    return pl.pallas_call(
        body, name=name, grid=(rows // tr,), in_specs=in_specs, out_specs=pl.BlockSpec((tr, cols), lambda i: (i, 0)),
        out_shape=jax.ShapeDtypeStruct((rows, cols), F32), compiler_params=_cparams(("parallel",)),
    )(*[p[0] for p in parts])


def _adam_small(g8, w, m, v):
    def body(g_ref, w_ref, m_ref, v_ref, go, do, mo, vo):
        g = g_ref[0]
        for r in range(1, N_DEV):
            g = g + g_ref[r]
        delta, m2, v2 = _adam_math(w_ref[...], g, m_ref[...], v_ref[...])
        go[...] = g
        do[...] = delta
        mo[...] = m2
        vo[...] = v2

    return pl.pallas_call(
        body, name="adam_small", out_shape=[jax.ShapeDtypeStruct(w.shape, F32)] * 4,
        compiler_params=pltpu.CompilerParams(vmem_limit_bytes=VMEM_LIMIT),
    )(g8, w, m, v)


VEC_W = 1024
TK = 2048
ELEMS_PER_BLOCK = 256 * 1024


def _dense(v):
    a, k = v.shape
    kp = -(-k // (8 * VEC_W)) * (8 * VEC_W)
    return jnp.pad(v, ((0, 0), (0, kp - k))).reshape(a, kp // VEC_W, VEC_W)


def _all_gather_vec(v, *, name):
    k = v.shape[1]
    return _all_gather_small(_dense(v)[0], name=name).reshape(N_DEV, -1)[:, :k]


def _my_pos():
    return lax.axis_index("x"), lax.axis_index("y"), lax.axis_index("c")


def _flip(v, bit):
    return (1 - v) if bit else v


def _all_gather_small(v, *, name):
    r, k = v.shape

    def body(v_ref, out_ref, send, recv, lsem):
        x, y, c = _my_pos()
        me = 4 * x + 2 * y + c
        local = pltpu.make_async_copy(v_ref, out_ref.at[me], lsem)
        local.start()
        sends = []
        for kk in range(1, N_DEV):
            peer = (_flip(x, kk & 4), _flip(y, kk & 2), _flip(c, kk & 1))
            cp = pltpu.make_async_remote_copy(src_ref=v_ref, dst_ref=out_ref.at[me], send_sem=send.at[kk - 1],
                                              recv_sem=recv.at[kk - 1], device_id=peer, device_id_type=MESH)
            cp.start()
            sends.append(cp)
        for kk in range(1, N_DEV):
            px, py, pc = _flip(x, kk & 4), _flip(y, kk & 2), _flip(c, kk & 1)
            src = 4 * px + 2 * py + pc
            pltpu.make_async_remote_copy(src_ref=v_ref, dst_ref=out_ref.at[src], send_sem=send.at[kk - 1],
                                         recv_sem=recv.at[kk - 1], device_id=(px, py, pc),
                                         device_id_type=MESH).wait_recv()
        for cp in sends:
            cp.wait_send()
        local.wait()

    return pl.pallas_call(
        body, name=name, out_shape=jax.ShapeDtypeStruct((N_DEV, r, k), v.dtype),
        in_specs=[pl.BlockSpec(memory_space=pltpu.VMEM)], out_specs=pl.BlockSpec(memory_space=pltpu.VMEM),
        scratch_shapes=[pltpu.SemaphoreType.DMA((N_DEV - 1,)), pltpu.SemaphoreType.DMA((N_DEV - 1,)),
                        pltpu.SemaphoreType.DMA],
        compiler_params=pltpu.CompilerParams(vmem_limit_bytes=VMEM_LIMIT),
    )(v)


def _ag_job(shards, rows=None, chained=None):
    n_arr = len(shards)

    def part(ref):
        return ref if rows is None else ref.at[pl.ds(rows[0], rows[1])]

    def tools(ins, outs, sems):
        send, recv, lsem = sems
        x, y, c = _my_pos()
        chips = [(1 - x, y), (x, 1 - y), (1 - x, 1 - y)]

        def copy(a, kk, block, to, src=None):
            dst = part(outs[a].at[4 * block[0] + 2 * block[1] + block[2]])
            return pltpu.make_async_remote_copy(src_ref=dst if src is None else part(src), dst_ref=dst,
                                                send_sem=send.at[a, kk], recv_sem=recv.at[a, kk],
                                                device_id=to, device_id_type=MESH)

        locals_ = [pltpu.make_async_copy(part(ins[a]), part(outs[a].at[4 * x + 2 * y + c]), lsem.at[a])
                   for a in range(n_arr)]
        firsts = []
        for a in range(n_arr):
            firsts.append(copy(a, 0, (x, y, c), (x, y, 1 - c), src=ins[a]))
            firsts += [copy(a, 1 + j, (x, y, c), (*chip, c), src=ins[a]) for j, chip in enumerate(chips)]
        return copy, locals_, firsts, chips, (x, y, c)

    def start(ins, outs, sems):
        _, locals_, firsts, _, _ = tools(ins, outs, sems)
        for cp in locals_ + firsts:
            cp.start()

    def finish(ins, outs, sems):
        copy, locals_, firsts, chips, (x, y, c) = tools(ins, outs, sems)
        me, sibling = (x, y, c), (x, y, 1 - c)
        passed = []
        for a in range(n_arr):
            for j, chip in enumerate(chips):
                copy(a, 1 + j, (*chip, c), me).wait_recv()
                fw = copy(a, 4 + j, (*chip, c), sibling)
                fw.start()
                passed.append(fw)
        for a in range(n_arr):
            copy(a, 0, sibling, me).wait_recv()
            for j, chip in enumerate(chips):
                copy(a, 4 + j, (*chip, 1 - c), me).wait_recv()
        for cp in firsts + passed:
            cp.wait_send()
        for lc in locals_:
            lc.wait()

    job = dict(ins=list(shards), outs=[jax.ShapeDtypeStruct((N_DEV,) + s.shape, s.dtype) for s in shards],
               sems=[pltpu.SemaphoreType.DMA((n_arr, 7)), pltpu.SemaphoreType.DMA((n_arr, 7)),
                     pltpu.SemaphoreType.DMA((n_arr,))], start=start, finish=finish)
    if chained is not None:
        job["ins"] = list(shards) + list(chained)
        job["alias"] = {n_arr + a: a for a in range(n_arr)}
    return job


def _exchange_job(arrays, n_slots, out_slots, src_of, dst_of, peer_of, rows=None, chained=None):
    n_arr = len(arrays)

    def copies(ins, outs, sems):
        send, recv = sems
        x, y, c = _my_pos()
        res = []
        for a in range(n_arr):
            for s in range(n_slots):
                src, dst = ins[a].at[src_of(s, x, y, c)], outs[a].at[dst_of(s)]
                if rows is not None:
                    src, dst = src.at[pl.ds(rows[0], rows[1])], dst.at[pl.ds(rows[0], rows[1])]
                res.append(pltpu.make_async_remote_copy(
                    src_ref=src, dst_ref=dst, send_sem=send.at[a, s], recv_sem=recv.at[a, s],
                    device_id=peer_of(s, x, y, c), device_id_type=MESH))
        return res

    def start(ins, outs, sems):
        for cp in copies(ins, outs, sems):
            cp.start()

    def finish(ins, outs, sems):
        cps = copies(ins, outs, sems)
        for cp in cps:
            cp.wait_recv()
        for cp in cps:
            cp.wait_send()

    job = dict(ins=list(arrays), outs=[jax.ShapeDtypeStruct((out_slots,) + g.shape[1:], g.dtype) for g in arrays],
               sems=[pltpu.SemaphoreType.DMA((n_arr, n_slots)), pltpu.SemaphoreType.DMA((n_arr, n_slots))],
               start=start, finish=finish)
    if chained is not None:
        job["ins"] = list(arrays) + list(chained)
        job["alias"] = {n_arr + a: a for a in range(n_arr)}
    return job


def _pair_job(grads):
    return _exchange_job(
        grads, 4, 4,
        src_of=lambda s, x, y, c: 4 * _flip(x, s & 2) + 2 * _flip(y, s & 1) + (1 - c),
        dst_of=lambda s: s, peer_of=lambda s, x, y, c: (x, y, 1 - c))


def _chip_job(sums, rows=None, chained=None):
    return _exchange_job(
        sums, 3, 3, src_of=lambda s, x, y, c: s, dst_of=lambda s: s,
        peer_of=lambda s, x, y, c: (_flip(x, (s + 1) & 2), _flip(y, (s + 1) & 1), c), rows=rows, chained=chained)


def _a2a_job(x):
    def copies(ins, outs, sems):
        send, recv, lsem = sems
        x_, y_, c_ = _my_pos()
        me = 4 * x_ + 2 * y_ + c_
        local = pltpu.make_async_copy(ins[0].at[me], outs[0].at[me], lsem)
        res = []
        for s in range(1, N_DEV):
            px, py, pc = _flip(x_, s & 4), _flip(y_, s & 2), _flip(c_, s & 1)
            res.append(pltpu.make_async_remote_copy(
                src_ref=ins[0].at[4 * px + 2 * py + pc], dst_ref=outs[0].at[me], send_sem=send.at[s - 1],
                recv_sem=recv.at[s - 1], device_id=(px, py, pc), device_id_type=MESH))
        return local, res

    def start(ins, outs, sems):
        local, res = copies(ins, outs, sems)
        local.start()
        for cp in res:
            cp.start()

    def finish(ins, outs, sems):
        local, res = copies(ins, outs, sems)
        for cp in res:
            cp.wait_recv()
        for cp in res:
            cp.wait_send()
        local.wait()

    return dict(ins=[x], outs=[jax.ShapeDtypeStruct(x.shape, x.dtype)],
                sems=[pltpu.SemaphoreType.DMA((N_DEV - 1,)), pltpu.SemaphoreType.DMA((N_DEV - 1,)),
                      pltpu.SemaphoreType.DMA], start=start, finish=finish)


def _run_sides(sides, *, name):
    n_si, n_so = len(sides.ins), len(sides.outs)

    def body(*refs):
        ins, outs, sems = refs[:n_si], refs[n_si:n_si + n_so], refs[n_si + n_so:]
        sides.run("start", ins, outs, sems)
        sides.run("finish", ins, outs, sems)

    return pl.pallas_call(
        body, name=name, out_shape=list(sides.outs), in_specs=[ANY_SPEC] * n_si, out_specs=[ANY_SPEC] * n_so,
        scratch_shapes=list(sides.sems), input_output_aliases=sides.aliases(0, 0),
    )(*sides.ins)


def _pair_add(g, t, *, name, tr, wire):
    _, r, cols = g.shape
    g4 = g.reshape(4, 2, r, cols)
    j0 = 1 if wire else 0

    def g_index(j, i):
        x, y, c = _my_pos()
        return (jnp.bitwise_xor(2 * x + y, j + j0), c, i, 0)

    def body(g_ref, t_ref, o_ref):
        o_ref[...] = (g_ref[...] + t_ref[...]).astype(o_ref.dtype)

    return pl.pallas_call(
        body, name=name, grid=(3 if wire else 1, r // tr),
        in_specs=[pl.BlockSpec((None, None, tr, cols), g_index),
                  pl.BlockSpec((None, tr, cols), lambda j, i: (j + j0, i, 0))],
        out_specs=pl.BlockSpec((None, tr, cols), lambda j, i: (j, i, 0)),
        out_shape=jax.ShapeDtypeStruct((3 if wire else 1, r, cols), BF16 if wire else F32),
        compiler_params=_cparams(("arbitrary", "arbitrary")),
    )(g4, t)


def _config(x, ctx, w_in, w_dec_f, gla_norm_g, sg_ln_g, w_s):
    n, d = x.shape[1], x.shape[2]
    tc = ctx.shape[1]
    h = gla_norm_g.shape[1]
    dv = gla_norm_g.shape[2] * N_DEV
    dk = dv // 2
    kw, vw = h * dk, h * dv
    lr = w_dec_f.shape[1]
    sgw = sg_ln_g.shape[1]
    cfg = dict(N=n, D=d, TC=tc, H=h, DV=dv, DK=dk, KW=kw, VW=vw, LR=lr, SGW=sgw, SG_G=w_s.shape[1],
               SG_C=w_s.shape[2], IN=w_in.shape[2] * N_DEV)
    cfg.update(K0=kw, V0=2 * kw, R0=2 * kw + vw, L0=2 * kw + 2 * vw, ZA=2 * kw + 2 * vw + LANES)
    cfg.update(U0=0, VV0=sgw, ZB=2 * sgw)
    assert dk == LANES and vw == 2 * kw and 2 * lr <= LANES
    assert cfg["R0"] % vw == 0 and vw % sgw == 0
    assert cfg["IN"] == 2 * kw + 2 * vw + 2 * lr + 2 * sgw
    return cfg


def _rope_tables(cfg):
    n, tc, dk = cfg["N"], cfg["TC"], cfg["DK"]
    m = dk // 4
    pos = jnp.arange(n)
    inv = ROPE_BASE ** (-jnp.arange(m, dtype=F32) / m)
    ang_r = (pos // GRID_W).astype(F32)[:, None] * inv[None, :]
    ang_c = (pos % GRID_W).astype(F32)[:, None] * inv[None, :]
    cos = jnp.concatenate([jnp.cos(ang_r)] * 2 + [jnp.cos(ang_c)] * 2, axis=1)
    sin = jnp.concatenate([-jnp.sin(ang_r), jnp.sin(ang_r), -jnp.sin(ang_c), jnp.sin(ang_c)], axis=1)
    scale = dk ** -0.5
    z = jnp.zeros((tc, dk), F32)
    one = jnp.ones((tc, dk), F32)
    return [jnp.concatenate([cos * scale, z]), jnp.concatenate([sin * scale, z]),
            jnp.concatenate([cos, one]), jnp.concatenate([sin, z])]


def _pair_sums(g, t, nm):
    rows_for = _tile(g.shape[1], max(8, ELEMS_PER_BLOCK // g.shape[2]), 16)
    return (_pair_add(g, t, name="rs_own_" + nm, tr=rows_for, wire=False),
            _pair_add(g, t, name="rs_wire_" + nm, tr=rows_for, wire=True))


def _local_step(x, ctx, target, mods, c_mods, w, cfg):
    n, d, tc = cfg["N"], cfg["D"], cfg["TC"]
    kw, vw, sgw, za, zb, lr = cfg["KW"], cfg["VW"], cfg["SGW"], cfg["ZA"], cfg["ZB"], cfg["LR"]
    sh1, sc1, g1, sh2, sc2, g2 = mods
    csh1, csc1 = c_mods
    rt = n + tc
    tb = math.gcd(256, math.gcd(n, tc))
    tr = math.gcd(128, tb)
    tr_s = math.gcd(64, tb)
    fs = w["sh_1"].shape[1]
    ff = fs * N_DEV
    cs_in = w["sh_in"].shape[1]
    r8 = d // N_DEV

    rows_in = _run_sides(_Sides([_a2a_job(w["sh_in"].reshape(N_DEV, r8, cs_in))]), name="a2a_w_in")[0]
    rows_in = rows_in.transpose(1, 0, 2).reshape(r8, cfg["IN"])
    lf0 = 2 * kw + 2 * vw
    sg0 = lf0 + 2 * lr
    wa_rows = jnp.concatenate([rows_in[:, :sg0], jnp.zeros((r8, LANES - 2 * lr), BF16)], axis=1)
    wb_rows = rows_in[:, sg0:]
    w_a = _run_sides(_Sides([_ag_job([wa_rows])]), name="ag_w_in_a")[0].reshape(d, za)

    hx = _norm_mod(x, w["pre1_g"], sh1, sc1, name="in_norm_x", tr=tr)
    hc = _norm_mod(ctx, w["pre1_g"], csh1, csc1, name="in_norm_ctx", tr=tr)
    a_all = jnp.concatenate([hx, hc], axis=0)

    tm_a = _tile(rt, 1152, 16)
    tm_n = _tile(n, 1024, 16)
    sh_o_rows = w["sh_o"].shape[0]
    o_cut = (sh_o_rows * 3 // 4) // 16 * 16
    z, w_b, wg_o = _matmul(
        a_all, w_a, "nn", rt, za, d, tm=tm_a, tn=_tile(za, 1152, LANES), tk=_tile(d, TK, LANES), name="mm_in_a",
        out_shapes=[jax.ShapeDtypeStruct((rt, za), F32)],
        sides=_Sides([_ag_job([wb_rows]), _ag_job([w["sh_o"]], rows=(0, o_cut))]))
    w_b = w_b.reshape(d, zb)
    z_b, wg_o, w_1 = _matmul(
        a_all, w_b, "nn", n, zb, d, tm=tm_n, tn=_tile(zb, 1024, LANES), tk=_tile(d, TK, LANES), name="mm_in_b",
        out_shapes=[jax.ShapeDtypeStruct((n, zb), F32)],
        sides=_Sides([_ag_job([w["sh_o"]], rows=(o_cut, sh_o_rows - o_cut), chained=[wg_o]),
                      _ag_job([w["sh_1"]], rows=(0, d // 4))]))
    w_o = wg_o.reshape(d, d)

    tabs = _rope_tables(cfg)
    qr, kr, la_f, la_b = _gla_prep(z, tabs, w["wdf_pad"], w["wdb_pad"], w["b_dec_f"], w["b_dec_b"], cfg, tr=tr)

    zero_st = jnp.zeros((cfg["H"], cfg["DV"], cfg["DK"]), F32)
    gla = dict(cfg=cfg, tb=tb)
    _, save_cf, st_cf = _gla_fwd(qr, kr, z, la_f, zero_st, rev=False, row_off=n, nrows=tc, name="gla_ctx_f", **gla)
    _, save_cb, st_cb = _gla_fwd(qr, kr, z, la_b, zero_st, rev=True, row_off=n, nrows=tc, name="gla_ctx_b", **gla)
    o_f, save_f, _, w_1 = _gla_fwd(
        qr, kr, z, la_f, st_cf, rev=False, row_off=0, nrows=n, name="gla_f",
        sides=_Sides([_ag_job([w["sh_1"]], rows=(d // 4, d // 4), chained=[w_1])]), **gla)
    o_b, save_b, _, w_1 = _gla_fwd(
        qr, kr, z, la_b, st_cb, rev=True, row_off=0, nrows=n, name="gla_b",
        sides=_Sides([_ag_job([w["sh_1"]], rows=(d // 2, d // 4), chained=[w_1])]), **gla)
    y_gla = _readout_fwd(o_f, o_b, z, w["gla_g"], cfg, tr=tr)
    y_sg = _sg_fwd(z_b, n, w["sg_ln_g"], w["sg_ln_b"], w["w_s"], w["bs_full"], cfg)
    ycat = jnp.concatenate([y_gla, y_sg], axis=1)

    mix, w_1 = _matmul(ycat, w_o, "nn", n, d, d, tm=tm_n, tn=_tile(d, 1024, LANES), tk=_tile(d, TK, LANES),
                       name="mm_o", out_shapes=[jax.ShapeDtypeStruct((n, d), F32)],
                       sides=_Sides([_ag_job([w["sh_1"]], rows=(3 * d // 4, d // 4), chained=[w_1])]))
    w2_cut = (fs // 8) // 16 * 16
    x1, h2, wg_2 = _mid_fwd(x, mix, g1, w["post1_g"], w["pre2_g"], sh2, sc2, tr=tr_s,
                            sides=_Sides([_ag_job([w["sh_2"]], rows=(0, w2_cut))]))

    tn_f = _tile(fs, 1024, LANES)
    tk_d = _tile(d, TK, LANES)

    def relu2(acc):
        return acc, jnp.square(jnp.maximum(acc, 0.0))

    a1, p1, wg_2 = _matmul(h2, w_1, "nn", n, ff, d, tm=tm_n, tn=tn_f, tk=tk_d, name="mm_1",
                           b_spec=_blocked_b_nn(fs, tk_d, tn_f), epilogue=relu2,
                           out_shapes=[jax.ShapeDtypeStruct((n, ff), BF16)] * 2,
                           sides=_Sides([_ag_job([w["sh_2"]], rows=(w2_cut, fs - w2_cut), chained=[wg_2])]))
    w_2 = wg_2.reshape(ff, d)
    tk_f = _tile(ff, TK, LANES)
    m2 = _matmul(p1, w_2, "nn", n, d, ff, tm=tm_n, tn=_tile(d, 1024, LANES), tk=tk_f, name="mm_2",
                 out_shapes=[jax.ShapeDtypeStruct((n, d), F32)])[0]

    dx2, dm2, dg2, dpost2, lossc = _head_bwd(x1, m2, target, g2, w["post2_g"], tr=tr_s)

    def drelu2(acc, a):
        return (acc * (2.0 * jnp.maximum(a.astype(F32), 0.0)),)

    da1 = _matmul(dm2, w_2, "nt", n, ff, d, tm=tm_n, tn=_tile(ff, 1024, LANES), tk=tk_d, name="mm_2_dx",
                  epilogue=drelu2, extras=(a1,), out_shapes=[jax.ShapeDtypeStruct((n, ff), BF16)])[0]
    tk_n = _tile(n, TK, 16)
    tm_d = _tile(d, 1024, LANES)
    g_1 = _matmul(h2, da1, "tn", d, ff, n, tm=tm_d, tn=tn_f, tk=tk_n, name="mm_1_dw",
                  out_specs=[_blocked_out(fs, tm_d, tn_f)],
                  out_shapes=[jax.ShapeDtypeStruct((N_DEV, d, fs), F32)])[0]
    dw_2, t_1 = _matmul(p1, dm2, "tn", ff, d, n, tm=_tile(ff, 1024, LANES), tn=_tile(d, 1024, LANES), tk=tk_n,
                        name="mm_2_dw", out_shapes=[jax.ShapeDtypeStruct((ff, d), F32)],
                        sides=_Sides([_pair_job([g_1])]))
    g_2 = dw_2.reshape(N_DEV, fs, d)
    p1_own, p1_wire = _pair_sums(g_1, t_1, "w_1")
    tk_fs = _tile(fs, TK, LANES)
    dh2, u_1, t_2 = _matmul(da1, w_1, "nt", n, d, ff, tm=tm_n, tn=_tile(d, 1024, LANES), tk=tk_fs, name="mm_1_dx",
                            b_spec=_blocked_b_nt(fs, _tile(d, 1024, LANES), tk_fs),
                            out_shapes=[jax.ShapeDtypeStruct((n, d), F32)],
                            sides=_Sides([_chip_job([p1_wire], rows=(0, d * 13 // 16)), _pair_job([g_2])]))
    p2_own, p2_wire = _pair_sums(g_2, t_2, "w_2")
    c2 = [0] + [(fs * f // 64) // 16 * 16 for f in (20, 35, 50)] + [fs]
    piece2 = lambda i: (c2[i], c2[i + 1] - c2[i])

    dx1, dmix, dsh2, dsc2, dpre2, dg1, dpost1, u_1 = _mid_bwd(
        dh2, x1, dx2, mix, sc2, w["pre2_g"], g1, w["post1_g"], tr=tr_s,
        sides=_Sides([_chip_job([p1_wire], rows=(d * 13 // 16, d - d * 13 // 16), chained=[u_1])]))
    dw_o, u_2 = _matmul(ycat, dmix, "tn", d, d, n, tm=tm_d, tn=_tile(d, 1024, LANES), tk=tk_n, name="mm_o_dw",
                        out_shapes=[jax.ShapeDtypeStruct((d, d), F32)],
                        sides=_Sides([_chip_job([p2_wire], rows=piece2(0))]))
    g_o = dw_o.reshape(N_DEV, r8, d)
    dycat, t_o, u_2 = _matmul(dmix, w_o, "nt", n, d, d, tm=tm_n, tn=_tile(d, 1024, LANES), tk=tk_d, name="mm_o_dx",
                              out_shapes=[jax.ShapeDtypeStruct((n, d), F32)],
                              sides=_Sides([_pair_job([g_o]), _chip_job([p2_wire], rows=piece2(1), chained=[u_2])]))
    po_own, po_wire = _pair_sums(g_o, t_o, "w_o")

    dz_b, dws, dbs_acc, dlng, dlnb = _sg_bwd(z_b, dycat, n, w["sg_ln_g"], w["sg_ln_b"], w["w_s"], w["bs_full"], cfg)
    dw_b, u_2 = _matmul(a_all, dz_b, "tn", d, zb, n, tm=tm_d, tn=_tile(zb, 1024, LANES), tk=tk_n, name="mm_in_dw_b",
                        out_shapes=[jax.ShapeDtypeStruct((d, zb), F32)],
                        sides=_Sides([_chip_job([p2_wire], rows=piece2(2), chained=[u_2])]))
    g_b = dw_b.reshape(N_DEV, r8, zb)
    do, dzr, dgla_g = _readout_bwd(o_f, o_b, z, dycat, w["gla_g"], cfg, tr=tr)

    *gf, t_b, u_2 = _gla_bwd(
        qr, kr, z, la_f, do, save_f, zero_st, rev=False, row_off=0, nrows=n, name="gla_f_bwd",
        sides=_Sides([_pair_job([g_b]), _chip_job([p2_wire], rows=piece2(3), chained=[u_2])]), **gla)
    pb_own, pb_wire = _pair_sums(g_b, t_b, "w_in_b")
    *gb, u_o = _gla_bwd(qr, kr, z, la_b, do, save_b, zero_st, rev=True, row_off=0, nrows=n, name="gla_b_bwd",
                        sides=_Sides([_chip_job([po_wire])]), **gla)
    do_c = jnp.zeros((tc, vw), BF16)
    gcf = _gla_bwd(qr, kr, z, la_f, do_c, save_cf, gf[4], rev=False, row_off=n, nrows=tc, name="gla_ctx_f_bwd",
                   **gla)
    gcb = _gla_bwd(qr, kr, z, la_b, do_c, save_cb, gb[4], rev=True, row_off=n, nrows=tc, name="gla_ctx_b_bwd",
                   **gla)

    post = dict(la_f=la_f, la_b=la_b, z=z, tabs=tabs, wdf_pad=w["wdf_pad"], wdb_pad=w["wdb_pad"], cfg=cfg, tr=tr)
    dzq, dzk, dzv, dzl, dwdf, dwdb, dbdf, dbdb = _gla_post(gf, gb, row_off=0, nrows=n, name="gla_post", **post)
    czq, czk, czv, czl, cwdf, cwdb, cbdf, cbdb = _gla_post(gcf, gcb, row_off=n, nrows=tc, name="gla_post_ctx",
                                                           **post)
    dz_a = jnp.concatenate([
        jnp.concatenate([dzq, dzk, dzv, dzr, dzl], axis=1),
        jnp.concatenate([czq, czk, czv, jnp.zeros((tc, vw), BF16), czl], axis=1)], axis=0)

    dw_a, u_b = _matmul(a_all, dz_a, "tn", d, za, rt, tm=tm_d, tn=_tile(za, 1152, LANES), tk=_tile(rt, 2176, 16),
                        name="mm_in_dw_a", out_shapes=[jax.ShapeDtypeStruct((d, za), F32)],
                        sides=_Sides([_chip_job([pb_wire])]))
    g_a = dw_a.reshape(N_DEV, r8, za)
    da_a, t_a = _matmul(dz_a, w_a, "nt", rt, d, za, tm=_tile(rt, 576, 16), tn=_tile(d, 1024, LANES),
                        tk=_tile(za, 3456, LANES), name="mm_in_dx_a", out_shapes=[jax.ShapeDtypeStruct((rt, d), F32)],
                        sides=_Sides([_pair_job([g_a])]))
    pa_own, pa_wire = _pair_sums(g_a, t_a, "w_in_a")
    cut_a = (r8 * 9 // 16) // 16 * 16
    tm_x = _tile(n, 512, 16)
    da_x, u_a = _matmul(dz_b, w_b, "nt", n, d, zb, tm=tm_x, tn=_tile(d, 1024, LANES), tk=_tile(zb, 4096, LANES),
                        name="mm_in_dx_b", epilogue=lambda acc, prev: (acc + prev,), extras=(da_a,),
                        out_shapes=[jax.ShapeDtypeStruct((n, d), F32)],
                        sides=_Sides([_chip_job([pa_wire], rows=(0, cut_a))]))

    grad_x, dsh1, dsc1, dpre1, u_a = _in_bwd(
        da_x, x, dx1, sc1, w["pre1_g"], row_off=0, tr=tr_s, name="in_bwd_x",
        sides=_Sides([_chip_job([pa_wire], rows=(cut_a, r8 - cut_a), chained=[u_a])]))
    dcsh1, dcsc1, dpre1_c = _in_bwd(da_a, ctx, None, csc1, w["pre1_g"], row_off=n, tr=tr_s, name="in_bwd_ctx")

    small = dict(
        pre1_g=dpre1 + dpre1_c, post1_g=dpost1, pre2_g=dpre2, post2_g=dpost2,
        w_dec_f=(dwdf + cwdf)[:lr], w_dec_b=(dwdb + cwdb)[lr:2 * lr], b_dec_f=dbdf + cbdf, b_dec_b=dbdb + cbdb,
        gla_norm_g=dgla_g, sg_ln_g=dlng, sg_ln_b=dlnb, w_s=dws,
        b_s=dbs_acc.reshape(cfg["SG_C"], cfg["SG_G"], sgw // cfg["SG_G"]).sum(-1).T)
    dmod = jnp.concatenate([dsh1, dsc1, dg1, dsh2, dsc2, dg2], axis=1)
    dmod_c = jnp.concatenate([dcsh1, dcsc1], axis=1)
    big = dict(w_in_a=(pa_own, u_a), w_in_b=(pb_own, u_b), w_o=(po_own, u_o), w_1=(p1_own, u_1), w_2=(p2_own, u_2))
    return lossc, grad_x, big, small, dmod, dmod_c


SMALL_NAMES = ["b_ada", "pre1_g", "post1_g", "pre2_g", "post2_g", "w_dec_f", "b_dec_f", "w_dec_b", "b_dec_b",
               "gla_norm_g", "sg_ln_g", "sg_ln_b", "w_s", "b_s", "c_ctx"]
WEIGHT_ORDER = ["c_ctx", "w_ada", "b_ada", "pre1_g", "post1_g", "pre2_g", "post2_g", "w_in", "w_dec_f", "b_dec_f",
                "w_dec_b", "b_dec_b", "gla_norm_g", "sg_ln_g", "sg_ln_b", "w_s", "b_s", "w_o", "w_1", "w_2"]


def kernel(x, c, ctx, c_ctx, w_ada, b_ada, pre1_g, post1_g, pre2_g, post2_g, w_in, w_dec_f, b_dec_f, w_dec_b, b_dec_b, gla_norm_g, sg_ln_g, sg_ln_b, w_s, b_s, w_o, w_1, w_2, loss_target, m_c_ctx, m_w_ada, m_b_ada, m_pre1_g, m_post1_g, m_pre2_g, m_post2_g, m_w_in, m_w_dec_f, m_b_dec_f, m_w_dec_b, m_b_dec_b, m_gla_norm_g, m_sg_ln_g, m_sg_ln_b, m_w_s, m_b_s, m_w_o, m_w_1, m_w_2, v_c_ctx, v_w_ada, v_b_ada, v_pre1_g, v_post1_g, v_pre2_g, v_post2_g, v_w_in, v_w_dec_f, v_b_dec_f, v_w_dec_b, v_b_dec_b, v_gla_norm_g, v_sg_ln_g, v_sg_ln_b, v_w_s, v_b_s, v_w_o, v_w_1, v_w_2):
    weights = dict(c_ctx=c_ctx, w_ada=w_ada, b_ada=b_ada, pre1_g=pre1_g, post1_g=post1_g, pre2_g=pre2_g,
                   post2_g=post2_g, w_in=w_in, w_dec_f=w_dec_f, b_dec_f=b_dec_f, w_dec_b=w_dec_b, b_dec_b=b_dec_b,
                   gla_norm_g=gla_norm_g, sg_ln_g=sg_ln_g, sg_ln_b=sg_ln_b, w_s=w_s, b_s=b_s, w_o=w_o, w_1=w_1,
                   w_2=w_2)
    mom_m = dict(c_ctx=m_c_ctx, w_ada=m_w_ada, b_ada=m_b_ada, pre1_g=m_pre1_g, post1_g=m_post1_g, pre2_g=m_pre2_g,
                 post2_g=m_post2_g, w_in=m_w_in, w_dec_f=m_w_dec_f, b_dec_f=m_b_dec_f, w_dec_b=m_w_dec_b,
                 b_dec_b=m_b_dec_b, gla_norm_g=m_gla_norm_g, sg_ln_g=m_sg_ln_g, sg_ln_b=m_sg_ln_b, w_s=m_w_s,
                 b_s=m_b_s, w_o=m_w_o, w_1=m_w_1, w_2=m_w_2)
    mom_v = dict(c_ctx=v_c_ctx, w_ada=v_w_ada, b_ada=v_b_ada, pre1_g=v_pre1_g, post1_g=v_post1_g, pre2_g=v_pre2_g,
                 post2_g=v_post2_g, w_in=v_w_in, w_dec_f=v_w_dec_f, b_dec_f=v_b_dec_f, w_dec_b=v_w_dec_b,
                 b_dec_b=v_b_dec_b, gla_norm_g=v_gla_norm_g, sg_ln_g=v_sg_ln_g, sg_ln_b=v_sg_ln_b, w_s=v_w_s,
                 b_s=v_b_s, w_o=v_w_o, w_1=v_w_1, w_2=v_w_2)

    cfg = _config(x, ctx, w_in, w_dec_f, gla_norm_g, sg_ln_g, w_s)
    n, d, h, dv, kw, vw, lr, sgw = (cfg[k] for k in ("N", "D", "H", "DV", "KW", "VW", "LR", "SGW"))
    dvs, kws = dv // N_DEV, kw // N_DEV
    ix, iy, ic = _my_pos()
    me = 4 * ix + 2 * iy + ic

    pack1 = jnp.concatenate([c.reshape(1, d), w_dec_f.reshape(1, lr * kws), w_dec_b.reshape(1, lr * kws),
                             gla_norm_g.reshape(1, h * dvs)], axis=1)
    g1 = _all_gather_vec(pack1, name="ag_small_in")
    c_all = g1[:, :d]
    o1 = d
    wdf = g1[:, o1:o1 + lr * kws].reshape(N_DEV, lr, kws).transpose(1, 0, 2).reshape(lr, kw)
    o1 += lr * kws
    wdb = g1[:, o1:o1 + lr * kws].reshape(N_DEV, lr, kws).transpose(1, 0, 2).reshape(lr, kw)
    o1 += lr * kws
    gla_g = g1[:, o1:o1 + h * dvs].reshape(N_DEV, h, dvs).transpose(1, 0, 2).reshape(1, h * dv)

    c16 = jnp.concatenate([c_all, jnp.broadcast_to(c_ctx.reshape(1, d), (N_DEV, d))], axis=0)
    ncol = w_ada.shape[2]
    wa = w_ada.reshape(d, ncol)
    b_mine = lax.dynamic_slice(b_ada, (0, me * ncol), (1, ncol))
    tn_ada = _tile(ncol, 512, LANES)
    mod_mine = _ada_fwd(c16, wa, b_mine, tn=tn_ada)
    mod_all = _all_gather_small(mod_mine, name="ag_mod").transpose(1, 0, 2).reshape(16, N_DEV * ncol)
    mod_b = lax.dynamic_slice(mod_all, (me, 0), (1, 6 * d))
    mods = [mod_b[:, i * d:(i + 1) * d] for i in range(6)]
    c_mods = [mod_all[N_DEV:N_DEV + 1, :d], mod_all[N_DEV:N_DEV + 1, d:2 * d]]

    zpad = lambda r: jnp.zeros((r, kw), F32)
    w = dict(
        sh_in=w_in.reshape(d, w_in.shape[2]).astype(BF16), sh_o=w_o.reshape(w_o.shape[1], d).astype(BF16),
        sh_1=w_1.reshape(d, w_1.shape[2]).astype(BF16), sh_2=w_2.reshape(w_2.shape[1], d).astype(BF16),
        pre1_g=pre1_g, post1_g=post1_g, pre2_g=pre2_g, post2_g=post2_g, b_dec_f=b_dec_f, b_dec_b=b_dec_b,
        wdf_pad=jnp.concatenate([wdf, zpad(LANES - lr)], axis=0),
        wdb_pad=jnp.concatenate([zpad(lr), wdb, zpad(LANES - 2 * lr)], axis=0),
        gla_g=gla_g, sg_ln_g=sg_ln_g, sg_ln_b=sg_ln_b, w_s=w_s[0],
        bs_full=jnp.repeat(b_s[0].T, sgw // cfg["SG_G"], axis=1))

    lossc, grad_x, big, small, dmod, dmod_c = _local_step(x[0], ctx[0], loss_target[0], mods, c_mods, w, cfg)
    loss = lax.psum(jnp.sum(lossc), AXES)

    order3 = ["pre1_g", "post1_g", "pre2_g", "post2_g", "w_dec_f", "b_dec_f", "w_dec_b", "b_dec_b", "gla_norm_g",
              "sg_ln_g", "sg_ln_b", "w_s", "b_s"]
    pieces = [dmod, dmod_c] + [small[k].reshape(1, -1) for k in order3]
    sizes = [p.shape[1] for p in pieces]
    g3 = _all_gather_vec(jnp.concatenate(pieces, axis=1), name="ag_small_grads")
    offs = [0]
    for s in sizes:
        offs.append(offs[-1] + s)
    dmod_all = g3[:, :6 * d]
    dmod_c_all = jnp.pad(g3[:, offs[1]:offs[2]], ((0, 0), (0, 4 * d)))
    parts8 = {k: g3[:, offs[2 + i]:offs[3 + i]] for i, k in enumerate(order3)}
    parts8["b_ada"] = dmod_all + dmod_c_all
    parts8["w_dec_f"] = lax.dynamic_slice(parts8["w_dec_f"].reshape(N_DEV, lr, kw), (0, 0, me * kws),
                                          (N_DEV, lr, kws)).reshape(N_DEV, -1)
    parts8["w_dec_b"] = lax.dynamic_slice(parts8["w_dec_b"].reshape(N_DEV, lr, kw), (0, 0, me * kws),
                                          (N_DEV, lr, kws)).reshape(N_DEV, -1)
    parts8["gla_norm_g"] = lax.dynamic_slice(parts8["gla_norm_g"].reshape(N_DEV, h, dv), (0, 0, me * dvs),
                                             (N_DEV, h, dvs)).reshape(N_DEV, -1)

    dm16 = jnp.concatenate([dmod_all, dmod_c_all], axis=0)
    dm_mine = lax.dynamic_slice(dm16, (0, me * ncol), (16, ncol))
    g_w_ada, dcc = _ada_bwd(c16, dm_mine, wa, c_ctx.reshape(1, d), tn=tn_ada)
    parts8["c_ctx"] = _all_gather_vec(dcc, name="ag_cctx")

    flat = lambda t: t.reshape(1, -1)
    g8 = _dense(jnp.concatenate([parts8[k] for k in SMALL_NAMES], axis=1))
    ws, ms, vs = [_dense(jnp.concatenate([flat(src[k]) for k in SMALL_NAMES], axis=1))[0]
                  for src in (weights, mom_m, mom_v)]
    res_small = [r.reshape(1, -1) for r in _adam_small(g8, ws, ms, vs)]
    out = {}
    off = 0
    for k in SMALL_NAMES:
        sz = weights[k].size
        out[k] = [r[:, off:off + sz].reshape(weights[k].shape) for r in res_small]
        off += sz

    rows_for = lambda r, cols: _tile(r, max(8, ELEMS_PER_BLOCK // cols), 16)
    r8, cs_in = d // N_DEV, w_in.shape[2]

    def adam(nm, parts, sides=_NO_SIDES):
        shp = weights[nm].shape
        r2 = (shp[1], shp[2])
        res = _adam_big(parts, weights[nm].reshape(r2), mom_m[nm].reshape(r2), mom_v[nm].reshape(r2),
                        name="adam_" + nm, tr=rows_for(*r2), sides=sides)
        out[nm] = [r.reshape(shp) for r in res[:4]]
        return res[4:]

    four = lambda own, u: [(own, 0), (u, 0), (u, 1), (u, 2)]
    red_a = _sum_parts(four(*big["w_in_a"]), r8, cfg["ZA"], name="rs_sum_w_in_a", tr=rows_for(r8, cfg["ZA"]))
    red_b = _sum_parts(four(*big["w_in_b"]), r8, cfg["ZB"], name="rs_sum_w_in_b", tr=rows_for(r8, cfg["ZB"]))
    red = jnp.concatenate([red_a[:, :cfg["L0"] + 2 * lr], red_b], axis=1)
    (g_in,) = adam("w_ada", [(g_w_ada, None)],
                   _Sides([_a2a_job(red.reshape(r8, N_DEV, cs_in).transpose(1, 0, 2))]))
    for nm in ("w_2", "w_1", "w_o"):
        adam(nm, four(*big[nm]))
    adam("w_in", [(g_in.reshape(d, cs_in), None)])

    outs = [loss, grad_x[None]]
    for i in range(4):
        outs += [out[k][i] for k in WEIGHT_ORDER]
    return tuple(outs)
```

```python
import math

import jax
import jax.numpy as jnp
from jax import lax
from jax.experimental import pallas as pl
from jax.experimental.pallas import tpu as pltpu

F32 = jnp.float32
BF16 = jnp.bfloat16
MXU_DTYPE = jnp.bfloat16
HI = lax.Precision.HIGHEST

N_DEV = 8
AXES = ("x", "y", "c")
MESH = pl.DeviceIdType.MESH
LANES = 128
VMEM_LIMIT = 56 * 1024 * 1024

EPS = 1e-6
GRID_W = 64
GLA_CHUNK = 64
GLA_TAU = 16.0
ROPE_BASE = 10000.0
ADAM_LR = 0.001
ADAM_B1 = 0.9
ADAM_B2 = 0.999
ADAM_EPS = 1e-08
ADAM_WD = 0.01
ADAM_STEP = 10


def _cparams(sem):
    return pltpu.CompilerParams(dimension_semantics=sem, vmem_limit_bytes=VMEM_LIMIT)


def _tile(n, target, align):
    if n <= target:
        return n
    best = None
    for t in range(align, target + 1, align):
        if n % t == 0:
            best = t
    assert best is not None, (n, target, align)
    return best


def _dg(a, b, dims, prec=None):
    return lax.dot_general(a, b, (dims, ((), ())), precision=prec, preferred_element_type=F32)


def _nn(a, b):
    return _dg(a.astype(MXU_DTYPE), b.astype(MXU_DTYPE), ((1,), (0,)))


def _nt(a, b):
    return _dg(a.astype(MXU_DTYPE), b.astype(MXU_DTYPE), ((1,), (1,)))


def _tn(a, b):
    return _dg(a.astype(MXU_DTYPE), b.astype(MXU_DTYPE), ((0,), (0,)))


def _sigmoid(x):
    return 1.0 / (1.0 + jnp.exp(-x))


def _silu(x):
    return x * _sigmoid(x)


def _dsilu(x):
    s = _sigmoid(x)
    return s * (1.0 + x * (1.0 - s))


def _gelu(x):
    return 0.5 * x * (1.0 + lax.erf(x * (1.0 / math.sqrt(2.0))))


def _dgelu(x):
    return 0.5 * (1.0 + lax.erf(x * (1.0 / math.sqrt(2.0)))) + x * jnp.exp(-0.5 * x * x) * (1.0 / math.sqrt(2.0 * math.pi))


def _rstd(x):
    return lax.rsqrt(jnp.mean(x * x, axis=-1, keepdims=True) + EPS)


def _rms_bwd(x, r, dn):
    return r * dn - x * (r * r * r) * jnp.mean(dn * x, axis=-1, keepdims=True)


def _colsum(x):
    return jnp.sum(x, axis=0, keepdims=True)


class _Sides:
    def __init__(self, jobs):
        self.jobs = list(jobs)
        self.ins = [a for j in self.jobs for a in j["ins"]]
        self.outs = [o for j in self.jobs for o in j["outs"]]
        self.sems = [s for j in self.jobs for s in j["sems"]]

    def aliases(self, in_base, out_base):
        res, oi, oo = {}, 0, 0
        for j in self.jobs:
            for a, b in j.get("alias", {}).items():
                res[in_base + oi + a] = out_base + oo + b
            oi += len(j["ins"])
            oo += len(j["outs"])
        return res

    def run(self, phase, in_refs, out_refs, sem_refs):
        oi = oo = os_ = 0
        for j in self.jobs:
            ni, no, ns = len(j["ins"]), len(j["outs"]), len(j["sems"])
            j[phase](in_refs[oi:oi + ni], out_refs[oo:oo + no], sem_refs[os_:os_ + ns])
            oi, oo, os_ = oi + ni, oo + no, os_ + ns


_NO_SIDES = _Sides([])
ANY_SPEC = pl.BlockSpec(memory_space=pl.ANY)


def _matmul(a, b, mode, m, n, k, *, tm, tn, tk, name, out_shapes, b_spec=None, out_specs=None,
            epilogue=None, extras=(), sides=_NO_SIDES):
    nk = k // tk
    assert m % tm == 0 and n % tn == 0 and k % tk == 0, (name, m, n, k, tm, tn, tk)
    dot = {"nn": _nn, "nt": _nt, "tn": _tn}[mode]
    if mode == "tn":
        a_spec = pl.BlockSpec((tk, tm), lambda i, j, kk: (kk, i))
    else:
        a_spec = pl.BlockSpec((tm, tk), lambda i, j, kk: (i, kk))
    if b_spec is None:
        if mode == "nt":
            b_spec = pl.BlockSpec((tn, tk), lambda i, j, kk: (j, kk))
        else:
            b_spec = pl.BlockSpec((tk, tn), lambda i, j, kk: (kk, j))
    mn_spec = pl.BlockSpec((tm, tn), lambda i, j, kk: (i, j))
    if out_specs is None:
        out_specs = [mn_spec] * len(out_shapes)
    n_extra = len(extras)
    n_out = len(out_shapes)
    n_si, n_so = len(sides.ins), len(sides.outs)
    ni, nj = m // tm, n // tn

    def body(a_ref, b_ref, *rest):
        extra_refs = rest[:n_extra]
        rest = rest[n_extra:]
        side_in, rest = rest[:n_si], rest[n_si:]
        out_refs, rest = rest[:n_out], rest[n_out:]
        side_out, rest = rest[:n_so], rest[n_so:]
        acc, side_sems = rest[0], rest[1:]
        i, j, kk = pl.program_id(0), pl.program_id(1), pl.program_id(2)

        if sides.jobs:
            @pl.when((i == 0) & (j == 0) & (kk == 0))
            def _():
                sides.run("start", side_in, side_out, side_sems)

        @pl.when(kk == 0)
        def _():
            acc[...] = jnp.zeros_like(acc)

        acc[...] += dot(a_ref[...], b_ref[...])

        @pl.when(kk == nk - 1)
        def _():
            vals = (acc[...],) if epilogue is None else epilogue(acc[...], *[e[...] for e in extra_refs])
            for o, v in zip(out_refs, vals):
                o[...] = v.astype(o.dtype)

        if sides.jobs:
            @pl.when((i == ni - 1) & (j == nj - 1) & (kk == nk - 1))
            def _():
                sides.run("finish", side_in, side_out, side_sems)

    sem = ("arbitrary",) * 3 if sides.jobs else ("parallel", "parallel", "arbitrary")
    res = pl.pallas_call(
        body, name=name, grid=(ni, nj, nk),
        in_specs=[a_spec, b_spec] + [mn_spec] * n_extra + [ANY_SPEC] * n_si,
        out_specs=list(out_specs) + [ANY_SPEC] * n_so, out_shape=list(out_shapes) + list(sides.outs),
        scratch_shapes=[pltpu.VMEM((tm, tn), F32)] + list(sides.sems),
        input_output_aliases=sides.aliases(2 + n_extra, n_out),
        compiler_params=_cparams(sem),
    )(a, b, *extras, *sides.ins)
    return res


def _blocked_b_nn(ns, tk, tn):
    assert ns % tn == 0
    return pl.BlockSpec((None, tk, tn), lambda i, j, kk: ((j * tn) // ns, kk, ((j * tn) % ns) // tn))


def _blocked_b_nt(ks, tn, tk):
    assert ks % tk == 0
    return pl.BlockSpec((None, tn, tk), lambda i, j, kk: ((kk * tk) // ks, j, ((kk * tk) % ks) // tk))


def _blocked_out(ns, tm, tn):
    assert ns % tn == 0
    return pl.BlockSpec((None, tm, tn), lambda i, j, kk: ((j * tn) // ns, i, ((j * tn) % ns) // tn))


def _rows_call(body, *, name, nblk, tr, row_ins, consts, row_outs, accs=(), sides=_NO_SIDES):
    n_ri, n_c, n_ro, n_acc = len(row_ins), len(consts), len(row_outs), len(accs)
    n_si, n_so = len(sides.ins), len(sides.outs)

    def kern(*refs):
        i = pl.program_id(0)
        rin, refs = refs[:n_ri], refs[n_ri:]
        cin, refs = refs[:n_c], refs[n_c:]
        side_in, refs = refs[:n_si], refs[n_si:]
        rout, refs = refs[:n_ro], refs[n_ro:]
        acc, refs = refs[:n_acc], refs[n_acc:]
        side_out, side_sems = refs[:n_so], refs[n_so:]

        if sides.jobs:
            @pl.when(i == 0)
            def _():
                sides.run("start", side_in, side_out, side_sems)

        if n_acc:
            @pl.when(i == 0)
            def _():
                for r in acc:
                    r[...] = jnp.zeros_like(r)

        body(rin, cin, rout, acc)

        if sides.jobs:
            @pl.when(i == nblk - 1)
            def _():
                sides.run("finish", side_in, side_out, side_sems)

    in_specs = [pl.BlockSpec((tr, w), lambda i, ro=ro, co=co: (i + ro, co)) for (_, w, ro, co) in row_ins]
    in_specs += [pl.BlockSpec(cst.shape, lambda i, nd=cst.ndim: (0,) * nd) for cst in consts]
    out_specs = [pl.BlockSpec((tr, w), lambda i: (i, 0)) for (_, w, _) in row_outs]
    out_specs += [pl.BlockSpec(s, lambda i, nd=len(s): (0,) * nd) for s in accs]
    out_shape = [jax.ShapeDtypeStruct((r, w), dt) for (r, w, dt) in row_outs]
    out_shape += [jax.ShapeDtypeStruct(s, F32) for s in accs]
    return pl.pallas_call(
        kern, name=name, grid=(nblk,), in_specs=in_specs + [ANY_SPEC] * n_si,
        out_specs=out_specs + [ANY_SPEC] * n_so, out_shape=out_shape + list(sides.outs),
        scratch_shapes=list(sides.sems),
        input_output_aliases=sides.aliases(n_ri + n_c, n_ro + n_acc),
        compiler_params=_cparams(("arbitrary",)),
    )(*[r[0] for r in row_ins], *consts, *sides.ins)


def _norm_mod(x, g, shift, scale, *, name, tr):
    rows, d = x.shape

    def body(rin, cin, rout, acc):
        xv = rin[0][...]
        n = xv * _rstd(xv) * cin[0][...]
        rout[0][...] = (n * (1.0 + cin[2][...]) + cin[1][...]).astype(BF16)

    return _rows_call(body, name=name, nblk=rows // tr, tr=tr, row_ins=[(x, d, 0, 0)],
                      consts=[g, shift, scale], row_outs=[(rows, d, BF16)])[0]


def _swap_halves(t, width):
    lane = lax.broadcasted_iota(jnp.int32, t.shape, 1)
    return jnp.where(lane % 64 < 32, pltpu.roll(t, width - 32, 1), pltpu.roll(t, 32, 1))


def _gla_prep(z, tabs, wdf_pad, wdb_pad, bdf, bdb, cfg, *, tr):
    rows = z.shape[0]
    kw, h = cfg["KW"], cfg["H"]

    def body(rin, cin, rout, acc):
        zq, zk, zl = rin[0][...], rin[1][...], rin[2][...]
        cq, sq, ck, sk = [jnp.concatenate([rin[3 + t][...]] * h, axis=1) for t in range(4)]
        rout[0][...] = zq * cq + _swap_halves(zq, kw) * sq
        rout[1][...] = zk * ck + _swap_halves(zk, kw) * sk
        for o, w, b in ((2, cin[0], cin[2]), (3, cin[1], cin[3])):
            a = _nn(zl, w[...]) + b[...]
            rout[o][...] = (jnp.minimum(a, 0.0) - jnp.log(1.0 + jnp.exp(-jnp.abs(a)))) * (1.0 / GLA_TAU)

    row_ins = [(z, kw, 0, 0), (z, kw, 0, 1), (z, LANES, 0, cfg["L0"] // LANES)]
    row_ins += [(t, LANES, 0, 0) for t in tabs]
    return _rows_call(body, name="gla_prep", nblk=rows // tr, tr=tr, row_ins=row_ins,
                      consts=[wdf_pad, wdb_pad, bdf, bdb], row_outs=[(rows, kw, F32)] * 4)


def _chunk_consts(rev):
    c = GLA_CHUNK
    r = lax.broadcasted_iota(jnp.int32, (c, c), 0)
    cc = lax.broadcasted_iota(jnp.int32, (c, c), 1)
    keep = (cc >= r) if rev else (cc <= r)
    return keep, keep.astype(F32)


def _heads_per_step(cfg):
    hb = 4 if cfg["H"] % 4 == 0 else (2 if cfg["H"] % 2 == 0 else 1)
    assert cfg["V0"] % (hb * cfg["DV"]) == 0
    return hb


def _chunk_decay(la, keep_f):
    b = _dg(keep_f, la, ((1,), (0,)), HI)
    return b, _colsum(la)


def _gla_fwd(qr, kr, z, la, st0, cfg, *, rev, row_off, nrows, tb, name, sides=_NO_SIDES):
    h, dk, dv = cfg["H"], cfg["DK"], cfg["DV"]
    c = GLA_CHUNK
    nsub = tb // c
    nblk = nrows // tb
    roff = row_off // tb
    hb = _heads_per_step(cfg)
    v_cb = cfg["V0"] // (hb * dv)
    n_si, n_so = len(sides.ins), len(sides.outs)

    def blk(j):
        return (nblk - 1 - j) if rev else j

    def body(q_ref, k_ref, v_ref, la_ref, st0_ref, *rest):
        side_in, rest = rest[:n_si], rest[n_si:]
        o_ref, save_ref, fin_ref = rest[:3]
        side_out, st, side_sems = rest[3:3 + n_so], rest[3 + n_so], rest[4 + n_so:]
        hh, j = pl.program_id(0), pl.program_id(1)

        if sides.jobs:
            @pl.when((hh == 0) & (j == 0))
            def _():
                sides.run("start", side_in, side_out, side_sems)

        @pl.when(j == 0)
        def _():
            st[...] = st0_ref[...]

        keep, keep_f = _chunk_consts(rev)
        order = range(nsub - 1, -1, -1) if rev else range(nsub)
        heads = range(hb)
        ksl = [slice(g * dk, (g + 1) * dk) for g in heads]
        vsl = [slice(g * dv, (g + 1) * dv) for g in heads]
        state = [st[g] for g in heads]
        for s in order:
            rs = pl.ds(s * c, c)
            q = [q_ref[rs, ksl[g]] for g in heads]
            k = [k_ref[rs, ksl[g]] for g in heads]
            v = [v_ref[rs, vsl[g]] for g in heads]
            bb = [_chunk_decay(la_ref[rs, ksl[g]], keep_f) for g in heads]
            qe = [q[g] * jnp.exp(bb[g][0]) for g in heads]
            ke = [k[g] * jnp.exp(-bb[g][0]) for g in heads]
            kl = [k[g] * jnp.exp(bb[g][1] - bb[g][0]) for g in heads]
            att = [jnp.where(keep, _nt(qe[g], ke[g]), 0.0) for g in heads]
            out = [_nt(qe[g], state[g]) + _nn(att[g], v[g]) for g in heads]
            new = [state[g] * jnp.exp(bb[g][1]) + _tn(v[g], kl[g]) for g in heads]
            for g in heads:
                save_ref[g, s] = state[g]
                o_ref[rs, vsl[g]] = out[g]
            state = new
        for g in heads:
            st[g] = state[g]

        @pl.when(j == nblk - 1)
        def _():
            fin_ref[...] = st[...]

        if sides.jobs:
            @pl.when((hh == h // hb - 1) & (j == nblk - 1))
            def _():
                sides.run("finish", side_in, side_out, side_sems)

    in_specs = [
        pl.BlockSpec((tb, hb * dk), lambda hh, j: (roff + blk(j), hh)),
        pl.BlockSpec((tb, hb * dk), lambda hh, j: (roff + blk(j), hh)),
        pl.BlockSpec((tb, hb * dv), lambda hh, j: (roff + blk(j), v_cb + hh)),
        pl.BlockSpec((tb, hb * dk), lambda hh, j: (roff + blk(j), hh)),
        pl.BlockSpec((hb, dv, dk), lambda hh, j: (hh, 0, 0)),
    ]
    out_specs = [
        pl.BlockSpec((tb, hb * dv), lambda hh, j: (blk(j), hh)),
        pl.BlockSpec((hb, nsub, dv, dk), lambda hh, j: (hh, blk(j), 0, 0)),
        pl.BlockSpec((hb, dv, dk), lambda hh, j: (hh, 0, 0)),
    ]
    out_shape = [
        jax.ShapeDtypeStruct((nrows, h * dv), F32),
        jax.ShapeDtypeStruct((h, nrows // c, dv, dk), F32),
        jax.ShapeDtypeStruct((h, dv, dk), F32),
    ]
    return pl.pallas_call(
        body, name=name, grid=(h // hb, nblk), in_specs=in_specs + [ANY_SPEC] * n_si,
        out_specs=out_specs + [ANY_SPEC] * n_so, out_shape=out_shape + list(sides.outs),
        scratch_shapes=[pltpu.VMEM((hb, dv, dk), F32)] + list(sides.sems),
        input_output_aliases=sides.aliases(5, 3),
        compiler_params=_cparams(("arbitrary", "arbitrary")),
    )(qr, kr, z, la, st0, *sides.ins)


def _gla_bwd(qr, kr, z, la, do, save, dst_init, cfg, *, rev, row_off, nrows, tb, name, sides=_NO_SIDES):
    h, dk, dv = cfg["H"], cfg["DK"], cfg["DV"]
    c = GLA_CHUNK
    nsub = tb // c
    nblk = nrows // tb
    roff = row_off // tb
    hb = _heads_per_step(cfg)
    v_cb = cfg["V0"] // (hb * dv)
    n_si, n_so = len(sides.ins), len(sides.outs)

    def blk(j):
        return j if rev else (nblk - 1 - j)

    def body(q_ref, k_ref, v_ref, la_ref, do_ref, save_ref, di_ref, *rest):
        side_in, rest = rest[:n_si], rest[n_si:]
        dq_ref, dk_ref, dv_ref, dla_ref, d0_ref = rest[:5]
        side_out, dst, side_sems = rest[5:5 + n_so], rest[5 + n_so], rest[6 + n_so:]
        hh, j = pl.program_id(0), pl.program_id(1)

        if sides.jobs:
            @pl.when((hh == 0) & (j == 0))
            def _():
                sides.run("start", side_in, side_out, side_sems)

        @pl.when(j == 0)
        def _():
            dst[...] = di_ref[...]

        keep, keep_f = _chunk_consts(rev)
        keep_t = _chunk_consts(not rev)[1]
        order = range(nsub) if rev else range(nsub - 1, -1, -1)
        heads = range(hb)
        ksl = [slice(g * dk, (g + 1) * dk) for g in heads]
        vsl = [slice(g * dv, (g + 1) * dv) for g in heads]
        d_after = [dst[g] for g in heads]
        for s in order:
            rs = pl.ds(s * c, c)
            q = [q_ref[rs, ksl[g]] for g in heads]
            k = [k_ref[rs, ksl[g]] for g in heads]
            v = [v_ref[rs, vsl[g]] for g in heads]
            lac = [la_ref[rs, ksl[g]] for g in heads]
            dout = [do_ref[rs, vsl[g]] for g in heads]
            s_in = [save_ref[g, s] for g in heads]
            bb = [_chunk_decay(lac[g], keep_f) for g in heads]
            eb = [jnp.exp(bb[g][0]) for g in heads]
            enb = [jnp.exp(-bb[g][0]) for g in heads]
            elb = [jnp.exp(bb[g][1] - bb[g][0]) for g in heads]
            etot = [jnp.exp(bb[g][1]) for g in heads]
            qe = [q[g] * eb[g] for g in heads]
            ke = [k[g] * enb[g] for g in heads]
            kl = [k[g] * elb[g] for g in heads]
            att = [jnp.where(keep, _nt(qe[g], ke[g]), 0.0) for g in heads]
            datt = [jnp.where(keep, _nt(dout[g], v[g]), 0.0) for g in heads]
            dqe = [_nn(dout[g], s_in[g]) + _nn(datt[g], ke[g]) for g in heads]
            dke = [_tn(datt[g], qe[g]) for g in heads]
            dkl = [_nn(v[g], d_after[g]) for g in heads]
            dvv = [_tn(att[g], dout[g]) + _nt(kl[g], d_after[g]) for g in heads]
            db = [dqe[g] * qe[g] - dke[g] * ke[g] - dkl[g] * kl[g] for g in heads]
            dbtot = [_colsum(dkl[g] * kl[g]) + _colsum(d_after[g] * s_in[g]) * etot[g] for g in heads]
            dla = [_dg(keep_t, db[g], ((1,), (0,)), HI) + dbtot[g] for g in heads]
            d_after = [d_after[g] * etot[g] + _tn(dout[g], qe[g]) for g in heads]
            for g in heads:
                dv_ref[rs, vsl[g]] = dvv[g]
                dla_ref[rs, ksl[g]] = dla[g]
                dq_ref[rs, ksl[g]] = dqe[g] * eb[g]
                dk_ref[rs, ksl[g]] = dke[g] * enb[g] + dkl[g] * elb[g]
        for g in heads:
            dst[g] = d_after[g]

        @pl.when(j == nblk - 1)
        def _():
            d0_ref[...] = dst[...]

        if sides.jobs:
            @pl.when((hh == h // hb - 1) & (j == nblk - 1))
            def _():
                sides.run("finish", side_in, side_out, side_sems)

    in_specs = [
        pl.BlockSpec((tb, hb * dk), lambda hh, j: (roff + blk(j), hh)),
        pl.BlockSpec((tb, hb * dk), lambda hh, j: (roff + blk(j), hh)),
        pl.BlockSpec((tb, hb * dv), lambda hh, j: (roff + blk(j), v_cb + hh)),
        pl.BlockSpec((tb, hb * dk), lambda hh, j: (roff + blk(j), hh)),
        pl.BlockSpec((tb, hb * dv), lambda hh, j: (blk(j), hh)),
        pl.BlockSpec((hb, nsub, dv, dk), lambda hh, j: (hh, blk(j), 0, 0)),
        pl.BlockSpec((hb, dv, dk), lambda hh, j: (hh, 0, 0)),
    ]
    out_specs = [
        pl.BlockSpec((tb, hb * dk), lambda hh, j: (blk(j), hh)),
        pl.BlockSpec((tb, hb * dk), lambda hh, j: (blk(j), hh)),
        pl.BlockSpec((tb, hb * dv), lambda hh, j: (blk(j), hh)),
        pl.BlockSpec((tb, hb * dk), lambda hh, j: (blk(j), hh)),
        pl.BlockSpec((hb, dv, dk), lambda hh, j: (hh, 0, 0)),
    ]
    out_shape = [
        jax.ShapeDtypeStruct((nrows, h * dk), F32),
        jax.ShapeDtypeStruct((nrows, h * dk), F32),
        jax.ShapeDtypeStruct((nrows, h * dv), F32),
        jax.ShapeDtypeStruct((nrows, h * dk), F32),
        jax.ShapeDtypeStruct((h, dv, dk), F32),
    ]
    return pl.pallas_call(
        body, name=name, grid=(h // hb, nblk), in_specs=in_specs + [ANY_SPEC] * n_si,
        out_specs=out_specs + [ANY_SPEC] * n_so, out_shape=out_shape + list(sides.outs),
        scratch_shapes=[pltpu.VMEM((hb, dv, dk), F32)] + list(sides.sems),
        input_output_aliases=sides.aliases(7, 5),
        compiler_params=_cparams(("arbitrary", "arbitrary")),
    )(qr, kr, z, la, do, save, dst_init, *sides.ins)


def _gla_post(gf, gb, la_f, la_b, z, tabs, wdf_pad, wdb_pad, cfg, *, row_off, nrows, tr, name):
    kw, vw = cfg["KW"], cfg["VW"]
    h = cfg["H"]
    ro = row_off // tr

    def body(rin, cin, rout, acc):
        dq = rin[0][...] + rin[1][...]
        dk_ = rin[2][...] + rin[3][...]
        zl = rin[10][...]
        cq, sq, ck, sk = [jnp.concatenate([rin[11 + t][...]] * h, axis=1) for t in range(4)]
        rout[0][...] = (dq * cq + _swap_halves(dq * sq, kw)).astype(BF16)
        rout[1][...] = (dk_ * ck + _swap_halves(dk_ * sk, kw)).astype(BF16)
        rout[2][...] = (rin[8][...] + rin[9][...]).astype(BF16)
        dzl = jnp.zeros(zl.shape, F32)
        for t, w in ((0, cin[0]), (1, cin[1])):
            la = rin[6 + t][...]
            da = rin[4 + t][...] * ((1.0 - jnp.exp(la * GLA_TAU)) * (1.0 / GLA_TAU))
            dzl = dzl + _nt(da, w[...])
            acc[t][...] += _tn(zl, da)
            acc[2 + t][...] += _colsum(da)
        rout[3][...] = dzl.astype(BF16)

    row_ins = [(gf[0], kw, 0, 0), (gb[0], kw, 0, 0), (gf[1], kw, 0, 0), (gb[1], kw, 0, 0),
               (gf[3], kw, 0, 0), (gb[3], kw, 0, 0), (la_f, kw, ro, 0), (la_b, kw, ro, 0),
               (gf[2], vw, 0, 0), (gb[2], vw, 0, 0), (z, LANES, ro, cfg["L0"] // LANES)]
    row_ins += [(t, LANES, ro, 0) for t in tabs]
    return _rows_call(body, name=name, nblk=nrows // tr, tr=tr, row_ins=row_ins, consts=[wdf_pad, wdb_pad],
                      row_outs=[(nrows, kw, BF16), (nrows, kw, BF16), (nrows, vw, BF16), (nrows, LANES, BF16)],
                      accs=[(LANES, kw), (LANES, kw), (1, kw), (1, kw)])


def _readout_fwd(o_f, o_b, z, g, cfg, *, tr):
    n, vw = o_f.shape
    h, dv = cfg["H"], cfg["DV"]

    def body(rin, cin, rout, acc):
        for hh in range(h):
            cs = slice(hh * dv, (hh + 1) * dv)
            oh = rin[0][:, cs] + rin[1][:, cs]
            y = oh * _rstd(oh) * cin[0][:, cs]
            rout[0][:, cs] = (y * _silu(rin[2][:, cs])).astype(BF16)

    return _rows_call(body, name="gla_readout", nblk=n // tr, tr=tr,
                      row_ins=[(o_f, vw, 0, 0), (o_b, vw, 0, 0), (z, vw, 0, cfg["R0"] // vw)], consts=[g],
                      row_outs=[(n, vw, BF16)])[0]


def _readout_bwd(o_f, o_b, z, dycat, g, cfg, *, tr):
    n, vw = o_f.shape
    h, dv = cfg["H"], cfg["DV"]

    def body(rin, cin, rout, acc):
        for hh in range(h):
            cs = slice(hh * dv, (hh + 1) * dv)
            oh = rin[0][:, cs] + rin[1][:, cs]
            r, dyg, gh = rin[2][:, cs], rin[3][:, cs], cin[0][:, cs]
            rs = _rstd(oh)
            dy = dyg * _silu(r)
            rout[0][:, cs] = _rms_bwd(oh, rs, dy * gh).astype(BF16)
            rout[1][:, cs] = (dyg * (oh * rs * gh) * _dsilu(r)).astype(BF16)
            acc[0][:, cs] += _colsum(dy * oh * rs)

    return _rows_call(body, name="gla_readout_bwd", nblk=n // tr, tr=tr,
                      row_ins=[(o_f, vw, 0, 0), (o_b, vw, 0, 0), (z, vw, 0, cfg["R0"] // vw), (dycat, vw, 0, 0)],
                      consts=[g], row_outs=[(n, vw, BF16), (n, vw, BF16)], accs=[(1, vw)])


def _sg_ln(vv):
    mu = jnp.mean(vv, axis=-1, keepdims=True)
    cen = vv - mu
    rstd = lax.rsqrt(jnp.mean(cen * cen, axis=-1, keepdims=True) + EPS)
    return cen * rstd, rstd


def _sg_fwd(z, n, lng, lnb, w_s, bs_full, cfg):
    sgw, grp, sc = cfg["SGW"], cfg["SG_G"], cfg["SG_C"]
    gw = sgw // grp

    def body(rin, cin, rout, acc):
        u = _gelu(rin[0][...])
        xhat, _ = _sg_ln(_gelu(rin[1][...]))
        vvn = xhat * cin[0][...] + cin[1][...]
        for gg in range(grp):
            cs = slice(gg * gw, (gg + 1) * gw)
            s = _nn(cin[2][gg], vvn[:, cs]) + cin[3][:, cs]
            rout[0][:, cs] = (u[:, cs] * s).astype(BF16)

    return _rows_call(body, name="sg_fwd", nblk=n // sc, tr=sc,
                      row_ins=[(z, sgw, 0, cfg["U0"] // sgw), (z, sgw, 0, cfg["VV0"] // sgw)],
                      consts=[lng, lnb, w_s, bs_full], row_outs=[(n, sgw, BF16)])[0]


def _sg_bwd(z, dycat, n, lng, lnb, w_s, bs_full, cfg):
    sgw, grp, sc = cfg["SGW"], cfg["SG_G"], cfg["SG_C"]
    gw = sgw // grp

    def body(rin, cin, rout, acc):
        up, vp, dy = rin[0][...], rin[1][...], rin[2][...]
        u = _gelu(up)
        xhat, rstd = _sg_ln(_gelu(vp))
        lng_v = cin[0][...]
        vvn = xhat * lng_v + cin[1][...]
        ds = dy * u
        acc[1][...] += ds
        dvvn_parts = []
        for gg in range(grp):
            cs = slice(gg * gw, (gg + 1) * gw)
            w = cin[2][gg]
            s = _nn(w, vvn[:, cs]) + cin[3][:, cs]
            rout[0][:, cs] = (dy[:, cs] * s * _dgelu(up[:, cs])).astype(BF16)
            acc[0][gg] += _nt(ds[:, cs], vvn[:, cs])
            dvvn_parts.append(_tn(w, ds[:, cs]))
        dvvn = jnp.concatenate(dvvn_parts, axis=1)
        acc[2][...] += _colsum(dvvn * xhat)
        acc[3][...] += _colsum(dvvn)
        dxh = dvvn * lng_v
        dvv = rstd * (dxh - jnp.mean(dxh, axis=-1, keepdims=True)
                      - xhat * jnp.mean(dxh * xhat, axis=-1, keepdims=True))
        rout[0][:, sgw:] = (dvv * _dgelu(vp)).astype(BF16)

    vw = cfg["VW"]
    return _rows_call(body, name="sg_bwd", nblk=n // sc, tr=sc,
                      row_ins=[(z, sgw, 0, cfg["U0"] // sgw), (z, sgw, 0, cfg["VV0"] // sgw),
                               (dycat, sgw, 0, vw // sgw)],
                      consts=[lng, lnb, w_s, bs_full], row_outs=[(n, 2 * sgw, BF16)],
                      accs=[(grp, sc, sc), (sc, sgw), (1, sgw), (1, sgw)])


def _mid_fwd(x, mix, g1, post1, pre2, sh2, sc2, *, tr, sides=_NO_SIDES):
    n, d = x.shape

    def body(rin, cin, rout, acc):
        xv, mv = rin[0][...], rin[1][...]
        x1 = xv + cin[0][...] * (mv * _rstd(mv) * cin[1][...])
        rout[0][...] = x1
        n2 = x1 * _rstd(x1) * cin[2][...]
        rout[1][...] = (n2 * (1.0 + cin[4][...]) + cin[3][...]).astype(BF16)

    return _rows_call(body, name="mid_fwd", nblk=n // tr, tr=tr, row_ins=[(x, d, 0, 0), (mix, d, 0, 0)],
                      consts=[g1, post1, pre2, sh2, sc2], row_outs=[(n, d, F32), (n, d, BF16)], sides=sides)


def _head_bwd(x1, m2, target, g2, post2, *, tr):
    n, d = x1.shape

    def body(rin, cin, rout, acc):
        x1v, mv, tv = rin[0][...], rin[1][...], rin[2][...]
        g2v, pg = cin[0][...], cin[1][...]
        r = _rstd(mv)
        y2 = mv * r * pg
        err = (x1v + g2v * y2) - tv
        acc[2][...] += _colsum(err * err) * (0.5 / d)
        dx2 = err * (1.0 / d)
        rout[0][...] = dx2
        dy2 = dx2 * g2v
        acc[0][...] += _colsum(dx2 * y2)
        acc[1][...] += _colsum(dy2 * mv * r)
        rout[1][...] = _rms_bwd(mv, r, dy2 * pg).astype(BF16)

    return _rows_call(body, name="head_bwd", nblk=n // tr, tr=tr,
                      row_ins=[(x1, d, 0, 0), (m2, d, 0, 0), (target, d, 0, 0)], consts=[g2, post2],
                      row_outs=[(n, d, F32), (n, d, BF16)], accs=[(1, d)] * 3)


def _mid_bwd(dh2, x1, dx2, mix, sc2, pre2, g1, post1, *, tr, sides=_NO_SIDES):
    n, d = x1.shape

    def body(rin, cin, rout, acc):
        dh, x1v, dx2v, mv = rin[0][...], rin[1][...], rin[2][...], rin[3][...]
        sc2v, pre2v, g1v, post1v = cin[0][...], cin[1][...], cin[2][...], cin[3][...]
        r2 = _rstd(x1v)
        xr = x1v * r2
        acc[0][...] += _colsum(dh)
        acc[1][...] += _colsum(dh * (xr * pre2v))
        dn2 = dh * (1.0 + sc2v)
        acc[2][...] += _colsum(dn2 * xr)
        dx1 = dx2v + _rms_bwd(x1v, r2, dn2 * pre2v)
        rout[0][...] = dx1
        r1 = _rstd(mv)
        mr = mv * r1
        acc[3][...] += _colsum(dx1 * (mr * post1v))
        dy1 = dx1 * g1v
        acc[4][...] += _colsum(dy1 * mr)
        rout[1][...] = _rms_bwd(mv, r1, dy1 * post1v).astype(BF16)

    return _rows_call(body, name="mid_bwd", nblk=n // tr, tr=tr,
                      row_ins=[(dh2, d, 0, 0), (x1, d, 0, 0), (dx2, d, 0, 0), (mix, d, 0, 0)],
                      consts=[sc2, pre2, g1, post1], row_outs=[(n, d, F32), (n, d, BF16)], accs=[(1, d)] * 5,
                      sides=sides)


def _in_bwd(da, x, dres, sc1, pre1, *, row_off, tr, name, sides=_NO_SIDES):
    n, d = x.shape
    with_res = dres is not None

    def body(rin, cin, rout, acc):
        dav, xv = rin[0][...], rin[1][...]
        sc1v, pre1v = cin[0][...], cin[1][...]
        r = _rstd(xv)
        xr = xv * r
        acc[0][...] += _colsum(dav)
        acc[1][...] += _colsum(dav * (xr * pre1v))
        dn = dav * (1.0 + sc1v)
        acc[2][...] += _colsum(dn * xr)
        if with_res:
            rout[0][...] = rin[2][...] + _rms_bwd(xv, r, dn * pre1v)

    row_ins = [(da, d, row_off // tr, 0), (x, d, 0, 0)] + ([(dres, d, 0, 0)] if with_res else [])
    return _rows_call(body, name=name, nblk=n // tr, tr=tr, row_ins=row_ins, consts=[sc1, pre1],
                      row_outs=[(n, d, F32)] if with_res else [], accs=[(1, d)] * 3, sides=sides)


def _ada_fwd(c16, w, b, *, tn):
    d, ncol = w.shape

    def body(c_ref, w_ref, b_ref, o_ref):
        o_ref[...] = _nn(_silu(c_ref[...]), w_ref[...]) + b_ref[...]

    return pl.pallas_call(
        body, name="ada_fwd", grid=(ncol // tn,),
        in_specs=[pl.BlockSpec((16, d), lambda j: (0, 0)), pl.BlockSpec((d, tn), lambda j: (0, j)),
                  pl.BlockSpec((1, tn), lambda j: (0, j))],
        out_specs=pl.BlockSpec((16, tn), lambda j: (0, j)),
        out_shape=jax.ShapeDtypeStruct((16, ncol), F32),
        compiler_params=_cparams(("arbitrary",)),
    )(c16, w, b)


def _ada_bwd(c16, dm, w, c_ctx, *, tn):
    d, ncol = w.shape

    def body(c_ref, dm_ref, w_ref, cc_ref, gw_ref, dcc_ref, acc):
        j = pl.program_id(0)

        @pl.when(j == 0)
        def _():
            acc[...] = jnp.zeros_like(acc)

        gw_ref[...] = _tn(_silu(c_ref[...]), dm_ref[...])
        acc[...] += _nt(dm_ref[...], w_ref[...])

        @pl.when(j == ncol // tn - 1)
        def _():
            dcc_ref[...] = _colsum(acc[8:16, :]) * _dsilu(cc_ref[...])

    return pl.pallas_call(
        body, name="ada_bwd", grid=(ncol // tn,),
        in_specs=[pl.BlockSpec((16, d), lambda j: (0, 0)), pl.BlockSpec((16, tn), lambda j: (0, j)),
                  pl.BlockSpec((d, tn), lambda j: (0, j)), pl.BlockSpec((1, d), lambda j: (0, 0))],
        out_specs=[pl.BlockSpec((d, tn), lambda j: (0, j)), pl.BlockSpec((1, d), lambda j: (0, 0))],
        out_shape=[jax.ShapeDtypeStruct((d, ncol), F32), jax.ShapeDtypeStruct((1, d), F32)],
        scratch_shapes=[pltpu.VMEM((16, d), F32)],
        compiler_params=_cparams(("arbitrary",)),
    )(c16, dm, w, c_ctx)


def _adam_math(w, g, m, v):
    m = ADAM_B1 * m + (1.0 - ADAM_B1) * g
    v = ADAM_B2 * v + (1.0 - ADAM_B2) * (g * g)
    m_hat = m / (1.0 - ADAM_B1 ** ADAM_STEP)
    v_hat = v / (1.0 - ADAM_B2 ** ADAM_STEP)
    delta = -ADAM_LR * (m_hat / (jnp.sqrt(v_hat) + ADAM_EPS) + ADAM_WD * w)
    return delta, m, v


def _adam_big(parts, w, m, v, *, name, tr, sides=_NO_SIDES):
    rows, cols = w.shape
    n_p = len(parts)
    n_si, n_so = len(sides.ins), len(sides.outs)
    nblk = rows // tr

    def body(*refs):
        ins, refs = refs[:n_p + 3], refs[n_p + 3:]
        side_in, refs = refs[:n_si], refs[n_si:]
        outs, side_out, side_sems = refs[:4], refs[4:4 + n_so], refs[4 + n_so:]
        i = pl.program_id(0)

        if sides.jobs:
            @pl.when(i == 0)
            def _():
                sides.run("start", side_in, side_out, side_sems)

        g = ins[0][...]
        for p in ins[1:n_p]:
            g = g + p[...].astype(F32)
        delta, m2, v2 = _adam_math(ins[n_p][...], g, ins[n_p + 1][...], ins[n_p + 2][...])
        outs[0][...] = g
        outs[1][...] = delta
        outs[2][...] = m2
        outs[3][...] = v2

        if sides.jobs:
            @pl.when(i == nblk - 1)
            def _():
                sides.run("finish", side_in, side_out, side_sems)

    plain = pl.BlockSpec((tr, cols), lambda i: (i, 0))
    in_specs = []
    for arr, idx in parts:
        if idx is None:
            in_specs.append(plain)
        else:
            in_specs.append(pl.BlockSpec((None, tr, cols), lambda i, idx=idx: (idx, i, 0)))
    in_specs += [plain] * 3
    return pl.pallas_call(
        body, name=name, grid=(nblk,), in_specs=in_specs + [ANY_SPEC] * n_si,
        out_specs=[plain] * 4 + [ANY_SPEC] * n_so,
        out_shape=[jax.ShapeDtypeStruct((rows, cols), F32)] * 4 + list(sides.outs),
        scratch_shapes=list(sides.sems), input_output_aliases=sides.aliases(n_p + 3, 4),
        compiler_params=_cparams(("arbitrary",) if sides.jobs else ("parallel",)),
    )(*[p[0] for p in parts], w, m, v, *sides.ins)


def _sum_parts(parts, rows, cols, *, name, tr):
    def body(*refs):
        g = refs[0][...].astype(F32)
        for p in refs[1:-1]:
            g = g + p[...].astype(F32)
        refs[-1][...] = g

    in_specs = [pl.BlockSpec((None, tr, cols), lambda i, idx=idx: (idx, i, 0)) for _, idx in parts]
    return pl.pallas_call(
        body, name=name, grid=(rows // tr,), in_specs=in_specs, out_specs=pl.BlockSpec((tr, cols), lambda i: (i, 0)),
        out_shape=jax.ShapeDtypeStruct((rows, cols), F32), compiler_params=_cparams(("parallel",)),
    )(*[p[0] for p in parts])


def _adam_small(g8, w, m, v):
    def body(g_ref, w_ref, m_ref, v_ref, go, do, mo, vo):
        g = g_ref[0]
        for r in range(1, N_DEV):
            g = g + g_ref[r]
        delta, m2, v2 = _adam_math(w_ref[...], g, m_ref[...], v_ref[...])
        go[...] = g
        do[...] = delta
        mo[...] = m2
        vo[...] = v2

    return pl.pallas_call(
        body, name="adam_small", out_shape=[jax.ShapeDtypeStruct(w.shape, F32)] * 4,
        compiler_params=pltpu.CompilerParams(vmem_limit_bytes=VMEM_LIMIT),
    )(g8, w, m, v)


VEC_W = 1024
TK = 2048
ELEMS_PER_BLOCK = 256 * 1024


def _dense(v):
    a, k = v.shape
    kp = -(-k // (8 * VEC_W)) * (8 * VEC_W)
    return jnp.pad(v, ((0, 0), (0, kp - k))).reshape(a, kp // VEC_W, VEC_W)


def _all_gather_vec(v, *, name):
    k = v.shape[1]
    return _all_gather_small(_dense(v)[0], name=name).reshape(N_DEV, -1)[:, :k]


def _my_pos():
    return lax.axis_index("x"), lax.axis_index("y"), lax.axis_index("c")


def _flip(v, bit):
    return (1 - v) if bit else v


def _all_gather_small(v, *, name):
    r, k = v.shape

    def body(v_ref, out_ref, send, recv, lsem):
        x, y, c = _my_pos()
        me = 4 * x + 2 * y + c
        local = pltpu.make_async_copy(v_ref, out_ref.at[me], lsem)
        local.start()
        sends = []
        for kk in range(1, N_DEV):
            peer = (_flip(x, kk & 4), _flip(y, kk & 2), _flip(c, kk & 1))
            cp = pltpu.make_async_remote_copy(src_ref=v_ref, dst_ref=out_ref.at[me], send_sem=send.at[kk - 1],
                                              recv_sem=recv.at[kk - 1], device_id=peer, device_id_type=MESH)
            cp.start()
            sends.append(cp)
        for kk in range(1, N_DEV):
            px, py, pc = _flip(x, kk & 4), _flip(y, kk & 2), _flip(c, kk & 1)
            src = 4 * px + 2 * py + pc
            pltpu.make_async_remote_copy(src_ref=v_ref, dst_ref=out_ref.at[src], send_sem=send.at[kk - 1],
                                         recv_sem=recv.at[kk - 1], device_id=(px, py, pc),
                                         device_id_type=MESH).wait_recv()
        for cp in sends:
            cp.wait_send()
        local.wait()

    return pl.pallas_call(
        body, name=name, out_shape=jax.ShapeDtypeStruct((N_DEV, r, k), v.dtype),
        in_specs=[pl.BlockSpec(memory_space=pltpu.VMEM)], out_specs=pl.BlockSpec(memory_space=pltpu.VMEM),
        scratch_shapes=[pltpu.SemaphoreType.DMA((N_DEV - 1,)), pltpu.SemaphoreType.DMA((N_DEV - 1,)),
                        pltpu.SemaphoreType.DMA],
        compiler_params=pltpu.CompilerParams(vmem_limit_bytes=VMEM_LIMIT),
    )(v)


def _ag_job(shards, rows=None, chained=None):
    n_arr = len(shards)

    def part(ref):
        return ref if rows is None else ref.at[pl.ds(rows[0], rows[1])]

    def tools(ins, outs, sems):
        send, recv, lsem = sems
        x, y, c = _my_pos()
        chips = [(1 - x, y), (x, 1 - y), (1 - x, 1 - y)]

        def copy(a, kk, block, to, src=None):
            dst = part(outs[a].at[4 * block[0] + 2 * block[1] + block[2]])
            return pltpu.make_async_remote_copy(src_ref=dst if src is None else part(src), dst_ref=dst,
                                                send_sem=send.at[a, kk], recv_sem=recv.at[a, kk],
                                                device_id=to, device_id_type=MESH)

        locals_ = [pltpu.make_async_copy(part(ins[a]), part(outs[a].at[4 * x + 2 * y + c]), lsem.at[a])
                   for a in range(n_arr)]
        firsts = []
        for a in range(n_arr):
            firsts.append(copy(a, 0, (x, y, c), (x, y, 1 - c), src=ins[a]))
            firsts += [copy(a, 1 + j, (x, y, c), (*chip, c), src=ins[a]) for j, chip in enumerate(chips)]
        return copy, locals_, firsts, chips, (x, y, c)

    def start(ins, outs, sems):
        _, locals_, firsts, _, _ = tools(ins, outs, sems)
        for cp in locals_ + firsts:
            cp.start()

    def finish(ins, outs, sems):
        copy, locals_, firsts, chips, (x, y, c) = tools(ins, outs, sems)
        me, sibling = (x, y, c), (x, y, 1 - c)
        passed = []
        for a in range(n_arr):
            for j, chip in enumerate(chips):
                copy(a, 1 + j, (*chip, c), me).wait_recv()
                fw = copy(a, 4 + j, (*chip, c), sibling)
                fw.start()
                passed.append(fw)
        for a in range(n_arr):
            copy(a, 0, sibling, me).wait_recv()
            for j, chip in enumerate(chips):
                copy(a, 4 + j, (*chip, 1 - c), me).wait_recv()
        for cp in firsts + passed:
            cp.wait_send()
        for lc in locals_:
            lc.wait()

    job = dict(ins=list(shards), outs=[jax.ShapeDtypeStruct((N_DEV,) + s.shape, s.dtype) for s in shards],
               sems=[pltpu.SemaphoreType.DMA((n_arr, 7)), pltpu.SemaphoreType.DMA((n_arr, 7)),
                     pltpu.SemaphoreType.DMA((n_arr,))], start=start, finish=finish)
    if chained is not None:
        job["ins"] = list(shards) + list(chained)
        job["alias"] = {n_arr + a: a for a in range(n_arr)}
    return job


def _exchange_job(arrays, n_slots, out_slots, src_of, dst_of, peer_of, rows=None, chained=None):
    n_arr = len(arrays)

    def copies(ins, outs, sems):
        send, recv = sems
        x, y, c = _my_pos()
        res = []
        for a in range(n_arr):
            for s in range(n_slots):
                src, dst = ins[a].at[src_of(s, x, y, c)], outs[a].at[dst_of(s)]
                if rows is not None:
                    src, dst = src.at[pl.ds(rows[0], rows[1])], dst.at[pl.ds(rows[0], rows[1])]
                res.append(pltpu.make_async_remote_copy(
                    src_ref=src, dst_ref=dst, send_sem=send.at[a, s], recv_sem=recv.at[a, s],
                    device_id=peer_of(s, x, y, c), device_id_type=MESH))
        return res

    def start(ins, outs, sems):
        for cp in copies(ins, outs, sems):
            cp.start()

    def finish(ins, outs, sems):
        cps = copies(ins, outs, sems)
        for cp in cps:
            cp.wait_recv()
        for cp in cps:
            cp.wait_send()

    job = dict(ins=list(arrays), outs=[jax.ShapeDtypeStruct((out_slots,) + g.shape[1:], g.dtype) for g in arrays],
               sems=[pltpu.SemaphoreType.DMA((n_arr, n_slots)), pltpu.SemaphoreType.DMA((n_arr, n_slots))],
               start=start, finish=finish)
    if chained is not None:
        job["ins"] = list(arrays) + list(chained)
        job["alias"] = {n_arr + a: a for a in range(n_arr)}
    return job


def _pair_job(grads):
    return _exchange_job(
        grads, 4, 4,
        src_of=lambda s, x, y, c: 4 * _flip(x, s & 2) + 2 * _flip(y, s & 1) + (1 - c),
        dst_of=lambda s: s, peer_of=lambda s, x, y, c: (x, y, 1 - c))


def _chip_job(sums, rows=None, chained=None):
    return _exchange_job(
        sums, 3, 3, src_of=lambda s, x, y, c: s, dst_of=lambda s: s,
        peer_of=lambda s, x, y, c: (_flip(x, (s + 1) & 2), _flip(y, (s + 1) & 1), c), rows=rows, chained=chained)


def _a2a_job(x):
    def copies(ins, outs, sems):
        send, recv, lsem = sems
        x_, y_, c_ = _my_pos()
        me = 4 * x_ + 2 * y_ + c_
        local = pltpu.make_async_copy(ins[0].at[me], outs[0].at[me], lsem)
        res = []
        for s in range(1, N_DEV):
            px, py, pc = _flip(x_, s & 4), _flip(y_, s & 2), _flip(c_, s & 1)
            res.append(pltpu.make_async_remote_copy(
                src_ref=ins[0].at[4 * px + 2 * py + pc], dst_ref=outs[0].at[me], send_sem=send.at[s - 1],
                recv_sem=recv.at[s - 1], device_id=(px, py, pc), device_id_type=MESH))
        return local, res

    def start(ins, outs, sems):
        local, res = copies(ins, outs, sems)
        local.start()
        for cp in res:
            cp.start()

    def finish(ins, outs, sems):
        local, res = copies(ins, outs, sems)
        for cp in res:
            cp.wait_recv()
        for cp in res:
            cp.wait_send()
        local.wait()

    return dict(ins=[x], outs=[jax.ShapeDtypeStruct(x.shape, x.dtype)],
                sems=[pltpu.SemaphoreType.DMA((N_DEV - 1,)), pltpu.SemaphoreType.DMA((N_DEV - 1,)),
                      pltpu.SemaphoreType.DMA], start=start, finish=finish)


def _run_sides(sides, *, name):
    n_si, n_so = len(sides.ins), len(sides.outs)

    def body(*refs):
        ins, outs, sems = refs[:n_si], refs[n_si:n_si + n_so], refs[n_si + n_so:]
        sides.run("start", ins, outs, sems)
        sides.run("finish", ins, outs, sems)

    return pl.pallas_call(
        body, name=name, out_shape=list(sides.outs), in_specs=[ANY_SPEC] * n_si, out_specs=[ANY_SPEC] * n_so,
        scratch_shapes=list(sides.sems), input_output_aliases=sides.aliases(0, 0),
    )(*sides.ins)


def _pair_add(g, t, *, name, tr, wire):
    _, r, cols = g.shape
    g4 = g.reshape(4, 2, r, cols)
    j0 = 1 if wire else 0

    def g_index(j, i):
        x, y, c = _my_pos()
        return (jnp.bitwise_xor(2 * x + y, j + j0), c, i, 0)

    def body(g_ref, t_ref, o_ref):
        o_ref[...] = (g_ref[...] + t_ref[...]).astype(o_ref.dtype)

    return pl.pallas_call(
        body, name=name, grid=(3 if wire else 1, r // tr),
        in_specs=[pl.BlockSpec((None, None, tr, cols), g_index),
                  pl.BlockSpec((None, tr, cols), lambda j, i: (j + j0, i, 0))],
        out_specs=pl.BlockSpec((None, tr, cols), lambda j, i: (j, i, 0)),
        out_shape=jax.ShapeDtypeStruct((3 if wire else 1, r, cols), BF16 if wire else F32),
        compiler_params=_cparams(("arbitrary", "arbitrary")),
    )(g4, t)


def _config(x, ctx, w_in, w_dec_f, gla_norm_g, sg_ln_g, w_s):
    n, d = x.shape[1], x.shape[2]
    tc = ctx.shape[1]
    h = gla_norm_g.shape[1]
    dv = gla_norm_g.shape[2] * N_DEV
    dk = dv // 2
    kw, vw = h * dk, h * dv
    lr = w_dec_f.shape[1]
    sgw = sg_ln_g.shape[1]
    cfg = dict(N=n, D=d, TC=tc, H=h, DV=dv, DK=dk, KW=kw, VW=vw, LR=lr, SGW=sgw, SG_G=w_s.shape[1],
               SG_C=w_s.shape[2], IN=w_in.shape[2] * N_DEV)
    cfg.update(K0=kw, V0=2 * kw, R0=2 * kw + vw, L0=2 * kw + 2 * vw, ZA=2 * kw + 2 * vw + LANES)
    cfg.update(U0=0, VV0=sgw, ZB=2 * sgw)
    assert dk == LANES and vw == 2 * kw and 2 * lr <= LANES
    assert cfg["R0"] % vw == 0 and vw % sgw == 0
    assert cfg["IN"] == 2 * kw + 2 * vw + 2 * lr + 2 * sgw
    return cfg


def _rope_tables(cfg):
    n, tc, dk = cfg["N"], cfg["TC"], cfg["DK"]
    m = dk // 4
    pos = jnp.arange(n)
    inv = ROPE_BASE ** (-jnp.arange(m, dtype=F32) / m)
    ang_r = (pos // GRID_W).astype(F32)[:, None] * inv[None, :]
    ang_c = (pos % GRID_W).astype(F32)[:, None] * inv[None, :]
    cos = jnp.concatenate([jnp.cos(ang_r)] * 2 + [jnp.cos(ang_c)] * 2, axis=1)
    sin = jnp.concatenate([-jnp.sin(ang_r), jnp.sin(ang_r), -jnp.sin(ang_c), jnp.sin(ang_c)], axis=1)
    scale = dk ** -0.5
    z = jnp.zeros((tc, dk), F32)
    one = jnp.ones((tc, dk), F32)
    return [jnp.concatenate([cos * scale, z]), jnp.concatenate([sin * scale, z]),
            jnp.concatenate([cos, one]), jnp.concatenate([sin, z])]


def _pair_sums(g, t, nm):
    rows_for = _tile(g.shape[1], max(8, ELEMS_PER_BLOCK // g.shape[2]), 16)
    return (_pair_add(g, t, name="rs_own_" + nm, tr=rows_for, wire=False),
            _pair_add(g, t, name="rs_wire_" + nm, tr=rows_for, wire=True))


def _local_step(x, ctx, target, mods, c_mods, w, cfg):
    n, d, tc = cfg["N"], cfg["D"], cfg["TC"]
    kw, vw, sgw, za, zb, lr = cfg["KW"], cfg["VW"], cfg["SGW"], cfg["ZA"], cfg["ZB"], cfg["LR"]
    sh1, sc1, g1, sh2, sc2, g2 = mods
    csh1, csc1 = c_mods
    rt = n + tc
    tb = math.gcd(256, math.gcd(n, tc))
    tr = math.gcd(128, tb)
    tr_s = math.gcd(128, tb)
    fs = w["sh_1"].shape[1]
    ff = fs * N_DEV
    cs_in = w["sh_in"].shape[1]
    r8 = d // N_DEV

    rows_in = _run_sides(_Sides([_a2a_job(w["sh_in"].reshape(N_DEV, r8, cs_in))]), name="a2a_w_in")[0]
    rows_in = rows_in.transpose(1, 0, 2).reshape(r8, cfg["IN"])
    lf0 = 2 * kw + 2 * vw
    sg0 = lf0 + 2 * lr
    wa_rows = jnp.concatenate([rows_in[:, :sg0], jnp.zeros((r8, LANES - 2 * lr), BF16)], axis=1)
    wb_rows = rows_in[:, sg0:]
    w_a = _run_sides(_Sides([_ag_job([wa_rows])]), name="ag_w_in_a")[0].reshape(d, za)

    hx = _norm_mod(x, w["pre1_g"], sh1, sc1, name="in_norm_x", tr=tr)
    hc = _norm_mod(ctx, w["pre1_g"], csh1, csc1, name="in_norm_ctx", tr=tr)
    a_all = jnp.concatenate([hx, hc], axis=0)

    tm_a = _tile(rt, 1152, 16)
    tm_n = _tile(n, 1024, 16)
    sh_o_rows = w["sh_o"].shape[0]
    o_cut = (sh_o_rows * 3 // 4) // 16 * 16
    z, w_b, wg_o = _matmul(
        a_all, w_a, "nn", rt, za, d, tm=tm_a, tn=_tile(za, 1152, LANES), tk=_tile(d, TK, LANES), name="mm_in_a",
        out_shapes=[jax.ShapeDtypeStruct((rt, za), F32)],
        sides=_Sides([_ag_job([wb_rows]), _ag_job([w["sh_o"]], rows=(0, o_cut))]))
    w_b = w_b.reshape(d, zb)
    z_b, wg_o, w_1 = _matmul(
        a_all, w_b, "nn", n, zb, d, tm=tm_n, tn=_tile(zb, 1024, LANES), tk=_tile(d, TK, LANES), name="mm_in_b",
        out_shapes=[jax.ShapeDtypeStruct((n, zb), F32)],
        sides=_Sides([_ag_job([w["sh_o"]], rows=(o_cut, sh_o_rows - o_cut), chained=[wg_o]),
                      _ag_job([w["sh_1"]], rows=(0, d // 4))]))
    w_o = wg_o.reshape(d, d)

    tabs = _rope_tables(cfg)
    qr, kr, la_f, la_b = _gla_prep(z, tabs, w["wdf_pad"], w["wdb_pad"], w["b_dec_f"], w["b_dec_b"], cfg, tr=tr)

    zero_st = jnp.zeros((cfg["H"], cfg["DV"], cfg["DK"]), F32)
    gla = dict(cfg=cfg, tb=tb)
    _, save_cf, st_cf = _gla_fwd(qr, kr, z, la_f, zero_st, rev=False, row_off=n, nrows=tc, name="gla_ctx_f", **gla)
    _, save_cb, st_cb = _gla_fwd(qr, kr, z, la_b, zero_st, rev=True, row_off=n, nrows=tc, name="gla_ctx_b", **gla)
    o_f, save_f, _, w_1 = _gla_fwd(
        qr, kr, z, la_f, st_cf, rev=False, row_off=0, nrows=n, name="gla_f",
        sides=_Sides([_ag_job([w["sh_1"]], rows=(d // 4, d // 4), chained=[w_1])]), **gla)
    o_b, save_b, _, w_1 = _gla_fwd(
        qr, kr, z, la_b, st_cb, rev=True, row_off=0, nrows=n, name="gla_b",
        sides=_Sides([_ag_job([w["sh_1"]], rows=(d // 2, d // 4), chained=[w_1])]), **gla)
    y_gla = _readout_fwd(o_f, o_b, z, w["gla_g"], cfg, tr=tr)
    y_sg = _sg_fwd(z_b, n, w["sg_ln_g"], w["sg_ln_b"], w["w_s"], w["bs_full"], cfg)
    ycat = jnp.concatenate([y_gla, y_sg], axis=1)

    mix, w_1 = _matmul(ycat, w_o, "nn", n, d, d, tm=tm_n, tn=_tile(d, 1024, LANES), tk=_tile(d, TK, LANES),
                       name="mm_o", out_shapes=[jax.ShapeDtypeStruct((n, d), F32)],
                       sides=_Sides([_ag_job([w["sh_1"]], rows=(3 * d // 4, d // 4), chained=[w_1])]))
    w2_cut = (fs // 8) // 16 * 16
    x1, h2, wg_2 = _mid_fwd(x, mix, g1, w["post1_g"], w["pre2_g"], sh2, sc2, tr=tr_s,
                            sides=_Sides([_ag_job([w["sh_2"]], rows=(0, w2_cut))]))

    tn_f = _tile(fs, 1024, LANES)
    tk_d = _tile(d, TK, LANES)

    def relu2(acc):
        return acc, jnp.square(jnp.maximum(acc, 0.0))

    a1, p1, wg_2 = _matmul(h2, w_1, "nn", n, ff, d, tm=tm_n, tn=tn_f, tk=tk_d, name="mm_1",
                           b_spec=_blocked_b_nn(fs, tk_d, tn_f), epilogue=relu2,
                           out_shapes=[jax.ShapeDtypeStruct((n, ff), BF16)] * 2,
                           sides=_Sides([_ag_job([w["sh_2"]], rows=(w2_cut, fs - w2_cut), chained=[wg_2])]))
    w_2 = wg_2.reshape(ff, d)
    tk_f = _tile(ff, TK, LANES)
    m2 = _matmul(p1, w_2, "nn", n, d, ff, tm=tm_n, tn=_tile(d, 1024, LANES), tk=tk_f, name="mm_2",
                 out_shapes=[jax.ShapeDtypeStruct((n, d), F32)])[0]

    dx2, dm2, dg2, dpost2, lossc = _head_bwd(x1, m2, target, g2, w["post2_g"], tr=tr_s)

    def drelu2(acc, a):
        return (acc * (2.0 * jnp.maximum(a.astype(F32), 0.0)),)

    da1 = _matmul(dm2, w_2, "nt", n, ff, d, tm=tm_n, tn=_tile(ff, 1024, LANES), tk=tk_d, name="mm_2_dx",
                  epilogue=drelu2, extras=(a1,), out_shapes=[jax.ShapeDtypeStruct((n, ff), BF16)])[0]
    tk_n = _tile(n, TK, 16)
    tm_d = _tile(d, 1024, LANES)
    g_1 = _matmul(h2, da1, "tn", d, ff, n, tm=tm_d, tn=tn_f, tk=tk_n, name="mm_1_dw",
                  out_specs=[_blocked_out(fs, tm_d, tn_f)],
                  out_shapes=[jax.ShapeDtypeStruct((N_DEV, d, fs), F32)])[0]
    dw_2, t_1 = _matmul(p1, dm2, "tn", ff, d, n, tm=_tile(ff, 1024, LANES), tn=_tile(d, 1024, LANES), tk=tk_n,
                        name="mm_2_dw", out_shapes=[jax.ShapeDtypeStruct((ff, d), F32)],
                        sides=_Sides([_pair_job([g_1])]))
    g_2 = dw_2.reshape(N_DEV, fs, d)
    p1_own, p1_wire = _pair_sums(g_1, t_1, "w_1")
    tk_fs = _tile(fs, TK, LANES)
    dh2, u_1, t_2 = _matmul(da1, w_1, "nt", n, d, ff, tm=tm_n, tn=_tile(d, 1024, LANES), tk=tk_fs, name="mm_1_dx",
                            b_spec=_blocked_b_nt(fs, _tile(d, 1024, LANES), tk_fs),
                            out_shapes=[jax.ShapeDtypeStruct((n, d), F32)],
                            sides=_Sides([_chip_job([p1_wire], rows=(0, d * 13 // 16)), _pair_job([g_2])]))
    p2_own, p2_wire = _pair_sums(g_2, t_2, "w_2")
    c2 = [0] + [(fs * f // 64) // 16 * 16 for f in (20, 35, 50)] + [fs]
    piece2 = lambda i: (c2[i], c2[i + 1] - c2[i])

    dx1, dmix, dsh2, dsc2, dpre2, dg1, dpost1, u_1 = _mid_bwd(
        dh2, x1, dx2, mix, sc2, w["pre2_g"], g1, w["post1_g"], tr=tr_s,
        sides=_Sides([_chip_job([p1_wire], rows=(d * 13 // 16, d - d * 13 // 16), chained=[u_1])]))
    dw_o, u_2 = _matmul(ycat, dmix, "tn", d, d, n, tm=tm_d, tn=_tile(d, 1024, LANES), tk=tk_n, name="mm_o_dw",
                        out_shapes=[jax.ShapeDtypeStruct((d, d), F32)],
                        sides=_Sides([_chip_job([p2_wire], rows=piece2(0))]))
    g_o = dw_o.reshape(N_DEV, r8, d)
    dycat, t_o, u_2 = _matmul(dmix, w_o, "nt", n, d, d, tm=tm_n, tn=_tile(d, 1024, LANES), tk=tk_d, name="mm_o_dx",
                              out_shapes=[jax.ShapeDtypeStruct((n, d), F32)],
                              sides=_Sides([_pair_job([g_o]), _chip_job([p2_wire], rows=piece2(1), chained=[u_2])]))
    po_own, po_wire = _pair_sums(g_o, t_o, "w_o")

    dz_b, dws, dbs_acc, dlng, dlnb = _sg_bwd(z_b, dycat, n, w["sg_ln_g"], w["sg_ln_b"], w["w_s"], w["bs_full"], cfg)
    dw_b, u_2 = _matmul(a_all, dz_b, "tn", d, zb, n, tm=tm_d, tn=_tile(zb, 1024, LANES), tk=tk_n, name="mm_in_dw_b",
                        out_shapes=[jax.ShapeDtypeStruct((d, zb), F32)],
                        sides=_Sides([_chip_job([p2_wire], rows=piece2(2), chained=[u_2])]))
    g_b = dw_b.reshape(N_DEV, r8, zb)
    do, dzr, dgla_g = _readout_bwd(o_f, o_b, z, dycat, w["gla_g"], cfg, tr=tr)

    *gf, t_b = _gla_bwd(qr, kr, z, la_f, do, save_f, zero_st, rev=False, row_off=0, nrows=n, name="gla_f_bwd",
                        sides=_Sides([_pair_job([g_b])]), **gla)
    pb_own, pb_wire = _pair_sums(g_b, t_b, "w_in_b")
    gb = _gla_bwd(qr, kr, z, la_b, do, save_b, zero_st, rev=True, row_off=0, nrows=n, name="gla_b_bwd", **gla)
    do_c = jnp.zeros((tc, vw), BF16)
    gcf = _gla_bwd(qr, kr, z, la_f, do_c, save_cf, gf[4], rev=False, row_off=n, nrows=tc, name="gla_ctx_f_bwd",
                   **gla)
    gcb = _gla_bwd(qr, kr, z, la_b, do_c, save_cb, gb[4], rev=True, row_off=n, nrows=tc, name="gla_ctx_b_bwd",
                   **gla)

    post = dict(la_f=la_f, la_b=la_b, z=z, tabs=tabs, wdf_pad=w["wdf_pad"], wdb_pad=w["wdb_pad"], cfg=cfg, tr=tr)
    dzq, dzk, dzv, dzl, dwdf, dwdb, dbdf, dbdb = _gla_post(gf, gb, row_off=0, nrows=n, name="gla_post", **post)
    czq, czk, czv, czl, cwdf, cwdb, cbdf, cbdb = _gla_post(gcf, gcb, row_off=n, nrows=tc, name="gla_post_ctx",
                                                           **post)
    dz_a = jnp.concatenate([
        jnp.concatenate([dzq, dzk, dzv, dzr, dzl], axis=1),
        jnp.concatenate([czq, czk, czv, jnp.zeros((tc, vw), BF16), czl], axis=1)], axis=0)

    dw_a, u_b, u_2 = _matmul(
        a_all, dz_a, "tn", d, za, rt, tm=tm_d, tn=_tile(za, 1152, LANES), tk=_tile(rt, 2176, 16), name="mm_in_dw_a",
        out_shapes=[jax.ShapeDtypeStruct((d, za), F32)],
        sides=_Sides([_chip_job([pb_wire]), _chip_job([p2_wire], rows=piece2(3), chained=[u_2])]))
    g_a = dw_a.reshape(N_DEV, r8, za)
    da_a, t_a, u_o = _matmul(dz_a, w_a, "nt", rt, d, za, tm=_tile(rt, 576, 16), tn=_tile(d, 512, LANES),
                             tk=za, name="mm_in_dx_a", out_shapes=[jax.ShapeDtypeStruct((rt, d), F32)],
                             sides=_Sides([_pair_job([g_a]), _chip_job([po_wire])]))
    pa_own, pa_wire = _pair_sums(g_a, t_a, "w_in_a")
    cut_a = (r8 * 9 // 16) // 16 * 16
    tm_x = _tile(n, 512, 16)
    da_x, u_a = _matmul(dz_b, w_b, "nt", n, d, zb, tm=tm_x, tn=_tile(d, 1024, LANES), tk=_tile(zb, 4096, LANES),
                        name="mm_in_dx_b", epilogue=lambda acc, prev: (acc + prev,), extras=(da_a,),
                        out_shapes=[jax.ShapeDtypeStruct((n, d), F32)],
                        sides=_Sides([_chip_job([pa_wire], rows=(0, cut_a))]))

    grad_x, dsh1, dsc1, dpre1, u_a = _in_bwd(
        da_x, x, dx1, sc1, w["pre1_g"], row_off=0, tr=tr_s, name="in_bwd_x",
        sides=_Sides([_chip_job([pa_wire], rows=(cut_a, r8 - cut_a), chained=[u_a])]))
    dcsh1, dcsc1, dpre1_c = _in_bwd(da_a, ctx, None, csc1, w["pre1_g"], row_off=n, tr=tr_s, name="in_bwd_ctx")

    small = dict(
        pre1_g=dpre1 + dpre1_c, post1_g=dpost1, pre2_g=dpre2, post2_g=dpost2,
        w_dec_f=(dwdf + cwdf)[:lr], w_dec_b=(dwdb + cwdb)[lr:2 * lr], b_dec_f=dbdf + cbdf, b_dec_b=dbdb + cbdb,
        gla_norm_g=dgla_g, sg_ln_g=dlng, sg_ln_b=dlnb, w_s=dws,
        b_s=dbs_acc.reshape(cfg["SG_C"], cfg["SG_G"], sgw // cfg["SG_G"]).sum(-1).T)
    dmod = jnp.concatenate([dsh1, dsc1, dg1, dsh2, dsc2, dg2], axis=1)
    dmod_c = jnp.concatenate([dcsh1, dcsc1], axis=1)
    big = dict(w_in_a=(pa_own, u_a), w_in_b=(pb_own, u_b), w_o=(po_own, u_o), w_1=(p1_own, u_1), w_2=(p2_own, u_2))
    return lossc, grad_x, big, small, dmod, dmod_c


SMALL_NAMES = ["b_ada", "pre1_g", "post1_g", "pre2_g", "post2_g", "w_dec_f", "b_dec_f", "w_dec_b", "b_dec_b",
               "gla_norm_g", "sg_ln_g", "sg_ln_b", "w_s", "b_s", "c_ctx"]
WEIGHT_ORDER = ["c_ctx", "w_ada", "b_ada", "pre1_g", "post1_g", "pre2_g", "post2_g", "w_in", "w_dec_f", "b_dec_f",
                "w_dec_b", "b_dec_b", "gla_norm_g", "sg_ln_g", "sg_ln_b", "w_s", "b_s", "w_o", "w_1", "w_2"]


def kernel(x, c, ctx, c_ctx, w_ada, b_ada, pre1_g, post1_g, pre2_g, post2_g, w_in, w_dec_f, b_dec_f, w_dec_b, b_dec_b, gla_norm_g, sg_ln_g, sg_ln_b, w_s, b_s, w_o, w_1, w_2, loss_target, m_c_ctx, m_w_ada, m_b_ada, m_pre1_g, m_post1_g, m_pre2_g, m_post2_g, m_w_in, m_w_dec_f, m_b_dec_f, m_w_dec_b, m_b_dec_b, m_gla_norm_g, m_sg_ln_g, m_sg_ln_b, m_w_s, m_b_s, m_w_o, m_w_1, m_w_2, v_c_ctx, v_w_ada, v_b_ada, v_pre1_g, v_post1_g, v_pre2_g, v_post2_g, v_w_in, v_w_dec_f, v_b_dec_f, v_w_dec_b, v_b_dec_b, v_gla_norm_g, v_sg_ln_g, v_sg_ln_b, v_w_s, v_b_s, v_w_o, v_w_1, v_w_2):
    weights = dict(c_ctx=c_ctx, w_ada=w_ada, b_ada=b_ada, pre1_g=pre1_g, post1_g=post1_g, pre2_g=pre2_g,
                   post2_g=post2_g, w_in=w_in, w_dec_f=w_dec_f, b_dec_f=b_dec_f, w_dec_b=w_dec_b, b_dec_b=b_dec_b,
                   gla_norm_g=gla_norm_g, sg_ln_g=sg_ln_g, sg_ln_b=sg_ln_b, w_s=w_s, b_s=b_s, w_o=w_o, w_1=w_1,
                   w_2=w_2)
    mom_m = dict(c_ctx=m_c_ctx, w_ada=m_w_ada, b_ada=m_b_ada, pre1_g=m_pre1_g, post1_g=m_post1_g, pre2_g=m_pre2_g,
                 post2_g=m_post2_g, w_in=m_w_in, w_dec_f=m_w_dec_f, b_dec_f=m_b_dec_f, w_dec_b=m_w_dec_b,
                 b_dec_b=m_b_dec_b, gla_norm_g=m_gla_norm_g, sg_ln_g=m_sg_ln_g, sg_ln_b=m_sg_ln_b, w_s=m_w_s,
                 b_s=m_b_s, w_o=m_w_o, w_1=m_w_1, w_2=m_w_2)
    mom_v = dict(c_ctx=v_c_ctx, w_ada=v_w_ada, b_ada=v_b_ada, pre1_g=v_pre1_g, post1_g=v_post1_g, pre2_g=v_pre2_g,
                 post2_g=v_post2_g, w_in=v_w_in, w_dec_f=v_w_dec_f, b_dec_f=v_b_dec_f, w_dec_b=v_w_dec_b,
                 b_dec_b=v_b_dec_b, gla_norm_g=v_gla_norm_g, sg_ln_g=v_sg_ln_g, sg_ln_b=v_sg_ln_b, w_s=v_w_s,
                 b_s=v_b_s, w_o=v_w_o, w_1=v_w_1, w_2=v_w_2)

    cfg = _config(x, ctx, w_in, w_dec_f, gla_norm_g, sg_ln_g, w_s)
    n, d, h, dv, kw, vw, lr, sgw = (cfg[k] for k in ("N", "D", "H", "DV", "KW", "VW", "LR", "SGW"))
    dvs, kws = dv // N_DEV, kw // N_DEV
    ix, iy, ic = _my_pos()
    me = 4 * ix + 2 * iy + ic

    pack1 = jnp.concatenate([c.reshape(1, d), w_dec_f.reshape(1, lr * kws), w_dec_b.reshape(1, lr * kws),
                             gla_norm_g.reshape(1, h * dvs)], axis=1)
    g1 = _all_gather_vec(pack1, name="ag_small_in")
    c_all = g1[:, :d]
    o1 = d
    wdf = g1[:, o1:o1 + lr * kws].reshape(N_DEV, lr, kws).transpose(1, 0, 2).reshape(lr, kw)
    o1 += lr * kws
    wdb = g1[:, o1:o1 + lr * kws].reshape(N_DEV, lr, kws).transpose(1, 0, 2).reshape(lr, kw)
    o1 += lr * kws
    gla_g = g1[:, o1:o1 + h * dvs].reshape(N_DEV, h, dvs).transpose(1, 0, 2).reshape(1, h * dv)

    c16 = jnp.concatenate([c_all, jnp.broadcast_to(c_ctx.reshape(1, d), (N_DEV, d))], axis=0)
    ncol = w_ada.shape[2]
    wa = w_ada.reshape(d, ncol)
    b_mine = lax.dynamic_slice(b_ada, (0, me * ncol), (1, ncol))
    tn_ada = _tile(ncol, 512, LANES)
    mod_mine = _ada_fwd(c16, wa, b_mine, tn=tn_ada)
    mod_all = _all_gather_small(mod_mine, name="ag_mod").transpose(1, 0, 2).reshape(16, N_DEV * ncol)
    mod_b = lax.dynamic_slice(mod_all, (me, 0), (1, 6 * d))
    mods = [mod_b[:, i * d:(i + 1) * d] for i in range(6)]
    c_mods = [mod_all[N_DEV:N_DEV + 1, :d], mod_all[N_DEV:N_DEV + 1, d:2 * d]]

    zpad = lambda r: jnp.zeros((r, kw), F32)
    w = dict(
        sh_in=w_in.reshape(d, w_in.shape[2]).astype(BF16), sh_o=w_o.reshape(w_o.shape[1], d).astype(BF16),
        sh_1=w_1.reshape(d, w_1.shape[2]).astype(BF16), sh_2=w_2.reshape(w_2.shape[1], d).astype(BF16),
        pre1_g=pre1_g, post1_g=post1_g, pre2_g=pre2_g, post2_g=post2_g, b_dec_f=b_dec_f, b_dec_b=b_dec_b,
        wdf_pad=jnp.concatenate([wdf, zpad(LANES - lr)], axis=0),
        wdb_pad=jnp.concatenate([zpad(lr), wdb, zpad(LANES - 2 * lr)], axis=0),
        gla_g=gla_g, sg_ln_g=sg_ln_g, sg_ln_b=sg_ln_b, w_s=w_s[0],
        bs_full=jnp.repeat(b_s[0].T, sgw // cfg["SG_G"], axis=1))

    lossc, grad_x, big, small, dmod, dmod_c = _local_step(x[0], ctx[0], loss_target[0], mods, c_mods, w, cfg)
    loss = lax.psum(jnp.sum(lossc), AXES)

    order3 = ["pre1_g", "post1_g", "pre2_g", "post2_g", "w_dec_f", "b_dec_f", "w_dec_b", "b_dec_b", "gla_norm_g",
              "sg_ln_g", "sg_ln_b", "w_s", "b_s"]
    pieces = [dmod, dmod_c] + [small[k].reshape(1, -1) for k in order3]
    sizes = [p.shape[1] for p in pieces]
    g3 = _all_gather_vec(jnp.concatenate(pieces, axis=1), name="ag_small_grads")
    offs = [0]
    for s in sizes:
        offs.append(offs[-1] + s)
    dmod_all = g3[:, :6 * d]
    dmod_c_all = jnp.pad(g3[:, offs[1]:offs[2]], ((0, 0), (0, 4 * d)))
    parts8 = {k: g3[:, offs[2 + i]:offs[3 + i]] for i, k in enumerate(order3)}
    parts8["b_ada"] = dmod_all + dmod_c_all
    parts8["w_dec_f"] = lax.dynamic_slice(parts8["w_dec_f"].reshape(N_DEV, lr, kw), (0, 0, me * kws),
                                          (N_DEV, lr, kws)).reshape(N_DEV, -1)
    parts8["w_dec_b"] = lax.dynamic_slice(parts8["w_dec_b"].reshape(N_DEV, lr, kw), (0, 0, me * kws),
                                          (N_DEV, lr, kws)).reshape(N_DEV, -1)
    parts8["gla_norm_g"] = lax.dynamic_slice(parts8["gla_norm_g"].reshape(N_DEV, h, dv), (0, 0, me * dvs),
                                             (N_DEV, h, dvs)).reshape(N_DEV, -1)

    dm16 = jnp.concatenate([dmod_all, dmod_c_all], axis=0)
    dm_mine = lax.dynamic_slice(dm16, (0, me * ncol), (16, ncol))
    g_w_ada, dcc = _ada_bwd(c16, dm_mine, wa, c_ctx.reshape(1, d), tn=tn_ada)
    parts8["c_ctx"] = _all_gather_vec(dcc, name="ag_cctx")

    flat = lambda t: t.reshape(1, -1)
    g8 = _dense(jnp.concatenate([parts8[k] for k in SMALL_NAMES], axis=1))
    ws, ms, vs = [_dense(jnp.concatenate([flat(src[k]) for k in SMALL_NAMES], axis=1))[0]
                  for src in (weights, mom_m, mom_v)]
    res_small = [r.reshape(1, -1) for r in _adam_small(g8, ws, ms, vs)]
    out = {}
    off = 0
    for k in SMALL_NAMES:
        sz = weights[k].size
        out[k] = [r[:, off:off + sz].reshape(weights[k].shape) for r in res_small]
        off += sz

    rows_for = lambda r, cols: _tile(r, max(8, ELEMS_PER_BLOCK // cols), 16)
    r8, cs_in = d // N_DEV, w_in.shape[2]

    def adam(nm, parts, sides=_NO_SIDES):
        shp = weights[nm].shape
        r2 = (shp[1], shp[2])
        res = _adam_big(parts, weights[nm].reshape(r2), mom_m[nm].reshape(r2), mom_v[nm].reshape(r2),
                        name="adam_" + nm, tr=rows_for(*r2), sides=sides)
        out[nm] = [r.reshape(shp) for r in res[:4]]
        return res[4:]

    four = lambda own, u: [(own, 0), (u, 0), (u, 1), (u, 2)]
    red_a = _sum_parts(four(*big["w_in_a"]), r8, cfg["ZA"], name="rs_sum_w_in_a", tr=rows_for(r8, cfg["ZA"]))
    red_b = _sum_parts(four(*big["w_in_b"]), r8, cfg["ZB"], name="rs_sum_w_in_b", tr=rows_for(r8, cfg["ZB"]))
    red = jnp.concatenate([red_a[:, :cfg["L0"] + 2 * lr], red_b], axis=1)
    (g_in,) = adam("w_ada", [(g_w_ada, None)],
                   _Sides([_a2a_job(red.reshape(r8, N_DEV, cs_in).transpose(1, 0, 2))]))
    for nm in ("w_2", "w_1", "w_o"):
        adam(nm, four(*big[nm]))
    adam("w_in", [(g_in.reshape(d, cs_in), None)])

    outs = [loss, grad_x[None]]
    for i in range(4):
        outs += [out[k][i] for k in WEIGHT_ORDER]
    return tuple(outs)
```

```python
import math

import jax
import jax.numpy as jnp
from jax import lax
from jax.experimental import pallas as pl
from jax.experimental.pallas import tpu as pltpu

F32 = jnp.float32
BF16 = jnp.bfloat16
MXU_DTYPE = jnp.bfloat16
HI = lax.Precision.HIGHEST

N_DEV = 8
AXES = ("x", "y", "c")
MESH = pl.DeviceIdType.MESH
LANES = 128
VMEM_LIMIT = 56 * 1024 * 1024

EPS = 1e-6
GRID_W = 64
GLA_CHUNK = 64
GLA_TAU = 16.0
ROPE_BASE = 10000.0
ADAM_LR = 0.001
ADAM_B1 = 0.9
ADAM_B2 = 0.999
ADAM_EPS = 1e-08
ADAM_WD = 0.01
ADAM_STEP = 10


def _cparams(sem):
    return pltpu.CompilerParams(dimension_semantics=sem, vmem_limit_bytes=VMEM_LIMIT)


def _tile(n, target, align):
    if n <= target:
        return n
    best = None
    for t in range(align, target + 1, align):
        if n % t == 0:
            best = t
    assert best is not None, (n, target, align)
    return best


def _dg(a, b, dims, prec=None):
    return lax.dot_general(a, b, (dims, ((), ())), precision=prec, preferred_element_type=F32)


def _nn(a, b):
    return _dg(a.astype(MXU_DTYPE), b.astype(MXU_DTYPE), ((1,), (0,)))


def _nt(a, b):
    return _dg(a.astype(MXU_DTYPE), b.astype(MXU_DTYPE), ((1,), (1,)))


def _tn(a, b):
    return _dg(a.astype(MXU_DTYPE), b.astype(MXU_DTYPE), ((0,), (0,)))


def _sigmoid(x):
    return 1.0 / (1.0 + jnp.exp(-x))


def _silu(x):
    return x * _sigmoid(x)


def _dsilu(x):
    s = _sigmoid(x)
    return s * (1.0 + x * (1.0 - s))


def _gelu(x):
    return 0.5 * x * (1.0 + lax.erf(x * (1.0 / math.sqrt(2.0))))


def _dgelu(x):
    return 0.5 * (1.0 + lax.erf(x * (1.0 / math.sqrt(2.0)))) + x * jnp.exp(-0.5 * x * x) * (1.0 / math.sqrt(2.0 * math.pi))


def _rstd(x):
    return lax.rsqrt(jnp.mean(x * x, axis=-1, keepdims=True) + EPS)


def _rms_bwd(x, r, dn):
    return r * dn - x * (r * r * r) * jnp.mean(dn * x, axis=-1, keepdims=True)


def _colsum(x):
    return jnp.sum(x, axis=0, keepdims=True)


class _Sides:
    def __init__(self, jobs):
        self.jobs = list(jobs)
        self.ins = [a for j in self.jobs for a in j["ins"]]
        self.outs = [o for j in self.jobs for o in j["outs"]]
        self.sems = [s for j in self.jobs for s in j["sems"]]

    def aliases(self, in_base, out_base):
        res, oi, oo = {}, 0, 0
        for j in self.jobs:
            for a, b in j.get("alias", {}).items():
                res[in_base + oi + a] = out_base + oo + b
            oi += len(j["ins"])
            oo += len(j["outs"])
        return res

    def has(self, phase):
        return any(phase in j for j in self.jobs)

    def run(self, phase, in_refs, out_refs, sem_refs):
        oi = oo = os_ = 0
        for j in self.jobs:
            ni, no, ns = len(j["ins"]), len(j["outs"]), len(j["sems"])
            if phase in j:
                j[phase](in_refs[oi:oi + ni], out_refs[oo:oo + no], sem_refs[os_:os_ + ns])
            oi, oo, os_ = oi + ni, oo + no, os_ + ns


_NO_SIDES = _Sides([])
ANY_SPEC = pl.BlockSpec(memory_space=pl.ANY)


def _matmul(a, b, mode, m, n, k, *, tm, tn, tk, name, out_shapes, b_spec=None, out_specs=None,
            epilogue=None, extras=(), sides=_NO_SIDES):
    nk = k // tk
    assert m % tm == 0 and n % tn == 0 and k % tk == 0, (name, m, n, k, tm, tn, tk)
    dot = {"nn": _nn, "nt": _nt, "tn": _tn}[mode]
    if mode == "tn":
        a_spec = pl.BlockSpec((tk, tm), lambda i, j, kk: (kk, i))
    else:
        a_spec = pl.BlockSpec((tm, tk), lambda i, j, kk: (i, kk))
    if b_spec is None:
        if mode == "nt":
            b_spec = pl.BlockSpec((tn, tk), lambda i, j, kk: (j, kk))
        else:
            b_spec = pl.BlockSpec((tk, tn), lambda i, j, kk: (kk, j))
    mn_spec = pl.BlockSpec((tm, tn), lambda i, j, kk: (i, j))
    if out_specs is None:
        out_specs = [mn_spec] * len(out_shapes)
    n_extra = len(extras)
    n_out = len(out_shapes)
    n_si, n_so = len(sides.ins), len(sides.outs)
    ni, nj = m // tm, n // tn

    def body(a_ref, b_ref, *rest):
        extra_refs = rest[:n_extra]
        rest = rest[n_extra:]
        side_in, rest = rest[:n_si], rest[n_si:]
        out_refs, rest = rest[:n_out], rest[n_out:]
        side_out, rest = rest[:n_so], rest[n_so:]
        acc, side_sems = rest[0], rest[1:]
        i, j, kk = pl.program_id(0), pl.program_id(1), pl.program_id(2)

        if sides.jobs:
            @pl.when((i == 0) & (j == 0) & (kk == 0))
            def _():
                sides.run("start", side_in, side_out, side_sems)

        if sides.has("middle"):
            mid = (ni * nj * nk) // 2
            mi, mj, mk = mid // (nj * nk), (mid // nk) % nj, mid % nk

            @pl.when((i == mi) & (j == mj) & (kk == mk))
            def _():
                sides.run("middle", side_in, side_out, side_sems)

        @pl.when(kk == 0)
        def _():
            acc[...] = jnp.zeros_like(acc)

        acc[...] += dot(a_ref[...], b_ref[...])

        @pl.when(kk == nk - 1)
        def _():
            vals = (acc[...],) if epilogue is None else epilogue(acc[...], *[e[...] for e in extra_refs])
            for o, v in zip(out_refs, vals):
                o[...] = v.astype(o.dtype)

        if sides.jobs:
            @pl.when((i == ni - 1) & (j == nj - 1) & (kk == nk - 1))
            def _():
                sides.run("finish", side_in, side_out, side_sems)

    sem = ("arbitrary",) * 3 if sides.jobs else ("parallel", "parallel", "arbitrary")
    res = pl.pallas_call(
        body, name=name, grid=(ni, nj, nk),
        in_specs=[a_spec, b_spec] + [mn_spec] * n_extra + [ANY_SPEC] * n_si,
        out_specs=list(out_specs) + [ANY_SPEC] * n_so, out_shape=list(out_shapes) + list(sides.outs),
        scratch_shapes=[pltpu.VMEM((tm, tn), F32)] + list(sides.sems),
        input_output_aliases=sides.aliases(2 + n_extra, n_out),
        compiler_params=_cparams(sem),
    )(a, b, *extras, *sides.ins)
    return res


def _blocked_b_nn(ns, tk, tn):
    assert ns % tn == 0
    return pl.BlockSpec((None, tk, tn), lambda i, j, kk: ((j * tn) // ns, kk, ((j * tn) % ns) // tn))


def _blocked_b_nt(ks, tn, tk):
    assert ks % tk == 0
    return pl.BlockSpec((None, tn, tk), lambda i, j, kk: ((kk * tk) // ks, j, ((kk * tk) % ks) // tk))


def _blocked_out(ns, tm, tn):
    assert ns % tn == 0
    return pl.BlockSpec((None, tm, tn), lambda i, j, kk: ((j * tn) // ns, i, ((j * tn) % ns) // tn))


def _rows_call(body, *, name, nblk, tr, row_ins, consts, row_outs, accs=(), sides=_NO_SIDES):
    n_ri, n_c, n_ro, n_acc = len(row_ins), len(consts), len(row_outs), len(accs)
    n_si, n_so = len(sides.ins), len(sides.outs)

    def kern(*refs):
        i = pl.program_id(0)
        rin, refs = refs[:n_ri], refs[n_ri:]
        cin, refs = refs[:n_c], refs[n_c:]
        side_in, refs = refs[:n_si], refs[n_si:]
        rout, refs = refs[:n_ro], refs[n_ro:]
        acc, refs = refs[:n_acc], refs[n_acc:]
        side_out, side_sems = refs[:n_so], refs[n_so:]

        if sides.jobs:
            @pl.when(i == 0)
            def _():
                sides.run("start", side_in, side_out, side_sems)

        if n_acc:
            @pl.when(i == 0)
            def _():
                for r in acc:
                    r[...] = jnp.zeros_like(r)

        body(rin, cin, rout, acc)

        if sides.jobs:
            @pl.when(i == nblk - 1)
            def _():
                sides.run("finish", side_in, side_out, side_sems)

    in_specs = [pl.BlockSpec((tr, w), lambda i, ro=ro, co=co: (i + ro, co)) for (_, w, ro, co) in row_ins]
    in_specs += [pl.BlockSpec(cst.shape, lambda i, nd=cst.ndim: (0,) * nd) for cst in consts]
    out_specs = [pl.BlockSpec((tr, w), lambda i: (i, 0)) for (_, w, _) in row_outs]
    out_specs += [pl.BlockSpec(s, lambda i, nd=len(s): (0,) * nd) for s in accs]
    out_shape = [jax.ShapeDtypeStruct((r, w), dt) for (r, w, dt) in row_outs]
    out_shape += [jax.ShapeDtypeStruct(s, F32) for s in accs]
    return pl.pallas_call(
        kern, name=name, grid=(nblk,), in_specs=in_specs + [ANY_SPEC] * n_si,
        out_specs=out_specs + [ANY_SPEC] * n_so, out_shape=out_shape + list(sides.outs),
        scratch_shapes=list(sides.sems),
        input_output_aliases=sides.aliases(n_ri + n_c, n_ro + n_acc),
        compiler_params=_cparams(("arbitrary",)),
    )(*[r[0] for r in row_ins], *consts, *sides.ins)


def _norm_mod(x, g, shift, scale, *, name, tr):
    rows, d = x.shape

    def body(rin, cin, rout, acc):
        xv = rin[0][...]
        n = xv * _rstd(xv) * cin[0][...]
        rout[0][...] = (n * (1.0 + cin[2][...]) + cin[1][...]).astype(BF16)

    return _rows_call(body, name=name, nblk=rows // tr, tr=tr, row_ins=[(x, d, 0, 0)],
                      consts=[g, shift, scale], row_outs=[(rows, d, BF16)])[0]


def _swap_halves(t, width):
    lane = lax.broadcasted_iota(jnp.int32, t.shape, 1)
    return jnp.where(lane % 64 < 32, pltpu.roll(t, width - 32, 1), pltpu.roll(t, 32, 1))


def _gla_prep(z, tabs, wdf_pad, wdb_pad, bdf, bdb, cfg, *, tr, sides=_NO_SIDES):
    rows = z.shape[0]
    kw, h = cfg["KW"], cfg["H"]

    def body(rin, cin, rout, acc):
        zq, zk, zl = rin[0][...], rin[1][...], rin[2][...]
        cq, sq, ck, sk = [jnp.concatenate([rin[3 + t][...]] * h, axis=1) for t in range(4)]
        rout[0][...] = zq * cq + _swap_halves(zq, kw) * sq
        rout[1][...] = zk * ck + _swap_halves(zk, kw) * sk
        for o, w, b in ((2, cin[0], cin[2]), (3, cin[1], cin[3])):
            a = _nn(zl, w[...]) + b[...]
            rout[o][...] = (jnp.minimum(a, 0.0) - jnp.log(1.0 + jnp.exp(-jnp.abs(a)))) * (1.0 / GLA_TAU)

    row_ins = [(z, kw, 0, 0), (z, kw, 0, 1), (z, LANES, 0, cfg["L0"] // LANES)]
    row_ins += [(t, LANES, 0, 0) for t in tabs]
    return _rows_call(body, name="gla_prep", nblk=rows // tr, tr=tr, row_ins=row_ins,
                      consts=[wdf_pad, wdb_pad, bdf, bdb], row_outs=[(rows, kw, F32)] * 4, sides=sides)


def _chunk_consts(rev):
    c = GLA_CHUNK
    r = lax.broadcasted_iota(jnp.int32, (c, c), 0)
    cc = lax.broadcasted_iota(jnp.int32, (c, c), 1)
    keep = (cc >= r) if rev else (cc <= r)
    return keep, keep.astype(F32)


def _heads_per_step(cfg):
    hb = 4 if cfg["H"] % 4 == 0 else (2 if cfg["H"] % 2 == 0 else 1)
    assert cfg["V0"] % (hb * cfg["DV"]) == 0
    return hb


def _chunk_decay(la, keep_f):
    b = _dg(keep_f, la, ((1,), (0,)), HI)
    return b, _colsum(la)


def _gla_fwd(qr, kr, z, la, st0, cfg, *, rev, row_off, nrows, tb, name, sides=_NO_SIDES):
    h, dk, dv = cfg["H"], cfg["DK"], cfg["DV"]
    c = GLA_CHUNK
    nsub = tb // c
    nblk = nrows // tb
    roff = row_off // tb
    hb = _heads_per_step(cfg)
    v_cb = cfg["V0"] // (hb * dv)
    n_si, n_so = len(sides.ins), len(sides.outs)

    def blk(j):
        return (nblk - 1 - j) if rev else j

    def body(q_ref, k_ref, v_ref, la_ref, st0_ref, *rest):
        side_in, rest = rest[:n_si], rest[n_si:]
        o_ref, save_ref, fin_ref = rest[:3]
        side_out, st, side_sems = rest[3:3 + n_so], rest[3 + n_so], rest[4 + n_so:]
        hh, j = pl.program_id(0), pl.program_id(1)

        if sides.jobs:
            @pl.when((hh == 0) & (j == 0))
            def _():
                sides.run("start", side_in, side_out, side_sems)

        @pl.when(j == 0)
        def _():
            st[...] = st0_ref[...]

        keep, keep_f = _chunk_consts(rev)
        order = range(nsub - 1, -1, -1) if rev else range(nsub)
        heads = range(hb)
        ksl = [slice(g * dk, (g + 1) * dk) for g in heads]
        vsl = [slice(g * dv, (g + 1) * dv) for g in heads]
        state = [st[g] for g in heads]
        for s in order:
            rs = pl.ds(s * c, c)
            q = [q_ref[rs, ksl[g]] for g in heads]
            k = [k_ref[rs, ksl[g]] for g in heads]
            v = [v_ref[rs, vsl[g]] for g in heads]
            bb = [_chunk_decay(la_ref[rs, ksl[g]], keep_f) for g in heads]
            qe = [q[g] * jnp.exp(bb[g][0]) for g in heads]
            ke = [k[g] * jnp.exp(-bb[g][0]) for g in heads]
            kl = [k[g] * jnp.exp(bb[g][1] - bb[g][0]) for g in heads]
            att = [jnp.where(keep, _nt(qe[g], ke[g]), 0.0) for g in heads]
            out = [_nt(qe[g], state[g]) + _nn(att[g], v[g]) for g in heads]
            new = [state[g] * jnp.exp(bb[g][1]) + _tn(v[g], kl[g]) for g in heads]
            for g in heads:
                save_ref[g, s] = state[g]
                o_ref[rs, vsl[g]] = out[g]
            state = new
        for g in heads:
            st[g] = state[g]

        @pl.when(j == nblk - 1)
        def _():
            fin_ref[...] = st[...]

        if sides.jobs:
            @pl.when((hh == h // hb - 1) & (j == nblk - 1))
            def _():
                sides.run("finish", side_in, side_out, side_sems)

    in_specs = [
        pl.BlockSpec((tb, hb * dk), lambda hh, j: (roff + blk(j), hh)),
        pl.BlockSpec((tb, hb * dk), lambda hh, j: (roff + blk(j), hh)),
        pl.BlockSpec((tb, hb * dv), lambda hh, j: (roff + blk(j), v_cb + hh)),
        pl.BlockSpec((tb, hb * dk), lambda hh, j: (roff + blk(j), hh)),
        pl.BlockSpec((hb, dv, dk), lambda hh, j: (hh, 0, 0)),
    ]
    out_specs = [
        pl.BlockSpec((tb, hb * dv), lambda hh, j: (blk(j), hh)),
        pl.BlockSpec((hb, nsub, dv, dk), lambda hh, j: (hh, blk(j), 0, 0)),
        pl.BlockSpec((hb, dv, dk), lambda hh, j: (hh, 0, 0)),
    ]
    out_shape = [
        jax.ShapeDtypeStruct((nrows, h * dv), F32),
        jax.ShapeDtypeStruct((h, nrows // c, dv, dk), F32),
        jax.ShapeDtypeStruct((h, dv, dk), F32),
    ]
    return pl.pallas_call(
        body, name=name, grid=(h // hb, nblk), in_specs=in_specs + [ANY_SPEC] * n_si,
        out_specs=out_specs + [ANY_SPEC] * n_so, out_shape=out_shape + list(sides.outs),
        scratch_shapes=[pltpu.VMEM((hb, dv, dk), F32)] + list(sides.sems),
        input_output_aliases=sides.aliases(5, 3),
        compiler_params=_cparams(("arbitrary", "arbitrary")),
    )(qr, kr, z, la, st0, *sides.ins)


def _gla_bwd(qr, kr, z, la, do, save, dst_init, cfg, *, rev, row_off, nrows, tb, name, sides=_NO_SIDES):
    h, dk, dv = cfg["H"], cfg["DK"], cfg["DV"]
    c = GLA_CHUNK
    nsub = tb // c
    nblk = nrows // tb
    roff = row_off // tb
    hb = _heads_per_step(cfg)
    v_cb = cfg["V0"] // (hb * dv)
    n_si, n_so = len(sides.ins), len(sides.outs)

    def blk(j):
        return j if rev else (nblk - 1 - j)

    def body(q_ref, k_ref, v_ref, la_ref, do_ref, save_ref, di_ref, *rest):
        side_in, rest = rest[:n_si], rest[n_si:]
        dq_ref, dk_ref, dv_ref, dla_ref, d0_ref = rest[:5]
        side_out, dst, side_sems = rest[5:5 + n_so], rest[5 + n_so], rest[6 + n_so:]
        hh, j = pl.program_id(0), pl.program_id(1)

        if sides.jobs:
            @pl.when((hh == 0) & (j == 0))
            def _():
                sides.run("start", side_in, side_out, side_sems)

        @pl.when(j == 0)
        def _():
            dst[...] = di_ref[...]

        keep, keep_f = _chunk_consts(rev)
        keep_t = _chunk_consts(not rev)[1]
        order = range(nsub) if rev else range(nsub - 1, -1, -1)
        heads = range(hb)
        ksl = [slice(g * dk, (g + 1) * dk) for g in heads]
        vsl = [slice(g * dv, (g + 1) * dv) for g in heads]
        d_after = [dst[g] for g in heads]
        for s in order:
            rs = pl.ds(s * c, c)
            q = [q_ref[rs, ksl[g]] for g in heads]
            k = [k_ref[rs, ksl[g]] for g in heads]
            v = [v_ref[rs, vsl[g]] for g in heads]
            lac = [la_ref[rs, ksl[g]] for g in heads]
            dout = [do_ref[rs, vsl[g]] for g in heads]
            s_in = [save_ref[g, s] for g in heads]
            bb = [_chunk_decay(lac[g], keep_f) for g in heads]
            eb = [jnp.exp(bb[g][0]) for g in heads]
            enb = [jnp.exp(-bb[g][0]) for g in heads]
            elb = [jnp.exp(bb[g][1] - bb[g][0]) for g in heads]
            etot = [jnp.exp(bb[g][1]) for g in heads]
            qe = [q[g] * eb[g] for g in heads]
            ke = [k[g] * enb[g] for g in heads]
            kl = [k[g] * elb[g] for g in heads]
            att = [jnp.where(keep, _nt(qe[g], ke[g]), 0.0) for g in heads]
            datt = [jnp.where(keep, _nt(dout[g], v[g]), 0.0) for g in heads]
            dqe = [_nn(dout[g], s_in[g]) + _nn(datt[g], ke[g]) for g in heads]
            dke = [_tn(datt[g], qe[g]) for g in heads]
            dkl = [_nn(v[g], d_after[g]) for g in heads]
            dvv = [_tn(att[g], dout[g]) + _nt(kl[g], d_after[g]) for g in heads]
            db = [dqe[g] * qe[g] - dke[g] * ke[g] - dkl[g] * kl[g] for g in heads]
            dbtot = [_colsum(dkl[g] * kl[g]) + _colsum(d_after[g] * s_in[g]) * etot[g] for g in heads]
            dla = [_dg(keep_t, db[g], ((1,), (0,)), HI) + dbtot[g] for g in heads]
            d_after = [d_after[g] * etot[g] + _tn(dout[g], qe[g]) for g in heads]
            for g in heads:
                dv_ref[rs, vsl[g]] = dvv[g]
                dla_ref[rs, ksl[g]] = dla[g]
                dq_ref[rs, ksl[g]] = dqe[g] * eb[g]
                dk_ref[rs, ksl[g]] = dke[g] * enb[g] + dkl[g] * elb[g]
        for g in heads:
            dst[g] = d_after[g]

        @pl.when(j == nblk - 1)
        def _():
            d0_ref[...] = dst[...]

        if sides.jobs:
            @pl.when((hh == h // hb - 1) & (j == nblk - 1))
            def _():
                sides.run("finish", side_in, side_out, side_sems)

    in_specs = [
        pl.BlockSpec((tb, hb * dk), lambda hh, j: (roff + blk(j), hh)),
        pl.BlockSpec((tb, hb * dk), lambda hh, j: (roff + blk(j), hh)),
        pl.BlockSpec((tb, hb * dv), lambda hh, j: (roff + blk(j), v_cb + hh)),
        pl.BlockSpec((tb, hb * dk), lambda hh, j: (roff + blk(j), hh)),
        pl.BlockSpec((tb, hb * dv), lambda hh, j: (blk(j), hh)),
        pl.BlockSpec((hb, nsub, dv, dk), lambda hh, j: (hh, blk(j), 0, 0)),
        pl.BlockSpec((hb, dv, dk), lambda hh, j: (hh, 0, 0)),
    ]
    out_specs = [
        pl.BlockSpec((tb, hb * dk), lambda hh, j: (blk(j), hh)),
        pl.BlockSpec((tb, hb * dk), lambda hh, j: (blk(j), hh)),
        pl.BlockSpec((tb, hb * dv), lambda hh, j: (blk(j), hh)),
        pl.BlockSpec((tb, hb * dk), lambda hh, j: (blk(j), hh)),
        pl.BlockSpec((hb, dv, dk), lambda hh, j: (hh, 0, 0)),
    ]
    out_shape = [
        jax.ShapeDtypeStruct((nrows, h * dk), F32),
        jax.ShapeDtypeStruct((nrows, h * dk), F32),
        jax.ShapeDtypeStruct((nrows, h * dv), F32),
        jax.ShapeDtypeStruct((nrows, h * dk), F32),
        jax.ShapeDtypeStruct((h, dv, dk), F32),
    ]
    return pl.pallas_call(
        body, name=name, grid=(h // hb, nblk), in_specs=in_specs + [ANY_SPEC] * n_si,
        out_specs=out_specs + [ANY_SPEC] * n_so, out_shape=out_shape + list(sides.outs),
        scratch_shapes=[pltpu.VMEM((hb, dv, dk), F32)] + list(sides.sems),
        input_output_aliases=sides.aliases(7, 5),
        compiler_params=_cparams(("arbitrary", "arbitrary")),
    )(qr, kr, z, la, do, save, dst_init, *sides.ins)


def _gla_post(gf, gb, la_f, la_b, z, tabs, wdf_pad, wdb_pad, cfg, *, row_off, nrows, tr, name):
    kw, vw = cfg["KW"], cfg["VW"]
    h = cfg["H"]
    ro = row_off // tr

    def body(rin, cin, rout, acc):
        dq = rin[0][...] + rin[1][...]
        dk_ = rin[2][...] + rin[3][...]
        zl = rin[10][...]
        cq, sq, ck, sk = [jnp.concatenate([rin[11 + t][...]] * h, axis=1) for t in range(4)]
        rout[0][...] = (dq * cq + _swap_halves(dq * sq, kw)).astype(BF16)
        rout[1][...] = (dk_ * ck + _swap_halves(dk_ * sk, kw)).astype(BF16)
        rout[2][...] = (rin[8][...] + rin[9][...]).astype(BF16)
        dzl = jnp.zeros(zl.shape, F32)
        for t, w in ((0, cin[0]), (1, cin[1])):
            la = rin[6 + t][...]
            da = rin[4 + t][...] * ((1.0 - jnp.exp(la * GLA_TAU)) * (1.0 / GLA_TAU))
            dzl = dzl + _nt(da, w[...])
            acc[t][...] += _tn(zl, da)
            acc[2 + t][...] += _colsum(da)
        rout[3][...] = dzl.astype(BF16)

    row_ins = [(gf[0], kw, 0, 0), (gb[0], kw, 0, 0), (gf[1], kw, 0, 0), (gb[1], kw, 0, 0),
               (gf[3], kw, 0, 0), (gb[3], kw, 0, 0), (la_f, kw, ro, 0), (la_b, kw, ro, 0),
               (gf[2], vw, 0, 0), (gb[2], vw, 0, 0), (z, LANES, ro, cfg["L0"] // LANES)]
    row_ins += [(t, LANES, ro, 0) for t in tabs]
    return _rows_call(body, name=name, nblk=nrows // tr, tr=tr, row_ins=row_ins, consts=[wdf_pad, wdb_pad],
                      row_outs=[(nrows, kw, BF16), (nrows, kw, BF16), (nrows, vw, BF16), (nrows, LANES, BF16)],
                      accs=[(LANES, kw), (LANES, kw), (1, kw), (1, kw)])


def _readout_fwd(o_f, o_b, z, g, cfg, *, tr, sides=_NO_SIDES):
    n, vw = o_f.shape
    h, dv = cfg["H"], cfg["DV"]

    def body(rin, cin, rout, acc):
        for hh in range(h):
            cs = slice(hh * dv, (hh + 1) * dv)
            oh = rin[0][:, cs] + rin[1][:, cs]
            y = oh * _rstd(oh) * cin[0][:, cs]
            rout[0][:, cs] = (y * _silu(rin[2][:, cs])).astype(BF16)

    return _rows_call(body, name="gla_readout", nblk=n // tr, tr=tr,
                      row_ins=[(o_f, vw, 0, 0), (o_b, vw, 0, 0), (z, vw, 0, cfg["R0"] // vw)], consts=[g],
                      row_outs=[(n, vw, BF16)], sides=sides)


def _readout_bwd(o_f, o_b, z, dycat, g, cfg, *, tr):
    n, vw = o_f.shape
    h, dv = cfg["H"], cfg["DV"]

    def body(rin, cin, rout, acc):
        for hh in range(h):
            cs = slice(hh * dv, (hh + 1) * dv)
            oh = rin[0][:, cs] + rin[1][:, cs]
            r, dyg, gh = rin[2][:, cs], rin[3][:, cs], cin[0][:, cs]
            rs = _rstd(oh)
            dy = dyg * _silu(r)
            rout[0][:, cs] = _rms_bwd(oh, rs, dy * gh).astype(BF16)
            rout[1][:, cs] = (dyg * (oh * rs * gh) * _dsilu(r)).astype(BF16)
            acc[0][:, cs] += _colsum(dy * oh * rs)

    return _rows_call(body, name="gla_readout_bwd", nblk=n // tr, tr=tr,
                      row_ins=[(o_f, vw, 0, 0), (o_b, vw, 0, 0), (z, vw, 0, cfg["R0"] // vw), (dycat, vw, 0, 0)],
                      consts=[g], row_outs=[(n, vw, BF16), (n, vw, BF16)], accs=[(1, vw)])


def _sg_ln(vv):
    mu = jnp.mean(vv, axis=-1, keepdims=True)
    cen = vv - mu
    rstd = lax.rsqrt(jnp.mean(cen * cen, axis=-1, keepdims=True) + EPS)
    return cen * rstd, rstd


def _sg_fwd(z, n, lng, lnb, w_s, bs_full, cfg, *, sides=_NO_SIDES):
    sgw, grp, sc = cfg["SGW"], cfg["SG_G"], cfg["SG_C"]
    gw = sgw // grp

    def body(rin, cin, rout, acc):
        u = _gelu(rin[0][...])
        xhat, _ = _sg_ln(_gelu(rin[1][...]))
        vvn = xhat * cin[0][...] + cin[1][...]
        for gg in range(grp):
            cs = slice(gg * gw, (gg + 1) * gw)
            s = _nn(cin[2][gg], vvn[:, cs]) + cin[3][:, cs]
            rout[0][:, cs] = (u[:, cs] * s).astype(BF16)

    return _rows_call(body, name="sg_fwd", nblk=n // sc, tr=sc,
                      row_ins=[(z, sgw, 0, cfg["U0"] // sgw), (z, sgw, 0, cfg["VV0"] // sgw)],
                      consts=[lng, lnb, w_s, bs_full], row_outs=[(n, sgw, BF16)], sides=sides)


def _sg_bwd(z, dycat, n, lng, lnb, w_s, bs_full, cfg):
    sgw, grp, sc = cfg["SGW"], cfg["SG_G"], cfg["SG_C"]
    gw = sgw // grp

    def body(rin, cin, rout, acc):
        up, vp, dy = rin[0][...], rin[1][...], rin[2][...]
        u = _gelu(up)
        xhat, rstd = _sg_ln(_gelu(vp))
        lng_v = cin[0][...]
        vvn = xhat * lng_v + cin[1][...]
        ds = dy * u
        acc[1][...] += ds
        dvvn_parts = []
        for gg in range(grp):
            cs = slice(gg * gw, (gg + 1) * gw)
            w = cin[2][gg]
            s = _nn(w, vvn[:, cs]) + cin[3][:, cs]
            rout[0][:, cs] = (dy[:, cs] * s * _dgelu(up[:, cs])).astype(BF16)
            acc[0][gg] += _nt(ds[:, cs], vvn[:, cs])
            dvvn_parts.append(_tn(w, ds[:, cs]))
        dvvn = jnp.concatenate(dvvn_parts, axis=1)
        acc[2][...] += _colsum(dvvn * xhat)
        acc[3][...] += _colsum(dvvn)
        dxh = dvvn * lng_v
        dvv = rstd * (dxh - jnp.mean(dxh, axis=-1, keepdims=True)
                      - xhat * jnp.mean(dxh * xhat, axis=-1, keepdims=True))
        rout[0][:, sgw:] = (dvv * _dgelu(vp)).astype(BF16)

    vw = cfg["VW"]
    return _rows_call(body, name="sg_bwd", nblk=n // sc, tr=sc,
                      row_ins=[(z, sgw, 0, cfg["U0"] // sgw), (z, sgw, 0, cfg["VV0"] // sgw),
                               (dycat, sgw, 0, vw // sgw)],
                      consts=[lng, lnb, w_s, bs_full], row_outs=[(n, 2 * sgw, BF16)],
                      accs=[(grp, sc, sc), (sc, sgw), (1, sgw), (1, sgw)])


def _mid_fwd(x, mix, g1, post1, pre2, sh2, sc2, *, tr, sides=_NO_SIDES):
    n, d = x.shape

    def body(rin, cin, rout, acc):
        xv, mv = rin[0][...], rin[1][...]
        x1 = xv + cin[0][...] * (mv * _rstd(mv) * cin[1][...])
        rout[0][...] = x1
        n2 = x1 * _rstd(x1) * cin[2][...]
        rout[1][...] = (n2 * (1.0 + cin[4][...]) + cin[3][...]).astype(BF16)

    return _rows_call(body, name="mid_fwd", nblk=n // tr, tr=tr, row_ins=[(x, d, 0, 0), (mix, d, 0, 0)],
                      consts=[g1, post1, pre2, sh2, sc2], row_outs=[(n, d, F32), (n, d, BF16)], sides=sides)


def _head_bwd(x1, m2, target, g2, post2, *, tr):
    n, d = x1.shape

    def body(rin, cin, rout, acc):
        x1v, mv, tv = rin[0][...], rin[1][...], rin[2][...]
        g2v, pg = cin[0][...], cin[1][...]
        r = _rstd(mv)
        y2 = mv * r * pg
        err = (x1v + g2v * y2) - tv
        acc[2][...] += _colsum(err * err) * (0.5 / d)
        dx2 = err * (1.0 / d)
        rout[0][...] = dx2
        dy2 = dx2 * g2v
        acc[0][...] += _colsum(dx2 * y2)
        acc[1][...] += _colsum(dy2 * mv * r)
        rout[1][...] = _rms_bwd(mv, r, dy2 * pg).astype(BF16)

    return _rows_call(body, name="head_bwd", nblk=n // tr, tr=tr,
                      row_ins=[(x1, d, 0, 0), (m2, d, 0, 0), (target, d, 0, 0)], consts=[g2, post2],
                      row_outs=[(n, d, F32), (n, d, BF16)], accs=[(1, d)] * 3)


def _mid_bwd(dh2, x1, dx2, mix, sc2, pre2, g1, post1, *, tr, sides=_NO_SIDES):
    n, d = x1.shape

    def body(rin, cin, rout, acc):
        dh, x1v, dx2v, mv = rin[0][...], rin[1][...], rin[2][...], rin[3][...]
        sc2v, pre2v, g1v, post1v = cin[0][...], cin[1][...], cin[2][...], cin[3][...]
        r2 = _rstd(x1v)
        xr = x1v * r2
        acc[0][...] += _colsum(dh)
        acc[1][...] += _colsum(dh * (xr * pre2v))
        dn2 = dh * (1.0 + sc2v)
        acc[2][...] += _colsum(dn2 * xr)
        dx1 = dx2v + _rms_bwd(x1v, r2, dn2 * pre2v)
        rout[0][...] = dx1
        r1 = _rstd(mv)
        mr = mv * r1
        acc[3][...] += _colsum(dx1 * (mr * post1v))
        dy1 = dx1 * g1v
        acc[4][...] += _colsum(dy1 * mr)
        rout[1][...] = _rms_bwd(mv, r1, dy1 * post1v).astype(BF16)

    return _rows_call(body, name="mid_bwd", nblk=n // tr, tr=tr,
                      row_ins=[(dh2, d, 0, 0), (x1, d, 0, 0), (dx2, d, 0, 0), (mix, d, 0, 0)],
                      consts=[sc2, pre2, g1, post1], row_outs=[(n, d, F32), (n, d, BF16)], accs=[(1, d)] * 5,
                      sides=sides)


def _in_bwd(da, x, dres, sc1, pre1, *, row_off, tr, name, sides=_NO_SIDES):
    n, d = x.shape
    with_res = dres is not None

    def body(rin, cin, rout, acc):
        dav, xv = rin[0][...], rin[1][...]
        sc1v, pre1v = cin[0][...], cin[1][...]
        r = _rstd(xv)
        xr = xv * r
        acc[0][...] += _colsum(dav)
        acc[1][...] += _colsum(dav * (xr * pre1v))
        dn = dav * (1.0 + sc1v)
        acc[2][...] += _colsum(dn * xr)
        if with_res:
            rout[0][...] = rin[2][...] + _rms_bwd(xv, r, dn * pre1v)

    row_ins = [(da, d, row_off // tr, 0), (x, d, 0, 0)] + ([(dres, d, 0, 0)] if with_res else [])
    return _rows_call(body, name=name, nblk=n // tr, tr=tr, row_ins=row_ins, consts=[sc1, pre1],
                      row_outs=[(n, d, F32)] if with_res else [], accs=[(1, d)] * 3, sides=sides)


def _ada_fwd(c16, w, b, *, tn):
    d, ncol = w.shape

    def body(c_ref, w_ref, b_ref, o_ref):
        o_ref[...] = _nn(_silu(c_ref[...]), w_ref[...]) + b_ref[...]

    return pl.pallas_call(
        body, name="ada_fwd", grid=(ncol // tn,),
        in_specs=[pl.BlockSpec((16, d), lambda j: (0, 0)), pl.BlockSpec((d, tn), lambda j: (0, j)),
                  pl.BlockSpec((1, tn), lambda j: (0, j))],
        out_specs=pl.BlockSpec((16, tn), lambda j: (0, j)),
        out_shape=jax.ShapeDtypeStruct((16, ncol), F32),
        compiler_params=_cparams(("arbitrary",)),
    )(c16, w, b)


def _ada_bwd(c16, dm, w, c_ctx, *, tn):
    d, ncol = w.shape

    def body(c_ref, dm_ref, w_ref, cc_ref, gw_ref, dcc_ref, acc):
        j = pl.program_id(0)

        @pl.when(j == 0)
        def _():
            acc[...] = jnp.zeros_like(acc)

        gw_ref[...] = _tn(_silu(c_ref[...]), dm_ref[...])
        acc[...] += _nt(dm_ref[...], w_ref[...])

        @pl.when(j == ncol // tn - 1)
        def _():
            dcc_ref[...] = _colsum(acc[8:16, :]) * _dsilu(cc_ref[...])

    return pl.pallas_call(
        body, name="ada_bwd", grid=(ncol // tn,),
        in_specs=[pl.BlockSpec((16, d), lambda j: (0, 0)), pl.BlockSpec((16, tn), lambda j: (0, j)),
                  pl.BlockSpec((d, tn), lambda j: (0, j)), pl.BlockSpec((1, d), lambda j: (0, 0))],
        out_specs=[pl.BlockSpec((d, tn), lambda j: (0, j)), pl.BlockSpec((1, d), lambda j: (0, 0))],
        out_shape=[jax.ShapeDtypeStruct((d, ncol), F32), jax.ShapeDtypeStruct((1, d), F32)],
        scratch_shapes=[pltpu.VMEM((16, d), F32)],
        compiler_params=_cparams(("arbitrary",)),
    )(c16, dm, w, c_ctx)


def _adam_math(w, g, m, v):
    m = ADAM_B1 * m + (1.0 - ADAM_B1) * g
    v = ADAM_B2 * v + (1.0 - ADAM_B2) * (g * g)
    m_hat = m / (1.0 - ADAM_B1 ** ADAM_STEP)
    v_hat = v / (1.0 - ADAM_B2 ** ADAM_STEP)
    delta = -ADAM_LR * (m_hat / (jnp.sqrt(v_hat) + ADAM_EPS) + ADAM_WD * w)
    return delta, m, v


def _adam_big(parts, w, m, v, *, name, tr, sides=_NO_SIDES):
    rows, cols = w.shape
    n_p = len(parts)
    n_si, n_so = len(sides.ins), len(sides.outs)
    nblk = rows // tr

    def body(*refs):
        ins, refs = refs[:n_p + 3], refs[n_p + 3:]
        side_in, refs = refs[:n_si], refs[n_si:]
        outs, side_out, side_sems = refs[:4], refs[4:4 + n_so], refs[4 + n_so:]
        i = pl.program_id(0)

        if sides.jobs:
            @pl.when(i == 0)
            def _():
                sides.run("start", side_in, side_out, side_sems)

        g = ins[0][...]
        for p in ins[1:n_p]:
            g = g + p[...].astype(F32)
        delta, m2, v2 = _adam_math(ins[n_p][...], g, ins[n_p + 1][...], ins[n_p + 2][...])
        outs[0][...] = g
        outs[1][...] = delta
        outs[2][...] = m2
        outs[3][...] = v2

        if sides.jobs:
            @pl.when(i == nblk - 1)
            def _():
                sides.run("finish", side_in, side_out, side_sems)

    plain = pl.BlockSpec((tr, cols), lambda i: (i, 0))
    in_specs = []
    for arr, idx in parts:
        if idx is None:
            in_specs.append(plain)
        else:
            in_specs.append(pl.BlockSpec((None, tr, cols), lambda i, idx=idx: (idx, i, 0)))
    in_specs += [plain] * 3
    return pl.pallas_call(
        body, name=name, grid=(nblk,), in_specs=in_specs + [ANY_SPEC] * n_si,
        out_specs=[plain] * 4 + [ANY_SPEC] * n_so,
        out_shape=[jax.ShapeDtypeStruct((rows, cols), F32)] * 4 + list(sides.outs),
        scratch_shapes=list(sides.sems), input_output_aliases=sides.aliases(n_p + 3, 4),
        compiler_params=_cparams(("arbitrary",) if sides.jobs else ("parallel",)),
    )(*[p[0] for p in parts], w, m, v, *sides.ins)


def _sum_parts(parts, rows, cols, *, name, tr):
    def body(*refs):
        g = refs[0][...].astype(F32)
        for p in refs[1:-1]:
            g = g + p[...].astype(F32)
        refs[-1][...] = g

    in_specs = [pl.BlockSpec((None, tr, cols), lambda i, idx=idx: (idx, i, 0)) for _, idx in parts]
    return pl.pallas_call(
        body, name=name, grid=(rows // tr,), in_specs=in_specs, out_specs=pl.BlockSpec((tr, cols), lambda i: (i, 0)),
        out_shape=jax.ShapeDtypeStruct((rows, cols), F32), compiler_params=_cparams(("parallel",)),
    )(*[p[0] for p in parts])


def _adam_small(g8, w, m, v):
    def body(g_ref, w_ref, m_ref, v_ref, go, do, mo, vo):
        g = g_ref[0]
        for r in range(1, N_DEV):
            g = g + g_ref[r]
        delta, m2, v2 = _adam_math(w_ref[...], g, m_ref[...], v_ref[...])
        go[...] = g
        do[...] = delta
        mo[...] = m2
        vo[...] = v2

    return pl.pallas_call(
        body, name="adam_small", out_shape=[jax.ShapeDtypeStruct(w.shape, F32)] * 4,
        compiler_params=pltpu.CompilerParams(vmem_limit_bytes=VMEM_LIMIT),
    )(g8, w, m, v)


VEC_W = 1024
TK = 2048
ELEMS_PER_BLOCK = 256 * 1024


def _dense(v):
    a, k = v.shape
    kp = -(-k // (8 * VEC_W)) * (8 * VEC_W)
    return jnp.pad(v, ((0, 0), (0, kp - k))).reshape(a, kp // VEC_W, VEC_W)


def _all_gather_vec(v, *, name):
    k = v.shape[1]
    return _all_gather_small(_dense(v)[0], name=name).reshape(N_DEV, -1)[:, :k]


def _my_pos():
    return lax.axis_index("x"), lax.axis_index("y"), lax.axis_index("c")


def _flip(v, bit):
    return (1 - v) if bit else v


def _all_gather_small(v, *, name):
    r, k = v.shape

    def body(v_ref, out_ref, send, recv, lsem):
        x, y, c = _my_pos()
        me = 4 * x + 2 * y + c
        local = pltpu.make_async_copy(v_ref, out_ref.at[me], lsem)
        local.start()
        sends = []
        for kk in range(1, N_DEV):
            peer = (_flip(x, kk & 4), _flip(y, kk & 2), _flip(c, kk & 1))
            cp = pltpu.make_async_remote_copy(src_ref=v_ref, dst_ref=out_ref.at[me], send_sem=send.at[kk - 1],
                                              recv_sem=recv.at[kk - 1], device_id=peer, device_id_type=MESH)
            cp.start()
            sends.append(cp)
        for kk in range(1, N_DEV):
            px, py, pc = _flip(x, kk & 4), _flip(y, kk & 2), _flip(c, kk & 1)
            src = 4 * px + 2 * py + pc
            pltpu.make_async_remote_copy(src_ref=v_ref, dst_ref=out_ref.at[src], send_sem=send.at[kk - 1],
                                         recv_sem=recv.at[kk - 1], device_id=(px, py, pc),
                                         device_id_type=MESH).wait_recv()
        for cp in sends:
            cp.wait_send()
        local.wait()

    return pl.pallas_call(
        body, name=name, out_shape=jax.ShapeDtypeStruct((N_DEV, r, k), v.dtype),
        in_specs=[pl.BlockSpec(memory_space=pltpu.VMEM)], out_specs=pl.BlockSpec(memory_space=pltpu.VMEM),
        scratch_shapes=[pltpu.SemaphoreType.DMA((N_DEV - 1,)), pltpu.SemaphoreType.DMA((N_DEV - 1,)),
                        pltpu.SemaphoreType.DMA],
        compiler_params=pltpu.CompilerParams(vmem_limit_bytes=VMEM_LIMIT),
    )(v)


def _ag_job(shards, rows=None, chained=None):
    n_arr = len(shards)

    def part(ref):
        return ref if rows is None else ref.at[pl.ds(rows[0], rows[1])]

    def tools(ins, outs, sems):
        send, recv, lsem = sems
        x, y, c = _my_pos()
        chips = [(1 - x, y), (x, 1 - y), (1 - x, 1 - y)]

        def copy(a, kk, block, to, src=None):
            dst = part(outs[a].at[4 * block[0] + 2 * block[1] + block[2]])
            return pltpu.make_async_remote_copy(src_ref=dst if src is None else part(src), dst_ref=dst,
                                                send_sem=send.at[a, kk], recv_sem=recv.at[a, kk],
                                                device_id=to, device_id_type=MESH)

        locals_ = [pltpu.make_async_copy(part(ins[a]), part(outs[a].at[4 * x + 2 * y + c]), lsem.at[a])
                   for a in range(n_arr)]
        firsts = []
        for a in range(n_arr):
            firsts.append(copy(a, 0, (x, y, c), (x, y, 1 - c), src=ins[a]))
            firsts += [copy(a, 1 + j, (x, y, c), (*chip, c), src=ins[a]) for j, chip in enumerate(chips)]
        return copy, locals_, firsts, chips, (x, y, c)

    def start(ins, outs, sems):
        _, locals_, firsts, _, _ = tools(ins, outs, sems)
        for cp in locals_ + firsts:
            cp.start()

    def finish(ins, outs, sems):
        copy, locals_, firsts, chips, (x, y, c) = tools(ins, outs, sems)
        me, sibling = (x, y, c), (x, y, 1 - c)
        passed = []
        for a in range(n_arr):
            for j, chip in enumerate(chips):
                copy(a, 1 + j, (*chip, c), me).wait_recv()
                fw = copy(a, 4 + j, (*chip, c), sibling)
                fw.start()
                passed.append(fw)
        for a in range(n_arr):
            copy(a, 0, sibling, me).wait_recv()
            for j, chip in enumerate(chips):
                copy(a, 4 + j, (*chip, 1 - c), me).wait_recv()
        for cp in firsts + passed:
            cp.wait_send()
        for lc in locals_:
            lc.wait()

    job = dict(ins=list(shards), outs=[jax.ShapeDtypeStruct((N_DEV,) + s.shape, s.dtype) for s in shards],
               sems=[pltpu.SemaphoreType.DMA((n_arr, 7)), pltpu.SemaphoreType.DMA((n_arr, 7)),
                     pltpu.SemaphoreType.DMA((n_arr,))], start=start, finish=finish)
    if chained is not None:
        job["ins"] = list(shards) + list(chained)
        job["alias"] = {n_arr + a: a for a in range(n_arr)}
    return job


def _ag3_job(shard, rows=None, chained=None):
    r0, nr = rows if rows is not None else (0, shard.shape[0])
    ha = (nr // 2) // 16 * 16
    halves = [(r0, ha), (r0 + ha, nr - ha)]

    def tools(ins, outs, sems):
        send, recv, lsem = sems
        x, y, c = _my_pos()
        me, sib = (x, y, c), (x, y, 1 - c)
        xn, yn, dg = (1 - x, y, c), (x, 1 - y, c), (1 - x, 1 - y, c)
        other = lambda p: (p[0], p[1], 1 - p[2])

        def blk(p, span=(r0, nr)):
            return outs[0].at[4 * p[0] + 2 * p[1] + p[2]].at[pl.ds(span[0], span[1])]

        mine = ins[0].at[pl.ds(r0, nr)]
        plan = [(mine, blk(me), sib, blk(sib)), (mine, blk(me), xn, blk(xn)), (mine, blk(me), yn, blk(yn)),
                (blk(xn, halves[0]), blk(xn, halves[0]), yn, blk(dg, halves[0])),
                (blk(yn, halves[1]), blk(yn, halves[1]), xn, blk(dg, halves[1])),
                (blk(xn), blk(xn), sib, blk(other(xn))), (blk(yn), blk(yn), sib, blk(other(yn))),
                (blk(dg, halves[0]), blk(dg, halves[0]), sib, blk(other(dg), halves[0])),
                (blk(dg, halves[1]), blk(dg, halves[1]), sib, blk(other(dg), halves[1]))]
        sends = [pltpu.make_async_remote_copy(src_ref=s, dst_ref=d, send_sem=send.at[k], recv_sem=recv.at[k],
                                              device_id=to, device_id_type=MESH)
                 for k, (s, d, to, _) in enumerate(plan)]
        recvs = [pltpu.make_async_remote_copy(src_ref=got, dst_ref=got, send_sem=send.at[k], recv_sem=recv.at[k],
                                              device_id=me, device_id_type=MESH)
                 for k, (_, _, _, got) in enumerate(plan)]
        local = pltpu.make_async_copy(mine, blk(me), lsem)
        return sends, recvs, local

    def start(ins, outs, sems):
        sends, _, local = tools(ins, outs, sems)
        local.start()
        for k in (0, 1, 2):
            sends[k].start()

    def middle(ins, outs, sems):
        sends, recvs, _ = tools(ins, outs, sems)
        recvs[1].wait_recv()
        sends[3].start()
        sends[5].start()
        recvs[2].wait_recv()
        sends[4].start()
        sends[6].start()

    def finish(ins, outs, sems):
        sends, recvs, local = tools(ins, outs, sems)
        recvs[3].wait_recv()
        sends[7].start()
        recvs[4].wait_recv()
        sends[8].start()
        for k in (0, 5, 6, 7, 8):
            recvs[k].wait_recv()
        for cp in sends:
            cp.wait_send()
        local.wait()

    job = dict(ins=[shard], outs=[jax.ShapeDtypeStruct((N_DEV,) + shard.shape, shard.dtype)],
               sems=[pltpu.SemaphoreType.DMA((9,)), pltpu.SemaphoreType.DMA((9,)), pltpu.SemaphoreType.DMA],
               start=start, middle=middle, finish=finish)
    if chained is not None:
        job["ins"] = [shard] + list(chained)
        job["alias"] = {1: 0}
    return job


def _exchange_job(arrays, n_slots, out_slots, src_of, dst_of, peer_of, rows=None, chained=None):
    n_arr = len(arrays)

    def copies(ins, outs, sems):
        send, recv = sems
        x, y, c = _my_pos()
        res = []
        for a in range(n_arr):
            for s in range(n_slots):
                src, dst = ins[a].at[src_of(s, x, y, c)], outs[a].at[dst_of(s)]
                if rows is not None:
                    src, dst = src.at[pl.ds(rows[0], rows[1])], dst.at[pl.ds(rows[0], rows[1])]
                res.append(pltpu.make_async_remote_copy(
                    src_ref=src, dst_ref=dst, send_sem=send.at[a, s], recv_sem=recv.at[a, s],
                    device_id=peer_of(s, x, y, c), device_id_type=MESH))
        return res

    def start(ins, outs, sems):
        for cp in copies(ins, outs, sems):
            cp.start()

    def finish(ins, outs, sems):
        cps = copies(ins, outs, sems)
        for cp in cps:
            cp.wait_recv()
        for cp in cps:
            cp.wait_send()

    job = dict(ins=list(arrays), outs=[jax.ShapeDtypeStruct((out_slots,) + g.shape[1:], g.dtype) for g in arrays],
               sems=[pltpu.SemaphoreType.DMA((n_arr, n_slots)), pltpu.SemaphoreType.DMA((n_arr, n_slots))],
               start=start, finish=finish)
    if chained is not None:
        job["ins"] = list(arrays) + list(chained)
        job["alias"] = {n_arr + a: a for a in range(n_arr)}
    return job


def _pair_job(grads):
    return _exchange_job(
        grads, 4, 4,
        src_of=lambda s, x, y, c: 4 * _flip(x, s & 2) + 2 * _flip(y, s & 1) + (1 - c),
        dst_of=lambda s: s, peer_of=lambda s, x, y, c: (x, y, 1 - c))


def _chip_job(sums, rows=None, chained=None):
    return _exchange_job(
        sums, 3, 3, src_of=lambda s, x, y, c: s, dst_of=lambda s: s,
        peer_of=lambda s, x, y, c: (_flip(x, (s + 1) & 2), _flip(y, (s + 1) & 1), c), rows=rows, chained=chained)


def _a2a_job(x):
    def copies(ins, outs, sems):
        send, recv, lsem = sems
        x_, y_, c_ = _my_pos()
        me = 4 * x_ + 2 * y_ + c_
        local = pltpu.make_async_copy(ins[0].at[me], outs[0].at[me], lsem)
        res = []
        for s in range(1, N_DEV):
            px, py, pc = _flip(x_, s & 4), _flip(y_, s & 2), _flip(c_, s & 1)
            res.append(pltpu.make_async_remote_copy(
                src_ref=ins[0].at[4 * px + 2 * py + pc], dst_ref=outs[0].at[me], send_sem=send.at[s - 1],
                recv_sem=recv.at[s - 1], device_id=(px, py, pc), device_id_type=MESH))
        return local, res

    def start(ins, outs, sems):
        local, res = copies(ins, outs, sems)
        local.start()
        for cp in res:
            cp.start()

    def finish(ins, outs, sems):
        local, res = copies(ins, outs, sems)
        for cp in res:
            cp.wait_recv()
        for cp in res:
            cp.wait_send()
        local.wait()

    return dict(ins=[x], outs=[jax.ShapeDtypeStruct(x.shape, x.dtype)],
                sems=[pltpu.SemaphoreType.DMA((N_DEV - 1,)), pltpu.SemaphoreType.DMA((N_DEV - 1,)),
                      pltpu.SemaphoreType.DMA], start=start, finish=finish)


def _run_sides(sides, *, name):
    n_si, n_so = len(sides.ins), len(sides.outs)

    def body(*refs):
        ins, outs, sems = refs[:n_si], refs[n_si:n_si + n_so], refs[n_si + n_so:]
        sides.run("start", ins, outs, sems)
        sides.run("finish", ins, outs, sems)

    return pl.pallas_call(
        body, name=name, out_shape=list(sides.outs), in_specs=[ANY_SPEC] * n_si, out_specs=[ANY_SPEC] * n_so,
        scratch_shapes=list(sides.sems), input_output_aliases=sides.aliases(0, 0),
    )(*sides.ins)


def _pair_add(g, t, *, name, tr, wire):
    _, r, cols = g.shape
    g4 = g.reshape(4, 2, r, cols)
    j0 = 1 if wire else 0

    def g_index(j, i):
        x, y, c = _my_pos()
        return (jnp.bitwise_xor(2 * x + y, j + j0), c, i, 0)

    def body(g_ref, t_ref, o_ref):
        o_ref[...] = (g_ref[...] + t_ref[...]).astype(o_ref.dtype)

    return pl.pallas_call(
        body, name=name, grid=(3 if wire else 1, r // tr),
        in_specs=[pl.BlockSpec((None, None, tr, cols), g_index),
                  pl.BlockSpec((None, tr, cols), lambda j, i: (j + j0, i, 0))],
        out_specs=pl.BlockSpec((None, tr, cols), lambda j, i: (j, i, 0)),
        out_shape=jax.ShapeDtypeStruct((3 if wire else 1, r, cols), BF16 if wire else F32),
        compiler_params=_cparams(("arbitrary", "arbitrary")),
    )(g4, t)


def _config(x, ctx, w_in, w_dec_f, gla_norm_g, sg_ln_g, w_s):
    n, d = x.shape[1], x.shape[2]
    tc = ctx.shape[1]
    h = gla_norm_g.shape[1]
    dv = gla_norm_g.shape[2] * N_DEV
    dk = dv // 2
    kw, vw = h * dk, h * dv
    lr = w_dec_f.shape[1]
    sgw = sg_ln_g.shape[1]
    cfg = dict(N=n, D=d, TC=tc, H=h, DV=dv, DK=dk, KW=kw, VW=vw, LR=lr, SGW=sgw, SG_G=w_s.shape[1],
               SG_C=w_s.shape[2], IN=w_in.shape[2] * N_DEV)
    cfg.update(K0=kw, V0=2 * kw, R0=2 * kw + vw, L0=2 * kw + 2 * vw, ZA=2 * kw + 2 * vw + LANES)
    cfg.update(U0=0, VV0=sgw, ZB=2 * sgw)
    assert dk == LANES and vw == 2 * kw and 2 * lr <= LANES
    assert cfg["R0"] % vw == 0 and vw % sgw == 0
    assert cfg["IN"] == 2 * kw + 2 * vw + 2 * lr + 2 * sgw
    return cfg


def _rope_tables(cfg):
    n, tc, dk = cfg["N"], cfg["TC"], cfg["DK"]
    m = dk // 4
    pos = jnp.arange(n)
    inv = ROPE_BASE ** (-jnp.arange(m, dtype=F32) / m)
    ang_r = (pos // GRID_W).astype(F32)[:, None] * inv[None, :]
    ang_c = (pos % GRID_W).astype(F32)[:, None] * inv[None, :]
    cos = jnp.concatenate([jnp.cos(ang_r)] * 2 + [jnp.cos(ang_c)] * 2, axis=1)
    sin = jnp.concatenate([-jnp.sin(ang_r), jnp.sin(ang_r), -jnp.sin(ang_c), jnp.sin(ang_c)], axis=1)
    scale = dk ** -0.5
    z = jnp.zeros((tc, dk), F32)
    one = jnp.ones((tc, dk), F32)
    return [jnp.concatenate([cos * scale, z]), jnp.concatenate([sin * scale, z]),
            jnp.concatenate([cos, one]), jnp.concatenate([sin, z])]


def _pair_sums(g, t, nm):
    rows_for = _tile(g.shape[1], max(8, ELEMS_PER_BLOCK // g.shape[2]), 16)
    return (_pair_add(g, t, name="rs_own_" + nm, tr=rows_for, wire=False),
            _pair_add(g, t, name="rs_wire_" + nm, tr=rows_for, wire=True))


def _local_step(x, ctx, target, mods, c_mods, w, cfg):
    n, d, tc = cfg["N"], cfg["D"], cfg["TC"]
    kw, vw, sgw, za, zb, lr = cfg["KW"], cfg["VW"], cfg["SGW"], cfg["ZA"], cfg["ZB"], cfg["LR"]
    sh1, sc1, g1, sh2, sc2, g2 = mods
    csh1, csc1 = c_mods
    rt = n + tc
    tb = math.gcd(256, math.gcd(n, tc))
    tr = math.gcd(128, tb)
    tr_s = math.gcd(128, tb)
    fs = w["sh_1"].shape[1]
    ff = fs * N_DEV
    cs_in = w["sh_in"].shape[1]
    r8 = d // N_DEV

    rows_in = _run_sides(_Sides([_a2a_job(w["sh_in"].reshape(N_DEV, r8, cs_in))]), name="a2a_w_in")[0]
    rows_in = rows_in.transpose(1, 0, 2).reshape(r8, cfg["IN"])
    lf0 = 2 * kw + 2 * vw
    sg0 = lf0 + 2 * lr
    wa_rows = jnp.concatenate([rows_in[:, :sg0], jnp.zeros((r8, LANES - 2 * lr), BF16)], axis=1)
    wb_rows = rows_in[:, sg0:]
    w_a = _run_sides(_Sides([_ag_job([wa_rows])]), name="ag_w_in_a")[0].reshape(d, za)

    hx = _norm_mod(x, w["pre1_g"], sh1, sc1, name="in_norm_x", tr=tr)
    hc = _norm_mod(ctx, w["pre1_g"], csh1, csc1, name="in_norm_ctx", tr=tr)
    a_all = jnp.concatenate([hx, hc], axis=0)

    tm_a = _tile(rt, 1152, 16)
    tm_n = _tile(n, 1024, 16)
    sh_o_rows = w["sh_o"].shape[0]
    o_cut = (sh_o_rows * 3 // 4) // 16 * 16
    z, w_b, wg_o = _matmul(
        a_all, w_a, "nn", rt, za, d, tm=tm_a, tn=_tile(za, 1152, LANES), tk=_tile(d, TK, LANES), name="mm_in_a",
        out_shapes=[jax.ShapeDtypeStruct((rt, za), F32)],
        sides=_Sides([_ag3_job(wb_rows), _ag3_job(w["sh_o"], rows=(0, o_cut))]))
    w_b = w_b.reshape(d, zb)
    s16 = d // 16
    w1_rows = lambda start, count: (start * s16, count * s16)
    z_b, wg_o, w_1 = _matmul(
        a_all, w_b, "nn", n, zb, d, tm=tm_n, tn=_tile(zb, 1024, LANES), tk=_tile(d, TK, LANES), name="mm_in_b",
        out_shapes=[jax.ShapeDtypeStruct((n, zb), F32)],
        sides=_Sides([_ag3_job(w["sh_o"], rows=(o_cut, sh_o_rows - o_cut), chained=[wg_o]),
                      _ag3_job(w["sh_1"], rows=w1_rows(0, 4))]))
    w_o = wg_o.reshape(d, d)

    tabs = _rope_tables(cfg)
    qr, kr, la_f, la_b, w_1 = _gla_prep(
        z, tabs, w["wdf_pad"], w["wdb_pad"], w["b_dec_f"], w["b_dec_b"], cfg, tr=tr,
        sides=_Sides([_ag_job([w["sh_1"]], rows=w1_rows(4, 1), chained=[w_1])]))

    zero_st = jnp.zeros((cfg["H"], cfg["DV"], cfg["DK"]), F32)
    gla = dict(cfg=cfg, tb=tb)
    _, save_cf, st_cf = _gla_fwd(qr, kr, z, la_f, zero_st, rev=False, row_off=n, nrows=tc, name="gla_ctx_f", **gla)
    _, save_cb, st_cb = _gla_fwd(qr, kr, z, la_b, zero_st, rev=True, row_off=n, nrows=tc, name="gla_ctx_b", **gla)
    o_f, save_f, _, w_1 = _gla_fwd(
        qr, kr, z, la_f, st_cf, rev=False, row_off=0, nrows=n, name="gla_f",
        sides=_Sides([_ag_job([w["sh_1"]], rows=w1_rows(5, 2), chained=[w_1])]), **gla)
    o_b, save_b, _, w_1 = _gla_fwd(
        qr, kr, z, la_b, st_cb, rev=True, row_off=0, nrows=n, name="gla_b",
        sides=_Sides([_ag_job([w["sh_1"]], rows=w1_rows(7, 2), chained=[w_1])]), **gla)
    y_gla, w_1 = _readout_fwd(o_f, o_b, z, w["gla_g"], cfg, tr=tr,
                              sides=_Sides([_ag_job([w["sh_1"]], rows=w1_rows(9, 1), chained=[w_1])]))
    y_sg, w_1 = _sg_fwd(z_b, n, w["sg_ln_g"], w["sg_ln_b"], w["w_s"], w["bs_full"], cfg,
                        sides=_Sides([_ag_job([w["sh_1"]], rows=w1_rows(10, 1), chained=[w_1])]))
    ycat = jnp.concatenate([y_gla, y_sg], axis=1)

    mix, w_1 = _matmul(ycat, w_o, "nn", n, d, d, tm=tm_n, tn=_tile(d, 1024, LANES), tk=_tile(d, TK, LANES),
                       name="mm_o", out_shapes=[jax.ShapeDtypeStruct((n, d), F32)],
                       sides=_Sides([_ag3_job(w["sh_1"], rows=w1_rows(11, 5), chained=[w_1])]))
    w2_cut = (fs // 8) // 16 * 16
    x1, h2, wg_2 = _mid_fwd(x, mix, g1, w["post1_g"], w["pre2_g"], sh2, sc2, tr=tr_s,
                            sides=_Sides([_ag_job([w["sh_2"]], rows=(0, w2_cut))]))

    tn_f = _tile(fs, 1024, LANES)
    tk_d = _tile(d, TK, LANES)

    def relu2(acc):
        return acc, jnp.square(jnp.maximum(acc, 0.0))

    a1, p1, wg_2 = _matmul(h2, w_1, "nn", n, ff, d, tm=tm_n, tn=tn_f, tk=tk_d, name="mm_1",
                           b_spec=_blocked_b_nn(fs, tk_d, tn_f), epilogue=relu2,
                           out_shapes=[jax.ShapeDtypeStruct((n, ff), BF16)] * 2,
                           sides=_Sides([_ag3_job(w["sh_2"], rows=(w2_cut, fs - w2_cut), chained=[wg_2])]))
    w_2 = wg_2.reshape(ff, d)
    tk_f = _tile(ff, TK, LANES)
    m2 = _matmul(p1, w_2, "nn", n, d, ff, tm=tm_n, tn=_tile(d, 1024, LANES), tk=tk_f, name="mm_2",
                 out_shapes=[jax.ShapeDtypeStruct((n, d), F32)])[0]

    dx2, dm2, dg2, dpost2, lossc = _head_bwd(x1, m2, target, g2, w["post2_g"], tr=tr_s)

    def drelu2(acc, a):
        return (acc * (2.0 * jnp.maximum(a.astype(F32), 0.0)),)

    da1 = _matmul(dm2, w_2, "nt", n, ff, d, tm=tm_n, tn=_tile(ff, 1024, LANES), tk=tk_d, name="mm_2_dx",
                  epilogue=drelu2, extras=(a1,), out_shapes=[jax.ShapeDtypeStruct((n, ff), BF16)])[0]
    tk_n = _tile(n, TK, 16)
    tm_d = _tile(d, 1024, LANES)
    g_1 = _matmul(h2, da1, "tn", d, ff, n, tm=tm_d, tn=tn_f, tk=tk_n, name="mm_1_dw",
                  out_specs=[_blocked_out(fs, tm_d, tn_f)],
                  out_shapes=[jax.ShapeDtypeStruct((N_DEV, d, fs), F32)])[0]
    dw_2, t_1 = _matmul(p1, dm2, "tn", ff, d, n, tm=_tile(ff, 1024, LANES), tn=_tile(d, 1024, LANES), tk=tk_n,
                        name="mm_2_dw", out_shapes=[jax.ShapeDtypeStruct((ff, d), F32)],
                        sides=_Sides([_pair_job([g_1])]))
    g_2 = dw_2.reshape(N_DEV, fs, d)
    p1_own, p1_wire = _pair_sums(g_1, t_1, "w_1")
    tk_fs = _tile(fs, TK, LANES)
    dh2, u_1, t_2 = _matmul(da1, w_1, "nt", n, d, ff, tm=tm_n, tn=_tile(d, 1024, LANES), tk=tk_fs, name="mm_1_dx",
                            b_spec=_blocked_b_nt(fs, _tile(d, 1024, LANES), tk_fs),
                            out_shapes=[jax.ShapeDtypeStruct((n, d), F32)],
                            sides=_Sides([_chip_job([p1_wire], rows=(0, d * 13 // 16)), _pair_job([g_2])]))
    p2_own, p2_wire = _pair_sums(g_2, t_2, "w_2")
    c2 = [0] + [(fs * f // 64) // 16 * 16 for f in (20, 35, 50)] + [fs]
    piece2 = lambda i: (c2[i], c2[i + 1] - c2[i])

    dx1, dmix, dsh2, dsc2, dpre2, dg1, dpost1, u_1 = _mid_bwd(
        dh2, x1, dx2, mix, sc2, w["pre2_g"], g1, w["post1_g"], tr=tr_s,
        sides=_Sides([_chip_job([p1_wire], rows=(d * 13 // 16, d - d * 13 // 16), chained=[u_1])]))
    dw_o, u_2 = _matmul(ycat, dmix, "tn", d, d, n, tm=tm_d, tn=_tile(d, 1024, LANES), tk=tk_n, name="mm_o_dw",
                        out_shapes=[jax.ShapeDtypeStruct((d, d), F32)],
                        sides=_Sides([_chip_job([p2_wire], rows=piece2(0))]))
    g_o = dw_o.reshape(N_DEV, r8, d)
    dycat, t_o, u_2 = _matmul(dmix, w_o, "nt", n, d, d, tm=tm_n, tn=_tile(d, 1024, LANES), tk=tk_d, name="mm_o_dx",
                              out_shapes=[jax.ShapeDtypeStruct((n, d), F32)],
                              sides=_Sides([_pair_job([g_o]), _chip_job([p2_wire], rows=piece2(1), chained=[u_2])]))
    po_own, po_wire = _pair_sums(g_o, t_o, "w_o")

    dz_b, dws, dbs_acc, dlng, dlnb = _sg_bwd(z_b, dycat, n, w["sg_ln_g"], w["sg_ln_b"], w["w_s"], w["bs_full"], cfg)
    dw_b, u_2 = _matmul(a_all, dz_b, "tn", d, zb, n, tm=tm_d, tn=_tile(zb, 1024, LANES), tk=tk_n, name="mm_in_dw_b",
                        out_shapes=[jax.ShapeDtypeStruct((d, zb), F32)],
                        sides=_Sides([_chip_job([p2_wire], rows=piece2(2), chained=[u_2])]))
    g_b = dw_b.reshape(N_DEV, r8, zb)
    do, dzr, dgla_g = _readout_bwd(o_f, o_b, z, dycat, w["gla_g"], cfg, tr=tr)

    *gf, t_b = _gla_bwd(qr, kr, z, la_f, do, save_f, zero_st, rev=False, row_off=0, nrows=n, name="gla_f_bwd",
                        sides=_Sides([_pair_job([g_b])]), **gla)
    pb_own, pb_wire = _pair_sums(g_b, t_b, "w_in_b")
    gb = _gla_bwd(qr, kr, z, la_b, do, save_b, zero_st, rev=True, row_off=0, nrows=n, name="gla_b_bwd", **gla)
    do_c = jnp.zeros((tc, vw), BF16)
    gcf = _gla_bwd(qr, kr, z, la_f, do_c, save_cf, gf[4], rev=False, row_off=n, nrows=tc, name="gla_ctx_f_bwd",
                   **gla)
    gcb = _gla_bwd(qr, kr, z, la_b, do_c, save_cb, gb[4], rev=True, row_off=n, nrows=tc, name="gla_ctx_b_bwd",
                   **gla)

    post = dict(la_f=la_f, la_b=la_b, z=z, tabs=tabs, wdf_pad=w["wdf_pad"], wdb_pad=w["wdb_pad"], cfg=cfg, tr=tr)
    dzq, dzk, dzv, dzl, dwdf, dwdb, dbdf, dbdb = _gla_post(gf, gb, row_off=0, nrows=n, name="gla_post", **post)
    czq, czk, czv, czl, cwdf, cwdb, cbdf, cbdb = _gla_post(gcf, gcb, row_off=n, nrows=tc, name="gla_post_ctx",
                                                           **post)
    dz_a = jnp.concatenate([
        jnp.concatenate([dzq, dzk, dzv, dzr, dzl], axis=1),
        jnp.concatenate([czq, czk, czv, jnp.zeros((tc, vw), BF16), czl], axis=1)], axis=0)

    dw_a, u_b, u_2 = _matmul(
        a_all, dz_a, "tn", d, za, rt, tm=tm_d, tn=_tile(za, 1152, LANES), tk=_tile(rt, 2176, 16), name="mm_in_dw_a",
        out_shapes=[jax.ShapeDtypeStruct((d, za), F32)],
        sides=_Sides([_chip_job([pb_wire]), _chip_job([p2_wire], rows=piece2(3), chained=[u_2])]))
    g_a = dw_a.reshape(N_DEV, r8, za)
    da_a, t_a, u_o = _matmul(dz_a, w_a, "nt", rt, d, za, tm=_tile(rt, 576, 16), tn=_tile(d, 512, LANES),
                             tk=za, name="mm_in_dx_a", out_shapes=[jax.ShapeDtypeStruct((rt, d), F32)],
                             sides=_Sides([_pair_job([g_a]), _chip_job([po_wire])]))
    pa_own, pa_wire = _pair_sums(g_a, t_a, "w_in_a")
    cut_a = (r8 * 9 // 16) // 16 * 16
    tm_x = _tile(n, 512, 16)
    da_x, u_a = _matmul(dz_b, w_b, "nt", n, d, zb, tm=tm_x, tn=_tile(d, 1024, LANES), tk=_tile(zb, 4096, LANES),
                        name="mm_in_dx_b", epilogue=lambda acc, prev: (acc + prev,), extras=(da_a,),
                        out_shapes=[jax.ShapeDtypeStruct((n, d), F32)],
                        sides=_Sides([_chip_job([pa_wire], rows=(0, cut_a))]))

    grad_x, dsh1, dsc1, dpre1, u_a = _in_bwd(
        da_x, x, dx1, sc1, w["pre1_g"], row_off=0, tr=tr_s, name="in_bwd_x",
        sides=_Sides([_chip_job([pa_wire], rows=(cut_a, r8 - cut_a), chained=[u_a])]))
    dcsh1, dcsc1, dpre1_c = _in_bwd(da_a, ctx, None, csc1, w["pre1_g"], row_off=n, tr=tr_s, name="in_bwd_ctx")

    small = dict(
        pre1_g=dpre1 + dpre1_c, post1_g=dpost1, pre2_g=dpre2, post2_g=dpost2,
        w_dec_f=(dwdf + cwdf)[:lr], w_dec_b=(dwdb + cwdb)[lr:2 * lr], b_dec_f=dbdf + cbdf, b_dec_b=dbdb + cbdb,
        gla_norm_g=dgla_g, sg_ln_g=dlng, sg_ln_b=dlnb, w_s=dws,
        b_s=dbs_acc.reshape(cfg["SG_C"], cfg["SG_G"], sgw // cfg["SG_G"]).sum(-1).T)
    dmod = jnp.concatenate([dsh1, dsc1, dg1, dsh2, dsc2, dg2], axis=1)
    dmod_c = jnp.concatenate([dcsh1, dcsc1], axis=1)
    big = dict(w_in_a=(pa_own, u_a), w_in_b=(pb_own, u_b), w_o=(po_own, u_o), w_1=(p1_own, u_1), w_2=(p2_own, u_2))
    return lossc, grad_x, big, small, dmod, dmod_c


SMALL_NAMES = ["b_ada", "pre1_g", "post1_g", "pre2_g", "post2_g", "w_dec_f", "b_dec_f", "w_dec_b", "b_dec_b",
               "gla_norm_g", "sg_ln_g", "sg_ln_b", "w_s", "b_s", "c_ctx"]
WEIGHT_ORDER = ["c_ctx", "w_ada", "b_ada", "pre1_g", "post1_g", "pre2_g", "post2_g", "w_in", "w_dec_f", "b_dec_f",
                "w_dec_b", "b_dec_b", "gla_norm_g", "sg_ln_g", "sg_ln_b", "w_s", "b_s", "w_o", "w_1", "w_2"]


def kernel(x, c, ctx, c_ctx, w_ada, b_ada, pre1_g, post1_g, pre2_g, post2_g, w_in, w_dec_f, b_dec_f, w_dec_b, b_dec_b, gla_norm_g, sg_ln_g, sg_ln_b, w_s, b_s, w_o, w_1, w_2, loss_target, m_c_ctx, m_w_ada, m_b_ada, m_pre1_g, m_post1_g, m_pre2_g, m_post2_g, m_w_in, m_w_dec_f, m_b_dec_f, m_w_dec_b, m_b_dec_b, m_gla_norm_g, m_sg_ln_g, m_sg_ln_b, m_w_s, m_b_s, m_w_o, m_w_1, m_w_2, v_c_ctx, v_w_ada, v_b_ada, v_pre1_g, v_post1_g, v_pre2_g, v_post2_g, v_w_in, v_w_dec_f, v_b_dec_f, v_w_dec_b, v_b_dec_b, v_gla_norm_g, v_sg_ln_g, v_sg_ln_b, v_w_s, v_b_s, v_w_o, v_w_1, v_w_2):
    weights = dict(c_ctx=c_ctx, w_ada=w_ada, b_ada=b_ada, pre1_g=pre1_g, post1_g=post1_g, pre2_g=pre2_g,
                   post2_g=post2_g, w_in=w_in, w_dec_f=w_dec_f, b_dec_f=b_dec_f, w_dec_b=w_dec_b, b_dec_b=b_dec_b,
                   gla_norm_g=gla_norm_g, sg_ln_g=sg_ln_g, sg_ln_b=sg_ln_b, w_s=w_s, b_s=b_s, w_o=w_o, w_1=w_1,
                   w_2=w_2)
    mom_m = dict(c_ctx=m_c_ctx, w_ada=m_w_ada, b_ada=m_b_ada, pre1_g=m_pre1_g, post1_g=m_post1_g, pre2_g=m_pre2_g,
                 post2_g=m_post2_g, w_in=m_w_in, w_dec_f=m_w_dec_f, b_dec_f=m_b_dec_f, w_dec_b=m_w_dec_b,
                 b_dec_b=m_b_dec_b, gla_norm_g=m_gla_norm_g, sg_ln_g=m_sg_ln_g, sg_ln_b=m_sg_ln_b, w_s=m_w_s,
                 b_s=m_b_s, w_o=m_w_o, w_1=m_w_1, w_2=m_w_2)
    mom_v = dict(c_ctx=v_c_ctx, w_ada=v_w_ada, b_ada=v_b_ada, pre1_g=v_pre1_g, post1_g=v_post1_g, pre2_g=v_pre2_g,
                 post2_g=v_post2_g, w_in=v_w_in, w_dec_f=v_w_dec_f, b_dec_f=v_b_dec_f, w_dec_b=v_w_dec_b,
                 b_dec_b=v_b_dec_b, gla_norm_g=v_gla_norm_g, sg_ln_g=v_sg_ln_g, sg_ln_b=v_sg_ln_b, w_s=v_w_s,
                 b_s=v_b_s, w_o=v_w_o, w_1=v_w_1, w_2=v_w_2)

    cfg = _config(x, ctx, w_in, w_dec_f, gla_norm_g, sg_ln_g, w_s)
    n, d, h, dv, kw, vw, lr, sgw = (cfg[k] for k in ("N", "D", "H", "DV", "KW", "VW", "LR", "SGW"))
    dvs, kws = dv // N_DEV, kw // N_DEV
    ix, iy, ic = _my_pos()
    me = 4 * ix + 2 * iy + ic

    pack1 = jnp.concatenate([c.reshape(1, d), w_dec_f.reshape(1, lr * kws), w_dec_b.reshape(1, lr * kws),
                             gla_norm_g.reshape(1, h * dvs)], axis=1)
    g1 = _all_gather_vec(pack1, name="ag_small_in")
    c_all = g1[:, :d]
    o1 = d
    wdf = g1[:, o1:o1 + lr * kws].reshape(N_DEV, lr, kws).transpose(1, 0, 2).reshape(lr, kw)
    o1 += lr * kws
    wdb = g1[:, o1:o1 + lr * kws].reshape(N_DEV, lr, kws).transpose(1, 0, 2).reshape(lr, kw)
    o1 += lr * kws
    gla_g = g1[:, o1:o1 + h * dvs].reshape(N_DEV, h, dvs).transpose(1, 0, 2).reshape(1, h * dv)

    c16 = jnp.concatenate([c_all, jnp.broadcast_to(c_ctx.reshape(1, d), (N_DEV, d))], axis=0)
    ncol = w_ada.shape[2]
    wa = w_ada.reshape(d, ncol)
    b_mine = lax.dynamic_slice(b_ada, (0, me * ncol), (1, ncol))
    tn_ada = _tile(ncol, 512, LANES)
    mod_mine = _ada_fwd(c16, wa, b_mine, tn=tn_ada)
    mod_all = _all_gather_small(mod_mine, name="ag_mod").transpose(1, 0, 2).reshape(16, N_DEV * ncol)
    mod_b = lax.dynamic_slice(mod_all, (me, 0), (1, 6 * d))
    mods = [mod_b[:, i * d:(i + 1) * d] for i in range(6)]
    c_mods = [mod_all[N_DEV:N_DEV + 1, :d], mod_all[N_DEV:N_DEV + 1, d:2 * d]]

    zpad = lambda r: jnp.zeros((r, kw), F32)
    w = dict(
        sh_in=w_in.reshape(d, w_in.shape[2]).astype(BF16), sh_o=w_o.reshape(w_o.shape[1], d).astype(BF16),
        sh_1=w_1.reshape(d, w_1.shape[2]).astype(BF16), sh_2=w_2.reshape(w_2.shape[1], d).astype(BF16),
        pre1_g=pre1_g, post1_g=post1_g, pre2_g=pre2_g, post2_g=post2_g, b_dec_f=b_dec_f, b_dec_b=b_dec_b,
        wdf_pad=jnp.concatenate([wdf, zpad(LANES - lr)], axis=0),
        wdb_pad=jnp.concatenate([zpad(lr), wdb, zpad(LANES - 2 * lr)], axis=0),
        gla_g=gla_g, sg_ln_g=sg_ln_g, sg_ln_b=sg_ln_b, w_s=w_s[0],
        bs_full=jnp.repeat(b_s[0].T, sgw // cfg["SG_G"], axis=1))

    lossc, grad_x, big, small, dmod, dmod_c = _local_step(x[0], ctx[0], loss_target[0], mods, c_mods, w, cfg)
    loss = lax.psum(jnp.sum(lossc), AXES)

    order3 = ["pre1_g", "post1_g", "pre2_g", "post2_g", "w_dec_f", "b_dec_f", "w_dec_b", "b_dec_b", "gla_norm_g",
              "sg_ln_g", "sg_ln_b", "w_s", "b_s"]
    pieces = [dmod, dmod_c] + [small[k].reshape(1, -1) for k in order3]
    sizes = [p.shape[1] for p in pieces]
    g3 = _all_gather_vec(jnp.concatenate(pieces, axis=1), name="ag_small_grads")
    offs = [0]
    for s in sizes:
        offs.append(offs[-1] + s)
    dmod_all = g3[:, :6 * d]
    dmod_c_all = jnp.pad(g3[:, offs[1]:offs[2]], ((0, 0), (0, 4 * d)))
    parts8 = {k: g3[:, offs[2 + i]:offs[3 + i]] for i, k in enumerate(order3)}
    parts8["b_ada"] = dmod_all + dmod_c_all
    parts8["w_dec_f"] = lax.dynamic_slice(parts8["w_dec_f"].reshape(N_DEV, lr, kw), (0, 0, me * kws),
                                          (N_DEV, lr, kws)).reshape(N_DEV, -1)
    parts8["w_dec_b"] = lax.dynamic_slice(parts8["w_dec_b"].reshape(N_DEV, lr, kw), (0, 0, me * kws),
                                          (N_DEV, lr, kws)).reshape(N_DEV, -1)
    parts8["gla_norm_g"] = lax.dynamic_slice(parts8["gla_norm_g"].reshape(N_DEV, h, dv), (0, 0, me * dvs),
                                             (N_DEV, h, dvs)).reshape(N_DEV, -1)

    dm16 = jnp.concatenate([dmod_all, dmod_c_all], axis=0)
    dm_mine = lax.dynamic_slice(dm16, (0, me * ncol), (16, ncol))
    g_w_ada, dcc = _ada_bwd(c16, dm_mine, wa, c_ctx.reshape(1, d), tn=tn_ada)
    parts8["c_ctx"] = _all_gather_vec(dcc, name="ag_cctx")

    flat = lambda t: t.reshape(1, -1)
    g8 = _dense(jnp.concatenate([parts8[k] for k in SMALL_NAMES], axis=1))
    ws, ms, vs = [_dense(jnp.concatenate([flat(src[k]) for k in SMALL_NAMES], axis=1))[0]
                  for src in (weights, mom_m, mom_v)]
    res_small = [r.reshape(1, -1) for r in _adam_small(g8, ws, ms, vs)]
    out = {}
    off = 0
    for k in SMALL_NAMES:
        sz = weights[k].size
        out[k] = [r[:, off:off + sz].reshape(weights[k].shape) for r in res_small]
        off += sz

    rows_for = lambda r, cols: _tile(r, max(8, ELEMS_PER_BLOCK // cols), 16)
    r8, cs_in = d // N_DEV, w_in.shape[2]

    def adam(nm, parts, sides=_NO_SIDES):
        shp = weights[nm].shape
        r2 = (shp[1], shp[2])
        res = _adam_big(parts, weights[nm].reshape(r2), mom_m[nm].reshape(r2), mom_v[nm].reshape(r2),
                        name="adam_" + nm, tr=rows_for(*r2), sides=sides)
        out[nm] = [r.reshape(shp) for r in res[:4]]
        return res[4:]

    four = lambda own, u: [(own, 0), (u, 0), (u, 1), (u, 2)]
    red_a = _sum_parts(four(*big["w_in_a"]), r8, cfg["ZA"], name="rs_sum_w_in_a", tr=rows_for(r8, cfg["ZA"]))
    red_b = _sum_parts(four(*big["w_in_b"]), r8, cfg["ZB"], name="rs_sum_w_in_b", tr=rows_for(r8, cfg["ZB"]))
    red = jnp.concatenate([red_a[:, :cfg["L0"] + 2 * lr], red_b], axis=1)
    (g_in,) = adam("w_ada", [(g_w_ada, None)],
                   _Sides([_a2a_job(red.reshape(r8, N_DEV, cs_in).transpose(1, 0, 2))]))
    for nm in ("w_2", "w_1", "w_o"):
        adam(nm, four(*big[nm]))
    adam("w_in", [(g_in.reshape(d, cs_in), None)])

    outs = [loss, grad_x[None]]
    for i in range(4):
        outs += [out[k][i] for k in WEIGHT_ORDER]
    return tuple(outs)
```

```python
import math

import jax
import jax.numpy as jnp
from jax import lax
from jax.experimental import pallas as pl
from jax.experimental.pallas import tpu as pltpu

F32 = jnp.float32
BF16 = jnp.bfloat16
MXU_DTYPE = jnp.bfloat16
HI = lax.Precision.HIGHEST

N_DEV = 8
AXES = ("x", "y", "c")
MESH = pl.DeviceIdType.MESH
LANES = 128
VMEM_LIMIT = 56 * 1024 * 1024

EPS = 1e-6
GRID_W = 64
GLA_CHUNK = 64
GLA_TAU = 16.0
ROPE_BASE = 10000.0
ADAM_LR = 0.001
ADAM_B1 = 0.9
ADAM_B2 = 0.999
ADAM_EPS = 1e-08
ADAM_WD = 0.01
ADAM_STEP = 10


def _cparams(sem):
    return pltpu.CompilerParams(dimension_semantics=sem, vmem_limit_bytes=VMEM_LIMIT)


def _tile(n, target, align):
    if n <= target:
        return n
    best = None
    for t in range(align, target + 1, align):
        if n % t == 0:
            best = t
    assert best is not None, (n, target, align)
    return best


def _dg(a, b, dims, prec=None):
    return lax.dot_general(a, b, (dims, ((), ())), precision=prec, preferred_element_type=F32)


def _nn(a, b):
    return _dg(a.astype(MXU_DTYPE), b.astype(MXU_DTYPE), ((1,), (0,)))


def _nt(a, b):
    return _dg(a.astype(MXU_DTYPE), b.astype(MXU_DTYPE), ((1,), (1,)))


def _tn(a, b):
    return _dg(a.astype(MXU_DTYPE), b.astype(MXU_DTYPE), ((0,), (0,)))


def _sigmoid(x):
    return 1.0 / (1.0 + jnp.exp(-x))


def _silu(x):
    return x * _sigmoid(x)


def _dsilu(x):
    s = _sigmoid(x)
    return s * (1.0 + x * (1.0 - s))


def _gelu(x):
    return 0.5 * x * (1.0 + lax.erf(x * (1.0 / math.sqrt(2.0))))


def _dgelu(x):
    return 0.5 * (1.0 + lax.erf(x * (1.0 / math.sqrt(2.0)))) + x * jnp.exp(-0.5 * x * x) * (1.0 / math.sqrt(2.0 * math.pi))


def _rstd(x):
    return lax.rsqrt(jnp.mean(x * x, axis=-1, keepdims=True) + EPS)


def _rms_bwd(x, r, dn):
    return r * dn - x * (r * r * r) * jnp.mean(dn * x, axis=-1, keepdims=True)


def _colsum(x):
    return jnp.sum(x, axis=0, keepdims=True)


class _Sides:
    def __init__(self, jobs):
        self.jobs = list(jobs)
        self.ins = [a for j in self.jobs for a in j["ins"]]
        self.outs = [o for j in self.jobs for o in j["outs"]]
        self.sems = [s for j in self.jobs for s in j["sems"]]

    def aliases(self, in_base, out_base):
        res, oi, oo = {}, 0, 0
        for j in self.jobs:
            for a, b in j.get("alias", {}).items():
                res[in_base + oi + a] = out_base + oo + b
            oi += len(j["ins"])
            oo += len(j["outs"])
        return res

    def has(self, phase):
        return any(phase in j for j in self.jobs)

    def run(self, phase, in_refs, out_refs, sem_refs):
        oi = oo = os_ = 0
        for j in self.jobs:
            ni, no, ns = len(j["ins"]), len(j["outs"]), len(j["sems"])
            if phase in j:
                j[phase](in_refs[oi:oi + ni], out_refs[oo:oo + no], sem_refs[os_:os_ + ns])
            oi, oo, os_ = oi + ni, oo + no, os_ + ns


_NO_SIDES = _Sides([])
ANY_SPEC = pl.BlockSpec(memory_space=pl.ANY)


def _matmul(a, b, mode, m, n, k, *, tm, tn, tk, name, out_shapes, b_spec=None, out_specs=None,
            epilogue=None, extras=(), sides=_NO_SIDES):
    nk = k // tk
    assert m % tm == 0 and n % tn == 0 and k % tk == 0, (name, m, n, k, tm, tn, tk)
    dot = {"nn": _nn, "nt": _nt, "tn": _tn}[mode]
    if mode == "tn":
        a_spec = pl.BlockSpec((tk, tm), lambda i, j, kk: (kk, i))
    else:
        a_spec = pl.BlockSpec((tm, tk), lambda i, j, kk: (i, kk))
    if b_spec is None:
        if mode == "nt":
            b_spec = pl.BlockSpec((tn, tk), lambda i, j, kk: (j, kk))
        else:
            b_spec = pl.BlockSpec((tk, tn), lambda i, j, kk: (kk, j))
    mn_spec = pl.BlockSpec((tm, tn), lambda i, j, kk: (i, j))
    if out_specs is None:
        out_specs = [mn_spec] * len(out_shapes)
    n_extra = len(extras)
    n_out = len(out_shapes)
    n_si, n_so = len(sides.ins), len(sides.outs)
    ni, nj = m // tm, n // tn

    def body(a_ref, b_ref, *rest):
        extra_refs = rest[:n_extra]
        rest = rest[n_extra:]
        side_in, rest = rest[:n_si], rest[n_si:]
        out_refs, rest = rest[:n_out], rest[n_out:]
        side_out, rest = rest[:n_so], rest[n_so:]
        acc, side_sems = rest[0], rest[1:]
        i, j, kk = pl.program_id(0), pl.program_id(1), pl.program_id(2)

        if sides.jobs:
            @pl.when((i == 0) & (j == 0) & (kk == 0))
            def _():
                sides.run("start", side_in, side_out, side_sems)

        if sides.has("middle"):
            mid = (ni * nj * nk * 2) // 3
            mi, mj, mk = mid // (nj * nk), (mid // nk) % nj, mid % nk

            @pl.when((i == mi) & (j == mj) & (kk == mk))
            def _():
                sides.run("middle", side_in, side_out, side_sems)

        @pl.when(kk == 0)
        def _():
            acc[...] = jnp.zeros_like(acc)

        acc[...] += dot(a_ref[...], b_ref[...])

        @pl.when(kk == nk - 1)
        def _():
            vals = (acc[...],) if epilogue is None else epilogue(acc[...], *[e[...] for e in extra_refs])
            for o, v in zip(out_refs, vals):
                o[...] = v.astype(o.dtype)

        if sides.jobs:
            @pl.when((i == ni - 1) & (j == nj - 1) & (kk == nk - 1))
            def _():
                sides.run("finish", side_in, side_out, side_sems)

    sem = ("arbitrary",) * 3 if sides.jobs else ("parallel", "parallel", "arbitrary")
    res = pl.pallas_call(
        body, name=name, grid=(ni, nj, nk),
        in_specs=[a_spec, b_spec] + [mn_spec] * n_extra + [ANY_SPEC] * n_si,
        out_specs=list(out_specs) + [ANY_SPEC] * n_so, out_shape=list(out_shapes) + list(sides.outs),
        scratch_shapes=[pltpu.VMEM((tm, tn), F32)] + list(sides.sems),
        input_output_aliases=sides.aliases(2 + n_extra, n_out),
        compiler_params=_cparams(sem),
    )(a, b, *extras, *sides.ins)
    return res


def _blocked_b_nn(ns, tk, tn):
    assert ns % tn == 0
    return pl.BlockSpec((None, tk, tn), lambda i, j, kk: ((j * tn) // ns, kk, ((j * tn) % ns) // tn))


def _blocked_b_nt(ks, tn, tk):
    assert ks % tk == 0
    return pl.BlockSpec((None, tn, tk), lambda i, j, kk: ((kk * tk) // ks, j, ((kk * tk) % ks) // tk))


def _blocked_out(ns, tm, tn):
    assert ns % tn == 0
    return pl.BlockSpec((None, tm, tn), lambda i, j, kk: ((j * tn) // ns, i, ((j * tn) % ns) // tn))


def _rows_call(body, *, name, nblk, tr, row_ins, consts, row_outs, accs=(), sides=_NO_SIDES):
    n_ri, n_c, n_ro, n_acc = len(row_ins), len(consts), len(row_outs), len(accs)
    n_si, n_so = len(sides.ins), len(sides.outs)

    def kern(*refs):
        i = pl.program_id(0)
        rin, refs = refs[:n_ri], refs[n_ri:]
        cin, refs = refs[:n_c], refs[n_c:]
        side_in, refs = refs[:n_si], refs[n_si:]
        rout, refs = refs[:n_ro], refs[n_ro:]
        acc, refs = refs[:n_acc], refs[n_acc:]
        side_out, side_sems = refs[:n_so], refs[n_so:]

        if sides.jobs:
            @pl.when(i == 0)
            def _():
                sides.run("start", side_in, side_out, side_sems)

        if n_acc:
            @pl.when(i == 0)
            def _():
                for r in acc:
                    r[...] = jnp.zeros_like(r)

        body(rin, cin, rout, acc)

        if sides.jobs:
            @pl.when(i == nblk - 1)
            def _():
                sides.run("finish", side_in, side_out, side_sems)

    in_specs = [pl.BlockSpec((tr, w), lambda i, ro=ro, co=co: (i + ro, co)) for (_, w, ro, co) in row_ins]
    in_specs += [pl.BlockSpec(cst.shape, lambda i, nd=cst.ndim: (0,) * nd) for cst in consts]
    out_specs = [pl.BlockSpec((tr, w), lambda i: (i, 0)) for (_, w, _) in row_outs]
    out_specs += [pl.BlockSpec(s, lambda i, nd=len(s): (0,) * nd) for s in accs]
    out_shape = [jax.ShapeDtypeStruct((r, w), dt) for (r, w, dt) in row_outs]
    out_shape += [jax.ShapeDtypeStruct(s, F32) for s in accs]
    return pl.pallas_call(
        kern, name=name, grid=(nblk,), in_specs=in_specs + [ANY_SPEC] * n_si,
        out_specs=out_specs + [ANY_SPEC] * n_so, out_shape=out_shape + list(sides.outs),
        scratch_shapes=list(sides.sems),
        input_output_aliases=sides.aliases(n_ri + n_c, n_ro + n_acc),
        compiler_params=_cparams(("arbitrary",)),
    )(*[r[0] for r in row_ins], *consts, *sides.ins)


def _norm_mod(x, g, shift, scale, *, name, tr, sides=_NO_SIDES):
    rows, d = x.shape

    def body(rin, cin, rout, acc):
        xv = rin[0][...]
        n = xv * _rstd(xv) * cin[0][...]
        rout[0][...] = (n * (1.0 + cin[2][...]) + cin[1][...]).astype(BF16)

    return _rows_call(body, name=name, nblk=rows // tr, tr=tr, row_ins=[(x, d, 0, 0)],
                      consts=[g, shift, scale], row_outs=[(rows, d, BF16)], sides=sides)


def _swap_halves(t, width):
    lane = lax.broadcasted_iota(jnp.int32, t.shape, 1)
    return jnp.where(lane % 64 < 32, pltpu.roll(t, width - 32, 1), pltpu.roll(t, 32, 1))


def _gla_prep(z, tabs, wdf_pad, wdb_pad, bdf, bdb, cfg, *, tr, sides=_NO_SIDES):
    rows = z.shape[0]
    kw, h = cfg["KW"], cfg["H"]

    def body(rin, cin, rout, acc):
        zq, zk, zl = rin[0][...], rin[1][...], rin[2][...]
        cq, sq, ck, sk = [jnp.concatenate([rin[3 + t][...]] * h, axis=1) for t in range(4)]
        rout[0][...] = zq * cq + _swap_halves(zq, kw) * sq
        rout[1][...] = zk * ck + _swap_halves(zk, kw) * sk
        for o, w, b in ((2, cin[0], cin[2]), (3, cin[1], cin[3])):
            a = _nn(zl, w[...]) + b[...]
            rout[o][...] = (jnp.minimum(a, 0.0) - jnp.log(1.0 + jnp.exp(-jnp.abs(a)))) * (1.0 / GLA_TAU)

    row_ins = [(z, kw, 0, 0), (z, kw, 0, 1), (z, LANES, 0, cfg["L0"] // LANES)]
    row_ins += [(t, LANES, 0, 0) for t in tabs]
    return _rows_call(body, name="gla_prep", nblk=rows // tr, tr=tr, row_ins=row_ins,
                      consts=[wdf_pad, wdb_pad, bdf, bdb], row_outs=[(rows, kw, F32)] * 4, sides=sides)


def _chunk_consts(rev):
    c = GLA_CHUNK
    r = lax.broadcasted_iota(jnp.int32, (c, c), 0)
    cc = lax.broadcasted_iota(jnp.int32, (c, c), 1)
    keep = (cc >= r) if rev else (cc <= r)
    return keep, keep.astype(F32)


def _heads_per_step(cfg):
    hb = 4 if cfg["H"] % 4 == 0 else (2 if cfg["H"] % 2 == 0 else 1)
    assert cfg["V0"] % (hb * cfg["DV"]) == 0
    return hb


def _chunk_decay(la, keep_f):
    b = _dg(keep_f, la, ((1,), (0,)), HI)
    return b, _colsum(la)


def _gla_fwd(qr, kr, z, la, st0, cfg, *, rev, row_off, nrows, tb, name, sides=_NO_SIDES):
    h, dk, dv = cfg["H"], cfg["DK"], cfg["DV"]
    c = GLA_CHUNK
    nsub = tb // c
    nblk = nrows // tb
    roff = row_off // tb
    hb = _heads_per_step(cfg)
    v_cb = cfg["V0"] // (hb * dv)
    n_si, n_so = len(sides.ins), len(sides.outs)

    def blk(j):
        return (nblk - 1 - j) if rev else j

    def body(q_ref, k_ref, v_ref, la_ref, st0_ref, *rest):
        side_in, rest = rest[:n_si], rest[n_si:]
        o_ref, save_ref, fin_ref = rest[:3]
        side_out, st, side_sems = rest[3:3 + n_so], rest[3 + n_so], rest[4 + n_so:]
        hh, j = pl.program_id(0), pl.program_id(1)

        if sides.jobs:
            @pl.when((hh == 0) & (j == 0))
            def _():
                sides.run("start", side_in, side_out, side_sems)

        @pl.when(j == 0)
        def _():
            st[...] = st0_ref[...]

        keep, keep_f = _chunk_consts(rev)
        order = range(nsub - 1, -1, -1) if rev else range(nsub)
        heads = range(hb)
        ksl = [slice(g * dk, (g + 1) * dk) for g in heads]
        vsl = [slice(g * dv, (g + 1) * dv) for g in heads]
        state = [st[g] for g in heads]
        for s in order:
            rs = pl.ds(s * c, c)
            q = [q_ref[rs, ksl[g]] for g in heads]
            k = [k_ref[rs, ksl[g]] for g in heads]
            v = [v_ref[rs, vsl[g]] for g in heads]
            bb = [_chunk_decay(la_ref[rs, ksl[g]], keep_f) for g in heads]
            qe = [q[g] * jnp.exp(bb[g][0]) for g in heads]
            ke = [k[g] * jnp.exp(-bb[g][0]) for g in heads]
            kl = [k[g] * jnp.exp(bb[g][1] - bb[g][0]) for g in heads]
            att = [jnp.where(keep, _nt(qe[g], ke[g]), 0.0) for g in heads]
            out = [_nt(qe[g], state[g]) + _nn(att[g], v[g]) for g in heads]
            new = [state[g] * jnp.exp(bb[g][1]) + _tn(v[g], kl[g]) for g in heads]
            for g in heads:
                save_ref[g, s] = state[g]
                o_ref[rs, vsl[g]] = out[g]
            state = new
        for g in heads:
            st[g] = state[g]

        @pl.when(j == nblk - 1)
        def _():
            fin_ref[...] = st[...]

        if sides.jobs:
            @pl.when((hh == h // hb - 1) & (j == nblk - 1))
            def _():
                sides.run("finish", side_in, side_out, side_sems)

    in_specs = [
        pl.BlockSpec((tb, hb * dk), lambda hh, j: (roff + blk(j), hh)),
        pl.BlockSpec((tb, hb * dk), lambda hh, j: (roff + blk(j), hh)),
        pl.BlockSpec((tb, hb * dv), lambda hh, j: (roff + blk(j), v_cb + hh)),
        pl.BlockSpec((tb, hb * dk), lambda hh, j: (roff + blk(j), hh)),
        pl.BlockSpec((hb, dv, dk), lambda hh, j: (hh, 0, 0)),
    ]
    out_specs = [
        pl.BlockSpec((tb, hb * dv), lambda hh, j: (blk(j), hh)),
        pl.BlockSpec((hb, nsub, dv, dk), lambda hh, j: (hh, blk(j), 0, 0)),
        pl.BlockSpec((hb, dv, dk), lambda hh, j: (hh, 0, 0)),
    ]
    out_shape = [
        jax.ShapeDtypeStruct((nrows, h * dv), F32),
        jax.ShapeDtypeStruct((h, nrows // c, dv, dk), F32),
        jax.ShapeDtypeStruct((h, dv, dk), F32),
    ]
    return pl.pallas_call(
        body, name=name, grid=(h // hb, nblk), in_specs=in_specs + [ANY_SPEC] * n_si,
        out_specs=out_specs + [ANY_SPEC] * n_so, out_shape=out_shape + list(sides.outs),
        scratch_shapes=[pltpu.VMEM((hb, dv, dk), F32)] + list(sides.sems),
        input_output_aliases=sides.aliases(5, 3),
        compiler_params=_cparams(("arbitrary", "arbitrary")),
    )(qr, kr, z, la, st0, *sides.ins)


def _gla_bwd(qr, kr, z, la, do, save, dst_init, cfg, *, rev, row_off, nrows, tb, name, sides=_NO_SIDES):
    h, dk, dv = cfg["H"], cfg["DK"], cfg["DV"]
    c = GLA_CHUNK
    nsub = tb // c
    nblk = nrows // tb
    roff = row_off // tb
    hb = _heads_per_step(cfg)
    v_cb = cfg["V0"] // (hb * dv)
    n_si, n_so = len(sides.ins), len(sides.outs)

    def blk(j):
        return j if rev else (nblk - 1 - j)

    def body(q_ref, k_ref, v_ref, la_ref, do_ref, save_ref, di_ref, *rest):
        side_in, rest = rest[:n_si], rest[n_si:]
        dq_ref, dk_ref, dv_ref, dla_ref, d0_ref = rest[:5]
        side_out, dst, side_sems = rest[5:5 + n_so], rest[5 + n_so], rest[6 + n_so:]
        hh, j = pl.program_id(0), pl.program_id(1)

        if sides.jobs:
            @pl.when((hh == 0) & (j == 0))
            def _():
                sides.run("start", side_in, side_out, side_sems)

        @pl.when(j == 0)
        def _():
            dst[...] = di_ref[...]

        keep, keep_f = _chunk_consts(rev)
        keep_t = _chunk_consts(not rev)[1]
        order = range(nsub) if rev else range(nsub - 1, -1, -1)
        heads = range(hb)
        ksl = [slice(g * dk, (g + 1) * dk) for g in heads]
        vsl = [slice(g * dv, (g + 1) * dv) for g in heads]
        d_after = [dst[g] for g in heads]
        for s in order:
            rs = pl.ds(s * c, c)
            q = [q_ref[rs, ksl[g]] for g in heads]
            k = [k_ref[rs, ksl[g]] for g in heads]
            v = [v_ref[rs, vsl[g]] for g in heads]
            lac = [la_ref[rs, ksl[g]] for g in heads]
            dout = [do_ref[rs, vsl[g]] for g in heads]
            s_in = [save_ref[g, s] for g in heads]
            bb = [_chunk_decay(lac[g], keep_f) for g in heads]
            eb = [jnp.exp(bb[g][0]) for g in heads]
            enb = [jnp.exp(-bb[g][0]) for g in heads]
            elb = [jnp.exp(bb[g][1] - bb[g][0]) for g in heads]
            etot = [jnp.exp(bb[g][1]) for g in heads]
            qe = [q[g] * eb[g] for g in heads]
            ke = [k[g] * enb[g] for g in heads]
            kl = [k[g] * elb[g] for g in heads]
            att = [jnp.where(keep, _nt(qe[g], ke[g]), 0.0) for g in heads]
            datt = [jnp.where(keep, _nt(dout[g], v[g]), 0.0) for g in heads]
            dqe = [_nn(dout[g], s_in[g]) + _nn(datt[g], ke[g]) for g in heads]
            dke = [_tn(datt[g], qe[g]) for g in heads]
            dkl = [_nn(v[g], d_after[g]) for g in heads]
            dvv = [_tn(att[g], dout[g]) + _nt(kl[g], d_after[g]) for g in heads]
            db = [dqe[g] * qe[g] - dke[g] * ke[g] - dkl[g] * kl[g] for g in heads]
            dbtot = [_colsum(dkl[g] * kl[g]) + _colsum(d_after[g] * s_in[g]) * etot[g] for g in heads]
            dla = [_dg(keep_t, db[g], ((1,), (0,)), HI) + dbtot[g] for g in heads]
            d_after = [d_after[g] * etot[g] + _tn(dout[g], qe[g]) for g in heads]
            for g in heads:
                dv_ref[rs, vsl[g]] = dvv[g]
                dla_ref[rs, ksl[g]] = dla[g]
                dq_ref[rs, ksl[g]] = dqe[g] * eb[g]
                dk_ref[rs, ksl[g]] = dke[g] * enb[g] + dkl[g] * elb[g]
        for g in heads:
            dst[g] = d_after[g]

        @pl.when(j == nblk - 1)
        def _():
            d0_ref[...] = dst[...]

        if sides.jobs:
            @pl.when((hh == h // hb - 1) & (j == nblk - 1))
            def _():
                sides.run("finish", side_in, side_out, side_sems)

    in_specs = [
        pl.BlockSpec((tb, hb * dk), lambda hh, j: (roff + blk(j), hh)),
        pl.BlockSpec((tb, hb * dk), lambda hh, j: (roff + blk(j), hh)),
        pl.BlockSpec((tb, hb * dv), lambda hh, j: (roff + blk(j), v_cb + hh)),
        pl.BlockSpec((tb, hb * dk), lambda hh, j: (roff + blk(j), hh)),
        pl.BlockSpec((tb, hb * dv), lambda hh, j: (blk(j), hh)),
        pl.BlockSpec((hb, nsub, dv, dk), lambda hh, j: (hh, blk(j), 0, 0)),
        pl.BlockSpec((hb, dv, dk), lambda hh, j: (hh, 0, 0)),
    ]
    out_specs = [
        pl.BlockSpec((tb, hb * dk), lambda hh, j: (blk(j), hh)),
        pl.BlockSpec((tb, hb * dk), lambda hh, j: (blk(j), hh)),
        pl.BlockSpec((tb, hb * dv), lambda hh, j: (blk(j), hh)),
        pl.BlockSpec((tb, hb * dk), lambda hh, j: (blk(j), hh)),
        pl.BlockSpec((hb, dv, dk), lambda hh, j: (hh, 0, 0)),
    ]
    out_shape = [
        jax.ShapeDtypeStruct((nrows, h * dk), F32),
        jax.ShapeDtypeStruct((nrows, h * dk), F32),
        jax.ShapeDtypeStruct((nrows, h * dv), F32),
        jax.ShapeDtypeStruct((nrows, h * dk), F32),
        jax.ShapeDtypeStruct((h, dv, dk), F32),
    ]
    return pl.pallas_call(
        body, name=name, grid=(h // hb, nblk), in_specs=in_specs + [ANY_SPEC] * n_si,
        out_specs=out_specs + [ANY_SPEC] * n_so, out_shape=out_shape + list(sides.outs),
        scratch_shapes=[pltpu.VMEM((hb, dv, dk), F32)] + list(sides.sems),
        input_output_aliases=sides.aliases(7, 5),
        compiler_params=_cparams(("arbitrary", "arbitrary")),
    )(qr, kr, z, la, do, save, dst_init, *sides.ins)


def _gla_post(gf, gb, la_f, la_b, z, tabs, wdf_pad, wdb_pad, cfg, *, row_off, nrows, tr, name):
    kw, vw = cfg["KW"], cfg["VW"]
    h = cfg["H"]
    ro = row_off // tr

    def body(rin, cin, rout, acc):
        dq = rin[0][...] + rin[1][...]
        dk_ = rin[2][...] + rin[3][...]
        zl = rin[10][...]
        cq, sq, ck, sk = [jnp.concatenate([rin[11 + t][...]] * h, axis=1) for t in range(4)]
        rout[0][...] = (dq * cq + _swap_halves(dq * sq, kw)).astype(BF16)
        rout[1][...] = (dk_ * ck + _swap_halves(dk_ * sk, kw)).astype(BF16)
        rout[2][...] = (rin[8][...] + rin[9][...]).astype(BF16)
        dzl = jnp.zeros(zl.shape, F32)
        for t, w in ((0, cin[0]), (1, cin[1])):
            la = rin[6 + t][...]
            da = rin[4 + t][...] * ((1.0 - jnp.exp(la * GLA_TAU)) * (1.0 / GLA_TAU))
            dzl = dzl + _nt(da, w[...])
            acc[t][...] += _tn(zl, da)
            acc[2 + t][...] += _colsum(da)
        rout[3][...] = dzl.astype(BF16)

    row_ins = [(gf[0], kw, 0, 0), (gb[0], kw, 0, 0), (gf[1], kw, 0, 0), (gb[1], kw, 0, 0),
               (gf[3], kw, 0, 0), (gb[3], kw, 0, 0), (la_f, kw, ro, 0), (la_b, kw, ro, 0),
               (gf[2], vw, 0, 0), (gb[2], vw, 0, 0), (z, LANES, ro, cfg["L0"] // LANES)]
    row_ins += [(t, LANES, ro, 0) for t in tabs]
    return _rows_call(body, name=name, nblk=nrows // tr, tr=tr, row_ins=row_ins, consts=[wdf_pad, wdb_pad],
                      row_outs=[(nrows, kw, BF16), (nrows, kw, BF16), (nrows, vw, BF16), (nrows, LANES, BF16)],
                      accs=[(LANES, kw), (LANES, kw), (1, kw), (1, kw)])


def _readout_fwd(o_f, o_b, z, g, cfg, *, tr, sides=_NO_SIDES):
    n, vw = o_f.shape
    h, dv = cfg["H"], cfg["DV"]

    def body(rin, cin, rout, acc):
        for hh in range(h):
            cs = slice(hh * dv, (hh + 1) * dv)
            oh = rin[0][:, cs] + rin[1][:, cs]
            y = oh * _rstd(oh) * cin[0][:, cs]
            rout[0][:, cs] = (y * _silu(rin[2][:, cs])).astype(BF16)

    return _rows_call(body, name="gla_readout", nblk=n // tr, tr=tr,
                      row_ins=[(o_f, vw, 0, 0), (o_b, vw, 0, 0), (z, vw, 0, cfg["R0"] // vw)], consts=[g],
                      row_outs=[(n, vw, BF16)], sides=sides)


def _readout_bwd(o_f, o_b, z, dycat, g, cfg, *, tr):
    n, vw = o_f.shape
    h, dv = cfg["H"], cfg["DV"]

    def body(rin, cin, rout, acc):
        for hh in range(h):
            cs = slice(hh * dv, (hh + 1) * dv)
            oh = rin[0][:, cs] + rin[1][:, cs]
            r, dyg, gh = rin[2][:, cs], rin[3][:, cs], cin[0][:, cs]
            rs = _rstd(oh)
            dy = dyg * _silu(r)
            rout[0][:, cs] = _rms_bwd(oh, rs, dy * gh).astype(BF16)
            rout[1][:, cs] = (dyg * (oh * rs * gh) * _dsilu(r)).astype(BF16)
            acc[0][:, cs] += _colsum(dy * oh * rs)

    return _rows_call(body, name="gla_readout_bwd", nblk=n // tr, tr=tr,
                      row_ins=[(o_f, vw, 0, 0), (o_b, vw, 0, 0), (z, vw, 0, cfg["R0"] // vw), (dycat, vw, 0, 0)],
                      consts=[g], row_outs=[(n, vw, BF16), (n, vw, BF16)], accs=[(1, vw)])


def _sg_ln(vv):
    mu = jnp.mean(vv, axis=-1, keepdims=True)
    cen = vv - mu
    rstd = lax.rsqrt(jnp.mean(cen * cen, axis=-1, keepdims=True) + EPS)
    return cen * rstd, rstd


def _sg_fwd(z, n, lng, lnb, w_s, bs_full, cfg, *, sides=_NO_SIDES):
    sgw, grp, sc = cfg["SGW"], cfg["SG_G"], cfg["SG_C"]
    gw = sgw // grp

    def body(rin, cin, rout, acc):
        u = _gelu(rin[0][...])
        xhat, _ = _sg_ln(_gelu(rin[1][...]))
        vvn = xhat * cin[0][...] + cin[1][...]
        for gg in range(grp):
            cs = slice(gg * gw, (gg + 1) * gw)
            s = _nn(cin[2][gg], vvn[:, cs]) + cin[3][:, cs]
            rout[0][:, cs] = (u[:, cs] * s).astype(BF16)

    return _rows_call(body, name="sg_fwd", nblk=n // sc, tr=sc,
                      row_ins=[(z, sgw, 0, cfg["U0"] // sgw), (z, sgw, 0, cfg["VV0"] // sgw)],
                      consts=[lng, lnb, w_s, bs_full], row_outs=[(n, sgw, BF16)], sides=sides)


def _sg_bwd(z, dycat, n, lng, lnb, w_s, bs_full, cfg):
    sgw, grp, sc = cfg["SGW"], cfg["SG_G"], cfg["SG_C"]
    gw = sgw // grp

    def body(rin, cin, rout, acc):
        up, vp, dy = rin[0][...], rin[1][...], rin[2][...]
        u = _gelu(up)
        xhat, rstd = _sg_ln(_gelu(vp))
        lng_v = cin[0][...]
        vvn = xhat * lng_v + cin[1][...]
        ds = dy * u
        acc[1][...] += ds
        dvvn_parts = []
        for gg in range(grp):
            cs = slice(gg * gw, (gg + 1) * gw)
            w = cin[2][gg]
            s = _nn(w, vvn[:, cs]) + cin[3][:, cs]
            rout[0][:, cs] = (dy[:, cs] * s * _dgelu(up[:, cs])).astype(BF16)
            acc[0][gg] += _nt(ds[:, cs], vvn[:, cs])
            dvvn_parts.append(_tn(w, ds[:, cs]))
        dvvn = jnp.concatenate(dvvn_parts, axis=1)
        acc[2][...] += _colsum(dvvn * xhat)
        acc[3][...] += _colsum(dvvn)
        dxh = dvvn * lng_v
        dvv = rstd * (dxh - jnp.mean(dxh, axis=-1, keepdims=True)
                      - xhat * jnp.mean(dxh * xhat, axis=-1, keepdims=True))
        rout[0][:, sgw:] = (dvv * _dgelu(vp)).astype(BF16)

    vw = cfg["VW"]
    return _rows_call(body, name="sg_bwd", nblk=n // sc, tr=sc,
                      row_ins=[(z, sgw, 0, cfg["U0"] // sgw), (z, sgw, 0, cfg["VV0"] // sgw),
                               (dycat, sgw, 0, vw // sgw)],
                      consts=[lng, lnb, w_s, bs_full], row_outs=[(n, 2 * sgw, BF16)],
                      accs=[(grp, sc, sc), (sc, sgw), (1, sgw), (1, sgw)])


def _mid_fwd(x, mix, g1, post1, pre2, sh2, sc2, *, tr, sides=_NO_SIDES):
    n, d = x.shape

    def body(rin, cin, rout, acc):
        xv, mv = rin[0][...], rin[1][...]
        x1 = xv + cin[0][...] * (mv * _rstd(mv) * cin[1][...])
        rout[0][...] = x1
        n2 = x1 * _rstd(x1) * cin[2][...]
        rout[1][...] = (n2 * (1.0 + cin[4][...]) + cin[3][...]).astype(BF16)

    return _rows_call(body, name="mid_fwd", nblk=n // tr, tr=tr, row_ins=[(x, d, 0, 0), (mix, d, 0, 0)],
                      consts=[g1, post1, pre2, sh2, sc2], row_outs=[(n, d, F32), (n, d, BF16)], sides=sides)


def _head_bwd(x1, m2, target, g2, post2, *, tr):
    n, d = x1.shape

    def body(rin, cin, rout, acc):
        x1v, mv, tv = rin[0][...], rin[1][...], rin[2][...]
        g2v, pg = cin[0][...], cin[1][...]
        r = _rstd(mv)
        y2 = mv * r * pg
        err = (x1v + g2v * y2) - tv
        acc[2][...] += _colsum(err * err) * (0.5 / d)
        dx2 = err * (1.0 / d)
        rout[0][...] = dx2
        dy2 = dx2 * g2v
        acc[0][...] += _colsum(dx2 * y2)
        acc[1][...] += _colsum(dy2 * mv * r)
        rout[1][...] = _rms_bwd(mv, r, dy2 * pg).astype(BF16)

    return _rows_call(body, name="head_bwd", nblk=n // tr, tr=tr,
                      row_ins=[(x1, d, 0, 0), (m2, d, 0, 0), (target, d, 0, 0)], consts=[g2, post2],
                      row_outs=[(n, d, F32), (n, d, BF16)], accs=[(1, d)] * 3)


def _mid_bwd(dh2, x1, dx2, mix, sc2, pre2, g1, post1, *, tr, sides=_NO_SIDES):
    n, d = x1.shape

    def body(rin, cin, rout, acc):
        dh, x1v, dx2v, mv = rin[0][...], rin[1][...], rin[2][...], rin[3][...]
        sc2v, pre2v, g1v, post1v = cin[0][...], cin[1][...], cin[2][...], cin[3][...]
        r2 = _rstd(x1v)
        xr = x1v * r2
        acc[0][...] += _colsum(dh)
        acc[1][...] += _colsum(dh * (xr * pre2v))
        dn2 = dh * (1.0 + sc2v)
        acc[2][...] += _colsum(dn2 * xr)
        dx1 = dx2v + _rms_bwd(x1v, r2, dn2 * pre2v)
        rout[0][...] = dx1
        r1 = _rstd(mv)
        mr = mv * r1
        acc[3][...] += _colsum(dx1 * (mr * post1v))
        dy1 = dx1 * g1v
        acc[4][...] += _colsum(dy1 * mr)
        rout[1][...] = _rms_bwd(mv, r1, dy1 * post1v).astype(BF16)

    return _rows_call(body, name="mid_bwd", nblk=n // tr, tr=tr,
                      row_ins=[(dh2, d, 0, 0), (x1, d, 0, 0), (dx2, d, 0, 0), (mix, d, 0, 0)],
                      consts=[sc2, pre2, g1, post1], row_outs=[(n, d, F32), (n, d, BF16)], accs=[(1, d)] * 5,
                      sides=sides)


def _in_bwd(da, x, dres, sc1, pre1, *, row_off, tr, name, sides=_NO_SIDES):
    n, d = x.shape
    with_res = dres is not None

    def body(rin, cin, rout, acc):
        dav, xv = rin[0][...], rin[1][...]
        sc1v, pre1v = cin[0][...], cin[1][...]
        r = _rstd(xv)
        xr = xv * r
        acc[0][...] += _colsum(dav)
        acc[1][...] += _colsum(dav * (xr * pre1v))
        dn = dav * (1.0 + sc1v)
        acc[2][...] += _colsum(dn * xr)
        if with_res:
            rout[0][...] = rin[2][...] + _rms_bwd(xv, r, dn * pre1v)

    row_ins = [(da, d, row_off // tr, 0), (x, d, 0, 0)] + ([(dres, d, 0, 0)] if with_res else [])
    return _rows_call(body, name=name, nblk=n // tr, tr=tr, row_ins=row_ins, consts=[sc1, pre1],
                      row_outs=[(n, d, F32)] if with_res else [], accs=[(1, d)] * 3, sides=sides)


def _ada_fwd(c16, w, b, *, tn):
    d, ncol = w.shape

    def body(c_ref, w_ref, b_ref, o_ref):
        o_ref[...] = _nn(_silu(c_ref[...]), w_ref[...]) + b_ref[...]

    return pl.pallas_call(
        body, name="ada_fwd", grid=(ncol // tn,),
        in_specs=[pl.BlockSpec((16, d), lambda j: (0, 0)), pl.BlockSpec((d, tn), lambda j: (0, j)),
                  pl.BlockSpec((1, tn), lambda j: (0, j))],
        out_specs=pl.BlockSpec((16, tn), lambda j: (0, j)),
        out_shape=jax.ShapeDtypeStruct((16, ncol), F32),
        compiler_params=_cparams(("arbitrary",)),
    )(c16, w, b)


def _ada_bwd(c16, dm, w, c_ctx, *, tn):
    d, ncol = w.shape

    def body(c_ref, dm_ref, w_ref, cc_ref, gw_ref, dcc_ref, acc):
        j = pl.program_id(0)

        @pl.when(j == 0)
        def _():
            acc[...] = jnp.zeros_like(acc)

        gw_ref[...] = _tn(_silu(c_ref[...]), dm_ref[...])
        acc[...] += _nt(dm_ref[...], w_ref[...])

        @pl.when(j == ncol // tn - 1)
        def _():
            dcc_ref[...] = _colsum(acc[8:16, :]) * _dsilu(cc_ref[...])

    return pl.pallas_call(
        body, name="ada_bwd", grid=(ncol // tn,),
        in_specs=[pl.BlockSpec((16, d), lambda j: (0, 0)), pl.BlockSpec((16, tn), lambda j: (0, j)),
                  pl.BlockSpec((d, tn), lambda j: (0, j)), pl.BlockSpec((1, d), lambda j: (0, 0))],
        out_specs=[pl.BlockSpec((d, tn), lambda j: (0, j)), pl.BlockSpec((1, d), lambda j: (0, 0))],
        out_shape=[jax.ShapeDtypeStruct((d, ncol), F32), jax.ShapeDtypeStruct((1, d), F32)],
        scratch_shapes=[pltpu.VMEM((16, d), F32)],
        compiler_params=_cparams(("arbitrary",)),
    )(c16, dm, w, c_ctx)


def _adam_math(w, g, m, v):
    m = ADAM_B1 * m + (1.0 - ADAM_B1) * g
    v = ADAM_B2 * v + (1.0 - ADAM_B2) * (g * g)
    m_hat = m / (1.0 - ADAM_B1 ** ADAM_STEP)
    v_hat = v / (1.0 - ADAM_B2 ** ADAM_STEP)
    delta = -ADAM_LR * (m_hat / (jnp.sqrt(v_hat) + ADAM_EPS) + ADAM_WD * w)
    return delta, m, v


def _adam_big(parts, w, m, v, *, name, tr, sides=_NO_SIDES):
    rows, cols = w.shape
    n_p = len(parts)
    n_si, n_so = len(sides.ins), len(sides.outs)
    nblk = rows // tr

    def body(*refs):
        ins, refs = refs[:n_p + 3], refs[n_p + 3:]
        side_in, refs = refs[:n_si], refs[n_si:]
        outs, side_out, side_sems = refs[:4], refs[4:4 + n_so], refs[4 + n_so:]
        i = pl.program_id(0)

        if sides.jobs:
            @pl.when(i == 0)
            def _():
                sides.run("start", side_in, side_out, side_sems)

        g = ins[0][...]
        for p in ins[1:n_p]:
            g = g + p[...].astype(F32)
        delta, m2, v2 = _adam_math(ins[n_p][...], g, ins[n_p + 1][...], ins[n_p + 2][...])
        outs[0][...] = g
        outs[1][...] = delta
        outs[2][...] = m2
        outs[3][...] = v2

        if sides.jobs:
            @pl.when(i == nblk - 1)
            def _():
                sides.run("finish", side_in, side_out, side_sems)

    plain = pl.BlockSpec((tr, cols), lambda i: (i, 0))
    in_specs = []
    for arr, idx in parts:
        if idx is None:
            in_specs.append(plain)
        else:
            in_specs.append(pl.BlockSpec((None, tr, cols), lambda i, idx=idx: (idx, i, 0)))
    in_specs += [plain] * 3
    return pl.pallas_call(
        body, name=name, grid=(nblk,), in_specs=in_specs + [ANY_SPEC] * n_si,
        out_specs=[plain] * 4 + [ANY_SPEC] * n_so,
        out_shape=[jax.ShapeDtypeStruct((rows, cols), F32)] * 4 + list(sides.outs),
        scratch_shapes=list(sides.sems), input_output_aliases=sides.aliases(n_p + 3, 4),
        compiler_params=_cparams(("arbitrary",) if sides.jobs else ("parallel",)),
    )(*[p[0] for p in parts], w, m, v, *sides.ins)


def _sum_parts(parts, rows, cols, *, name, tr):
    def body(*refs):
        g = refs[0][...].astype(F32)
        for p in refs[1:-1]:
            g = g + p[...].astype(F32)
        refs[-1][...] = g

    in_specs = [pl.BlockSpec((None, tr, cols), lambda i, idx=idx: (idx, i, 0)) for _, idx in parts]
    return pl.pallas_call(
        body, name=name, grid=(rows // tr,), in_specs=in_specs, out_specs=pl.BlockSpec((tr, cols), lambda i: (i, 0)),
        out_shape=jax.ShapeDtypeStruct((rows, cols), F32), compiler_params=_cparams(("parallel",)),
    )(*[p[0] for p in parts])


def _adam_small(g8, w, m, v):
    def body(g_ref, w_ref, m_ref, v_ref, go, do, mo, vo):
        g = g_ref[0]
        for r in range(1, N_DEV):
            g = g + g_ref[r]
        delta, m2, v2 = _adam_math(w_ref[...], g, m_ref[...], v_ref[...])
        go[...] = g
        do[...] = delta
        mo[...] = m2
        vo[...] = v2

    return pl.pallas_call(
        body, name="adam_small", out_shape=[jax.ShapeDtypeStruct(w.shape, F32)] * 4,
        compiler_params=pltpu.CompilerParams(vmem_limit_bytes=VMEM_LIMIT),
    )(g8, w, m, v)


VEC_W = 1024
TK = 2048
ELEMS_PER_BLOCK = 256 * 1024


def _dense(v):
    a, k = v.shape
    kp = -(-k // (8 * VEC_W)) * (8 * VEC_W)
    return jnp.pad(v, ((0, 0), (0, kp - k))).reshape(a, kp // VEC_W, VEC_W)


def _all_gather_vec(v, *, name):
    k = v.shape[1]
    return _all_gather_small(_dense(v)[0], name=name).reshape(N_DEV, -1)[:, :k]


def _my_pos():
    return lax.axis_index("x"), lax.axis_index("y"), lax.axis_index("c")


def _flip(v, bit):
    return (1 - v) if bit else v


def _all_gather_small(v, *, name):
    r, k = v.shape

    def body(v_ref, out_ref, send, recv, lsem):
        x, y, c = _my_pos()
        me = 4 * x + 2 * y + c
        local = pltpu.make_async_copy(v_ref, out_ref.at[me], lsem)
        local.start()
        sends = []
        for kk in range(1, N_DEV):
            peer = (_flip(x, kk & 4), _flip(y, kk & 2), _flip(c, kk & 1))
            cp = pltpu.make_async_remote_copy(src_ref=v_ref, dst_ref=out_ref.at[me], send_sem=send.at[kk - 1],
                                              recv_sem=recv.at[kk - 1], device_id=peer, device_id_type=MESH)
            cp.start()
            sends.append(cp)
        for kk in range(1, N_DEV):
            px, py, pc = _flip(x, kk & 4), _flip(y, kk & 2), _flip(c, kk & 1)
            src = 4 * px + 2 * py + pc
            pltpu.make_async_remote_copy(src_ref=v_ref, dst_ref=out_ref.at[src], send_sem=send.at[kk - 1],
                                         recv_sem=recv.at[kk - 1], device_id=(px, py, pc),
                                         device_id_type=MESH).wait_recv()
        for cp in sends:
            cp.wait_send()
        local.wait()

    return pl.pallas_call(
        body, name=name, out_shape=jax.ShapeDtypeStruct((N_DEV, r, k), v.dtype),
        in_specs=[pl.BlockSpec(memory_space=pltpu.VMEM)], out_specs=pl.BlockSpec(memory_space=pltpu.VMEM),
        scratch_shapes=[pltpu.SemaphoreType.DMA((N_DEV - 1,)), pltpu.SemaphoreType.DMA((N_DEV - 1,)),
                        pltpu.SemaphoreType.DMA],
        compiler_params=pltpu.CompilerParams(vmem_limit_bytes=VMEM_LIMIT),
    )(v)


def _ag_job(shards, rows=None, chained=None):
    n_arr = len(shards)

    def part(ref):
        return ref if rows is None else ref.at[pl.ds(rows[0], rows[1])]

    def tools(ins, outs, sems):
        send, recv, lsem = sems
        x, y, c = _my_pos()
        chips = [(1 - x, y), (x, 1 - y), (1 - x, 1 - y)]

        def copy(a, kk, block, to, src=None):
            dst = part(outs[a].at[4 * block[0] + 2 * block[1] + block[2]])
            return pltpu.make_async_remote_copy(src_ref=dst if src is None else part(src), dst_ref=dst,
                                                send_sem=send.at[a, kk], recv_sem=recv.at[a, kk],
                                                device_id=to, device_id_type=MESH)

        locals_ = [pltpu.make_async_copy(part(ins[a]), part(outs[a].at[4 * x + 2 * y + c]), lsem.at[a])
                   for a in range(n_arr)]
        firsts = []
        for a in range(n_arr):
            firsts.append(copy(a, 0, (x, y, c), (x, y, 1 - c), src=ins[a]))
            firsts += [copy(a, 1 + j, (x, y, c), (*chip, c), src=ins[a]) for j, chip in enumerate(chips)]
        return copy, locals_, firsts, chips, (x, y, c)

    def start(ins, outs, sems):
        _, locals_, firsts, _, _ = tools(ins, outs, sems)
        for cp in locals_ + firsts:
            cp.start()

    def finish(ins, outs, sems):
        copy, locals_, firsts, chips, (x, y, c) = tools(ins, outs, sems)
        me, sibling = (x, y, c), (x, y, 1 - c)
        passed = []
        for a in range(n_arr):
            for j, chip in enumerate(chips):
                copy(a, 1 + j, (*chip, c), me).wait_recv()
                fw = copy(a, 4 + j, (*chip, c), sibling)
                fw.start()
                passed.append(fw)
        for a in range(n_arr):
            copy(a, 0, sibling, me).wait_recv()
            for j, chip in enumerate(chips):
                copy(a, 4 + j, (*chip, 1 - c), me).wait_recv()
        for cp in firsts + passed:
            cp.wait_send()
        for lc in locals_:
            lc.wait()

    job = dict(ins=list(shards), outs=[jax.ShapeDtypeStruct((N_DEV,) + s.shape, s.dtype) for s in shards],
               sems=[pltpu.SemaphoreType.DMA((n_arr, 7)), pltpu.SemaphoreType.DMA((n_arr, 7)),
                     pltpu.SemaphoreType.DMA((n_arr,))], start=start, finish=finish)
    if chained is not None:
        job["ins"] = list(shards) + list(chained)
        job["alias"] = {n_arr + a: a for a in range(n_arr)}
    return job


def _ag3_job(shard, rows=None, chained=None):
    r0, nr = rows if rows is not None else (0, shard.shape[0])
    ha = (nr // 2) // 16 * 16
    halves = [(r0, ha), (r0 + ha, nr - ha)]

    def tools(ins, outs, sems):
        send, recv, lsem = sems
        x, y, c = _my_pos()
        me, sib = (x, y, c), (x, y, 1 - c)
        xn, yn, dg = (1 - x, y, c), (x, 1 - y, c), (1 - x, 1 - y, c)
        other = lambda p: (p[0], p[1], 1 - p[2])

        def blk(p, span=(r0, nr)):
            return outs[0].at[4 * p[0] + 2 * p[1] + p[2]].at[pl.ds(span[0], span[1])]

        mine = ins[0].at[pl.ds(r0, nr)]
        plan = [(mine, blk(me), sib, blk(sib)), (mine, blk(me), xn, blk(xn)), (mine, blk(me), yn, blk(yn)),
                (blk(xn, halves[0]), blk(xn, halves[0]), yn, blk(dg, halves[0])),
                (blk(yn, halves[1]), blk(yn, halves[1]), xn, blk(dg, halves[1])),
                (blk(xn), blk(xn), sib, blk(other(xn))), (blk(yn), blk(yn), sib, blk(other(yn))),
                (blk(dg, halves[0]), blk(dg, halves[0]), sib, blk(other(dg), halves[0])),
                (blk(dg, halves[1]), blk(dg, halves[1]), sib, blk(other(dg), halves[1]))]
        sends = [pltpu.make_async_remote_copy(src_ref=s, dst_ref=d, send_sem=send.at[k], recv_sem=recv.at[k],
                                              device_id=to, device_id_type=MESH)
                 for k, (s, d, to, _) in enumerate(plan)]
        recvs = [pltpu.make_async_remote_copy(src_ref=got, dst_ref=got, send_sem=send.at[k], recv_sem=recv.at[k],
                                              device_id=me, device_id_type=MESH)
                 for k, (_, _, _, got) in enumerate(plan)]
        local = pltpu.make_async_copy(mine, blk(me), lsem)
        return sends, recvs, local

    def start(ins, outs, sems):
        sends, _, local = tools(ins, outs, sems)
        local.start()
        for k in (0, 1, 2):
            sends[k].start()

    def middle(ins, outs, sems):
        sends, recvs, _ = tools(ins, outs, sems)
        recvs[1].wait_recv()
        sends[3].start()
        sends[5].start()
        recvs[2].wait_recv()
        sends[4].start()
        sends[6].start()

    def finish(ins, outs, sems):
        sends, recvs, local = tools(ins, outs, sems)
        recvs[3].wait_recv()
        sends[7].start()
        recvs[4].wait_recv()
        sends[8].start()
        for k in (0, 5, 6, 7, 8):
            recvs[k].wait_recv()
        for cp in sends:
            cp.wait_send()
        local.wait()

    job = dict(ins=[shard], outs=[jax.ShapeDtypeStruct((N_DEV,) + shard.shape, shard.dtype)],
               sems=[pltpu.SemaphoreType.DMA((9,)), pltpu.SemaphoreType.DMA((9,)), pltpu.SemaphoreType.DMA],
               start=start, middle=middle, finish=finish)
    if chained is not None:
        job["ins"] = [shard] + list(chained)
        job["alias"] = {1: 0}
    return job


def _exchange_job(arrays, n_slots, out_slots, src_of, dst_of, peer_of, rows=None, chained=None):
    n_arr = len(arrays)

    def copies(ins, outs, sems):
        send, recv = sems
        x, y, c = _my_pos()
        res = []
        for a in range(n_arr):
            for s in range(n_slots):
                src, dst = ins[a].at[src_of(s, x, y, c)], outs[a].at[dst_of(s)]
                if rows is not None:
                    src, dst = src.at[pl.ds(rows[0], rows[1])], dst.at[pl.ds(rows[0], rows[1])]
                res.append(pltpu.make_async_remote_copy(
                    src_ref=src, dst_ref=dst, send_sem=send.at[a, s], recv_sem=recv.at[a, s],
                    device_id=peer_of(s, x, y, c), device_id_type=MESH))
        return res

    def start(ins, outs, sems):
        for cp in copies(ins, outs, sems):
            cp.start()

    def finish(ins, outs, sems):
        cps = copies(ins, outs, sems)
        for cp in cps:
            cp.wait_recv()
        for cp in cps:
            cp.wait_send()

    job = dict(ins=list(arrays), outs=[jax.ShapeDtypeStruct((out_slots,) + g.shape[1:], g.dtype) for g in arrays],
               sems=[pltpu.SemaphoreType.DMA((n_arr, n_slots)), pltpu.SemaphoreType.DMA((n_arr, n_slots))],
               start=start, finish=finish)
    if chained is not None:
        job["ins"] = list(arrays) + list(chained)
        job["alias"] = {n_arr + a: a for a in range(n_arr)}
    return job


def _pair_job(grads):
    return _exchange_job(
        grads, 4, 4,
        src_of=lambda s, x, y, c: 4 * _flip(x, s & 2) + 2 * _flip(y, s & 1) + (1 - c),
        dst_of=lambda s: s, peer_of=lambda s, x, y, c: (x, y, 1 - c))


def _chip_job(sums, rows=None, chained=None):
    return _exchange_job(
        sums, 3, 3, src_of=lambda s, x, y, c: s, dst_of=lambda s: s,
        peer_of=lambda s, x, y, c: (_flip(x, (s + 1) & 2), _flip(y, (s + 1) & 1), c), rows=rows, chained=chained)


def _a2a_job(x):
    def copies(ins, outs, sems):
        send, recv, lsem = sems
        x_, y_, c_ = _my_pos()
        me = 4 * x_ + 2 * y_ + c_
        local = pltpu.make_async_copy(ins[0].at[me], outs[0].at[me], lsem)
        res = []
        for s in range(1, N_DEV):
            px, py, pc = _flip(x_, s & 4), _flip(y_, s & 2), _flip(c_, s & 1)
            res.append(pltpu.make_async_remote_copy(
                src_ref=ins[0].at[4 * px + 2 * py + pc], dst_ref=outs[0].at[me], send_sem=send.at[s - 1],
                recv_sem=recv.at[s - 1], device_id=(px, py, pc), device_id_type=MESH))
        return local, res

    def start(ins, outs, sems):
        local, res = copies(ins, outs, sems)
        local.start()
        for cp in res:
            cp.start()

    def finish(ins, outs, sems):
        local, res = copies(ins, outs, sems)
        for cp in res:
            cp.wait_recv()
        for cp in res:
            cp.wait_send()
        local.wait()

    return dict(ins=[x], outs=[jax.ShapeDtypeStruct(x.shape, x.dtype)],
                sems=[pltpu.SemaphoreType.DMA((N_DEV - 1,)), pltpu.SemaphoreType.DMA((N_DEV - 1,)),
                      pltpu.SemaphoreType.DMA], start=start, finish=finish)


def _run_sides(sides, *, name):
    n_si, n_so = len(sides.ins), len(sides.outs)

    def body(*refs):
        ins, outs, sems = refs[:n_si], refs[n_si:n_si + n_so], refs[n_si + n_so:]
        sides.run("start", ins, outs, sems)
        sides.run("finish", ins, outs, sems)

    return pl.pallas_call(
        body, name=name, out_shape=list(sides.outs), in_specs=[ANY_SPEC] * n_si, out_specs=[ANY_SPEC] * n_so,
        scratch_shapes=list(sides.sems), input_output_aliases=sides.aliases(0, 0),
    )(*sides.ins)


def _pair_add(g, t, *, name, tr, wire):
    _, r, cols = g.shape
    g4 = g.reshape(4, 2, r, cols)
    j0 = 1 if wire else 0

    def g_index(j, i):
        x, y, c = _my_pos()
        return (jnp.bitwise_xor(2 * x + y, j + j0), c, i, 0)

    def body(g_ref, t_ref, o_ref):
        o_ref[...] = (g_ref[...] + t_ref[...]).astype(o_ref.dtype)

    return pl.pallas_call(
        body, name=name, grid=(3 if wire else 1, r // tr),
        in_specs=[pl.BlockSpec((None, None, tr, cols), g_index),
                  pl.BlockSpec((None, tr, cols), lambda j, i: (j + j0, i, 0))],
        out_specs=pl.BlockSpec((None, tr, cols), lambda j, i: (j, i, 0)),
        out_shape=jax.ShapeDtypeStruct((3 if wire else 1, r, cols), BF16 if wire else F32),
        compiler_params=_cparams(("arbitrary", "arbitrary")),
    )(g4, t)


def _config(x, ctx, w_in, w_dec_f, gla_norm_g, sg_ln_g, w_s):
    n, d = x.shape[1], x.shape[2]
    tc = ctx.shape[1]
    h = gla_norm_g.shape[1]
    dv = gla_norm_g.shape[2] * N_DEV
    dk = dv // 2
    kw, vw = h * dk, h * dv
    lr = w_dec_f.shape[1]
    sgw = sg_ln_g.shape[1]
    cfg = dict(N=n, D=d, TC=tc, H=h, DV=dv, DK=dk, KW=kw, VW=vw, LR=lr, SGW=sgw, SG_G=w_s.shape[1],
               SG_C=w_s.shape[2], IN=w_in.shape[2] * N_DEV)
    cfg.update(K0=kw, V0=2 * kw, R0=2 * kw + vw, L0=2 * kw + 2 * vw, ZA=2 * kw + 2 * vw + LANES)
    cfg.update(U0=0, VV0=sgw, ZB=2 * sgw)
    assert dk == LANES and vw == 2 * kw and 2 * lr <= LANES
    assert cfg["R0"] % vw == 0 and vw % sgw == 0
    assert cfg["IN"] == 2 * kw + 2 * vw + 2 * lr + 2 * sgw
    return cfg


def _rope_tables(cfg):
    n, tc, dk = cfg["N"], cfg["TC"], cfg["DK"]
    m = dk // 4
    pos = jnp.arange(n)
    inv = ROPE_BASE ** (-jnp.arange(m, dtype=F32) / m)
    ang_r = (pos // GRID_W).astype(F32)[:, None] * inv[None, :]
    ang_c = (pos % GRID_W).astype(F32)[:, None] * inv[None, :]
    cos = jnp.concatenate([jnp.cos(ang_r)] * 2 + [jnp.cos(ang_c)] * 2, axis=1)
    sin = jnp.concatenate([-jnp.sin(ang_r), jnp.sin(ang_r), -jnp.sin(ang_c), jnp.sin(ang_c)], axis=1)
    scale = dk ** -0.5
    z = jnp.zeros((tc, dk), F32)
    one = jnp.ones((tc, dk), F32)
    return [jnp.concatenate([cos * scale, z]), jnp.concatenate([sin * scale, z]),
            jnp.concatenate([cos, one]), jnp.concatenate([sin, z])]


def _pair_sums(g, t, nm):
    rows_for = _tile(g.shape[1], max(8, ELEMS_PER_BLOCK // g.shape[2]), 16)
    return (_pair_add(g, t, name="rs_own_" + nm, tr=rows_for, wire=False),
            _pair_add(g, t, name="rs_wire_" + nm, tr=rows_for, wire=True))


def _local_step(x, ctx, target, mods, c_mods, w, cfg):
    n, d, tc = cfg["N"], cfg["D"], cfg["TC"]
    kw, vw, sgw, za, zb, lr = cfg["KW"], cfg["VW"], cfg["SGW"], cfg["ZA"], cfg["ZB"], cfg["LR"]
    sh1, sc1, g1, sh2, sc2, g2 = mods
    csh1, csc1 = c_mods
    rt = n + tc
    tb = math.gcd(256, math.gcd(n, tc))
    tr = math.gcd(128, tb)
    tr_s = math.gcd(128, tb)
    fs = w["sh_1"].shape[1]
    ff = fs * N_DEV
    cs_in = w["sh_in"].shape[1]
    r8 = d // N_DEV

    rows_in = _run_sides(_Sides([_a2a_job(w["sh_in"].reshape(N_DEV, r8, cs_in))]), name="a2a_w_in")[0]
    rows_in = rows_in.transpose(1, 0, 2).reshape(r8, cfg["IN"])
    lf0 = 2 * kw + 2 * vw
    sg0 = lf0 + 2 * lr
    wa_rows = jnp.concatenate([rows_in[:, :sg0], jnp.zeros((r8, LANES - 2 * lr), BF16)], axis=1)
    wb_rows = rows_in[:, sg0:]
    hx, w_a = _norm_mod(x, w["pre1_g"], sh1, sc1, name="in_norm_x", tr=tr, sides=_Sides([_ag_job([wa_rows])]))
    w_a = w_a.reshape(d, za)
    (hc,) = _norm_mod(ctx, w["pre1_g"], csh1, csc1, name="in_norm_ctx", tr=tr)
    a_all = jnp.concatenate([hx, hc], axis=0)

    tm_a = _tile(rt, 1152, 16)
    tm_n = _tile(n, 1024, 16)
    sh_o_rows = w["sh_o"].shape[0]
    o_cut = (sh_o_rows * 3 // 4) // 16 * 16
    z, w_b, wg_o = _matmul(
        a_all, w_a, "nn", rt, za, d, tm=tm_a, tn=_tile(za, 1152, LANES), tk=_tile(d, TK, LANES), name="mm_in_a",
        out_shapes=[jax.ShapeDtypeStruct((rt, za), F32)],
        sides=_Sides([_ag3_job(wb_rows), _ag3_job(w["sh_o"], rows=(0, o_cut))]))
    w_b = w_b.reshape(d, zb)
    s16 = d // 16
    w1_rows = lambda start, count: (start * s16, count * s16)
    z_b, wg_o, w_1 = _matmul(
        a_all, w_b, "nn", n, zb, d, tm=tm_n, tn=_tile(zb, 1024, LANES), tk=_tile(d, TK, LANES), name="mm_in_b",
        out_shapes=[jax.ShapeDtypeStruct((n, zb), F32)],
        sides=_Sides([_ag3_job(w["sh_o"], rows=(o_cut, sh_o_rows - o_cut), chained=[wg_o]),
                      _ag3_job(w["sh_1"], rows=w1_rows(0, 4))]))
    w_o = wg_o.reshape(d, d)

    tabs = _rope_tables(cfg)
    qr, kr, la_f, la_b, w_1 = _gla_prep(
        z, tabs, w["wdf_pad"], w["wdb_pad"], w["b_dec_f"], w["b_dec_b"], cfg, tr=tr,
        sides=_Sides([_ag_job([w["sh_1"]], rows=w1_rows(4, 1), chained=[w_1])]))

    zero_st = jnp.zeros((cfg["H"], cfg["DV"], cfg["DK"]), F32)
    gla = dict(cfg=cfg, tb=tb)
    _, save_cf, st_cf = _gla_fwd(qr, kr, z, la_f, zero_st, rev=False, row_off=n, nrows=tc, name="gla_ctx_f", **gla)
    _, save_cb, st_cb = _gla_fwd(qr, kr, z, la_b, zero_st, rev=True, row_off=n, nrows=tc, name="gla_ctx_b", **gla)
    o_f, save_f, _, w_1 = _gla_fwd(
        qr, kr, z, la_f, st_cf, rev=False, row_off=0, nrows=n, name="gla_f",
        sides=_Sides([_ag_job([w["sh_1"]], rows=w1_rows(5, 2), chained=[w_1])]), **gla)
    o_b, save_b, _, w_1 = _gla_fwd(
        qr, kr, z, la_b, st_cb, rev=True, row_off=0, nrows=n, name="gla_b",
        sides=_Sides([_ag_job([w["sh_1"]], rows=w1_rows(7, 2), chained=[w_1])]), **gla)
    y_gla, w_1 = _readout_fwd(o_f, o_b, z, w["gla_g"], cfg, tr=tr,
                              sides=_Sides([_ag_job([w["sh_1"]], rows=w1_rows(9, 1), chained=[w_1])]))
    y_sg, w_1 = _sg_fwd(z_b, n, w["sg_ln_g"], w["sg_ln_b"], w["w_s"], w["bs_full"], cfg,
                        sides=_Sides([_ag_job([w["sh_1"]], rows=w1_rows(10, 1), chained=[w_1])]))
    ycat = jnp.concatenate([y_gla, y_sg], axis=1)

    mix, w_1 = _matmul(ycat, w_o, "nn", n, d, d, tm=tm_n, tn=_tile(d, 1024, LANES), tk=_tile(d, TK, LANES),
                       name="mm_o", out_shapes=[jax.ShapeDtypeStruct((n, d), F32)],
                       sides=_Sides([_ag3_job(w["sh_1"], rows=w1_rows(11, 5), chained=[w_1])]))
    w2_cut = (fs // 8) // 16 * 16
    x1, h2, wg_2 = _mid_fwd(x, mix, g1, w["post1_g"], w["pre2_g"], sh2, sc2, tr=tr_s,
                            sides=_Sides([_ag_job([w["sh_2"]], rows=(0, w2_cut))]))

    tn_f = _tile(fs, 1024, LANES)
    tk_d = _tile(d, TK, LANES)

    def relu2(acc):
        return acc, jnp.square(jnp.maximum(acc, 0.0))

    a1, p1, wg_2 = _matmul(h2, w_1, "nn", n, ff, d, tm=tm_n, tn=tn_f, tk=tk_d, name="mm_1",
                           b_spec=_blocked_b_nn(fs, tk_d, tn_f), epilogue=relu2,
                           out_shapes=[jax.ShapeDtypeStruct((n, ff), BF16)] * 2,
                           sides=_Sides([_ag3_job(w["sh_2"], rows=(w2_cut, fs - w2_cut), chained=[wg_2])]))
    w_2 = wg_2.reshape(ff, d)
    tk_f = _tile(ff, TK, LANES)
    m2 = _matmul(p1, w_2, "nn", n, d, ff, tm=tm_n, tn=_tile(d, 1024, LANES), tk=tk_f, name="mm_2",
                 out_shapes=[jax.ShapeDtypeStruct((n, d), F32)])[0]

    dx2, dm2, dg2, dpost2, lossc = _head_bwd(x1, m2, target, g2, w["post2_g"], tr=tr_s)

    def drelu2(acc, a):
        return (acc * (2.0 * jnp.maximum(a.astype(F32), 0.0)),)

    da1 = _matmul(dm2, w_2, "nt", n, ff, d, tm=tm_n, tn=_tile(ff, 1024, LANES), tk=tk_d, name="mm_2_dx",
                  epilogue=drelu2, extras=(a1,), out_shapes=[jax.ShapeDtypeStruct((n, ff), BF16)])[0]
    tk_n = _tile(n, TK, 16)
    tm_d = _tile(d, 1024, LANES)
    g_1 = _matmul(h2, da1, "tn", d, ff, n, tm=tm_d, tn=tn_f, tk=tk_n, name="mm_1_dw",
                  out_specs=[_blocked_out(fs, tm_d, tn_f)],
                  out_shapes=[jax.ShapeDtypeStruct((N_DEV, d, fs), F32)])[0]
    dw_2, t_1 = _matmul(p1, dm2, "tn", ff, d, n, tm=_tile(ff, 1024, LANES), tn=_tile(d, 1024, LANES), tk=tk_n,
                        name="mm_2_dw", out_shapes=[jax.ShapeDtypeStruct((ff, d), F32)],
                        sides=_Sides([_pair_job([g_1])]))
    g_2 = dw_2.reshape(N_DEV, fs, d)
    p1_own, p1_wire = _pair_sums(g_1, t_1, "w_1")
    tk_fs = _tile(fs, TK, LANES)
    dh2, u_1, t_2 = _matmul(da1, w_1, "nt", n, d, ff, tm=tm_n, tn=_tile(d, 1024, LANES), tk=tk_fs, name="mm_1_dx",
                            b_spec=_blocked_b_nt(fs, _tile(d, 1024, LANES), tk_fs),
                            out_shapes=[jax.ShapeDtypeStruct((n, d), F32)],
                            sides=_Sides([_chip_job([p1_wire], rows=(0, d * 13 // 16)), _pair_job([g_2])]))
    p2_own, p2_wire = _pair_sums(g_2, t_2, "w_2")
    c2 = [0] + [(fs * f // 64) // 16 * 16 for f in (20, 35, 50, 57)] + [fs]
    piece2 = lambda i: (c2[i], c2[i + 1] - c2[i])

    dx1, dmix, dsh2, dsc2, dpre2, dg1, dpost1, u_1 = _mid_bwd(
        dh2, x1, dx2, mix, sc2, w["pre2_g"], g1, w["post1_g"], tr=tr_s,
        sides=_Sides([_chip_job([p1_wire], rows=(d * 13 // 16, d - d * 13 // 16), chained=[u_1])]))
    dw_o, u_2 = _matmul(ycat, dmix, "tn", d, d, n, tm=tm_d, tn=_tile(d, 1024, LANES), tk=tk_n, name="mm_o_dw",
                        out_shapes=[jax.ShapeDtypeStruct((d, d), F32)],
                        sides=_Sides([_chip_job([p2_wire], rows=piece2(0))]))
    g_o = dw_o.reshape(N_DEV, r8, d)
    dycat, t_o, u_2 = _matmul(dmix, w_o, "nt", n, d, d, tm=tm_n, tn=_tile(d, 1024, LANES), tk=tk_d, name="mm_o_dx",
                              out_shapes=[jax.ShapeDtypeStruct((n, d), F32)],
                              sides=_Sides([_pair_job([g_o]), _chip_job([p2_wire], rows=piece2(1), chained=[u_2])]))
    po_own, po_wire = _pair_sums(g_o, t_o, "w_o")

    dz_b, dws, dbs_acc, dlng, dlnb = _sg_bwd(z_b, dycat, n, w["sg_ln_g"], w["sg_ln_b"], w["w_s"], w["bs_full"], cfg)
    dw_b, u_2 = _matmul(a_all, dz_b, "tn", d, zb, n, tm=tm_d, tn=_tile(zb, 1024, LANES), tk=tk_n, name="mm_in_dw_b",
                        out_shapes=[jax.ShapeDtypeStruct((d, zb), F32)],
                        sides=_Sides([_chip_job([p2_wire], rows=piece2(2), chained=[u_2])]))
    g_b = dw_b.reshape(N_DEV, r8, zb)
    do, dzr, dgla_g = _readout_bwd(o_f, o_b, z, dycat, w["gla_g"], cfg, tr=tr)

    *gf, t_b, u_2 = _gla_bwd(
        qr, kr, z, la_f, do, save_f, zero_st, rev=False, row_off=0, nrows=n, name="gla_f_bwd",
        sides=_Sides([_pair_job([g_b]), _chip_job([p2_wire], rows=piece2(3), chained=[u_2])]), **gla)
    pb_own, pb_wire = _pair_sums(g_b, t_b, "w_in_b")
    *gb, u_2 = _gla_bwd(qr, kr, z, la_b, do, save_b, zero_st, rev=True, row_off=0, nrows=n, name="gla_b_bwd",
                        sides=_Sides([_chip_job([p2_wire], rows=piece2(4), chained=[u_2])]), **gla)
    do_c = jnp.zeros((tc, vw), BF16)
    gcf = _gla_bwd(qr, kr, z, la_f, do_c, save_cf, gf[4], rev=False, row_off=n, nrows=tc, name="gla_ctx_f_bwd",
                   **gla)
    gcb = _gla_bwd(qr, kr, z, la_b, do_c, save_cb, gb[4], rev=True, row_off=n, nrows=tc, name="gla_ctx_b_bwd",
                   **gla)

    post = dict(la_f=la_f, la_b=la_b, z=z, tabs=tabs, wdf_pad=w["wdf_pad"], wdb_pad=w["wdb_pad"], cfg=cfg, tr=tr)
    dzq, dzk, dzv, dzl, dwdf, dwdb, dbdf, dbdb = _gla_post(gf, gb, row_off=0, nrows=n, name="gla_post", **post)
    czq, czk, czv, czl, cwdf, cwdb, cbdf, cbdb = _gla_post(gcf, gcb, row_off=n, nrows=tc, name="gla_post_ctx",
                                                           **post)
    dz_a = jnp.concatenate([
        jnp.concatenate([dzq, dzk, dzv, dzr, dzl], axis=1),
        jnp.concatenate([czq, czk, czv, jnp.zeros((tc, vw), BF16), czl], axis=1)], axis=0)

    dw_a, u_b = _matmul(
        a_all, dz_a, "tn", d, za, rt, tm=tm_d, tn=_tile(za, 1152, LANES), tk=_tile(rt, 2176, 16), name="mm_in_dw_a",
        out_shapes=[jax.ShapeDtypeStruct((d, za), F32)], sides=_Sides([_chip_job([pb_wire])]))
    g_a = dw_a.reshape(N_DEV, r8, za)
    da_a, t_a, u_o = _matmul(dz_a, w_a, "nt", rt, d, za, tm=_tile(rt, 576, 16), tn=_tile(d, 512, LANES),
                             tk=za, name="mm_in_dx_a", out_shapes=[jax.ShapeDtypeStruct((rt, d), F32)],
                             sides=_Sides([_pair_job([g_a]), _chip_job([po_wire])]))
    pa_own, pa_wire = _pair_sums(g_a, t_a, "w_in_a")
    cut_a = (r8 * 9 // 16) // 16 * 16
    tm_x = _tile(n, 512, 16)
    da_x, u_a = _matmul(dz_b, w_b, "nt", n, d, zb, tm=tm_x, tn=_tile(d, 1024, LANES), tk=_tile(zb, 4096, LANES),
                        name="mm_in_dx_b", epilogue=lambda acc, prev: (acc + prev,), extras=(da_a,),
                        out_shapes=[jax.ShapeDtypeStruct((n, d), F32)],
                        sides=_Sides([_chip_job([pa_wire], rows=(0, cut_a))]))

    grad_x, dsh1, dsc1, dpre1, u_a = _in_bwd(
        da_x, x, dx1, sc1, w["pre1_g"], row_off=0, tr=tr_s, name="in_bwd_x",
        sides=_Sides([_chip_job([pa_wire], rows=(cut_a, r8 - cut_a), chained=[u_a])]))
    dcsh1, dcsc1, dpre1_c = _in_bwd(da_a, ctx, None, csc1, w["pre1_g"], row_off=n, tr=tr_s, name="in_bwd_ctx")

    small = dict(
        pre1_g=dpre1 + dpre1_c, post1_g=dpost1, pre2_g=dpre2, post2_g=dpost2,
        w_dec_f=(dwdf + cwdf)[:lr], w_dec_b=(dwdb + cwdb)[lr:2 * lr], b_dec_f=dbdf + cbdf, b_dec_b=dbdb + cbdb,
        gla_norm_g=dgla_g, sg_ln_g=dlng, sg_ln_b=dlnb, w_s=dws,
        b_s=dbs_acc.reshape(cfg["SG_C"], cfg["SG_G"], sgw // cfg["SG_G"]).sum(-1).T)
    dmod = jnp.concatenate([dsh1, dsc1, dg1, dsh2, dsc2, dg2], axis=1)
    dmod_c = jnp.concatenate([dcsh1, dcsc1], axis=1)
    big = dict(w_in_a=(pa_own, u_a), w_in_b=(pb_own, u_b), w_o=(po_own, u_o), w_1=(p1_own, u_1), w_2=(p2_own, u_2))
    return lossc, grad_x, big, small, dmod, dmod_c


SMALL_NAMES = ["b_ada", "pre1_g", "post1_g", "pre2_g", "post2_g", "w_dec_f", "b_dec_f", "w_dec_b", "b_dec_b",
               "gla_norm_g", "sg_ln_g", "sg_ln_b", "w_s", "b_s", "c_ctx"]
WEIGHT_ORDER = ["c_ctx", "w_ada", "b_ada", "pre1_g", "post1_g", "pre2_g", "post2_g", "w_in", "w_dec_f", "b_dec_f",
                "w_dec_b", "b_dec_b", "gla_norm_g", "sg_ln_g", "sg_ln_b", "w_s", "b_s", "w_o", "w_1", "w_2"]


def kernel(x, c, ctx, c_ctx, w_ada, b_ada, pre1_g, post1_g, pre2_g, post2_g, w_in, w_dec_f, b_dec_f, w_dec_b, b_dec_b, gla_norm_g, sg_ln_g, sg_ln_b, w_s, b_s, w_o, w_1, w_2, loss_target, m_c_ctx, m_w_ada, m_b_ada, m_pre1_g, m_post1_g, m_pre2_g, m_post2_g, m_w_in, m_w_dec_f, m_b_dec_f, m_w_dec_b, m_b_dec_b, m_gla_norm_g, m_sg_ln_g, m_sg_ln_b, m_w_s, m_b_s, m_w_o, m_w_1, m_w_2, v_c_ctx, v_w_ada, v_b_ada, v_pre1_g, v_post1_g, v_pre2_g, v_post2_g, v_w_in, v_w_dec_f, v_b_dec_f, v_w_dec_b, v_b_dec_b, v_gla_norm_g, v_sg_ln_g, v_sg_ln_b, v_w_s, v_b_s, v_w_o, v_w_1, v_w_2):
    weights = dict(c_ctx=c_ctx, w_ada=w_ada, b_ada=b_ada, pre1_g=pre1_g, post1_g=post1_g, pre2_g=pre2_g,
                   post2_g=post2_g, w_in=w_in, w_dec_f=w_dec_f, b_dec_f=b_dec_f, w_dec_b=w_dec_b, b_dec_b=b_dec_b,
                   gla_norm_g=gla_norm_g, sg_ln_g=sg_ln_g, sg_ln_b=sg_ln_b, w_s=w_s, b_s=b_s, w_o=w_o, w_1=w_1,
                   w_2=w_2)
    mom_m = dict(c_ctx=m_c_ctx, w_ada=m_w_ada, b_ada=m_b_ada, pre1_g=m_pre1_g, post1_g=m_post1_g, pre2_g=m_pre2_g,
                 post2_g=m_post2_g, w_in=m_w_in, w_dec_f=m_w_dec_f, b_dec_f=m_b_dec_f, w_dec_b=m_w_dec_b,
                 b_dec_b=m_b_dec_b, gla_norm_g=m_gla_norm_g, sg_ln_g=m_sg_ln_g, sg_ln_b=m_sg_ln_b, w_s=m_w_s,
                 b_s=m_b_s, w_o=m_w_o, w_1=m_w_1, w_2=m_w_2)
    mom_v = dict(c_ctx=v_c_ctx, w_ada=v_w_ada, b_ada=v_b_ada, pre1_g=v_pre1_g, post1_g=v_post1_g, pre2_g=v_pre2_g,
                 post2_g=v_post2_g, w_in=v_w_in, w_dec_f=v_w_dec_f, b_dec_f=v_b_dec_f, w_dec_b=v_w_dec_b,
                 b_dec_b=v_b_dec_b, gla_norm_g=v_gla_norm_g, sg_ln_g=v_sg_ln_g, sg_ln_b=v_sg_ln_b, w_s=v_w_s,
                 b_s=v_b_s, w_o=v_w_o, w_1=v_w_1, w_2=v_w_2)

    cfg = _config(x, ctx, w_in, w_dec_f, gla_norm_g, sg_ln_g, w_s)
    n, d, h, dv, kw, vw, lr, sgw = (cfg[k] for k in ("N", "D", "H", "DV", "KW", "VW", "LR", "SGW"))
    dvs, kws = dv // N_DEV, kw // N_DEV
    ix, iy, ic = _my_pos()
    me = 4 * ix + 2 * iy + ic

    pack1 = jnp.concatenate([c.reshape(1, d), w_dec_f.reshape(1, lr * kws), w_dec_b.reshape(1, lr * kws),
                             gla_norm_g.reshape(1, h * dvs)], axis=1)
    g1 = _all_gather_vec(pack1, name="ag_small_in")
    c_all = g1[:, :d]
    o1 = d
    wdf = g1[:, o1:o1 + lr * kws].reshape(N_DEV, lr, kws).transpose(1, 0, 2).reshape(lr, kw)
    o1 += lr * kws
    wdb = g1[:, o1:o1 + lr * kws].reshape(N_DEV, lr, kws).transpose(1, 0, 2).reshape(lr, kw)
    o1 += lr * kws
    gla_g = g1[:, o1:o1 + h * dvs].reshape(N_DEV, h, dvs).transpose(1, 0, 2).reshape(1, h * dv)

    c16 = jnp.concatenate([c_all, jnp.broadcast_to(c_ctx.reshape(1, d), (N_DEV, d))], axis=0)
    ncol = w_ada.shape[2]
    wa = w_ada.reshape(d, ncol)
    b_mine = lax.dynamic_slice(b_ada, (0, me * ncol), (1, ncol))
    tn_ada = _tile(ncol, 512, LANES)
    mod_mine = _ada_fwd(c16, wa, b_mine, tn=tn_ada)
    mod_all = _all_gather_small(mod_mine, name="ag_mod").transpose(1, 0, 2).reshape(16, N_DEV * ncol)
    mod_b = lax.dynamic_slice(mod_all, (me, 0), (1, 6 * d))
    mods = [mod_b[:, i * d:(i + 1) * d] for i in range(6)]
    c_mods = [mod_all[N_DEV:N_DEV + 1, :d], mod_all[N_DEV:N_DEV + 1, d:2 * d]]

    zpad = lambda r: jnp.zeros((r, kw), F32)
    w = dict(
        sh_in=w_in.reshape(d, w_in.shape[2]).astype(BF16), sh_o=w_o.reshape(w_o.shape[1], d).astype(BF16),
        sh_1=w_1.reshape(d, w_1.shape[2]).astype(BF16), sh_2=w_2.reshape(w_2.shape[1], d).astype(BF16),
        pre1_g=pre1_g, post1_g=post1_g, pre2_g=pre2_g, post2_g=post2_g, b_dec_f=b_dec_f, b_dec_b=b_dec_b,
        wdf_pad=jnp.concatenate([wdf, zpad(LANES - lr)], axis=0),
        wdb_pad=jnp.concatenate([zpad(lr), wdb, zpad(LANES - 2 * lr)], axis=0),
        gla_g=gla_g, sg_ln_g=sg_ln_g, sg_ln_b=sg_ln_b, w_s=w_s[0],
        bs_full=jnp.repeat(b_s[0].T, sgw // cfg["SG_G"], axis=1))

    lossc, grad_x, big, small, dmod, dmod_c = _local_step(x[0], ctx[0], loss_target[0], mods, c_mods, w, cfg)
    loss = lax.psum(jnp.sum(lossc), AXES)

    order3 = ["pre1_g", "post1_g", "pre2_g", "post2_g", "w_dec_f", "b_dec_f", "w_dec_b", "b_dec_b", "gla_norm_g",
              "sg_ln_g", "sg_ln_b", "w_s", "b_s"]
    pieces = [dmod, dmod_c] + [small[k].reshape(1, -1) for k in order3]
    sizes = [p.shape[1] for p in pieces]
    g3 = _all_gather_vec(jnp.concatenate(pieces, axis=1), name="ag_small_grads")
    offs = [0]
    for s in sizes:
        offs.append(offs[-1] + s)
    dmod_all = g3[:, :6 * d]
    dmod_c_all = jnp.pad(g3[:, offs[1]:offs[2]], ((0, 0), (0, 4 * d)))
    parts8 = {k: g3[:, offs[2 + i]:offs[3 + i]] for i, k in enumerate(order3)}
    parts8["b_ada"] = dmod_all + dmod_c_all
    parts8["w_dec_f"] = lax.dynamic_slice(parts8["w_dec_f"].reshape(N_DEV, lr, kw), (0, 0, me * kws),
                                          (N_DEV, lr, kws)).reshape(N_DEV, -1)
    parts8["w_dec_b"] = lax.dynamic_slice(parts8["w_dec_b"].reshape(N_DEV, lr, kw), (0, 0, me * kws),
                                          (N_DEV, lr, kws)).reshape(N_DEV, -1)
    parts8["gla_norm_g"] = lax.dynamic_slice(parts8["gla_norm_g"].reshape(N_DEV, h, dv), (0, 0, me * dvs),
                                             (N_DEV, h, dvs)).reshape(N_DEV, -1)

    dm16 = jnp.concatenate([dmod_all, dmod_c_all], axis=0)
    dm_mine = lax.dynamic_slice(dm16, (0, me * ncol), (16, ncol))
    g_w_ada, dcc = _ada_bwd(c16, dm_mine, wa, c_ctx.reshape(1, d), tn=tn_ada)
    parts8["c_ctx"] = _all_gather_vec(dcc, name="ag_cctx")

    flat = lambda t: t.reshape(1, -1)
    g8 = _dense(jnp.concatenate([parts8[k] for k in SMALL_NAMES], axis=1))
    ws, ms, vs = [_dense(jnp.concatenate([flat(src[k]) for k in SMALL_NAMES], axis=1))[0]
                  for src in (weights, mom_m, mom_v)]
    res_small = [r.reshape(1, -1) for r in _adam_small(g8, ws, ms, vs)]
    out = {}
    off = 0
    for k in SMALL_NAMES:
        sz = weights[k].size
        out[k] = [r[:, off:off + sz].reshape(weights[k].shape) for r in res_small]
        off += sz

    rows_for = lambda r, cols: _tile(r, max(8, ELEMS_PER_BLOCK // cols), 16)
    r8, cs_in = d // N_DEV, w_in.shape[2]

    def adam(nm, parts, sides=_NO_SIDES):
        shp = weights[nm].shape
        r2 = (shp[1], shp[2])
        res = _adam_big(parts, weights[nm].reshape(r2), mom_m[nm].reshape(r2), mom_v[nm].reshape(r2),
                        name="adam_" + nm, tr=rows_for(*r2), sides=sides)
        out[nm] = [r.reshape(shp) for r in res[:4]]
        return res[4:]

    four = lambda own, u: [(own, 0), (u, 0), (u, 1), (u, 2)]
    red_a = _sum_parts(four(*big["w_in_a"]), r8, cfg["ZA"], name="rs_sum_w_in_a", tr=rows_for(r8, cfg["ZA"]))
    red_b = _sum_parts(four(*big["w_in_b"]), r8, cfg["ZB"], name="rs_sum_w_in_b", tr=rows_for(r8, cfg["ZB"]))
    red = jnp.concatenate([red_a[:, :cfg["L0"] + 2 * lr], red_b], axis=1)
    (g_in,) = adam("w_ada", [(g_w_ada, None)],
                   _Sides([_a2a_job(red.reshape(r8, N_DEV, cs_in).transpose(1, 0, 2))]))
    for nm in ("w_2", "w_1", "w_o"):
        adam(nm, four(*big[nm]))
    adam("w_in", [(g_in.reshape(d, cs_in), None)])

    outs = [loss, grad_x[None]]
    for i in range(4):
        outs += [out[k][i] for k in WEIGHT_ORDER]
    return tuple(outs)
```

```python
import math

import jax
import jax.numpy as jnp
from jax import lax
from jax.experimental import pallas as pl
from jax.experimental.pallas import tpu as pltpu

F32 = jnp.float32
BF16 = jnp.bfloat16
MXU_DTYPE = jnp.bfloat16
HI = lax.Precision.HIGHEST

N_DEV = 8
AXES = ("x", "y", "c")
MESH = pl.DeviceIdType.MESH
LANES = 128
VMEM_LIMIT = 56 * 1024 * 1024

EPS = 1e-6
GRID_W = 64
GLA_CHUNK = 64
GLA_TAU = 16.0
ROPE_BASE = 10000.0
ADAM_LR = 0.001
ADAM_B1 = 0.9
ADAM_B2 = 0.999
ADAM_EPS = 1e-08
ADAM_WD = 0.01
ADAM_STEP = 10


def _cparams(sem):
    return pltpu.CompilerParams(dimension_semantics=sem, vmem_limit_bytes=VMEM_LIMIT)


def _tile(n, target, align):
    if n <= target:
        return n
    best = None
    for t in range(align, target + 1, align):
        if n % t == 0:
            best = t
    assert best is not None, (n, target, align)
    return best


def _dg(a, b, dims, prec=None):
    return lax.dot_general(a, b, (dims, ((), ())), precision=prec, preferred_element_type=F32)


def _nn(a, b):
    return _dg(a.astype(MXU_DTYPE), b.astype(MXU_DTYPE), ((1,), (0,)))


def _nt(a, b):
    return _dg(a.astype(MXU_DTYPE), b.astype(MXU_DTYPE), ((1,), (1,)))


def _tn(a, b):
    return _dg(a.astype(MXU_DTYPE), b.astype(MXU_DTYPE), ((0,), (0,)))


def _sigmoid(x):
    return 1.0 / (1.0 + jnp.exp(-x))


def _silu(x):
    return x * _sigmoid(x)


def _dsilu(x):
    s = _sigmoid(x)
    return s * (1.0 + x * (1.0 - s))


def _gelu(x):
    return 0.5 * x * (1.0 + lax.erf(x * (1.0 / math.sqrt(2.0))))


def _dgelu(x):
    return 0.5 * (1.0 + lax.erf(x * (1.0 / math.sqrt(2.0)))) + x * jnp.exp(-0.5 * x * x) * (1.0 / math.sqrt(2.0 * math.pi))


def _rstd(x):
    return lax.rsqrt(jnp.mean(x * x, axis=-1, keepdims=True) + EPS)


def _rms_bwd(x, r, dn):
    return r * dn - x * (r * r * r) * jnp.mean(dn * x, axis=-1, keepdims=True)


def _colsum(x):
    return jnp.sum(x, axis=0, keepdims=True)


class _Sides:
    def __init__(self, jobs):
        self.jobs = list(jobs)
        self.ins = [a for j in self.jobs for a in j["ins"]]
        self.outs = [o for j in self.jobs for o in j["outs"]]
        self.sems = [s for j in self.jobs for s in j["sems"]]

    def aliases(self, in_base, out_base):
        res, oi, oo = {}, 0, 0
        for j in self.jobs:
            for a, b in j.get("alias", {}).items():
                res[in_base + oi + a] = out_base + oo + b
            oi += len(j["ins"])
            oo += len(j["outs"])
        return res

    def has(self, phase):
        return any(phase in j for j in self.jobs)

    def run(self, phase, in_refs, out_refs, sem_refs):
        oi = oo = os_ = 0
        for j in self.jobs:
            ni, no, ns = len(j["ins"]), len(j["outs"]), len(j["sems"])
            if phase in j:
                j[phase](in_refs[oi:oi + ni], out_refs[oo:oo + no], sem_refs[os_:os_ + ns])
            oi, oo, os_ = oi + ni, oo + no, os_ + ns


_NO_SIDES = _Sides([])
ANY_SPEC = pl.BlockSpec(memory_space=pl.ANY)


def _matmul(a, b, mode, m, n, k, *, tm, tn, tk, name, out_shapes, b_spec=None, out_specs=None,
            epilogue=None, extras=(), sides=_NO_SIDES):
    nk = k // tk
    assert m % tm == 0 and n % tn == 0 and k % tk == 0, (name, m, n, k, tm, tn, tk)
    dot = {"nn": _nn, "nt": _nt, "tn": _tn}[mode]
    if mode == "tn":
        a_spec = pl.BlockSpec((tk, tm), lambda i, j, kk: (kk, i))
    else:
        a_spec = pl.BlockSpec((tm, tk), lambda i, j, kk: (i, kk))
    if b_spec is None:
        if mode == "nt":
            b_spec = pl.BlockSpec((tn, tk), lambda i, j, kk: (j, kk))
        else:
            b_spec = pl.BlockSpec((tk, tn), lambda i, j, kk: (kk, j))
    mn_spec = pl.BlockSpec((tm, tn), lambda i, j, kk: (i, j))
    if out_specs is None:
        out_specs = [mn_spec] * len(out_shapes)
    n_extra = len(extras)
    n_out = len(out_shapes)
    n_si, n_so = len(sides.ins), len(sides.outs)
    ni, nj = m // tm, n // tn

    def body(a_ref, b_ref, *rest):
        extra_refs = rest[:n_extra]
        rest = rest[n_extra:]
        side_in, rest = rest[:n_si], rest[n_si:]
        out_refs, rest = rest[:n_out], rest[n_out:]
        side_out, rest = rest[:n_so], rest[n_so:]
        acc, side_sems = rest[0], rest[1:]
        i, j, kk = pl.program_id(0), pl.program_id(1), pl.program_id(2)

        if sides.jobs:
            @pl.when((i == 0) & (j == 0) & (kk == 0))
            def _():
                sides.run("start", side_in, side_out, side_sems)

        if sides.has("middle"):
            mid = (ni * nj * nk * 2) // 3
            mi, mj, mk = mid // (nj * nk), (mid // nk) % nj, mid % nk

            @pl.when((i == mi) & (j == mj) & (kk == mk))
            def _():
                sides.run("middle", side_in, side_out, side_sems)

        @pl.when(kk == 0)
        def _():
            acc[...] = jnp.zeros_like(acc)

        acc[...] += dot(a_ref[...], b_ref[...])

        @pl.when(kk == nk - 1)
        def _():
            vals = (acc[...],) if epilogue is None else epilogue(acc[...], *[e[...] for e in extra_refs])
            for o, v in zip(out_refs, vals):
                o[...] = v.astype(o.dtype)

        if sides.jobs:
            @pl.when((i == ni - 1) & (j == nj - 1) & (kk == nk - 1))
            def _():
                sides.run("finish", side_in, side_out, side_sems)

    sem = ("arbitrary",) * 3 if sides.jobs else ("parallel", "parallel", "arbitrary")
    res = pl.pallas_call(
        body, name=name, grid=(ni, nj, nk),
        in_specs=[a_spec, b_spec] + [mn_spec] * n_extra + [ANY_SPEC] * n_si,
        out_specs=list(out_specs) + [ANY_SPEC] * n_so, out_shape=list(out_shapes) + list(sides.outs),
        scratch_shapes=[pltpu.VMEM((tm, tn), F32)] + list(sides.sems),
        input_output_aliases=sides.aliases(2 + n_extra, n_out),
        compiler_params=_cparams(sem),
    )(a, b, *extras, *sides.ins)
    return res


def _blocked_b_nn(ns, tk, tn):
    assert ns % tn == 0
    return pl.BlockSpec((None, tk, tn), lambda i, j, kk: ((j * tn) // ns, kk, ((j * tn) % ns) // tn))


def _blocked_b_nt(ks, tn, tk):
    assert ks % tk == 0
    return pl.BlockSpec((None, tn, tk), lambda i, j, kk: ((kk * tk) // ks, j, ((kk * tk) % ks) // tk))


def _blocked_out(ns, tm, tn):
    assert ns % tn == 0
    return pl.BlockSpec((None, tm, tn), lambda i, j, kk: ((j * tn) // ns, i, ((j * tn) % ns) // tn))


def _rows_call(body, *, name, nblk, tr, row_ins, consts, row_outs, accs=(), sides=_NO_SIDES):
    n_ri, n_c, n_ro, n_acc = len(row_ins), len(consts), len(row_outs), len(accs)
    n_si, n_so = len(sides.ins), len(sides.outs)

    def kern(*refs):
        i = pl.program_id(0)
        rin, refs = refs[:n_ri], refs[n_ri:]
        cin, refs = refs[:n_c], refs[n_c:]
        side_in, refs = refs[:n_si], refs[n_si:]
        rout, refs = refs[:n_ro], refs[n_ro:]
        acc, refs = refs[:n_acc], refs[n_acc:]
        side_out, side_sems = refs[:n_so], refs[n_so:]

        if sides.jobs:
            @pl.when(i == 0)
            def _():
                sides.run("start", side_in, side_out, side_sems)

        if n_acc:
            @pl.when(i == 0)
            def _():
                for r in acc:
                    r[...] = jnp.zeros_like(r)

        body(rin, cin, rout, acc)

        if sides.jobs:
            @pl.when(i == nblk - 1)
            def _():
                sides.run("finish", side_in, side_out, side_sems)

    in_specs = [pl.BlockSpec((tr, w), lambda i, ro=ro, co=co: (i + ro, co)) for (_, w, ro, co) in row_ins]
    in_specs += [pl.BlockSpec(cst.shape, lambda i, nd=cst.ndim: (0,) * nd) for cst in consts]
    out_specs = [pl.BlockSpec((tr, w), lambda i: (i, 0)) for (_, w, _) in row_outs]
    out_specs += [pl.BlockSpec(s, lambda i, nd=len(s): (0,) * nd) for s in accs]
    out_shape = [jax.ShapeDtypeStruct((r, w), dt) for (r, w, dt) in row_outs]
    out_shape += [jax.ShapeDtypeStruct(s, F32) for s in accs]
    return pl.pallas_call(
        kern, name=name, grid=(nblk,), in_specs=in_specs + [ANY_SPEC] * n_si,
        out_specs=out_specs + [ANY_SPEC] * n_so, out_shape=out_shape + list(sides.outs),
        scratch_shapes=list(sides.sems),
        input_output_aliases=sides.aliases(n_ri + n_c, n_ro + n_acc),
        compiler_params=_cparams(("arbitrary",)),
    )(*[r[0] for r in row_ins], *consts, *sides.ins)


def _norm_mod(x, g, shift, scale, *, name, tr, sides=_NO_SIDES):
    rows, d = x.shape

    def body(rin, cin, rout, acc):
        xv = rin[0][...]
        n = xv * _rstd(xv) * cin[0][...]
        rout[0][...] = (n * (1.0 + cin[2][...]) + cin[1][...]).astype(BF16)

    return _rows_call(body, name=name, nblk=rows // tr, tr=tr, row_ins=[(x, d, 0, 0)],
                      consts=[g, shift, scale], row_outs=[(rows, d, BF16)], sides=sides)


def _swap_halves(t, width):
    lane = lax.broadcasted_iota(jnp.int32, t.shape, 1)
    return jnp.where(lane % 64 < 32, pltpu.roll(t, width - 32, 1), pltpu.roll(t, 32, 1))


def _gla_prep(z, tabs, wdf_pad, wdb_pad, bdf, bdb, cfg, *, tr, sides=_NO_SIDES):
    rows = z.shape[0]
    kw, h = cfg["KW"], cfg["H"]

    def body(rin, cin, rout, acc):
        zq, zk, zl = rin[0][...], rin[1][...], rin[2][...]
        cq, sq, ck, sk = [jnp.concatenate([rin[3 + t][...]] * h, axis=1) for t in range(4)]
        rout[0][...] = zq * cq + _swap_halves(zq, kw) * sq
        rout[1][...] = zk * ck + _swap_halves(zk, kw) * sk
        for o, w, b in ((2, cin[0], cin[2]), (3, cin[1], cin[3])):
            a = _nn(zl, w[...]) + b[...]
            rout[o][...] = (jnp.minimum(a, 0.0) - jnp.log(1.0 + jnp.exp(-jnp.abs(a)))) * (1.0 / GLA_TAU)

    row_ins = [(z, kw, 0, 0), (z, kw, 0, 1), (z, LANES, 0, cfg["L0"] // LANES)]
    row_ins += [(t, LANES, 0, 0) for t in tabs]
    return _rows_call(body, name="gla_prep", nblk=rows // tr, tr=tr, row_ins=row_ins,
                      consts=[wdf_pad, wdb_pad, bdf, bdb], row_outs=[(rows, kw, F32)] * 4, sides=sides)


def _chunk_consts(rev):
    c = GLA_CHUNK
    r = lax.broadcasted_iota(jnp.int32, (c, c), 0)
    cc = lax.broadcasted_iota(jnp.int32, (c, c), 1)
    keep = (cc >= r) if rev else (cc <= r)
    return keep, keep.astype(F32)


def _heads_per_step(cfg):
    hb = 4 if cfg["H"] % 4 == 0 else (2 if cfg["H"] % 2 == 0 else 1)
    assert cfg["V0"] % (hb * cfg["DV"]) == 0
    return hb


def _chunk_decay(la, keep_f):
    b = _dg(keep_f, la, ((1,), (0,)), HI)
    return b, _colsum(la)


def _gla_fwd(qr, kr, z, la, st0, cfg, *, rev, row_off, nrows, tb, name, sides=_NO_SIDES):
    h, dk, dv = cfg["H"], cfg["DK"], cfg["DV"]
    c = GLA_CHUNK
    nsub = tb // c
    nblk = nrows // tb
    roff = row_off // tb
    hb = _heads_per_step(cfg)
    v_cb = cfg["V0"] // (hb * dv)
    n_si, n_so = len(sides.ins), len(sides.outs)

    def blk(j):
        return (nblk - 1 - j) if rev else j

    def body(q_ref, k_ref, v_ref, la_ref, st0_ref, *rest):
        side_in, rest = rest[:n_si], rest[n_si:]
        o_ref, save_ref, fin_ref = rest[:3]
        side_out, st, side_sems = rest[3:3 + n_so], rest[3 + n_so], rest[4 + n_so:]
        hh, j = pl.program_id(0), pl.program_id(1)

        if sides.jobs:
            @pl.when((hh == 0) & (j == 0))
            def _():
                sides.run("start", side_in, side_out, side_sems)

        @pl.when(j == 0)
        def _():
            st[...] = st0_ref[...]

        keep, keep_f = _chunk_consts(rev)
        order = range(nsub - 1, -1, -1) if rev else range(nsub)
        heads = range(hb)
        ksl = [slice(g * dk, (g + 1) * dk) for g in heads]
        vsl = [slice(g * dv, (g + 1) * dv) for g in heads]
        state = [st[g] for g in heads]
        for s in order:
            rs = pl.ds(s * c, c)
            q = [q_ref[rs, ksl[g]] for g in heads]
            k = [k_ref[rs, ksl[g]] for g in heads]
            v = [v_ref[rs, vsl[g]] for g in heads]
            bb = [_chunk_decay(la_ref[rs, ksl[g]], keep_f) for g in heads]
            qe = [q[g] * jnp.exp(bb[g][0]) for g in heads]
            ke = [k[g] * jnp.exp(-bb[g][0]) for g in heads]
            kl = [k[g] * jnp.exp(bb[g][1] - bb[g][0]) for g in heads]
            att = [jnp.where(keep, _nt(qe[g], ke[g]), 0.0) for g in heads]
            out = [_nt(qe[g], state[g]) + _nn(att[g], v[g]) for g in heads]
            new = [state[g] * jnp.exp(bb[g][1]) + _tn(v[g], kl[g]) for g in heads]
            for g in heads:
                save_ref[g, s] = state[g]
                o_ref[rs, vsl[g]] = out[g]
            state = new
        for g in heads:
            st[g] = state[g]

        @pl.when(j == nblk - 1)
        def _():
            fin_ref[...] = st[...]

        if sides.jobs:
            @pl.when((hh == h // hb - 1) & (j == nblk - 1))
            def _():
                sides.run("finish", side_in, side_out, side_sems)

    in_specs = [
        pl.BlockSpec((tb, hb * dk), lambda hh, j: (roff + blk(j), hh)),
        pl.BlockSpec((tb, hb * dk), lambda hh, j: (roff + blk(j), hh)),
        pl.BlockSpec((tb, hb * dv), lambda hh, j: (roff + blk(j), v_cb + hh)),
        pl.BlockSpec((tb, hb * dk), lambda hh, j: (roff + blk(j), hh)),
        pl.BlockSpec((hb, dv, dk), lambda hh, j: (hh, 0, 0)),
    ]
    out_specs = [
        pl.BlockSpec((tb, hb * dv), lambda hh, j: (blk(j), hh)),
        pl.BlockSpec((hb, nsub, dv, dk), lambda hh, j: (hh, blk(j), 0, 0)),
        pl.BlockSpec((hb, dv, dk), lambda hh, j: (hh, 0, 0)),
    ]
    out_shape = [
        jax.ShapeDtypeStruct((nrows, h * dv), F32),
        jax.ShapeDtypeStruct((h, nrows // c, dv, dk), F32),
        jax.ShapeDtypeStruct((h, dv, dk), F32),
    ]
    return pl.pallas_call(
        body, name=name, grid=(h // hb, nblk), in_specs=in_specs + [ANY_SPEC] * n_si,
        out_specs=out_specs + [ANY_SPEC] * n_so, out_shape=out_shape + list(sides.outs),
        scratch_shapes=[pltpu.VMEM((hb, dv, dk), F32)] + list(sides.sems),
        input_output_aliases=sides.aliases(5, 3),
        compiler_params=_cparams(("arbitrary", "arbitrary")),
    )(qr, kr, z, la, st0, *sides.ins)


def _gla_bwd(qr, kr, z, la, do, save, dst_init, cfg, *, rev, row_off, nrows, tb, name, sides=_NO_SIDES):
    h, dk, dv = cfg["H"], cfg["DK"], cfg["DV"]
    c = GLA_CHUNK
    nsub = tb // c
    nblk = nrows // tb
    roff = row_off // tb
    hb = _heads_per_step(cfg)
    v_cb = cfg["V0"] // (hb * dv)
    n_si, n_so = len(sides.ins), len(sides.outs)

    def blk(j):
        return j if rev else (nblk - 1 - j)

    def body(q_ref, k_ref, v_ref, la_ref, do_ref, save_ref, di_ref, *rest):
        side_in, rest = rest[:n_si], rest[n_si:]
        dq_ref, dk_ref, dv_ref, dla_ref, d0_ref = rest[:5]
        side_out, dst, side_sems = rest[5:5 + n_so], rest[5 + n_so], rest[6 + n_so:]
        hh, j = pl.program_id(0), pl.program_id(1)

        if sides.jobs:
            @pl.when((hh == 0) & (j == 0))
            def _():
                sides.run("start", side_in, side_out, side_sems)

        @pl.when(j == 0)
        def _():
            dst[...] = di_ref[...]

        keep, keep_f = _chunk_consts(rev)
        keep_t = _chunk_consts(not rev)[1]
        order = range(nsub) if rev else range(nsub - 1, -1, -1)
        heads = range(hb)
        ksl = [slice(g * dk, (g + 1) * dk) for g in heads]
        vsl = [slice(g * dv, (g + 1) * dv) for g in heads]
        d_after = [dst[g] for g in heads]
        for s in order:
            rs = pl.ds(s * c, c)
            q = [q_ref[rs, ksl[g]] for g in heads]
            k = [k_ref[rs, ksl[g]] for g in heads]
            v = [v_ref[rs, vsl[g]] for g in heads]
            lac = [la_ref[rs, ksl[g]] for g in heads]
            dout = [do_ref[rs, vsl[g]] for g in heads]
            s_in = [save_ref[g, s] for g in heads]
            bb = [_chunk_decay(lac[g], keep_f) for g in heads]
            eb = [jnp.exp(bb[g][0]) for g in heads]
            enb = [jnp.exp(-bb[g][0]) for g in heads]
            elb = [jnp.exp(bb[g][1] - bb[g][0]) for g in heads]
            etot = [jnp.exp(bb[g][1]) for g in heads]
            qe = [q[g] * eb[g] for g in heads]
            ke = [k[g] * enb[g] for g in heads]
            kl = [k[g] * elb[g] for g in heads]
            att = [jnp.where(keep, _nt(qe[g], ke[g]), 0.0) for g in heads]
            datt = [jnp.where(keep, _nt(dout[g], v[g]), 0.0) for g in heads]
            dqe = [_nn(dout[g], s_in[g]) + _nn(datt[g], ke[g]) for g in heads]
            dke = [_tn(datt[g], qe[g]) for g in heads]
            dkl = [_nn(v[g], d_after[g]) for g in heads]
            dvv = [_tn(att[g], dout[g]) + _nt(kl[g], d_after[g]) for g in heads]
            db = [dqe[g] * qe[g] - dke[g] * ke[g] - dkl[g] * kl[g] for g in heads]
            dbtot = [_colsum(dkl[g] * kl[g]) + _colsum(d_after[g] * s_in[g]) * etot[g] for g in heads]
            dla = [_dg(keep_t, db[g], ((1,), (0,)), HI) + dbtot[g] for g in heads]
            d_after = [d_after[g] * etot[g] + _tn(dout[g], qe[g]) for g in heads]
            for g in heads:
                dv_ref[rs, vsl[g]] = dvv[g]
                dla_ref[rs, ksl[g]] = dla[g]
                dq_ref[rs, ksl[g]] = dqe[g] * eb[g]
                dk_ref[rs, ksl[g]] = dke[g] * enb[g] + dkl[g] * elb[g]
        for g in heads:
            dst[g] = d_after[g]

        @pl.when(j == nblk - 1)
        def _():
            d0_ref[...] = dst[...]

        if sides.jobs:
            @pl.when((hh == h // hb - 1) & (j == nblk - 1))
            def _():
                sides.run("finish", side_in, side_out, side_sems)

    in_specs = [
        pl.BlockSpec((tb, hb * dk), lambda hh, j: (roff + blk(j), hh)),
        pl.BlockSpec((tb, hb * dk), lambda hh, j: (roff + blk(j), hh)),
        pl.BlockSpec((tb, hb * dv), lambda hh, j: (roff + blk(j), v_cb + hh)),
        pl.BlockSpec((tb, hb * dk), lambda hh, j: (roff + blk(j), hh)),
        pl.BlockSpec((tb, hb * dv), lambda hh, j: (blk(j), hh)),
        pl.BlockSpec((hb, nsub, dv, dk), lambda hh, j: (hh, blk(j), 0, 0)),
        pl.BlockSpec((hb, dv, dk), lambda hh, j: (hh, 0, 0)),
    ]
    out_specs = [
        pl.BlockSpec((tb, hb * dk), lambda hh, j: (blk(j), hh)),
        pl.BlockSpec((tb, hb * dk), lambda hh, j: (blk(j), hh)),
        pl.BlockSpec((tb, hb * dv), lambda hh, j: (blk(j), hh)),
        pl.BlockSpec((tb, hb * dk), lambda hh, j: (blk(j), hh)),
        pl.BlockSpec((hb, dv, dk), lambda hh, j: (hh, 0, 0)),
    ]
    out_shape = [
        jax.ShapeDtypeStruct((nrows, h * dk), F32),
        jax.ShapeDtypeStruct((nrows, h * dk), F32),
        jax.ShapeDtypeStruct((nrows, h * dv), F32),
        jax.ShapeDtypeStruct((nrows, h * dk), F32),
        jax.ShapeDtypeStruct((h, dv, dk), F32),
    ]
    return pl.pallas_call(
        body, name=name, grid=(h // hb, nblk), in_specs=in_specs + [ANY_SPEC] * n_si,
        out_specs=out_specs + [ANY_SPEC] * n_so, out_shape=out_shape + list(sides.outs),
        scratch_shapes=[pltpu.VMEM((hb, dv, dk), F32)] + list(sides.sems),
        input_output_aliases=sides.aliases(7, 5),
        compiler_params=_cparams(("arbitrary", "arbitrary")),
    )(qr, kr, z, la, do, save, dst_init, *sides.ins)


def _gla_post(gf, gb, la_f, la_b, z, tabs, wdf_pad, wdb_pad, cfg, *, row_off, nrows, tr, name):
    kw, vw = cfg["KW"], cfg["VW"]
    h = cfg["H"]
    ro = row_off // tr

    def body(rin, cin, rout, acc):
        dq = rin[0][...] + rin[1][...]
        dk_ = rin[2][...] + rin[3][...]
        zl = rin[10][...]
        cq, sq, ck, sk = [jnp.concatenate([rin[11 + t][...]] * h, axis=1) for t in range(4)]
        rout[0][...] = (dq * cq + _swap_halves(dq * sq, kw)).astype(BF16)
        rout[1][...] = (dk_ * ck + _swap_halves(dk_ * sk, kw)).astype(BF16)
        rout[2][...] = (rin[8][...] + rin[9][...]).astype(BF16)
        dzl = jnp.zeros(zl.shape, F32)
        for t, w in ((0, cin[0]), (1, cin[1])):
            la = rin[6 + t][...]
            da = rin[4 + t][...] * ((1.0 - jnp.exp(la * GLA_TAU)) * (1.0 / GLA_TAU))
            dzl = dzl + _nt(da, w[...])
            acc[t][...] += _tn(zl, da)
            acc[2 + t][...] += _colsum(da)
        rout[3][...] = dzl.astype(BF16)

    row_ins = [(gf[0], kw, 0, 0), (gb[0], kw, 0, 0), (gf[1], kw, 0, 0), (gb[1], kw, 0, 0),
               (gf[3], kw, 0, 0), (gb[3], kw, 0, 0), (la_f, kw, ro, 0), (la_b, kw, ro, 0),
               (gf[2], vw, 0, 0), (gb[2], vw, 0, 0), (z, LANES, ro, cfg["L0"] // LANES)]
    row_ins += [(t, LANES, ro, 0) for t in tabs]
    return _rows_call(body, name=name, nblk=nrows // tr, tr=tr, row_ins=row_ins, consts=[wdf_pad, wdb_pad],
                      row_outs=[(nrows, kw, BF16), (nrows, kw, BF16), (nrows, vw, BF16), (nrows, LANES, BF16)],
                      accs=[(LANES, kw), (LANES, kw), (1, kw), (1, kw)])


def _readout_fwd(o_f, o_b, z, g, cfg, *, tr, sides=_NO_SIDES):
    n, vw = o_f.shape
    h, dv = cfg["H"], cfg["DV"]

    def body(rin, cin, rout, acc):
        for hh in range(h):
            cs = slice(hh * dv, (hh + 1) * dv)
            oh = rin[0][:, cs] + rin[1][:, cs]
            y = oh * _rstd(oh) * cin[0][:, cs]
            rout[0][:, cs] = (y * _silu(rin[2][:, cs])).astype(BF16)

    return _rows_call(body, name="gla_readout", nblk=n // tr, tr=tr,
                      row_ins=[(o_f, vw, 0, 0), (o_b, vw, 0, 0), (z, vw, 0, cfg["R0"] // vw)], consts=[g],
                      row_outs=[(n, vw, BF16)], sides=sides)


def _readout_bwd(o_f, o_b, z, dycat, g, cfg, *, tr):
    n, vw = o_f.shape
    h, dv = cfg["H"], cfg["DV"]

    def body(rin, cin, rout, acc):
        for hh in range(h):
            cs = slice(hh * dv, (hh + 1) * dv)
            oh = rin[0][:, cs] + rin[1][:, cs]
            r, dyg, gh = rin[2][:, cs], rin[3][:, cs], cin[0][:, cs]
            rs = _rstd(oh)
            dy = dyg * _silu(r)
            rout[0][:, cs] = _rms_bwd(oh, rs, dy * gh).astype(BF16)
            rout[1][:, cs] = (dyg * (oh * rs * gh) * _dsilu(r)).astype(BF16)
            acc[0][:, cs] += _colsum(dy * oh * rs)

    return _rows_call(body, name="gla_readout_bwd", nblk=n // tr, tr=tr,
                      row_ins=[(o_f, vw, 0, 0), (o_b, vw, 0, 0), (z, vw, 0, cfg["R0"] // vw), (dycat, vw, 0, 0)],
                      consts=[g], row_outs=[(n, vw, BF16), (n, vw, BF16)], accs=[(1, vw)])


def _sg_ln(vv):
    mu = jnp.mean(vv, axis=-1, keepdims=True)
    cen = vv - mu
    rstd = lax.rsqrt(jnp.mean(cen * cen, axis=-1, keepdims=True) + EPS)
    return cen * rstd, rstd


def _sg_fwd(z, n, lng, lnb, w_s, bs_full, cfg, *, sides=_NO_SIDES):
    sgw, grp, sc = cfg["SGW"], cfg["SG_G"], cfg["SG_C"]
    gw = sgw // grp

    def body(rin, cin, rout, acc):
        u = _gelu(rin[0][...])
        xhat, _ = _sg_ln(_gelu(rin[1][...]))
        vvn = xhat * cin[0][...] + cin[1][...]
        for gg in range(grp):
            cs = slice(gg * gw, (gg + 1) * gw)
            s = _nn(cin[2][gg], vvn[:, cs]) + cin[3][:, cs]
            rout[0][:, cs] = (u[:, cs] * s).astype(BF16)

    return _rows_call(body, name="sg_fwd", nblk=n // sc, tr=sc,
                      row_ins=[(z, sgw, 0, cfg["U0"] // sgw), (z, sgw, 0, cfg["VV0"] // sgw)],
                      consts=[lng, lnb, w_s, bs_full], row_outs=[(n, sgw, BF16)], sides=sides)


def _sg_bwd(z, dycat, n, lng, lnb, w_s, bs_full, cfg):
    sgw, grp, sc = cfg["SGW"], cfg["SG_G"], cfg["SG_C"]
    gw = sgw // grp

    def body(rin, cin, rout, acc):
        up, vp, dy = rin[0][...], rin[1][...], rin[2][...]
        u = _gelu(up)
        xhat, rstd = _sg_ln(_gelu(vp))
        lng_v = cin[0][...]
        vvn = xhat * lng_v + cin[1][...]
        ds = dy * u
        acc[1][...] += ds
        dvvn_parts = []
        for gg in range(grp):
            cs = slice(gg * gw, (gg + 1) * gw)
            w = cin[2][gg]
            s = _nn(w, vvn[:, cs]) + cin[3][:, cs]
            rout[0][:, cs] = (dy[:, cs] * s * _dgelu(up[:, cs])).astype(BF16)
            acc[0][gg] += _nt(ds[:, cs], vvn[:, cs])
            dvvn_parts.append(_tn(w, ds[:, cs]))
        dvvn = jnp.concatenate(dvvn_parts, axis=1)
        acc[2][...] += _colsum(dvvn * xhat)
        acc[3][...] += _colsum(dvvn)
        dxh = dvvn * lng_v
        dvv = rstd * (dxh - jnp.mean(dxh, axis=-1, keepdims=True)
                      - xhat * jnp.mean(dxh * xhat, axis=-1, keepdims=True))
        rout[0][:, sgw:] = (dvv * _dgelu(vp)).astype(BF16)

    vw = cfg["VW"]
    return _rows_call(body, name="sg_bwd", nblk=n // sc, tr=sc,
                      row_ins=[(z, sgw, 0, cfg["U0"] // sgw), (z, sgw, 0, cfg["VV0"] // sgw),
                               (dycat, sgw, 0, vw // sgw)],
                      consts=[lng, lnb, w_s, bs_full], row_outs=[(n, 2 * sgw, BF16)],
                      accs=[(grp, sc, sc), (sc, sgw), (1, sgw), (1, sgw)])


def _mid_fwd(x, mix, g1, post1, pre2, sh2, sc2, *, tr, sides=_NO_SIDES):
    n, d = x.shape

    def body(rin, cin, rout, acc):
        xv, mv = rin[0][...], rin[1][...]
        x1 = xv + cin[0][...] * (mv * _rstd(mv) * cin[1][...])
        rout[0][...] = x1
        n2 = x1 * _rstd(x1) * cin[2][...]
        rout[1][...] = (n2 * (1.0 + cin[4][...]) + cin[3][...]).astype(BF16)

    return _rows_call(body, name="mid_fwd", nblk=n // tr, tr=tr, row_ins=[(x, d, 0, 0), (mix, d, 0, 0)],
                      consts=[g1, post1, pre2, sh2, sc2], row_outs=[(n, d, F32), (n, d, BF16)], sides=sides)


def _head_bwd(x1, m2, target, g2, post2, *, tr):
    n, d = x1.shape

    def body(rin, cin, rout, acc):
        x1v, mv, tv = rin[0][...], rin[1][...], rin[2][...]
        g2v, pg = cin[0][...], cin[1][...]
        r = _rstd(mv)
        y2 = mv * r * pg
        err = (x1v + g2v * y2) - tv
        acc[2][...] += _colsum(err * err) * (0.5 / d)
        dx2 = err * (1.0 / d)
        rout[0][...] = dx2
        dy2 = dx2 * g2v
        acc[0][...] += _colsum(dx2 * y2)
        acc[1][...] += _colsum(dy2 * mv * r)
        rout[1][...] = _rms_bwd(mv, r, dy2 * pg).astype(BF16)

    return _rows_call(body, name="head_bwd", nblk=n // tr, tr=tr,
                      row_ins=[(x1, d, 0, 0), (m2, d, 0, 0), (target, d, 0, 0)], consts=[g2, post2],
                      row_outs=[(n, d, F32), (n, d, BF16)], accs=[(1, d)] * 3)


def _mid_bwd(dh2, x1, dx2, mix, sc2, pre2, g1, post1, *, tr, sides=_NO_SIDES):
    n, d = x1.shape

    def body(rin, cin, rout, acc):
        dh, x1v, dx2v, mv = rin[0][...], rin[1][...], rin[2][...], rin[3][...]
        sc2v, pre2v, g1v, post1v = cin[0][...], cin[1][...], cin[2][...], cin[3][...]
        r2 = _rstd(x1v)
        xr = x1v * r2
        acc[0][...] += _colsum(dh)
        acc[1][...] += _colsum(dh * (xr * pre2v))
        dn2 = dh * (1.0 + sc2v)
        acc[2][...] += _colsum(dn2 * xr)
        dx1 = dx2v + _rms_bwd(x1v, r2, dn2 * pre2v)
        rout[0][...] = dx1
        r1 = _rstd(mv)
        mr = mv * r1
        acc[3][...] += _colsum(dx1 * (mr * post1v))
        dy1 = dx1 * g1v
        acc[4][...] += _colsum(dy1 * mr)
        rout[1][...] = _rms_bwd(mv, r1, dy1 * post1v).astype(BF16)

    return _rows_call(body, name="mid_bwd", nblk=n // tr, tr=tr,
                      row_ins=[(dh2, d, 0, 0), (x1, d, 0, 0), (dx2, d, 0, 0), (mix, d, 0, 0)],
                      consts=[sc2, pre2, g1, post1], row_outs=[(n, d, F32), (n, d, BF16)], accs=[(1, d)] * 5,
                      sides=sides)


def _in_bwd(da, x, dres, sc1, pre1, *, row_off, tr, name, sides=_NO_SIDES):
    n, d = x.shape
    with_res = dres is not None

    def body(rin, cin, rout, acc):
        dav, xv = rin[0][...], rin[1][...]
        sc1v, pre1v = cin[0][...], cin[1][...]
        r = _rstd(xv)
        xr = xv * r
        acc[0][...] += _colsum(dav)
        acc[1][...] += _colsum(dav * (xr * pre1v))
        dn = dav * (1.0 + sc1v)
        acc[2][...] += _colsum(dn * xr)
        if with_res:
            rout[0][...] = rin[2][...] + _rms_bwd(xv, r, dn * pre1v)

    row_ins = [(da, d, row_off // tr, 0), (x, d, 0, 0)] + ([(dres, d, 0, 0)] if with_res else [])
    return _rows_call(body, name=name, nblk=n // tr, tr=tr, row_ins=row_ins, consts=[sc1, pre1],
                      row_outs=[(n, d, F32)] if with_res else [], accs=[(1, d)] * 3, sides=sides)


def _ada_fwd(c16, w, b, *, tn):
    d, ncol = w.shape

    def body(c_ref, w_ref, b_ref, o_ref):
        o_ref[...] = _nn(_silu(c_ref[...]), w_ref[...]) + b_ref[...]

    return pl.pallas_call(
        body, name="ada_fwd", grid=(ncol // tn,),
        in_specs=[pl.BlockSpec((16, d), lambda j: (0, 0)), pl.BlockSpec((d, tn), lambda j: (0, j)),
                  pl.BlockSpec((1, tn), lambda j: (0, j))],
        out_specs=pl.BlockSpec((16, tn), lambda j: (0, j)),
        out_shape=jax.ShapeDtypeStruct((16, ncol), F32),
        compiler_params=_cparams(("arbitrary",)),
    )(c16, w, b)


def _ada_bwd(c16, dm, w, c_ctx, *, tn):
    d, ncol = w.shape

    def body(c_ref, dm_ref, w_ref, cc_ref, gw_ref, dcc_ref, acc):
        j = pl.program_id(0)

        @pl.when(j == 0)
        def _():
            acc[...] = jnp.zeros_like(acc)

        gw_ref[...] = _tn(_silu(c_ref[...]), dm_ref[...])
        acc[...] += _nt(dm_ref[...], w_ref[...])

        @pl.when(j == ncol // tn - 1)
        def _():
            dcc_ref[...] = _colsum(acc[8:16, :]) * _dsilu(cc_ref[...])

    return pl.pallas_call(
        body, name="ada_bwd", grid=(ncol // tn,),
        in_specs=[pl.BlockSpec((16, d), lambda j: (0, 0)), pl.BlockSpec((16, tn), lambda j: (0, j)),
                  pl.BlockSpec((d, tn), lambda j: (0, j)), pl.BlockSpec((1, d), lambda j: (0, 0))],
        out_specs=[pl.BlockSpec((d, tn), lambda j: (0, j)), pl.BlockSpec((1, d), lambda j: (0, 0))],
        out_shape=[jax.ShapeDtypeStruct((d, ncol), F32), jax.ShapeDtypeStruct((1, d), F32)],
        scratch_shapes=[pltpu.VMEM((16, d), F32)],
        compiler_params=_cparams(("arbitrary",)),
    )(c16, dm, w, c_ctx)


def _adam_math(w, g, m, v):
    m = ADAM_B1 * m + (1.0 - ADAM_B1) * g
    v = ADAM_B2 * v + (1.0 - ADAM_B2) * (g * g)
    m_hat = m / (1.0 - ADAM_B1 ** ADAM_STEP)
    v_hat = v / (1.0 - ADAM_B2 ** ADAM_STEP)
    delta = -ADAM_LR * (m_hat / (jnp.sqrt(v_hat) + ADAM_EPS) + ADAM_WD * w)
    return delta, m, v


def _adam_big(parts, w, m, v, *, name, tr, sides=_NO_SIDES):
    rows, cols = w.shape
    n_p = len(parts)
    n_si, n_so = len(sides.ins), len(sides.outs)
    nblk = rows // tr

    def body(*refs):
        ins, refs = refs[:n_p + 3], refs[n_p + 3:]
        side_in, refs = refs[:n_si], refs[n_si:]
        outs, side_out, side_sems = refs[:4], refs[4:4 + n_so], refs[4 + n_so:]
        i = pl.program_id(0)

        if sides.jobs:
            @pl.when(i == 0)
            def _():
                sides.run("start", side_in, side_out, side_sems)

        g = ins[0][...]
        for p in ins[1:n_p]:
            g = g + p[...].astype(F32)
        delta, m2, v2 = _adam_math(ins[n_p][...], g, ins[n_p + 1][...], ins[n_p + 2][...])
        outs[0][...] = g
        outs[1][...] = delta
        outs[2][...] = m2
        outs[3][...] = v2

        if sides.jobs:
            @pl.when(i == nblk - 1)
            def _():
                sides.run("finish", side_in, side_out, side_sems)

    plain = pl.BlockSpec((tr, cols), lambda i: (i, 0))
    in_specs = []
    for arr, idx in parts:
        if idx is None:
            in_specs.append(plain)
        else:
            in_specs.append(pl.BlockSpec((None, tr, cols), lambda i, idx=idx: (idx, i, 0)))
    in_specs += [plain] * 3
    return pl.pallas_call(
        body, name=name, grid=(nblk,), in_specs=in_specs + [ANY_SPEC] * n_si,
        out_specs=[plain] * 4 + [ANY_SPEC] * n_so,
        out_shape=[jax.ShapeDtypeStruct((rows, cols), F32)] * 4 + list(sides.outs),
        scratch_shapes=list(sides.sems), input_output_aliases=sides.aliases(n_p + 3, 4),
        compiler_params=_cparams(("arbitrary",) if sides.jobs else ("parallel",)),
    )(*[p[0] for p in parts], w, m, v, *sides.ins)


def _sum_parts(parts, rows, cols, *, name, tr):
    def body(*refs):
        g = refs[0][...].astype(F32)
        for p in refs[1:-1]:
            g = g + p[...].astype(F32)
        refs[-1][...] = g

    in_specs = [pl.BlockSpec((None, tr, cols), lambda i, idx=idx: (idx, i, 0)) for _, idx in parts]
    return pl.pallas_call(
        body, name=name, grid=(rows // tr,), in_specs=in_specs, out_specs=pl.BlockSpec((tr, cols), lambda i: (i, 0)),
        out_shape=jax.ShapeDtypeStruct((rows, cols), F32), compiler_params=_cparams(("parallel",)),
    )(*[p[0] for p in parts])


def _adam_small(g8, w, m, v):
    def body(g_ref, w_ref, m_ref, v_ref, go, do, mo, vo):
        g = g_ref[0]
        for r in range(1, N_DEV):
            g = g + g_ref[r]
        delta, m2, v2 = _adam_math(w_ref[...], g, m_ref[...], v_ref[...])
        go[...] = g
        do[...] = delta
        mo[...] = m2
        vo[...] = v2

    return pl.pallas_call(
        body, name="adam_small", out_shape=[jax.ShapeDtypeStruct(w.shape, F32)] * 4,
        compiler_params=pltpu.CompilerParams(vmem_limit_bytes=VMEM_LIMIT),
    )(g8, w, m, v)


VEC_W = 1024
TK = 2048
ELEMS_PER_BLOCK = 256 * 1024


def _dense(v):
    a, k = v.shape
    kp = -(-k // (8 * VEC_W)) * (8 * VEC_W)
    return jnp.pad(v, ((0, 0), (0, kp - k))).reshape(a, kp // VEC_W, VEC_W)


def _all_gather_vec(v, *, name):
    k = v.shape[1]
    return _all_gather_small(_dense(v)[0], name=name).reshape(N_DEV, -1)[:, :k]


def _my_pos():
    return lax.axis_index("x"), lax.axis_index("y"), lax.axis_index("c")


def _flip(v, bit):
    return (1 - v) if bit else v


def _all_gather_small(v, *, name):
    r, k = v.shape

    def body(v_ref, out_ref, send, recv, lsem):
        x, y, c = _my_pos()
        me = 4 * x + 2 * y + c
        local = pltpu.make_async_copy(v_ref, out_ref.at[me], lsem)
        local.start()
        sends = []
        for kk in range(1, N_DEV):
            peer = (_flip(x, kk & 4), _flip(y, kk & 2), _flip(c, kk & 1))
            cp = pltpu.make_async_remote_copy(src_ref=v_ref, dst_ref=out_ref.at[me], send_sem=send.at[kk - 1],
                                              recv_sem=recv.at[kk - 1], device_id=peer, device_id_type=MESH)
            cp.start()
            sends.append(cp)
        for kk in range(1, N_DEV):
            px, py, pc = _flip(x, kk & 4), _flip(y, kk & 2), _flip(c, kk & 1)
            src = 4 * px + 2 * py + pc
            pltpu.make_async_remote_copy(src_ref=v_ref, dst_ref=out_ref.at[src], send_sem=send.at[kk - 1],
                                         recv_sem=recv.at[kk - 1], device_id=(px, py, pc),
                                         device_id_type=MESH).wait_recv()
        for cp in sends:
            cp.wait_send()
        local.wait()

    return pl.pallas_call(
        body, name=name, out_shape=jax.ShapeDtypeStruct((N_DEV, r, k), v.dtype),
        in_specs=[pl.BlockSpec(memory_space=pltpu.VMEM)], out_specs=pl.BlockSpec(memory_space=pltpu.VMEM),
        scratch_shapes=[pltpu.SemaphoreType.DMA((N_DEV - 1,)), pltpu.SemaphoreType.DMA((N_DEV - 1,)),
                        pltpu.SemaphoreType.DMA],
        compiler_params=pltpu.CompilerParams(vmem_limit_bytes=VMEM_LIMIT),
    )(v)


def _ag_job(shards, rows=None, chained=None):
    n_arr = len(shards)

    def part(ref):
        return ref if rows is None else ref.at[pl.ds(rows[0], rows[1])]

    def tools(ins, outs, sems):
        send, recv, lsem = sems
        x, y, c = _my_pos()
        chips = [(1 - x, y), (x, 1 - y), (1 - x, 1 - y)]

        def copy(a, kk, block, to, src=None):
            dst = part(outs[a].at[4 * block[0] + 2 * block[1] + block[2]])
            return pltpu.make_async_remote_copy(src_ref=dst if src is None else part(src), dst_ref=dst,
                                                send_sem=send.at[a, kk], recv_sem=recv.at[a, kk],
                                                device_id=to, device_id_type=MESH)

        locals_ = [pltpu.make_async_copy(part(ins[a]), part(outs[a].at[4 * x + 2 * y + c]), lsem.at[a])
                   for a in range(n_arr)]
        firsts = []
        for a in range(n_arr):
            firsts.append(copy(a, 0, (x, y, c), (x, y, 1 - c), src=ins[a]))
            firsts += [copy(a, 1 + j, (x, y, c), (*chip, c), src=ins[a]) for j, chip in enumerate(chips)]
        return copy, locals_, firsts, chips, (x, y, c)

    def start(ins, outs, sems):
        _, locals_, firsts, _, _ = tools(ins, outs, sems)
        for cp in locals_ + firsts:
            cp.start()

    def finish(ins, outs, sems):
        copy, locals_, firsts, chips, (x, y, c) = tools(ins, outs, sems)
        me, sibling = (x, y, c), (x, y, 1 - c)
        passed = []
        for a in range(n_arr):
            for j, chip in enumerate(chips):
                copy(a, 1 + j, (*chip, c), me).wait_recv()
                fw = copy(a, 4 + j, (*chip, c), sibling)
                fw.start()
                passed.append(fw)
        for a in range(n_arr):
            copy(a, 0, sibling, me).wait_recv()
            for j, chip in enumerate(chips):
                copy(a, 4 + j, (*chip, 1 - c), me).wait_recv()
        for cp in firsts + passed:
            cp.wait_send()
        for lc in locals_:
            lc.wait()

    job = dict(ins=list(shards), outs=[jax.ShapeDtypeStruct((N_DEV,) + s.shape, s.dtype) for s in shards],
               sems=[pltpu.SemaphoreType.DMA((n_arr, 7)), pltpu.SemaphoreType.DMA((n_arr, 7)),
                     pltpu.SemaphoreType.DMA((n_arr,))], start=start, finish=finish)
    if chained is not None:
        job["ins"] = list(shards) + list(chained)
        job["alias"] = {n_arr + a: a for a in range(n_arr)}
    return job


def _ag3_job(shard, rows=None, chained=None):
    r0, nr = rows if rows is not None else (0, shard.shape[0])
    ha = (nr // 2) // 16 * 16
    halves = [(r0, ha), (r0 + ha, nr - ha)]

    def tools(ins, outs, sems):
        send, recv, lsem = sems
        x, y, c = _my_pos()
        me, sib = (x, y, c), (x, y, 1 - c)
        xn, yn, dg = (1 - x, y, c), (x, 1 - y, c), (1 - x, 1 - y, c)
        other = lambda p: (p[0], p[1], 1 - p[2])

        def blk(p, span=(r0, nr)):
            return outs[0].at[4 * p[0] + 2 * p[1] + p[2]].at[pl.ds(span[0], span[1])]

        mine = ins[0].at[pl.ds(r0, nr)]
        plan = [(mine, blk(me), sib, blk(sib)), (mine, blk(me), xn, blk(xn)), (mine, blk(me), yn, blk(yn)),
                (blk(xn, halves[0]), blk(xn, halves[0]), yn, blk(dg, halves[0])),
                (blk(yn, halves[1]), blk(yn, halves[1]), xn, blk(dg, halves[1])),
                (blk(xn), blk(xn), sib, blk(other(xn))), (blk(yn), blk(yn), sib, blk(other(yn))),
                (blk(dg, halves[0]), blk(dg, halves[0]), sib, blk(other(dg), halves[0])),
                (blk(dg, halves[1]), blk(dg, halves[1]), sib, blk(other(dg), halves[1]))]
        sends = [pltpu.make_async_remote_copy(src_ref=s, dst_ref=d, send_sem=send.at[k], recv_sem=recv.at[k],
                                              device_id=to, device_id_type=MESH)
                 for k, (s, d, to, _) in enumerate(plan)]
        recvs = [pltpu.make_async_remote_copy(src_ref=got, dst_ref=got, send_sem=send.at[k], recv_sem=recv.at[k],
                                              device_id=me, device_id_type=MESH)
                 for k, (_, _, _, got) in enumerate(plan)]
        local = pltpu.make_async_copy(mine, blk(me), lsem)
        return sends, recvs, local

    def start(ins, outs, sems):
        sends, _, local = tools(ins, outs, sems)
        local.start()
        for k in (0, 1, 2):
            sends[k].start()

    def middle(ins, outs, sems):
        sends, recvs, _ = tools(ins, outs, sems)
        recvs[1].wait_recv()
        sends[3].start()
        sends[5].start()
        recvs[2].wait_recv()
        sends[4].start()
        sends[6].start()

    def finish(ins, outs, sems):
        sends, recvs, local = tools(ins, outs, sems)
        recvs[3].wait_recv()
        sends[7].start()
        recvs[4].wait_recv()
        sends[8].start()
        for k in (0, 5, 6, 7, 8):
            recvs[k].wait_recv()
        for cp in sends:
            cp.wait_send()
        local.wait()

    job = dict(ins=[shard], outs=[jax.ShapeDtypeStruct((N_DEV,) + shard.shape, shard.dtype)],
               sems=[pltpu.SemaphoreType.DMA((9,)), pltpu.SemaphoreType.DMA((9,)), pltpu.SemaphoreType.DMA],
               start=start, middle=middle, finish=finish)
    if chained is not None:
        job["ins"] = [shard] + list(chained)
        job["alias"] = {1: 0}
    return job


def _exchange_job(arrays, n_slots, out_slots, src_of, dst_of, peer_of, rows=None, chained=None):
    n_arr = len(arrays)

    def copies(ins, outs, sems):
        send, recv = sems
        x, y, c = _my_pos()
        res = []
        for a in range(n_arr):
            for s in range(n_slots):
                src, dst = ins[a].at[src_of(s, x, y, c)], outs[a].at[dst_of(s)]
                if rows is not None:
                    src, dst = src.at[pl.ds(rows[0], rows[1])], dst.at[pl.ds(rows[0], rows[1])]
                res.append(pltpu.make_async_remote_copy(
                    src_ref=src, dst_ref=dst, send_sem=send.at[a, s], recv_sem=recv.at[a, s],
                    device_id=peer_of(s, x, y, c), device_id_type=MESH))
        return res

    def start(ins, outs, sems):
        for cp in copies(ins, outs, sems):
            cp.start()

    def finish(ins, outs, sems):
        cps = copies(ins, outs, sems)
        for cp in cps:
            cp.wait_recv()
        for cp in cps:
            cp.wait_send()

    job = dict(ins=list(arrays), outs=[jax.ShapeDtypeStruct((out_slots,) + g.shape[1:], g.dtype) for g in arrays],
               sems=[pltpu.SemaphoreType.DMA((n_arr, n_slots)), pltpu.SemaphoreType.DMA((n_arr, n_slots))],
               start=start, finish=finish)
    if chained is not None:
        job["ins"] = list(arrays) + list(chained)
        job["alias"] = {n_arr + a: a for a in range(n_arr)}
    return job


def _pair_job(grads):
    return _exchange_job(
        grads, 4, 4,
        src_of=lambda s, x, y, c: 4 * _flip(x, s & 2) + 2 * _flip(y, s & 1) + (1 - c),
        dst_of=lambda s: s, peer_of=lambda s, x, y, c: (x, y, 1 - c))


def _chip_job(sums, rows=None, chained=None):
    return _exchange_job(
        sums, 3, 3, src_of=lambda s, x, y, c: s, dst_of=lambda s: s,
        peer_of=lambda s, x, y, c: (_flip(x, (s + 1) & 2), _flip(y, (s + 1) & 1), c), rows=rows, chained=chained)


def _a2a_job(x):
    def copies(ins, outs, sems):
        send, recv, lsem = sems
        x_, y_, c_ = _my_pos()
        me = 4 * x_ + 2 * y_ + c_
        local = pltpu.make_async_copy(ins[0].at[me], outs[0].at[me], lsem)
        res = []
        for s in range(1, N_DEV):
            px, py, pc = _flip(x_, s & 4), _flip(y_, s & 2), _flip(c_, s & 1)
            res.append(pltpu.make_async_remote_copy(
                src_ref=ins[0].at[4 * px + 2 * py + pc], dst_ref=outs[0].at[me], send_sem=send.at[s - 1],
                recv_sem=recv.at[s - 1], device_id=(px, py, pc), device_id_type=MESH))
        return local, res

    def start(ins, outs, sems):
        local, res = copies(ins, outs, sems)
        local.start()
        for cp in res:
            cp.start()

    def finish(ins, outs, sems):
        local, res = copies(ins, outs, sems)
        for cp in res:
            cp.wait_recv()
        for cp in res:
            cp.wait_send()
        local.wait()

    return dict(ins=[x], outs=[jax.ShapeDtypeStruct(x.shape, x.dtype)],
                sems=[pltpu.SemaphoreType.DMA((N_DEV - 1,)), pltpu.SemaphoreType.DMA((N_DEV - 1,)),
                      pltpu.SemaphoreType.DMA], start=start, finish=finish)


def _run_sides(sides, *, name):
    n_si, n_so = len(sides.ins), len(sides.outs)

    def body(*refs):
        ins, outs, sems = refs[:n_si], refs[n_si:n_si + n_so], refs[n_si + n_so:]
        sides.run("start", ins, outs, sems)
        sides.run("finish", ins, outs, sems)

    return pl.pallas_call(
        body, name=name, out_shape=list(sides.outs), in_specs=[ANY_SPEC] * n_si, out_specs=[ANY_SPEC] * n_so,
        scratch_shapes=list(sides.sems), input_output_aliases=sides.aliases(0, 0),
    )(*sides.ins)


def _pair_add(g, t, *, name, tr, wire):
    _, r, cols = g.shape
    g4 = g.reshape(4, 2, r, cols)
    j0 = 1 if wire else 0

    def g_index(j, i):
        x, y, c = _my_pos()
        return (jnp.bitwise_xor(2 * x + y, j + j0), c, i, 0)

    def body(g_ref, t_ref, o_ref):
        o_ref[...] = (g_ref[...] + t_ref[...]).astype(o_ref.dtype)

    return pl.pallas_call(
        body, name=name, grid=(3 if wire else 1, r // tr),
        in_specs=[pl.BlockSpec((None, None, tr, cols), g_index),
                  pl.BlockSpec((None, tr, cols), lambda j, i: (j + j0, i, 0))],
        out_specs=pl.BlockSpec((None, tr, cols), lambda j, i: (j, i, 0)),
        out_shape=jax.ShapeDtypeStruct((3 if wire else 1, r, cols), BF16 if wire else F32),
        compiler_params=_cparams(("arbitrary", "arbitrary")),
    )(g4, t)


def _config(x, ctx, w_in, w_dec_f, gla_norm_g, sg_ln_g, w_s):
    n, d = x.shape[1], x.shape[2]
    tc = ctx.shape[1]
    h = gla_norm_g.shape[1]
    dv = gla_norm_g.shape[2] * N_DEV
    dk = dv // 2
    kw, vw = h * dk, h * dv
    lr = w_dec_f.shape[1]
    sgw = sg_ln_g.shape[1]
    cfg = dict(N=n, D=d, TC=tc, H=h, DV=dv, DK=dk, KW=kw, VW=vw, LR=lr, SGW=sgw, SG_G=w_s.shape[1],
               SG_C=w_s.shape[2], IN=w_in.shape[2] * N_DEV)
    cfg.update(K0=kw, V0=2 * kw, R0=2 * kw + vw, L0=2 * kw + 2 * vw, ZA=2 * kw + 2 * vw + LANES)
    cfg.update(U0=0, VV0=sgw, ZB=2 * sgw)
    assert dk == LANES and vw == 2 * kw and 2 * lr <= LANES
    assert cfg["R0"] % vw == 0 and vw % sgw == 0
    assert cfg["IN"] == 2 * kw + 2 * vw + 2 * lr + 2 * sgw
    return cfg


def _rope_tables(cfg):
    n, tc, dk = cfg["N"], cfg["TC"], cfg["DK"]
    m = dk // 4
    pos = jnp.arange(n)
    inv = ROPE_BASE ** (-jnp.arange(m, dtype=F32) / m)
    ang_r = (pos // GRID_W).astype(F32)[:, None] * inv[None, :]
    ang_c = (pos % GRID_W).astype(F32)[:, None] * inv[None, :]
    cos = jnp.concatenate([jnp.cos(ang_r)] * 2 + [jnp.cos(ang_c)] * 2, axis=1)
    sin = jnp.concatenate([-jnp.sin(ang_r), jnp.sin(ang_r), -jnp.sin(ang_c), jnp.sin(ang_c)], axis=1)
    scale = dk ** -0.5
    z = jnp.zeros((tc, dk), F32)
    one = jnp.ones((tc, dk), F32)
    return [jnp.concatenate([cos * scale, z]), jnp.concatenate([sin * scale, z]),
            jnp.concatenate([cos, one]), jnp.concatenate([sin, z])]


def _pair_sums(g, t, nm):
    rows_for = _tile(g.shape[1], max(8, ELEMS_PER_BLOCK // g.shape[2]), 16)
    return (_pair_add(g, t, name="rs_own_" + nm, tr=rows_for, wire=False),
            _pair_add(g, t, name="rs_wire_" + nm, tr=rows_for, wire=True))


def _local_step(x, ctx, target, mods, c_mods, w, cfg):
    n, d, tc = cfg["N"], cfg["D"], cfg["TC"]
    kw, vw, sgw, za, zb, lr = cfg["KW"], cfg["VW"], cfg["SGW"], cfg["ZA"], cfg["ZB"], cfg["LR"]
    sh1, sc1, g1, sh2, sc2, g2 = mods
    csh1, csc1 = c_mods
    rt = n + tc
    tb = math.gcd(256, math.gcd(n, tc))
    tr = math.gcd(128, tb)
    tr_s = math.gcd(128, tb)
    fs = w["sh_1"].shape[1]
    ff = fs * N_DEV
    cs_in = w["sh_in"].shape[1]
    r8 = d // N_DEV

    rows_in = _run_sides(_Sides([_a2a_job(w["sh_in"].reshape(N_DEV, r8, cs_in))]), name="a2a_w_in")[0]
    rows_in = rows_in.transpose(1, 0, 2).reshape(r8, cfg["IN"])
    lf0 = 2 * kw + 2 * vw
    sg0 = lf0 + 2 * lr
    wa_rows = jnp.concatenate([rows_in[:, :sg0], jnp.zeros((r8, LANES - 2 * lr), BF16)], axis=1)
    wb_rows = rows_in[:, sg0:]
    hx, w_a = _norm_mod(x, w["pre1_g"], sh1, sc1, name="in_norm_x", tr=tr, sides=_Sides([_ag_job([wa_rows])]))
    w_a = w_a.reshape(d, za)
    (hc,) = _norm_mod(ctx, w["pre1_g"], csh1, csc1, name="in_norm_ctx", tr=tr)
    a_all = jnp.concatenate([hx, hc], axis=0)

    tm_a = _tile(rt, 1152, 16)
    tm_n = _tile(n, 1024, 16)
    sh_o_rows = w["sh_o"].shape[0]
    o_cut = (sh_o_rows * 3 // 4) // 16 * 16
    z, w_b, wg_o = _matmul(
        a_all, w_a, "nn", rt, za, d, tm=tm_a, tn=_tile(za, 1152, LANES), tk=_tile(d, TK, LANES), name="mm_in_a",
        out_shapes=[jax.ShapeDtypeStruct((rt, za), F32)],
        sides=_Sides([_ag3_job(wb_rows), _ag3_job(w["sh_o"], rows=(0, o_cut))]))
    w_b = w_b.reshape(d, zb)
    s16 = d // 16
    w1_rows = lambda start, count: (start * s16, count * s16)
    z_b, wg_o, w_1 = _matmul(
        a_all, w_b, "nn", n, zb, d, tm=tm_n, tn=_tile(zb, 1024, LANES), tk=_tile(d, TK, LANES), name="mm_in_b",
        out_shapes=[jax.ShapeDtypeStruct((n, zb), F32)],
        sides=_Sides([_ag3_job(w["sh_o"], rows=(o_cut, sh_o_rows - o_cut), chained=[wg_o]),
                      _ag3_job(w["sh_1"], rows=w1_rows(0, 4))]))
    w_o = wg_o.reshape(d, d)

    tabs = _rope_tables(cfg)
    qr, kr, la_f, la_b, w_1 = _gla_prep(
        z, tabs, w["wdf_pad"], w["wdb_pad"], w["b_dec_f"], w["b_dec_b"], cfg, tr=tr,
        sides=_Sides([_ag_job([w["sh_1"]], rows=w1_rows(4, 1), chained=[w_1])]))

    zero_st = jnp.zeros((cfg["H"], cfg["DV"], cfg["DK"]), F32)
    gla = dict(cfg=cfg, tb=tb)
    _, save_cf, st_cf = _gla_fwd(qr, kr, z, la_f, zero_st, rev=False, row_off=n, nrows=tc, name="gla_ctx_f", **gla)
    _, save_cb, st_cb = _gla_fwd(qr, kr, z, la_b, zero_st, rev=True, row_off=n, nrows=tc, name="gla_ctx_b", **gla)
    o_f, save_f, _, w_1 = _gla_fwd(
        qr, kr, z, la_f, st_cf, rev=False, row_off=0, nrows=n, name="gla_f",
        sides=_Sides([_ag_job([w["sh_1"]], rows=w1_rows(5, 2), chained=[w_1])]), **gla)
    o_b, save_b, _, w_1 = _gla_fwd(
        qr, kr, z, la_b, st_cb, rev=True, row_off=0, nrows=n, name="gla_b",
        sides=_Sides([_ag_job([w["sh_1"]], rows=w1_rows(7, 2), chained=[w_1])]), **gla)
    y_gla, w_1 = _readout_fwd(o_f, o_b, z, w["gla_g"], cfg, tr=tr,
                              sides=_Sides([_ag_job([w["sh_1"]], rows=w1_rows(9, 1), chained=[w_1])]))
    y_sg, w_1 = _sg_fwd(z_b, n, w["sg_ln_g"], w["sg_ln_b"], w["w_s"], w["bs_full"], cfg,
                        sides=_Sides([_ag_job([w["sh_1"]], rows=w1_rows(10, 1), chained=[w_1])]))
    ycat = jnp.concatenate([y_gla, y_sg], axis=1)

    mix, w_1 = _matmul(ycat, w_o, "nn", n, d, d, tm=tm_n, tn=_tile(d, 1024, LANES), tk=_tile(d, TK, LANES),
                       name="mm_o", out_shapes=[jax.ShapeDtypeStruct((n, d), F32)],
                       sides=_Sides([_ag3_job(w["sh_1"], rows=w1_rows(11, 5), chained=[w_1])]))
    w2_cut = (fs // 8) // 16 * 16
    x1, h2, wg_2 = _mid_fwd(x, mix, g1, w["post1_g"], w["pre2_g"], sh2, sc2, tr=tr_s,
                            sides=_Sides([_ag_job([w["sh_2"]], rows=(0, w2_cut))]))

    tn_f = _tile(fs, 1024, LANES)
    tk_d = _tile(d, TK, LANES)

    def relu2(acc):
        return acc, jnp.square(jnp.maximum(acc, 0.0))

    a1, p1, wg_2 = _matmul(h2, w_1, "nn", n, ff, d, tm=tm_n, tn=tn_f, tk=tk_d, name="mm_1",
                           b_spec=_blocked_b_nn(fs, tk_d, tn_f), epilogue=relu2,
                           out_shapes=[jax.ShapeDtypeStruct((n, ff), BF16)] * 2,
                           sides=_Sides([_ag3_job(w["sh_2"], rows=(w2_cut, fs - w2_cut), chained=[wg_2])]))
    w_2 = wg_2.reshape(ff, d)
    tk_f = _tile(ff, TK, LANES)
    m2 = _matmul(p1, w_2, "nn", n, d, ff, tm=tm_n, tn=_tile(d, 1024, LANES), tk=tk_f, name="mm_2",
                 out_shapes=[jax.ShapeDtypeStruct((n, d), F32)])[0]

    dx2, dm2, dg2, dpost2, lossc = _head_bwd(x1, m2, target, g2, w["post2_g"], tr=tr_s)

    def drelu2(acc, a):
        return (acc * (2.0 * jnp.maximum(a.astype(F32), 0.0)),)

    tk_n = _tile(n, TK, 16)
    tm_d = _tile(d, 1024, LANES)
    dw_2 = _matmul(p1, dm2, "tn", ff, d, n, tm=_tile(ff, 1024, LANES), tn=_tile(d, 1024, LANES), tk=tk_n,
                   name="mm_2_dw", out_shapes=[jax.ShapeDtypeStruct((ff, d), F32)])[0]
    g_2 = dw_2.reshape(N_DEV, fs, d)
    da1, t_2 = _matmul(dm2, w_2, "nt", n, ff, d, tm=tm_n, tn=_tile(ff, 1024, LANES), tk=tk_d, name="mm_2_dx",
                       epilogue=drelu2, extras=(a1,), out_shapes=[jax.ShapeDtypeStruct((n, ff), BF16)],
                       sides=_Sides([_pair_job([g_2])]))
    p2_own, p2_wire = _pair_sums(g_2, t_2, "w_2")
    cut2 = (fs * 54 // 64) // 16 * 16
    g_1, u_2 = _matmul(h2, da1, "tn", d, ff, n, tm=tm_d, tn=tn_f, tk=tk_n, name="mm_1_dw",
                       out_specs=[_blocked_out(fs, tm_d, tn_f)],
                       out_shapes=[jax.ShapeDtypeStruct((N_DEV, d, fs), F32)],
                       sides=_Sides([_chip_job([p2_wire], rows=(0, cut2))]))
    tk_fs = _tile(fs, TK, LANES)
    dh2, t_1, u_2 = _matmul(da1, w_1, "nt", n, d, ff, tm=tm_n, tn=_tile(d, 1024, LANES), tk=tk_fs, name="mm_1_dx",
                            b_spec=_blocked_b_nt(fs, _tile(d, 1024, LANES), tk_fs),
                            out_shapes=[jax.ShapeDtypeStruct((n, d), F32)],
                            sides=_Sides([_pair_job([g_1]),
                                          _chip_job([p2_wire], rows=(cut2, fs - cut2), chained=[u_2])]))
    p1_own, p1_wire = _pair_sums(g_1, t_1, "w_1")
    c1 = [0] + [(d * f // 64) // 16 * 16 for f in (12, 32, 47)] + [d]
    piece1 = lambda i: (c1[i], c1[i + 1] - c1[i])

    dx1, dmix, dsh2, dsc2, dpre2, dg1, dpost1, u_1 = _mid_bwd(
        dh2, x1, dx2, mix, sc2, w["pre2_g"], g1, w["post1_g"], tr=tr_s,
        sides=_Sides([_chip_job([p1_wire], rows=piece1(0))]))
    dw_o, u_1 = _matmul(ycat, dmix, "tn", d, d, n, tm=tm_d, tn=_tile(d, 1024, LANES), tk=tk_n, name="mm_o_dw",
                        out_shapes=[jax.ShapeDtypeStruct((d, d), F32)],
                        sides=_Sides([_chip_job([p1_wire], rows=piece1(1), chained=[u_1])]))
    g_o = dw_o.reshape(N_DEV, r8, d)
    dycat, t_o, u_1 = _matmul(dmix, w_o, "nt", n, d, d, tm=tm_n, tn=_tile(d, 1024, LANES), tk=tk_d, name="mm_o_dx",
                              out_shapes=[jax.ShapeDtypeStruct((n, d), F32)],
                              sides=_Sides([_pair_job([g_o]), _chip_job([p1_wire], rows=piece1(2), chained=[u_1])]))
    po_own, po_wire = _pair_sums(g_o, t_o, "w_o")

    dz_b, dws, dbs_acc, dlng, dlnb = _sg_bwd(z_b, dycat, n, w["sg_ln_g"], w["sg_ln_b"], w["w_s"], w["bs_full"], cfg)
    dw_b, u_1 = _matmul(a_all, dz_b, "tn", d, zb, n, tm=tm_d, tn=_tile(zb, 1024, LANES), tk=tk_n, name="mm_in_dw_b",
                        out_shapes=[jax.ShapeDtypeStruct((d, zb), F32)],
                        sides=_Sides([_chip_job([p1_wire], rows=piece1(3), chained=[u_1])]))
    g_b = dw_b.reshape(N_DEV, r8, zb)
    do, dzr, dgla_g = _readout_bwd(o_f, o_b, z, dycat, w["gla_g"], cfg, tr=tr)

    *gf, t_b = _gla_bwd(qr, kr, z, la_f, do, save_f, zero_st, rev=False, row_off=0, nrows=n, name="gla_f_bwd",
                        sides=_Sides([_pair_job([g_b])]), **gla)
    pb_own, pb_wire = _pair_sums(g_b, t_b, "w_in_b")
    gb = _gla_bwd(qr, kr, z, la_b, do, save_b, zero_st, rev=True, row_off=0, nrows=n, name="gla_b_bwd", **gla)
    do_c = jnp.zeros((tc, vw), BF16)
    gcf = _gla_bwd(qr, kr, z, la_f, do_c, save_cf, gf[4], rev=False, row_off=n, nrows=tc, name="gla_ctx_f_bwd",
                   **gla)
    gcb = _gla_bwd(qr, kr, z, la_b, do_c, save_cb, gb[4], rev=True, row_off=n, nrows=tc, name="gla_ctx_b_bwd",
                   **gla)

    post = dict(la_f=la_f, la_b=la_b, z=z, tabs=tabs, wdf_pad=w["wdf_pad"], wdb_pad=w["wdb_pad"], cfg=cfg, tr=tr)
    dzq, dzk, dzv, dzl, dwdf, dwdb, dbdf, dbdb = _gla_post(gf, gb, row_off=0, nrows=n, name="gla_post", **post)
    czq, czk, czv, czl, cwdf, cwdb, cbdf, cbdb = _gla_post(gcf, gcb, row_off=n, nrows=tc, name="gla_post_ctx",
                                                           **post)
    dz_a = jnp.concatenate([
        jnp.concatenate([dzq, dzk, dzv, dzr, dzl], axis=1),
        jnp.concatenate([czq, czk, czv, jnp.zeros((tc, vw), BF16), czl], axis=1)], axis=0)

    dw_a, u_b = _matmul(
        a_all, dz_a, "tn", d, za, rt, tm=tm_d, tn=_tile(za, 1152, LANES), tk=_tile(rt, 2176, 16), name="mm_in_dw_a",
        out_shapes=[jax.ShapeDtypeStruct((d, za), F32)], sides=_Sides([_chip_job([pb_wire])]))
    g_a = dw_a.reshape(N_DEV, r8, za)
    da_a, t_a, u_o = _matmul(dz_a, w_a, "nt", rt, d, za, tm=_tile(rt, 576, 16), tn=_tile(d, 512, LANES),
                             tk=za, name="mm_in_dx_a", out_shapes=[jax.ShapeDtypeStruct((rt, d), F32)],
                             sides=_Sides([_pair_job([g_a]), _chip_job([po_wire])]))
    pa_own, pa_wire = _pair_sums(g_a, t_a, "w_in_a")
    cut_a = (r8 * 9 // 16) // 16 * 16
    tm_x = _tile(n, 512, 16)
    da_x, u_a = _matmul(dz_b, w_b, "nt", n, d, zb, tm=tm_x, tn=_tile(d, 1024, LANES), tk=_tile(zb, 4096, LANES),
                        name="mm_in_dx_b", epilogue=lambda acc, prev: (acc + prev,), extras=(da_a,),
                        out_shapes=[jax.ShapeDtypeStruct((n, d), F32)],
                        sides=_Sides([_chip_job([pa_wire], rows=(0, cut_a))]))

    grad_x, dsh1, dsc1, dpre1, u_a = _in_bwd(
        da_x, x, dx1, sc1, w["pre1_g"], row_off=0, tr=tr_s, name="in_bwd_x",
        sides=_Sides([_chip_job([pa_wire], rows=(cut_a, r8 - cut_a), chained=[u_a])]))
    dcsh1, dcsc1, dpre1_c = _in_bwd(da_a, ctx, None, csc1, w["pre1_g"], row_off=n, tr=tr_s, name="in_bwd_ctx")

    small = dict(
        pre1_g=dpre1 + dpre1_c, post1_g=dpost1, pre2_g=dpre2, post2_g=dpost2,
        w_dec_f=(dwdf + cwdf)[:lr], w_dec_b=(dwdb + cwdb)[lr:2 * lr], b_dec_f=dbdf + cbdf, b_dec_b=dbdb + cbdb,
        gla_norm_g=dgla_g, sg_ln_g=dlng, sg_ln_b=dlnb, w_s=dws,
        b_s=dbs_acc.reshape(cfg["SG_C"], cfg["SG_G"], sgw // cfg["SG_G"]).sum(-1).T)
    dmod = jnp.concatenate([dsh1, dsc1, dg1, dsh2, dsc2, dg2], axis=1)
    dmod_c = jnp.concatenate([dcsh1, dcsc1], axis=1)
    big = dict(w_in_a=(pa_own, u_a), w_in_b=(pb_own, u_b), w_o=(po_own, u_o), w_1=(p1_own, u_1), w_2=(p2_own, u_2))
    return lossc, grad_x, big, small, dmod, dmod_c


SMALL_NAMES = ["b_ada", "pre1_g", "post1_g", "pre2_g", "post2_g", "w_dec_f", "b_dec_f", "w_dec_b", "b_dec_b",
               "gla_norm_g", "sg_ln_g", "sg_ln_b", "w_s", "b_s", "c_ctx"]
WEIGHT_ORDER = ["c_ctx", "w_ada", "b_ada", "pre1_g", "post1_g", "pre2_g", "post2_g", "w_in", "w_dec_f", "b_dec_f",
                "w_dec_b", "b_dec_b", "gla_norm_g", "sg_ln_g", "sg_ln_b", "w_s", "b_s", "w_o", "w_1", "w_2"]


def kernel(x, c, ctx, c_ctx, w_ada, b_ada, pre1_g, post1_g, pre2_g, post2_g, w_in, w_dec_f, b_dec_f, w_dec_b, b_dec_b, gla_norm_g, sg_ln_g, sg_ln_b, w_s, b_s, w_o, w_1, w_2, loss_target, m_c_ctx, m_w_ada, m_b_ada, m_pre1_g, m_post1_g, m_pre2_g, m_post2_g, m_w_in, m_w_dec_f, m_b_dec_f, m_w_dec_b, m_b_dec_b, m_gla_norm_g, m_sg_ln_g, m_sg_ln_b, m_w_s, m_b_s, m_w_o, m_w_1, m_w_2, v_c_ctx, v_w_ada, v_b_ada, v_pre1_g, v_post1_g, v_pre2_g, v_post2_g, v_w_in, v_w_dec_f, v_b_dec_f, v_w_dec_b, v_b_dec_b, v_gla_norm_g, v_sg_ln_g, v_sg_ln_b, v_w_s, v_b_s, v_w_o, v_w_1, v_w_2):
    weights = dict(c_ctx=c_ctx, w_ada=w_ada, b_ada=b_ada, pre1_g=pre1_g, post1_g=post1_g, pre2_g=pre2_g,
                   post2_g=post2_g, w_in=w_in, w_dec_f=w_dec_f, b_dec_f=b_dec_f, w_dec_b=w_dec_b, b_dec_b=b_dec_b,
                   gla_norm_g=gla_norm_g, sg_ln_g=sg_ln_g, sg_ln_b=sg_ln_b, w_s=w_s, b_s=b_s, w_o=w_o, w_1=w_1,
                   w_2=w_2)
    mom_m = dict(c_ctx=m_c_ctx, w_ada=m_w_ada, b_ada=m_b_ada, pre1_g=m_pre1_g, post1_g=m_post1_g, pre2_g=m_pre2_g,
                 post2_g=m_post2_g, w_in=m_w_in, w_dec_f=m_w_dec_f, b_dec_f=m_b_dec_f, w_dec_b=m_w_dec_b,
                 b_dec_b=m_b_dec_b, gla_norm_g=m_gla_norm_g, sg_ln_g=m_sg_ln_g, sg_ln_b=m_sg_ln_b, w_s=m_w_s,
                 b_s=m_b_s, w_o=m_w_o, w_1=m_w_1, w_2=m_w_2)
    mom_v = dict(c_ctx=v_c_ctx, w_ada=v_w_ada, b_ada=v_b_ada, pre1_g=v_pre1_g, post1_g=v_post1_g, pre2_g=v_pre2_g,
                 post2_g=v_post2_g, w_in=v_w_in, w_dec_f=v_w_dec_f, b_dec_f=v_b_dec_f, w_dec_b=v_w_dec_b,
                 b_dec_b=v_b_dec_b, gla_norm_g=v_gla_norm_g, sg_ln_g=v_sg_ln_g, sg_ln_b=v_sg_ln_b, w_s=v_w_s,
                 b_s=v_b_s, w_o=v_w_o, w_1=v_w_1, w_2=v_w_2)

    cfg = _config(x, ctx, w_in, w_dec_f, gla_norm_g, sg_ln_g, w_s)
    n, d, h, dv, kw, vw, lr, sgw = (cfg[k] for k in ("N", "D", "H", "DV", "KW", "VW", "LR", "SGW"))
    dvs, kws = dv // N_DEV, kw // N_DEV
    ix, iy, ic = _my_pos()
    me = 4 * ix + 2 * iy + ic

    pack1 = jnp.concatenate([c.reshape(1, d), w_dec_f.reshape(1, lr * kws), w_dec_b.reshape(1, lr * kws),
                             gla_norm_g.reshape(1, h * dvs)], axis=1)
    g1 = _all_gather_vec(pack1, name="ag_small_in")
    c_all = g1[:, :d]
    o1 = d
    wdf = g1[:, o1:o1 + lr * kws].reshape(N_DEV, lr, kws).transpose(1, 0, 2).reshape(lr, kw)
    o1 += lr * kws
    wdb = g1[:, o1:o1 + lr * kws].reshape(N_DEV, lr, kws).transpose(1, 0, 2).reshape(lr, kw)
    o1 += lr * kws
    gla_g = g1[:, o1:o1 + h * dvs].reshape(N_DEV, h, dvs).transpose(1, 0, 2).reshape(1, h * dv)

    c16 = jnp.concatenate([c_all, jnp.broadcast_to(c_ctx.reshape(1, d), (N_DEV, d))], axis=0)
    ncol = w_ada.shape[2]
    wa = w_ada.reshape(d, ncol)
    b_mine = lax.dynamic_slice(b_ada, (0, me * ncol), (1, ncol))
    tn_ada = _tile(ncol, 512, LANES)
    mod_mine = _ada_fwd(c16, wa, b_mine, tn=tn_ada)
    mod_all = _all_gather_small(mod_mine, name="ag_mod").transpose(1, 0, 2).reshape(16, N_DEV * ncol)
    mod_b = lax.dynamic_slice(mod_all, (me, 0), (1, 6 * d))
    mods = [mod_b[:, i * d:(i + 1) * d] for i in range(6)]
    c_mods = [mod_all[N_DEV:N_DEV + 1, :d], mod_all[N_DEV:N_DEV + 1, d:2 * d]]

    zpad = lambda r: jnp.zeros((r, kw), F32)
    w = dict(
        sh_in=w_in.reshape(d, w_in.shape[2]).astype(BF16), sh_o=w_o.reshape(w_o.shape[1], d).astype(BF16),
        sh_1=w_1.reshape(d, w_1.shape[2]).astype(BF16), sh_2=w_2.reshape(w_2.shape[1], d).astype(BF16),
        pre1_g=pre1_g, post1_g=post1_g, pre2_g=pre2_g, post2_g=post2_g, b_dec_f=b_dec_f, b_dec_b=b_dec_b,
        wdf_pad=jnp.concatenate([wdf, zpad(LANES - lr)], axis=0),
        wdb_pad=jnp.concatenate([zpad(lr), wdb, zpad(LANES - 2 * lr)], axis=0),
        gla_g=gla_g, sg_ln_g=sg_ln_g, sg_ln_b=sg_ln_b, w_s=w_s[0],
        bs_full=jnp.repeat(b_s[0].T, sgw // cfg["SG_G"], axis=1))

    lossc, grad_x, big, small, dmod, dmod_c = _local_step(x[0], ctx[0], loss_target[0], mods, c_mods, w, cfg)
    loss = lax.psum(jnp.sum(lossc), AXES)

    order3 = ["pre1_g", "post1_g", "pre2_g", "post2_g", "w_dec_f", "b_dec_f", "w_dec_b", "b_dec_b", "gla_norm_g",
              "sg_ln_g", "sg_ln_b", "w_s", "b_s"]
    pieces = [dmod, dmod_c] + [small[k].reshape(1, -1) for k in order3]
    sizes = [p.shape[1] for p in pieces]
    g3 = _all_gather_vec(jnp.concatenate(pieces, axis=1), name="ag_small_grads")
    offs = [0]
    for s in sizes:
        offs.append(offs[-1] + s)
    dmod_all = g3[:, :6 * d]
    dmod_c_all = jnp.pad(g3[:, offs[1]:offs[2]], ((0, 0), (0, 4 * d)))
    parts8 = {k: g3[:, offs[2 + i]:offs[3 + i]] for i, k in enumerate(order3)}
    parts8["b_ada"] = dmod_all + dmod_c_all
    parts8["w_dec_f"] = lax.dynamic_slice(parts8["w_dec_f"].reshape(N_DEV, lr, kw), (0, 0, me * kws),
                                          (N_DEV, lr, kws)).reshape(N_DEV, -1)
    parts8["w_dec_b"] = lax.dynamic_slice(parts8["w_dec_b"].reshape(N_DEV, lr, kw), (0, 0, me * kws),
                                          (N_DEV, lr, kws)).reshape(N_DEV, -1)
    parts8["gla_norm_g"] = lax.dynamic_slice(parts8["gla_norm_g"].reshape(N_DEV, h, dv), (0, 0, me * dvs),
                                             (N_DEV, h, dvs)).reshape(N_DEV, -1)

    dm16 = jnp.concatenate([dmod_all, dmod_c_all], axis=0)
    dm_mine = lax.dynamic_slice(dm16, (0, me * ncol), (16, ncol))
    g_w_ada, dcc = _ada_bwd(c16, dm_mine, wa, c_ctx.reshape(1, d), tn=tn_ada)
    parts8["c_ctx"] = _all_gather_vec(dcc, name="ag_cctx")

    flat = lambda t: t.reshape(1, -1)
    g8 = _dense(jnp.concatenate([parts8[k] for k in SMALL_NAMES], axis=1))
    ws, ms, vs = [_dense(jnp.concatenate([flat(src[k]) for k in SMALL_NAMES], axis=1))[0]
                  for src in (weights, mom_m, mom_v)]
    res_small = [r.reshape(1, -1) for r in _adam_small(g8, ws, ms, vs)]
    out = {}
    off = 0
    for k in SMALL_NAMES:
        sz = weights[k].size
        out[k] = [r[:, off:off + sz].reshape(weights[k].shape) for r in res_small]
        off += sz

    rows_for = lambda r, cols: _tile(r, max(8, ELEMS_PER_BLOCK // cols), 16)
    r8, cs_in = d // N_DEV, w_in.shape[2]

    def adam(nm, parts, sides=_NO_SIDES):
        shp = weights[nm].shape
        r2 = (shp[1], shp[2])
        res = _adam_big(parts, weights[nm].reshape(r2), mom_m[nm].reshape(r2), mom_v[nm].reshape(r2),
                        name="adam_" + nm, tr=rows_for(*r2), sides=sides)
        out[nm] = [r.reshape(shp) for r in res[:4]]
        return res[4:]

    four = lambda own, u: [(own, 0), (u, 0), (u, 1), (u, 2)]
    red_a = _sum_parts(four(*big["w_in_a"]), r8, cfg["ZA"], name="rs_sum_w_in_a", tr=rows_for(r8, cfg["ZA"]))
    red_b = _sum_parts(four(*big["w_in_b"]), r8, cfg["ZB"], name="rs_sum_w_in_b", tr=rows_for(r8, cfg["ZB"]))
    red = jnp.concatenate([red_a[:, :cfg["L0"] + 2 * lr], red_b], axis=1)
    (g_in,) = adam("w_ada", [(g_w_ada, None)],
                   _Sides([_a2a_job(red.reshape(r8, N_DEV, cs_in).transpose(1, 0, 2))]))
    for nm in ("w_2", "w_1", "w_o"):
        adam(nm, four(*big[nm]))
    adam("w_in", [(g_in.reshape(d, cs_in), None)])

    outs = [loss, grad_x[None]]
    for i in range(4):
        outs += [out[k][i] for k in WEIGHT_ORDER]
    return tuple(outs)
```

```python
import math

import jax
import jax.numpy as jnp
from jax import lax
from jax.experimental import pallas as pl
from jax.experimental.pallas import tpu as pltpu

F32 = jnp.float32
BF16 = jnp.bfloat16
MXU_DTYPE = jnp.bfloat16
HI = lax.Precision.HIGHEST

N_DEV = 8
AXES = ("x", "y", "c")
MESH = pl.DeviceIdType.MESH
LANES = 128
VMEM_LIMIT = 56 * 1024 * 1024

EPS = 1e-6
GRID_W = 64
GLA_CHUNK = 64
GLA_TAU = 16.0
ROPE_BASE = 10000.0
ADAM_LR = 0.001
ADAM_B1 = 0.9
ADAM_B2 = 0.999
ADAM_EPS = 1e-08
ADAM_WD = 0.01
ADAM_STEP = 10


def _cparams(sem):
    return pltpu.CompilerParams(dimension_semantics=sem, vmem_limit_bytes=VMEM_LIMIT)


def _tile(n, target, align):
    if n <= target:
        return n
    best = None
    for t in range(align, target + 1, align):
        if n % t == 0:
            best = t
    assert best is not None, (n, target, align)
    return best


def _dg(a, b, dims, prec=None):
    return lax.dot_general(a, b, (dims, ((), ())), precision=prec, preferred_element_type=F32)


def _nn(a, b):
    return _dg(a.astype(MXU_DTYPE), b.astype(MXU_DTYPE), ((1,), (0,)))


def _nt(a, b):
    return _dg(a.astype(MXU_DTYPE), b.astype(MXU_DTYPE), ((1,), (1,)))


def _tn(a, b):
    return _dg(a.astype(MXU_DTYPE), b.astype(MXU_DTYPE), ((0,), (0,)))


def _sigmoid(x):
    return 1.0 / (1.0 + jnp.exp(-x))


def _silu(x):
    return x * _sigmoid(x)


def _dsilu(x):
    s = _sigmoid(x)
    return s * (1.0 + x * (1.0 - s))


def _gelu(x):
    return 0.5 * x * (1.0 + lax.erf(x * (1.0 / math.sqrt(2.0))))


def _dgelu(x):
    return 0.5 * (1.0 + lax.erf(x * (1.0 / math.sqrt(2.0)))) + x * jnp.exp(-0.5 * x * x) * (1.0 / math.sqrt(2.0 * math.pi))


def _rstd(x):
    return lax.rsqrt(jnp.mean(x * x, axis=-1, keepdims=True) + EPS)


def _rms_bwd(x, r, dn):
    return r * dn - x * (r * r * r) * jnp.mean(dn * x, axis=-1, keepdims=True)


def _colsum(x):
    return jnp.sum(x, axis=0, keepdims=True)


class _Sides:
    def __init__(self, jobs):
        self.jobs = list(jobs)
        self.ins = [a for j in self.jobs for a in j["ins"]]
        self.outs = [o for j in self.jobs for o in j["outs"]]
        self.sems = [s for j in self.jobs for s in j["sems"]]

    def aliases(self, in_base, out_base):
        res, oi, oo = {}, 0, 0
        for j in self.jobs:
            for a, b in j.get("alias", {}).items():
                res[in_base + oi + a] = out_base + oo + b
            oi += len(j["ins"])
            oo += len(j["outs"])
        return res

    def has(self, phase):
        return any(phase in j for j in self.jobs)

    def run(self, phase, in_refs, out_refs, sem_refs):
        oi = oo = os_ = 0
        for j in self.jobs:
            ni, no, ns = len(j["ins"]), len(j["outs"]), len(j["sems"])
            if phase in j:
                j[phase](in_refs[oi:oi + ni], out_refs[oo:oo + no], sem_refs[os_:os_ + ns])
            oi, oo, os_ = oi + ni, oo + no, os_ + ns


_NO_SIDES = _Sides([])
ANY_SPEC = pl.BlockSpec(memory_space=pl.ANY)


def _matmul(a, b, mode, m, n, k, *, tm, tn, tk, name, out_shapes, b_spec=None, out_specs=None,
            epilogue=None, extras=(), sides=_NO_SIDES):
    nk = k // tk
    assert m % tm == 0 and n % tn == 0 and k % tk == 0, (name, m, n, k, tm, tn, tk)
    dot = {"nn": _nn, "nt": _nt, "tn": _tn}[mode]
    if mode == "tn":
        a_spec = pl.BlockSpec((tk, tm), lambda i, j, kk: (kk, i))
    else:
        a_spec = pl.BlockSpec((tm, tk), lambda i, j, kk: (i, kk))
    if b_spec is None:
        if mode == "nt":
            b_spec = pl.BlockSpec((tn, tk), lambda i, j, kk: (j, kk))
        else:
            b_spec = pl.BlockSpec((tk, tn), lambda i, j, kk: (kk, j))
    mn_spec = pl.BlockSpec((tm, tn), lambda i, j, kk: (i, j))
    if out_specs is None:
        out_specs = [mn_spec] * len(out_shapes)
    n_extra = len(extras)
    n_out = len(out_shapes)
    n_si, n_so = len(sides.ins), len(sides.outs)
    ni, nj = m // tm, n // tn

    def body(a_ref, b_ref, *rest):
        extra_refs = rest[:n_extra]
        rest = rest[n_extra:]
        side_in, rest = rest[:n_si], rest[n_si:]
        out_refs, rest = rest[:n_out], rest[n_out:]
        side_out, rest = rest[:n_so], rest[n_so:]
        acc, side_sems = rest[0], rest[1:]
        i, j, kk = pl.program_id(0), pl.program_id(1), pl.program_id(2)

        if sides.jobs:
            @pl.when((i == 0) & (j == 0) & (kk == 0))
            def _():
                sides.run("start", side_in, side_out, side_sems)

        if sides.has("middle"):
            mid = (ni * nj * nk * 2) // 3
            mi, mj, mk = mid // (nj * nk), (mid // nk) % nj, mid % nk

            @pl.when((i == mi) & (j == mj) & (kk == mk))
            def _():
                sides.run("middle", side_in, side_out, side_sems)

        @pl.when(kk == 0)
        def _():
            acc[...] = jnp.zeros_like(acc)

        acc[...] += dot(a_ref[...], b_ref[...])

        @pl.when(kk == nk - 1)
        def _():
            vals = (acc[...],) if epilogue is None else epilogue(acc[...], *[e[...] for e in extra_refs])
            for o, v in zip(out_refs, vals):
                o[...] = v.astype(o.dtype)

        if sides.jobs:
            @pl.when((i == ni - 1) & (j == nj - 1) & (kk == nk - 1))
            def _():
                sides.run("finish", side_in, side_out, side_sems)

    sem = ("arbitrary",) * 3 if sides.jobs else ("parallel", "parallel", "arbitrary")
    res = pl.pallas_call(
        body, name=name, grid=(ni, nj, nk),
        in_specs=[a_spec, b_spec] + [mn_spec] * n_extra + [ANY_SPEC] * n_si,
        out_specs=list(out_specs) + [ANY_SPEC] * n_so, out_shape=list(out_shapes) + list(sides.outs),
        scratch_shapes=[pltpu.VMEM((tm, tn), F32)] + list(sides.sems),
        input_output_aliases=sides.aliases(2 + n_extra, n_out),
        compiler_params=_cparams(sem),
    )(a, b, *extras, *sides.ins)
    return res


def _blocked_b_nn(ns, tk, tn):
    assert ns % tn == 0
    return pl.BlockSpec((None, tk, tn), lambda i, j, kk: ((j * tn) // ns, kk, ((j * tn) % ns) // tn))


def _blocked_b_nt(ks, tn, tk):
    assert ks % tk == 0
    return pl.BlockSpec((None, tn, tk), lambda i, j, kk: ((kk * tk) // ks, j, ((kk * tk) % ks) // tk))


def _blocked_out(ns, tm, tn):
    assert ns % tn == 0
    return pl.BlockSpec((None, tm, tn), lambda i, j, kk: ((j * tn) // ns, i, ((j * tn) % ns) // tn))


def _rows_call(body, *, name, nblk, tr, row_ins, consts, row_outs, accs=(), sides=_NO_SIDES):
    n_ri, n_c, n_ro, n_acc = len(row_ins), len(consts), len(row_outs), len(accs)
    n_si, n_so = len(sides.ins), len(sides.outs)

    def kern(*refs):
        i = pl.program_id(0)
        rin, refs = refs[:n_ri], refs[n_ri:]
        cin, refs = refs[:n_c], refs[n_c:]
        side_in, refs = refs[:n_si], refs[n_si:]
        rout, refs = refs[:n_ro], refs[n_ro:]
        acc, refs = refs[:n_acc], refs[n_acc:]
        side_out, side_sems = refs[:n_so], refs[n_so:]

        if sides.jobs:
            @pl.when(i == 0)
            def _():
                sides.run("start", side_in, side_out, side_sems)

        if n_acc:
            @pl.when(i == 0)
            def _():
                for r in acc:
                    r[...] = jnp.zeros_like(r)

        body(rin, cin, rout, acc)

        if sides.jobs:
            @pl.when(i == nblk - 1)
            def _():
                sides.run("finish", side_in, side_out, side_sems)

    in_specs = [pl.BlockSpec((tr, w), lambda i, ro=ro, co=co: (i + ro, co)) for (_, w, ro, co) in row_ins]
    in_specs += [pl.BlockSpec(cst.shape, lambda i, nd=cst.ndim: (0,) * nd) for cst in consts]
    out_specs = [pl.BlockSpec((tr, w), lambda i: (i, 0)) for (_, w, _) in row_outs]
    out_specs += [pl.BlockSpec(s, lambda i, nd=len(s): (0,) * nd) for s in accs]
    out_shape = [jax.ShapeDtypeStruct((r, w), dt) for (r, w, dt) in row_outs]
    out_shape += [jax.ShapeDtypeStruct(s, F32) for s in accs]
    return pl.pallas_call(
        kern, name=name, grid=(nblk,), in_specs=in_specs + [ANY_SPEC] * n_si,
        out_specs=out_specs + [ANY_SPEC] * n_so, out_shape=out_shape + list(sides.outs),
        scratch_shapes=list(sides.sems),
        input_output_aliases=sides.aliases(n_ri + n_c, n_ro + n_acc),
        compiler_params=_cparams(("arbitrary",)),
    )(*[r[0] for r in row_ins], *consts, *sides.ins)


def _norm_mod(x, g, shift, scale, *, name, tr, sides=_NO_SIDES):
    rows, d = x.shape

    def body(rin, cin, rout, acc):
        xv = rin[0][...]
        n = xv * _rstd(xv) * cin[0][...]
        rout[0][...] = (n * (1.0 + cin[2][...]) + cin[1][...]).astype(BF16)

    return _rows_call(body, name=name, nblk=rows // tr, tr=tr, row_ins=[(x, d, 0, 0)],
                      consts=[g, shift, scale], row_outs=[(rows, d, BF16)], sides=sides)


def _swap_halves(t, width):
    lane = lax.broadcasted_iota(jnp.int32, t.shape, 1)
    return jnp.where(lane % 64 < 32, pltpu.roll(t, width - 32, 1), pltpu.roll(t, 32, 1))


def _gla_prep(z, tabs, wdf_pad, wdb_pad, bdf, bdb, cfg, *, tr, sides=_NO_SIDES):
    rows = z.shape[0]
    kw, h = cfg["KW"], cfg["H"]

    def body(rin, cin, rout, acc):
        zq, zk, zl = rin[0][...], rin[1][...], rin[2][...]
        cq, sq, ck, sk = [jnp.concatenate([rin[3 + t][...]] * h, axis=1) for t in range(4)]
        rout[0][...] = zq * cq + _swap_halves(zq, kw) * sq
        rout[1][...] = zk * ck + _swap_halves(zk, kw) * sk
        for o, w, b in ((2, cin[0], cin[2]), (3, cin[1], cin[3])):
            a = _nn(zl, w[...]) + b[...]
            rout[o][...] = (jnp.minimum(a, 0.0) - jnp.log(1.0 + jnp.exp(-jnp.abs(a)))) * (1.0 / GLA_TAU)

    row_ins = [(z, kw, 0, 0), (z, kw, 0, 1), (z, LANES, 0, cfg["L0"] // LANES)]
    row_ins += [(t, LANES, 0, 0) for t in tabs]
    return _rows_call(body, name="gla_prep", nblk=rows // tr, tr=tr, row_ins=row_ins,
                      consts=[wdf_pad, wdb_pad, bdf, bdb], row_outs=[(rows, kw, F32)] * 4, sides=sides)


def _chunk_consts(rev):
    c = GLA_CHUNK
    r = lax.broadcasted_iota(jnp.int32, (c, c), 0)
    cc = lax.broadcasted_iota(jnp.int32, (c, c), 1)
    keep = (cc >= r) if rev else (cc <= r)
    return keep, keep.astype(F32)


def _heads_per_step(cfg):
    hb = 4 if cfg["H"] % 4 == 0 else (2 if cfg["H"] % 2 == 0 else 1)
    assert cfg["V0"] % (hb * cfg["DV"]) == 0
    return hb


def _chunk_decay(la, keep_f):
    b = _dg(keep_f, la, ((1,), (0,)), HI)
    return b, _colsum(la)


def _gla_fwd(qr, kr, z, la, st0, cfg, *, rev, row_off, nrows, tb, name, sides=_NO_SIDES):
    h, dk, dv = cfg["H"], cfg["DK"], cfg["DV"]
    c = GLA_CHUNK
    nsub = tb // c
    nblk = nrows // tb
    roff = row_off // tb
    hb = _heads_per_step(cfg)
    v_cb = cfg["V0"] // (hb * dv)
    n_si, n_so = len(sides.ins), len(sides.outs)

    def blk(j):
        return (nblk - 1 - j) if rev else j

    def body(q_ref, k_ref, v_ref, la_ref, st0_ref, *rest):
        side_in, rest = rest[:n_si], rest[n_si:]
        o_ref, save_ref, fin_ref = rest[:3]
        side_out, st, side_sems = rest[3:3 + n_so], rest[3 + n_so], rest[4 + n_so:]
        hh, j = pl.program_id(0), pl.program_id(1)

        if sides.jobs:
            @pl.when((hh == 0) & (j == 0))
            def _():
                sides.run("start", side_in, side_out, side_sems)

        @pl.when(j == 0)
        def _():
            st[...] = st0_ref[...]

        keep, keep_f = _chunk_consts(rev)
        order = range(nsub - 1, -1, -1) if rev else range(nsub)
        heads = range(hb)
        ksl = [slice(g * dk, (g + 1) * dk) for g in heads]
        vsl = [slice(g * dv, (g + 1) * dv) for g in heads]
        state = [st[g] for g in heads]
        for s in order:
            rs = pl.ds(s * c, c)
            q = [q_ref[rs, ksl[g]] for g in heads]
            k = [k_ref[rs, ksl[g]] for g in heads]
            v = [v_ref[rs, vsl[g]] for g in heads]
            bb = [_chunk_decay(la_ref[rs, ksl[g]], keep_f) for g in heads]
            qe = [q[g] * jnp.exp(bb[g][0]) for g in heads]
            ke = [k[g] * jnp.exp(-bb[g][0]) for g in heads]
            kl = [k[g] * jnp.exp(bb[g][1] - bb[g][0]) for g in heads]
            att = [jnp.where(keep, _nt(qe[g], ke[g]), 0.0) for g in heads]
            out = [_nt(qe[g], state[g]) + _nn(att[g], v[g]) for g in heads]
            new = [state[g] * jnp.exp(bb[g][1]) + _tn(v[g], kl[g]) for g in heads]
            for g in heads:
                save_ref[g, s] = state[g]
                o_ref[rs, vsl[g]] = out[g]
            state = new
        for g in heads:
            st[g] = state[g]

        @pl.when(j == nblk - 1)
        def _():
            fin_ref[...] = st[...]

        if sides.jobs:
            @pl.when((hh == h // hb - 1) & (j == nblk - 1))
            def _():
                sides.run("finish", side_in, side_out, side_sems)

    in_specs = [
        pl.BlockSpec((tb, hb * dk), lambda hh, j: (roff + blk(j), hh)),
        pl.BlockSpec((tb, hb * dk), lambda hh, j: (roff + blk(j), hh)),
        pl.BlockSpec((tb, hb * dv), lambda hh, j: (roff + blk(j), v_cb + hh)),
        pl.BlockSpec((tb, hb * dk), lambda hh, j: (roff + blk(j), hh)),
        pl.BlockSpec((hb, dv, dk), lambda hh, j: (hh, 0, 0)),
    ]
    out_specs = [
        pl.BlockSpec((tb, hb * dv), lambda hh, j: (blk(j), hh)),
        pl.BlockSpec((hb, nsub, dv, dk), lambda hh, j: (hh, blk(j), 0, 0)),
        pl.BlockSpec((hb, dv, dk), lambda hh, j: (hh, 0, 0)),
    ]
    out_shape = [
        jax.ShapeDtypeStruct((nrows, h * dv), F32),
        jax.ShapeDtypeStruct((h, nrows // c, dv, dk), F32),
        jax.ShapeDtypeStruct((h, dv, dk), F32),
    ]
    return pl.pallas_call(
        body, name=name, grid=(h // hb, nblk), in_specs=in_specs + [ANY_SPEC] * n_si,
        out_specs=out_specs + [ANY_SPEC] * n_so, out_shape=out_shape + list(sides.outs),
        scratch_shapes=[pltpu.VMEM((hb, dv, dk), F32)] + list(sides.sems),
        input_output_aliases=sides.aliases(5, 3),
        compiler_params=_cparams(("arbitrary", "arbitrary")),
    )(qr, kr, z, la, st0, *sides.ins)


def _gla_bwd(qr, kr, z, la, do, save, dst_init, cfg, *, rev, row_off, nrows, tb, name, sides=_NO_SIDES):
    h, dk, dv = cfg["H"], cfg["DK"], cfg["DV"]
    c = GLA_CHUNK
    nsub = tb // c
    nblk = nrows // tb
    roff = row_off // tb
    hb = _heads_per_step(cfg)
    v_cb = cfg["V0"] // (hb * dv)
    n_si, n_so = len(sides.ins), len(sides.outs)

    def blk(j):
        return j if rev else (nblk - 1 - j)

    def body(q_ref, k_ref, v_ref, la_ref, do_ref, save_ref, di_ref, *rest):
        side_in, rest = rest[:n_si], rest[n_si:]
        dq_ref, dk_ref, dv_ref, dla_ref, d0_ref = rest[:5]
        side_out, dst, side_sems = rest[5:5 + n_so], rest[5 + n_so], rest[6 + n_so:]
        hh, j = pl.program_id(0), pl.program_id(1)

        if sides.jobs:
            @pl.when((hh == 0) & (j == 0))
            def _():
                sides.run("start", side_in, side_out, side_sems)

        @pl.when(j == 0)
        def _():
            dst[...] = di_ref[...]

        keep, keep_f = _chunk_consts(rev)
        keep_t = _chunk_consts(not rev)[1]
        order = range(nsub) if rev else range(nsub - 1, -1, -1)
        heads = range(hb)
        ksl = [slice(g * dk, (g + 1) * dk) for g in heads]
        vsl = [slice(g * dv, (g + 1) * dv) for g in heads]
        d_after = [dst[g] for g in heads]
        for s in order:
            rs = pl.ds(s * c, c)
            q = [q_ref[rs, ksl[g]] for g in heads]
            k = [k_ref[rs, ksl[g]] for g in heads]
            v = [v_ref[rs, vsl[g]] for g in heads]
            lac = [la_ref[rs, ksl[g]] for g in heads]
            dout = [do_ref[rs, vsl[g]] for g in heads]
            s_in = [save_ref[g, s] for g in heads]
            bb = [_chunk_decay(lac[g], keep_f) for g in heads]
            eb = [jnp.exp(bb[g][0]) for g in heads]
            enb = [jnp.exp(-bb[g][0]) for g in heads]
            elb = [jnp.exp(bb[g][1] - bb[g][0]) for g in heads]
            etot = [jnp.exp(bb[g][1]) for g in heads]
            qe = [q[g] * eb[g] for g in heads]
            ke = [k[g] * enb[g] for g in heads]
            kl = [k[g] * elb[g] for g in heads]
            att = [jnp.where(keep, _nt(qe[g], ke[g]), 0.0) for g in heads]
            datt = [jnp.where(keep, _nt(dout[g], v[g]), 0.0) for g in heads]
            dqe = [_nn(dout[g], s_in[g]) + _nn(datt[g], ke[g]) for g in heads]
            dke = [_tn(datt[g], qe[g]) for g in heads]
            dkl = [_nn(v[g], d_after[g]) for g in heads]
            dvv = [_tn(att[g], dout[g]) + _nt(kl[g], d_after[g]) for g in heads]
            db = [dqe[g] * qe[g] - dke[g] * ke[g] - dkl[g] * kl[g] for g in heads]
            dbtot = [_colsum(dkl[g] * kl[g]) + _colsum(d_after[g] * s_in[g]) * etot[g] for g in heads]
            dla = [_dg(keep_t, db[g], ((1,), (0,)), HI) + dbtot[g] for g in heads]
            d_after = [d_after[g] * etot[g] + _tn(dout[g], qe[g]) for g in heads]
            for g in heads:
                dv_ref[rs, vsl[g]] = dvv[g]
                dla_ref[rs, ksl[g]] = dla[g]
                dq_ref[rs, ksl[g]] = dqe[g] * eb[g]
                dk_ref[rs, ksl[g]] = dke[g] * enb[g] + dkl[g] * elb[g]
        for g in heads:
            dst[g] = d_after[g]

        @pl.when(j == nblk - 1)
        def _():
            d0_ref[...] = dst[...]

        if sides.jobs:
            @pl.when((hh == h // hb - 1) & (j == nblk - 1))
            def _():
                sides.run("finish", side_in, side_out, side_sems)

    in_specs = [
        pl.BlockSpec((tb, hb * dk), lambda hh, j: (roff + blk(j), hh)),
        pl.BlockSpec((tb, hb * dk), lambda hh, j: (roff + blk(j), hh)),
        pl.BlockSpec((tb, hb * dv), lambda hh, j: (roff + blk(j), v_cb + hh)),
        pl.BlockSpec((tb, hb * dk), lambda hh, j: (roff + blk(j), hh)),
        pl.BlockSpec((tb, hb * dv), lambda hh, j: (blk(j), hh)),
        pl.BlockSpec((hb, nsub, dv, dk), lambda hh, j: (hh, blk(j), 0, 0)),
        pl.BlockSpec((hb, dv, dk), lambda hh, j: (hh, 0, 0)),
    ]
    out_specs = [
        pl.BlockSpec((tb, hb * dk), lambda hh, j: (blk(j), hh)),
        pl.BlockSpec((tb, hb * dk), lambda hh, j: (blk(j), hh)),
        pl.BlockSpec((tb, hb * dv), lambda hh, j: (blk(j), hh)),
        pl.BlockSpec((tb, hb * dk), lambda hh, j: (blk(j), hh)),
        pl.BlockSpec((hb, dv, dk), lambda hh, j: (hh, 0, 0)),
    ]
    out_shape = [
        jax.ShapeDtypeStruct((nrows, h * dk), F32),
        jax.ShapeDtypeStruct((nrows, h * dk), F32),
        jax.ShapeDtypeStruct((nrows, h * dv), F32),
        jax.ShapeDtypeStruct((nrows, h * dk), F32),
        jax.ShapeDtypeStruct((h, dv, dk), F32),
    ]
    return pl.pallas_call(
        body, name=name, grid=(h // hb, nblk), in_specs=in_specs + [ANY_SPEC] * n_si,
        out_specs=out_specs + [ANY_SPEC] * n_so, out_shape=out_shape + list(sides.outs),
        scratch_shapes=[pltpu.VMEM((hb, dv, dk), F32)] + list(sides.sems),
        input_output_aliases=sides.aliases(7, 5),
        compiler_params=_cparams(("arbitrary", "arbitrary")),
    )(qr, kr, z, la, do, save, dst_init, *sides.ins)


def _gla_post(gf, gb, la_f, la_b, z, tabs, wdf_pad, wdb_pad, cfg, *, row_off, nrows, tr, name):
    kw, vw = cfg["KW"], cfg["VW"]
    h = cfg["H"]
    ro = row_off // tr

    def body(rin, cin, rout, acc):
        dq = rin[0][...] + rin[1][...]
        dk_ = rin[2][...] + rin[3][...]
        zl = rin[10][...]
        cq, sq, ck, sk = [jnp.concatenate([rin[11 + t][...]] * h, axis=1) for t in range(4)]
        rout[0][...] = (dq * cq + _swap_halves(dq * sq, kw)).astype(BF16)
        rout[1][...] = (dk_ * ck + _swap_halves(dk_ * sk, kw)).astype(BF16)
        rout[2][...] = (rin[8][...] + rin[9][...]).astype(BF16)
        dzl = jnp.zeros(zl.shape, F32)
        for t, w in ((0, cin[0]), (1, cin[1])):
            la = rin[6 + t][...]
            da = rin[4 + t][...] * ((1.0 - jnp.exp(la * GLA_TAU)) * (1.0 / GLA_TAU))
            dzl = dzl + _nt(da, w[...])
            acc[t][...] += _tn(zl, da)
            acc[2 + t][...] += _colsum(da)
        rout[3][...] = dzl.astype(BF16)

    row_ins = [(gf[0], kw, 0, 0), (gb[0], kw, 0, 0), (gf[1], kw, 0, 0), (gb[1], kw, 0, 0),
               (gf[3], kw, 0, 0), (gb[3], kw, 0, 0), (la_f, kw, ro, 0), (la_b, kw, ro, 0),
               (gf[2], vw, 0, 0), (gb[2], vw, 0, 0), (z, LANES, ro, cfg["L0"] // LANES)]
    row_ins += [(t, LANES, ro, 0) for t in tabs]
    return _rows_call(body, name=name, nblk=nrows // tr, tr=tr, row_ins=row_ins, consts=[wdf_pad, wdb_pad],
                      row_outs=[(nrows, kw, BF16), (nrows, kw, BF16), (nrows, vw, BF16), (nrows, LANES, BF16)],
                      accs=[(LANES, kw), (LANES, kw), (1, kw), (1, kw)])


def _readout_fwd(o_f, o_b, z, g, cfg, *, tr, sides=_NO_SIDES):
    n, vw = o_f.shape
    h, dv = cfg["H"], cfg["DV"]

    def body(rin, cin, rout, acc):
        for hh in range(h):
            cs = slice(hh * dv, (hh + 1) * dv)
            oh = rin[0][:, cs] + rin[1][:, cs]
            y = oh * _rstd(oh) * cin[0][:, cs]
            rout[0][:, cs] = (y * _silu(rin[2][:, cs])).astype(BF16)

    return _rows_call(body, name="gla_readout", nblk=n // tr, tr=tr,
                      row_ins=[(o_f, vw, 0, 0), (o_b, vw, 0, 0), (z, vw, 0, cfg["R0"] // vw)], consts=[g],
                      row_outs=[(n, vw, BF16)], sides=sides)


def _readout_bwd(o_f, o_b, z, dycat, g, cfg, *, tr):
    n, vw = o_f.shape
    h, dv = cfg["H"], cfg["DV"]

    def body(rin, cin, rout, acc):
        for hh in range(h):
            cs = slice(hh * dv, (hh + 1) * dv)
            oh = rin[0][:, cs] + rin[1][:, cs]
            r, dyg, gh = rin[2][:, cs], rin[3][:, cs], cin[0][:, cs]
            rs = _rstd(oh)
            dy = dyg * _silu(r)
            rout[0][:, cs] = _rms_bwd(oh, rs, dy * gh).astype(BF16)
            rout[1][:, cs] = (dyg * (oh * rs * gh) * _dsilu(r)).astype(BF16)
            acc[0][:, cs] += _colsum(dy * oh * rs)

    return _rows_call(body, name="gla_readout_bwd", nblk=n // tr, tr=tr,
                      row_ins=[(o_f, vw, 0, 0), (o_b, vw, 0, 0), (z, vw, 0, cfg["R0"] // vw), (dycat, vw, 0, 0)],
                      consts=[g], row_outs=[(n, vw, BF16), (n, vw, BF16)], accs=[(1, vw)])


def _sg_ln(vv):
    mu = jnp.mean(vv, axis=-1, keepdims=True)
    cen = vv - mu
    rstd = lax.rsqrt(jnp.mean(cen * cen, axis=-1, keepdims=True) + EPS)
    return cen * rstd, rstd


def _sg_fwd(z, n, lng, lnb, w_s, bs_full, cfg, *, sides=_NO_SIDES):
    sgw, grp, sc = cfg["SGW"], cfg["SG_G"], cfg["SG_C"]
    gw = sgw // grp

    def body(rin, cin, rout, acc):
        u = _gelu(rin[0][...])
        xhat, _ = _sg_ln(_gelu(rin[1][...]))
        vvn = xhat * cin[0][...] + cin[1][...]
        for gg in range(grp):
            cs = slice(gg * gw, (gg + 1) * gw)
            s = _nn(cin[2][gg], vvn[:, cs]) + cin[3][:, cs]
            rout[0][:, cs] = (u[:, cs] * s).astype(BF16)

    return _rows_call(body, name="sg_fwd", nblk=n // sc, tr=sc,
                      row_ins=[(z, sgw, 0, cfg["U0"] // sgw), (z, sgw, 0, cfg["VV0"] // sgw)],
                      consts=[lng, lnb, w_s, bs_full], row_outs=[(n, sgw, BF16)], sides=sides)


def _sg_bwd(z, dycat, n, lng, lnb, w_s, bs_full, cfg):
    sgw, grp, sc = cfg["SGW"], cfg["SG_G"], cfg["SG_C"]
    gw = sgw // grp

    def body(rin, cin, rout, acc):
        up, vp, dy = rin[0][...], rin[1][...], rin[2][...]
        u = _gelu(up)
        xhat, rstd = _sg_ln(_gelu(vp))
        lng_v = cin[0][...]
        vvn = xhat * lng_v + cin[1][...]
        ds = dy * u
        acc[1][...] += ds
        dvvn_parts = []
        for gg in range(grp):
            cs = slice(gg * gw, (gg + 1) * gw)
            w = cin[2][gg]
            s = _nn(w, vvn[:, cs]) + cin[3][:, cs]
            rout[0][:, cs] = (dy[:, cs] * s * _dgelu(up[:, cs])).astype(BF16)
            acc[0][gg] += _nt(ds[:, cs], vvn[:, cs])
            dvvn_parts.append(_tn(w, ds[:, cs]))
        dvvn = jnp.concatenate(dvvn_parts, axis=1)
        acc[2][...] += _colsum(dvvn * xhat)
        acc[3][...] += _colsum(dvvn)
        dxh = dvvn * lng_v
        dvv = rstd * (dxh - jnp.mean(dxh, axis=-1, keepdims=True)
                      - xhat * jnp.mean(dxh * xhat, axis=-1, keepdims=True))
        rout[0][:, sgw:] = (dvv * _dgelu(vp)).astype(BF16)

    vw = cfg["VW"]
    return _rows_call(body, name="sg_bwd", nblk=n // sc, tr=sc,
                      row_ins=[(z, sgw, 0, cfg["U0"] // sgw), (z, sgw, 0, cfg["VV0"] // sgw),
                               (dycat, sgw, 0, vw // sgw)],
                      consts=[lng, lnb, w_s, bs_full], row_outs=[(n, 2 * sgw, BF16)],
                      accs=[(grp, sc, sc), (sc, sgw), (1, sgw), (1, sgw)])


def _mid_fwd(x, mix, g1, post1, pre2, sh2, sc2, *, tr, sides=_NO_SIDES):
    n, d = x.shape

    def body(rin, cin, rout, acc):
        xv, mv = rin[0][...], rin[1][...]
        x1 = xv + cin[0][...] * (mv * _rstd(mv) * cin[1][...])
        rout[0][...] = x1
        n2 = x1 * _rstd(x1) * cin[2][...]
        rout[1][...] = (n2 * (1.0 + cin[4][...]) + cin[3][...]).astype(BF16)

    return _rows_call(body, name="mid_fwd", nblk=n // tr, tr=tr, row_ins=[(x, d, 0, 0), (mix, d, 0, 0)],
                      consts=[g1, post1, pre2, sh2, sc2], row_outs=[(n, d, F32), (n, d, BF16)], sides=sides)


def _head_bwd(x1, m2, target, g2, post2, *, tr):
    n, d = x1.shape

    def body(rin, cin, rout, acc):
        x1v, mv, tv = rin[0][...], rin[1][...], rin[2][...]
        g2v, pg = cin[0][...], cin[1][...]
        r = _rstd(mv)
        y2 = mv * r * pg
        err = (x1v + g2v * y2) - tv
        acc[2][...] += _colsum(err * err) * (0.5 / d)
        dx2 = err * (1.0 / d)
        rout[0][...] = dx2
        dy2 = dx2 * g2v
        acc[0][...] += _colsum(dx2 * y2)
        acc[1][...] += _colsum(dy2 * mv * r)
        rout[1][...] = _rms_bwd(mv, r, dy2 * pg).astype(BF16)

    return _rows_call(body, name="head_bwd", nblk=n // tr, tr=tr,
                      row_ins=[(x1, d, 0, 0), (m2, d, 0, 0), (target, d, 0, 0)], consts=[g2, post2],
                      row_outs=[(n, d, F32), (n, d, BF16)], accs=[(1, d)] * 3)


def _mid_bwd(dh2, x1, dx2, mix, sc2, pre2, g1, post1, *, tr, sides=_NO_SIDES):
    n, d = x1.shape

    def body(rin, cin, rout, acc):
        dh, x1v, dx2v, mv = rin[0][...], rin[1][...], rin[2][...], rin[3][...]
        sc2v, pre2v, g1v, post1v = cin[0][...], cin[1][...], cin[2][...], cin[3][...]
        r2 = _rstd(x1v)
        xr = x1v * r2
        acc[0][...] += _colsum(dh)
        acc[1][...] += _colsum(dh * (xr * pre2v))
        dn2 = dh * (1.0 + sc2v)
        acc[2][...] += _colsum(dn2 * xr)
        dx1 = dx2v + _rms_bwd(x1v, r2, dn2 * pre2v)
        rout[0][...] = dx1
        r1 = _rstd(mv)
        mr = mv * r1
        acc[3][...] += _colsum(dx1 * (mr * post1v))
        dy1 = dx1 * g1v
        acc[4][...] += _colsum(dy1 * mr)
        rout[1][...] = _rms_bwd(mv, r1, dy1 * post1v).astype(BF16)

    return _rows_call(body, name="mid_bwd", nblk=n // tr, tr=tr,
                      row_ins=[(dh2, d, 0, 0), (x1, d, 0, 0), (dx2, d, 0, 0), (mix, d, 0, 0)],
                      consts=[sc2, pre2, g1, post1], row_outs=[(n, d, F32), (n, d, BF16)], accs=[(1, d)] * 5,
                      sides=sides)


def _in_bwd(da, x, dres, sc1, pre1, *, row_off, tr, name, sides=_NO_SIDES):
    n, d = x.shape
    with_res = dres is not None

    def body(rin, cin, rout, acc):
        dav, xv = rin[0][...], rin[1][...]
        sc1v, pre1v = cin[0][...], cin[1][...]
        r = _rstd(xv)
        xr = xv * r
        acc[0][...] += _colsum(dav)
        acc[1][...] += _colsum(dav * (xr * pre1v))
        dn = dav * (1.0 + sc1v)
        acc[2][...] += _colsum(dn * xr)
        if with_res:
            rout[0][...] = rin[2][...] + _rms_bwd(xv, r, dn * pre1v)

    row_ins = [(da, d, row_off // tr, 0), (x, d, 0, 0)] + ([(dres, d, 0, 0)] if with_res else [])
    return _rows_call(body, name=name, nblk=n // tr, tr=tr, row_ins=row_ins, consts=[sc1, pre1],
                      row_outs=[(n, d, F32)] if with_res else [], accs=[(1, d)] * 3, sides=sides)


def _ada_fwd(c16, w, b, *, tn):
    d, ncol = w.shape

    def body(c_ref, w_ref, b_ref, o_ref):
        o_ref[...] = _nn(_silu(c_ref[...]), w_ref[...]) + b_ref[...]

    return pl.pallas_call(
        body, name="ada_fwd", grid=(ncol // tn,),
        in_specs=[pl.BlockSpec((16, d), lambda j: (0, 0)), pl.BlockSpec((d, tn), lambda j: (0, j)),
                  pl.BlockSpec((1, tn), lambda j: (0, j))],
        out_specs=pl.BlockSpec((16, tn), lambda j: (0, j)),
        out_shape=jax.ShapeDtypeStruct((16, ncol), F32),
        compiler_params=_cparams(("arbitrary",)),
    )(c16, w, b)


def _ada_bwd(c16, dm, w, c_ctx, *, tn):
    d, ncol = w.shape

    def body(c_ref, dm_ref, w_ref, cc_ref, gw_ref, dcc_ref, acc):
        j = pl.program_id(0)

        @pl.when(j == 0)
        def _():
            acc[...] = jnp.zeros_like(acc)

        gw_ref[...] = _tn(_silu(c_ref[...]), dm_ref[...])
        acc[...] += _nt(dm_ref[...], w_ref[...])

        @pl.when(j == ncol // tn - 1)
        def _():
            dcc_ref[...] = _colsum(acc[8:16, :]) * _dsilu(cc_ref[...])

    return pl.pallas_call(
        body, name="ada_bwd", grid=(ncol // tn,),
        in_specs=[pl.BlockSpec((16, d), lambda j: (0, 0)), pl.BlockSpec((16, tn), lambda j: (0, j)),
                  pl.BlockSpec((d, tn), lambda j: (0, j)), pl.BlockSpec((1, d), lambda j: (0, 0))],
        out_specs=[pl.BlockSpec((d, tn), lambda j: (0, j)), pl.BlockSpec((1, d), lambda j: (0, 0))],
        out_shape=[jax.ShapeDtypeStruct((d, ncol), F32), jax.ShapeDtypeStruct((1, d), F32)],
        scratch_shapes=[pltpu.VMEM((16, d), F32)],
        compiler_params=_cparams(("arbitrary",)),
    )(c16, dm, w, c_ctx)


def _adam_math(w, g, m, v):
    m = ADAM_B1 * m + (1.0 - ADAM_B1) * g
    v = ADAM_B2 * v + (1.0 - ADAM_B2) * (g * g)
    m_hat = m / (1.0 - ADAM_B1 ** ADAM_STEP)
    v_hat = v / (1.0 - ADAM_B2 ** ADAM_STEP)
    delta = -ADAM_LR * (m_hat / (jnp.sqrt(v_hat) + ADAM_EPS) + ADAM_WD * w)
    return delta, m, v


OWN = "own"


def _part_spec(idx, tr, cols):
    if idx == OWN:
        def own_index(i):
            x, y, c = _my_pos()
            return (2 * x + y, c, i, 0)
        return pl.BlockSpec((None, None, tr, cols), own_index)
    return pl.BlockSpec((None, tr, cols), lambda i: (idx, i, 0))


def _part_operand(arr, idx):
    return arr.reshape(4, 2, arr.shape[1], arr.shape[2]) if idx == OWN else arr


def _adam_big(parts, w, m, v, *, name, tr, sides=_NO_SIDES):
    rows, cols = w.shape
    n_p = len(parts)
    n_si, n_so = len(sides.ins), len(sides.outs)
    nblk = rows // tr

    def body(*refs):
        ins, refs = refs[:n_p + 3], refs[n_p + 3:]
        side_in, refs = refs[:n_si], refs[n_si:]
        outs, side_out, side_sems = refs[:4], refs[4:4 + n_so], refs[4 + n_so:]
        i = pl.program_id(0)

        if sides.jobs:
            @pl.when(i == 0)
            def _():
                sides.run("start", side_in, side_out, side_sems)

        g = ins[0][...]
        for p in ins[1:n_p]:
            g = g + p[...].astype(F32)
        delta, m2, v2 = _adam_math(ins[n_p][...], g, ins[n_p + 1][...], ins[n_p + 2][...])
        outs[0][...] = g
        outs[1][...] = delta
        outs[2][...] = m2
        outs[3][...] = v2

        if sides.jobs:
            @pl.when(i == nblk - 1)
            def _():
                sides.run("finish", side_in, side_out, side_sems)

    plain = pl.BlockSpec((tr, cols), lambda i: (i, 0))
    in_specs = [plain if idx is None else _part_spec(idx, tr, cols) for _, idx in parts]
    parts = [(_part_operand(arr, idx), idx) for arr, idx in parts]
    in_specs += [plain] * 3
    return pl.pallas_call(
        body, name=name, grid=(nblk,), in_specs=in_specs + [ANY_SPEC] * n_si,
        out_specs=[plain] * 4 + [ANY_SPEC] * n_so,
        out_shape=[jax.ShapeDtypeStruct((rows, cols), F32)] * 4 + list(sides.outs),
        scratch_shapes=list(sides.sems), input_output_aliases=sides.aliases(n_p + 3, 4),
        compiler_params=_cparams(("arbitrary",) if sides.jobs else ("parallel",)),
    )(*[p[0] for p in parts], w, m, v, *sides.ins)


def _sum_parts(parts, rows, cols, *, name, tr):
    def body(*refs):
        g = refs[0][...].astype(F32)
        for p in refs[1:-1]:
            g = g + p[...].astype(F32)
        refs[-1][...] = g

    in_specs = [_part_spec(idx, tr, cols) for _, idx in parts]
    return pl.pallas_call(
        body, name=name, grid=(rows // tr,), in_specs=in_specs, out_specs=pl.BlockSpec((tr, cols), lambda i: (i, 0)),
        out_shape=jax.ShapeDtypeStruct((rows, cols), F32), compiler_params=_cparams(("parallel",)),
    )(*[_part_operand(arr, idx) for arr, idx in parts])


def _adam_small(g8, w, m, v):
    def body(g_ref, w_ref, m_ref, v_ref, go, do, mo, vo):
        g = g_ref[0]
        for r in range(1, N_DEV):
            g = g + g_ref[r]
        delta, m2, v2 = _adam_math(w_ref[...], g, m_ref[...], v_ref[...])
        go[...] = g
        do[...] = delta
        mo[...] = m2
        vo[...] = v2

    return pl.pallas_call(
        body, name="adam_small", out_shape=[jax.ShapeDtypeStruct(w.shape, F32)] * 4,
        compiler_params=pltpu.CompilerParams(vmem_limit_bytes=VMEM_LIMIT),
    )(g8, w, m, v)


VEC_W = 1024
TK = 2048
ELEMS_PER_BLOCK = 256 * 1024


def _dense(v):
    a, k = v.shape
    kp = -(-k // (8 * VEC_W)) * (8 * VEC_W)
    return jnp.pad(v, ((0, 0), (0, kp - k))).reshape(a, kp // VEC_W, VEC_W)


def _all_gather_vec(v, *, name):
    k = v.shape[1]
    return _all_gather_small(_dense(v)[0], name=name).reshape(N_DEV, -1)[:, :k]


def _my_pos():
    return lax.axis_index("x"), lax.axis_index("y"), lax.axis_index("c")


def _flip(v, bit):
    return (1 - v) if bit else v


def _all_gather_small(v, *, name):
    r, k = v.shape

    def body(v_ref, out_ref, send, recv, lsem):
        x, y, c = _my_pos()
        me = 4 * x + 2 * y + c
        local = pltpu.make_async_copy(v_ref, out_ref.at[me], lsem)
        local.start()
        sends = []
        for kk in range(1, N_DEV):
            peer = (_flip(x, kk & 4), _flip(y, kk & 2), _flip(c, kk & 1))
            cp = pltpu.make_async_remote_copy(src_ref=v_ref, dst_ref=out_ref.at[me], send_sem=send.at[kk - 1],
                                              recv_sem=recv.at[kk - 1], device_id=peer, device_id_type=MESH)
            cp.start()
            sends.append(cp)
        for kk in range(1, N_DEV):
            px, py, pc = _flip(x, kk & 4), _flip(y, kk & 2), _flip(c, kk & 1)
            src = 4 * px + 2 * py + pc
            pltpu.make_async_remote_copy(src_ref=v_ref, dst_ref=out_ref.at[src], send_sem=send.at[kk - 1],
                                         recv_sem=recv.at[kk - 1], device_id=(px, py, pc),
                                         device_id_type=MESH).wait_recv()
        for cp in sends:
            cp.wait_send()
        local.wait()

    return pl.pallas_call(
        body, name=name, out_shape=jax.ShapeDtypeStruct((N_DEV, r, k), v.dtype),
        in_specs=[pl.BlockSpec(memory_space=pltpu.VMEM)], out_specs=pl.BlockSpec(memory_space=pltpu.VMEM),
        scratch_shapes=[pltpu.SemaphoreType.DMA((N_DEV - 1,)), pltpu.SemaphoreType.DMA((N_DEV - 1,)),
                        pltpu.SemaphoreType.DMA],
        compiler_params=pltpu.CompilerParams(vmem_limit_bytes=VMEM_LIMIT),
    )(v)


def _ag_job(shards, rows=None, chained=None):
    n_arr = len(shards)

    def part(ref):
        return ref if rows is None else ref.at[pl.ds(rows[0], rows[1])]

    def tools(ins, outs, sems):
        send, recv, lsem = sems
        x, y, c = _my_pos()
        chips = [(1 - x, y), (x, 1 - y), (1 - x, 1 - y)]

        def copy(a, kk, block, to, src=None):
            dst = part(outs[a].at[4 * block[0] + 2 * block[1] + block[2]])
            return pltpu.make_async_remote_copy(src_ref=dst if src is None else part(src), dst_ref=dst,
                                                send_sem=send.at[a, kk], recv_sem=recv.at[a, kk],
                                                device_id=to, device_id_type=MESH)

        locals_ = [pltpu.make_async_copy(part(ins[a]), part(outs[a].at[4 * x + 2 * y + c]), lsem.at[a])
                   for a in range(n_arr)]
        firsts = []
        for a in range(n_arr):
            firsts.append(copy(a, 0, (x, y, c), (x, y, 1 - c), src=ins[a]))
            firsts += [copy(a, 1 + j, (x, y, c), (*chip, c), src=ins[a]) for j, chip in enumerate(chips)]
        return copy, locals_, firsts, chips, (x, y, c)

    def start(ins, outs, sems):
        _, locals_, firsts, _, _ = tools(ins, outs, sems)
        for cp in locals_ + firsts:
            cp.start()

    def finish(ins, outs, sems):
        copy, locals_, firsts, chips, (x, y, c) = tools(ins, outs, sems)
        me, sibling = (x, y, c), (x, y, 1 - c)
        passed = []
        for a in range(n_arr):
            for j, chip in enumerate(chips):
                copy(a, 1 + j, (*chip, c), me).wait_recv()
                fw = copy(a, 4 + j, (*chip, c), sibling)
                fw.start()
                passed.append(fw)
        for a in range(n_arr):
            copy(a, 0, sibling, me).wait_recv()
            for j, chip in enumerate(chips):
                copy(a, 4 + j, (*chip, 1 - c), me).wait_recv()
        for cp in firsts + passed:
            cp.wait_send()
        for lc in locals_:
            lc.wait()

    job = dict(ins=list(shards), outs=[jax.ShapeDtypeStruct((N_DEV,) + s.shape, s.dtype) for s in shards],
               sems=[pltpu.SemaphoreType.DMA((n_arr, 7)), pltpu.SemaphoreType.DMA((n_arr, 7)),
                     pltpu.SemaphoreType.DMA((n_arr,))], start=start, finish=finish)
    if chained is not None:
        job["ins"] = list(shards) + list(chained)
        job["alias"] = {n_arr + a: a for a in range(n_arr)}
    return job


def _ag3_job(shard, rows=None, chained=None):
    r0, nr = rows if rows is not None else (0, shard.shape[0])
    ha = (nr // 2) // 16 * 16
    halves = [(r0, ha), (r0 + ha, nr - ha)]

    def tools(ins, outs, sems):
        send, recv, lsem = sems
        x, y, c = _my_pos()
        me, sib = (x, y, c), (x, y, 1 - c)
        xn, yn, dg = (1 - x, y, c), (x, 1 - y, c), (1 - x, 1 - y, c)
        other = lambda p: (p[0], p[1], 1 - p[2])

        def blk(p, span=(r0, nr)):
            return outs[0].at[4 * p[0] + 2 * p[1] + p[2]].at[pl.ds(span[0], span[1])]

        mine = ins[0].at[pl.ds(r0, nr)]
        plan = [(mine, blk(me), sib, blk(sib)), (mine, blk(me), xn, blk(xn)), (mine, blk(me), yn, blk(yn)),
                (blk(xn, halves[0]), blk(xn, halves[0]), yn, blk(dg, halves[0])),
                (blk(yn, halves[1]), blk(yn, halves[1]), xn, blk(dg, halves[1])),
                (blk(xn), blk(xn), sib, blk(other(xn))), (blk(yn), blk(yn), sib, blk(other(yn))),
                (blk(dg, halves[0]), blk(dg, halves[0]), sib, blk(other(dg), halves[0])),
                (blk(dg, halves[1]), blk(dg, halves[1]), sib, blk(other(dg), halves[1]))]
        sends = [pltpu.make_async_remote_copy(src_ref=s, dst_ref=d, send_sem=send.at[k], recv_sem=recv.at[k],
                                              device_id=to, device_id_type=MESH)
                 for k, (s, d, to, _) in enumerate(plan)]
        recvs = [pltpu.make_async_remote_copy(src_ref=got, dst_ref=got, send_sem=send.at[k], recv_sem=recv.at[k],
                                              device_id=me, device_id_type=MESH)
                 for k, (_, _, _, got) in enumerate(plan)]
        local = pltpu.make_async_copy(mine, blk(me), lsem)
        return sends, recvs, local

    def start(ins, outs, sems):
        sends, _, local = tools(ins, outs, sems)
        local.start()
        for k in (0, 1, 2):
            sends[k].start()

    def middle(ins, outs, sems):
        sends, recvs, _ = tools(ins, outs, sems)
        recvs[1].wait_recv()
        sends[3].start()
        sends[5].start()
        recvs[2].wait_recv()
        sends[4].start()
        sends[6].start()

    def finish(ins, outs, sems):
        sends, recvs, local = tools(ins, outs, sems)
        recvs[3].wait_recv()
        sends[7].start()
        recvs[4].wait_recv()
        sends[8].start()
        for k in (0, 5, 6, 7, 8):
            recvs[k].wait_recv()
        for cp in sends:
            cp.wait_send()
        local.wait()

    job = dict(ins=[shard], outs=[jax.ShapeDtypeStruct((N_DEV,) + shard.shape, shard.dtype)],
               sems=[pltpu.SemaphoreType.DMA((9,)), pltpu.SemaphoreType.DMA((9,)), pltpu.SemaphoreType.DMA],
               start=start, middle=middle, finish=finish)
    if chained is not None:
        job["ins"] = [shard] + list(chained)
        job["alias"] = {1: 0}
    return job


def _exchange_job(arrays, n_slots, out_slots, src_of, dst_of, peer_of, rows=None, chained=None):
    n_arr = len(arrays)

    def copies(ins, outs, sems):
        send, recv = sems
        x, y, c = _my_pos()
        res = []
        for a in range(n_arr):
            for s in range(n_slots):
                src, dst = ins[a].at[src_of(s, x, y, c)], outs[a].at[dst_of(s)]
                if rows is not None:
                    src, dst = src.at[pl.ds(rows[0], rows[1])], dst.at[pl.ds(rows[0], rows[1])]
                res.append(pltpu.make_async_remote_copy(
                    src_ref=src, dst_ref=dst, send_sem=send.at[a, s], recv_sem=recv.at[a, s],
                    device_id=peer_of(s, x, y, c), device_id_type=MESH))
        return res

    def start(ins, outs, sems):
        for cp in copies(ins, outs, sems):
            cp.start()

    def finish(ins, outs, sems):
        cps = copies(ins, outs, sems)
        for cp in cps:
            cp.wait_recv()
        for cp in cps:
            cp.wait_send()

    job = dict(ins=list(arrays), outs=[jax.ShapeDtypeStruct((out_slots,) + g.shape[1:], g.dtype) for g in arrays],
               sems=[pltpu.SemaphoreType.DMA((n_arr, n_slots)), pltpu.SemaphoreType.DMA((n_arr, n_slots))],
               start=start, finish=finish)
    if chained is not None:
        job["ins"] = list(arrays) + list(chained)
        job["alias"] = {n_arr + a: a for a in range(n_arr)}
    return job


def _pair_job(grads):
    return _exchange_job(
        grads, 4, 4,
        src_of=lambda s, x, y, c: 4 * _flip(x, s & 2) + 2 * _flip(y, s & 1) + (1 - c),
        dst_of=lambda s: s, peer_of=lambda s, x, y, c: (x, y, 1 - c))


def _chip_job(sums, rows=None, chained=None):
    return _exchange_job(
        sums, 3, 3, src_of=lambda s, x, y, c: s, dst_of=lambda s: s,
        peer_of=lambda s, x, y, c: (_flip(x, (s + 1) & 2), _flip(y, (s + 1) & 1), c), rows=rows, chained=chained)


def _a2a_job(x):
    def copies(ins, outs, sems):
        send, recv, lsem = sems
        x_, y_, c_ = _my_pos()
        me = 4 * x_ + 2 * y_ + c_
        local = pltpu.make_async_copy(ins[0].at[me], outs[0].at[me], lsem)
        res = []
        for s in range(1, N_DEV):
            px, py, pc = _flip(x_, s & 4), _flip(y_, s & 2), _flip(c_, s & 1)
            res.append(pltpu.make_async_remote_copy(
                src_ref=ins[0].at[4 * px + 2 * py + pc], dst_ref=outs[0].at[me], send_sem=send.at[s - 1],
                recv_sem=recv.at[s - 1], device_id=(px, py, pc), device_id_type=MESH))
        return local, res

    def start(ins, outs, sems):
        local, res = copies(ins, outs, sems)
        local.start()
        for cp in res:
            cp.start()

    def finish(ins, outs, sems):
        local, res = copies(ins, outs, sems)
        for cp in res:
            cp.wait_recv()
        for cp in res:
            cp.wait_send()
        local.wait()

    return dict(ins=[x], outs=[jax.ShapeDtypeStruct(x.shape, x.dtype)],
                sems=[pltpu.SemaphoreType.DMA((N_DEV - 1,)), pltpu.SemaphoreType.DMA((N_DEV - 1,)),
                      pltpu.SemaphoreType.DMA], start=start, finish=finish)


def _run_sides(sides, *, name):
    n_si, n_so = len(sides.ins), len(sides.outs)

    def body(*refs):
        ins, outs, sems = refs[:n_si], refs[n_si:n_si + n_so], refs[n_si + n_so:]
        sides.run("start", ins, outs, sems)
        sides.run("finish", ins, outs, sems)

    return pl.pallas_call(
        body, name=name, out_shape=list(sides.outs), in_specs=[ANY_SPEC] * n_si, out_specs=[ANY_SPEC] * n_so,
        scratch_shapes=list(sides.sems), input_output_aliases=sides.aliases(0, 0),
    )(*sides.ins)


def _pair_add(g, t, *, name, tr, wire):
    _, r, cols = g.shape
    g4 = g.reshape(4, 2, r, cols)
    j0 = 1 if wire else 0

    def g_index(j, i):
        x, y, c = _my_pos()
        return (jnp.bitwise_xor(2 * x + y, j + j0), c, i, 0)

    def body(g_ref, t_ref, o_ref):
        o_ref[...] = (g_ref[...] + t_ref[...]).astype(o_ref.dtype)

    return pl.pallas_call(
        body, name=name, grid=(3 if wire else 1, r // tr),
        in_specs=[pl.BlockSpec((None, None, tr, cols), g_index),
                  pl.BlockSpec((None, tr, cols), lambda j, i: (j + j0, i, 0))],
        out_specs=pl.BlockSpec((None, tr, cols), lambda j, i: (j, i, 0)),
        out_shape=jax.ShapeDtypeStruct((3 if wire else 1, r, cols), BF16 if wire else F32),
        compiler_params=_cparams(("arbitrary", "arbitrary")),
    )(g4, t)


def _config(x, ctx, w_in, w_dec_f, gla_norm_g, sg_ln_g, w_s):
    n, d = x.shape[1], x.shape[2]
    tc = ctx.shape[1]
    h = gla_norm_g.shape[1]
    dv = gla_norm_g.shape[2] * N_DEV
    dk = dv // 2
    kw, vw = h * dk, h * dv
    lr = w_dec_f.shape[1]
    sgw = sg_ln_g.shape[1]
    cfg = dict(N=n, D=d, TC=tc, H=h, DV=dv, DK=dk, KW=kw, VW=vw, LR=lr, SGW=sgw, SG_G=w_s.shape[1],
               SG_C=w_s.shape[2], IN=w_in.shape[2] * N_DEV)
    cfg.update(K0=kw, V0=2 * kw, R0=2 * kw + vw, L0=2 * kw + 2 * vw, ZA=2 * kw + 2 * vw + LANES)
    cfg.update(U0=0, VV0=sgw, ZB=2 * sgw)
    assert dk == LANES and vw == 2 * kw and 2 * lr <= LANES
    assert cfg["R0"] % vw == 0 and vw % sgw == 0
    assert cfg["IN"] == 2 * kw + 2 * vw + 2 * lr + 2 * sgw
    return cfg


def _rope_tables(cfg):
    n, tc, dk = cfg["N"], cfg["TC"], cfg["DK"]
    m = dk // 4
    pos = jnp.arange(n)
    inv = ROPE_BASE ** (-jnp.arange(m, dtype=F32) / m)
    ang_r = (pos // GRID_W).astype(F32)[:, None] * inv[None, :]
    ang_c = (pos % GRID_W).astype(F32)[:, None] * inv[None, :]
    cos = jnp.concatenate([jnp.cos(ang_r)] * 2 + [jnp.cos(ang_c)] * 2, axis=1)
    sin = jnp.concatenate([-jnp.sin(ang_r), jnp.sin(ang_r), -jnp.sin(ang_c), jnp.sin(ang_c)], axis=1)
    scale = dk ** -0.5
    z = jnp.zeros((tc, dk), F32)
    one = jnp.ones((tc, dk), F32)
    return [jnp.concatenate([cos * scale, z]), jnp.concatenate([sin * scale, z]),
            jnp.concatenate([cos, one]), jnp.concatenate([sin, z])]


def _pair_sums(g, t, nm):
    rows_for = _tile(g.shape[1], max(8, ELEMS_PER_BLOCK // g.shape[2]), 16)
    return (g, t), _pair_add(g, t, name="rs_wire_" + nm, tr=rows_for, wire=True)


def _local_step(x, ctx, target, mods, c_mods, w, cfg):
    n, d, tc = cfg["N"], cfg["D"], cfg["TC"]
    kw, vw, sgw, za, zb, lr = cfg["KW"], cfg["VW"], cfg["SGW"], cfg["ZA"], cfg["ZB"], cfg["LR"]
    sh1, sc1, g1, sh2, sc2, g2 = mods
    csh1, csc1 = c_mods
    rt = n + tc
    tb = math.gcd(256, math.gcd(n, tc))
    tr = math.gcd(128, tb)
    tr_s = math.gcd(128, tb)
    fs = w["sh_1"].shape[1]
    ff = fs * N_DEV
    cs_in = w["sh_in"].shape[1]
    r8 = d // N_DEV

    rows_in = _run_sides(_Sides([_a2a_job(w["sh_in"].reshape(N_DEV, r8, cs_in))]), name="a2a_w_in")[0]
    rows_in = rows_in.transpose(1, 0, 2).reshape(r8, cfg["IN"])
    lf0 = 2 * kw + 2 * vw
    sg0 = lf0 + 2 * lr
    wa_rows = jnp.concatenate([rows_in[:, :sg0], jnp.zeros((r8, LANES - 2 * lr), BF16)], axis=1)
    wb_rows = rows_in[:, sg0:]
    hx, w_a = _norm_mod(x, w["pre1_g"], sh1, sc1, name="in_norm_x", tr=tr, sides=_Sides([_ag_job([wa_rows])]))
    w_a = w_a.reshape(d, za)
    (hc,) = _norm_mod(ctx, w["pre1_g"], csh1, csc1, name="in_norm_ctx", tr=tr)
    a_all = jnp.concatenate([hx, hc], axis=0)

    tm_a = _tile(rt, 1152, 16)
    tm_n = _tile(n, 1024, 16)
    sh_o_rows = w["sh_o"].shape[0]
    o_cut = (sh_o_rows * 3 // 4) // 16 * 16
    z, w_b, wg_o = _matmul(
        a_all, w_a, "nn", rt, za, d, tm=tm_a, tn=_tile(za, 1152, LANES), tk=_tile(d, TK, LANES), name="mm_in_a",
        out_shapes=[jax.ShapeDtypeStruct((rt, za), F32)],
        sides=_Sides([_ag3_job(wb_rows), _ag3_job(w["sh_o"], rows=(0, o_cut))]))
    w_b = w_b.reshape(d, zb)
    s16 = d // 16
    w1_rows = lambda start, count: (start * s16, count * s16)
    z_b, wg_o, w_1 = _matmul(
        a_all, w_b, "nn", n, zb, d, tm=tm_n, tn=_tile(zb, 1024, LANES), tk=_tile(d, TK, LANES), name="mm_in_b",
        out_shapes=[jax.ShapeDtypeStruct((n, zb), F32)],
        sides=_Sides([_ag3_job(w["sh_o"], rows=(o_cut, sh_o_rows - o_cut), chained=[wg_o]),
                      _ag3_job(w["sh_1"], rows=w1_rows(0, 4))]))
    w_o = wg_o.reshape(d, d)

    tabs = _rope_tables(cfg)
    qr, kr, la_f, la_b, w_1 = _gla_prep(
        z, tabs, w["wdf_pad"], w["wdb_pad"], w["b_dec_f"], w["b_dec_b"], cfg, tr=tr,
        sides=_Sides([_ag_job([w["sh_1"]], rows=w1_rows(4, 1), chained=[w_1])]))

    zero_st = jnp.zeros((cfg["H"], cfg["DV"], cfg["DK"]), F32)
    gla = dict(cfg=cfg, tb=tb)
    _, save_cf, st_cf = _gla_fwd(qr, kr, z, la_f, zero_st, rev=False, row_off=n, nrows=tc, name="gla_ctx_f", **gla)
    _, save_cb, st_cb = _gla_fwd(qr, kr, z, la_b, zero_st, rev=True, row_off=n, nrows=tc, name="gla_ctx_b", **gla)
    o_f, save_f, _, w_1 = _gla_fwd(
        qr, kr, z, la_f, st_cf, rev=False, row_off=0, nrows=n, name="gla_f",
        sides=_Sides([_ag_job([w["sh_1"]], rows=w1_rows(5, 2), chained=[w_1])]), **gla)
    o_b, save_b, _, w_1 = _gla_fwd(
        qr, kr, z, la_b, st_cb, rev=True, row_off=0, nrows=n, name="gla_b",
        sides=_Sides([_ag_job([w["sh_1"]], rows=w1_rows(7, 2), chained=[w_1])]), **gla)
    y_gla, w_1 = _readout_fwd(o_f, o_b, z, w["gla_g"], cfg, tr=tr,
                              sides=_Sides([_ag_job([w["sh_1"]], rows=w1_rows(9, 1), chained=[w_1])]))
    y_sg, w_1 = _sg_fwd(z_b, n, w["sg_ln_g"], w["sg_ln_b"], w["w_s"], w["bs_full"], cfg,
                        sides=_Sides([_ag_job([w["sh_1"]], rows=w1_rows(10, 1), chained=[w_1])]))
    ycat = jnp.concatenate([y_gla, y_sg], axis=1)

    mix, w_1 = _matmul(ycat, w_o, "nn", n, d, d, tm=tm_n, tn=_tile(d, 1024, LANES), tk=_tile(d, TK, LANES),
                       name="mm_o", out_shapes=[jax.ShapeDtypeStruct((n, d), F32)],
                       sides=_Sides([_ag3_job(w["sh_1"], rows=w1_rows(11, 5), chained=[w_1])]))
    w2_cut = (fs // 8) // 16 * 16
    x1, h2, wg_2 = _mid_fwd(x, mix, g1, w["post1_g"], w["pre2_g"], sh2, sc2, tr=tr_s,
                            sides=_Sides([_ag_job([w["sh_2"]], rows=(0, w2_cut))]))

    tn_f = _tile(fs, 1024, LANES)
    tk_d = _tile(d, TK, LANES)

    def relu2(acc):
        return acc, jnp.square(jnp.maximum(acc, 0.0))

    a1, p1, wg_2 = _matmul(h2, w_1, "nn", n, ff, d, tm=tm_n, tn=tn_f, tk=tk_d, name="mm_1",
                           b_spec=_blocked_b_nn(fs, tk_d, tn_f), epilogue=relu2,
                           out_shapes=[jax.ShapeDtypeStruct((n, ff), BF16)] * 2,
                           sides=_Sides([_ag3_job(w["sh_2"], rows=(w2_cut, fs - w2_cut), chained=[wg_2])]))
    w_2 = wg_2.reshape(ff, d)
    tk_f = _tile(ff, TK, LANES)
    m2 = _matmul(p1, w_2, "nn", n, d, ff, tm=tm_n, tn=_tile(d, 1024, LANES), tk=tk_f, name="mm_2",
                 out_shapes=[jax.ShapeDtypeStruct((n, d), F32)])[0]

    dx2, dm2, dg2, dpost2, lossc = _head_bwd(x1, m2, target, g2, w["post2_g"], tr=tr_s)

    def drelu2(acc, a):
        return (acc * (2.0 * jnp.maximum(a.astype(F32), 0.0)),)

    tk_n = _tile(n, TK, 16)
    tm_d = _tile(d, 1024, LANES)
    dw_2 = _matmul(p1, dm2, "tn", ff, d, n, tm=_tile(ff, 1024, LANES), tn=_tile(d, 1024, LANES), tk=tk_n,
                   name="mm_2_dw", out_shapes=[jax.ShapeDtypeStruct((ff, d), F32)])[0]
    g_2 = dw_2.reshape(N_DEV, fs, d)
    da1, t_2 = _matmul(dm2, w_2, "nt", n, ff, d, tm=tm_n, tn=_tile(ff, 1024, LANES), tk=tk_d, name="mm_2_dx",
                       epilogue=drelu2, extras=(a1,), out_shapes=[jax.ShapeDtypeStruct((n, ff), BF16)],
                       sides=_Sides([_pair_job([g_2])]))
    p2_own, p2_wire = _pair_sums(g_2, t_2, "w_2")
    cut2 = (fs * 54 // 64) // 16 * 16
    g_1, u_2 = _matmul(h2, da1, "tn", d, ff, n, tm=tm_d, tn=tn_f, tk=tk_n, name="mm_1_dw",
                       out_specs=[_blocked_out(fs, tm_d, tn_f)],
                       out_shapes=[jax.ShapeDtypeStruct((N_DEV, d, fs), F32)],
                       sides=_Sides([_chip_job([p2_wire], rows=(0, cut2))]))
    tk_fs = _tile(fs, TK, LANES)
    dh2, t_1, u_2 = _matmul(da1, w_1, "nt", n, d, ff, tm=tm_n, tn=_tile(d, 1024, LANES), tk=tk_fs, name="mm_1_dx",
                            b_spec=_blocked_b_nt(fs, _tile(d, 1024, LANES), tk_fs),
                            out_shapes=[jax.ShapeDtypeStruct((n, d), F32)],
                            sides=_Sides([_pair_job([g_1]),
                                          _chip_job([p2_wire], rows=(cut2, fs - cut2), chained=[u_2])]))
    p1_own, p1_wire = _pair_sums(g_1, t_1, "w_1")
    c1 = [0] + [(d * f // 64) // 16 * 16 for f in (12, 32, 47)] + [d]
    piece1 = lambda i: (c1[i], c1[i + 1] - c1[i])

    dx1, dmix, dsh2, dsc2, dpre2, dg1, dpost1, u_1 = _mid_bwd(
        dh2, x1, dx2, mix, sc2, w["pre2_g"], g1, w["post1_g"], tr=tr_s,
        sides=_Sides([_chip_job([p1_wire], rows=piece1(0))]))
    dw_o, u_1 = _matmul(ycat, dmix, "tn", d, d, n, tm=tm_d, tn=_tile(d, 1024, LANES), tk=tk_n, name="mm_o_dw",
                        out_shapes=[jax.ShapeDtypeStruct((d, d), F32)],
                        sides=_Sides([_chip_job([p1_wire], rows=piece1(1), chained=[u_1])]))
    g_o = dw_o.reshape(N_DEV, r8, d)
    dycat, t_o, u_1 = _matmul(dmix, w_o, "nt", n, d, d, tm=tm_n, tn=_tile(d, 1024, LANES), tk=tk_d, name="mm_o_dx",
                              out_shapes=[jax.ShapeDtypeStruct((n, d), F32)],
                              sides=_Sides([_pair_job([g_o]), _chip_job([p1_wire], rows=piece1(2), chained=[u_1])]))
    po_own, po_wire = _pair_sums(g_o, t_o, "w_o")

    dz_b, dws, dbs_acc, dlng, dlnb = _sg_bwd(z_b, dycat, n, w["sg_ln_g"], w["sg_ln_b"], w["w_s"], w["bs_full"], cfg)
    dw_b, u_1 = _matmul(a_all, dz_b, "tn", d, zb, n, tm=tm_d, tn=_tile(zb, 1024, LANES), tk=tk_n, name="mm_in_dw_b",
                        out_shapes=[jax.ShapeDtypeStruct((d, zb), F32)],
                        sides=_Sides([_chip_job([p1_wire], rows=piece1(3), chained=[u_1])]))
    g_b = dw_b.reshape(N_DEV, r8, zb)
    do, dzr, dgla_g = _readout_bwd(o_f, o_b, z, dycat, w["gla_g"], cfg, tr=tr)

    *gf, t_b = _gla_bwd(qr, kr, z, la_f, do, save_f, zero_st, rev=False, row_off=0, nrows=n, name="gla_f_bwd",
                        sides=_Sides([_pair_job([g_b])]), **gla)
    pb_own, pb_wire = _pair_sums(g_b, t_b, "w_in_b")
    gb = _gla_bwd(qr, kr, z, la_b, do, save_b, zero_st, rev=True, row_off=0, nrows=n, name="gla_b_bwd", **gla)
    do_c = jnp.zeros((tc, vw), BF16)
    gcf = _gla_bwd(qr, kr, z, la_f, do_c, save_cf, gf[4], rev=False, row_off=n, nrows=tc, name="gla_ctx_f_bwd",
                   **gla)
    gcb = _gla_bwd(qr, kr, z, la_b, do_c, save_cb, gb[4], rev=True, row_off=n, nrows=tc, name="gla_ctx_b_bwd",
                   **gla)

    post = dict(la_f=la_f, la_b=la_b, z=z, tabs=tabs, wdf_pad=w["wdf_pad"], wdb_pad=w["wdb_pad"], cfg=cfg, tr=tr)
    dzq, dzk, dzv, dzl, dwdf, dwdb, dbdf, dbdb = _gla_post(gf, gb, row_off=0, nrows=n, name="gla_post", **post)
    czq, czk, czv, czl, cwdf, cwdb, cbdf, cbdb = _gla_post(gcf, gcb, row_off=n, nrows=tc, name="gla_post_ctx",
                                                           **post)
    dz_a = jnp.concatenate([
        jnp.concatenate([dzq, dzk, dzv, dzr, dzl], axis=1),
        jnp.concatenate([czq, czk, czv, jnp.zeros((tc, vw), BF16), czl], axis=1)], axis=0)

    dw_a, u_b = _matmul(
        a_all, dz_a, "tn", d, za, rt, tm=tm_d, tn=_tile(za, 1152, LANES), tk=_tile(rt, 2176, 16), name="mm_in_dw_a",
        out_shapes=[jax.ShapeDtypeStruct((d, za), F32)], sides=_Sides([_chip_job([pb_wire])]))
    g_a = dw_a.reshape(N_DEV, r8, za)
    da_a, t_a, u_o = _matmul(dz_a, w_a, "nt", rt, d, za, tm=_tile(rt, 576, 16), tn=_tile(d, 512, LANES),
                             tk=za, name="mm_in_dx_a", out_shapes=[jax.ShapeDtypeStruct((rt, d), F32)],
                             sides=_Sides([_pair_job([g_a]), _chip_job([po_wire])]))
    pa_own, pa_wire = _pair_sums(g_a, t_a, "w_in_a")
    cut_a = (r8 * 9 // 16) // 16 * 16
    tm_x = _tile(n, 512, 16)
    da_x, u_a = _matmul(dz_b, w_b, "nt", n, d, zb, tm=tm_x, tn=_tile(d, 1024, LANES), tk=_tile(zb, 4096, LANES),
                        name="mm_in_dx_b", epilogue=lambda acc, prev: (acc + prev,), extras=(da_a,),
                        out_shapes=[jax.ShapeDtypeStruct((n, d), F32)],
                        sides=_Sides([_chip_job([pa_wire], rows=(0, cut_a))]))

    grad_x, dsh1, dsc1, dpre1, u_a = _in_bwd(
        da_x, x, dx1, sc1, w["pre1_g"], row_off=0, tr=tr_s, name="in_bwd_x",
        sides=_Sides([_chip_job([pa_wire], rows=(cut_a, r8 - cut_a), chained=[u_a])]))
    dcsh1, dcsc1, dpre1_c = _in_bwd(da_a, ctx, None, csc1, w["pre1_g"], row_off=n, tr=tr_s, name="in_bwd_ctx")

    small = dict(
        pre1_g=dpre1 + dpre1_c, post1_g=dpost1, pre2_g=dpre2, post2_g=dpost2,
        w_dec_f=(dwdf + cwdf)[:lr], w_dec_b=(dwdb + cwdb)[lr:2 * lr], b_dec_f=dbdf + cbdf, b_dec_b=dbdb + cbdb,
        gla_norm_g=dgla_g, sg_ln_g=dlng, sg_ln_b=dlnb, w_s=dws,
        b_s=dbs_acc.reshape(cfg["SG_C"], cfg["SG_G"], sgw // cfg["SG_G"]).sum(-1).T)
    dmod = jnp.concatenate([dsh1, dsc1, dg1, dsh2, dsc2, dg2], axis=1)
    dmod_c = jnp.concatenate([dcsh1, dcsc1], axis=1)
    big = dict(w_in_a=(pa_own, u_a), w_in_b=(pb_own, u_b), w_o=(po_own, u_o), w_1=(p1_own, u_1), w_2=(p2_own, u_2))
    return lossc, grad_x, big, small, dmod, dmod_c


SMALL_NAMES = ["b_ada", "pre1_g", "post1_g", "pre2_g", "post2_g", "w_dec_f", "b_dec_f", "w_dec_b", "b_dec_b",
               "gla_norm_g", "sg_ln_g", "sg_ln_b", "w_s", "b_s", "c_ctx"]
WEIGHT_ORDER = ["c_ctx", "w_ada", "b_ada", "pre1_g", "post1_g", "pre2_g", "post2_g", "w_in", "w_dec_f", "b_dec_f",
                "w_dec_b", "b_dec_b", "gla_norm_g", "sg_ln_g", "sg_ln_b", "w_s", "b_s", "w_o", "w_1", "w_2"]


def kernel(x, c, ctx, c_ctx, w_ada, b_ada, pre1_g, post1_g, pre2_g, post2_g, w_in, w_dec_f, b_dec_f, w_dec_b, b_dec_b, gla_norm_g, sg_ln_g, sg_ln_b, w_s, b_s, w_o, w_1, w_2, loss_target, m_c_ctx, m_w_ada, m_b_ada, m_pre1_g, m_post1_g, m_pre2_g, m_post2_g, m_w_in, m_w_dec_f, m_b_dec_f, m_w_dec_b, m_b_dec_b, m_gla_norm_g, m_sg_ln_g, m_sg_ln_b, m_w_s, m_b_s, m_w_o, m_w_1, m_w_2, v_c_ctx, v_w_ada, v_b_ada, v_pre1_g, v_post1_g, v_pre2_g, v_post2_g, v_w_in, v_w_dec_f, v_b_dec_f, v_w_dec_b, v_b_dec_b, v_gla_norm_g, v_sg_ln_g, v_sg_ln_b, v_w_s, v_b_s, v_w_o, v_w_1, v_w_2):
    weights = dict(c_ctx=c_ctx, w_ada=w_ada, b_ada=b_ada, pre1_g=pre1_g, post1_g=post1_g, pre2_g=pre2_g,
                   post2_g=post2_g, w_in=w_in, w_dec_f=w_dec_f, b_dec_f=b_dec_f, w_dec_b=w_dec_b, b_dec_b=b_dec_b,
                   gla_norm_g=gla_norm_g, sg_ln_g=sg_ln_g, sg_ln_b=sg_ln_b, w_s=w_s, b_s=b_s, w_o=w_o, w_1=w_1,
                   w_2=w_2)
    mom_m = dict(c_ctx=m_c_ctx, w_ada=m_w_ada, b_ada=m_b_ada, pre1_g=m_pre1_g, post1_g=m_post1_g, pre2_g=m_pre2_g,
                 post2_g=m_post2_g, w_in=m_w_in, w_dec_f=m_w_dec_f, b_dec_f=m_b_dec_f, w_dec_b=m_w_dec_b,
                 b_dec_b=m_b_dec_b, gla_norm_g=m_gla_norm_g, sg_ln_g=m_sg_ln_g, sg_ln_b=m_sg_ln_b, w_s=m_w_s,
                 b_s=m_b_s, w_o=m_w_o, w_1=m_w_1, w_2=m_w_2)
    mom_v = dict(c_ctx=v_c_ctx, w_ada=v_w_ada, b_ada=v_b_ada, pre1_g=v_pre1_g, post1_g=v_post1_g, pre2_g=v_pre2_g,
                 post2_g=v_post2_g, w_in=v_w_in, w_dec_f=v_w_dec_f, b_dec_f=v_b_dec_f, w_dec_b=v_w_dec_b,
                 b_dec_b=v_b_dec_b, gla_norm_g=v_gla_norm_g, sg_ln_g=v_sg_ln_g, sg_ln_b=v_sg_ln_b, w_s=v_w_s,
                 b_s=v_b_s, w_o=v_w_o, w_1=v_w_1, w_2=v_w_2)

    cfg = _config(x, ctx, w_in, w_dec_f, gla_norm_g, sg_ln_g, w_s)
    n, d, h, dv, kw, vw, lr, sgw = (cfg[k] for k in ("N", "D", "H", "DV", "KW", "VW", "LR", "SGW"))
    dvs, kws = dv // N_DEV, kw // N_DEV
    ix, iy, ic = _my_pos()
    me = 4 * ix + 2 * iy + ic

    pack1 = jnp.concatenate([c.reshape(1, d), w_dec_f.reshape(1, lr * kws), w_dec_b.reshape(1, lr * kws),
                             gla_norm_g.reshape(1, h * dvs)], axis=1)
    g1 = _all_gather_vec(pack1, name="ag_small_in")
    c_all = g1[:, :d]
    o1 = d
    wdf = g1[:, o1:o1 + lr * kws].reshape(N_DEV, lr, kws).transpose(1, 0, 2).reshape(lr, kw)
    o1 += lr * kws
    wdb = g1[:, o1:o1 + lr * kws].reshape(N_DEV, lr, kws).transpose(1, 0, 2).reshape(lr, kw)
    o1 += lr * kws
    gla_g = g1[:, o1:o1 + h * dvs].reshape(N_DEV, h, dvs).transpose(1, 0, 2).reshape(1, h * dv)

    c16 = jnp.concatenate([c_all, jnp.broadcast_to(c_ctx.reshape(1, d), (N_DEV, d))], axis=0)
    ncol = w_ada.shape[2]
    wa = w_ada.reshape(d, ncol)
    b_mine = lax.dynamic_slice(b_ada, (0, me * ncol), (1, ncol))
    tn_ada = _tile(ncol, 512, LANES)
    mod_mine = _ada_fwd(c16, wa, b_mine, tn=tn_ada)
    mod_all = _all_gather_small(mod_mine, name="ag_mod").transpose(1, 0, 2).reshape(16, N_DEV * ncol)
    mod_b = lax.dynamic_slice(mod_all, (me, 0), (1, 6 * d))
    mods = [mod_b[:, i * d:(i + 1) * d] for i in range(6)]
    c_mods = [mod_all[N_DEV:N_DEV + 1, :d], mod_all[N_DEV:N_DEV + 1, d:2 * d]]

    zpad = lambda r: jnp.zeros((r, kw), F32)
    w = dict(
        sh_in=w_in.reshape(d, w_in.shape[2]).astype(BF16), sh_o=w_o.reshape(w_o.shape[1], d).astype(BF16),
        sh_1=w_1.reshape(d, w_1.shape[2]).astype(BF16), sh_2=w_2.reshape(w_2.shape[1], d).astype(BF16),
        pre1_g=pre1_g, post1_g=post1_g, pre2_g=pre2_g, post2_g=post2_g, b_dec_f=b_dec_f, b_dec_b=b_dec_b,
        wdf_pad=jnp.concatenate([wdf, zpad(LANES - lr)], axis=0),
        wdb_pad=jnp.concatenate([zpad(lr), wdb, zpad(LANES - 2 * lr)], axis=0),
        gla_g=gla_g, sg_ln_g=sg_ln_g, sg_ln_b=sg_ln_b, w_s=w_s[0],
        bs_full=jnp.repeat(b_s[0].T, sgw // cfg["SG_G"], axis=1))

    lossc, grad_x, big, small, dmod, dmod_c = _local_step(x[0], ctx[0], loss_target[0], mods, c_mods, w, cfg)
    loss = lax.psum(jnp.sum(lossc), AXES)

    order3 = ["pre1_g", "post1_g", "pre2_g", "post2_g", "w_dec_f", "b_dec_f", "w_dec_b", "b_dec_b", "gla_norm_g",
              "sg_ln_g", "sg_ln_b", "w_s", "b_s"]
    pieces = [dmod, dmod_c] + [small[k].reshape(1, -1) for k in order3]
    sizes = [p.shape[1] for p in pieces]
    g3 = _all_gather_vec(jnp.concatenate(pieces, axis=1), name="ag_small_grads")
    offs = [0]
    for s in sizes:
        offs.append(offs[-1] + s)
    dmod_all = g3[:, :6 * d]
    dmod_c_all = jnp.pad(g3[:, offs[1]:offs[2]], ((0, 0), (0, 4 * d)))
    parts8 = {k: g3[:, offs[2 + i]:offs[3 + i]] for i, k in enumerate(order3)}
    parts8["b_ada"] = dmod_all + dmod_c_all
    parts8["w_dec_f"] = lax.dynamic_slice(parts8["w_dec_f"].reshape(N_DEV, lr, kw), (0, 0, me * kws),
                                          (N_DEV, lr, kws)).reshape(N_DEV, -1)
    parts8["w_dec_b"] = lax.dynamic_slice(parts8["w_dec_b"].reshape(N_DEV, lr, kw), (0, 0, me * kws),
                                          (N_DEV, lr, kws)).reshape(N_DEV, -1)
    parts8["gla_norm_g"] = lax.dynamic_slice(parts8["gla_norm_g"].reshape(N_DEV, h, dv), (0, 0, me * dvs),
                                             (N_DEV, h, dvs)).reshape(N_DEV, -1)

    dm16 = jnp.concatenate([dmod_all, dmod_c_all], axis=0)
    dm_mine = lax.dynamic_slice(dm16, (0, me * ncol), (16, ncol))
    g_w_ada, dcc = _ada_bwd(c16, dm_mine, wa, c_ctx.reshape(1, d), tn=tn_ada)
    parts8["c_ctx"] = _all_gather_vec(dcc, name="ag_cctx")

    flat = lambda t: t.reshape(1, -1)
    g8 = _dense(jnp.concatenate([parts8[k] for k in SMALL_NAMES], axis=1))
    ws, ms, vs = [_dense(jnp.concatenate([flat(src[k]) for k in SMALL_NAMES], axis=1))[0]
                  for src in (weights, mom_m, mom_v)]
    res_small = [r.reshape(1, -1) for r in _adam_small(g8, ws, ms, vs)]
    out = {}
    off = 0
    for k in SMALL_NAMES:
        sz = weights[k].size
        out[k] = [r[:, off:off + sz].reshape(weights[k].shape) for r in res_small]
        off += sz

    rows_for = lambda r, cols: _tile(r, max(8, ELEMS_PER_BLOCK // cols), 16)
    r8, cs_in = d // N_DEV, w_in.shape[2]

    def adam(nm, parts, sides=_NO_SIDES):
        shp = weights[nm].shape
        r2 = (shp[1], shp[2])
        res = _adam_big(parts, weights[nm].reshape(r2), mom_m[nm].reshape(r2), mom_v[nm].reshape(r2),
                        name="adam_" + nm, tr=rows_for(*r2), sides=sides)
        out[nm] = [r.reshape(shp) for r in res[:4]]
        return res[4:]

    four = lambda own, u: [(own[0], OWN), (own[1], 0), (u, 0), (u, 1), (u, 2)]
    red_a = _sum_parts(four(*big["w_in_a"]), r8, cfg["ZA"], name="rs_sum_w_in_a", tr=rows_for(r8, cfg["ZA"]))
    red_b = _sum_parts(four(*big["w_in_b"]), r8, cfg["ZB"], name="rs_sum_w_in_b", tr=rows_for(r8, cfg["ZB"]))
    red = jnp.concatenate([red_a[:, :cfg["L0"] + 2 * lr], red_b], axis=1)
    (g_in,) = adam("w_ada", [(g_w_ada, None)],
                   _Sides([_a2a_job(red.reshape(r8, N_DEV, cs_in).transpose(1, 0, 2))]))
    for nm in ("w_2", "w_1", "w_o"):
        adam(nm, four(*big[nm]))
    adam("w_in", [(g_in.reshape(d, cs_in), None)])

    outs = [loss, grad_x[None]]
    for i in range(4):
        outs += [out[k][i] for k in WEIGHT_ORDER]
    return tuple(outs)
```
